```python
import math
import jax, jax.numpy as jnp
from jax import lax
import numpy as np

D_MODEL = 1024
BATCH = 16
SEQ = 4096
DEPTH = 2

EPS = 1e-6
SSD_HEADS = 16
SSD_HEAD_DIM = 64
SSD_WIDTH = SSD_HEADS * SSD_HEAD_DIM
SSD_GROUPS = 2
SSD_STATE = 128
SSD_CONV = 4
SSD_CHUNK = 128
SSD_CONV_CH = SSD_WIDTH + 2 * SSD_GROUPS * SSD_STATE
DT_MIN = 1e-3
DT_MAX = 1e-1
POOL_GROUPS = 4
POOL_GROUP_DIM = 128
POOL_WIDTH = POOL_GROUPS * POOL_GROUP_DIM
POOL_WINDOWS = (2, 4, 8, 16)
MLA_HEADS = 8
MLA_Q_RANK = 384
MLA_KV_RANK = 256
MLA_NOPE = 64
MLA_ROPE = 32
MLA_V = 64
MLA_QK = MLA_NOPE + MLA_ROPE
MLA_WIDTH = MLA_HEADS * MLA_V
ROPE_THETA = 10000.0
Q_BLOCK = 128
MIX_WIDTH = SSD_WIDTH + POOL_WIDTH + MLA_WIDTH
IN_SIZES = (SSD_WIDTH, SSD_CONV_CH, SSD_HEADS, POOL_WIDTH, MLA_Q_RANK, MLA_KV_RANK, MLA_ROPE)
IN_COLS = SSD_WIDTH + SSD_CONV_CH + SSD_HEADS + POOL_WIDTH + MLA_Q_RANK + MLA_KV_RANK + MLA_ROPE
D_FF = 2816
FFN_CONV = 3

kernel_name = "hybrid_ssd_pool_mla_convffn"


def rmsnorm(x, w):
    xf = x.astype(jnp.float32)
    var = jnp.mean(xf * xf, axis=-1, keepdims=True)
    return (xf * lax.rsqrt(var + EPS)).astype(x.dtype) * w


def causal_dwconv(x, w, b):
    K = w.shape[0]
    S = x.shape[1]
    xp = jnp.pad(x, ((0, 0), (K - 1, 0), (0, 0)))
    acc = xp[:, 0:S] * w[0] + b
    for k in range(1, K):
        acc = acc + xp[:, k:k + S] * w[k]
    return acc


def rope(x, cos, sin):
    x1, x2 = jnp.split(x, 2, axis=-1)
    return jnp.concatenate([x1 * cos - x2 * sin, x1 * sin + x2 * cos], axis=-1)


def rope_tables(positions):
    inv_freq = ROPE_THETA ** (-jnp.arange(0, MLA_ROPE, 2, dtype=jnp.float32) / MLA_ROPE)
    ang = positions.astype(jnp.float32)[..., None] * inv_freq
    return jnp.cos(ang), jnp.sin(ang)


def ssd_mixer(z, xbc, dt_raw, conv_w, conv_b, dt_bias, a_log, d_skip, norm_w):
    Bsz, S, _ = xbc.shape
    G, E, P, N, L = SSD_GROUPS, SSD_HEADS // SSD_GROUPS, SSD_HEAD_DIM, SSD_STATE, SSD_CHUNK
    nc = S // L
    xbc = jax.nn.silu(causal_dwconv(xbc, conv_w, conv_b))
    xs, bs, cs = jnp.split(xbc, [SSD_WIDTH, SSD_WIDTH + G * N], axis=-1)
    xs = xs.reshape(Bsz, nc, L, G, E, P)
    bs = bs.reshape(Bsz, nc, L, G, N)
    cs = cs.reshape(Bsz, nc, L, G, N)
    dt = jax.nn.softplus((dt_raw + dt_bias).astype(jnp.float32)).reshape(Bsz, nc, L, G, E)
    a = -jnp.exp(a_log.astype(jnp.float32)).reshape(G, E)
    da = dt * a
    xdt = xs * dt[..., None]
    da_cum = jnp.cumsum(da, axis=2)
    causal = jnp.tril(jnp.ones((L, L), dtype=bool))
    seg = da_cum[:, :, :, None] - da_cum[:, :, None, :]
    decay = jnp.exp(jnp.where(causal[None, None, :, :, None, None], seg, -jnp.inf))
    cb = jnp.einsum("bclgn,bcsgn->bclsg", cs, bs)
    y_diag = jnp.einsum("bclsg,bclsge,bcsgep->bclgep", cb, decay, xdt)
    decay_to_end = jnp.exp(da_cum[:, :, -1:] - da_cum)
    chunk_states = jnp.einsum("bclgn,bclge,bclgep->bcgepn", bs, decay_to_end, xdt)
    chunk_decay = jnp.exp(da_cum[:, :, -1])

    def step(h, inp):
        dec, st = inp
        return h * dec[..., None, None] + st, h

    h0 = jnp.zeros((Bsz, G, E, P, N), dtype=chunk_states.dtype)
    _, h_in = lax.scan(step, h0, (jnp.moveaxis(chunk_decay, 1, 0), jnp.moveaxis(chunk_states, 1, 0)))
    h_in = jnp.moveaxis(h_in, 0, 1)
    y_off = jnp.einsum("bclgn,bcgepn,bclge->bclgep", cs, h_in, jnp.exp(da_cum))
    y = y_diag + y_off + xs * d_skip.reshape(G, E)[:, :, None]
    y = y.reshape(Bsz, S, SSD_WIDTH)
    return rmsnorm(y * jax.nn.silu(z), norm_w)


def pool_mixer(u, pool_w, pool_scale):
    Bsz, S, _ = u.shape
    uf = u.astype(jnp.float32)
    csum = jnp.pad(jnp.cumsum(uf, axis=1), ((0, 0), (1, 0), (0, 0)))
    count = jnp.arange(1, S + 1, dtype=jnp.float32)[:, None]
    means = []
    for gi, w in enumerate(POOL_WINDOWS):
        c = csum[:, :, gi * POOL_GROUP_DIM:(gi + 1) * POOL_GROUP_DIM]
        lag = jnp.pad(c, ((0, 0), (w - 1, 0), (0, 0)))[:, :S]
        means.append((c[:, 1:] - lag) / jnp.minimum(count, float(w)))
    pooled = (jnp.concatenate(means, axis=-1) - uf).astype(u.dtype)
    pooled = pooled.reshape(Bsz, S, POOL_GROUPS, POOL_GROUP_DIM)
    y = jnp.einsum("bsgc,gcd->bsgd", pooled, pool_w).reshape(Bsz, S, POOL_WIDTH)
    return y * pool_scale


def mla_mixer(c_q, c_kv, k_pe, cos, sin, q_norm, w_uq, kv_norm, w_ukv):
    Bsz, S, _ = c_q.shape
    H = MLA_HEADS
    q = (rmsnorm(c_q, q_norm) @ w_uq).reshape(Bsz, S, H, MLA_QK)
    kv = (rmsnorm(c_kv, kv_norm) @ w_ukv).reshape(Bsz, S, H, MLA_NOPE + MLA_V)
    q_nope, q_pe = jnp.split(q, [MLA_NOPE], axis=-1)
    k_nope, v = jnp.split(kv, [MLA_NOPE], axis=-1)
    q_pe = rope(q_pe, cos[:, :, None, :], sin[:, :, None, :])
    k_pe = rope(k_pe, cos, sin)
    q = jnp.concatenate([q_nope, q_pe], axis=-1)
    k = jnp.concatenate([k_nope, jnp.broadcast_to(k_pe[:, :, None, :], (Bsz, S, H, MLA_ROPE))], axis=-1)
    scale = 1.0 / math.sqrt(MLA_QK)
    nb = S // Q_BLOCK
    qb = jnp.moveaxis(q.reshape(Bsz, nb, Q_BLOCK, H, MLA_QK), 1, 0)
    key_pos = jnp.arange(S)

    def attend(args):
        q_blk, i = args
        s = jnp.einsum("bqhd,bkhd->bhqk", q_blk, k).astype(jnp.float32) * scale
        q_pos = i * Q_BLOCK + jnp.arange(Q_BLOCK)
        s = jnp.where(key_pos[None, :] <= q_pos[:, None], s, -jnp.inf)
        p = jax.nn.softmax(s, axis=-1).astype(v.dtype)
        return jnp.einsum("bhqk,bkhd->bqhd", p, v)

    o = lax.map(attend, (qb, jnp.arange(nb)))
    return jnp.moveaxis(o, 0, 1).reshape(Bsz, S, MLA_WIDTH)


def conv_ffn(h, w_up, conv_w, conv_b, w_down):
    up = causal_dwconv(h @ w_up, conv_w, conv_b)
    gate, val = jnp.split(up, 2, axis=-1)
    return (jax.nn.silu(gate) * val) @ w_down


def _fwd_setup_inputs(seed: int = 0) -> dict:
    key = jax.random.key(seed)
    ks = jax.random.split(key, 24)
    f32 = jnp.float32

    def nrm(k, shape, scale):
        return jax.random.normal(k, shape, f32) * scale

    def gain(k, shape):
        return 1.0 + 0.02 * jax.random.normal(k, shape, f32)

    x = jax.random.normal(ks[0], (BATCH, SEQ, D_MODEL), f32)
    offsets = jax.random.randint(ks[1], (BATCH, 1), 0, 1024, dtype=jnp.int32)
    positions = (offsets + jnp.arange(SEQ, dtype=jnp.int32)[None, :]).astype(jnp.int32)
    u_dt = jax.random.uniform(ks[2], (DEPTH, SSD_HEADS), f32)
    dt0 = jnp.exp(u_dt * (math.log(DT_MAX) - math.log(DT_MIN)) + math.log(DT_MIN))
    dt_bias = dt0 + jnp.log(-jnp.expm1(-dt0))
    a_log = jnp.log(jax.random.uniform(ks[3], (DEPTH, SSD_HEADS), f32, 1.0, 16.0))
    return {
        "x": x,
        "positions": positions,
        "attn_norm": gain(ks[4], (DEPTH, D_MODEL)),
        "w_in": nrm(ks[5], (DEPTH, D_MODEL, IN_COLS), D_MODEL ** -0.5),
        "ssd_conv_w": nrm(ks[6], (DEPTH, SSD_CONV, SSD_CONV_CH), SSD_CONV ** -0.5),
        "ssd_conv_b": nrm(ks[7], (DEPTH, SSD_CONV_CH), 0.02),
        "ssd_dt_bias": dt_bias,
        "ssd_a_log": a_log,
        "ssd_d": 1.0 + 0.1 * jax.random.normal(ks[8], (DEPTH, SSD_HEADS), f32),
        "ssd_norm": gain(ks[9], (DEPTH, SSD_WIDTH)),
        "pool_w": nrm(ks[10], (DEPTH, POOL_GROUPS, POOL_GROUP_DIM, POOL_GROUP_DIM), POOL_GROUP_DIM ** -0.5),
        "pool_scale": gain(ks[11], (DEPTH, POOL_WIDTH)),
        "mla_q_norm": gain(ks[12], (DEPTH, MLA_Q_RANK)),
        "mla_w_uq": nrm(ks[13], (DEPTH, MLA_Q_RANK, MLA_HEADS * MLA_QK), MLA_Q_RANK ** -0.5),
        "mla_kv_norm": gain(ks[14], (DEPTH, MLA_KV_RANK)),
        "mla_w_ukv": nrm(ks[15], (DEPTH, MLA_KV_RANK, MLA_HEADS * (MLA_NOPE + MLA_V)), MLA_KV_RANK ** -0.5),
        "w_out": nrm(ks[16], (DEPTH, MIX_WIDTH, D_MODEL), MIX_WIDTH ** -0.5),
        "ffn_norm": gain(ks[17], (DEPTH, D_MODEL)),
        "ffn_w_up": nrm(ks[18], (DEPTH, D_MODEL, 2 * D_FF), D_MODEL ** -0.5),
        "ffn_conv_w": nrm(ks[19], (DEPTH, FFN_CONV, 2 * D_FF), FFN_CONV ** -0.5),
        "ffn_conv_b": nrm(ks[20], (DEPTH, 2 * D_FF), 0.02),
        "ffn_w_down": nrm(ks[21], (DEPTH, D_FF, D_MODEL), D_FF ** -0.5),
        "final_norm": gain(ks[22], (D_MODEL,)),
    }


def _fwd_reference(x, positions, attn_norm, w_in, ssd_conv_w, ssd_conv_b, ssd_dt_bias, ssd_a_log,
              ssd_d, ssd_norm, pool_w, pool_scale, mla_q_norm, mla_w_uq, mla_kv_norm, mla_w_ukv,
              w_out, ffn_norm, ffn_w_up, ffn_conv_w, ffn_conv_b, ffn_w_down, final_norm):
    cos, sin = rope_tables(positions)
    splits = [int(s) for s in np.cumsum(IN_SIZES)[:-1]]
    for l in range(DEPTH):
        h = rmsnorm(x, attn_norm[l])
        proj = h @ w_in[l]
        z, xbc, dt_raw, u, c_q, c_kv, k_pe = jnp.split(proj, splits, axis=-1)
        y_ssd = ssd_mixer(z, xbc, dt_raw, ssd_conv_w[l], ssd_conv_b[l], ssd_dt_bias[l],
                          ssd_a_log[l], ssd_d[l], ssd_norm[l])
        y_pool = pool_mixer(u, pool_w[l], pool_scale[l])
        y_mla = mla_mixer(c_q, c_kv, k_pe, cos, sin, mla_q_norm[l], mla_w_uq[l],
                          mla_kv_norm[l], mla_w_ukv[l])
        x = x + jnp.concatenate([y_ssd, y_pool, y_mla], axis=-1) @ w_out[l]
        h = rmsnorm(x, ffn_norm[l])
        x = x + conv_ffn(h, ffn_w_up[l], ffn_conv_w[l], ffn_conv_b[l], ffn_w_down[l])
    return rmsnorm(x, final_norm)


import jax as _jax
import jax.numpy as _jnp

TWIN_FORMAT = 'train_step'
FWD_PARAMS = ['x', 'positions', 'attn_norm', 'w_in', 'ssd_conv_w', 'ssd_conv_b', 'ssd_dt_bias', 'ssd_a_log', 'ssd_d', 'ssd_norm', 'pool_w', 'pool_scale', 'mla_q_norm', 'mla_w_uq', 'mla_kv_norm', 'mla_w_ukv', 'w_out', 'ffn_norm', 'ffn_w_up', 'ffn_conv_w', 'ffn_conv_b', 'ffn_w_down', 'final_norm']
TWIN_WEIGHTS = ['attn_norm', 'w_in', 'ssd_conv_w', 'ssd_conv_b', 'ssd_dt_bias', 'ssd_a_log', 'ssd_d', 'ssd_norm', 'pool_w', 'pool_scale', 'mla_q_norm', 'mla_w_uq', 'mla_kv_norm', 'mla_w_ukv', 'w_out', 'ffn_norm', 'ffn_w_up', 'ffn_conv_w', 'ffn_conv_b', 'ffn_w_down', 'final_norm']
TWIN_DIFF_INPUT = 'x'
TWIN_INPUTS = ['x', 'positions', 'attn_norm', 'w_in', 'ssd_conv_w', 'ssd_conv_b', 'ssd_dt_bias', 'ssd_a_log', 'ssd_d', 'ssd_norm', 'pool_w', 'pool_scale', 'mla_q_norm', 'mla_w_uq', 'mla_kv_norm', 'mla_w_ukv', 'w_out', 'ffn_norm', 'ffn_w_up', 'ffn_conv_w', 'ffn_conv_b', 'ffn_w_down', 'final_norm', 'loss_target', 'm_attn_norm', 'm_w_in', 'm_ssd_conv_w', 'm_ssd_conv_b', 'm_ssd_dt_bias', 'm_ssd_a_log', 'm_ssd_d', 'm_ssd_norm', 'm_pool_w', 'm_pool_scale', 'm_mla_q_norm', 'm_mla_w_uq', 'm_mla_kv_norm', 'm_mla_w_ukv', 'm_w_out', 'm_ffn_norm', 'm_ffn_w_up', 'm_ffn_conv_w', 'm_ffn_conv_b', 'm_ffn_w_down', 'm_final_norm', 'v_attn_norm', 'v_w_in', 'v_ssd_conv_w', 'v_ssd_conv_b', 'v_ssd_dt_bias', 'v_ssd_a_log', 'v_ssd_d', 'v_ssd_norm', 'v_pool_w', 'v_pool_scale', 'v_mla_q_norm', 'v_mla_w_uq', 'v_mla_kv_norm', 'v_mla_w_ukv', 'v_w_out', 'v_ffn_norm', 'v_ffn_w_up', 'v_ffn_conv_w', 'v_ffn_conv_b', 'v_ffn_w_down', 'v_final_norm']
TWIN_OUTPUTS = ['loss', 'grad_x', 'grad_attn_norm', 'grad_w_in', 'grad_ssd_conv_w', 'grad_ssd_conv_b', 'grad_ssd_dt_bias', 'grad_ssd_a_log', 'grad_ssd_d', 'grad_ssd_norm', 'grad_pool_w', 'grad_pool_scale', 'grad_mla_q_norm', 'grad_mla_w_uq', 'grad_mla_kv_norm', 'grad_mla_w_ukv', 'grad_w_out', 'grad_ffn_norm', 'grad_ffn_w_up', 'grad_ffn_conv_w', 'grad_ffn_conv_b', 'grad_ffn_w_down', 'grad_final_norm', 'delta_attn_norm', 'delta_w_in', 'delta_ssd_conv_w', 'delta_ssd_conv_b', 'delta_ssd_dt_bias', 'delta_ssd_a_log', 'delta_ssd_d', 'delta_ssd_norm', 'delta_pool_w', 'delta_pool_scale', 'delta_mla_q_norm', 'delta_mla_w_uq', 'delta_mla_kv_norm', 'delta_mla_w_ukv', 'delta_w_out', 'delta_ffn_norm', 'delta_ffn_w_up', 'delta_ffn_conv_w', 'delta_ffn_conv_b', 'delta_ffn_w_down', 'delta_final_norm', 'new_m_attn_norm', 'new_m_w_in', 'new_m_ssd_conv_w', 'new_m_ssd_conv_b', 'new_m_ssd_dt_bias', 'new_m_ssd_a_log', 'new_m_ssd_d', 'new_m_ssd_norm', 'new_m_pool_w', 'new_m_pool_scale', 'new_m_mla_q_norm', 'new_m_mla_w_uq', 'new_m_mla_kv_norm', 'new_m_mla_w_ukv', 'new_m_w_out', 'new_m_ffn_norm', 'new_m_ffn_w_up', 'new_m_ffn_conv_w', 'new_m_ffn_conv_b', 'new_m_ffn_w_down', 'new_m_final_norm', 'new_v_attn_norm', 'new_v_w_in', 'new_v_ssd_conv_w', 'new_v_ssd_conv_b', 'new_v_ssd_dt_bias', 'new_v_ssd_a_log', 'new_v_ssd_d', 'new_v_ssd_norm', 'new_v_pool_w', 'new_v_pool_scale', 'new_v_mla_q_norm', 'new_v_mla_w_uq', 'new_v_mla_kv_norm', 'new_v_mla_w_ukv', 'new_v_w_out', 'new_v_ffn_norm', 'new_v_ffn_w_up', 'new_v_ffn_conv_w', 'new_v_ffn_conv_b', 'new_v_ffn_w_down', 'new_v_final_norm']
TWIN_LEAF_KINDS = {'loss': 'loss', 'grad_x': 'grad_x', 'grad_attn_norm': 'grad_w', 'grad_w_in': 'grad_w', 'grad_ssd_conv_w': 'grad_w', 'grad_ssd_conv_b': 'grad_w', 'grad_ssd_dt_bias': 'grad_w', 'grad_ssd_a_log': 'grad_w', 'grad_ssd_d': 'grad_w', 'grad_ssd_norm': 'grad_w', 'grad_pool_w': 'grad_w', 'grad_pool_scale': 'grad_w', 'grad_mla_q_norm': 'grad_w', 'grad_mla_w_uq': 'grad_w', 'grad_mla_kv_norm': 'grad_w', 'grad_mla_w_ukv': 'grad_w', 'grad_w_out': 'grad_w', 'grad_ffn_norm': 'grad_w', 'grad_ffn_w_up': 'grad_w', 'grad_ffn_conv_w': 'grad_w', 'grad_ffn_conv_b': 'grad_w', 'grad_ffn_w_down': 'grad_w', 'grad_final_norm': 'grad_w', 'delta_attn_norm': 'delta_w', 'delta_w_in': 'delta_w', 'delta_ssd_conv_w': 'delta_w', 'delta_ssd_conv_b': 'delta_w', 'delta_ssd_dt_bias': 'delta_w', 'delta_ssd_a_log': 'delta_w', 'delta_ssd_d': 'delta_w', 'delta_ssd_norm': 'delta_w', 'delta_pool_w': 'delta_w', 'delta_pool_scale': 'delta_w', 'delta_mla_q_norm': 'delta_w', 'delta_mla_w_uq': 'delta_w', 'delta_mla_kv_norm': 'delta_w', 'delta_mla_w_ukv': 'delta_w', 'delta_w_out': 'delta_w', 'delta_ffn_norm': 'delta_w', 'delta_ffn_w_up': 'delta_w', 'delta_ffn_conv_w': 'delta_w', 'delta_ffn_conv_b': 'delta_w', 'delta_ffn_w_down': 'delta_w', 'delta_final_norm': 'delta_w', 'new_m_attn_norm': 'new_m', 'new_m_w_in': 'new_m', 'new_m_ssd_conv_w': 'new_m', 'new_m_ssd_conv_b': 'new_m', 'new_m_ssd_dt_bias': 'new_m', 'new_m_ssd_a_log': 'new_m', 'new_m_ssd_d': 'new_m', 'new_m_ssd_norm': 'new_m', 'new_m_pool_w': 'new_m', 'new_m_pool_scale': 'new_m', 'new_m_mla_q_norm': 'new_m', 'new_m_mla_w_uq': 'new_m', 'new_m_mla_kv_norm': 'new_m', 'new_m_mla_w_ukv': 'new_m', 'new_m_w_out': 'new_m', 'new_m_ffn_norm': 'new_m', 'new_m_ffn_w_up': 'new_m', 'new_m_ffn_conv_w': 'new_m', 'new_m_ffn_conv_b': 'new_m', 'new_m_ffn_w_down': 'new_m', 'new_m_final_norm': 'new_m', 'new_v_attn_norm': 'new_v', 'new_v_w_in': 'new_v', 'new_v_ssd_conv_w': 'new_v', 'new_v_ssd_conv_b': 'new_v', 'new_v_ssd_dt_bias': 'new_v', 'new_v_ssd_a_log': 'new_v', 'new_v_ssd_d': 'new_v', 'new_v_ssd_norm': 'new_v', 'new_v_pool_w': 'new_v', 'new_v_pool_scale': 'new_v', 'new_v_mla_q_norm': 'new_v', 'new_v_mla_w_uq': 'new_v', 'new_v_mla_kv_norm': 'new_v', 'new_v_mla_w_ukv': 'new_v', 'new_v_w_out': 'new_v', 'new_v_ffn_norm': 'new_v', 'new_v_ffn_w_up': 'new_v', 'new_v_ffn_conv_w': 'new_v', 'new_v_ffn_conv_b': 'new_v', 'new_v_ffn_w_down': 'new_v', 'new_v_final_norm': 'new_v'}


def _forward(args):
    return _fwd_reference(*[args[k] for k in FWD_PARAMS])


def _output_shape():
    out = _jax.eval_shape(lambda: _forward(_fwd_setup_inputs(0)))
    return out.shape, out.dtype

N_MICROBATCH = 1
ADAM_LR = 0.001
ADAM_B1 = 0.9
ADAM_B2 = 0.999
ADAM_EPS = 1e-08
ADAM_WD = 0.01
ADAM_STEP = 10
PER_EXAMPLE_BATCH_AXIS = {'x': 0, 'positions': 0, 'loss_target': 0}
SHARED_INPUTS = []
_WEIGHT_DTYPES = {'attn_norm': _jnp.float32, 'w_in': _jnp.float32, 'ssd_conv_w': _jnp.float32, 'ssd_conv_b': _jnp.float32, 'ssd_dt_bias': _jnp.float32, 'ssd_a_log': _jnp.float32, 'ssd_d': _jnp.float32, 'ssd_norm': _jnp.float32, 'pool_w': _jnp.float32, 'pool_scale': _jnp.float32, 'mla_q_norm': _jnp.float32, 'mla_w_uq': _jnp.float32, 'mla_kv_norm': _jnp.float32, 'mla_w_ukv': _jnp.float32, 'w_out': _jnp.float32, 'ffn_norm': _jnp.float32, 'ffn_w_up': _jnp.float32, 'ffn_conv_w': _jnp.float32, 'ffn_conv_b': _jnp.float32, 'ffn_w_down': _jnp.float32, 'final_norm': _jnp.float32}
MOMENT_SCALE = {'attn_norm': 2.400376e-01, 'w_in': 1.268497e-01, 'ssd_conv_w': 1.298669e-01, 'ssd_conv_b': 1.990375e-01, 'ssd_dt_bias': 3.804062e-01, 'ssd_a_log': 1.269558e+00, 'ssd_d': 7.725423e-01, 'ssd_norm': 1.553342e-01, 'pool_w': 1.280927e-01, 'pool_scale': 1.232647e-01, 'mla_q_norm': 3.187556e-02, 'mla_w_uq': 2.176084e-02, 'mla_kv_norm': 5.866096e-02, 'mla_w_ukv': 2.904966e-02, 'w_out': 1.744932e-01, 'ffn_norm': 1.713855e-01, 'ffn_w_up': 6.552749e-02, 'ffn_conv_w': 6.522694e-02, 'ffn_conv_b': 6.772893e-02, 'ffn_w_down': 1.074874e-01, 'final_norm': 6.376006e+01}


def _to_microbatches(a, axis):
    t = _jnp.moveaxis(a, axis, 0)
    t = t.reshape((N_MICROBATCH, t.shape[0] // N_MICROBATCH) + t.shape[1:])
    return _jnp.moveaxis(t, 1, axis + 1)


def setup_inputs(seed: int = 0) -> dict:
    inp = _fwd_setup_inputs(seed)
    key = _jax.random.fold_in(_jax.random.key(seed), 7919)
    shape, _ = _output_shape()
    out = dict(inp)
    out["loss_target"] = _jax.random.normal(_jax.random.fold_in(key, 0), shape, _jnp.float32)
    for i, name in enumerate(TWIN_WEIGHTS):
        w = inp[name].astype(_jnp.float32)
        if MOMENT_SCALE is None:
            s = _jnp.sqrt(_jnp.mean(_jnp.square(w)) + 1e-30)
        else:
            s = MOMENT_SCALE[name]
        km, kv = _jax.random.split(_jax.random.fold_in(key, i + 1))
        out[name] = w
        out["m_" + name] = s * _jax.random.normal(km, w.shape, _jnp.float32)
        out["v_" + name] = (s * s) * _jax.random.uniform(kv, w.shape, _jnp.float32, 0.5, 1.5)
    if N_MICROBATCH > 1:
        for name, axis in PER_EXAMPLE_BATCH_AXIS.items():
            out[name] = _to_microbatches(out[name], axis)
    return {'x': out['x'], 'positions': out['positions'], 'attn_norm': out['attn_norm'], 'w_in': out['w_in'], 'ssd_conv_w': out['ssd_conv_w'], 'ssd_conv_b': out['ssd_conv_b'], 'ssd_dt_bias': out['ssd_dt_bias'], 'ssd_a_log': out['ssd_a_log'], 'ssd_d': out['ssd_d'], 'ssd_norm': out['ssd_norm'], 'pool_w': out['pool_w'], 'pool_scale': out['pool_scale'], 'mla_q_norm': out['mla_q_norm'], 'mla_w_uq': out['mla_w_uq'], 'mla_kv_norm': out['mla_kv_norm'], 'mla_w_ukv': out['mla_w_ukv'], 'w_out': out['w_out'], 'ffn_norm': out['ffn_norm'], 'ffn_w_up': out['ffn_w_up'], 'ffn_conv_w': out['ffn_conv_w'], 'ffn_conv_b': out['ffn_conv_b'], 'ffn_w_down': out['ffn_w_down'], 'final_norm': out['final_norm'], 'loss_target': out['loss_target'], 'm_attn_norm': out['m_attn_norm'], 'm_w_in': out['m_w_in'], 'm_ssd_conv_w': out['m_ssd_conv_w'], 'm_ssd_conv_b': out['m_ssd_conv_b'], 'm_ssd_dt_bias': out['m_ssd_dt_bias'], 'm_ssd_a_log': out['m_ssd_a_log'], 'm_ssd_d': out['m_ssd_d'], 'm_ssd_norm': out['m_ssd_norm'], 'm_pool_w': out['m_pool_w'], 'm_pool_scale': out['m_pool_scale'], 'm_mla_q_norm': out['m_mla_q_norm'], 'm_mla_w_uq': out['m_mla_w_uq'], 'm_mla_kv_norm': out['m_mla_kv_norm'], 'm_mla_w_ukv': out['m_mla_w_ukv'], 'm_w_out': out['m_w_out'], 'm_ffn_norm': out['m_ffn_norm'], 'm_ffn_w_up': out['m_ffn_w_up'], 'm_ffn_conv_w': out['m_ffn_conv_w'], 'm_ffn_conv_b': out['m_ffn_conv_b'], 'm_ffn_w_down': out['m_ffn_w_down'], 'm_final_norm': out['m_final_norm'], 'v_attn_norm': out['v_attn_norm'], 'v_w_in': out['v_w_in'], 'v_ssd_conv_w': out['v_ssd_conv_w'], 'v_ssd_conv_b': out['v_ssd_conv_b'], 'v_ssd_dt_bias': out['v_ssd_dt_bias'], 'v_ssd_a_log': out['v_ssd_a_log'], 'v_ssd_d': out['v_ssd_d'], 'v_ssd_norm': out['v_ssd_norm'], 'v_pool_w': out['v_pool_w'], 'v_pool_scale': out['v_pool_scale'], 'v_mla_q_norm': out['v_mla_q_norm'], 'v_mla_w_uq': out['v_mla_w_uq'], 'v_mla_kv_norm': out['v_mla_kv_norm'], 'v_mla_w_ukv': out['v_mla_w_ukv'], 'v_w_out': out['v_w_out'], 'v_ffn_norm': out['v_ffn_norm'], 'v_ffn_w_up': out['v_ffn_w_up'], 'v_ffn_conv_w': out['v_ffn_conv_w'], 'v_ffn_conv_b': out['v_ffn_conv_b'], 'v_ffn_w_down': out['v_ffn_w_down'], 'v_final_norm': out['v_final_norm']}


def _loss(weights, diff, rest, loss_target):
    with _jax.named_scope("forward"):
        args = {**rest, TWIN_DIFF_INPUT: diff, **{k: w.astype(_WEIGHT_DTYPES[k]) for k, w in weights.items()}}
        y = _forward(args)
    with _jax.named_scope("loss_head"):
        err = _jnp.square(y.astype(_jnp.float32) - loss_target)
        return 0.5 * _jnp.sum(_jnp.mean(err, axis=-1)) if err.ndim else 0.5 * err


def _adamw(w, g, m, v):
    m = ADAM_B1 * m + (1.0 - ADAM_B1) * g
    v = ADAM_B2 * v + (1.0 - ADAM_B2) * _jnp.square(g)
    m_hat = m / (1.0 - ADAM_B1 ** ADAM_STEP)
    v_hat = v / (1.0 - ADAM_B2 ** ADAM_STEP)
    delta = -ADAM_LR * (m_hat / (_jnp.sqrt(v_hat) + ADAM_EPS) + ADAM_WD * w)
    return delta, m, v


def reference(x, positions, attn_norm, w_in, ssd_conv_w, ssd_conv_b, ssd_dt_bias, ssd_a_log, ssd_d, ssd_norm, pool_w, pool_scale, mla_q_norm, mla_w_uq, mla_kv_norm, mla_w_ukv, w_out, ffn_norm, ffn_w_up, ffn_conv_w, ffn_conv_b, ffn_w_down, final_norm, loss_target, m_attn_norm, m_w_in, m_ssd_conv_w, m_ssd_conv_b, m_ssd_dt_bias, m_ssd_a_log, m_ssd_d, m_ssd_norm, m_pool_w, m_pool_scale, m_mla_q_norm, m_mla_w_uq, m_mla_kv_norm, m_mla_w_ukv, m_w_out, m_ffn_norm, m_ffn_w_up, m_ffn_conv_w, m_ffn_conv_b, m_ffn_w_down, m_final_norm, v_attn_norm, v_w_in, v_ssd_conv_w, v_ssd_conv_b, v_ssd_dt_bias, v_ssd_a_log, v_ssd_d, v_ssd_norm, v_pool_w, v_pool_scale, v_mla_q_norm, v_mla_w_uq, v_mla_kv_norm, v_mla_w_ukv, v_w_out, v_ffn_norm, v_ffn_w_up, v_ffn_conv_w, v_ffn_conv_b, v_ffn_w_down, v_final_norm):
    given = dict(x=x, positions=positions, attn_norm=attn_norm, w_in=w_in, ssd_conv_w=ssd_conv_w, ssd_conv_b=ssd_conv_b, ssd_dt_bias=ssd_dt_bias, ssd_a_log=ssd_a_log, ssd_d=ssd_d, ssd_norm=ssd_norm, pool_w=pool_w, pool_scale=pool_scale, mla_q_norm=mla_q_norm, mla_w_uq=mla_w_uq, mla_kv_norm=mla_kv_norm, mla_w_ukv=mla_w_ukv, w_out=w_out, ffn_norm=ffn_norm, ffn_w_up=ffn_w_up, ffn_conv_w=ffn_conv_w, ffn_conv_b=ffn_conv_b, ffn_w_down=ffn_w_down, final_norm=final_norm, loss_target=loss_target, m_attn_norm=m_attn_norm, m_w_in=m_w_in, m_ssd_conv_w=m_ssd_conv_w, m_ssd_conv_b=m_ssd_conv_b, m_ssd_dt_bias=m_ssd_dt_bias, m_ssd_a_log=m_ssd_a_log, m_ssd_d=m_ssd_d, m_ssd_norm=m_ssd_norm, m_pool_w=m_pool_w, m_pool_scale=m_pool_scale, m_mla_q_norm=m_mla_q_norm, m_mla_w_uq=m_mla_w_uq, m_mla_kv_norm=m_mla_kv_norm, m_mla_w_ukv=m_mla_w_ukv, m_w_out=m_w_out, m_ffn_norm=m_ffn_norm, m_ffn_w_up=m_ffn_w_up, m_ffn_conv_w=m_ffn_conv_w, m_ffn_conv_b=m_ffn_conv_b, m_ffn_w_down=m_ffn_w_down, m_final_norm=m_final_norm, v_attn_norm=v_attn_norm, v_w_in=v_w_in, v_ssd_conv_w=v_ssd_conv_w, v_ssd_conv_b=v_ssd_conv_b, v_ssd_dt_bias=v_ssd_dt_bias, v_ssd_a_log=v_ssd_a_log, v_ssd_d=v_ssd_d, v_ssd_norm=v_ssd_norm, v_pool_w=v_pool_w, v_pool_scale=v_pool_scale, v_mla_q_norm=v_mla_q_norm, v_mla_w_uq=v_mla_w_uq, v_mla_kv_norm=v_mla_kv_norm, v_mla_w_ukv=v_mla_w_ukv, v_w_out=v_w_out, v_ffn_norm=v_ffn_norm, v_ffn_w_up=v_ffn_w_up, v_ffn_conv_w=v_ffn_conv_w, v_ffn_conv_b=v_ffn_conv_b, v_ffn_w_down=v_ffn_w_down, v_final_norm=v_final_norm)
    weights = {n: given[n] for n in TWIN_WEIGHTS}
    shared = {n: given[n] for n in SHARED_INPUTS}
    per_example = {n: given[n] for n in ['x', 'positions']}
    grad_fn = _jax.value_and_grad(_loss, argnums=(0, 1))

    def one_microbatch(ex, loss_target):
        ex = dict(ex)
        diff = ex.pop(TWIN_DIFF_INPUT)
        return grad_fn(weights, diff, {**shared, **ex}, loss_target)

    if N_MICROBATCH == 1:
        loss, (grad_w, grad_x) = one_microbatch(per_example, given["loss_target"])
    else:
        def body(carry, xs):
            loss_sum, grad_sum = carry
            l_k, (gw_k, gx_k) = one_microbatch(xs[0], xs[1])
            with _jax.named_scope("update"):
                return (loss_sum + l_k, _jax.tree.map(_jnp.add, grad_sum, gw_k)), gx_k

        init = (_jnp.zeros((), _jnp.float32), _jax.tree.map(_jnp.zeros_like, weights))
        (loss, grad_w), grad_x = _jax.lax.scan(body, init, (per_example, given["loss_target"]))
    with _jax.named_scope("update"):
        delta_w, new_m, new_v = {}, {}, {}
        for n in TWIN_WEIGHTS:
            delta_w[n], new_m[n], new_v[n] = _adamw(weights[n], grad_w[n], given["m_" + n], given["v_" + n])
    return (loss, grad_x, *[grad_w[n] for n in TWIN_WEIGHTS], *[delta_w[n] for n in TWIN_WEIGHTS],
            *[new_m[n] for n in TWIN_WEIGHTS], *[new_v[n] for n in TWIN_WEIGHTS])
```

```python
import functools
import math

import jax
import jax.numpy as jnp
from jax import lax
from jax.experimental import pallas as pl
from jax.experimental.pallas import tpu as pltpu

F32, BF16 = jnp.float32, jnp.bfloat16
SDS = jax.ShapeDtypeStruct
MESH = pl.DeviceIdType.MESH
AXES = ("x", "y", "c")
N_DEV = 8

D = 1024
EPS = 1e-6
SSD_HEADS, SSD_P, SSD_W, SSD_G, SSD_N, SSD_K, SSD_L, SSD_XBC = 16, 64, 1024, 2, 128, 4, 128, 1536
POOL_G, POOL_D, POOL_W, POOL_WIN = 4, 128, 512, (2, 4, 8, 16)
MLA_H, MLA_QR, MLA_KVR, MLA_NOPE, MLA_ROPE, MLA_V, MLA_QK = 8, 384, 256, 64, 32, 64, 96
ROPE_THETA = 10000.0
MIX = 2048
DFF, FFN_K = 2816, 3
DEPTH = 2
ADAM_LR, ADAM_B1, ADAM_B2, ADAM_EPS, ADAM_WD, ADAM_STEP = 0.001, 0.9, 0.999, 1e-08, 0.01, 10

Z0, XBC0, U0, CQ0, CKV0, DT0, KPE0, PW = 0, 1024, 2560, 3072, 3584, 3840, 3968, 4096
LANES = 128
YCAT = SSD_W + POOL_W + MLA_H * LANES
NEG = -1e30
VMEM_LIMIT = 56 * 1024 * 1024


def _tile(n, pref, mult):
    if n <= pref:
        return n
    for d in range(pref, 0, -mult):
        if d % mult == 0 and n % d == 0:
            return d
    return n


def _dg(a, b, ca, cb, prec=None):
    return lax.dot_general(a, b, (((ca,), (cb,)), ((), ())), preferred_element_type=F32, precision=prec)


def _nn(a, b):
    return _dg(a.astype(BF16), b.astype(BF16), 1, 0)


def _nt(a, b):
    return _dg(a.astype(BF16), b.astype(BF16), 1, 1)


def _tn(a, b):
    return _dg(a.astype(BF16), b.astype(BF16), 0, 0)


def _sig(x):
    return jax.nn.sigmoid(x)


def _silu(x):
    return x * _sig(x)


def _dsilu(x):
    s = _sig(x)
    return s * (1.0 + x * (1.0 - s))


def _pc(body, *, name, grid, in_specs, out_specs, out_shape, scratch=()):
    return pl.pallas_call(
        body, name=name, grid=grid, in_specs=in_specs, out_specs=out_specs, out_shape=out_shape,
        scratch_shapes=list(scratch), compiler_params=pltpu.CompilerParams(vmem_limit_bytes=VMEM_LIMIT))


def _rsum(x):
    return jnp.sum(x, axis=1, keepdims=True)


def _csum(x):
    return jnp.sum(x, axis=0, keepdims=True)


def mm(a, b, *, ta=False, tb=False, add=None, out_dtype=F32, name):
    M, K = (a.shape[1], a.shape[0]) if ta else a.shape
    N = b.shape[0] if tb else b.shape[1]
    tm = _tile(M, 512, LANES if ta else 8)
    tn = _tile(N, 512, LANES)
    tk = _tile(K, 1024, LANES)
    nk = K // tk

    def body(*refs):
        if add is None:
            a_ref, b_ref, o_ref, acc = refs
        else:
            a_ref, b_ref, add_ref, o_ref, acc = refs
        k = pl.program_id(2)

        @pl.when(k == 0)
        def _():
            acc[...] = jnp.zeros_like(acc)

        acc[...] += _dg(a_ref[...].astype(BF16), b_ref[...].astype(BF16), 0 if ta else 1, 1 if tb else 0)

        @pl.when(k == nk - 1)
        def _():
            r = acc[...]
            if add is not None:
                r = r + add_ref[...].astype(F32)
            o_ref[...] = r.astype(out_dtype)

    a_spec = pl.BlockSpec((tk, tm), lambda i, j, k: (k, i)) if ta else pl.BlockSpec((tm, tk), lambda i, j, k: (i, k))
    b_spec = pl.BlockSpec((tn, tk), lambda i, j, k: (j, k)) if tb else pl.BlockSpec((tk, tn), lambda i, j, k: (k, j))
    o_spec = pl.BlockSpec((tm, tn), lambda i, j, k: (i, j))
    ins, specs = [a, b], [a_spec, b_spec]
    if add is not None:
        ins.append(add)
        specs.append(o_spec)
    return _pc(body, name=name, grid=(M // tm, N // tn, nk), in_specs=specs, out_specs=o_spec,
               out_shape=SDS((M, N), out_dtype), scratch=[pltpu.VMEM((tm, tn), F32)])(*ins)


def rms_fwd(x, g, *, col0=0, width=None, name):
    T = x.shape[0]
    W = width or x.shape[1]
    tm = _tile(T, 512, 8)

    def body(x_ref, g_ref, o_ref):
        v = x_ref[...]
        r = lax.rsqrt(jnp.mean(v * v, axis=1, keepdims=True) + EPS)
        o_ref[...] = ((v * r) * g_ref[...]).astype(BF16)

    return _pc(body, name=name, grid=(T // tm,),
               in_specs=[pl.BlockSpec((tm, W), lambda i: (i, col0 // W)), pl.BlockSpec((1, W), lambda i: (0, 0))],
               out_specs=pl.BlockSpec((tm, W), lambda i: (i, 0)), out_shape=SDS((T, W), BF16))(x, g)


def rms_bwd(x, g, dh, *, col0=0, width=None, add=None, name):
    T = x.shape[0]
    W = width or x.shape[1]
    tm = _tile(T, 512, 8)

    def body(*refs):
        if add is None:
            x_ref, g_ref, dh_ref, dx_ref, dg_ref = refs
        else:
            x_ref, g_ref, dh_ref, add_ref, dx_ref, dg_ref = refs
        v = x_ref[...]
        r = lax.rsqrt(jnp.mean(v * v, axis=1, keepdims=True) + EPS)
        xh = v * r
        d = dh_ref[...].astype(F32)
        dxh = d * g_ref[...]
        dx = r * (dxh - xh * jnp.mean(dxh * xh, axis=1, keepdims=True))
        if add is not None:
            dx = dx + add_ref[...]
        dx_ref[...] = dx

        @pl.when(pl.program_id(0) == 0)
        def _():
            dg_ref[...] = jnp.zeros_like(dg_ref)

        dg_ref[...] += _csum(d * xh)

    row = pl.BlockSpec((tm, W), lambda i: (i, 0))
    vec = pl.BlockSpec((1, W), lambda i: (0, 0))
    ins = [x, g, dh] + ([] if add is None else [add])
    specs = [pl.BlockSpec((tm, W), lambda i: (i, col0 // W)), vec, row] + ([] if add is None else [row])
    return _pc(body, name=name, grid=(T // tm,), in_specs=specs, out_specs=[row, vec],
               out_shape=[SDS((T, W), F32), SDS((1, W), F32)])(*ins)


def gated_rms_fwd(y, proj, g, *, name):
    T = y.shape[0]
    tm = _tile(T, 512, 8)

    def body(y_ref, z_ref, g_ref, o_ref):
        v = y_ref[...] * _silu(z_ref[...])
        r = lax.rsqrt(jnp.mean(v * v, axis=1, keepdims=True) + EPS)
        o_ref[...] = ((v * r) * g_ref[...]).astype(BF16)

    row = pl.BlockSpec((tm, SSD_W), lambda i: (i, 0))
    return _pc(body, name=name, grid=(T // tm,), in_specs=[row, row, pl.BlockSpec((1, SSD_W), lambda i: (0, 0))],
               out_specs=row, out_shape=SDS((T, SSD_W), BF16))(y, proj, g)


def gated_rms_bwd(y, proj, g, dycat, *, name):
    T = y.shape[0]
    tm = _tile(T, 512, 8)

    def body(y_ref, z_ref, g_ref, d_ref, dy_ref, dz_ref, dg_ref):
        yv, z = y_ref[...], z_ref[...]
        sz = _silu(z)
        v = yv * sz
        r = lax.rsqrt(jnp.mean(v * v, axis=1, keepdims=True) + EPS)
        vh = v * r
        d = d_ref[...]
        dvh = d * g_ref[...]
        dv = r * (dvh - vh * jnp.mean(dvh * vh, axis=1, keepdims=True))
        dy_ref[...] = dv * sz
        dz_ref[...] = dv * yv * _dsilu(z)

        @pl.when(pl.program_id(0) == 0)
        def _():
            dg_ref[...] = jnp.zeros_like(dg_ref)

        dg_ref[...] += _csum(d * vh)

    row = pl.BlockSpec((tm, SSD_W), lambda i: (i, 0))
    vec = pl.BlockSpec((1, SSD_W), lambda i: (0, 0))
    return _pc(body, name=name, grid=(T // tm,), in_specs=[row, row, vec, row], out_specs=[row, row, vec],
               out_shape=[SDS((T, SSD_W), F32), SDS((T, SSD_W), F32), SDS((1, SSD_W), F32)])(y, proj, g, dycat)


def final_loss(x, g, tgt, *, name):
    T = x.shape[0]
    tm = _tile(T, 512, 8)

    def body(x_ref, g_ref, t_ref, dx_ref, dg_ref, l_ref):
        v = x_ref[...]
        gg = g_ref[...]
        r = lax.rsqrt(jnp.mean(v * v, axis=1, keepdims=True) + EPS)
        xh = v * r
        err = xh * gg - t_ref[...]
        part = 0.5 * _csum(jnp.mean(err * err, axis=1, keepdims=True))
        d = err * (1.0 / D)
        dxh = d * gg
        dx_ref[...] = r * (dxh - xh * jnp.mean(dxh * xh, axis=1, keepdims=True))

        @pl.when(pl.program_id(0) == 0)
        def _():
            dg_ref[...] = jnp.zeros_like(dg_ref)
            l_ref[...] = jnp.zeros_like(l_ref)

        dg_ref[...] += _csum(d * xh)
        l_ref[...] += jnp.broadcast_to(part, (1, LANES))

    row = pl.BlockSpec((tm, D), lambda i: (i, 0))
    vec = pl.BlockSpec((1, D), lambda i: (0, 0))
    return _pc(body, name=name, grid=(T // tm,), in_specs=[row, vec, row],
               out_specs=[row, vec, pl.BlockSpec((1, LANES), lambda i: (0, 0))],
               out_shape=[SDS((T, D), F32), SDS((1, D), F32), SDS((1, LANES), F32)])(x, g, tgt)


HALO = 8


def _prev_map(ts, col):
    return lambda b, i, j: (b, jnp.maximum(i * (ts // HALO) - 1, 0), col(j))


def _next_map(ts, n_halo_blocks, col):
    return lambda b, i, j: (b, jnp.minimum((i + 1) * (ts // HALO), n_halo_blocks - 1), col(j))


def conv_silu_fwd(proj3, w, b, *, name):
    Bl, S, _ = proj3.shape
    C, K = SSD_XBC, SSD_K
    ts, tc = _tile(S, 512, 8), 512
    c0 = XBC0 // tc

    def body(xp_ref, x_ref, w_ref, b_ref, o_ref, ext):
        i = pl.program_id(1)
        ext[0:HALO, :] = jnp.where(i > 0, xp_ref[0], 0.0)
        ext[HALO:HALO + ts, :] = x_ref[0]
        acc = b_ref[...] + w_ref[0:1, :] * ext[pl.ds(HALO - (K - 1), ts), :]
        for k in range(1, K):
            acc = acc + w_ref[k:k + 1, :] * ext[pl.ds(HALO - (K - 1) + k, ts), :]
        o_ref[0] = _silu(acc)

    return _pc(body, name=name, grid=(Bl, S // ts, C // tc),
               in_specs=[pl.BlockSpec((1, HALO, tc), _prev_map(ts, lambda j: c0 + j)),
                         pl.BlockSpec((1, ts, tc), lambda b, i, j: (b, i, c0 + j)),
                         pl.BlockSpec((K, tc), lambda b, i, j: (0, j)),
                         pl.BlockSpec((1, tc), lambda b, i, j: (0, j))],
               out_specs=pl.BlockSpec((1, ts, tc), lambda b, i, j: (b, i, j)),
               out_shape=SDS((Bl, S, C), F32), scratch=[pltpu.VMEM((HALO + ts, tc), F32)])(proj3, proj3, w, b)


def conv_silu_bwd(proj3, w, b, dact, *, name):
    Bl, S, _ = proj3.shape
    C, K = SSD_XBC, SSD_K
    ts, tc = _tile(S, 512, 8), 512
    c0 = XBC0 // tc
    ns = S // ts

    def body(xp_ref, x_ref, xn_ref, d_ref, dn_ref, w_ref, b_ref, dx_ref, dw_ref, db_ref, ext, ext2):
        bb, i = pl.program_id(1), pl.program_id(2)
        last = i == ns - 1
        ext[0:HALO, :] = jnp.where(i > 0, xp_ref[0], 0.0)
        ext[HALO:HALO + ts, :] = x_ref[0]
        ext[HALO + ts:2 * HALO + ts, :] = jnp.where(last, 0.0, xn_ref[0])
        acc = b_ref[...] + w_ref[0:1, :] * ext[pl.ds(HALO - (K - 1), ts + HALO), :]
        for k in range(1, K):
            acc = acc + w_ref[k:k + 1, :] * ext[pl.ds(HALO - (K - 1) + k, ts + HALO), :]
        dsl = _dsilu(acc)
        du = d_ref[0] * dsl[0:ts]
        ext2[0:ts, :] = du
        ext2[ts:ts + HALO, :] = jnp.where(last, 0.0, dn_ref[0]) * dsl[ts:ts + HALO]
        dx = w_ref[0:1, :] * ext2[pl.ds(K - 1, ts), :]
        for k in range(1, K):
            dx = dx + w_ref[k:k + 1, :] * ext2[pl.ds(K - 1 - k, ts), :]
        dx_ref[0] = dx

        @pl.when((bb == 0) & (i == 0))
        def _():
            dw_ref[...] = jnp.zeros_like(dw_ref)
            db_ref[...] = jnp.zeros_like(db_ref)

        for k in range(K):
            dw_ref[k:k + 1, :] += _csum(du * ext[pl.ds(HALO - (K - 1) + k, ts), :])
        db_ref[...] += _csum(du)

    nhb = S // HALO
    cx = lambda j: c0 + j
    cj = lambda j: j
    return _pc(body, name=name, grid=(C // tc, Bl, ns),
               in_specs=[pl.BlockSpec((1, HALO, tc), lambda j, b, i: _prev_map(ts, cx)(b, i, j)),
                         pl.BlockSpec((1, ts, tc), lambda j, b, i: (b, i, c0 + j)),
                         pl.BlockSpec((1, HALO, tc), lambda j, b, i: _next_map(ts, nhb, cx)(b, i, j)),
                         pl.BlockSpec((1, ts, tc), lambda j, b, i: (b, i, j)),
                         pl.BlockSpec((1, HALO, tc), lambda j, b, i: _next_map(ts, nhb, cj)(b, i, j)),
                         pl.BlockSpec((K, tc), lambda j, b, i: (0, j)),
                         pl.BlockSpec((1, tc), lambda j, b, i: (0, j))],
               out_specs=[pl.BlockSpec((1, ts, tc), lambda j, b, i: (b, i, j)),
                          pl.BlockSpec((K, tc), lambda j, b, i: (0, j)),
                          pl.BlockSpec((1, tc), lambda j, b, i: (0, j))],
               out_shape=[SDS((Bl, S, C), F32), SDS((K, C), F32), SDS((1, C), F32)],
               scratch=[pltpu.VMEM((2 * HALO + ts, tc), F32), pltpu.VMEM((HALO + ts, tc), F32)],
               )(proj3, proj3, proj3, dact, dact, w, b)


def ffn_act_fwd(pre3, w, b, *, name):
    Bl, S, _ = pre3.shape
    K = FFN_K
    ts, tc = _tile(S, 512, 8), 256
    nj = DFF // tc

    def body(gp_ref, g_ref, vp_ref, v_ref, wg_ref, wv_ref, bg_ref, bv_ref, o_ref, eg, ev):
        i = pl.program_id(1)
        outs = []
        for p_ref, m_ref, w_ref, b_ref, ext in ((gp_ref, g_ref, wg_ref, bg_ref, eg), (vp_ref, v_ref, wv_ref, bv_ref, ev)):
            ext[0:HALO, :] = jnp.where(i > 0, p_ref[0], 0.0)
            ext[HALO:HALO + ts, :] = m_ref[0]
            acc = b_ref[...] + w_ref[0:1, :] * ext[pl.ds(HALO - (K - 1), ts), :]
            for k in range(1, K):
                acc = acc + w_ref[k:k + 1, :] * ext[pl.ds(HALO - (K - 1) + k, ts), :]
            outs.append(acc)
        o_ref[0] = (_silu(outs[0]) * outs[1]).astype(BF16)

    main = lambda off: pl.BlockSpec((1, ts, tc), lambda b, i, j: (b, i, off + j))
    prev = lambda off: pl.BlockSpec((1, HALO, tc), _prev_map(ts, lambda j: off + j))
    wsp = lambda off: pl.BlockSpec((K, tc), lambda b, i, j: (0, off + j))
    bsp = lambda off: pl.BlockSpec((1, tc), lambda b, i, j: (0, off + j))
    return _pc(body, name=name, grid=(Bl, S // ts, nj),
               in_specs=[prev(0), main(0), prev(nj), main(nj), wsp(0), wsp(nj), bsp(0), bsp(nj)],
               out_specs=pl.BlockSpec((1, ts, tc), lambda b, i, j: (b, i, j)),
               out_shape=SDS((Bl, S, DFF), BF16),
               scratch=[pltpu.VMEM((HALO + ts, tc), F32), pltpu.VMEM((HALO + ts, tc), F32)],
               )(pre3, pre3, pre3, pre3, w, w, b, b)


def ffn_act_bwd(pre3, w, b, dact, *, name):
    Bl, S, _ = pre3.shape
    K = FFN_K
    ts, tc = _tile(S, 512, 8), 256
    nj = DFF // tc
    ns = S // ts

    def body(gp_ref, g_ref, gn_ref, vp_ref, v_ref, vn_ref, d_ref, dn_ref, wg_ref, wv_ref, bg_ref, bv_ref,
             dg_ref, dv_ref, dwg_ref, dwv_ref, dbg_ref, dbv_ref, eg, ev, e2g, e2v):
        bb, i = pl.program_id(1), pl.program_id(2)
        last = i == ns - 1
        ups = []
        for p_ref, m_ref, n_ref, w_ref, b_ref, ext in ((gp_ref, g_ref, gn_ref, wg_ref, bg_ref, eg),
                                                       (vp_ref, v_ref, vn_ref, wv_ref, bv_ref, ev)):
            ext[0:HALO, :] = jnp.where(i > 0, p_ref[0], 0.0)
            ext[HALO:HALO + ts, :] = m_ref[0]
            ext[HALO + ts:2 * HALO + ts, :] = jnp.where(last, 0.0, n_ref[0])
            acc = b_ref[...] + w_ref[0:1, :] * ext[pl.ds(HALO - (K - 1), ts + HALO), :]
            for k in range(1, K):
                acc = acc + w_ref[k:k + 1, :] * ext[pl.ds(HALO - (K - 1) + k, ts + HALO), :]
            ups.append(acc)
        ug, uv = ups
        dg_e = uv * _dsilu(ug)
        dv_e = _silu(ug)
        d_main = d_ref[0]
        d_next = jnp.where(last, 0.0, dn_ref[0])
        dug = d_main * dg_e[0:ts]
        duv = d_main * dv_e[0:ts]
        e2g[0:ts, :] = dug
        e2g[ts:ts + HALO, :] = d_next * dg_e[ts:ts + HALO]
        e2v[0:ts, :] = duv
        e2v[ts:ts + HALO, :] = d_next * dv_e[ts:ts + HALO]

        @pl.when((bb == 0) & (i == 0))
        def _():
            for r in (dwg_ref, dwv_ref, dbg_ref, dbv_ref):
                r[...] = jnp.zeros_like(r)

        for w_ref, e2, ext, du, o_ref, dw_ref, db_ref in ((wg_ref, e2g, eg, dug, dg_ref, dwg_ref, dbg_ref),
                                                          (wv_ref, e2v, ev, duv, dv_ref, dwv_ref, dbv_ref)):
            dx = w_ref[0:1, :] * e2[pl.ds(K - 1, ts), :]
            for k in range(1, K):
                dx = dx + w_ref[k:k + 1, :] * e2[pl.ds(K - 1 - k, ts), :]
            o_ref[0] = dx.astype(BF16)
            for k in range(K):
                dw_ref[k:k + 1, :] += _csum(du * ext[pl.ds(HALO - (K - 1) + k, ts), :])
            db_ref[...] += _csum(du)

    nhb = S // HALO
    main = lambda off: pl.BlockSpec((1, ts, tc), lambda j, b, i: (b, i, off + j))
    prev = lambda off: pl.BlockSpec((1, HALO, tc), lambda j, b, i: _prev_map(ts, lambda jj: off + jj)(b, i, j))
    nxt = lambda off: pl.BlockSpec((1, HALO, tc), lambda j, b, i: _next_map(ts, nhb, lambda jj: off + jj)(b, i, j))
    wsp = lambda off: pl.BlockSpec((K, tc), lambda j, b, i: (0, off + j))
    bsp = lambda off: pl.BlockSpec((1, tc), lambda j, b, i: (0, off + j))
    outs = _pc(body, name=name, grid=(nj, Bl, ns),
               in_specs=[prev(0), main(0), nxt(0), prev(nj), main(nj), nxt(nj), main(0), nxt(0),
                         wsp(0), wsp(nj), bsp(0), bsp(nj)],
               out_specs=[main(0), main(0), wsp(0), wsp(0), bsp(0), bsp(0)],
               out_shape=[SDS((Bl, S, DFF), BF16), SDS((Bl, S, DFF), BF16), SDS((K, DFF), F32), SDS((K, DFF), F32),
                          SDS((1, DFF), F32), SDS((1, DFF), F32)],
               scratch=[pltpu.VMEM((2 * HALO + ts, tc), F32), pltpu.VMEM((2 * HALO + ts, tc), F32),
                        pltpu.VMEM((HALO + ts, tc), F32), pltpu.VMEM((HALO + ts, tc), F32)],
               )(pre3, pre3, pre3, pre3, pre3, pre3, dact, dact, w, w, b, b)
    return outs


PHALO = 16


def _pool_window_sums(ext, base, ts, step):
    s = ext[pl.ds(base, ts), :]
    out = []
    for i in range(1, PHALO):
        s = s + ext[pl.ds(base + step * i, ts), :]
        if i + 1 in POOL_WIN:
            out.append(s)
    return out


def _pick(g, vals):
    r = vals[-1]
    for k in range(len(vals) - 2, -1, -1):
        r = jnp.where(g == k, vals[k], r)
    return r


def _pool_count(g, i, ts, rows):
    t = (i * ts + lax.broadcasted_iota(jnp.int32, (rows, 1), 0) + 1).astype(F32)
    return jnp.minimum(t, _pick(g, [float(w) for w in POOL_WIN]))


def _pooled(up_ref, u_ref, ext, g, i, ts):
    ext[0:PHALO, :] = jnp.where(i > 0, up_ref[0], 0.0)
    u = u_ref[0]
    ext[PHALO:PHALO + ts, :] = u
    sums = _pool_window_sums(ext, PHALO, ts, -1)
    return _pick(g, sums) / _pool_count(g, i, ts, ts) - u


def pool_fwd(proj3, pool_w, scale, *, name):
    Bl, S, _ = proj3.shape
    ts = _tile(S, 512, 16)
    c0 = U0 // POOL_D

    def body(up_ref, u_ref, w_ref, s_ref, o_ref, ext):
        i, g = pl.program_id(1), pl.program_id(2)
        pooled = _pooled(up_ref, u_ref, ext, g, i, ts)
        o_ref[0] = (_nn(pooled, w_ref[0]) * s_ref[...]).astype(BF16)

    return _pc(body, name=name, grid=(Bl, S // ts, POOL_G),
               in_specs=[pl.BlockSpec((1, PHALO, POOL_D), lambda b, i, g: (b, jnp.maximum(i * (ts // PHALO) - 1, 0), c0 + g)),
                         pl.BlockSpec((1, ts, POOL_D), lambda b, i, g: (b, i, c0 + g)),
                         pl.BlockSpec((1, POOL_D, POOL_D), lambda b, i, g: (g, 0, 0)),
                         pl.BlockSpec((1, POOL_D), lambda b, i, g: (0, g))],
               out_specs=pl.BlockSpec((1, ts, POOL_D), lambda b, i, g: (b, i, g)),
               out_shape=SDS((Bl, S, POOL_W), BF16), scratch=[pltpu.VMEM((PHALO + ts, POOL_D), F32)],
               )(proj3, proj3, pool_w, scale)


def pool_bwd(proj3, pool_w, scale, dycat3, *, name):
    Bl, S, _ = proj3.shape
    ts = _tile(S, 512, 16)
    ns = S // ts
    c0 = U0 // POOL_D
    d0 = SSD_W // POOL_D
    nhb = S // PHALO

    def body(up_ref, u_ref, d_ref, dn_ref, w_ref, s_ref, du_ref, dw_ref, ds_ref, ext, ext2):
        g, bb, i = pl.program_id(0), pl.program_id(1), pl.program_id(2)
        last = i == ns - 1
        pooled = _pooled(up_ref, u_ref, ext, g, i, ts)
        wm = w_ref[0]
        sc = s_ref[...]
        dy = d_ref[0]
        dp_main = dy * sc
        dpool = _nt(dp_main, wm)
        dpool_n = _nt(jnp.where(last, 0.0, dn_ref[0]) * sc, wm)
        ext2[0:ts, :] = dpool / _pool_count(g, i, ts, ts)
        ext2[ts:ts + PHALO, :] = dpool_n / _pool_count(g, i + 1, ts, PHALO)
        sums = _pool_window_sums(ext2, 0, ts, 1)
        du_ref[0] = _pick(g, sums) - dpool

        @pl.when((bb == 0) & (i == 0))
        def _():
            dw_ref[...] = jnp.zeros_like(dw_ref)
            ds_ref[...] = jnp.zeros_like(ds_ref)

        dw_ref[0] += _tn(pooled, dp_main)
        ds_ref[...] += _csum(dy * _nn(pooled, wm))

    return _pc(body, name=name, grid=(POOL_G, Bl, ns),
               in_specs=[pl.BlockSpec((1, PHALO, POOL_D), lambda g, b, i: (b, jnp.maximum(i * (ts // PHALO) - 1, 0), c0 + g)),
                         pl.BlockSpec((1, ts, POOL_D), lambda g, b, i: (b, i, c0 + g)),
                         pl.BlockSpec((1, ts, POOL_D), lambda g, b, i: (b, i, d0 + g)),
                         pl.BlockSpec((1, PHALO, POOL_D), lambda g, b, i: (b, jnp.minimum((i + 1) * (ts // PHALO), nhb - 1), d0 + g)),
                         pl.BlockSpec((1, POOL_D, POOL_D), lambda g, b, i: (g, 0, 0)),
                         pl.BlockSpec((1, POOL_D), lambda g, b, i: (0, g))],
               out_specs=[pl.BlockSpec((1, ts, POOL_D), lambda g, b, i: (b, i, g)),
                          pl.BlockSpec((1, POOL_D, POOL_D), lambda g, b, i: (g, 0, 0)),
                          pl.BlockSpec((1, POOL_D), lambda g, b, i: (0, g))],
               out_shape=[SDS((Bl, S, POOL_W), F32), SDS((POOL_G, POOL_D, POOL_D), F32), SDS((1, POOL_W), F32)],
               scratch=[pltpu.VMEM((PHALO + ts, POOL_D), F32), pltpu.VMEM((PHALO + ts, POOL_D), F32)],
               )(proj3, proj3, dycat3, dycat3, pool_w, scale)


NPAIR = SSD_HEADS // 2


def _ssd_common(sm, bias, alog):
    L = SSD_L
    dt = jax.nn.softplus(sm + bias)
    a = -jnp.exp(alog)
    da = dt * a
    r = lax.broadcasted_iota(jnp.int32, (L, L), 0)
    c = lax.broadcasted_iota(jnp.int32, (L, L), 1)
    tri = (r >= c).astype(F32)
    cum = _dg(tri, da, 1, 0, lax.Precision.HIGHEST)
    return dt, a, cum, cum.T, r >= c


def _lanes(lo, hi, shape=(1, LANES)):
    lane = lax.broadcasted_iota(jnp.int32, shape, len(shape) - 1)
    return (lane >= lo) & (lane < hi)


def _onehot_lane(h):
    return (lax.broadcasted_iota(jnp.int32, (1, LANES), 1) == h).astype(F32)


def ssd_fwd(xbc3, proj3, bias, alog, dskip, *, name):
    Bl, S, _ = xbc3.shape
    L = SSD_L
    nc = S // L

    def body(xbc_ref, sm_ref, bias_ref, alog_ref, d_ref, y_ref, hin_ref, H):
        c = pl.program_id(1)

        @pl.when(c == 0)
        def _():
            H[...] = jnp.zeros_like(H)

        dt, a, cum, cumT, mask = _ssd_common(sm_ref[0], bias_ref[...], alog_ref[...])
        lo = _lanes(0, SSD_P)
        rowlo = lax.broadcasted_iota(jnp.int32, (LANES, LANES), 0) < SSD_P
        cb = []
        for g in range(SSD_G):
            Bg = xbc_ref[0, :, SSD_W + g * SSD_N:SSD_W + (g + 1) * SSD_N]
            Cg = xbc_ref[0, :, SSD_W + SSD_G * SSD_N + g * SSD_N:SSD_W + SSD_G * SSD_N + (g + 1) * SSD_N]
            cb.append((Bg, Cg, _nt(Cg, Bg)))
        for j in range(NPAIR):
            h0, h1 = 2 * j, 2 * j + 1
            Bg, Cg, CB = cb[j // (NPAIR // SSD_G)]
            X = xbc_ref[0, :, j * LANES:(j + 1) * LANES]
            c0, c1 = cum[:, h0:h0 + 1], cum[:, h1:h1 + 1]
            r0, r1 = cumT[h0:h0 + 1, :], cumT[h1:h1 + 1, :]
            cl0, cl1 = cum[L - 1:L, h0:h0 + 1], cum[L - 1:L, h1:h1 + 1]
            Xt = X * jnp.where(lo, dt[:, h0:h0 + 1], dt[:, h1:h1 + 1])
            M0 = CB * jnp.exp(jnp.where(mask, c0 - r0, NEG))
            M1 = CB * jnp.exp(jnp.where(mask, c1 - r1, NEG))
            Yd = jnp.where(lo, _nn(M0, Xt), _nn(M1, Xt))
            Hp = H[j]
            hin_ref[0, 0, j] = Hp
            Z = _nt(Cg, Hp)
            y_ref[0, :, j * LANES:(j + 1) * LANES] = Yd + jnp.where(lo, jnp.exp(c0), jnp.exp(c1)) * Z + X * d_ref[j:j + 1, :]
            wl = jnp.where(lo, jnp.exp(cl0 - c0), jnp.exp(cl1 - c1))
            H[j] = jnp.where(rowlo, jnp.exp(cl0), jnp.exp(cl1)) * Hp + _tn(wl * Xt, Bg)

    vec = pl.BlockSpec((1, LANES), lambda b, c: (0, 0))
    return _pc(body, name=name, grid=(Bl, nc),
               in_specs=[pl.BlockSpec((1, L, SSD_XBC), lambda b, c: (b, c, 0)),
                         pl.BlockSpec((1, L, LANES), lambda b, c: (b, c, DT0 // LANES)),
                         vec, vec, pl.BlockSpec((NPAIR, LANES), lambda b, c: (0, 0))],
               out_specs=[pl.BlockSpec((1, L, SSD_W), lambda b, c: (b, c, 0)),
                          pl.BlockSpec((1, 1, NPAIR, LANES, LANES), lambda b, c: (b, c, 0, 0, 0))],
               out_shape=[SDS((Bl, S, SSD_W), F32), SDS((Bl, nc, NPAIR, LANES, LANES), F32)],
               scratch=[pltpu.VMEM((NPAIR, LANES, LANES), F32)])(xbc3, proj3, bias, alog, dskip)


def ssd_bwd(xbc3, proj3, hin, dy3, bias, alog, dskip, *, name):
    Bl, S, _ = xbc3.shape
    L = SSD_L
    nc = S // L

    def body(xbc_ref, sm_ref, hin_ref, dy_ref, bias_ref, alog_ref, d_ref, dx_ref, ddt_ref, dpar_ref, dd_ref, dH, ddacc):
        bb, i = pl.program_id(0), pl.program_id(1)

        @pl.when(i == 0)
        def _():
            dH[...] = jnp.zeros_like(dH)

        @pl.when((bb == 0) & (i == 0))
        def _():
            dpar_ref[...] = jnp.zeros_like(dpar_ref)
            ddacc[...] = jnp.zeros_like(ddacc)

        sm = sm_ref[0]
        dt, a, cum, cumT, mask = _ssd_common(sm, bias_ref[...], alog_ref[...])
        maskf = mask.astype(F32)
        lo = _lanes(0, SSD_P)
        rowlo = lax.broadcasted_iota(jnp.int32, (LANES, LANES), 0) < SSD_P
        lastrow = (lax.broadcasted_iota(jnp.int32, (L, 1), 0) == L - 1).astype(F32)
        dcum = jnp.zeros((L, LANES), F32)
        ddt = jnp.zeros((L, LANES), F32)
        grp = []
        for g in range(SSD_G):
            Bg = xbc_ref[0, :, SSD_W + g * SSD_N:SSD_W + (g + 1) * SSD_N]
            Cg = xbc_ref[0, :, SSD_W + SSD_G * SSD_N + g * SSD_N:SSD_W + SSD_G * SSD_N + (g + 1) * SSD_N]
            grp.append(dict(B=Bg, C=Cg, CB=_nt(Cg, Bg), dB=jnp.zeros((L, SSD_N), F32), dC=jnp.zeros((L, SSD_N), F32),
                            dCB=jnp.zeros((L, L), F32)))
        for j in range(NPAIR):
            h0, h1 = 2 * j, 2 * j + 1
            G = grp[j // (NPAIR // SSD_G)]
            Bg, Cg, CB = G["B"], G["C"], G["CB"]
            X = xbc_ref[0, :, j * LANES:(j + 1) * LANES]
            dY = dy_ref[0, :, j * LANES:(j + 1) * LANES]
            c0, c1 = cum[:, h0:h0 + 1], cum[:, h1:h1 + 1]
            r0, r1 = cumT[h0:h0 + 1, :], cumT[h1:h1 + 1, :]
            cl0, cl1 = cum[L - 1:L, h0:h0 + 1], cum[L - 1:L, h1:h1 + 1]
            oh0, oh1 = _onehot_lane(h0), _onehot_lane(h1)
            dtl = jnp.where(lo, dt[:, h0:h0 + 1], dt[:, h1:h1 + 1])
            Xt = X * dtl
            e0, e1 = jnp.exp(c0), jnp.exp(c1)
            el = jnp.where(lo, e0, e1)
            w0, w1 = jnp.exp(cl0 - c0), jnp.exp(cl1 - c1)
            wl = jnp.where(lo, w0, w1)
            Hp = hin_ref[0, 0, j]
            dS = dH[j]
            dX = dY * d_ref[j:j + 1, :]
            ddacc[j:j + 1, :] += _csum(dY * X)
            Z = _nt(Cg, Hp)
            dZ = dY * el
            t = dY * Z
            dcum = dcum + (_rsum(jnp.where(lo, t, 0.0)) * e0) * oh0 + (_rsum(jnp.where(lo, 0.0, t)) * e1) * oh1
            G["dC"] = G["dC"] + _nn(dZ, Hp)
            dHy = _tn(dZ, Cg)
            Gm = _nt(Bg, dS)
            dXt = wl * Gm
            u = Xt * Gm
            q0 = _rsum(jnp.where(lo, u, 0.0)) * w0
            q1 = _rsum(jnp.where(lo, 0.0, u)) * w1
            dcum = dcum + (lastrow * _csum(q0) - q0) * oh0 + (lastrow * _csum(q1) - q1) * oh1
            G["dB"] = G["dB"] + _nn(wl * Xt, dS)
            g0, g1 = jnp.exp(cl0), jnp.exp(cl1)
            prod = dS * Hp
            dg0 = _csum(_rsum(jnp.where(rowlo, prod, 0.0)))
            dg1 = _csum(_rsum(jnp.where(rowlo, 0.0, prod)))
            dcum = dcum + lastrow * ((dg0 * g0) * oh0 + (dg1 * g1) * oh1)
            dH[j] = jnp.where(rowlo, g0, g1) * dS + dHy
            for ch, rh, mh, oh in ((c0, r0, lo, oh0), (c1, r1, jnp.logical_not(lo), oh1)):
                decay = jnp.exp(jnp.where(mask, ch - rh, NEG))
                Mh = CB * decay
                dM = _nt(jnp.where(mh, dY, 0.0), Xt) * maskf
                dXt = dXt + jnp.where(mh, _tn(Mh, dY), 0.0)
                G["dCB"] = G["dCB"] + dM * decay
                Q = dM * Mh
                dcum = dcum + (_rsum(Q) - _rsum(Q.T)) * oh
            dX = dX + dXt * dtl
            s = dXt * X
            ddt = ddt + _rsum(jnp.where(lo, s, 0.0)) * oh0 + _rsum(jnp.where(lo, 0.0, s)) * oh1
            dx_ref[0, :, j * LANES:(j + 1) * LANES] = dX
        for g in range(SSD_G):
            G = grp[g]
            dC = G["dC"] + _nn(G["dCB"], G["B"])
            dB = G["dB"] + _tn(G["dCB"], G["C"])
            dx_ref[0, :, SSD_W + g * SSD_N:SSD_W + (g + 1) * SSD_N] = dB
            dx_ref[0, :, SSD_W + SSD_G * SSD_N + g * SSD_N:SSD_W + SSD_G * SSD_N + (g + 1) * SSD_N] = dC
        r = lax.broadcasted_iota(jnp.int32, (L, L), 0)
        c = lax.broadcasted_iota(jnp.int32, (L, L), 1)
        dda = _dg((c >= r).astype(F32), dcum, 1, 0, lax.Precision.HIGHEST)
        heads = _lanes(0, SSD_HEADS)
        ddt = ddt + dda * a
        draw = jnp.where(heads, ddt * _sig(sm + bias_ref[...]), 0.0)
        ddt_ref[0] = draw
        dpar_ref[0:1, :] += _csum(draw)
        dpar_ref[1:2, :] += _csum(jnp.where(heads, dda * dt * a, 0.0))

        @pl.when((bb == Bl - 1) & (i == nc - 1))
        def _():
            acc = ddacc[...]
            lane = lax.broadcasted_iota(jnp.int32, (NPAIR, LANES), 1)
            s0 = _rsum(jnp.where(lane < SSD_P, acc, 0.0))
            s1 = _rsum(jnp.where(lane < SSD_P, 0.0, acc))
            dd_ref[...] = jnp.where(lane == 0, s0, jnp.where(lane == 1, s1, 0.0))

    vec = pl.BlockSpec((1, LANES), lambda b, i: (0, 0))
    par = pl.BlockSpec((NPAIR, LANES), lambda b, i: (0, 0))
    return _pc(body, name=name, grid=(Bl, nc),
               in_specs=[pl.BlockSpec((1, L, SSD_XBC), lambda b, i: (b, nc - 1 - i, 0)),
                         pl.BlockSpec((1, L, LANES), lambda b, i: (b, nc - 1 - i, DT0 // LANES)),
                         pl.BlockSpec((1, 1, NPAIR, LANES, LANES), lambda b, i: (b, nc - 1 - i, 0, 0, 0)),
                         pl.BlockSpec((1, L, SSD_W), lambda b, i: (b, nc - 1 - i, 0)),
                         vec, vec, par],
               out_specs=[pl.BlockSpec((1, L, SSD_XBC), lambda b, i: (b, nc - 1 - i, 0)),
                          pl.BlockSpec((1, L, LANES), lambda b, i: (b, nc - 1 - i, 0)),
                          par, par],
               out_shape=[SDS((Bl, S, SSD_XBC), F32), SDS((Bl, S, LANES), F32), SDS((NPAIR, LANES), F32),
                          SDS((NPAIR, LANES), F32)],
               scratch=[pltpu.VMEM((NPAIR, LANES, LANES), F32), pltpu.VMEM((NPAIR, LANES), F32)],
               )(xbc3, proj3, hin, dy3, bias, alog, dskip)


PE_LO, PE_MID, PE_HI = MLA_NOPE, MLA_NOPE + MLA_ROPE // 2, MLA_NOPE + MLA_ROPE
ATT_SCALE = 1.0 / math.sqrt(MLA_QK)


def _swap(x):
    first, second = _lanes(PE_LO, PE_MID), _lanes(PE_MID, PE_HI)
    half = MLA_ROPE // 2
    return jnp.where(first, -pltpu.roll(x, LANES - half, 1), jnp.where(second, pltpu.roll(x, half, 1), 0.0))


def _rope_tables(pos, invf):
    ang = pos * invf
    pe = _lanes(PE_LO, PE_HI)
    return jnp.where(pe, jnp.cos(ang), 1.0), jnp.where(pe, jnp.sin(ang), 0.0)


def mla_prep_fwd(qt, kvt, proj, pos, invf, *, name):
    T = qt.shape[0]
    tm = _tile(T, 512, 8)

    def body(q_ref, k_ref, kpe_ref, pos_ref, f_ref, qo_ref, ko_ref):
        cs, sn = _rope_tables(pos_ref[...], f_ref[...])
        q = q_ref[...]
        k = k_ref[...] + kpe_ref[...]
        qo_ref[...] = ((q * cs + _swap(q) * sn) * ATT_SCALE).astype(BF16)
        ko_ref[...] = (k * cs + _swap(k) * sn).astype(BF16)

    tile = pl.BlockSpec((tm, LANES), lambda i, h: (i, h))
    return _pc(body, name=name, grid=(T // tm, MLA_H),
               in_specs=[tile, tile, pl.BlockSpec((tm, LANES), lambda i, h: (i, KPE0 // LANES)),
                         pl.BlockSpec((tm, 1), lambda i, h: (i, 0)), pl.BlockSpec((1, LANES), lambda i, h: (0, 0))],
               out_specs=[tile, tile], out_shape=[SDS((T, MLA_H * LANES), BF16)] * 2)(qt, kvt, proj, pos, invf)


def mla_prep_bwd(dqr, dkr, pos, invf, *, name):
    T = dqr.shape[0]
    tm = _tile(T, 512, 8)

    def body(dq_ref, dk_ref, pos_ref, f_ref, qo_ref, ko_ref, kpe_ref):
        cs, sn = _rope_tables(pos_ref[...], f_ref[...])
        dq = dq_ref[...] * ATT_SCALE
        dk = dk_ref[...]
        qo_ref[...] = dq * cs - _swap(dq * sn)
        dkk = dk * cs - _swap(dk * sn)
        pe = _lanes(PE_LO, PE_HI)
        ko_ref[...] = jnp.where(pe, 0.0, dkk)

        @pl.when(pl.program_id(1) == 0)
        def _():
            kpe_ref[...] = jnp.zeros_like(kpe_ref)

        kpe_ref[...] += jnp.where(pe, dkk, 0.0)

    tile = pl.BlockSpec((tm, LANES), lambda i, h: (i, h))
    return _pc(body, name=name, grid=(T // tm, MLA_H),
               in_specs=[tile, tile, pl.BlockSpec((tm, 1), lambda i, h: (i, 0)), pl.BlockSpec((1, LANES), lambda i, h: (0, 0))],
               out_specs=[tile, tile, pl.BlockSpec((tm, LANES), lambda i, h: (i, 0))],
               out_shape=[SDS((T, MLA_H * LANES), F32), SDS((T, MLA_H * LANES), F32), SDS((T, LANES), F32)],
               )(dqr, dkr, pos, invf)


def _att_tile(S):
    return _tile(S, 512, LANES)


def _causal(s, qi, kj, t):
    r = qi * t + lax.broadcasted_iota(jnp.int32, (t, t), 0)
    c = kj * t + lax.broadcasted_iota(jnp.int32, (t, t), 1)
    return jnp.where(c <= r, s, NEG)


def flash_fwd(qr, kr, kvt, Bl, *, name):
    T = qr.shape[0]
    S = T // Bl
    t = _att_tile(S)
    n = S // t

    def body(q_ref, k_ref, v_ref, o_ref, lse_ref, m, l, acc):
        qi, kj = pl.program_id(2), pl.program_id(3)

        @pl.when(kj == 0)
        def _():
            m[...] = jnp.full_like(m, NEG)
            l[...] = jnp.zeros_like(l)
            acc[...] = jnp.zeros_like(acc)

        @pl.when(kj <= qi)
        def _():
            s = _causal(_nt(q_ref[...], k_ref[...]), qi, kj, t)
            mn = jnp.maximum(m[...], jnp.max(s, axis=1, keepdims=True))
            p = jnp.exp(s - mn)
            al = jnp.exp(m[...] - mn)
            l[...] = al * l[...] + _rsum(p)
            acc[...] = al * acc[...] + _nn(p, v_ref[...])
            m[...] = mn

        @pl.when(kj == n - 1)
        def _():
            o_ref[...] = (acc[...] / l[...]).astype(BF16)
            lse_ref[...] = jnp.broadcast_to(m[...] + jnp.log(l[...]), (t, LANES))

    qs = pl.BlockSpec((t, LANES), lambda b, h, qi, kj: (b * n + qi, h))
    ks = pl.BlockSpec((t, LANES), lambda b, h, qi, kj: (b * n + jnp.minimum(kj, qi), h))
    vs = pl.BlockSpec((t, LANES), lambda b, h, qi, kj: (b * n + jnp.minimum(kj, qi), MLA_H + h))
    return _pc(body, name=name, grid=(Bl, MLA_H, n, n), in_specs=[qs, ks, vs], out_specs=[qs, qs],
               out_shape=[SDS((T, MLA_H * LANES), BF16), SDS((T, MLA_H * LANES), F32)],
               scratch=[pltpu.VMEM((t, 1), F32), pltpu.VMEM((t, 1), F32), pltpu.VMEM((t, LANES), F32)])(qr, kr, kvt)


def flash_bwd_dq(qr, kr, kvt, o, lse, dycat, Bl, *, name):
    T = qr.shape[0]
    S = T // Bl
    t = _att_tile(S)
    n = S // t
    do0 = (SSD_W + POOL_W) // LANES

    def body(q_ref, k_ref, v_ref, o_ref, lse_ref, do_ref, dq_ref, acc):
        qi, kj = pl.program_id(2), pl.program_id(3)

        @pl.when(kj == 0)
        def _():
            acc[...] = jnp.zeros_like(acc)

        @pl.when(kj <= qi)
        def _():
            do = do_ref[...]
            k = k_ref[...]
            s = _causal(_nt(q_ref[...], k), qi, kj, t)
            p = jnp.exp(s - lse_ref[:, 0:1])
            dp = _nt(do, v_ref[...])
            dl = _rsum(do * o_ref[...].astype(F32))
            acc[...] += _nn(p * (dp - dl), k)

        @pl.when(kj == n - 1)
        def _():
            dq_ref[...] = acc[...]

    qs = pl.BlockSpec((t, LANES), lambda b, h, qi, kj: (b * n + qi, h))
    ks = pl.BlockSpec((t, LANES), lambda b, h, qi, kj: (b * n + jnp.minimum(kj, qi), h))
    vs = pl.BlockSpec((t, LANES), lambda b, h, qi, kj: (b * n + jnp.minimum(kj, qi), MLA_H + h))
    ds = pl.BlockSpec((t, LANES), lambda b, h, qi, kj: (b * n + qi, do0 + h))
    return _pc(body, name=name, grid=(Bl, MLA_H, n, n), in_specs=[qs, ks, vs, qs, qs, ds], out_specs=qs,
               out_shape=SDS((T, MLA_H * LANES), F32), scratch=[pltpu.VMEM((t, LANES), F32)])(qr, kr, kvt, o, lse, dycat)


def flash_bwd_dkv(qr, kr, kvt, o, lse, dycat, Bl, *, name):
    T = qr.shape[0]
    S = T // Bl
    t = _att_tile(S)
    n = S // t
    do0 = (SSD_W + POOL_W) // LANES

    def body(q_ref, k_ref, v_ref, o_ref, lse_ref, do_ref, dk_ref, dv_ref, dka, dva):
        kj, qi = pl.program_id(2), pl.program_id(3)

        @pl.when(qi == 0)
        def _():
            dka[...] = jnp.zeros_like(dka)
            dva[...] = jnp.zeros_like(dva)

        @pl.when(qi >= kj)
        def _():
            do = do_ref[...]
            q = q_ref[...]
            s = _causal(_nt(q, k_ref[...]), qi, kj, t)
            p = jnp.exp(s - lse_ref[:, 0:1])
            dp = _nt(do, v_ref[...])
            dl = _rsum(do * o_ref[...].astype(F32))
            dva[...] += _tn(p, do)
            dka[...] += _tn(p * (dp - dl), q)

        @pl.when(qi == n - 1)
        def _():
            dk_ref[...] = dka[...]
            dv_ref[...] = dva[...]

    qs = pl.BlockSpec((t, LANES), lambda b, h, kj, qi: (b * n + jnp.maximum(qi, kj), h))
    ks = pl.BlockSpec((t, LANES), lambda b, h, kj, qi: (b * n + kj, h))
    vs = pl.BlockSpec((t, LANES), lambda b, h, kj, qi: (b * n + kj, MLA_H + h))
    ds = pl.BlockSpec((t, LANES), lambda b, h, kj, qi: (b * n + jnp.maximum(qi, kj), do0 + h))
    return _pc(body, name=name, grid=(Bl, MLA_H, n, n), in_specs=[qs, ks, vs, qs, qs, ds], out_specs=[ks, ks],
               out_shape=[SDS((T, MLA_H * LANES), F32)] * 2,
               scratch=[pltpu.VMEM((t, LANES), F32), pltpu.VMEM((t, LANES), F32)])(qr, kr, kvt, o, lse, dycat)


def _rows2d(a):
    return a.reshape(-1, a.shape[-1])


def add2(a, b, *, name):
    shp = a.shape
    a2, b2 = _rows2d(a), _rows2d(b)
    R, C = a2.shape
    tm = _tile(R, 512, 8)

    def body(a_ref, b_ref, o_ref):
        o_ref[...] = a_ref[...] + b_ref[...]

    blk = pl.BlockSpec((tm, C), lambda i: (i, 0))
    return _pc(body, name=name, grid=(R // tm,), in_specs=[blk, blk], out_specs=blk, out_shape=SDS((R, C), F32))(a2, b2).reshape(shp)


def adamw(w, m, v, parts, *, name):
    shp = w.shape
    w2, m2, v2 = _rows2d(w), _rows2d(m), _rows2d(v)
    R, C = w2.shape
    P = parts.shape[0]
    p3 = parts.reshape(P, R, C)
    tm = _tile(R, 256, 8)
    bc1 = 1.0 - ADAM_B1 ** ADAM_STEP
    bc2 = 1.0 - ADAM_B2 ** ADAM_STEP

    def body(w_ref, m_ref, v_ref, p_ref, g_ref, d_ref, nm_ref, nv_ref):
        g = p_ref[0]
        for k in range(1, P):
            g = g + p_ref[k]
        mm_ = ADAM_B1 * m_ref[...] + (1.0 - ADAM_B1) * g
        vv = ADAM_B2 * v_ref[...] + (1.0 - ADAM_B2) * (g * g)
        g_ref[...] = g
        nm_ref[...] = mm_
        nv_ref[...] = vv
        d_ref[...] = -ADAM_LR * ((mm_ / bc1) / (jnp.sqrt(vv / bc2) + ADAM_EPS) + ADAM_WD * w_ref[...])

    blk = pl.BlockSpec((tm, C), lambda i: (i, 0))
    outs = _pc(body, name=name, grid=(R // tm,), in_specs=[blk, blk, blk, pl.BlockSpec((P, tm, C), lambda i: (0, i, 0))],
               out_specs=[blk] * 4, out_shape=[SDS((R, C), F32)] * 4)(w2, m2, v2, p3)
    return [o.reshape(shp) for o in outs]


ANY = pl.BlockSpec(memory_space=pl.ANY)


def _place():
    return lax.axis_index("x"), lax.axis_index("y"), lax.axis_index("c")


def all_gather_many(xs, *, name):
    n = len(xs)

    def body(*refs):
        x_refs, o_refs = refs[:n], refs[n:2 * n]
        send_sems, recv_sems, local_sems = refs[2 * n:]
        x, y, c = _place()
        me, sibling = (x, y, c), (x, y, 1 - c)
        chips = [(1 - x, y), (x, 1 - y), (1 - x, 1 - y)]

        def rows(a, p):
            return o_refs[a].at[4 * p[0] + 2 * p[1] + p[2]]

        def copy(a, k, block, to, src=None):
            return pltpu.make_async_remote_copy(
                src_ref=rows(a, block) if src is None else src, dst_ref=rows(a, block),
                send_sem=send_sems.at[7 * a + k], recv_sem=recv_sems.at[7 * a + k], device_id=to, device_id_type=MESH)

        mine = [pltpu.make_async_copy(x_refs[a], rows(a, me), local_sems.at[a]) for a in range(n)]
        for cp in mine:
            cp.start()
        first = []
        for a in range(n):
            first.append(copy(a, 0, me, sibling, src=x_refs[a]))
            first += [copy(a, 1 + j, me, (*chip, c), src=x_refs[a]) for j, chip in enumerate(chips)]
        for cp in first:
            cp.start()
        passed = []
        for j, chip in enumerate(chips):
            for a in range(n):
                copy(a, 1 + j, (*chip, c), me).wait_recv()
                cp = copy(a, 4 + j, (*chip, c), sibling)
                cp.start()
                passed.append(cp)
        for a in range(n):
            copy(a, 0, sibling, me).wait_recv()
            for j, chip in enumerate(chips):
                copy(a, 4 + j, (*chip, 1 - c), me).wait_recv()
        for cp in first + passed:
            cp.wait_send()
        for cp in mine:
            cp.wait()

    return pl.pallas_call(
        body, name=name, in_specs=[ANY] * n, out_specs=[ANY] * n,
        out_shape=[SDS((N_DEV,) + a.shape, a.dtype) for a in xs],
        scratch_shapes=[pltpu.SemaphoreType.DMA((7 * n,)), pltpu.SemaphoreType.DMA((7 * n,)), pltpu.SemaphoreType.DMA((n,))],
    )(*xs)


def exchange_sibling(gs, *, name):
    n = len(gs)

    def body(*refs):
        g_refs, o_refs = refs[:n], refs[n:2 * n]
        send_sems, recv_sems = refs[2 * n:]
        x, y, c = _place()
        cps = []
        for a in range(n):
            for k in range(4):
                cps.append(pltpu.make_async_remote_copy(
                    src_ref=g_refs[a].at[2 * k + (1 - c)], dst_ref=o_refs[a].at[k],
                    send_sem=send_sems.at[4 * a + k], recv_sem=recv_sems.at[4 * a + k], device_id=(x, y, 1 - c), device_id_type=MESH))
        for cp in cps:
            cp.start()
        for cp in cps:
            cp.wait()

    return pl.pallas_call(
        body, name=name, in_specs=[ANY] * n, out_specs=[ANY] * n,
        out_shape=[SDS((4,) + g.shape[1:], g.dtype) for g in gs],
        scratch_shapes=[pltpu.SemaphoreType.DMA((4 * n,)), pltpu.SemaphoreType.DMA((4 * n,))],
    )(*gs)


def exchange_chips(As, *, name):
    n = len(As)

    def body(*refs):
        a_refs, o_refs = refs[:n], refs[n:2 * n]
        send_sems, recv_sems = refs[2 * n:]
        x, y, c = _place()
        chips = [(1 - x, y), (x, 1 - y), (1 - x, 1 - y)]
        cps = []
        for a in range(n):
            for j, chip in enumerate(chips):
                cps.append(pltpu.make_async_remote_copy(
                    src_ref=a_refs[a].at[2 * chip[0] + chip[1]], dst_ref=o_refs[a].at[j],
                    send_sem=send_sems.at[3 * a + j], recv_sem=recv_sems.at[3 * a + j], device_id=(*chip, c), device_id_type=MESH))
        for cp in cps:
            cp.start()
        for cp in cps:
            cp.wait()

    return pl.pallas_call(
        body, name=name, in_specs=[ANY] * n, out_specs=[ANY] * n,
        out_shape=[SDS((3,) + a.shape[1:], a.dtype) for a in As],
        scratch_shapes=[pltpu.SemaphoreType.DMA((3 * n,)), pltpu.SemaphoreType.DMA((3 * n,))],
    )(*As)


def _owner_major(full, axis):
    shp = full.shape
    r = full.reshape(shp[:axis] + (N_DEV, shp[axis] // N_DEV) + shp[axis + 1:])
    return jnp.moveaxis(r, axis, 0)


def _from_owner_major(g8, axis):
    r = jnp.moveaxis(g8, 0, axis)
    shp = r.shape
    return r.reshape(shp[:axis] + (shp[axis] * shp[axis + 1],) + shp[axis + 2:])


def _perm_w_in(w):
    z = jnp.zeros((w.shape[0], LANES), w.dtype)
    dt = jnp.pad(w[:, 2560:2576], ((0, 0), (0, LANES - SSD_HEADS)))
    kpe = jnp.pad(w[:, 3728:3760], ((0, 0), (PE_LO, LANES - PE_HI)))
    return jnp.concatenate([w[:, 0:1024], w[:, 1024:2560], w[:, 2576:3088], w[:, 3088:3472], z, w[:, 3472:3728], dt, kpe], axis=1)


def _unperm_w_in(g):
    return jnp.concatenate([g[:, Z0:Z0 + 1024], g[:, XBC0:XBC0 + 1536], g[:, DT0:DT0 + SSD_HEADS], g[:, U0:U0 + 512],
                            g[:, CQ0:CQ0 + 384], g[:, CKV0:CKV0 + 256], g[:, KPE0 + PE_LO:KPE0 + PE_HI]], axis=1)


def _perm_w_uq(w):
    return jnp.pad(w.reshape(MLA_QR, MLA_H, MLA_QK), ((0, 0), (0, 0), (0, LANES - MLA_QK))).reshape(MLA_QR, MLA_H * LANES)


def _unperm_w_uq(g):
    return g.reshape(MLA_QR, MLA_H, LANES)[:, :, :MLA_QK].reshape(MLA_QR, MLA_H * MLA_QK)


def _perm_w_ukv(w):
    w3 = w.reshape(MLA_KVR, MLA_H, MLA_NOPE + MLA_V)
    pad = ((0, 0), (0, 0), (0, LANES - MLA_NOPE))
    k = jnp.pad(w3[:, :, :MLA_NOPE], pad).reshape(MLA_KVR, MLA_H * LANES)
    v = jnp.pad(w3[:, :, MLA_NOPE:], pad).reshape(MLA_KVR, MLA_H * LANES)
    return jnp.concatenate([k, v], axis=1)


def _unperm_w_ukv(g):
    k = g[:, :MLA_H * LANES].reshape(MLA_KVR, MLA_H, LANES)[:, :, :MLA_NOPE]
    v = g[:, MLA_H * LANES:].reshape(MLA_KVR, MLA_H, LANES)[:, :, :MLA_V]
    return jnp.concatenate([k, v], axis=2).reshape(MLA_KVR, MLA_H * (MLA_NOPE + MLA_V))


def _perm_w_out(w):
    m = jnp.pad(w[SSD_W + POOL_W:].reshape(MLA_H, MLA_V, D), ((0, 0), (0, LANES - MLA_V), (0, 0))).reshape(MLA_H * LANES, D)
    return jnp.concatenate([w[:SSD_W + POOL_W], m], axis=0)


def _unperm_w_out(g):
    m = g[SSD_W + POOL_W:].reshape(MLA_H, LANES, D)[:, :MLA_V].reshape(MLA_H * MLA_V, D)
    return jnp.concatenate([g[:SSD_W + POOL_W], m], axis=0)


def _lane_pad(v):
    return jnp.pad(v.reshape(1, -1), ((0, 0), (0, LANES - v.shape[-1])))


SMALL = ("attn_norm", "ssd_conv_b", "ssd_dt_bias", "ssd_a_log", "ssd_d", "ssd_norm", "pool_w", "pool_scale",
         "mla_q_norm", "mla_kv_norm", "ffn_norm", "ffn_conv_b", "final_norm")
SHARDED = {"w_in": 2, "ssd_conv_w": 2, "mla_w_uq": 2, "mla_w_ukv": 2, "w_out": 1, "ffn_w_up": 2, "ffn_conv_w": 2,
           "ffn_w_down": 1}
ALL_W = ("attn_norm", "w_in", "ssd_conv_w", "ssd_conv_b", "ssd_dt_bias", "ssd_a_log", "ssd_d", "ssd_norm", "pool_w",
         "pool_scale", "mla_q_norm", "mla_w_uq", "mla_kv_norm", "mla_w_ukv", "w_out", "ffn_norm", "ffn_w_up",
         "ffn_conv_w", "ffn_conv_b", "ffn_w_down", "final_norm")


def _pack_small(d):
    rows, layout = [], []
    for k in SMALL:
        a = d[k].reshape(-1)
        n = a.shape[0]
        r = -(-n // LANES)
        rows.append(jnp.pad(a, (0, r * LANES - n)).reshape(r, LANES))
        layout.append((k, n, r, d[k].shape))
    buf = jnp.concatenate(rows, axis=0)
    pad = (-buf.shape[0]) % 8
    return jnp.pad(buf, ((0, pad), (0, 0))), layout


def _unpack_small(buf, layout):
    out, r0 = {}, 0
    for k, n, r, shp in layout:
        out[k] = buf[r0:r0 + r].reshape(-1)[:n].reshape(shp)
        r0 += r
    return out


def _layer_fwd(l, x, W, pos, invf, Bl):
    T = x.shape[0]
    S = T // Bl
    n = f"l{l}_"
    h = rms_fwd(x, W["attn_norm"], name=n + "attn_norm")
    proj = mm(h, W["w_in"], name=n + "w_in")
    proj3 = proj.reshape(Bl, S, PW)
    xbc3 = conv_silu_fwd(proj3, W["ssd_conv_w"], W["ssd_conv_b"], name=n + "ssd_conv")
    y3, hin = ssd_fwd(xbc3, proj3, W["ssd_dt_bias"], W["ssd_a_log"], W["ssd_d"], name=n + "ssd_scan")
    y = y3.reshape(T, SSD_W)
    y_ssd = gated_rms_fwd(y, proj, W["ssd_norm"], name=n + "ssd_gate_norm")
    y_pool = pool_fwd(proj3, W["pool_w"], W["pool_scale"], name=n + "pool").reshape(T, POOL_W)
    qn = rms_fwd(proj, W["mla_q_norm"], col0=CQ0, width=MLA_QR, name=n + "q_norm")
    kvn = rms_fwd(proj, W["mla_kv_norm"], col0=CKV0, width=MLA_KVR, name=n + "kv_norm")
    qt = mm(qn, W["mla_w_uq"], name=n + "w_uq")
    kvt = mm(kvn, W["mla_w_ukv"], name=n + "w_ukv")
    qr, kr = mla_prep_fwd(qt, kvt, proj, pos, invf, name=n + "rope")
    o, lse = flash_fwd(qr, kr, kvt, Bl, name=n + "attn")
    ycat = jnp.concatenate([y_ssd, y_pool, o], axis=1)
    x1 = mm(ycat, W["w_out"], add=x, name=n + "w_out")
    h2 = rms_fwd(x1, W["ffn_norm"], name=n + "ffn_norm")
    pre = mm(h2, W["ffn_w_up"], name=n + "w_up")
    pre3 = pre.reshape(Bl, S, 2 * DFF)
    act = ffn_act_fwd(pre3, W["ffn_conv_w"], W["ffn_conv_b"], name=n + "ffn_act").reshape(T, DFF)
    x2 = mm(act, W["ffn_w_down"], add=x1, name=n + "w_down")
    saved = dict(x=x, h=h, proj=proj, xbc3=xbc3, hin=hin, y=y, qn=qn, kvn=kvn, kvt=kvt, qr=qr, kr=kr, o=o, lse=lse,
                 ycat=ycat, x1=x1, h2=h2, pre3=pre3, act=act)
    return x2, saved


def _layer_bwd(l, dx2, W, sv, pos, invf, Bl):
    T = dx2.shape[0]
    S = T // Bl
    n = f"l{l}_b_"
    g = {}
    g["ffn_w_down"] = mm(sv["act"], dx2, ta=True, name=n + "dw_down")
    dact = mm(dx2, W["ffn_w_down"], tb=True, name=n + "dact")
    dpg, dpv, dwg, dwv, dbg, dbv = ffn_act_bwd(sv["pre3"], W["ffn_conv_w"], W["ffn_conv_b"], dact.reshape(Bl, S, DFF),
                                               name=n + "ffn_act")
    g["ffn_conv_w"] = jnp.concatenate([dwg, dwv], axis=1)
    g["ffn_conv_b"] = jnp.concatenate([dbg, dbv], axis=1)
    dpg, dpv = dpg.reshape(T, DFF), dpv.reshape(T, DFF)
    g["ffn_w_up"] = jnp.concatenate([mm(sv["h2"], dpg, ta=True, name=n + "dw_up_g"),
                                     mm(sv["h2"], dpv, ta=True, name=n + "dw_up_v")], axis=1)
    dh2 = mm(dpg, W["ffn_w_up"][:, :DFF], tb=True, name=n + "dh2_g")
    dh2 = mm(dpv, W["ffn_w_up"][:, DFF:], tb=True, add=dh2, name=n + "dh2_v")
    dx1, g["ffn_norm"] = rms_bwd(sv["x1"], W["ffn_norm"], dh2, add=dx2, name=n + "ffn_norm")
    g["w_out"] = mm(sv["ycat"], dx1, ta=True, name=n + "dw_out")
    dycat = mm(dx1, W["w_out"], tb=True, name=n + "dycat")
    proj, proj3 = sv["proj"], sv["proj"].reshape(Bl, S, PW)
    dy, dz, g["ssd_norm"] = gated_rms_bwd(sv["y"], proj, W["ssd_norm"], dycat, name=n + "ssd_gate_norm")
    dxa, ddt, dpar, dd = ssd_bwd(sv["xbc3"], proj3, sv["hin"], dy.reshape(Bl, S, SSD_W), W["ssd_dt_bias"], W["ssd_a_log"],
                                 W["ssd_d"], name=n + "ssd_scan")
    g["ssd_dt_bias"] = dpar[0, :SSD_HEADS]
    g["ssd_a_log"] = dpar[1, :SSD_HEADS]
    g["ssd_d"] = dd[:, :2].reshape(SSD_HEADS)
    dxbc, g["ssd_conv_w"], g["ssd_conv_b"] = conv_silu_bwd(proj3, W["ssd_conv_w"], W["ssd_conv_b"], dxa, name=n + "ssd_conv")
    du, g["pool_w"], g["pool_scale"] = pool_bwd(proj3, W["pool_w"], W["pool_scale"], dycat.reshape(Bl, S, YCAT), name=n + "pool")
    dqr = flash_bwd_dq(sv["qr"], sv["kr"], sv["kvt"], sv["o"], sv["lse"], dycat, Bl, name=n + "attn_dq")
    dkr, dv = flash_bwd_dkv(sv["qr"], sv["kr"], sv["kvt"], sv["o"], sv["lse"], dycat, Bl, name=n + "attn_dkv")
    dqt, dkt, dkpe = mla_prep_bwd(dqr, dkr, pos, invf, name=n + "rope")
    wk, wv = W["mla_w_ukv"][:, :MLA_H * LANES], W["mla_w_ukv"][:, MLA_H * LANES:]
    g["mla_w_ukv"] = jnp.concatenate([mm(sv["kvn"], dkt, ta=True, name=n + "dw_uk"),
                                      mm(sv["kvn"], dv, ta=True, name=n + "dw_uv")], axis=1)
    dkvn = mm(dkt, wk, tb=True, name=n + "dkvn_k")
    dkvn = mm(dv, wv, tb=True, add=dkvn, name=n + "dkvn_v")
    g["mla_w_uq"] = mm(sv["qn"], dqt, ta=True, name=n + "dw_uq")
    dqn = mm(dqt, W["mla_w_uq"], tb=True, name=n + "dqn")
    dcq, g["mla_q_norm"] = rms_bwd(proj, W["mla_q_norm"], dqn, col0=CQ0, width=MLA_QR, name=n + "q_norm")
    dckv, g["mla_kv_norm"] = rms_bwd(proj, W["mla_kv_norm"], dkvn, col0=CKV0, width=MLA_KVR, name=n + "kv_norm")
    dproj = jnp.concatenate([dz, dxbc.reshape(T, SSD_XBC), du.reshape(T, POOL_W), dcq, jnp.zeros((T, LANES), F32), dckv,
                             ddt.reshape(T, LANES), dkpe], axis=1).astype(BF16)
    g["w_in"] = mm(sv["h"], dproj, ta=True, name=n + "dw_in")
    dh = mm(dproj, W["w_in"], tb=True, name=n + "dh")
    dx, g["attn_norm"] = rms_bwd(sv["x"], W["attn_norm"], dh, add=dx1, name=n + "attn_norm")
    return dx, g


def kernel(x, positions, attn_norm, w_in, ssd_conv_w, ssd_conv_b, ssd_dt_bias, ssd_a_log, ssd_d, ssd_norm, pool_w, pool_scale, mla_q_norm, mla_w_uq, mla_kv_norm, mla_w_ukv, w_out, ffn_norm, ffn_w_up, ffn_conv_w, ffn_conv_b, ffn_w_down, final_norm, loss_target, m_attn_norm, m_w_in, m_ssd_conv_w, m_ssd_conv_b, m_ssd_dt_bias, m_ssd_a_log, m_ssd_d, m_ssd_norm, m_pool_w, m_pool_scale, m_mla_q_norm, m_mla_w_uq, m_mla_kv_norm, m_mla_w_ukv, m_w_out, m_ffn_norm, m_ffn_w_up, m_ffn_conv_w, m_ffn_conv_b, m_ffn_w_down, m_final_norm, v_attn_norm, v_w_in, v_ssd_conv_w, v_ssd_conv_b, v_ssd_dt_bias, v_ssd_a_log, v_ssd_d, v_ssd_norm, v_pool_w, v_pool_scale, v_mla_q_norm, v_mla_w_uq, v_mla_kv_norm, v_mla_w_ukv, v_w_out, v_ffn_norm, v_ffn_w_up, v_ffn_conv_w, v_ffn_conv_b, v_ffn_w_down, v_final_norm):
    a = locals()
    Wt = {k: a[k] for k in ALL_W}
    Mo = {k: a["m_" + k] for k in ALL_W}
    Vo = {k: a["v_" + k] for k in ALL_W}
    Bl, S, _ = x.shape
    T = Bl * S

    names = list(SHARDED)
    conv = ("ssd_conv_w", "ffn_conv_w")
    shards = [Wt[k] if k in conv else Wt[k].astype(BF16) for k in names]
    gathered = all_gather_many(shards, name="gather_weights")
    full = {k: _from_owner_major(g8, SHARDED[k]) for k, g8 in zip(names, gathered)}

    pos = positions.astype(F32).reshape(T, 1)
    inv_freq = ROPE_THETA ** (-jnp.arange(0, MLA_ROPE, 2, dtype=F32) / MLA_ROPE)
    invf = jnp.pad(jnp.concatenate([inv_freq, inv_freq]), (PE_LO, LANES - PE_HI)).reshape(1, LANES)

    layers = []
    for l in range(DEPTH):
        layers.append({
            "attn_norm": attn_norm[l].reshape(1, D), "w_in": _perm_w_in(full["w_in"][l]),
            "ssd_conv_w": full["ssd_conv_w"][l], "ssd_conv_b": ssd_conv_b[l].reshape(1, SSD_XBC),
            "ssd_dt_bias": _lane_pad(ssd_dt_bias[l]), "ssd_a_log": _lane_pad(ssd_a_log[l]),
            "ssd_d": jnp.repeat(ssd_d[l].reshape(NPAIR, 2), SSD_P, axis=1), "ssd_norm": ssd_norm[l].reshape(1, SSD_W),
            "pool_w": pool_w[l].astype(BF16), "pool_scale": pool_scale[l].reshape(1, POOL_W),
            "mla_q_norm": mla_q_norm[l].reshape(1, MLA_QR), "mla_w_uq": _perm_w_uq(full["mla_w_uq"][l]),
            "mla_kv_norm": mla_kv_norm[l].reshape(1, MLA_KVR), "mla_w_ukv": _perm_w_ukv(full["mla_w_ukv"][l]),
            "w_out": _perm_w_out(full["w_out"][l]), "ffn_norm": ffn_norm[l].reshape(1, D),
            "ffn_w_up": full["ffn_w_up"][l], "ffn_conv_w": full["ffn_conv_w"][l],
            "ffn_conv_b": ffn_conv_b[l].reshape(1, 2 * DFF), "ffn_w_down": full["ffn_w_down"][l]})

    xc = x.reshape(T, D)
    saved = []
    for l in range(DEPTH):
        xc, sv = _layer_fwd(l, xc, layers[l], pos, invf, Bl)
        saved.append(sv)
    dx, g_final, loss_part = final_loss(xc, final_norm.reshape(1, D), loss_target.reshape(T, D), name="final_loss")
    grads = [None] * DEPTH
    for l in reversed(range(DEPTH)):
        dx, grads[l] = _layer_bwd(l, dx, layers[l], saved[l], pos, invf, Bl)
    loss = lax.psum(loss_part[0, 0], AXES)

    unperm = {"w_in": _unperm_w_in, "mla_w_uq": _unperm_w_uq, "mla_w_ukv": _unperm_w_ukv, "w_out": _unperm_w_out}
    part = {}
    for k in ALL_W:
        if k == "final_norm":
            part[k] = g_final.reshape(D)
        else:
            shp = list(Wt[k].shape[1:])
            if k in SHARDED:
                shp[SHARDED[k] - 1] *= N_DEV
            part[k] = jnp.stack([unperm.get(k, lambda t: t)(grads[l][k]).reshape(shp) for l in range(DEPTH)])

    xi, yi, ci = _place()
    chip = 2 * xi + yi
    g8 = [_owner_major(part[k], SHARDED[k]) for k in names]
    r1 = exchange_sibling(g8, name="rs_sibling")
    keep = [lax.dynamic_index_in_dim(g.reshape((4, 2) + g.shape[1:]), ci, 1, keepdims=False) for g in g8]
    chip_sum = [add2(kp, r, name="rs_add_" + k) for k, kp, r in zip(names, keep, r1)]
    r2 = exchange_chips(chip_sum, name="rs_chips")
    out_g, out_d, out_m, out_v = {}, {}, {}, {}
    for k, cs, r in zip(names, chip_sum, r2):
        own = lax.dynamic_index_in_dim(cs, chip, 0, keepdims=True)
        parts = jnp.concatenate([own, r], axis=0)
        out_g[k], out_d[k], out_m[k], out_v[k] = adamw(Wt[k], Mo[k], Vo[k], parts, name="adamw_" + k)

    pg, layout = _pack_small(part)
    pw, _ = _pack_small(Wt)
    pm, _ = _pack_small(Mo)
    pv, _ = _pack_small(Vo)
    (pg8,) = all_gather_many([pg], name="gather_small_grads")
    sg, sd, sm, sv_ = adamw(pw, pm, pv, pg8, name="adamw_small")
    for dst, buf in ((out_g, sg), (out_d, sd), (out_m, sm), (out_v, sv_)):
        dst.update(_unpack_small(buf, layout))

    return (loss, dx.reshape(Bl, S, D), *[out_g[k] for k in ALL_W], *[out_d[k] for k in ALL_W],
            *[out_m[k] for k in ALL_W], *[out_v[k] for k in ALL_W])
```

```python
import functools
import math

import jax
import jax.numpy as jnp
from jax import lax
from jax.experimental import pallas as pl
from jax.experimental.pallas import tpu as pltpu

F32, BF16 = jnp.float32, jnp.bfloat16
SDS = jax.ShapeDtypeStruct
MESH = pl.DeviceIdType.MESH
AXES = ("x", "y", "c")
N_DEV = 8

D = 1024
EPS = 1e-6
SSD_HEADS, SSD_P, SSD_W, SSD_G, SSD_N, SSD_K, SSD_L, SSD_XBC = 16, 64, 1024, 2, 128, 4, 128, 1536
POOL_G, POOL_D, POOL_W, POOL_WIN = 4, 128, 512, (2, 4, 8, 16)
MLA_H, MLA_QR, MLA_KVR, MLA_NOPE, MLA_ROPE, MLA_V, MLA_QK = 8, 384, 256, 64, 32, 64, 96
ROPE_THETA = 10000.0
MIX = 2048
DFF, FFN_K = 2816, 3
DEPTH = 2
ADAM_LR, ADAM_B1, ADAM_B2, ADAM_EPS, ADAM_WD, ADAM_STEP = 0.001, 0.9, 0.999, 1e-08, 0.01, 10

Z0, XBC0, U0, CQ0, CKV0, DT0, KPE0, PW = 0, 1024, 2560, 3072, 3584, 3840, 3968, 4096
LANES = 128
YCAT = SSD_W + POOL_W + MLA_H * LANES
NEG = -1e30
VMEM_LIMIT = 56 * 1024 * 1024
MM_ROW_TILE, MM_LANE_TILE, MM_FULL_K = 1024, 1408, 2816


def _tile(n, pref, mult):
    if n <= pref:
        return n
    for d in range(pref, 0, -mult):
        if d % mult == 0 and n % d == 0:
            return d
    return n


def _dg(a, b, ca, cb, prec=None):
    return lax.dot_general(a, b, (((ca,), (cb,)), ((), ())), preferred_element_type=F32, precision=prec)


def _nn(a, b):
    return _dg(a.astype(BF16), b.astype(BF16), 1, 0)


def _nt(a, b):
    return _dg(a.astype(BF16), b.astype(BF16), 1, 1)


def _tn(a, b):
    return _dg(a.astype(BF16), b.astype(BF16), 0, 0)


def _sig(x):
    return jax.nn.sigmoid(x)


def _silu(x):
    return x * _sig(x)


def _dsilu(x):
    s = _sig(x)
    return s * (1.0 + x * (1.0 - s))


def _pc(body, *, name, grid, in_specs, out_specs, out_shape, scratch=()):
    return pl.pallas_call(
        body, name=name, grid=grid, in_specs=in_specs, out_specs=out_specs, out_shape=out_shape,
        scratch_shapes=list(scratch), compiler_params=pltpu.CompilerParams(vmem_limit_bytes=VMEM_LIMIT))


def _rsum(x):
    return jnp.sum(x, axis=1, keepdims=True)


def _csum(x):
    return jnp.sum(x, axis=0, keepdims=True)


def mm(a, b, *, ta=False, tb=False, add=None, out_dtype=F32, name):
    M, K = (a.shape[1], a.shape[0]) if ta else a.shape
    N = b.shape[0] if tb else b.shape[1]
    tm = _tile(M, MM_LANE_TILE, LANES) if ta else _tile(M, MM_ROW_TILE, 8)
    tn = _tile(N, MM_LANE_TILE, LANES)
    if ta:
        tk = _tile(K, MM_ROW_TILE, 8)
    else:
        tk = K if K <= MM_FULL_K else _tile(K, 2048, LANES)
    nk = K // tk

    def body(*refs):
        if add is None:
            a_ref, b_ref, o_ref = refs[:3]
        else:
            a_ref, b_ref, add_ref, o_ref = refs[:4]
        part = _dg(a_ref[...].astype(BF16), b_ref[...].astype(BF16), 0 if ta else 1, 1 if tb else 0)

        def finish(r):
            if add is not None:
                r = r + add_ref[...].astype(F32)
            o_ref[...] = r.astype(out_dtype)

        if nk == 1:
            finish(part)
            return
        acc = refs[-1]
        k = pl.program_id(2)

        @pl.when(k == 0)
        def _():
            acc[...] = part

        @pl.when(k > 0)
        def _():
            acc[...] += part

        @pl.when(k == nk - 1)
        def _():
            finish(acc[...])

    a_spec = pl.BlockSpec((tk, tm), lambda i, j, k: (k, i)) if ta else pl.BlockSpec((tm, tk), lambda i, j, k: (i, k))
    b_spec = pl.BlockSpec((tn, tk), lambda i, j, k: (j, k)) if tb else pl.BlockSpec((tk, tn), lambda i, j, k: (k, j))
    o_spec = pl.BlockSpec((tm, tn), lambda i, j, k: (i, j))
    ins, specs = [a, b], [a_spec, b_spec]
    if add is not None:
        ins.append(add)
        specs.append(o_spec)
    return _pc(body, name=name, grid=(M // tm, N // tn, nk), in_specs=specs, out_specs=o_spec,
               out_shape=SDS((M, N), out_dtype), scratch=[pltpu.VMEM((tm, tn), F32)] if nk > 1 else [])(*ins)


def rms_fwd(x, g, *, col0=0, width=None, name):
    T = x.shape[0]
    W = width or x.shape[1]
    tm = _tile(T, 512, 8)

    def body(x_ref, g_ref, o_ref):
        v = x_ref[...]
        r = lax.rsqrt(jnp.mean(v * v, axis=1, keepdims=True) + EPS)
        o_ref[...] = ((v * r) * g_ref[...]).astype(BF16)

    return _pc(body, name=name, grid=(T // tm,),
               in_specs=[pl.BlockSpec((tm, W), lambda i: (i, col0 // W)), pl.BlockSpec((1, W), lambda i: (0, 0))],
               out_specs=pl.BlockSpec((tm, W), lambda i: (i, 0)), out_shape=SDS((T, W), BF16))(x, g)


def rms_bwd(x, g, dh, *, col0=0, width=None, add=None, name):
    T = x.shape[0]
    W = width or x.shape[1]
    tm = _tile(T, 512, 8)

    def body(*refs):
        if add is None:
            x_ref, g_ref, dh_ref, dx_ref, dg_ref = refs
        else:
            x_ref, g_ref, dh_ref, add_ref, dx_ref, dxb_ref, dg_ref = refs
        v = x_ref[...]
        r = lax.rsqrt(jnp.mean(v * v, axis=1, keepdims=True) + EPS)
        xh = v * r
        d = dh_ref[...].astype(F32)
        dxh = d * g_ref[...]
        dx = r * (dxh - xh * jnp.mean(dxh * xh, axis=1, keepdims=True))
        if add is not None:
            dx = dx + add_ref[...]
            dxb_ref[...] = dx.astype(BF16)
        dx_ref[...] = dx

        @pl.when(pl.program_id(0) == 0)
        def _():
            dg_ref[...] = jnp.zeros_like(dg_ref)

        dg_ref[...] += _csum(d * xh)

    row = pl.BlockSpec((tm, W), lambda i: (i, 0))
    vec = pl.BlockSpec((1, W), lambda i: (0, 0))
    ins = [x, g, dh] + ([] if add is None else [add])
    specs = [pl.BlockSpec((tm, W), lambda i: (i, col0 // W)), vec, row] + ([] if add is None else [row])
    if add is None:
        return _pc(body, name=name, grid=(T // tm,), in_specs=specs, out_specs=[row, vec],
                   out_shape=[SDS((T, W), F32), SDS((1, W), F32)])(*ins)
    return _pc(body, name=name, grid=(T // tm,), in_specs=specs, out_specs=[row, row, vec],
               out_shape=[SDS((T, W), F32), SDS((T, W), BF16), SDS((1, W), F32)])(*ins)


def gated_rms_fwd(y, proj, g, *, name):
    T = y.shape[0]
    tm = _tile(T, 512, 8)

    def body(y_ref, z_ref, g_ref, o_ref):
        v = y_ref[...] * _silu(z_ref[...])
        r = lax.rsqrt(jnp.mean(v * v, axis=1, keepdims=True) + EPS)
        o_ref[...] = ((v * r) * g_ref[...]).astype(BF16)

    row = pl.BlockSpec((tm, SSD_W), lambda i: (i, 0))
    return _pc(body, name=name, grid=(T // tm,), in_specs=[row, row, pl.BlockSpec((1, SSD_W), lambda i: (0, 0))],
               out_specs=row, out_shape=SDS((T, SSD_W), BF16))(y, proj, g)


def gated_rms_bwd(y, proj, g, dycat, *, name):
    T = y.shape[0]
    tm = _tile(T, 512, 8)

    def body(y_ref, z_ref, g_ref, d_ref, dy_ref, dz_ref, dg_ref):
        yv, z = y_ref[...], z_ref[...]
        sz = _silu(z)
        v = yv * sz
        r = lax.rsqrt(jnp.mean(v * v, axis=1, keepdims=True) + EPS)
        vh = v * r
        d = d_ref[...]
        dvh = d * g_ref[...]
        dv = r * (dvh - vh * jnp.mean(dvh * vh, axis=1, keepdims=True))
        dy_ref[...] = dv * sz
        dz_ref[...] = dv * yv * _dsilu(z)

        @pl.when(pl.program_id(0) == 0)
        def _():
            dg_ref[...] = jnp.zeros_like(dg_ref)

        dg_ref[...] += _csum(d * vh)

    row = pl.BlockSpec((tm, SSD_W), lambda i: (i, 0))
    vec = pl.BlockSpec((1, SSD_W), lambda i: (0, 0))
    return _pc(body, name=name, grid=(T // tm,), in_specs=[row, row, vec, row], out_specs=[row, row, vec],
               out_shape=[SDS((T, SSD_W), F32), SDS((T, SSD_W), F32), SDS((1, SSD_W), F32)])(y, proj, g, dycat)


def final_loss(x, g, tgt, *, name):
    T = x.shape[0]
    tm = _tile(T, 512, 8)

    def body(x_ref, g_ref, t_ref, dx_ref, dxb_ref, dg_ref, l_ref):
        v = x_ref[...]
        gg = g_ref[...]
        r = lax.rsqrt(jnp.mean(v * v, axis=1, keepdims=True) + EPS)
        xh = v * r
        err = xh * gg - t_ref[...]
        part = 0.5 * _csum(jnp.mean(err * err, axis=1, keepdims=True))
        d = err * (1.0 / D)
        dxh = d * gg
        dx = r * (dxh - xh * jnp.mean(dxh * xh, axis=1, keepdims=True))
        dx_ref[...] = dx
        dxb_ref[...] = dx.astype(BF16)

        @pl.when(pl.program_id(0) == 0)
        def _():
            dg_ref[...] = jnp.zeros_like(dg_ref)
            l_ref[...] = jnp.zeros_like(l_ref)

        dg_ref[...] += _csum(d * xh)
        l_ref[...] += jnp.broadcast_to(part, (1, LANES))

    row = pl.BlockSpec((tm, D), lambda i: (i, 0))
    vec = pl.BlockSpec((1, D), lambda i: (0, 0))
    return _pc(body, name=name, grid=(T // tm,), in_specs=[row, vec, row],
               out_specs=[row, row, vec, pl.BlockSpec((1, LANES), lambda i: (0, 0))],
               out_shape=[SDS((T, D), F32), SDS((T, D), BF16), SDS((1, D), F32), SDS((1, LANES), F32)])(x, g, tgt)


HALO = 8


def _prev_map(ts, col):
    return lambda b, i, j: (b, jnp.maximum(i * (ts // HALO) - 1, 0), col(j))


def _next_map(ts, n_halo_blocks, col):
    return lambda b, i, j: (b, jnp.minimum((i + 1) * (ts // HALO), n_halo_blocks - 1), col(j))


def conv_silu_fwd(proj3, w, b, *, name):
    Bl, S, _ = proj3.shape
    C, K = SSD_XBC, SSD_K
    ts, tc = _tile(S, 512, 8), 512
    c0 = XBC0 // tc

    def body(xp_ref, x_ref, w_ref, b_ref, o_ref, ext):
        i = pl.program_id(1)
        ext[0:HALO, :] = jnp.where(i > 0, xp_ref[0], 0.0)
        ext[HALO:HALO + ts, :] = x_ref[0]
        acc = b_ref[...] + w_ref[0:1, :] * ext[pl.ds(HALO - (K - 1), ts), :]
        for k in range(1, K):
            acc = acc + w_ref[k:k + 1, :] * ext[pl.ds(HALO - (K - 1) + k, ts), :]
        o_ref[0] = _silu(acc)

    return _pc(body, name=name, grid=(Bl, S // ts, C // tc),
               in_specs=[pl.BlockSpec((1, HALO, tc), _prev_map(ts, lambda j: c0 + j)),
                         pl.BlockSpec((1, ts, tc), lambda b, i, j: (b, i, c0 + j)),
                         pl.BlockSpec((K, tc), lambda b, i, j: (0, j)),
                         pl.BlockSpec((1, tc), lambda b, i, j: (0, j))],
               out_specs=pl.BlockSpec((1, ts, tc), lambda b, i, j: (b, i, j)),
               out_shape=SDS((Bl, S, C), F32), scratch=[pltpu.VMEM((HALO + ts, tc), F32)])(proj3, proj3, w, b)


def conv_silu_bwd(proj3, w, b, dact, *, name):
    Bl, S, _ = proj3.shape
    C, K = SSD_XBC, SSD_K
    ts, tc = _tile(S, 512, 8), 512
    c0 = XBC0 // tc
    ns = S // ts

    def body(xp_ref, x_ref, xn_ref, d_ref, dn_ref, w_ref, b_ref, dx_ref, dw_ref, db_ref, ext, ext2):
        bb, i = pl.program_id(1), pl.program_id(2)
        last = i == ns - 1
        ext[0:HALO, :] = jnp.where(i > 0, xp_ref[0], 0.0)
        ext[HALO:HALO + ts, :] = x_ref[0]
        ext[HALO + ts:2 * HALO + ts, :] = jnp.where(last, 0.0, xn_ref[0])
        acc = b_ref[...] + w_ref[0:1, :] * ext[pl.ds(HALO - (K - 1), ts + HALO), :]
        for k in range(1, K):
            acc = acc + w_ref[k:k + 1, :] * ext[pl.ds(HALO - (K - 1) + k, ts + HALO), :]
        dsl = _dsilu(acc)
        du = d_ref[0] * dsl[0:ts]
        ext2[0:ts, :] = du
        ext2[ts:ts + HALO, :] = jnp.where(last, 0.0, dn_ref[0]) * dsl[ts:ts + HALO]
        dx = w_ref[0:1, :] * ext2[pl.ds(K - 1, ts), :]
        for k in range(1, K):
            dx = dx + w_ref[k:k + 1, :] * ext2[pl.ds(K - 1 - k, ts), :]
        dx_ref[0] = dx

        @pl.when((bb == 0) & (i == 0))
        def _():
            dw_ref[...] = jnp.zeros_like(dw_ref)
            db_ref[...] = jnp.zeros_like(db_ref)

        for k in range(K):
            dw_ref[k:k + 1, :] += _csum(du * ext[pl.ds(HALO - (K - 1) + k, ts), :])
        db_ref[...] += _csum(du)

    nhb = S // HALO
    cx = lambda j: c0 + j
    cj = lambda j: j
    return _pc(body, name=name, grid=(C // tc, Bl, ns),
               in_specs=[pl.BlockSpec((1, HALO, tc), lambda j, b, i: _prev_map(ts, cx)(b, i, j)),
                         pl.BlockSpec((1, ts, tc), lambda j, b, i: (b, i, c0 + j)),
                         pl.BlockSpec((1, HALO, tc), lambda j, b, i: _next_map(ts, nhb, cx)(b, i, j)),
                         pl.BlockSpec((1, ts, tc), lambda j, b, i: (b, i, j)),
                         pl.BlockSpec((1, HALO, tc), lambda j, b, i: _next_map(ts, nhb, cj)(b, i, j)),
                         pl.BlockSpec((K, tc), lambda j, b, i: (0, j)),
                         pl.BlockSpec((1, tc), lambda j, b, i: (0, j))],
               out_specs=[pl.BlockSpec((1, ts, tc), lambda j, b, i: (b, i, j)),
                          pl.BlockSpec((K, tc), lambda j, b, i: (0, j)),
                          pl.BlockSpec((1, tc), lambda j, b, i: (0, j))],
               out_shape=[SDS((Bl, S, C), F32), SDS((K, C), F32), SDS((1, C), F32)],
               scratch=[pltpu.VMEM((2 * HALO + ts, tc), F32), pltpu.VMEM((HALO + ts, tc), F32)],
               )(proj3, proj3, proj3, dact, dact, w, b)


def ffn_act_fwd(pre3, w, b, *, name):
    Bl, S, _ = pre3.shape
    K = FFN_K
    ts, tc = _tile(S, 512, 8), 256
    nj = DFF // tc

    def body(gp_ref, g_ref, vp_ref, v_ref, wg_ref, wv_ref, bg_ref, bv_ref, o_ref, eg, ev):
        i = pl.program_id(1)
        outs = []
        for p_ref, m_ref, w_ref, b_ref, ext in ((gp_ref, g_ref, wg_ref, bg_ref, eg), (vp_ref, v_ref, wv_ref, bv_ref, ev)):
            ext[0:HALO, :] = jnp.where(i > 0, p_ref[0], 0.0)
            ext[HALO:HALO + ts, :] = m_ref[0]
            acc = b_ref[...] + w_ref[0:1, :] * ext[pl.ds(HALO - (K - 1), ts), :]
            for k in range(1, K):
                acc = acc + w_ref[k:k + 1, :] * ext[pl.ds(HALO - (K - 1) + k, ts), :]
            outs.append(acc)
        o_ref[0] = (_silu(outs[0]) * outs[1]).astype(BF16)

    main = lambda off: pl.BlockSpec((1, ts, tc), lambda b, i, j: (b, i, off + j))
    prev = lambda off: pl.BlockSpec((1, HALO, tc), _prev_map(ts, lambda j: off + j))
    wsp = lambda off: pl.BlockSpec((K, tc), lambda b, i, j: (0, off + j))
    bsp = lambda off: pl.BlockSpec((1, tc), lambda b, i, j: (0, off + j))
    return _pc(body, name=name, grid=(Bl, S // ts, nj),
               in_specs=[prev(0), main(0), prev(nj), main(nj), wsp(0), wsp(nj), bsp(0), bsp(nj)],
               out_specs=pl.BlockSpec((1, ts, tc), lambda b, i, j: (b, i, j)),
               out_shape=SDS((Bl, S, DFF), BF16),
               scratch=[pltpu.VMEM((HALO + ts, tc), F32), pltpu.VMEM((HALO + ts, tc), F32)],
               )(pre3, pre3, pre3, pre3, w, w, b, b)


def ffn_act_bwd(pre3, w, b, dact, *, name):
    Bl, S, _ = pre3.shape
    K = FFN_K
    ts, tc = _tile(S, 512, 8), 256
    nj = DFF // tc
    ns = S // ts

    def body(gp_ref, g_ref, gn_ref, vp_ref, v_ref, vn_ref, d_ref, dn_ref, wg_ref, wv_ref, bg_ref, bv_ref,
             dg_ref, dv_ref, dwg_ref, dwv_ref, dbg_ref, dbv_ref, eg, ev, e2g, e2v):
        bb, i = pl.program_id(1), pl.program_id(2)
        last = i == ns - 1
        ups = []
        for p_ref, m_ref, n_ref, w_ref, b_ref, ext in ((gp_ref, g_ref, gn_ref, wg_ref, bg_ref, eg),
                                                       (vp_ref, v_ref, vn_ref, wv_ref, bv_ref, ev)):
            ext[0:HALO, :] = jnp.where(i > 0, p_ref[0], 0.0)
            ext[HALO:HALO + ts, :] = m_ref[0]
            ext[HALO + ts:2 * HALO + ts, :] = jnp.where(last, 0.0, n_ref[0])
            acc = b_ref[...] + w_ref[0:1, :] * ext[pl.ds(HALO - (K - 1), ts + HALO), :]
            for k in range(1, K):
                acc = acc + w_ref[k:k + 1, :] * ext[pl.ds(HALO - (K - 1) + k, ts + HALO), :]
            ups.append(acc)
        ug, uv = ups
        dg_e = uv * _dsilu(ug)
        dv_e = _silu(ug)
        d_main = d_ref[0]
        d_next = jnp.where(last, 0.0, dn_ref[0])
        dug = d_main * dg_e[0:ts]
        duv = d_main * dv_e[0:ts]
        e2g[0:ts, :] = dug
        e2g[ts:ts + HALO, :] = d_next * dg_e[ts:ts + HALO]
        e2v[0:ts, :] = duv
        e2v[ts:ts + HALO, :] = d_next * dv_e[ts:ts + HALO]

        @pl.when((bb == 0) & (i == 0))
        def _():
            for r in (dwg_ref, dwv_ref, dbg_ref, dbv_ref):
                r[...] = jnp.zeros_like(r)

        for w_ref, e2, ext, du, o_ref, dw_ref, db_ref in ((wg_ref, e2g, eg, dug, dg_ref, dwg_ref, dbg_ref),
                                                          (wv_ref, e2v, ev, duv, dv_ref, dwv_ref, dbv_ref)):
            dx = w_ref[0:1, :] * e2[pl.ds(K - 1, ts), :]
            for k in range(1, K):
                dx = dx + w_ref[k:k + 1, :] * e2[pl.ds(K - 1 - k, ts), :]
            o_ref[0] = dx.astype(BF16)
            for k in range(K):
                dw_ref[k:k + 1, :] += _csum(du * ext[pl.ds(HALO - (K - 1) + k, ts), :])
            db_ref[...] += _csum(du)

    nhb = S // HALO
    main = lambda off: pl.BlockSpec((1, ts, tc), lambda j, b, i: (b, i, off + j))
    prev = lambda off: pl.BlockSpec((1, HALO, tc), lambda j, b, i: _prev_map(ts, lambda jj: off + jj)(b, i, j))
    nxt = lambda off: pl.BlockSpec((1, HALO, tc), lambda j, b, i: _next_map(ts, nhb, lambda jj: off + jj)(b, i, j))
    wsp = lambda off: pl.BlockSpec((K, tc), lambda j, b, i: (0, off + j))
    bsp = lambda off: pl.BlockSpec((1, tc), lambda j, b, i: (0, off + j))
    outs = _pc(body, name=name, grid=(nj, Bl, ns),
               in_specs=[prev(0), main(0), nxt(0), prev(nj), main(nj), nxt(nj), main(0), nxt(0),
                         wsp(0), wsp(nj), bsp(0), bsp(nj)],
               out_specs=[main(0), main(0), wsp(0), wsp(0), bsp(0), bsp(0)],
               out_shape=[SDS((Bl, S, DFF), BF16), SDS((Bl, S, DFF), BF16), SDS((K, DFF), F32), SDS((K, DFF), F32),
                          SDS((1, DFF), F32), SDS((1, DFF), F32)],
               scratch=[pltpu.VMEM((2 * HALO + ts, tc), F32), pltpu.VMEM((2 * HALO + ts, tc), F32),
                        pltpu.VMEM((HALO + ts, tc), F32), pltpu.VMEM((HALO + ts, tc), F32)],
               )(pre3, pre3, pre3, pre3, pre3, pre3, dact, dact, w, w, b, b)
    return outs


PHALO = 16


def _pool_window_sums(ext, base, ts, step):
    s = ext[pl.ds(base, ts), :]
    out = []
    for i in range(1, PHALO):
        s = s + ext[pl.ds(base + step * i, ts), :]
        if i + 1 in POOL_WIN:
            out.append(s)
    return out


def _pick(g, vals):
    r = vals[-1]
    for k in range(len(vals) - 2, -1, -1):
        r = jnp.where(g == k, vals[k], r)
    return r


def _pool_count(g, i, ts, rows):
    t = (i * ts + lax.broadcasted_iota(jnp.int32, (rows, 1), 0) + 1).astype(F32)
    return jnp.minimum(t, _pick(g, [float(w) for w in POOL_WIN]))


def _pooled(up_ref, u_ref, ext, g, i, ts):
    ext[0:PHALO, :] = jnp.where(i > 0, up_ref[0], 0.0)
    u = u_ref[0]
    ext[PHALO:PHALO + ts, :] = u
    sums = _pool_window_sums(ext, PHALO, ts, -1)
    return _pick(g, sums) / _pool_count(g, i, ts, ts) - u


def pool_fwd(proj3, pool_w, scale, *, name):
    Bl, S, _ = proj3.shape
    ts = _tile(S, 512, 16)
    c0 = U0 // POOL_D

    def body(up_ref, u_ref, w_ref, s_ref, o_ref, ext):
        i, g = pl.program_id(1), pl.program_id(2)
        pooled = _pooled(up_ref, u_ref, ext, g, i, ts)
        o_ref[0] = (_nn(pooled, w_ref[0]) * s_ref[...]).astype(BF16)

    return _pc(body, name=name, grid=(Bl, S // ts, POOL_G),
               in_specs=[pl.BlockSpec((1, PHALO, POOL_D), lambda b, i, g: (b, jnp.maximum(i * (ts // PHALO) - 1, 0), c0 + g)),
                         pl.BlockSpec((1, ts, POOL_D), lambda b, i, g: (b, i, c0 + g)),
                         pl.BlockSpec((1, POOL_D, POOL_D), lambda b, i, g: (g, 0, 0)),
                         pl.BlockSpec((1, POOL_D), lambda b, i, g: (0, g))],
               out_specs=pl.BlockSpec((1, ts, POOL_D), lambda b, i, g: (b, i, g)),
               out_shape=SDS((Bl, S, POOL_W), BF16), scratch=[pltpu.VMEM((PHALO + ts, POOL_D), F32)],
               )(proj3, proj3, pool_w, scale)


def pool_bwd(proj3, pool_w, scale, dycat3, *, name):
    Bl, S, _ = proj3.shape
    ts = _tile(S, 512, 16)
    ns = S // ts
    c0 = U0 // POOL_D
    d0 = SSD_W // POOL_D
    nhb = S // PHALO

    def body(up_ref, u_ref, d_ref, dn_ref, w_ref, s_ref, du_ref, dw_ref, ds_ref, ext, ext2):
        g, bb, i = pl.program_id(0), pl.program_id(1), pl.program_id(2)
        last = i == ns - 1
        pooled = _pooled(up_ref, u_ref, ext, g, i, ts)
        wm = w_ref[0]
        sc = s_ref[...]
        dy = d_ref[0]
        dp_main = dy * sc
        dpool = _nt(dp_main, wm)
        dpool_n = _nt(jnp.where(last, 0.0, dn_ref[0]) * sc, wm)
        ext2[0:ts, :] = dpool / _pool_count(g, i, ts, ts)
        ext2[ts:ts + PHALO, :] = dpool_n / _pool_count(g, i + 1, ts, PHALO)
        sums = _pool_window_sums(ext2, 0, ts, 1)
        du_ref[0] = _pick(g, sums) - dpool

        @pl.when((bb == 0) & (i == 0))
        def _():
            dw_ref[...] = jnp.zeros_like(dw_ref)
            ds_ref[...] = jnp.zeros_like(ds_ref)

        dw_ref[0] += _tn(pooled, dp_main)
        ds_ref[...] += _csum(dy * _nn(pooled, wm))

    return _pc(body, name=name, grid=(POOL_G, Bl, ns),
               in_specs=[pl.BlockSpec((1, PHALO, POOL_D), lambda g, b, i: (b, jnp.maximum(i * (ts // PHALO) - 1, 0), c0 + g)),
                         pl.BlockSpec((1, ts, POOL_D), lambda g, b, i: (b, i, c0 + g)),
                         pl.BlockSpec((1, ts, POOL_D), lambda g, b, i: (b, i, d0 + g)),
                         pl.BlockSpec((1, PHALO, POOL_D), lambda g, b, i: (b, jnp.minimum((i + 1) * (ts // PHALO), nhb - 1), d0 + g)),
                         pl.BlockSpec((1, POOL_D, POOL_D), lambda g, b, i: (g, 0, 0)),
                         pl.BlockSpec((1, POOL_D), lambda g, b, i: (0, g))],
               out_specs=[pl.BlockSpec((1, ts, POOL_D), lambda g, b, i: (b, i, g)),
                          pl.BlockSpec((1, POOL_D, POOL_D), lambda g, b, i: (g, 0, 0)),
                          pl.BlockSpec((1, POOL_D), lambda g, b, i: (0, g))],
               out_shape=[SDS((Bl, S, POOL_W), F32), SDS((POOL_G, POOL_D, POOL_D), F32), SDS((1, POOL_W), F32)],
               scratch=[pltpu.VMEM((PHALO + ts, POOL_D), F32), pltpu.VMEM((PHALO + ts, POOL_D), F32)],
               )(proj3, proj3, dycat3, dycat3, pool_w, scale)


NPAIR = SSD_HEADS // 2


def _ssd_common(sm, bias, alog):
    L = SSD_L
    dt = jax.nn.softplus(sm + bias)
    a = -jnp.exp(alog)
    da = dt * a
    r = lax.broadcasted_iota(jnp.int32, (L, L), 0)
    c = lax.broadcasted_iota(jnp.int32, (L, L), 1)
    tri = (r >= c).astype(F32)
    cum = _dg(tri, da, 1, 0, lax.Precision.HIGHEST)
    return dt, a, cum, cum.T, r >= c


def _lanes(lo, hi, shape=(1, LANES)):
    lane = lax.broadcasted_iota(jnp.int32, shape, len(shape) - 1)
    return (lane >= lo) & (lane < hi)


def _onehot_lane(h):
    return (lax.broadcasted_iota(jnp.int32, (1, LANES), 1) == h).astype(F32)


def ssd_fwd(xbc3, proj3, bias, alog, dskip, *, name):
    Bl, S, _ = xbc3.shape
    L = SSD_L
    nc = S // L

    def body(xbc_ref, sm_ref, bias_ref, alog_ref, d_ref, y_ref, hin_ref, H):
        c = pl.program_id(1)

        @pl.when(c == 0)
        def _():
            H[...] = jnp.zeros_like(H)

        dt, a, cum, cumT, mask = _ssd_common(sm_ref[0], bias_ref[...], alog_ref[...])
        lo = _lanes(0, SSD_P)
        rowlo = lax.broadcasted_iota(jnp.int32, (LANES, LANES), 0) < SSD_P
        cb = []
        for g in range(SSD_G):
            Bg = xbc_ref[0, :, SSD_W + g * SSD_N:SSD_W + (g + 1) * SSD_N]
            Cg = xbc_ref[0, :, SSD_W + SSD_G * SSD_N + g * SSD_N:SSD_W + SSD_G * SSD_N + (g + 1) * SSD_N]
            cb.append((Bg, Cg, _nt(Cg, Bg)))
        for j in range(NPAIR):
            h0, h1 = 2 * j, 2 * j + 1
            Bg, Cg, CB = cb[j // (NPAIR // SSD_G)]
            X = xbc_ref[0, :, j * LANES:(j + 1) * LANES]
            c0, c1 = cum[:, h0:h0 + 1], cum[:, h1:h1 + 1]
            r0, r1 = cumT[h0:h0 + 1, :], cumT[h1:h1 + 1, :]
            cl0, cl1 = cum[L - 1:L, h0:h0 + 1], cum[L - 1:L, h1:h1 + 1]
            Xt = X * jnp.where(lo, dt[:, h0:h0 + 1], dt[:, h1:h1 + 1])
            M0 = CB * jnp.exp(jnp.where(mask, c0 - r0, NEG))
            M1 = CB * jnp.exp(jnp.where(mask, c1 - r1, NEG))
            Yd = jnp.where(lo, _nn(M0, Xt), _nn(M1, Xt))
            Hp = H[j]
            hin_ref[0, 0, j] = Hp
            Z = _nt(Cg, Hp)
            y_ref[0, :, j * LANES:(j + 1) * LANES] = Yd + jnp.where(lo, jnp.exp(c0), jnp.exp(c1)) * Z + X * d_ref[j:j + 1, :]
            wl = jnp.where(lo, jnp.exp(cl0 - c0), jnp.exp(cl1 - c1))
            H[j] = jnp.where(rowlo, jnp.exp(cl0), jnp.exp(cl1)) * Hp + _tn(wl * Xt, Bg)

    vec = pl.BlockSpec((1, LANES), lambda b, c: (0, 0))
    return _pc(body, name=name, grid=(Bl, nc),
               in_specs=[pl.BlockSpec((1, L, SSD_XBC), lambda b, c: (b, c, 0)),
                         pl.BlockSpec((1, L, LANES), lambda b, c: (b, c, DT0 // LANES)),
                         vec, vec, pl.BlockSpec((NPAIR, LANES), lambda b, c: (0, 0))],
               out_specs=[pl.BlockSpec((1, L, SSD_W), lambda b, c: (b, c, 0)),
                          pl.BlockSpec((1, 1, NPAIR, LANES, LANES), lambda b, c: (b, c, 0, 0, 0))],
               out_shape=[SDS((Bl, S, SSD_W), F32), SDS((Bl, nc, NPAIR, LANES, LANES), F32)],
               scratch=[pltpu.VMEM((NPAIR, LANES, LANES), F32)])(xbc3, proj3, bias, alog, dskip)


def ssd_bwd(xbc3, proj3, hin, dy3, bias, alog, dskip, *, name):
    Bl, S, _ = xbc3.shape
    L = SSD_L
    nc = S // L

    def body(xbc_ref, sm_ref, hin_ref, dy_ref, bias_ref, alog_ref, d_ref, dx_ref, ddt_ref, dpar_ref, dd_ref, dH, ddacc):
        bb, i = pl.program_id(0), pl.program_id(1)

        @pl.when(i == 0)
        def _():
            dH[...] = jnp.zeros_like(dH)

        @pl.when((bb == 0) & (i == 0))
        def _():
            dpar_ref[...] = jnp.zeros_like(dpar_ref)
            ddacc[...] = jnp.zeros_like(ddacc)

        sm = sm_ref[0]
        dt, a, cum, cumT, mask = _ssd_common(sm, bias_ref[...], alog_ref[...])
        maskf = mask.astype(F32)
        lo = _lanes(0, SSD_P)
        rowlo = lax.broadcasted_iota(jnp.int32, (LANES, LANES), 0) < SSD_P
        lastrow = (lax.broadcasted_iota(jnp.int32, (L, 1), 0) == L - 1).astype(F32)
        dcum = jnp.zeros((L, LANES), F32)
        ddt = jnp.zeros((L, LANES), F32)
        grp = []
        for g in range(SSD_G):
            Bg = xbc_ref[0, :, SSD_W + g * SSD_N:SSD_W + (g + 1) * SSD_N]
            Cg = xbc_ref[0, :, SSD_W + SSD_G * SSD_N + g * SSD_N:SSD_W + SSD_G * SSD_N + (g + 1) * SSD_N]
            grp.append(dict(B=Bg, C=Cg, CB=_nt(Cg, Bg), dB=jnp.zeros((L, SSD_N), F32), dC=jnp.zeros((L, SSD_N), F32),
                            dCB=jnp.zeros((L, L), F32)))
        for j in range(NPAIR):
            h0, h1 = 2 * j, 2 * j + 1
            G = grp[j // (NPAIR // SSD_G)]
            Bg, Cg, CB = G["B"], G["C"], G["CB"]
            X = xbc_ref[0, :, j * LANES:(j + 1) * LANES]
            dY = dy_ref[0, :, j * LANES:(j + 1) * LANES]
            c0, c1 = cum[:, h0:h0 + 1], cum[:, h1:h1 + 1]
            r0, r1 = cumT[h0:h0 + 1, :], cumT[h1:h1 + 1, :]
            cl0, cl1 = cum[L - 1:L, h0:h0 + 1], cum[L - 1:L, h1:h1 + 1]
            oh0, oh1 = _onehot_lane(h0), _onehot_lane(h1)
            dtl = jnp.where(lo, dt[:, h0:h0 + 1], dt[:, h1:h1 + 1])
            Xt = X * dtl
            e0, e1 = jnp.exp(c0), jnp.exp(c1)
            el = jnp.where(lo, e0, e1)
            w0, w1 = jnp.exp(cl0 - c0), jnp.exp(cl1 - c1)
            wl = jnp.where(lo, w0, w1)
            Hp = hin_ref[0, 0, j]
            dS = dH[j]
            dX = dY * d_ref[j:j + 1, :]
            ddacc[j:j + 1, :] += _csum(dY * X)
            Z = _nt(Cg, Hp)
            dZ = dY * el
            t = dY * Z
            dcum = dcum + (_rsum(jnp.where(lo, t, 0.0)) * e0) * oh0 + (_rsum(jnp.where(lo, 0.0, t)) * e1) * oh1
            G["dC"] = G["dC"] + _nn(dZ, Hp)
            dHy = _tn(dZ, Cg)
            Gm = _nt(Bg, dS)
            dXt = wl * Gm
            u = Xt * Gm
            q0 = _rsum(jnp.where(lo, u, 0.0)) * w0
            q1 = _rsum(jnp.where(lo, 0.0, u)) * w1
            dcum = dcum + (lastrow * _csum(q0) - q0) * oh0 + (lastrow * _csum(q1) - q1) * oh1
            G["dB"] = G["dB"] + _nn(wl * Xt, dS)
            g0, g1 = jnp.exp(cl0), jnp.exp(cl1)
            prod = dS * Hp
            dg0 = _csum(_rsum(jnp.where(rowlo, prod, 0.0)))
            dg1 = _csum(_rsum(jnp.where(rowlo, 0.0, prod)))
            dcum = dcum + lastrow * ((dg0 * g0) * oh0 + (dg1 * g1) * oh1)
            dH[j] = jnp.where(rowlo, g0, g1) * dS + dHy
            for ch, rh, mh, oh in ((c0, r0, lo, oh0), (c1, r1, jnp.logical_not(lo), oh1)):
                decay = jnp.exp(jnp.where(mask, ch - rh, NEG))
                Mh = CB * decay
                dM = _nt(jnp.where(mh, dY, 0.0), Xt) * maskf
                dXt = dXt + jnp.where(mh, _tn(Mh, dY), 0.0)
                G["dCB"] = G["dCB"] + dM * decay
                Q = dM * Mh
                dcum = dcum + (_rsum(Q) - _rsum(Q.T)) * oh
            dX = dX + dXt * dtl
            s = dXt * X
            ddt = ddt + _rsum(jnp.where(lo, s, 0.0)) * oh0 + _rsum(jnp.where(lo, 0.0, s)) * oh1
            dx_ref[0, :, j * LANES:(j + 1) * LANES] = dX
        for g in range(SSD_G):
            G = grp[g]
            dC = G["dC"] + _nn(G["dCB"], G["B"])
            dB = G["dB"] + _tn(G["dCB"], G["C"])
            dx_ref[0, :, SSD_W + g * SSD_N:SSD_W + (g + 1) * SSD_N] = dB
            dx_ref[0, :, SSD_W + SSD_G * SSD_N + g * SSD_N:SSD_W + SSD_G * SSD_N + (g + 1) * SSD_N] = dC
        r = lax.broadcasted_iota(jnp.int32, (L, L), 0)
        c = lax.broadcasted_iota(jnp.int32, (L, L), 1)
        dda = _dg((c >= r).astype(F32), dcum, 1, 0, lax.Precision.HIGHEST)
        heads = _lanes(0, SSD_HEADS)
        ddt = ddt + dda * a
        draw = jnp.where(heads, ddt * _sig(sm + bias_ref[...]), 0.0)
        ddt_ref[0] = draw
        dpar_ref[0:1, :] += _csum(draw)
        dpar_ref[1:2, :] += _csum(jnp.where(heads, dda * dt * a, 0.0))

        @pl.when((bb == Bl - 1) & (i == nc - 1))
        def _():
            acc = ddacc[...]
            lane = lax.broadcasted_iota(jnp.int32, (NPAIR, LANES), 1)
            s0 = _rsum(jnp.where(lane < SSD_P, acc, 0.0))
            s1 = _rsum(jnp.where(lane < SSD_P, 0.0, acc))
            dd_ref[...] = jnp.where(lane == 0, s0, jnp.where(lane == 1, s1, 0.0))

    vec = pl.BlockSpec((1, LANES), lambda b, i: (0, 0))
    par = pl.BlockSpec((NPAIR, LANES), lambda b, i: (0, 0))
    return _pc(body, name=name, grid=(Bl, nc),
               in_specs=[pl.BlockSpec((1, L, SSD_XBC), lambda b, i: (b, nc - 1 - i, 0)),
                         pl.BlockSpec((1, L, LANES), lambda b, i: (b, nc - 1 - i, DT0 // LANES)),
                         pl.BlockSpec((1, 1, NPAIR, LANES, LANES), lambda b, i: (b, nc - 1 - i, 0, 0, 0)),
                         pl.BlockSpec((1, L, SSD_W), lambda b, i: (b, nc - 1 - i, 0)),
                         vec, vec, par],
               out_specs=[pl.BlockSpec((1, L, SSD_XBC), lambda b, i: (b, nc - 1 - i, 0)),
                          pl.BlockSpec((1, L, LANES), lambda b, i: (b, nc - 1 - i, 0)),
                          par, par],
               out_shape=[SDS((Bl, S, SSD_XBC), F32), SDS((Bl, S, LANES), F32), SDS((NPAIR, LANES), F32),
                          SDS((NPAIR, LANES), F32)],
               scratch=[pltpu.VMEM((NPAIR, LANES, LANES), F32), pltpu.VMEM((NPAIR, LANES), F32)],
               )(xbc3, proj3, hin, dy3, bias, alog, dskip)


PE_LO, PE_MID, PE_HI = MLA_NOPE, MLA_NOPE + MLA_ROPE // 2, MLA_NOPE + MLA_ROPE
ATT_SCALE = 1.0 / math.sqrt(MLA_QK)


def _swap(x):
    first, second = _lanes(PE_LO, PE_MID), _lanes(PE_MID, PE_HI)
    half = MLA_ROPE // 2
    return jnp.where(first, -pltpu.roll(x, LANES - half, 1), jnp.where(second, pltpu.roll(x, half, 1), 0.0))


def rope_tables(pos, invf, *, name):
    T = pos.shape[0]
    tm = _tile(T, 512, 8)

    def body(pos_ref, f_ref, c_ref, s_ref):
        ang = pos_ref[...] * f_ref[...]
        pe = _lanes(PE_LO, PE_HI)
        c_ref[...] = jnp.where(pe, jnp.cos(ang), 1.0)
        s_ref[...] = jnp.where(pe, jnp.sin(ang), 0.0)

    tile = pl.BlockSpec((tm, LANES), lambda i: (i, 0))
    return _pc(body, name=name, grid=(T // tm,),
               in_specs=[pl.BlockSpec((tm, 1), lambda i: (i, 0)), pl.BlockSpec((1, LANES), lambda i: (0, 0))],
               out_specs=[tile, tile], out_shape=[SDS((T, LANES), F32)] * 2)(pos, invf)


V_ONE = MLA_V


def mla_prep_fwd(qt, kvt, proj, cs, sn, *, name):
    T = qt.shape[0]
    tm = _tile(T, 512, 8)

    def body(q_ref, k_ref, v_ref, kpe_ref, c_ref, s_ref, qo_ref, ko_ref, vo_ref):
        c, s = c_ref[...], s_ref[...]
        q = q_ref[...]
        k = k_ref[...] + kpe_ref[...]
        qo_ref[...] = ((q * c + _swap(q) * s) * ATT_SCALE).astype(BF16)
        ko_ref[...] = (k * c + _swap(k) * s).astype(BF16)
        vo_ref[...] = jnp.where(_lanes(V_ONE, V_ONE + 1), 1.0, v_ref[...]).astype(BF16)

    tile = pl.BlockSpec((tm, LANES), lambda i, h: (i, h))
    tab = pl.BlockSpec((tm, LANES), lambda i, h: (i, 0))
    return _pc(body, name=name, grid=(T // tm, MLA_H),
               in_specs=[tile, tile, pl.BlockSpec((tm, LANES), lambda i, h: (i, MLA_H + h)),
                         pl.BlockSpec((tm, LANES), lambda i, h: (i, KPE0 // LANES)), tab, tab],
               out_specs=[tile, tile, tile], out_shape=[SDS((T, MLA_H * LANES), BF16)] * 3)(qt, kvt, kvt, proj, cs, sn)


def mla_prep_bwd(dqr, dkr, cs, sn, *, name):
    T = dqr.shape[0]
    tm = _tile(T, 512, 8)

    def body(dq_ref, dk_ref, c_ref, s_ref, qo_ref, ko_ref, kpe_ref):
        c, s = c_ref[...], s_ref[...]
        dq = dq_ref[...] * ATT_SCALE
        dk = dk_ref[...]
        qo_ref[...] = (dq * c - _swap(dq * s)).astype(BF16)
        dkk = dk * c - _swap(dk * s)
        pe = _lanes(PE_LO, PE_HI)
        ko_ref[...] = jnp.where(pe, 0.0, dkk).astype(BF16)

        @pl.when(pl.program_id(1) == 0)
        def _():
            kpe_ref[...] = jnp.zeros_like(kpe_ref)

        kpe_ref[...] += jnp.where(pe, dkk, 0.0)

    tile = pl.BlockSpec((tm, LANES), lambda i, h: (i, h))
    tab = pl.BlockSpec((tm, LANES), lambda i, h: (i, 0))
    return _pc(body, name=name, grid=(T // tm, MLA_H), in_specs=[tile, tile, tab, tab], out_specs=[tile, tile, tab],
               out_shape=[SDS((T, MLA_H * LANES), BF16), SDS((T, MLA_H * LANES), BF16), SDS((T, LANES), F32)],
               )(dqr, dkr, cs, sn)


def _att_tile(S):
    return _tile(S, 512, LANES)


def _rep(x, n):
    return x if n == 1 else jnp.concatenate([x] * n, axis=1)


def _diag_mask(t, transposed=False):
    r = lax.broadcasted_iota(jnp.int32, (t, t), 0)
    c = lax.broadcasted_iota(jnp.int32, (t, t), 1)
    return (c >= r) if transposed else (c <= r)


def flash_fwd(qr, kr, vr, Bl, *, name):
    T = qr.shape[0]
    S = T // Bl
    t = _att_tile(S)
    n = S // t
    nl = t // LANES

    def body(q_ref, k_ref, v_ref, o_ref, lse_ref, lset_ref, m, acc):
        qi = pl.program_id(2)
        q = q_ref[...]
        m[...] = jnp.full_like(m, NEG)
        acc[...] = jnp.zeros_like(acc)

        def block(kj, masked):
            off = pl.multiple_of(kj * t, t)
            s = _nt(q, k_ref[pl.ds(off, t), :])
            if masked:
                s = jnp.where(_diag_mask(t), s, NEG)
            mo = m[...]
            mn = jnp.maximum(mo, jnp.max(s, axis=1, keepdims=True))
            p = jnp.exp(s - _rep(mn, nl))
            acc[...] = jnp.exp(mo - mn) * acc[...] + _nn(p, v_ref[pl.ds(off, t), :])
            m[...] = mn

        def loop(kj, c):
            block(kj, False)
            return c

        lax.fori_loop(0, qi, loop, 0)
        block(qi, True)
        a = acc[...]
        l = a[:, V_ONE:V_ONE + 1]
        o_ref[...] = jnp.where(_lanes(0, MLA_V), a / l, 0.0).astype(BF16)
        lse = m[...] + jnp.log(l)
        lse_ref[...] = lse
        lset_ref[...] = lse.T[0:8, :]

    qs = pl.BlockSpec((t, LANES), lambda b, h, qi: (b * n + qi, h))
    seq = pl.BlockSpec((S, LANES), lambda b, h, qi: (b, h))
    return _pc(body, name=name, grid=(Bl, MLA_H, n), in_specs=[qs, seq, seq],
               out_specs=[qs, qs, pl.BlockSpec((8, t), lambda b, h, qi: (b * MLA_H + h, qi))],
               out_shape=[SDS((T, MLA_H * LANES), BF16), SDS((T, MLA_H * LANES), F32), SDS((Bl * MLA_H * 8, S), F32)],
               scratch=[pltpu.VMEM((t, LANES), F32), pltpu.VMEM((t, LANES), F32)])(qr, kr, vr)


def flash_bwd_dq(qr, kr, vr, o, lse, dycat, Bl, *, name):
    T = qr.shape[0]
    S = T // Bl
    t = _att_tile(S)
    n = S // t
    nl = t // LANES
    do0 = (SSD_W + POOL_W) // LANES

    def body(q_ref, k_ref, v_ref, o_ref, lse_ref, do_ref, dq_ref, dlt_ref, acc, dl):
        qi = pl.program_id(2)
        q = q_ref[...]
        do = do_ref[...]
        dob = do.astype(BF16)
        dl[...] = jnp.broadcast_to(_rsum(do * o_ref[...].astype(F32)), (t, LANES))
        acc[...] = jnp.zeros_like(acc)

        def block(kj, masked):
            off = pl.multiple_of(kj * t, t)
            k = k_ref[pl.ds(off, t), :]
            s = _nt(q, k)
            if masked:
                s = jnp.where(_diag_mask(t), s, NEG)
            p = jnp.exp(s - _rep(lse_ref[...], nl))
            dp = _nt(dob, v_ref[pl.ds(off, t), :])
            acc[...] += _nn(p * (dp - _rep(dl[...], nl)), k)

        def loop(kj, c):
            block(kj, False)
            return c

        lax.fori_loop(0, qi, loop, 0)
        block(qi, True)
        dq_ref[...] = acc[...]
        dlt_ref[...] = dl[...].T[0:8, :]

    qs = pl.BlockSpec((t, LANES), lambda b, h, qi: (b * n + qi, h))
    seq = pl.BlockSpec((S, LANES), lambda b, h, qi: (b, h))
    return _pc(body, name=name, grid=(Bl, MLA_H, n),
               in_specs=[qs, seq, seq, qs, qs, pl.BlockSpec((t, LANES), lambda b, h, qi: (b * n + qi, do0 + h))],
               out_specs=[qs, pl.BlockSpec((8, t), lambda b, h, qi: (b * MLA_H + h, qi))],
               out_shape=[SDS((T, MLA_H * LANES), F32), SDS((Bl * MLA_H * 8, S), F32)],
               scratch=[pltpu.VMEM((t, LANES), F32), pltpu.VMEM((t, LANES), F32)])(qr, kr, vr, o, lse, dycat)


def flash_bwd_dkv(qr, kr, vr, lset, dlt, dycat, Bl, *, name):
    T = qr.shape[0]
    S = T // Bl
    t = _att_tile(S)
    n = S // t
    do0 = (SSD_W + POOL_W) // LANES

    def body(q_ref, k_ref, v_ref, lset_ref, dlt_ref, do_ref, dk_ref, dv_ref, dka, dva):
        kj = pl.program_id(2)
        k = k_ref[...]
        v = v_ref[...]
        dka[...] = jnp.zeros_like(dka)
        dva[...] = jnp.zeros_like(dva)

        def block(qi, masked):
            off = pl.multiple_of(qi * t, t)
            q = q_ref[pl.ds(off, t), :]
            do = do_ref[pl.ds(off, t), :].astype(BF16)
            st = _nt(k, q)
            if masked:
                st = jnp.where(_diag_mask(t, True), st, NEG)
            pt = jnp.exp(st - lset_ref[0:1, pl.ds(off, t)])
            dst = pt * (_nt(v, do) - dlt_ref[0:1, pl.ds(off, t)])
            dva[...] += _nn(pt, do)
            dka[...] += _nn(dst, q)

        def loop(qi, c):
            block(qi, False)
            return c

        block(kj, True)
        lax.fori_loop(kj + 1, n, loop, 0)
        dk_ref[...] = dka[...]
        dv_ref[...] = dva[...].astype(BF16)

    ks = pl.BlockSpec((t, LANES), lambda b, h, kj: (b * n + kj, h))
    seq = pl.BlockSpec((S, LANES), lambda b, h, kj: (b, h))
    rows = pl.BlockSpec((8, S), lambda b, h, kj: (b * MLA_H + h, 0))
    return _pc(body, name=name, grid=(Bl, MLA_H, n),
               in_specs=[seq, ks, ks, rows, rows, pl.BlockSpec((S, LANES), lambda b, h, kj: (b, do0 + h))],
               out_specs=[ks, ks], out_shape=[SDS((T, MLA_H * LANES), F32), SDS((T, MLA_H * LANES), BF16)],
               scratch=[pltpu.VMEM((t, LANES), F32), pltpu.VMEM((t, LANES), F32)])(qr, kr, vr, lset, dlt, dycat)


def _rows2d(a):
    return a.reshape(-1, a.shape[-1])


def add2(a, b, *, name):
    shp = a.shape
    a2, b2 = _rows2d(a), _rows2d(b)
    R, C = a2.shape
    tm = _tile(R, 512, 8)

    def body(a_ref, b_ref, o_ref):
        o_ref[...] = a_ref[...] + b_ref[...]

    blk = pl.BlockSpec((tm, C), lambda i: (i, 0))
    return _pc(body, name=name, grid=(R // tm,), in_specs=[blk, blk], out_specs=blk, out_shape=SDS((R, C), F32))(a2, b2).reshape(shp)


def adamw(w, m, v, parts, *, name):
    shp = w.shape
    w2, m2, v2 = _rows2d(w), _rows2d(m), _rows2d(v)
    R, C = w2.shape
    P = parts.shape[0]
    p3 = parts.reshape(P, R, C)
    tm = _tile(R, 256, 8)
    bc1 = 1.0 - ADAM_B1 ** ADAM_STEP
    bc2 = 1.0 - ADAM_B2 ** ADAM_STEP

    def body(w_ref, m_ref, v_ref, p_ref, g_ref, d_ref, nm_ref, nv_ref):
        g = p_ref[0]
        for k in range(1, P):
            g = g + p_ref[k]
        mm_ = ADAM_B1 * m_ref[...] + (1.0 - ADAM_B1) * g
        vv = ADAM_B2 * v_ref[...] + (1.0 - ADAM_B2) * (g * g)
        g_ref[...] = g
        nm_ref[...] = mm_
        nv_ref[...] = vv
        d_ref[...] = -ADAM_LR * ((mm_ / bc1) / (jnp.sqrt(vv / bc2) + ADAM_EPS) + ADAM_WD * w_ref[...])

    blk = pl.BlockSpec((tm, C), lambda i: (i, 0))
    outs = _pc(body, name=name, grid=(R // tm,), in_specs=[blk, blk, blk, pl.BlockSpec((P, tm, C), lambda i: (0, i, 0))],
               out_specs=[blk] * 4, out_shape=[SDS((R, C), F32)] * 4)(w2, m2, v2, p3)
    return [o.reshape(shp) for o in outs]


ANY = pl.BlockSpec(memory_space=pl.ANY)


def _place():
    return lax.axis_index("x"), lax.axis_index("y"), lax.axis_index("c")


def all_gather_many(xs, *, name):
    n = len(xs)

    def body(*refs):
        x_refs, o_refs = refs[:n], refs[n:2 * n]
        send_sems, recv_sems, local_sems = refs[2 * n:]
        x, y, c = _place()
        me, sibling = (x, y, c), (x, y, 1 - c)
        chips = [(1 - x, y), (x, 1 - y), (1 - x, 1 - y)]

        def rows(a, p):
            return o_refs[a].at[4 * p[0] + 2 * p[1] + p[2]]

        def copy(a, k, block, to, src=None):
            return pltpu.make_async_remote_copy(
                src_ref=rows(a, block) if src is None else src, dst_ref=rows(a, block),
                send_sem=send_sems.at[7 * a + k], recv_sem=recv_sems.at[7 * a + k], device_id=to, device_id_type=MESH)

        mine = [pltpu.make_async_copy(x_refs[a], rows(a, me), local_sems.at[a]) for a in range(n)]
        for cp in mine:
            cp.start()
        first = []
        for a in range(n):
            first.append(copy(a, 0, me, sibling, src=x_refs[a]))
            first += [copy(a, 1 + j, me, (*chip, c), src=x_refs[a]) for j, chip in enumerate(chips)]
        for cp in first:
            cp.start()
        passed = []
        for j, chip in enumerate(chips):
            for a in range(n):
                copy(a, 1 + j, (*chip, c), me).wait_recv()
                cp = copy(a, 4 + j, (*chip, c), sibling)
                cp.start()
                passed.append(cp)
        for a in range(n):
            copy(a, 0, sibling, me).wait_recv()
            for j, chip in enumerate(chips):
                copy(a, 4 + j, (*chip, 1 - c), me).wait_recv()
        for cp in first + passed:
            cp.wait_send()
        for cp in mine:
            cp.wait()

    return pl.pallas_call(
        body, name=name, in_specs=[ANY] * n, out_specs=[ANY] * n,
        out_shape=[SDS((N_DEV,) + a.shape, a.dtype) for a in xs],
        scratch_shapes=[pltpu.SemaphoreType.DMA((7 * n,)), pltpu.SemaphoreType.DMA((7 * n,)), pltpu.SemaphoreType.DMA((n,))],
    )(*xs)


def exchange_sibling(gs, *, name):
    n = len(gs)

    def body(*refs):
        g_refs, o_refs = refs[:n], refs[n:2 * n]
        send_sems, recv_sems = refs[2 * n:]
        x, y, c = _place()
        cps = []
        for a in range(n):
            for k in range(4):
                cps.append(pltpu.make_async_remote_copy(
                    src_ref=g_refs[a].at[2 * k + (1 - c)], dst_ref=o_refs[a].at[k],
                    send_sem=send_sems.at[4 * a + k], recv_sem=recv_sems.at[4 * a + k], device_id=(x, y, 1 - c), device_id_type=MESH))
        for cp in cps:
            cp.start()
        for cp in cps:
            cp.wait()

    return pl.pallas_call(
        body, name=name, in_specs=[ANY] * n, out_specs=[ANY] * n,
        out_shape=[SDS((4,) + g.shape[1:], g.dtype) for g in gs],
        scratch_shapes=[pltpu.SemaphoreType.DMA((4 * n,)), pltpu.SemaphoreType.DMA((4 * n,))],
    )(*gs)


def exchange_chips(As, *, name):
    n = len(As)

    def body(*refs):
        a_refs, o_refs = refs[:n], refs[n:2 * n]
        send_sems, recv_sems = refs[2 * n:]
        x, y, c = _place()
        chips = [(1 - x, y), (x, 1 - y), (1 - x, 1 - y)]
        cps = []
        for a in range(n):
            for j, chip in enumerate(chips):
                cps.append(pltpu.make_async_remote_copy(
                    src_ref=a_refs[a].at[2 * chip[0] + chip[1]], dst_ref=o_refs[a].at[j],
                    send_sem=send_sems.at[3 * a + j], recv_sem=recv_sems.at[3 * a + j], device_id=(*chip, c), device_id_type=MESH))
        for cp in cps:
            cp.start()
        for cp in cps:
            cp.wait()

    return pl.pallas_call(
        body, name=name, in_specs=[ANY] * n, out_specs=[ANY] * n,
        out_shape=[SDS((3,) + a.shape[1:], a.dtype) for a in As],
        scratch_shapes=[pltpu.SemaphoreType.DMA((3 * n,)), pltpu.SemaphoreType.DMA((3 * n,))],
    )(*As)


def _owner_major(full, axis):
    shp = full.shape
    r = full.reshape(shp[:axis] + (N_DEV, shp[axis] // N_DEV) + shp[axis + 1:])
    return jnp.moveaxis(r, axis, 0)


def _from_owner_major(g8, axis):
    r = jnp.moveaxis(g8, 0, axis)
    shp = r.shape
    return r.reshape(shp[:axis] + (shp[axis] * shp[axis + 1],) + shp[axis + 2:])


def _perm_w_in(w):
    z = jnp.zeros((w.shape[0], LANES), w.dtype)
    dt = jnp.pad(w[:, 2560:2576], ((0, 0), (0, LANES - SSD_HEADS)))
    kpe = jnp.pad(w[:, 3728:3760], ((0, 0), (PE_LO, LANES - PE_HI)))
    return jnp.concatenate([w[:, 0:1024], w[:, 1024:2560], w[:, 2576:3088], w[:, 3088:3472], z, w[:, 3472:3728], dt, kpe], axis=1)


def _unperm_w_in(g):
    return jnp.concatenate([g[:, Z0:Z0 + 1024], g[:, XBC0:XBC0 + 1536], g[:, DT0:DT0 + SSD_HEADS], g[:, U0:U0 + 512],
                            g[:, CQ0:CQ0 + 384], g[:, CKV0:CKV0 + 256], g[:, KPE0 + PE_LO:KPE0 + PE_HI]], axis=1)


def _perm_w_uq(w):
    return jnp.pad(w.reshape(MLA_QR, MLA_H, MLA_QK), ((0, 0), (0, 0), (0, LANES - MLA_QK))).reshape(MLA_QR, MLA_H * LANES)


def _unperm_w_uq(g):
    return g.reshape(MLA_QR, MLA_H, LANES)[:, :, :MLA_QK].reshape(MLA_QR, MLA_H * MLA_QK)


def _perm_w_ukv(w):
    w3 = w.reshape(MLA_KVR, MLA_H, MLA_NOPE + MLA_V)
    pad = ((0, 0), (0, 0), (0, LANES - MLA_NOPE))
    k = jnp.pad(w3[:, :, :MLA_NOPE], pad).reshape(MLA_KVR, MLA_H * LANES)
    v = jnp.pad(w3[:, :, MLA_NOPE:], pad).reshape(MLA_KVR, MLA_H * LANES)
    return jnp.concatenate([k, v], axis=1)


def _unperm_w_ukv(g):
    k = g[:, :MLA_H * LANES].reshape(MLA_KVR, MLA_H, LANES)[:, :, :MLA_NOPE]
    v = g[:, MLA_H * LANES:].reshape(MLA_KVR, MLA_H, LANES)[:, :, :MLA_V]
    return jnp.concatenate([k, v], axis=2).reshape(MLA_KVR, MLA_H * (MLA_NOPE + MLA_V))


def _perm_w_out(w):
    m = jnp.pad(w[SSD_W + POOL_W:].reshape(MLA_H, MLA_V, D), ((0, 0), (0, LANES - MLA_V), (0, 0))).reshape(MLA_H * LANES, D)
    return jnp.concatenate([w[:SSD_W + POOL_W], m], axis=0)


def _unperm_w_out(g):
    m = g[SSD_W + POOL_W:].reshape(MLA_H, LANES, D)[:, :MLA_V].reshape(MLA_H * MLA_V, D)
    return jnp.concatenate([g[:SSD_W + POOL_W], m], axis=0)


def _lane_pad(v):
    return jnp.pad(v.reshape(1, -1), ((0, 0), (0, LANES - v.shape[-1])))


SMALL = ("attn_norm", "ssd_conv_b", "ssd_dt_bias", "ssd_a_log", "ssd_d", "ssd_norm", "pool_w", "pool_scale",
         "mla_q_norm", "mla_kv_norm", "ffn_norm", "ffn_conv_b", "final_norm")
SHARDED = {"w_in": 2, "ssd_conv_w": 2, "mla_w_uq": 2, "mla_w_ukv": 2, "w_out": 1, "ffn_w_up": 2, "ffn_conv_w": 2,
           "ffn_w_down": 1}
ALL_W = ("attn_norm", "w_in", "ssd_conv_w", "ssd_conv_b", "ssd_dt_bias", "ssd_a_log", "ssd_d", "ssd_norm", "pool_w",
         "pool_scale", "mla_q_norm", "mla_w_uq", "mla_kv_norm", "mla_w_ukv", "w_out", "ffn_norm", "ffn_w_up",
         "ffn_conv_w", "ffn_conv_b", "ffn_w_down", "final_norm")


def _pack_small(d):
    rows, layout = [], []
    for k in SMALL:
        a = d[k].reshape(-1)
        n = a.shape[0]
        r = -(-n // LANES)
        rows.append(jnp.pad(a, (0, r * LANES - n)).reshape(r, LANES))
        layout.append((k, n, r, d[k].shape))
    buf = jnp.concatenate(rows, axis=0)
    pad = (-buf.shape[0]) % 8
    return jnp.pad(buf, ((0, pad), (0, 0))), layout


def _unpack_small(buf, layout):
    out, r0 = {}, 0
    for k, n, r, shp in layout:
        out[k] = buf[r0:r0 + r].reshape(-1)[:n].reshape(shp)
        r0 += r
    return out


def _layer_fwd(l, x, W, cs, sn, Bl):
    T = x.shape[0]
    S = T // Bl
    n = f"l{l}_"
    h = rms_fwd(x, W["attn_norm"], name=n + "attn_norm")
    proj = mm(h, W["w_in"], name=n + "w_in")
    proj3 = proj.reshape(Bl, S, PW)
    xbc3 = conv_silu_fwd(proj3, W["ssd_conv_w"], W["ssd_conv_b"], name=n + "ssd_conv")
    y3, hin = ssd_fwd(xbc3, proj3, W["ssd_dt_bias"], W["ssd_a_log"], W["ssd_d"], name=n + "ssd_scan")
    y = y3.reshape(T, SSD_W)
    y_ssd = gated_rms_fwd(y, proj, W["ssd_norm"], name=n + "ssd_gate_norm")
    y_pool = pool_fwd(proj3, W["pool_w"], W["pool_scale"], name=n + "pool").reshape(T, POOL_W)
    qn = rms_fwd(proj, W["mla_q_norm"], col0=CQ0, width=MLA_QR, name=n + "q_norm")
    kvn = rms_fwd(proj, W["mla_kv_norm"], col0=CKV0, width=MLA_KVR, name=n + "kv_norm")
    qt = mm(qn, W["mla_w_uq"], name=n + "w_uq")
    kvt = mm(kvn, W["mla_w_ukv"], name=n + "w_ukv")
    qr, kr, vr = mla_prep_fwd(qt, kvt, proj, cs, sn, name=n + "rope")
    o, lse, lset = flash_fwd(qr, kr, vr, Bl, name=n + "attn")
    ycat = jnp.concatenate([y_ssd, y_pool, o], axis=1)
    x1 = mm(ycat, W["w_out"], add=x, name=n + "w_out")
    h2 = rms_fwd(x1, W["ffn_norm"], name=n + "ffn_norm")
    pre = mm(h2, W["ffn_w_up"], name=n + "w_up")
    pre3 = pre.reshape(Bl, S, 2 * DFF)
    act = ffn_act_fwd(pre3, W["ffn_conv_w"], W["ffn_conv_b"], name=n + "ffn_act").reshape(T, DFF)
    x2 = mm(act, W["ffn_w_down"], add=x1, name=n + "w_down")
    saved = dict(x=x, h=h, proj=proj, xbc3=xbc3, hin=hin, y=y, qn=qn, kvn=kvn, vr=vr, qr=qr, kr=kr, o=o, lse=lse, lset=lset,
                 ycat=ycat, x1=x1, h2=h2, pre3=pre3, act=act)
    return x2, saved


def _layer_bwd(l, dx2, dx2b, W, sv, cs, sn, Bl):
    T = dx2.shape[0]
    S = T // Bl
    n = f"l{l}_b_"
    g = {}
    g["ffn_w_down"] = mm(sv["act"], dx2b, ta=True, name=n + "dw_down")
    dact = mm(dx2b, W["ffn_w_down"], tb=True, name=n + "dact")
    dpg, dpv, dwg, dwv, dbg, dbv = ffn_act_bwd(sv["pre3"], W["ffn_conv_w"], W["ffn_conv_b"], dact.reshape(Bl, S, DFF),
                                               name=n + "ffn_act")
    g["ffn_conv_w"] = jnp.concatenate([dwg, dwv], axis=1)
    g["ffn_conv_b"] = jnp.concatenate([dbg, dbv], axis=1)
    dpg, dpv = dpg.reshape(T, DFF), dpv.reshape(T, DFF)
    g["ffn_w_up"] = jnp.concatenate([mm(sv["h2"], dpg, ta=True, name=n + "dw_up_g"),
                                     mm(sv["h2"], dpv, ta=True, name=n + "dw_up_v")], axis=1)
    dh2 = mm(dpg, W["ffn_w_up"][:, :DFF], tb=True, name=n + "dh2_g")
    dh2 = mm(dpv, W["ffn_w_up"][:, DFF:], tb=True, add=dh2, name=n + "dh2_v")
    dx1, dx1b, g["ffn_norm"] = rms_bwd(sv["x1"], W["ffn_norm"], dh2, add=dx2, name=n + "ffn_norm")
    g["w_out"] = mm(sv["ycat"], dx1b, ta=True, name=n + "dw_out")
    dycat = mm(dx1b, W["w_out"], tb=True, name=n + "dycat")
    proj, proj3 = sv["proj"], sv["proj"].reshape(Bl, S, PW)
    dy, dz, g["ssd_norm"] = gated_rms_bwd(sv["y"], proj, W["ssd_norm"], dycat, name=n + "ssd_gate_norm")
    dxa, ddt, dpar, dd = ssd_bwd(sv["xbc3"], proj3, sv["hin"], dy.reshape(Bl, S, SSD_W), W["ssd_dt_bias"], W["ssd_a_log"],
                                 W["ssd_d"], name=n + "ssd_scan")
    g["ssd_dt_bias"] = dpar[0, :SSD_HEADS]
    g["ssd_a_log"] = dpar[1, :SSD_HEADS]
    g["ssd_d"] = dd[:, :2].reshape(SSD_HEADS)
    dxbc, g["ssd_conv_w"], g["ssd_conv_b"] = conv_silu_bwd(proj3, W["ssd_conv_w"], W["ssd_conv_b"], dxa, name=n + "ssd_conv")
    du, g["pool_w"], g["pool_scale"] = pool_bwd(proj3, W["pool_w"], W["pool_scale"], dycat.reshape(Bl, S, YCAT), name=n + "pool")
    dqr, dlt = flash_bwd_dq(sv["qr"], sv["kr"], sv["vr"], sv["o"], sv["lse"], dycat, Bl, name=n + "attn_dq")
    dkr, dv = flash_bwd_dkv(sv["qr"], sv["kr"], sv["vr"], sv["lset"], dlt, dycat, Bl, name=n + "attn_dkv")
    dqt, dkt, dkpe = mla_prep_bwd(dqr, dkr, cs, sn, name=n + "rope")
    wk, wv = W["mla_w_ukv"][:, :MLA_H * LANES], W["mla_w_ukv"][:, MLA_H * LANES:]
    g["mla_w_ukv"] = jnp.concatenate([mm(sv["kvn"], dkt, ta=True, name=n + "dw_uk"),
                                      mm(sv["kvn"], dv, ta=True, name=n + "dw_uv")], axis=1)
    dkvn = mm(dkt, wk, tb=True, name=n + "dkvn_k")
    dkvn = mm(dv, wv, tb=True, add=dkvn, name=n + "dkvn_v")
    g["mla_w_uq"] = mm(sv["qn"], dqt, ta=True, name=n + "dw_uq")
    dqn = mm(dqt, W["mla_w_uq"], tb=True, name=n + "dqn")
    dcq, g["mla_q_norm"] = rms_bwd(proj, W["mla_q_norm"], dqn, col0=CQ0, width=MLA_QR, name=n + "q_norm")
    dckv, g["mla_kv_norm"] = rms_bwd(proj, W["mla_kv_norm"], dkvn, col0=CKV0, width=MLA_KVR, name=n + "kv_norm")
    dproj = jnp.concatenate([dz, dxbc.reshape(T, SSD_XBC), du.reshape(T, POOL_W), dcq, jnp.zeros((T, LANES), F32), dckv,
                             ddt.reshape(T, LANES), dkpe], axis=1).astype(BF16)
    g["w_in"] = mm(sv["h"], dproj, ta=True, name=n + "dw_in")
    dh = mm(dproj, W["w_in"], tb=True, name=n + "dh")
    dx, dxb, g["attn_norm"] = rms_bwd(sv["x"], W["attn_norm"], dh, add=dx1, name=n + "attn_norm")
    return dx, dxb, g


def kernel(x, positions, attn_norm, w_in, ssd_conv_w, ssd_conv_b, ssd_dt_bias, ssd_a_log, ssd_d, ssd_norm, pool_w, pool_scale, mla_q_norm, mla_w_uq, mla_kv_norm, mla_w_ukv, w_out, ffn_norm, ffn_w_up, ffn_conv_w, ffn_conv_b, ffn_w_down, final_norm, loss_target, m_attn_norm, m_w_in, m_ssd_conv_w, m_ssd_conv_b, m_ssd_dt_bias, m_ssd_a_log, m_ssd_d, m_ssd_norm, m_pool_w, m_pool_scale, m_mla_q_norm, m_mla_w_uq, m_mla_kv_norm, m_mla_w_ukv, m_w_out, m_ffn_norm, m_ffn_w_up, m_ffn_conv_w, m_ffn_conv_b, m_ffn_w_down, m_final_norm, v_attn_norm, v_w_in, v_ssd_conv_w, v_ssd_conv_b, v_ssd_dt_bias, v_ssd_a_log, v_ssd_d, v_ssd_norm, v_pool_w, v_pool_scale, v_mla_q_norm, v_mla_w_uq, v_mla_kv_norm, v_mla_w_ukv, v_w_out, v_ffn_norm, v_ffn_w_up, v_ffn_conv_w, v_ffn_conv_b, v_ffn_w_down, v_final_norm):
    a = locals()
    Wt = {k: a[k] for k in ALL_W}
    Mo = {k: a["m_" + k] for k in ALL_W}
    Vo = {k: a["v_" + k] for k in ALL_W}
    Bl, S, _ = x.shape
    T = Bl * S

    names = list(SHARDED)
    conv = ("ssd_conv_w", "ffn_conv_w")
    shards = [Wt[k] if k in conv else Wt[k].astype(BF16) for k in names]
    gathered = all_gather_many(shards, name="gather_weights")
    full = {k: _from_owner_major(g8, SHARDED[k]) for k, g8 in zip(names, gathered)}

    pos = positions.astype(F32).reshape(T, 1)
    inv_freq = ROPE_THETA ** (-jnp.arange(0, MLA_ROPE, 2, dtype=F32) / MLA_ROPE)
    invf = jnp.pad(jnp.concatenate([inv_freq, inv_freq]), (PE_LO, LANES - PE_HI)).reshape(1, LANES)

    layers = []
    for l in range(DEPTH):
        layers.append({
            "attn_norm": attn_norm[l].reshape(1, D), "w_in": _perm_w_in(full["w_in"][l]),
            "ssd_conv_w": full["ssd_conv_w"][l], "ssd_conv_b": ssd_conv_b[l].reshape(1, SSD_XBC),
            "ssd_dt_bias": _lane_pad(ssd_dt_bias[l]), "ssd_a_log": _lane_pad(ssd_a_log[l]),
            "ssd_d": jnp.repeat(ssd_d[l].reshape(NPAIR, 2), SSD_P, axis=1), "ssd_norm": ssd_norm[l].reshape(1, SSD_W),
            "pool_w": pool_w[l].astype(BF16), "pool_scale": pool_scale[l].reshape(1, POOL_W),
            "mla_q_norm": mla_q_norm[l].reshape(1, MLA_QR), "mla_w_uq": _perm_w_uq(full["mla_w_uq"][l]),
            "mla_kv_norm": mla_kv_norm[l].reshape(1, MLA_KVR), "mla_w_ukv": _perm_w_ukv(full["mla_w_ukv"][l]),
            "w_out": _perm_w_out(full["w_out"][l]), "ffn_norm": ffn_norm[l].reshape(1, D),
            "ffn_w_up": full["ffn_w_up"][l], "ffn_conv_w": full["ffn_conv_w"][l],
            "ffn_conv_b": ffn_conv_b[l].reshape(1, 2 * DFF), "ffn_w_down": full["ffn_w_down"][l]})

    cs, sn = rope_tables(pos, invf, name="rope_tables")
    xc = x.reshape(T, D)
    saved = []
    for l in range(DEPTH):
        xc, sv = _layer_fwd(l, xc, layers[l], cs, sn, Bl)
        saved.append(sv)
    dx, dxb, g_final, loss_part = final_loss(xc, final_norm.reshape(1, D), loss_target.reshape(T, D), name="final_loss")
    grads = [None] * DEPTH
    for l in reversed(range(DEPTH)):
        dx, dxb, grads[l] = _layer_bwd(l, dx, dxb, layers[l], saved[l], cs, sn, Bl)
    loss = lax.psum(loss_part[0, 0], AXES)

    unperm = {"w_in": _unperm_w_in, "mla_w_uq": _unperm_w_uq, "mla_w_ukv": _unperm_w_ukv, "w_out": _unperm_w_out}
    part = {}
    for k in ALL_W:
        if k == "final_norm":
            part[k] = g_final.reshape(D)
        else:
            shp = list(Wt[k].shape[1:])
            if k in SHARDED:
                shp[SHARDED[k] - 1] *= N_DEV
            part[k] = jnp.stack([unperm.get(k, lambda t: t)(grads[l][k]).reshape(shp) for l in range(DEPTH)])

    xi, yi, ci = _place()
    chip = 2 * xi + yi
    g8 = [_owner_major(part[k], SHARDED[k]) for k in names]
    r1 = exchange_sibling(g8, name="rs_sibling")
    keep = [lax.dynamic_index_in_dim(g.reshape((4, 2) + g.shape[1:]), ci, 1, keepdims=False) for g in g8]
    chip_sum = [add2(kp, r, name="rs_add_" + k) for k, kp, r in zip(names, keep, r1)]
    r2 = exchange_chips(chip_sum, name="rs_chips")
    out_g, out_d, out_m, out_v = {}, {}, {}, {}
    for k, cs, r in zip(names, chip_sum, r2):
        own = lax.dynamic_index_in_dim(cs, chip, 0, keepdims=True)
        parts = jnp.concatenate([own, r], axis=0)
        out_g[k], out_d[k], out_m[k], out_v[k] = adamw(Wt[k], Mo[k], Vo[k], parts, name="adamw_" + k)

    pg, layout = _pack_small(part)
    pw, _ = _pack_small(Wt)
    pm, _ = _pack_small(Mo)
    pv, _ = _pack_small(Vo)
    (pg8,) = all_gather_many([pg], name="gather_small_grads")
    sg, sd, sm, sv_ = adamw(pw, pm, pv, pg8, name="adamw_small")
    for dst, buf in ((out_g, sg), (out_d, sd), (out_m, sm), (out_v, sv_)):
        dst.update(_unpack_small(buf, layout))

    return (loss, dx.reshape(Bl, S, D), *[out_g[k] for k in ALL_W], *[out_d[k] for k in ALL_W],
            *[out_m[k] for k in ALL_W], *[out_v[k] for k in ALL_W])
```

```python
import functools
import math

import jax
import jax.numpy as jnp
from jax import lax
from jax.experimental import pallas as pl
from jax.experimental.pallas import tpu as pltpu

F32, BF16 = jnp.float32, jnp.bfloat16
SDS = jax.ShapeDtypeStruct
MESH = pl.DeviceIdType.MESH
AXES = ("x", "y", "c")
N_DEV = 8

D = 1024
EPS = 1e-6
SSD_HEADS, SSD_P, SSD_W, SSD_G, SSD_N, SSD_K, SSD_L, SSD_XBC = 16, 64, 1024, 2, 128, 4, 128, 1536
POOL_G, POOL_D, POOL_W, POOL_WIN = 4, 128, 512, (2, 4, 8, 16)
MLA_H, MLA_QR, MLA_KVR, MLA_NOPE, MLA_ROPE, MLA_V, MLA_QK = 8, 384, 256, 64, 32, 64, 96
ROPE_THETA = 10000.0
MIX = 2048
DFF, FFN_K = 2816, 3
DEPTH = 2
ADAM_LR, ADAM_B1, ADAM_B2, ADAM_EPS, ADAM_WD, ADAM_STEP = 0.001, 0.9, 0.999, 1e-08, 0.01, 10

Z0, XBC0, U0, CQ0, CKV0, DT0, KPE0, PW = 0, 1024, 2560, 3072, 3584, 3840, 3968, 4096
LANES = 128
YCAT = SSD_W + POOL_W + MLA_H * LANES
NEG = -1e30
VMEM_LIMIT = 56 * 1024 * 1024
MM_ROW_TILE, MM_LANE_TILE, MM_FULL_K = 1024, 1408, 2816


def _tile(n, pref, mult):
    if n <= pref:
        return n
    for d in range(pref, 0, -mult):
        if d % mult == 0 and n % d == 0:
            return d
    return n


def _dg(a, b, ca, cb, prec=None):
    return lax.dot_general(a, b, (((ca,), (cb,)), ((), ())), preferred_element_type=F32, precision=prec)


def _nn(a, b):
    return _dg(a.astype(BF16), b.astype(BF16), 1, 0)


def _nt(a, b):
    return _dg(a.astype(BF16), b.astype(BF16), 1, 1)


def _tn(a, b):
    return _dg(a.astype(BF16), b.astype(BF16), 0, 0)


def _sig(x):
    return jax.nn.sigmoid(x)


def _silu(x):
    return x * _sig(x)


def _dsilu(x):
    s = _sig(x)
    return s * (1.0 + x * (1.0 - s))


def _pc(body, *, name, grid, in_specs, out_specs, out_shape, scratch=()):
    return pl.pallas_call(
        body, name=name, grid=grid, in_specs=in_specs, out_specs=out_specs, out_shape=out_shape,
        scratch_shapes=list(scratch), compiler_params=pltpu.CompilerParams(vmem_limit_bytes=VMEM_LIMIT))


def _rsum(x):
    return jnp.sum(x, axis=1, keepdims=True)


def _csum(x):
    return jnp.sum(x, axis=0, keepdims=True)


def mm(a, b, *, ta=False, tb=False, add=None, out_dtype=F32, b_k0=0, name):
    M, K = (a.shape[1], a.shape[0]) if ta else a.shape
    N = b.shape[0] if tb else b.shape[1]
    assert tb or b_k0 == 0
    tm = _tile(M, MM_LANE_TILE, LANES) if ta else _tile(M, MM_ROW_TILE, 8)
    tn = _tile(N, MM_LANE_TILE, LANES)
    if ta:
        tk = _tile(K, MM_ROW_TILE, 8)
    else:
        tk = K if K <= MM_FULL_K else _tile(K, 2048, LANES)
    nk = K // tk

    def body(*refs):
        if add is None:
            a_ref, b_ref, o_ref = refs[:3]
        else:
            a_ref, b_ref, add_ref, o_ref = refs[:4]
        part = _dg(a_ref[...].astype(BF16), b_ref[...].astype(BF16), 0 if ta else 1, 1 if tb else 0)

        def finish(r):
            if add is not None:
                r = r + add_ref[...].astype(F32)
            o_ref[...] = r.astype(out_dtype)

        if nk == 1:
            finish(part)
            return
        acc = refs[-1]
        k = pl.program_id(2)

        @pl.when(k == 0)
        def _():
            acc[...] = part

        @pl.when(k > 0)
        def _():
            acc[...] += part

        @pl.when(k == nk - 1)
        def _():
            finish(acc[...])

    a_spec = pl.BlockSpec((tk, tm), lambda i, j, k: (k, i)) if ta else pl.BlockSpec((tm, tk), lambda i, j, k: (i, k))
    assert b_k0 % tk == 0
    kb0 = b_k0 // tk
    b_spec = pl.BlockSpec((tn, tk), lambda i, j, k: (j, kb0 + k)) if tb else pl.BlockSpec((tk, tn), lambda i, j, k: (k, j))
    o_spec = pl.BlockSpec((tm, tn), lambda i, j, k: (i, j))
    ins, specs = [a, b], [a_spec, b_spec]
    if add is not None:
        ins.append(add)
        specs.append(o_spec)
    return _pc(body, name=name, grid=(M // tm, N // tn, nk), in_specs=specs, out_specs=o_spec,
               out_shape=SDS((M, N), out_dtype), scratch=[pltpu.VMEM((tm, tn), F32)] if nk > 1 else [])(*ins)


def rms_fwd(x, g, *, col0=0, width=None, name):
    T = x.shape[0]
    W = width or x.shape[1]
    tm = _tile(T, 512, 8)

    def body(x_ref, g_ref, o_ref):
        v = x_ref[...]
        r = lax.rsqrt(jnp.mean(v * v, axis=1, keepdims=True) + EPS)
        o_ref[...] = ((v * r) * g_ref[...]).astype(BF16)

    return _pc(body, name=name, grid=(T // tm,),
               in_specs=[pl.BlockSpec((tm, W), lambda i: (i, col0 // W)), pl.BlockSpec((1, W), lambda i: (0, 0))],
               out_specs=pl.BlockSpec((tm, W), lambda i: (i, 0)), out_shape=SDS((T, W), BF16))(x, g)


def rms_bwd(x, g, dh, *, col0=0, width=None, add=None, name):
    T = x.shape[0]
    W = width or x.shape[1]
    tm = _tile(T, 512, 8)

    def body(*refs):
        if add is None:
            x_ref, g_ref, dh_ref, dx_ref, dg_ref = refs
        else:
            x_ref, g_ref, dh_ref, add_ref, dx_ref, dxb_ref, dg_ref = refs
        v = x_ref[...]
        r = lax.rsqrt(jnp.mean(v * v, axis=1, keepdims=True) + EPS)
        xh = v * r
        d = dh_ref[...].astype(F32)
        dxh = d * g_ref[...]
        dx = r * (dxh - xh * jnp.mean(dxh * xh, axis=1, keepdims=True))
        if add is not None:
            dx = dx + add_ref[...]
            dxb_ref[...] = dx.astype(BF16)
        dx_ref[...] = dx

        @pl.when(pl.program_id(0) == 0)
        def _():
            dg_ref[...] = jnp.zeros_like(dg_ref)

        dg_ref[...] += _csum(d * xh)

    row = pl.BlockSpec((tm, W), lambda i: (i, 0))
    vec = pl.BlockSpec((1, W), lambda i: (0, 0))
    ins = [x, g, dh] + ([] if add is None else [add])
    specs = [pl.BlockSpec((tm, W), lambda i: (i, col0 // W)), vec, row] + ([] if add is None else [row])
    if add is None:
        return _pc(body, name=name, grid=(T // tm,), in_specs=specs, out_specs=[row, vec],
                   out_shape=[SDS((T, W), F32), SDS((1, W), F32)])(*ins)
    return _pc(body, name=name, grid=(T // tm,), in_specs=specs, out_specs=[row, row, vec],
               out_shape=[SDS((T, W), F32), SDS((T, W), BF16), SDS((1, W), F32)])(*ins)


def gated_rms_fwd(y, proj, g, *, name):
    T = y.shape[0]
    tm = _tile(T, 512, 8)

    def body(y_ref, z_ref, g_ref, o_ref):
        v = y_ref[...] * _silu(z_ref[...])
        r = lax.rsqrt(jnp.mean(v * v, axis=1, keepdims=True) + EPS)
        o_ref[...] = ((v * r) * g_ref[...]).astype(BF16)

    row = pl.BlockSpec((tm, SSD_W), lambda i: (i, 0))
    return _pc(body, name=name, grid=(T // tm,), in_specs=[row, row, pl.BlockSpec((1, SSD_W), lambda i: (0, 0))],
               out_specs=row, out_shape=SDS((T, SSD_W), BF16))(y, proj, g)


def gated_rms_bwd(y, proj, g, dycat, *, name):
    T = y.shape[0]
    tm = _tile(T, 512, 8)

    def body(y_ref, z_ref, g_ref, d_ref, dy_ref, dz_ref, dg_ref):
        yv, z = y_ref[...], z_ref[...]
        sz = _silu(z)
        v = yv * sz
        r = lax.rsqrt(jnp.mean(v * v, axis=1, keepdims=True) + EPS)
        vh = v * r
        d = d_ref[...]
        dvh = d * g_ref[...]
        dv = r * (dvh - vh * jnp.mean(dvh * vh, axis=1, keepdims=True))
        dy_ref[...] = dv * sz
        dz_ref[...] = dv * yv * _dsilu(z)

        @pl.when(pl.program_id(0) == 0)
        def _():
            dg_ref[...] = jnp.zeros_like(dg_ref)

        dg_ref[...] += _csum(d * vh)

    row = pl.BlockSpec((tm, SSD_W), lambda i: (i, 0))
    vec = pl.BlockSpec((1, SSD_W), lambda i: (0, 0))
    return _pc(body, name=name, grid=(T // tm,), in_specs=[row, row, vec, row], out_specs=[row, row, vec],
               out_shape=[SDS((T, SSD_W), F32), SDS((T, SSD_W), F32), SDS((1, SSD_W), F32)])(y, proj, g, dycat)


def final_loss(x, g, tgt, *, name):
    T = x.shape[0]
    tm = _tile(T, 512, 8)

    def body(x_ref, g_ref, t_ref, dx_ref, dxb_ref, dg_ref, l_ref):
        v = x_ref[...]
        gg = g_ref[...]
        r = lax.rsqrt(jnp.mean(v * v, axis=1, keepdims=True) + EPS)
        xh = v * r
        err = xh * gg - t_ref[...]
        part = 0.5 * _csum(jnp.mean(err * err, axis=1, keepdims=True))
        d = err * (1.0 / D)
        dxh = d * gg
        dx = r * (dxh - xh * jnp.mean(dxh * xh, axis=1, keepdims=True))
        dx_ref[...] = dx
        dxb_ref[...] = dx.astype(BF16)

        @pl.when(pl.program_id(0) == 0)
        def _():
            dg_ref[...] = jnp.zeros_like(dg_ref)
            l_ref[...] = jnp.zeros_like(l_ref)

        dg_ref[...] += _csum(d * xh)
        l_ref[...] += jnp.broadcast_to(part, (1, LANES))

    row = pl.BlockSpec((tm, D), lambda i: (i, 0))
    vec = pl.BlockSpec((1, D), lambda i: (0, 0))
    return _pc(body, name=name, grid=(T // tm,), in_specs=[row, vec, row],
               out_specs=[row, row, vec, pl.BlockSpec((1, LANES), lambda i: (0, 0))],
               out_shape=[SDS((T, D), F32), SDS((T, D), BF16), SDS((1, D), F32), SDS((1, LANES), F32)])(x, g, tgt)


HALO = 8


def _prev_map(ts, col):
    return lambda b, i, j: (b, jnp.maximum(i * (ts // HALO) - 1, 0), col(j))


def _next_map(ts, n_halo_blocks, col):
    return lambda b, i, j: (b, jnp.minimum((i + 1) * (ts // HALO), n_halo_blocks - 1), col(j))


def conv_silu_fwd(proj3, w, b, *, name):
    Bl, S, _ = proj3.shape
    C, K = SSD_XBC, SSD_K
    ts, tc = _tile(S, 512, 8), 512
    c0 = XBC0 // tc

    def body(xp_ref, x_ref, w_ref, b_ref, o_ref, ext):
        i = pl.program_id(1)
        ext[0:HALO, :] = jnp.where(i > 0, xp_ref[0], 0.0)
        ext[HALO:HALO + ts, :] = x_ref[0]
        acc = b_ref[...] + w_ref[0:1, :] * ext[pl.ds(HALO - (K - 1), ts), :]
        for k in range(1, K):
            acc = acc + w_ref[k:k + 1, :] * ext[pl.ds(HALO - (K - 1) + k, ts), :]
        o_ref[0] = _silu(acc)

    return _pc(body, name=name, grid=(Bl, S // ts, C // tc),
               in_specs=[pl.BlockSpec((1, HALO, tc), _prev_map(ts, lambda j: c0 + j)),
                         pl.BlockSpec((1, ts, tc), lambda b, i, j: (b, i, c0 + j)),
                         pl.BlockSpec((K, tc), lambda b, i, j: (0, j)),
                         pl.BlockSpec((1, tc), lambda b, i, j: (0, j))],
               out_specs=pl.BlockSpec((1, ts, tc), lambda b, i, j: (b, i, j)),
               out_shape=SDS((Bl, S, C), F32), scratch=[pltpu.VMEM((HALO + ts, tc), F32)])(proj3, proj3, w, b)


def conv_silu_bwd(proj3, w, b, dact, *, name):
    Bl, S, _ = proj3.shape
    C, K = SSD_XBC, SSD_K
    ts, tc = _tile(S, 512, 8), 512
    c0 = XBC0 // tc
    ns = S // ts

    def body(xp_ref, x_ref, xn_ref, d_ref, dn_ref, w_ref, b_ref, dx_ref, dw_ref, db_ref, ext, ext2):
        bb, i = pl.program_id(1), pl.program_id(2)
        last = i == ns - 1
        ext[0:HALO, :] = jnp.where(i > 0, xp_ref[0], 0.0)
        ext[HALO:HALO + ts, :] = x_ref[0]
        ext[HALO + ts:2 * HALO + ts, :] = jnp.where(last, 0.0, xn_ref[0])
        acc = b_ref[...] + w_ref[0:1, :] * ext[pl.ds(HALO - (K - 1), ts + HALO), :]
        for k in range(1, K):
            acc = acc + w_ref[k:k + 1, :] * ext[pl.ds(HALO - (K - 1) + k, ts + HALO), :]
        dsl = _dsilu(acc)
        du = d_ref[0] * dsl[0:ts]
        ext2[0:ts, :] = du
        ext2[ts:ts + HALO, :] = jnp.where(last, 0.0, dn_ref[0]) * dsl[ts:ts + HALO]
        dx = w_ref[0:1, :] * ext2[pl.ds(K - 1, ts), :]
        for k in range(1, K):
            dx = dx + w_ref[k:k + 1, :] * ext2[pl.ds(K - 1 - k, ts), :]
        dx_ref[0] = dx

        @pl.when((bb == 0) & (i == 0))
        def _():
            dw_ref[...] = jnp.zeros_like(dw_ref)
            db_ref[...] = jnp.zeros_like(db_ref)

        for k in range(K):
            dw_ref[k:k + 1, :] += _csum(du * ext[pl.ds(HALO - (K - 1) + k, ts), :])
        db_ref[...] += _csum(du)

    nhb = S // HALO
    cx = lambda j: c0 + j
    cj = lambda j: j
    return _pc(body, name=name, grid=(C // tc, Bl, ns),
               in_specs=[pl.BlockSpec((1, HALO, tc), lambda j, b, i: _prev_map(ts, cx)(b, i, j)),
                         pl.BlockSpec((1, ts, tc), lambda j, b, i: (b, i, c0 + j)),
                         pl.BlockSpec((1, HALO, tc), lambda j, b, i: _next_map(ts, nhb, cx)(b, i, j)),
                         pl.BlockSpec((1, ts, tc), lambda j, b, i: (b, i, j)),
                         pl.BlockSpec((1, HALO, tc), lambda j, b, i: _next_map(ts, nhb, cj)(b, i, j)),
                         pl.BlockSpec((K, tc), lambda j, b, i: (0, j)),
                         pl.BlockSpec((1, tc), lambda j, b, i: (0, j))],
               out_specs=[pl.BlockSpec((1, ts, tc), lambda j, b, i: (b, i, j)),
                          pl.BlockSpec((K, tc), lambda j, b, i: (0, j)),
                          pl.BlockSpec((1, tc), lambda j, b, i: (0, j))],
               out_shape=[SDS((Bl, S, C), F32), SDS((K, C), F32), SDS((1, C), F32)],
               scratch=[pltpu.VMEM((2 * HALO + ts, tc), F32), pltpu.VMEM((HALO + ts, tc), F32)],
               )(proj3, proj3, proj3, dact, dact, w, b)


def ffn_act_fwd(pre3, w, b, *, name):
    Bl, S, _ = pre3.shape
    K = FFN_K
    ts, tc = _tile(S, 512, 8), 256
    nj = DFF // tc

    def body(gp_ref, g_ref, vp_ref, v_ref, wg_ref, wv_ref, bg_ref, bv_ref, o_ref, eg, ev):
        i = pl.program_id(1)
        outs = []
        for p_ref, m_ref, w_ref, b_ref, ext in ((gp_ref, g_ref, wg_ref, bg_ref, eg), (vp_ref, v_ref, wv_ref, bv_ref, ev)):
            ext[0:HALO, :] = jnp.where(i > 0, p_ref[0], 0.0)
            ext[HALO:HALO + ts, :] = m_ref[0]
            acc = b_ref[...] + w_ref[0:1, :] * ext[pl.ds(HALO - (K - 1), ts), :]
            for k in range(1, K):
                acc = acc + w_ref[k:k + 1, :] * ext[pl.ds(HALO - (K - 1) + k, ts), :]
            outs.append(acc)
        o_ref[0] = (_silu(outs[0]) * outs[1]).astype(BF16)

    main = lambda off: pl.BlockSpec((1, ts, tc), lambda b, i, j: (b, i, off + j))
    prev = lambda off: pl.BlockSpec((1, HALO, tc), _prev_map(ts, lambda j: off + j))
    wsp = lambda off: pl.BlockSpec((K, tc), lambda b, i, j: (0, off + j))
    bsp = lambda off: pl.BlockSpec((1, tc), lambda b, i, j: (0, off + j))
    return _pc(body, name=name, grid=(Bl, S // ts, nj),
               in_specs=[prev(0), main(0), prev(nj), main(nj), wsp(0), wsp(nj), bsp(0), bsp(nj)],
               out_specs=pl.BlockSpec((1, ts, tc), lambda b, i, j: (b, i, j)),
               out_shape=SDS((Bl, S, DFF), BF16),
               scratch=[pltpu.VMEM((HALO + ts, tc), F32), pltpu.VMEM((HALO + ts, tc), F32)],
               )(pre3, pre3, pre3, pre3, w, w, b, b)


def ffn_act_bwd(pre3, w, b, dact, *, name):
    Bl, S, _ = pre3.shape
    K = FFN_K
    ts, tc = _tile(S, 512, 8), 256
    nj = DFF // tc
    ns = S // ts

    def body(gp_ref, g_ref, gn_ref, vp_ref, v_ref, vn_ref, d_ref, dn_ref, wg_ref, wv_ref, bg_ref, bv_ref,
             dg_ref, dv_ref, dwg_ref, dwv_ref, dbg_ref, dbv_ref, eg, ev, e2g, e2v):
        bb, i = pl.program_id(1), pl.program_id(2)
        last = i == ns - 1
        ups = []
        for p_ref, m_ref, n_ref, w_ref, b_ref, ext in ((gp_ref, g_ref, gn_ref, wg_ref, bg_ref, eg),
                                                       (vp_ref, v_ref, vn_ref, wv_ref, bv_ref, ev)):
            ext[0:HALO, :] = jnp.where(i > 0, p_ref[0], 0.0)
            ext[HALO:HALO + ts, :] = m_ref[0]
            ext[HALO + ts:2 * HALO + ts, :] = jnp.where(last, 0.0, n_ref[0])
            acc = b_ref[...] + w_ref[0:1, :] * ext[pl.ds(HALO - (K - 1), ts + HALO), :]
            for k in range(1, K):
                acc = acc + w_ref[k:k + 1, :] * ext[pl.ds(HALO - (K - 1) + k, ts + HALO), :]
            ups.append(acc)
        ug, uv = ups
        dg_e = uv * _dsilu(ug)
        dv_e = _silu(ug)
        d_main = d_ref[0]
        d_next = jnp.where(last, 0.0, dn_ref[0])
        dug = d_main * dg_e[0:ts]
        duv = d_main * dv_e[0:ts]
        e2g[0:ts, :] = dug
        e2g[ts:ts + HALO, :] = d_next * dg_e[ts:ts + HALO]
        e2v[0:ts, :] = duv
        e2v[ts:ts + HALO, :] = d_next * dv_e[ts:ts + HALO]

        @pl.when((bb == 0) & (i == 0))
        def _():
            for r in (dwg_ref, dwv_ref, dbg_ref, dbv_ref):
                r[...] = jnp.zeros_like(r)

        for w_ref, e2, ext, du, o_ref, dw_ref, db_ref in ((wg_ref, e2g, eg, dug, dg_ref, dwg_ref, dbg_ref),
                                                          (wv_ref, e2v, ev, duv, dv_ref, dwv_ref, dbv_ref)):
            dx = w_ref[0:1, :] * e2[pl.ds(K - 1, ts), :]
            for k in range(1, K):
                dx = dx + w_ref[k:k + 1, :] * e2[pl.ds(K - 1 - k, ts), :]
            o_ref[0] = dx.astype(BF16)
            for k in range(K):
                dw_ref[k:k + 1, :] += _csum(du * ext[pl.ds(HALO - (K - 1) + k, ts), :])
            db_ref[...] += _csum(du)

    nhb = S // HALO
    main = lambda off: pl.BlockSpec((1, ts, tc), lambda j, b, i: (b, i, off + j))
    prev = lambda off: pl.BlockSpec((1, HALO, tc), lambda j, b, i: _prev_map(ts, lambda jj: off + jj)(b, i, j))
    nxt = lambda off: pl.BlockSpec((1, HALO, tc), lambda j, b, i: _next_map(ts, nhb, lambda jj: off + jj)(b, i, j))
    wsp = lambda off: pl.BlockSpec((K, tc), lambda j, b, i: (0, off + j))
    bsp = lambda off: pl.BlockSpec((1, tc), lambda j, b, i: (0, off + j))
    outs = _pc(body, name=name, grid=(nj, Bl, ns),
               in_specs=[prev(0), main(0), nxt(0), prev(nj), main(nj), nxt(nj), main(0), nxt(0),
                         wsp(0), wsp(nj), bsp(0), bsp(nj)],
               out_specs=[main(0), main(0), wsp(0), wsp(0), bsp(0), bsp(0)],
               out_shape=[SDS((Bl, S, DFF), BF16), SDS((Bl, S, DFF), BF16), SDS((K, DFF), F32), SDS((K, DFF), F32),
                          SDS((1, DFF), F32), SDS((1, DFF), F32)],
               scratch=[pltpu.VMEM((2 * HALO + ts, tc), F32), pltpu.VMEM((2 * HALO + ts, tc), F32),
                        pltpu.VMEM((HALO + ts, tc), F32), pltpu.VMEM((HALO + ts, tc), F32)],
               )(pre3, pre3, pre3, pre3, pre3, pre3, dact, dact, w, w, b, b)
    return outs


PHALO = 16


def _pool_window_sums(ext, base, ts, step):
    s = ext[pl.ds(base, ts), :]
    out = []
    for i in range(1, PHALO):
        s = s + ext[pl.ds(base + step * i, ts), :]
        if i + 1 in POOL_WIN:
            out.append(s)
    return out


def _pick(g, vals):
    r = vals[-1]
    for k in range(len(vals) - 2, -1, -1):
        r = jnp.where(g == k, vals[k], r)
    return r


def _pool_count(g, i, ts, rows):
    t = (i * ts + lax.broadcasted_iota(jnp.int32, (rows, 1), 0) + 1).astype(F32)
    return jnp.minimum(t, _pick(g, [float(w) for w in POOL_WIN]))


def _pooled(up_ref, u_ref, ext, g, i, ts):
    ext[0:PHALO, :] = jnp.where(i > 0, up_ref[0], 0.0)
    u = u_ref[0]
    ext[PHALO:PHALO + ts, :] = u
    sums = _pool_window_sums(ext, PHALO, ts, -1)
    return _pick(g, sums) / _pool_count(g, i, ts, ts) - u


def pool_fwd(proj3, pool_w, scale, *, name):
    Bl, S, _ = proj3.shape
    ts = _tile(S, 512, 16)
    c0 = U0 // POOL_D

    def body(up_ref, u_ref, w_ref, s_ref, o_ref, ext):
        i, g = pl.program_id(1), pl.program_id(2)
        pooled = _pooled(up_ref, u_ref, ext, g, i, ts)
        o_ref[0] = (_nn(pooled, w_ref[0]) * s_ref[...]).astype(BF16)

    return _pc(body, name=name, grid=(Bl, S // ts, POOL_G),
               in_specs=[pl.BlockSpec((1, PHALO, POOL_D), lambda b, i, g: (b, jnp.maximum(i * (ts // PHALO) - 1, 0), c0 + g)),
                         pl.BlockSpec((1, ts, POOL_D), lambda b, i, g: (b, i, c0 + g)),
                         pl.BlockSpec((1, POOL_D, POOL_D), lambda b, i, g: (g, 0, 0)),
                         pl.BlockSpec((1, POOL_D), lambda b, i, g: (0, g))],
               out_specs=pl.BlockSpec((1, ts, POOL_D), lambda b, i, g: (b, i, g)),
               out_shape=SDS((Bl, S, POOL_W), BF16), scratch=[pltpu.VMEM((PHALO + ts, POOL_D), F32)],
               )(proj3, proj3, pool_w, scale)


def pool_bwd(proj3, pool_w, scale, dycat3, *, name):
    Bl, S, _ = proj3.shape
    ts = _tile(S, 512, 16)
    ns = S // ts
    c0 = U0 // POOL_D
    d0 = SSD_W // POOL_D
    nhb = S // PHALO

    def body(up_ref, u_ref, d_ref, dn_ref, w_ref, s_ref, du_ref, dw_ref, ds_ref, ext, ext2):
        g, bb, i = pl.program_id(0), pl.program_id(1), pl.program_id(2)
        last = i == ns - 1
        pooled = _pooled(up_ref, u_ref, ext, g, i, ts)
        wm = w_ref[0]
        sc = s_ref[...]
        dy = d_ref[0]
        dp_main = dy * sc
        dpool = _nt(dp_main, wm)
        dpool_n = _nt(jnp.where(last, 0.0, dn_ref[0]) * sc, wm)
        ext2[0:ts, :] = dpool / _pool_count(g, i, ts, ts)
        ext2[ts:ts + PHALO, :] = dpool_n / _pool_count(g, i + 1, ts, PHALO)
        sums = _pool_window_sums(ext2, 0, ts, 1)
        du_ref[0] = _pick(g, sums) - dpool

        @pl.when((bb == 0) & (i == 0))
        def _():
            dw_ref[...] = jnp.zeros_like(dw_ref)
            ds_ref[...] = jnp.zeros_like(ds_ref)

        dw_ref[0] += _tn(pooled, dp_main)
        ds_ref[...] += _csum(dy * _nn(pooled, wm))

    return _pc(body, name=name, grid=(POOL_G, Bl, ns),
               in_specs=[pl.BlockSpec((1, PHALO, POOL_D), lambda g, b, i: (b, jnp.maximum(i * (ts // PHALO) - 1, 0), c0 + g)),
                         pl.BlockSpec((1, ts, POOL_D), lambda g, b, i: (b, i, c0 + g)),
                         pl.BlockSpec((1, ts, POOL_D), lambda g, b, i: (b, i, d0 + g)),
                         pl.BlockSpec((1, PHALO, POOL_D), lambda g, b, i: (b, jnp.minimum((i + 1) * (ts // PHALO), nhb - 1), d0 + g)),
                         pl.BlockSpec((1, POOL_D, POOL_D), lambda g, b, i: (g, 0, 0)),
                         pl.BlockSpec((1, POOL_D), lambda g, b, i: (0, g))],
               out_specs=[pl.BlockSpec((1, ts, POOL_D), lambda g, b, i: (b, i, g)),
                          pl.BlockSpec((1, POOL_D, POOL_D), lambda g, b, i: (g, 0, 0)),
                          pl.BlockSpec((1, POOL_D), lambda g, b, i: (0, g))],
               out_shape=[SDS((Bl, S, POOL_W), F32), SDS((POOL_G, POOL_D, POOL_D), F32), SDS((1, POOL_W), F32)],
               scratch=[pltpu.VMEM((PHALO + ts, POOL_D), F32), pltpu.VMEM((PHALO + ts, POOL_D), F32)],
               )(proj3, proj3, dycat3, dycat3, pool_w, scale)


NPAIR = SSD_HEADS // 2


def _ssd_common(sm, bias, alog):
    L = SSD_L
    dt = jax.nn.softplus(sm + bias)
    a = -jnp.exp(alog)
    da = dt * a
    r = lax.broadcasted_iota(jnp.int32, (L, L), 0)
    c = lax.broadcasted_iota(jnp.int32, (L, L), 1)
    tri = (r >= c).astype(F32)
    cum = _dg(tri, da, 1, 0, lax.Precision.HIGHEST)
    return dt, a, cum, cum.T, r >= c


def _lanes(lo, hi, shape=(1, LANES)):
    lane = lax.broadcasted_iota(jnp.int32, shape, len(shape) - 1)
    return (lane >= lo) & (lane < hi)


def _onehot_lane(h):
    return (lax.broadcasted_iota(jnp.int32, (1, LANES), 1) == h).astype(F32)


def ssd_fwd(xbc3, proj3, bias, alog, dskip, *, name):
    Bl, S, _ = xbc3.shape
    L = SSD_L
    nc = S // L

    def body(xbc_ref, sm_ref, bias_ref, alog_ref, d_ref, y_ref, hin_ref, H):
        c = pl.program_id(1)

        @pl.when(c == 0)
        def _():
            H[...] = jnp.zeros_like(H)

        dt, a, cum, cumT, mask = _ssd_common(sm_ref[0], bias_ref[...], alog_ref[...])
        lo = _lanes(0, SSD_P)
        rowlo = lax.broadcasted_iota(jnp.int32, (LANES, LANES), 0) < SSD_P
        cb = []
        for g in range(SSD_G):
            Bg = xbc_ref[0, :, SSD_W + g * SSD_N:SSD_W + (g + 1) * SSD_N]
            Cg = xbc_ref[0, :, SSD_W + SSD_G * SSD_N + g * SSD_N:SSD_W + SSD_G * SSD_N + (g + 1) * SSD_N]
            cb.append((Bg, Cg, _nt(Cg, Bg)))
        for j in range(NPAIR):
            h0, h1 = 2 * j, 2 * j + 1
            Bg, Cg, CB = cb[j // (NPAIR // SSD_G)]
            X = xbc_ref[0, :, j * LANES:(j + 1) * LANES]
            c0, c1 = cum[:, h0:h0 + 1], cum[:, h1:h1 + 1]
            r0, r1 = cumT[h0:h0 + 1, :], cumT[h1:h1 + 1, :]
            cl0, cl1 = cum[L - 1:L, h0:h0 + 1], cum[L - 1:L, h1:h1 + 1]
            Xt = X * jnp.where(lo, dt[:, h0:h0 + 1], dt[:, h1:h1 + 1])
            M0 = CB * jnp.exp(jnp.where(mask, c0 - r0, NEG))
            M1 = CB * jnp.exp(jnp.where(mask, c1 - r1, NEG))
            Yd = jnp.where(lo, _nn(M0, Xt), _nn(M1, Xt))
            Hp = H[j]
            hin_ref[0, 0, j] = Hp
            Z = _nt(Cg, Hp)
            y_ref[0, :, j * LANES:(j + 1) * LANES] = Yd + jnp.where(lo, jnp.exp(c0), jnp.exp(c1)) * Z + X * d_ref[j:j + 1, :]
            wl = jnp.where(lo, jnp.exp(cl0 - c0), jnp.exp(cl1 - c1))
            H[j] = jnp.where(rowlo, jnp.exp(cl0), jnp.exp(cl1)) * Hp + _tn(wl * Xt, Bg)

    vec = pl.BlockSpec((1, LANES), lambda b, c: (0, 0))
    return _pc(body, name=name, grid=(Bl, nc),
               in_specs=[pl.BlockSpec((1, L, SSD_XBC), lambda b, c: (b, c, 0)),
                         pl.BlockSpec((1, L, LANES), lambda b, c: (b, c, DT0 // LANES)),
                         vec, vec, pl.BlockSpec((NPAIR, LANES), lambda b, c: (0, 0))],
               out_specs=[pl.BlockSpec((1, L, SSD_W), lambda b, c: (b, c, 0)),
                          pl.BlockSpec((1, 1, NPAIR, LANES, LANES), lambda b, c: (b, c, 0, 0, 0))],
               out_shape=[SDS((Bl, S, SSD_W), F32), SDS((Bl, nc, NPAIR, LANES, LANES), F32)],
               scratch=[pltpu.VMEM((NPAIR, LANES, LANES), F32)])(xbc3, proj3, bias, alog, dskip)


def ssd_bwd(xbc3, proj3, hin, dy3, bias, alog, dskip, *, name):
    Bl, S, _ = xbc3.shape
    L = SSD_L
    nc = S // L

    def body(xbc_ref, sm_ref, hin_ref, dy_ref, bias_ref, alog_ref, d_ref, dx_ref, ddt_ref, dpar_ref, dd_ref, dH, ddacc):
        bb, i = pl.program_id(0), pl.program_id(1)

        @pl.when(i == 0)
        def _():
            dH[...] = jnp.zeros_like(dH)

        @pl.when((bb == 0) & (i == 0))
        def _():
            dpar_ref[...] = jnp.zeros_like(dpar_ref)
            ddacc[...] = jnp.zeros_like(ddacc)

        sm = sm_ref[0]
        dt, a, cum, cumT, mask = _ssd_common(sm, bias_ref[...], alog_ref[...])
        maskf = mask.astype(F32)
        lo = _lanes(0, SSD_P)
        rowlo = lax.broadcasted_iota(jnp.int32, (LANES, LANES), 0) < SSD_P
        lastrow = (lax.broadcasted_iota(jnp.int32, (L, 1), 0) == L - 1).astype(F32)
        dcum = jnp.zeros((L, LANES), F32)
        ddt = jnp.zeros((L, LANES), F32)
        grp = []
        for g in range(SSD_G):
            Bg = xbc_ref[0, :, SSD_W + g * SSD_N:SSD_W + (g + 1) * SSD_N]
            Cg = xbc_ref[0, :, SSD_W + SSD_G * SSD_N + g * SSD_N:SSD_W + SSD_G * SSD_N + (g + 1) * SSD_N]
            grp.append(dict(B=Bg, C=Cg, CB=_nt(Cg, Bg), dB=jnp.zeros((L, SSD_N), F32), dC=jnp.zeros((L, SSD_N), F32),
                            dCB=jnp.zeros((L, L), F32)))
        for j in range(NPAIR):
            h0, h1 = 2 * j, 2 * j + 1
            G = grp[j // (NPAIR // SSD_G)]
            Bg, Cg, CB = G["B"], G["C"], G["CB"]
            X = xbc_ref[0, :, j * LANES:(j + 1) * LANES]
            dY = dy_ref[0, :, j * LANES:(j + 1) * LANES]
            c0, c1 = cum[:, h0:h0 + 1], cum[:, h1:h1 + 1]
            r0, r1 = cumT[h0:h0 + 1, :], cumT[h1:h1 + 1, :]
            cl0, cl1 = cum[L - 1:L, h0:h0 + 1], cum[L - 1:L, h1:h1 + 1]
            oh0, oh1 = _onehot_lane(h0), _onehot_lane(h1)
            dtl = jnp.where(lo, dt[:, h0:h0 + 1], dt[:, h1:h1 + 1])
            Xt = X * dtl
            e0, e1 = jnp.exp(c0), jnp.exp(c1)
            el = jnp.where(lo, e0, e1)
            w0, w1 = jnp.exp(cl0 - c0), jnp.exp(cl1 - c1)
            wl = jnp.where(lo, w0, w1)
            Hp = hin_ref[0, 0, j]
            dS = dH[j]
            dX = dY * d_ref[j:j + 1, :]
            ddacc[j:j + 1, :] += _csum(dY * X)
            Z = _nt(Cg, Hp)
            dZ = dY * el
            t = dY * Z
            dcum = dcum + (_rsum(jnp.where(lo, t, 0.0)) * e0) * oh0 + (_rsum(jnp.where(lo, 0.0, t)) * e1) * oh1
            G["dC"] = G["dC"] + _nn(dZ, Hp)
            dHy = _tn(dZ, Cg)
            Gm = _nt(Bg, dS)
            dXt = wl * Gm
            u = Xt * Gm
            q0 = _rsum(jnp.where(lo, u, 0.0)) * w0
            q1 = _rsum(jnp.where(lo, 0.0, u)) * w1
            dcum = dcum + (lastrow * _csum(q0) - q0) * oh0 + (lastrow * _csum(q1) - q1) * oh1
            G["dB"] = G["dB"] + _nn(wl * Xt, dS)
            g0, g1 = jnp.exp(cl0), jnp.exp(cl1)
            prod = dS * Hp
            dg0 = _csum(_rsum(jnp.where(rowlo, prod, 0.0)))
            dg1 = _csum(_rsum(jnp.where(rowlo, 0.0, prod)))
            dcum = dcum + lastrow * ((dg0 * g0) * oh0 + (dg1 * g1) * oh1)
            dH[j] = jnp.where(rowlo, g0, g1) * dS + dHy
            for ch, rh, mh, oh in ((c0, r0, lo, oh0), (c1, r1, jnp.logical_not(lo), oh1)):
                decay = jnp.exp(jnp.where(mask, ch - rh, NEG))
                Mh = CB * decay
                dM = _nt(jnp.where(mh, dY, 0.0), Xt) * maskf
                dXt = dXt + jnp.where(mh, _tn(Mh, dY), 0.0)
                G["dCB"] = G["dCB"] + dM * decay
                Q = dM * Mh
                dcum = dcum + (_rsum(Q) - _rsum(Q.T)) * oh
            dX = dX + dXt * dtl
            s = dXt * X
            ddt = ddt + _rsum(jnp.where(lo, s, 0.0)) * oh0 + _rsum(jnp.where(lo, 0.0, s)) * oh1
            dx_ref[0, :, j * LANES:(j + 1) * LANES] = dX
        for g in range(SSD_G):
            G = grp[g]
            dC = G["dC"] + _nn(G["dCB"], G["B"])
            dB = G["dB"] + _tn(G["dCB"], G["C"])
            dx_ref[0, :, SSD_W + g * SSD_N:SSD_W + (g + 1) * SSD_N] = dB
            dx_ref[0, :, SSD_W + SSD_G * SSD_N + g * SSD_N:SSD_W + SSD_G * SSD_N + (g + 1) * SSD_N] = dC
        r = lax.broadcasted_iota(jnp.int32, (L, L), 0)
        c = lax.broadcasted_iota(jnp.int32, (L, L), 1)
        dda = _dg((c >= r).astype(F32), dcum, 1, 0, lax.Precision.HIGHEST)
        heads = _lanes(0, SSD_HEADS)
        ddt = ddt + dda * a
        draw = jnp.where(heads, ddt * _sig(sm + bias_ref[...]), 0.0)
        ddt_ref[0] = draw
        dpar_ref[0:1, :] += _csum(draw)
        dpar_ref[1:2, :] += _csum(jnp.where(heads, dda * dt * a, 0.0))

        @pl.when((bb == Bl - 1) & (i == nc - 1))
        def _():
            acc = ddacc[...]
            lane = lax.broadcasted_iota(jnp.int32, (NPAIR, LANES), 1)
            s0 = _rsum(jnp.where(lane < SSD_P, acc, 0.0))
            s1 = _rsum(jnp.where(lane < SSD_P, 0.0, acc))
            dd_ref[...] = jnp.where(lane == 0, s0, jnp.where(lane == 1, s1, 0.0))

    vec = pl.BlockSpec((1, LANES), lambda b, i: (0, 0))
    par = pl.BlockSpec((NPAIR, LANES), lambda b, i: (0, 0))
    return _pc(body, name=name, grid=(Bl, nc),
               in_specs=[pl.BlockSpec((1, L, SSD_XBC), lambda b, i: (b, nc - 1 - i, 0)),
                         pl.BlockSpec((1, L, LANES), lambda b, i: (b, nc - 1 - i, DT0 // LANES)),
                         pl.BlockSpec((1, 1, NPAIR, LANES, LANES), lambda b, i: (b, nc - 1 - i, 0, 0, 0)),
                         pl.BlockSpec((1, L, SSD_W), lambda b, i: (b, nc - 1 - i, 0)),
                         vec, vec, par],
               out_specs=[pl.BlockSpec((1, L, SSD_XBC), lambda b, i: (b, nc - 1 - i, 0)),
                          pl.BlockSpec((1, L, LANES), lambda b, i: (b, nc - 1 - i, 0)),
                          par, par],
               out_shape=[SDS((Bl, S, SSD_XBC), F32), SDS((Bl, S, LANES), F32), SDS((NPAIR, LANES), F32),
                          SDS((NPAIR, LANES), F32)],
               scratch=[pltpu.VMEM((NPAIR, LANES, LANES), F32), pltpu.VMEM((NPAIR, LANES), F32)],
               )(xbc3, proj3, hin, dy3, bias, alog, dskip)


PE_LO, PE_MID, PE_HI = MLA_NOPE, MLA_NOPE + MLA_ROPE // 2, MLA_NOPE + MLA_ROPE
ATT_SCALE = 1.0 / math.sqrt(MLA_QK)


def _swap(x):
    first, second = _lanes(PE_LO, PE_MID), _lanes(PE_MID, PE_HI)
    half = MLA_ROPE // 2
    return jnp.where(first, -pltpu.roll(x, LANES - half, 1), jnp.where(second, pltpu.roll(x, half, 1), 0.0))


def rope_tables(pos, invf, *, name):
    T = pos.shape[0]
    tm = _tile(T, 512, 8)

    def body(pos_ref, f_ref, c_ref, s_ref):
        ang = pos_ref[...] * f_ref[...]
        pe = _lanes(PE_LO, PE_HI)
        c_ref[...] = jnp.where(pe, jnp.cos(ang), 1.0)
        s_ref[...] = jnp.where(pe, jnp.sin(ang), 0.0)

    tile = pl.BlockSpec((tm, LANES), lambda i: (i, 0))
    return _pc(body, name=name, grid=(T // tm,),
               in_specs=[pl.BlockSpec((tm, 1), lambda i: (i, 0)), pl.BlockSpec((1, LANES), lambda i: (0, 0))],
               out_specs=[tile, tile], out_shape=[SDS((T, LANES), F32)] * 2)(pos, invf)


V_ONE = MLA_V


def mla_prep_fwd(qt, kvt, proj, cs, sn, *, name):
    T = qt.shape[0]
    tm = _tile(T, 512, 8)

    def body(q_ref, k_ref, v_ref, kpe_ref, c_ref, s_ref, qo_ref, ko_ref, vo_ref):
        c, s = c_ref[...], s_ref[...]
        q = q_ref[...]
        k = k_ref[...] + kpe_ref[...]
        qo_ref[...] = ((q * c + _swap(q) * s) * ATT_SCALE).astype(BF16)
        ko_ref[...] = (k * c + _swap(k) * s).astype(BF16)
        vo_ref[...] = jnp.where(_lanes(V_ONE, V_ONE + 1), 1.0, v_ref[...]).astype(BF16)

    tile = pl.BlockSpec((tm, LANES), lambda i, h: (i, h))
    tab = pl.BlockSpec((tm, LANES), lambda i, h: (i, 0))
    return _pc(body, name=name, grid=(T // tm, MLA_H),
               in_specs=[tile, tile, pl.BlockSpec((tm, LANES), lambda i, h: (i, MLA_H + h)),
                         pl.BlockSpec((tm, LANES), lambda i, h: (i, KPE0 // LANES)), tab, tab],
               out_specs=[tile, tile, tile], out_shape=[SDS((T, MLA_H * LANES), BF16)] * 3)(qt, kvt, kvt, proj, cs, sn)


def mla_prep_bwd(dqr, dkr, cs, sn, *, name):
    T = dqr.shape[0]
    tm = _tile(T, 512, 8)

    def body(dq_ref, dk_ref, c_ref, s_ref, qo_ref, ko_ref, kpe_ref):
        c, s = c_ref[...], s_ref[...]
        dq = dq_ref[...] * ATT_SCALE
        dk = dk_ref[...]
        qo_ref[...] = (dq * c - _swap(dq * s)).astype(BF16)
        dkk = dk * c - _swap(dk * s)
        pe = _lanes(PE_LO, PE_HI)
        ko_ref[...] = jnp.where(pe, 0.0, dkk).astype(BF16)

        @pl.when(pl.program_id(1) == 0)
        def _():
            kpe_ref[...] = jnp.zeros_like(kpe_ref)

        kpe_ref[...] += jnp.where(pe, dkk, 0.0)

    tile = pl.BlockSpec((tm, LANES), lambda i, h: (i, h))
    tab = pl.BlockSpec((tm, LANES), lambda i, h: (i, 0))
    return _pc(body, name=name, grid=(T // tm, MLA_H), in_specs=[tile, tile, tab, tab], out_specs=[tile, tile, tab],
               out_shape=[SDS((T, MLA_H * LANES), BF16), SDS((T, MLA_H * LANES), BF16), SDS((T, LANES), F32)],
               )(dqr, dkr, cs, sn)


def _att_tile(S):
    return _tile(S, 512, LANES)


def _rep(x, n):
    return x if n == 1 else jnp.concatenate([x] * n, axis=1)


def _diag_mask(t, transposed=False):
    r = lax.broadcasted_iota(jnp.int32, (t, t), 0)
    c = lax.broadcasted_iota(jnp.int32, (t, t), 1)
    return (c >= r) if transposed else (c <= r)


def flash_fwd(qr, kr, vr, Bl, *, name):
    T = qr.shape[0]
    S = T // Bl
    t = _att_tile(S)
    n = S // t
    nl = t // LANES

    def body(q_ref, k_ref, v_ref, o_ref, lse_ref, lset_ref, m, acc):
        qi = pl.program_id(2)
        q = q_ref[...]
        m[...] = jnp.full_like(m, NEG)
        acc[...] = jnp.zeros_like(acc)

        def block(kj, masked):
            off = pl.multiple_of(kj * t, t)
            s = _nt(q, k_ref[pl.ds(off, t), :])
            if masked:
                s = jnp.where(_diag_mask(t), s, NEG)
            mo = m[...]
            mn = jnp.maximum(mo, jnp.max(s, axis=1, keepdims=True))
            p = jnp.exp((s - _rep(mn, nl)).astype(BF16))
            acc[...] = jnp.exp(mo - mn) * acc[...] + _nn(p, v_ref[pl.ds(off, t), :])
            m[...] = mn

        def loop(kj, c):
            block(kj, False)
            return c

        lax.fori_loop(0, qi, loop, 0)
        block(qi, True)
        a = acc[...]
        l = a[:, V_ONE:V_ONE + 1]
        o_ref[...] = jnp.where(_lanes(0, MLA_V), a / l, 0.0).astype(BF16)
        lse = m[...] + jnp.log(l)
        lse_ref[...] = lse
        lset_ref[...] = lse.T[0:8, :]

    qs = pl.BlockSpec((t, LANES), lambda b, h, qi: (b * n + qi, h))
    seq = pl.BlockSpec((S, LANES), lambda b, h, qi: (b, h))
    return _pc(body, name=name, grid=(Bl, MLA_H, n), in_specs=[qs, seq, seq],
               out_specs=[qs, qs, pl.BlockSpec((8, t), lambda b, h, qi: (b * MLA_H + h, qi))],
               out_shape=[SDS((T, MLA_H * LANES), BF16), SDS((T, MLA_H * LANES), F32), SDS((Bl * MLA_H * 8, S), F32)],
               scratch=[pltpu.VMEM((t, LANES), F32), pltpu.VMEM((t, LANES), F32)])(qr, kr, vr)


def flash_bwd_dq(qr, kr, vr, o, lse, dycat, Bl, *, name):
    T = qr.shape[0]
    S = T // Bl
    t = _att_tile(S)
    n = S // t
    nl = t // LANES
    do0 = (SSD_W + POOL_W) // LANES

    def body(q_ref, k_ref, v_ref, o_ref, lse_ref, do_ref, dq_ref, dlt_ref, acc, dl):
        qi = pl.program_id(2)
        q = q_ref[...]
        do = do_ref[...]
        dob = do.astype(BF16)
        dl[...] = jnp.broadcast_to(_rsum(do * o_ref[...].astype(F32)), (t, LANES))
        acc[...] = jnp.zeros_like(acc)

        def block(kj, masked):
            off = pl.multiple_of(kj * t, t)
            k = k_ref[pl.ds(off, t), :]
            s = _nt(q, k)
            if masked:
                s = jnp.where(_diag_mask(t), s, NEG)
            p = jnp.exp((s - _rep(lse_ref[...], nl)).astype(BF16))
            dp = _nt(dob, v_ref[pl.ds(off, t), :])
            acc[...] += _nn(p * (dp - _rep(dl[...], nl)), k)

        def loop(kj, c):
            block(kj, False)
            return c

        lax.fori_loop(0, qi, loop, 0)
        block(qi, True)
        dq_ref[...] = acc[...]
        dlt_ref[...] = dl[...].T[0:8, :]

    qs = pl.BlockSpec((t, LANES), lambda b, h, qi: (b * n + qi, h))
    seq = pl.BlockSpec((S, LANES), lambda b, h, qi: (b, h))
    return _pc(body, name=name, grid=(Bl, MLA_H, n),
               in_specs=[qs, seq, seq, qs, qs, pl.BlockSpec((t, LANES), lambda b, h, qi: (b * n + qi, do0 + h))],
               out_specs=[qs, pl.BlockSpec((8, t), lambda b, h, qi: (b * MLA_H + h, qi))],
               out_shape=[SDS((T, MLA_H * LANES), F32), SDS((Bl * MLA_H * 8, S), F32)],
               scratch=[pltpu.VMEM((t, LANES), F32), pltpu.VMEM((t, LANES), F32)])(qr, kr, vr, o, lse, dycat)


def flash_bwd_dkv(qr, kr, vr, lset, dlt, dycat, Bl, *, name):
    T = qr.shape[0]
    S = T // Bl
    t = _att_tile(S)
    n = S // t
    do0 = (SSD_W + POOL_W) // LANES

    def body(q_ref, k_ref, v_ref, lset_ref, dlt_ref, do_ref, dk_ref, dv_ref, dka, dva):
        kj = pl.program_id(2)
        k = k_ref[...]
        v = v_ref[...]
        dka[...] = jnp.zeros_like(dka)
        dva[...] = jnp.zeros_like(dva)

        def block(qi, masked):
            off = pl.multiple_of(qi * t, t)
            q = q_ref[pl.ds(off, t), :]
            do = do_ref[pl.ds(off, t), :].astype(BF16)
            st = _nt(k, q)
            if masked:
                st = jnp.where(_diag_mask(t, True), st, NEG)
            pt = jnp.exp((st - lset_ref[0:1, pl.ds(off, t)]).astype(BF16))
            dst = pt * (_nt(v, do) - dlt_ref[0:1, pl.ds(off, t)])
            dva[...] += _nn(pt, do)
            dka[...] += _nn(dst, q)

        def loop(qi, c):
            block(qi, False)
            return c

        block(kj, True)
        lax.fori_loop(kj + 1, n, loop, 0)
        dk_ref[...] = dka[...]
        dv_ref[...] = dva[...].astype(BF16)

    ks = pl.BlockSpec((t, LANES), lambda b, h, kj: (b * n + kj, h))
    seq = pl.BlockSpec((S, LANES), lambda b, h, kj: (b, h))
    rows = pl.BlockSpec((8, S), lambda b, h, kj: (b * MLA_H + h, 0))
    return _pc(body, name=name, grid=(Bl, MLA_H, n),
               in_specs=[seq, ks, ks, rows, rows, pl.BlockSpec((S, LANES), lambda b, h, kj: (b, do0 + h))],
               out_specs=[ks, ks], out_shape=[SDS((T, MLA_H * LANES), F32), SDS((T, MLA_H * LANES), BF16)],
               scratch=[pltpu.VMEM((t, LANES), F32), pltpu.VMEM((t, LANES), F32)])(qr, kr, vr, lset, dlt, dycat)


def _rows2d(a):
    return a.reshape(-1, a.shape[-1])


def add2(a, b, *, name):
    shp = a.shape
    a2, b2 = _rows2d(a), _rows2d(b)
    R, C = a2.shape
    tm = _tile(R, 512, 8)

    def body(a_ref, b_ref, o_ref, ob_ref):
        s = a_ref[...] + b_ref[...]
        o_ref[...] = s
        ob_ref[...] = s.astype(BF16)

    blk = pl.BlockSpec((tm, C), lambda i: (i, 0))
    o, ob = _pc(body, name=name, grid=(R // tm,), in_specs=[blk, blk], out_specs=[blk, blk],
                out_shape=[SDS((R, C), F32), SDS((R, C), BF16)])(a2, b2)
    return o.reshape(shp), ob.reshape(shp)


def adamw(w, m, v, parts, *, name):
    shp = w.shape
    w2, m2, v2 = _rows2d(w), _rows2d(m), _rows2d(v)
    R, C = w2.shape
    p3 = [p.reshape(p.shape[0], R, C) for p in parts]
    tm = _tile(R, 256, 8)
    bc1 = 1.0 - ADAM_B1 ** ADAM_STEP
    bc2 = 1.0 - ADAM_B2 ** ADAM_STEP

    def body(w_ref, m_ref, v_ref, *refs):
        p_refs, (g_ref, d_ref, nm_ref, nv_ref) = refs[:len(p3)], refs[len(p3):]
        g = None
        for p_ref, p in zip(p_refs, p3):
            for k in range(p.shape[0]):
                term = p_ref[k].astype(F32)
                g = term if g is None else g + term
        mm_ = ADAM_B1 * m_ref[...] + (1.0 - ADAM_B1) * g
        vv = ADAM_B2 * v_ref[...] + (1.0 - ADAM_B2) * (g * g)
        g_ref[...] = g
        nm_ref[...] = mm_
        nv_ref[...] = vv
        d_ref[...] = -ADAM_LR * ((mm_ / bc1) / (jnp.sqrt(vv / bc2) + ADAM_EPS) + ADAM_WD * w_ref[...])

    blk = pl.BlockSpec((tm, C), lambda i: (i, 0))
    pspecs = [pl.BlockSpec((p.shape[0], tm, C), lambda i: (0, i, 0)) for p in p3]
    outs = _pc(body, name=name, grid=(R // tm,), in_specs=[blk, blk, blk] + pspecs,
               out_specs=[blk] * 4, out_shape=[SDS((R, C), F32)] * 4)(w2, m2, v2, *p3)
    return [o.reshape(shp) for o in outs]


ANY = pl.BlockSpec(memory_space=pl.ANY)


def _place():
    return lax.axis_index("x"), lax.axis_index("y"), lax.axis_index("c")


def all_gather_many(xs, *, name):
    n = len(xs)

    def body(*refs):
        x_refs, o_refs = refs[:n], refs[n:2 * n]
        send_sems, recv_sems, local_sems = refs[2 * n:]
        x, y, c = _place()
        me, sibling = (x, y, c), (x, y, 1 - c)
        chips = [(1 - x, y), (x, 1 - y), (1 - x, 1 - y)]

        def rows(a, p):
            return o_refs[a].at[4 * p[0] + 2 * p[1] + p[2]]

        def copy(a, k, block, to, src=None):
            return pltpu.make_async_remote_copy(
                src_ref=rows(a, block) if src is None else src, dst_ref=rows(a, block),
                send_sem=send_sems.at[7 * a + k], recv_sem=recv_sems.at[7 * a + k], device_id=to, device_id_type=MESH)

        mine = [pltpu.make_async_copy(x_refs[a], rows(a, me), local_sems.at[a]) for a in range(n)]
        for cp in mine:
            cp.start()
        first = []
        for a in range(n):
            first.append(copy(a, 0, me, sibling, src=x_refs[a]))
            first += [copy(a, 1 + j, me, (*chip, c), src=x_refs[a]) for j, chip in enumerate(chips)]
        for cp in first:
            cp.start()
        passed = []
        for j, chip in enumerate(chips):
            for a in range(n):
                copy(a, 1 + j, (*chip, c), me).wait_recv()
                cp = copy(a, 4 + j, (*chip, c), sibling)
                cp.start()
                passed.append(cp)
        for a in range(n):
            copy(a, 0, sibling, me).wait_recv()
            for j, chip in enumerate(chips):
                copy(a, 4 + j, (*chip, 1 - c), me).wait_recv()
        for cp in first + passed:
            cp.wait_send()
        for cp in mine:
            cp.wait()

    return pl.pallas_call(
        body, name=name, in_specs=[ANY] * n, out_specs=[ANY] * n,
        out_shape=[SDS((N_DEV,) + a.shape, a.dtype) for a in xs],
        scratch_shapes=[pltpu.SemaphoreType.DMA((7 * n,)), pltpu.SemaphoreType.DMA((7 * n,)), pltpu.SemaphoreType.DMA((n,))],
    )(*xs)


def exchange_sibling(gs, *, name):
    n = len(gs)

    def body(*refs):
        g_refs, o_refs = refs[:n], refs[n:2 * n]
        send_sems, recv_sems = refs[2 * n:]
        x, y, c = _place()
        cps = []
        for a in range(n):
            for k in range(4):
                cps.append(pltpu.make_async_remote_copy(
                    src_ref=g_refs[a].at[2 * k + (1 - c)], dst_ref=o_refs[a].at[k],
                    send_sem=send_sems.at[4 * a + k], recv_sem=recv_sems.at[4 * a + k], device_id=(x, y, 1 - c), device_id_type=MESH))
        for cp in cps:
            cp.start()
        for cp in cps:
            cp.wait()

    return pl.pallas_call(
        body, name=name, in_specs=[ANY] * n, out_specs=[ANY] * n,
        out_shape=[SDS((4,) + g.shape[1:], g.dtype) for g in gs],
        scratch_shapes=[pltpu.SemaphoreType.DMA((4 * n,)), pltpu.SemaphoreType.DMA((4 * n,))],
    )(*gs)


def exchange_chips(As, *, name):
    n = len(As)

    def body(*refs):
        a_refs, o_refs = refs[:n], refs[n:2 * n]
        send_sems, recv_sems = refs[2 * n:]
        x, y, c = _place()
        chips = [(1 - x, y), (x, 1 - y), (1 - x, 1 - y)]
        cps = []
        for a in range(n):
            for j, chip in enumerate(chips):
                cps.append(pltpu.make_async_remote_copy(
                    src_ref=a_refs[a].at[2 * chip[0] + chip[1]], dst_ref=o_refs[a].at[j],
                    send_sem=send_sems.at[3 * a + j], recv_sem=recv_sems.at[3 * a + j], device_id=(*chip, c), device_id_type=MESH))
        for cp in cps:
            cp.start()
        for cp in cps:
            cp.wait()

    return pl.pallas_call(
        body, name=name, in_specs=[ANY] * n, out_specs=[ANY] * n,
        out_shape=[SDS((3,) + a.shape[1:], a.dtype) for a in As],
        scratch_shapes=[pltpu.SemaphoreType.DMA((3 * n,)), pltpu.SemaphoreType.DMA((3 * n,))],
    )(*As)


def _owner_major(full, axis):
    shp = full.shape
    r = full.reshape(shp[:axis] + (N_DEV, shp[axis] // N_DEV) + shp[axis + 1:])
    return jnp.moveaxis(r, axis, 0)


def _from_owner_major(g8, axis):
    r = jnp.moveaxis(g8, 0, axis)
    shp = r.shape
    return r.reshape(shp[:axis] + (shp[axis] * shp[axis + 1],) + shp[axis + 2:])


def _perm_w_in(w):
    z = jnp.zeros((w.shape[0], LANES), w.dtype)
    dt = jnp.pad(w[:, 2560:2576], ((0, 0), (0, LANES - SSD_HEADS)))
    kpe = jnp.pad(w[:, 3728:3760], ((0, 0), (PE_LO, LANES - PE_HI)))
    return jnp.concatenate([w[:, 0:1024], w[:, 1024:2560], w[:, 2576:3088], w[:, 3088:3472], z, w[:, 3472:3728], dt, kpe], axis=1)


def _unperm_w_in(g):
    return jnp.concatenate([g[:, Z0:Z0 + 1024], g[:, XBC0:XBC0 + 1536], g[:, DT0:DT0 + SSD_HEADS], g[:, U0:U0 + 512],
                            g[:, CQ0:CQ0 + 384], g[:, CKV0:CKV0 + 256], g[:, KPE0 + PE_LO:KPE0 + PE_HI]], axis=1)


def _perm_w_uq(w):
    return jnp.pad(w.reshape(MLA_QR, MLA_H, MLA_QK), ((0, 0), (0, 0), (0, LANES - MLA_QK))).reshape(MLA_QR, MLA_H * LANES)


def _unperm_w_uq(g):
    return g.reshape(MLA_QR, MLA_H, LANES)[:, :, :MLA_QK].reshape(MLA_QR, MLA_H * MLA_QK)


def _perm_w_ukv(w):
    w3 = w.reshape(MLA_KVR, MLA_H, MLA_NOPE + MLA_V)
    pad = ((0, 0), (0, 0), (0, LANES - MLA_NOPE))
    k = jnp.pad(w3[:, :, :MLA_NOPE], pad).reshape(MLA_KVR, MLA_H * LANES)
    v = jnp.pad(w3[:, :, MLA_NOPE:], pad).reshape(MLA_KVR, MLA_H * LANES)
    return jnp.concatenate([k, v], axis=1)


def _unperm_w_ukv(g):
    k = g[:, :MLA_H * LANES].reshape(MLA_KVR, MLA_H, LANES)[:, :, :MLA_NOPE]
    v = g[:, MLA_H * LANES:].reshape(MLA_KVR, MLA_H, LANES)[:, :, :MLA_V]
    return jnp.concatenate([k, v], axis=2).reshape(MLA_KVR, MLA_H * (MLA_NOPE + MLA_V))


def _perm_w_out(w):
    m = jnp.pad(w[SSD_W + POOL_W:].reshape(MLA_H, MLA_V, D), ((0, 0), (0, LANES - MLA_V), (0, 0))).reshape(MLA_H * LANES, D)
    return jnp.concatenate([w[:SSD_W + POOL_W], m], axis=0)


def _unperm_w_out(g):
    m = g[SSD_W + POOL_W:].reshape(MLA_H, LANES, D)[:, :MLA_V].reshape(MLA_H * MLA_V, D)
    return jnp.concatenate([g[:SSD_W + POOL_W], m], axis=0)


def _lane_pad(v):
    return jnp.pad(v.reshape(1, -1), ((0, 0), (0, LANES - v.shape[-1])))


SMALL = ("attn_norm", "ssd_conv_b", "ssd_dt_bias", "ssd_a_log", "ssd_d", "ssd_norm", "pool_w", "pool_scale",
         "mla_q_norm", "mla_kv_norm", "ffn_norm", "ffn_conv_b", "final_norm")
SHARDED = {"w_in": 2, "ssd_conv_w": 2, "mla_w_uq": 2, "mla_w_ukv": 2, "w_out": 1, "ffn_w_up": 2, "ffn_conv_w": 2,
           "ffn_w_down": 1}
ALL_W = ("attn_norm", "w_in", "ssd_conv_w", "ssd_conv_b", "ssd_dt_bias", "ssd_a_log", "ssd_d", "ssd_norm", "pool_w",
         "pool_scale", "mla_q_norm", "mla_w_uq", "mla_kv_norm", "mla_w_ukv", "w_out", "ffn_norm", "ffn_w_up",
         "ffn_conv_w", "ffn_conv_b", "ffn_w_down", "final_norm")


def _pack_small(d):
    rows, layout = [], []
    for k in SMALL:
        a = d[k].reshape(-1)
        n = a.shape[0]
        r = -(-n // LANES)
        rows.append(jnp.pad(a, (0, r * LANES - n)).reshape(r, LANES))
        layout.append((k, n, r, d[k].shape))
    buf = jnp.concatenate(rows, axis=0)
    pad = (-buf.shape[0]) % 8
    return jnp.pad(buf, ((0, pad), (0, 0))), layout


def _unpack_small(buf, layout):
    out, r0 = {}, 0
    for k, n, r, shp in layout:
        out[k] = buf[r0:r0 + r].reshape(-1)[:n].reshape(shp)
        r0 += r
    return out


def _layer_fwd(l, x, W, cs, sn, Bl):
    T = x.shape[0]
    S = T // Bl
    n = f"l{l}_"
    h = rms_fwd(x, W["attn_norm"], name=n + "attn_norm")
    proj = mm(h, W["w_in"], name=n + "w_in")
    proj3 = proj.reshape(Bl, S, PW)
    xbc3 = conv_silu_fwd(proj3, W["ssd_conv_w"], W["ssd_conv_b"], name=n + "ssd_conv")
    y3, hin = ssd_fwd(xbc3, proj3, W["ssd_dt_bias"], W["ssd_a_log"], W["ssd_d"], name=n + "ssd_scan")
    y = y3.reshape(T, SSD_W)
    y_ssd = gated_rms_fwd(y, proj, W["ssd_norm"], name=n + "ssd_gate_norm")
    y_pool = pool_fwd(proj3, W["pool_w"], W["pool_scale"], name=n + "pool").reshape(T, POOL_W)
    qn = rms_fwd(proj, W["mla_q_norm"], col0=CQ0, width=MLA_QR, name=n + "q_norm")
    kvn = rms_fwd(proj, W["mla_kv_norm"], col0=CKV0, width=MLA_KVR, name=n + "kv_norm")
    qt = mm(qn, W["mla_w_uq"], name=n + "w_uq")
    kvt = mm(kvn, W["mla_w_ukv"], name=n + "w_ukv")
    qr, kr, vr = mla_prep_fwd(qt, kvt, proj, cs, sn, name=n + "rope")
    o, lse, lset = flash_fwd(qr, kr, vr, Bl, name=n + "attn")
    ycat = jnp.concatenate([y_ssd, y_pool, o], axis=1)
    x1 = mm(ycat, W["w_out"], add=x, name=n + "w_out")
    h2 = rms_fwd(x1, W["ffn_norm"], name=n + "ffn_norm")
    pre = mm(h2, W["ffn_w_up"], name=n + "w_up")
    pre3 = pre.reshape(Bl, S, 2 * DFF)
    act = ffn_act_fwd(pre3, W["ffn_conv_w"], W["ffn_conv_b"], name=n + "ffn_act").reshape(T, DFF)
    x2 = mm(act, W["ffn_w_down"], add=x1, name=n + "w_down")
    saved = dict(x=x, h=h, proj=proj, xbc3=xbc3, hin=hin, y=y, qn=qn, kvn=kvn, vr=vr, qr=qr, kr=kr, o=o, lse=lse, lset=lset,
                 ycat=ycat, x1=x1, h2=h2, pre3=pre3, act=act)
    return x2, saved


def _layer_bwd(l, dx2, dx2b, W, sv, cs, sn, Bl):
    T = dx2.shape[0]
    S = T // Bl
    n = f"l{l}_b_"
    g = {}
    g["ffn_w_down"] = mm(sv["act"], dx2b, ta=True, name=n + "dw_down")
    dact = mm(dx2b, W["ffn_w_down"], tb=True, name=n + "dact")
    dpg, dpv, dwg, dwv, dbg, dbv = ffn_act_bwd(sv["pre3"], W["ffn_conv_w"], W["ffn_conv_b"], dact.reshape(Bl, S, DFF),
                                               name=n + "ffn_act")
    g["ffn_conv_w"] = jnp.concatenate([dwg, dwv], axis=1)
    g["ffn_conv_b"] = jnp.concatenate([dbg, dbv], axis=1)
    dpg, dpv = dpg.reshape(T, DFF), dpv.reshape(T, DFF)
    g["ffn_w_up"] = jnp.concatenate([mm(sv["h2"], dpg, ta=True, name=n + "dw_up_g"),
                                     mm(sv["h2"], dpv, ta=True, name=n + "dw_up_v")], axis=1)
    dh2 = mm(dpg, W["ffn_w_up"], tb=True, name=n + "dh2_g")
    dh2 = mm(dpv, W["ffn_w_up"], tb=True, b_k0=DFF, add=dh2, name=n + "dh2_v")
    dx1, dx1b, g["ffn_norm"] = rms_bwd(sv["x1"], W["ffn_norm"], dh2, add=dx2, name=n + "ffn_norm")
    g["w_out"] = mm(sv["ycat"], dx1b, ta=True, name=n + "dw_out")
    dycat = mm(dx1b, W["w_out"], tb=True, name=n + "dycat")
    proj, proj3 = sv["proj"], sv["proj"].reshape(Bl, S, PW)
    dy, dz, g["ssd_norm"] = gated_rms_bwd(sv["y"], proj, W["ssd_norm"], dycat, name=n + "ssd_gate_norm")
    dxa, ddt, dpar, dd = ssd_bwd(sv["xbc3"], proj3, sv["hin"], dy.reshape(Bl, S, SSD_W), W["ssd_dt_bias"], W["ssd_a_log"],
                                 W["ssd_d"], name=n + "ssd_scan")
    g["ssd_dt_bias"] = dpar[0, :SSD_HEADS]
    g["ssd_a_log"] = dpar[1, :SSD_HEADS]
    g["ssd_d"] = dd[:, :2].reshape(SSD_HEADS)
    dxbc, g["ssd_conv_w"], g["ssd_conv_b"] = conv_silu_bwd(proj3, W["ssd_conv_w"], W["ssd_conv_b"], dxa, name=n + "ssd_conv")
    du, g["pool_w"], g["pool_scale"] = pool_bwd(proj3, W["pool_w"], W["pool_scale"], dycat.reshape(Bl, S, YCAT), name=n + "pool")
    dqr, dlt = flash_bwd_dq(sv["qr"], sv["kr"], sv["vr"], sv["o"], sv["lse"], dycat, Bl, name=n + "attn_dq")
    dkr, dv = flash_bwd_dkv(sv["qr"], sv["kr"], sv["vr"], sv["lset"], dlt, dycat, Bl, name=n + "attn_dkv")
    dqt, dkt, dkpe = mla_prep_bwd(dqr, dkr, cs, sn, name=n + "rope")
    g["mla_w_ukv"] = jnp.concatenate([mm(sv["kvn"], dkt, ta=True, name=n + "dw_uk"),
                                      mm(sv["kvn"], dv, ta=True, name=n + "dw_uv")], axis=1)
    dkvn = mm(dkt, W["mla_w_ukv"], tb=True, name=n + "dkvn_k")
    dkvn = mm(dv, W["mla_w_ukv"], tb=True, b_k0=MLA_H * LANES, add=dkvn, name=n + "dkvn_v")
    g["mla_w_uq"] = mm(sv["qn"], dqt, ta=True, name=n + "dw_uq")
    dqn = mm(dqt, W["mla_w_uq"], tb=True, name=n + "dqn")
    dcq, g["mla_q_norm"] = rms_bwd(proj, W["mla_q_norm"], dqn, col0=CQ0, width=MLA_QR, name=n + "q_norm")
    dckv, g["mla_kv_norm"] = rms_bwd(proj, W["mla_kv_norm"], dkvn, col0=CKV0, width=MLA_KVR, name=n + "kv_norm")
    dproj = jnp.concatenate([dz, dxbc.reshape(T, SSD_XBC), du.reshape(T, POOL_W), dcq, jnp.zeros((T, LANES), F32), dckv,
                             ddt.reshape(T, LANES), dkpe], axis=1).astype(BF16)
    g["w_in"] = mm(sv["h"], dproj, ta=True, name=n + "dw_in")
    dh = mm(dproj, W["w_in"], tb=True, name=n + "dh")
    dx, dxb, g["attn_norm"] = rms_bwd(sv["x"], W["attn_norm"], dh, add=dx1, name=n + "attn_norm")
    return dx, dxb, g


def kernel(x, positions, attn_norm, w_in, ssd_conv_w, ssd_conv_b, ssd_dt_bias, ssd_a_log, ssd_d, ssd_norm, pool_w, pool_scale, mla_q_norm, mla_w_uq, mla_kv_norm, mla_w_ukv, w_out, ffn_norm, ffn_w_up, ffn_conv_w, ffn_conv_b, ffn_w_down, final_norm, loss_target, m_attn_norm, m_w_in, m_ssd_conv_w, m_ssd_conv_b, m_ssd_dt_bias, m_ssd_a_log, m_ssd_d, m_ssd_norm, m_pool_w, m_pool_scale, m_mla_q_norm, m_mla_w_uq, m_mla_kv_norm, m_mla_w_ukv, m_w_out, m_ffn_norm, m_ffn_w_up, m_ffn_conv_w, m_ffn_conv_b, m_ffn_w_down, m_final_norm, v_attn_norm, v_w_in, v_ssd_conv_w, v_ssd_conv_b, v_ssd_dt_bias, v_ssd_a_log, v_ssd_d, v_ssd_norm, v_pool_w, v_pool_scale, v_mla_q_norm, v_mla_w_uq, v_mla_kv_norm, v_mla_w_ukv, v_w_out, v_ffn_norm, v_ffn_w_up, v_ffn_conv_w, v_ffn_conv_b, v_ffn_w_down, v_final_norm):
    a = locals()
    Wt = {k: a[k] for k in ALL_W}
    Mo = {k: a["m_" + k] for k in ALL_W}
    Vo = {k: a["v_" + k] for k in ALL_W}
    Bl, S, _ = x.shape
    T = Bl * S

    names = list(SHARDED)
    conv = ("ssd_conv_w", "ffn_conv_w")
    shards = [Wt[k] if k in conv else Wt[k].astype(BF16) for k in names]
    gathered = all_gather_many(shards, name="gather_weights")
    full = {k: _from_owner_major(g8, SHARDED[k]) for k, g8 in zip(names, gathered)}

    pos = positions.astype(F32).reshape(T, 1)
    inv_freq = ROPE_THETA ** (-jnp.arange(0, MLA_ROPE, 2, dtype=F32) / MLA_ROPE)
    invf = jnp.pad(jnp.concatenate([inv_freq, inv_freq]), (PE_LO, LANES - PE_HI)).reshape(1, LANES)

    layers = []
    for l in range(DEPTH):
        layers.append({
            "attn_norm": attn_norm[l].reshape(1, D), "w_in": _perm_w_in(full["w_in"][l]),
            "ssd_conv_w": full["ssd_conv_w"][l], "ssd_conv_b": ssd_conv_b[l].reshape(1, SSD_XBC),
            "ssd_dt_bias": _lane_pad(ssd_dt_bias[l]), "ssd_a_log": _lane_pad(ssd_a_log[l]),
            "ssd_d": jnp.repeat(ssd_d[l].reshape(NPAIR, 2), SSD_P, axis=1), "ssd_norm": ssd_norm[l].reshape(1, SSD_W),
            "pool_w": pool_w[l].astype(BF16), "pool_scale": pool_scale[l].reshape(1, POOL_W),
            "mla_q_norm": mla_q_norm[l].reshape(1, MLA_QR), "mla_w_uq": _perm_w_uq(full["mla_w_uq"][l]),
            "mla_kv_norm": mla_kv_norm[l].reshape(1, MLA_KVR), "mla_w_ukv": _perm_w_ukv(full["mla_w_ukv"][l]),
            "w_out": _perm_w_out(full["w_out"][l]), "ffn_norm": ffn_norm[l].reshape(1, D),
            "ffn_w_up": full["ffn_w_up"][l], "ffn_conv_w": full["ffn_conv_w"][l],
            "ffn_conv_b": ffn_conv_b[l].reshape(1, 2 * DFF), "ffn_w_down": full["ffn_w_down"][l]})

    cs, sn = rope_tables(pos, invf, name="rope_tables")
    xc = x.reshape(T, D)
    saved = []
    for l in range(DEPTH):
        xc, sv = _layer_fwd(l, xc, layers[l], cs, sn, Bl)
        saved.append(sv)
    dx, dxb, g_final, loss_part = final_loss(xc, final_norm.reshape(1, D), loss_target.reshape(T, D), name="final_loss")
    grads = [None] * DEPTH
    for l in reversed(range(DEPTH)):
        dx, dxb, grads[l] = _layer_bwd(l, dx, dxb, layers[l], saved[l], cs, sn, Bl)
    loss = lax.psum(loss_part[0, 0], AXES)

    unperm = {"w_in": _unperm_w_in, "mla_w_uq": _unperm_w_uq, "mla_w_ukv": _unperm_w_ukv, "w_out": _unperm_w_out}
    part = {}
    for k in ALL_W:
        if k == "final_norm":
            part[k] = g_final.reshape(D)
        else:
            shp = list(Wt[k].shape[1:])
            if k in SHARDED:
                shp[SHARDED[k] - 1] *= N_DEV
            part[k] = jnp.stack([unperm.get(k, lambda t: t)(grads[l][k]).reshape(shp) for l in range(DEPTH)])

    xi, yi, ci = _place()
    chip = 2 * xi + yi
    g8 = [_owner_major(part[k], SHARDED[k]) for k in names]
    r1 = exchange_sibling(g8, name="rs_sibling")
    keep = [lax.dynamic_index_in_dim(g.reshape((4, 2) + g.shape[1:]), ci, 1, keepdims=False) for g in g8]
    sums = [add2(kp, r, name="rs_add_" + k) for k, kp, r in zip(names, keep, r1)]
    r2 = exchange_chips([sb for _, sb in sums], name="rs_chips")
    out_g, out_d, out_m, out_v = {}, {}, {}, {}
    for k, (s32, _), r in zip(names, sums, r2):
        own = lax.dynamic_index_in_dim(s32, chip, 0, keepdims=True)
        out_g[k], out_d[k], out_m[k], out_v[k] = adamw(Wt[k], Mo[k], Vo[k], [own, r], name="adamw_" + k)

    pg, layout = _pack_small(part)
    pw, _ = _pack_small(Wt)
    pm, _ = _pack_small(Mo)
    pv, _ = _pack_small(Vo)
    (pg8,) = all_gather_many([pg], name="gather_small_grads")
    sg, sd, sm, sv_ = adamw(pw, pm, pv, [pg8], name="adamw_small")
    for dst, buf in ((out_g, sg), (out_d, sd), (out_m, sm), (out_v, sv_)):
        dst.update(_unpack_small(buf, layout))

    return (loss, dx.reshape(Bl, S, D), *[out_g[k] for k in ALL_W], *[out_d[k] for k in ALL_W],
            *[out_m[k] for k in ALL_W], *[out_v[k] for k in ALL_W])
```

```python
import functools
import math

import jax
import jax.numpy as jnp
from jax import lax
from jax.experimental import pallas as pl
from jax.experimental.pallas import tpu as pltpu

F32, BF16 = jnp.float32, jnp.bfloat16
SDS = jax.ShapeDtypeStruct
MESH = pl.DeviceIdType.MESH
AXES = ("x", "y", "c")
N_DEV = 8

D = 1024
EPS = 1e-6
SSD_HEADS, SSD_P, SSD_W, SSD_G, SSD_N, SSD_K, SSD_L, SSD_XBC = 16, 64, 1024, 2, 128, 4, 128, 1536
POOL_G, POOL_D, POOL_W, POOL_WIN = 4, 128, 512, (2, 4, 8, 16)
MLA_H, MLA_QR, MLA_KVR, MLA_NOPE, MLA_ROPE, MLA_V, MLA_QK = 8, 384, 256, 64, 32, 64, 96
ROPE_THETA = 10000.0
MIX = 2048
DFF, FFN_K = 2816, 3
DEPTH = 2
ADAM_LR, ADAM_B1, ADAM_B2, ADAM_EPS, ADAM_WD, ADAM_STEP = 0.001, 0.9, 0.999, 1e-08, 0.01, 10

Z0, XBC0, U0, CQ0, CKV0, DT0, KPE0, PW = 0, 1024, 2560, 3072, 3584, 3840, 3968, 4096
LANES = 128
YCAT = SSD_W + POOL_W + MLA_H * LANES
NEG = -1e30
VMEM_LIMIT = 56 * 1024 * 1024
MM_ROW_TILE, MM_LANE_TILE, MM_FULL_K = 1024, 1408, 2816


def _tile(n, pref, mult):
    if n <= pref:
        return n
    for d in range(pref, 0, -mult):
        if d % mult == 0 and n % d == 0:
            return d
    return n


def _dg(a, b, ca, cb, prec=None):
    return lax.dot_general(a, b, (((ca,), (cb,)), ((), ())), preferred_element_type=F32, precision=prec)


def _nn(a, b):
    return _dg(a.astype(BF16), b.astype(BF16), 1, 0)


def _nt(a, b):
    return _dg(a.astype(BF16), b.astype(BF16), 1, 1)


def _tn(a, b):
    return _dg(a.astype(BF16), b.astype(BF16), 0, 0)


def _sig(x):
    return jax.nn.sigmoid(x)


def _silu(x):
    return x * _sig(x)


def _dsilu(x):
    s = _sig(x)
    return s * (1.0 + x * (1.0 - s))


def _pc(body, *, name, grid, in_specs, out_specs, out_shape, scratch=()):
    return pl.pallas_call(
        body, name=name, grid=grid, in_specs=in_specs, out_specs=out_specs, out_shape=out_shape,
        scratch_shapes=list(scratch), compiler_params=pltpu.CompilerParams(vmem_limit_bytes=VMEM_LIMIT))


def _rsum(x):
    return jnp.sum(x, axis=1, keepdims=True)


def _csum(x):
    return jnp.sum(x, axis=0, keepdims=True)


def mm(a, b, *, ta=False, tb=False, add=None, out_dtype=F32, b_k0=0, name):
    M, K = (a.shape[1], a.shape[0]) if ta else a.shape
    N = b.shape[0] if tb else b.shape[1]
    assert tb or b_k0 == 0
    tm = _tile(M, MM_LANE_TILE, LANES) if ta else _tile(M, MM_ROW_TILE, 8)
    tn = _tile(N, MM_LANE_TILE, LANES)
    if ta:
        tk = _tile(K, MM_ROW_TILE, 8)
    else:
        tk = K if K <= MM_FULL_K else _tile(K, 2048, LANES)
    nk = K // tk

    def body(*refs):
        if add is None:
            a_ref, b_ref, o_ref = refs[:3]
        else:
            a_ref, b_ref, add_ref, o_ref = refs[:4]
        part = _dg(a_ref[...].astype(BF16), b_ref[...].astype(BF16), 0 if ta else 1, 1 if tb else 0)

        def finish(r):
            if add is not None:
                r = r + add_ref[...].astype(F32)
            o_ref[...] = r.astype(out_dtype)

        if nk == 1:
            finish(part)
            return
        acc = refs[-1]
        k = pl.program_id(2)

        @pl.when(k == 0)
        def _():
            acc[...] = part

        @pl.when(k > 0)
        def _():
            acc[...] += part

        @pl.when(k == nk - 1)
        def _():
            finish(acc[...])

    a_spec = pl.BlockSpec((tk, tm), lambda i, j, k: (k, i)) if ta else pl.BlockSpec((tm, tk), lambda i, j, k: (i, k))
    assert b_k0 % tk == 0
    kb0 = b_k0 // tk
    b_spec = pl.BlockSpec((tn, tk), lambda i, j, k: (j, kb0 + k)) if tb else pl.BlockSpec((tk, tn), lambda i, j, k: (k, j))
    o_spec = pl.BlockSpec((tm, tn), lambda i, j, k: (i, j))
    ins, specs = [a, b], [a_spec, b_spec]
    if add is not None:
        ins.append(add)
        specs.append(o_spec)
    return _pc(body, name=name, grid=(M // tm, N // tn, nk), in_specs=specs, out_specs=o_spec,
               out_shape=SDS((M, N), out_dtype), scratch=[pltpu.VMEM((tm, tn), F32)] if nk > 1 else [])(*ins)


def rms_fwd(x, g, *, col0=0, width=None, name):
    T = x.shape[0]
    W = width or x.shape[1]
    tm = _tile(T, 512, 8)

    def body(x_ref, g_ref, o_ref):
        v = x_ref[...]
        r = lax.rsqrt(jnp.mean(v * v, axis=1, keepdims=True) + EPS)
        o_ref[...] = ((v * r) * g_ref[...]).astype(BF16)

    return _pc(body, name=name, grid=(T // tm,),
               in_specs=[pl.BlockSpec((tm, W), lambda i: (i, col0 // W)), pl.BlockSpec((1, W), lambda i: (0, 0))],
               out_specs=pl.BlockSpec((tm, W), lambda i: (i, 0)), out_shape=SDS((T, W), BF16))(x, g)


def rms_bwd(x, g, dh, *, col0=0, width=None, add=None, name):
    T = x.shape[0]
    W = width or x.shape[1]
    tm = _tile(T, 512, 8)

    def body(*refs):
        if add is None:
            x_ref, g_ref, dh_ref, dx_ref, dg_ref = refs
        else:
            x_ref, g_ref, dh_ref, add_ref, dx_ref, dxb_ref, dg_ref = refs
        v = x_ref[...]
        r = lax.rsqrt(jnp.mean(v * v, axis=1, keepdims=True) + EPS)
        xh = v * r
        d = dh_ref[...].astype(F32)
        dxh = d * g_ref[...]
        dx = r * (dxh - xh * jnp.mean(dxh * xh, axis=1, keepdims=True))
        if add is not None:
            dx = dx + add_ref[...]
            dxb_ref[...] = dx.astype(BF16)
        dx_ref[...] = dx

        @pl.when(pl.program_id(0) == 0)
        def _():
            dg_ref[...] = jnp.zeros_like(dg_ref)

        dg_ref[...] += _csum(d * xh)

    row = pl.BlockSpec((tm, W), lambda i: (i, 0))
    vec = pl.BlockSpec((1, W), lambda i: (0, 0))
    ins = [x, g, dh] + ([] if add is None else [add])
    specs = [pl.BlockSpec((tm, W), lambda i: (i, col0 // W)), vec, row] + ([] if add is None else [row])
    if add is None:
        return _pc(body, name=name, grid=(T // tm,), in_specs=specs, out_specs=[row, vec],
                   out_shape=[SDS((T, W), F32), SDS((1, W), F32)])(*ins)
    return _pc(body, name=name, grid=(T // tm,), in_specs=specs, out_specs=[row, row, vec],
               out_shape=[SDS((T, W), F32), SDS((T, W), BF16), SDS((1, W), F32)])(*ins)


def gated_rms_fwd(y, proj, g, *, name):
    T = y.shape[0]
    tm = _tile(T, 512, 8)

    def body(y_ref, z_ref, g_ref, o_ref):
        v = y_ref[...] * _silu(z_ref[...])
        r = lax.rsqrt(jnp.mean(v * v, axis=1, keepdims=True) + EPS)
        o_ref[...] = ((v * r) * g_ref[...]).astype(BF16)

    row = pl.BlockSpec((tm, SSD_W), lambda i: (i, 0))
    return _pc(body, name=name, grid=(T // tm,), in_specs=[row, row, pl.BlockSpec((1, SSD_W), lambda i: (0, 0))],
               out_specs=row, out_shape=SDS((T, SSD_W), BF16))(y, proj, g)


def gated_rms_bwd(y, proj, g, dycat, *, name):
    T = y.shape[0]
    tm = _tile(T, 512, 8)

    def body(y_ref, z_ref, g_ref, d_ref, dy_ref, dz_ref, dg_ref):
        yv, z = y_ref[...], z_ref[...]
        sz = _silu(z)
        v = yv * sz
        r = lax.rsqrt(jnp.mean(v * v, axis=1, keepdims=True) + EPS)
        vh = v * r
        d = d_ref[...]
        dvh = d * g_ref[...]
        dv = r * (dvh - vh * jnp.mean(dvh * vh, axis=1, keepdims=True))
        dy_ref[...] = dv * sz
        dz_ref[...] = dv * yv * _dsilu(z)

        @pl.when(pl.program_id(0) == 0)
        def _():
            dg_ref[...] = jnp.zeros_like(dg_ref)

        dg_ref[...] += _csum(d * vh)

    row = pl.BlockSpec((tm, SSD_W), lambda i: (i, 0))
    vec = pl.BlockSpec((1, SSD_W), lambda i: (0, 0))
    return _pc(body, name=name, grid=(T // tm,), in_specs=[row, row, vec, row], out_specs=[row, row, vec],
               out_shape=[SDS((T, SSD_W), F32), SDS((T, SSD_W), F32), SDS((1, SSD_W), F32)])(y, proj, g, dycat)


def final_loss(x, g, tgt, *, name):
    T = x.shape[0]
    tm = _tile(T, 512, 8)

    def body(x_ref, g_ref, t_ref, dx_ref, dxb_ref, dg_ref, l_ref):
        v = x_ref[...]
        gg = g_ref[...]
        r = lax.rsqrt(jnp.mean(v * v, axis=1, keepdims=True) + EPS)
        xh = v * r
        err = xh * gg - t_ref[...]
        part = 0.5 * _csum(jnp.mean(err * err, axis=1, keepdims=True))
        d = err * (1.0 / D)
        dxh = d * gg
        dx = r * (dxh - xh * jnp.mean(dxh * xh, axis=1, keepdims=True))
        dx_ref[...] = dx
        dxb_ref[...] = dx.astype(BF16)

        @pl.when(pl.program_id(0) == 0)
        def _():
            dg_ref[...] = jnp.zeros_like(dg_ref)
            l_ref[...] = jnp.zeros_like(l_ref)

        dg_ref[...] += _csum(d * xh)
        l_ref[...] += jnp.broadcast_to(part, (1, LANES))

    row = pl.BlockSpec((tm, D), lambda i: (i, 0))
    vec = pl.BlockSpec((1, D), lambda i: (0, 0))
    return _pc(body, name=name, grid=(T // tm,), in_specs=[row, vec, row],
               out_specs=[row, row, vec, pl.BlockSpec((1, LANES), lambda i: (0, 0))],
               out_shape=[SDS((T, D), F32), SDS((T, D), BF16), SDS((1, D), F32), SDS((1, LANES), F32)])(x, g, tgt)


HALO = 8


def _prev_map(ts, col):
    return lambda b, i, j: (b, jnp.maximum(i * (ts // HALO) - 1, 0), col(j))


def _next_map(ts, n_halo_blocks, col):
    return lambda b, i, j: (b, jnp.minimum((i + 1) * (ts // HALO), n_halo_blocks - 1), col(j))


def conv_silu_fwd(proj3, w, b, *, name):
    Bl, S, _ = proj3.shape
    C, K = SSD_XBC, SSD_K
    ts, tc = _tile(S, 512, 8), 512
    c0 = XBC0 // tc

    def body(xp_ref, x_ref, w_ref, b_ref, o_ref, ext):
        i = pl.program_id(1)
        ext[0:HALO, :] = jnp.where(i > 0, xp_ref[0], 0.0)
        ext[HALO:HALO + ts, :] = x_ref[0]
        acc = b_ref[...] + w_ref[0:1, :] * ext[pl.ds(HALO - (K - 1), ts), :]
        for k in range(1, K):
            acc = acc + w_ref[k:k + 1, :] * ext[pl.ds(HALO - (K - 1) + k, ts), :]
        o_ref[0] = _silu(acc)

    return _pc(body, name=name, grid=(Bl, S // ts, C // tc),
               in_specs=[pl.BlockSpec((1, HALO, tc), _prev_map(ts, lambda j: c0 + j)),
                         pl.BlockSpec((1, ts, tc), lambda b, i, j: (b, i, c0 + j)),
                         pl.BlockSpec((K, tc), lambda b, i, j: (0, j)),
                         pl.BlockSpec((1, tc), lambda b, i, j: (0, j))],
               out_specs=pl.BlockSpec((1, ts, tc), lambda b, i, j: (b, i, j)),
               out_shape=SDS((Bl, S, C), F32), scratch=[pltpu.VMEM((HALO + ts, tc), F32)])(proj3, proj3, w, b)


def conv_silu_bwd(proj3, w, b, dact, *, name):
    Bl, S, _ = proj3.shape
    C, K = SSD_XBC, SSD_K
    ts, tc = _tile(S, 512, 8), 512
    c0 = XBC0 // tc
    ns = S // ts

    def body(xp_ref, x_ref, xn_ref, d_ref, dn_ref, w_ref, b_ref, dx_ref, dw_ref, db_ref, ext, ext2):
        bb, i = pl.program_id(1), pl.program_id(2)
        last = i == ns - 1
        ext[0:HALO, :] = jnp.where(i > 0, xp_ref[0], 0.0)
        ext[HALO:HALO + ts, :] = x_ref[0]
        ext[HALO + ts:2 * HALO + ts, :] = jnp.where(last, 0.0, xn_ref[0])
        acc = b_ref[...] + w_ref[0:1, :] * ext[pl.ds(HALO - (K - 1), ts + HALO), :]
        for k in range(1, K):
            acc = acc + w_ref[k:k + 1, :] * ext[pl.ds(HALO - (K - 1) + k, ts + HALO), :]
        dsl = _dsilu(acc)
        du = d_ref[0] * dsl[0:ts]
        ext2[0:ts, :] = du
        ext2[ts:ts + HALO, :] = jnp.where(last, 0.0, dn_ref[0]) * dsl[ts:ts + HALO]
        dx = w_ref[0:1, :] * ext2[pl.ds(K - 1, ts), :]
        for k in range(1, K):
            dx = dx + w_ref[k:k + 1, :] * ext2[pl.ds(K - 1 - k, ts), :]
        dx_ref[0] = dx

        @pl.when((bb == 0) & (i == 0))
        def _():
            dw_ref[...] = jnp.zeros_like(dw_ref)
            db_ref[...] = jnp.zeros_like(db_ref)

        for k in range(K):
            dw_ref[k:k + 1, :] += _csum(du * ext[pl.ds(HALO - (K - 1) + k, ts), :])
        db_ref[...] += _csum(du)

    nhb = S // HALO
    cx = lambda j: c0 + j
    cj = lambda j: j
    return _pc(body, name=name, grid=(C // tc, Bl, ns),
               in_specs=[pl.BlockSpec((1, HALO, tc), lambda j, b, i: _prev_map(ts, cx)(b, i, j)),
                         pl.BlockSpec((1, ts, tc), lambda j, b, i: (b, i, c0 + j)),
                         pl.BlockSpec((1, HALO, tc), lambda j, b, i: _next_map(ts, nhb, cx)(b, i, j)),
                         pl.BlockSpec((1, ts, tc), lambda j, b, i: (b, i, j)),
                         pl.BlockSpec((1, HALO, tc), lambda j, b, i: _next_map(ts, nhb, cj)(b, i, j)),
                         pl.BlockSpec((K, tc), lambda j, b, i: (0, j)),
                         pl.BlockSpec((1, tc), lambda j, b, i: (0, j))],
               out_specs=[pl.BlockSpec((1, ts, tc), lambda j, b, i: (b, i, j)),
                          pl.BlockSpec((K, tc), lambda j, b, i: (0, j)),
                          pl.BlockSpec((1, tc), lambda j, b, i: (0, j))],
               out_shape=[SDS((Bl, S, C), F32), SDS((K, C), F32), SDS((1, C), F32)],
               scratch=[pltpu.VMEM((2 * HALO + ts, tc), F32), pltpu.VMEM((HALO + ts, tc), F32)],
               )(proj3, proj3, proj3, dact, dact, w, b)


def ffn_act_fwd(pre3, w, b, *, name):
    Bl, S, _ = pre3.shape
    K = FFN_K
    ts, tc = _tile(S, 512, 8), 256
    nj = DFF // tc

    def body(gp_ref, g_ref, vp_ref, v_ref, wg_ref, wv_ref, bg_ref, bv_ref, o_ref, eg, ev):
        i = pl.program_id(1)
        outs = []
        for p_ref, m_ref, w_ref, b_ref, ext in ((gp_ref, g_ref, wg_ref, bg_ref, eg), (vp_ref, v_ref, wv_ref, bv_ref, ev)):
            ext[0:HALO, :] = jnp.where(i > 0, p_ref[0], 0.0)
            ext[HALO:HALO + ts, :] = m_ref[0]
            acc = b_ref[...] + w_ref[0:1, :] * ext[pl.ds(HALO - (K - 1), ts), :]
            for k in range(1, K):
                acc = acc + w_ref[k:k + 1, :] * ext[pl.ds(HALO - (K - 1) + k, ts), :]
            outs.append(acc)
        o_ref[0] = (_silu(outs[0]) * outs[1]).astype(BF16)

    main = lambda off: pl.BlockSpec((1, ts, tc), lambda b, i, j: (b, i, off + j))
    prev = lambda off: pl.BlockSpec((1, HALO, tc), _prev_map(ts, lambda j: off + j))
    wsp = lambda off: pl.BlockSpec((K, tc), lambda b, i, j: (0, off + j))
    bsp = lambda off: pl.BlockSpec((1, tc), lambda b, i, j: (0, off + j))
    return _pc(body, name=name, grid=(Bl, S // ts, nj),
               in_specs=[prev(0), main(0), prev(nj), main(nj), wsp(0), wsp(nj), bsp(0), bsp(nj)],
               out_specs=pl.BlockSpec((1, ts, tc), lambda b, i, j: (b, i, j)),
               out_shape=SDS((Bl, S, DFF), BF16),
               scratch=[pltpu.VMEM((HALO + ts, tc), F32), pltpu.VMEM((HALO + ts, tc), F32)],
               )(pre3, pre3, pre3, pre3, w, w, b, b)


def ffn_act_bwd(pre3, w, b, dact, *, name):
    Bl, S, _ = pre3.shape
    K = FFN_K
    ts, tc = _tile(S, 512, 8), 256
    nj = DFF // tc
    ns = S // ts

    def body(gp_ref, g_ref, gn_ref, vp_ref, v_ref, vn_ref, d_ref, dn_ref, wg_ref, wv_ref, bg_ref, bv_ref,
             dg_ref, dv_ref, dwg_ref, dwv_ref, dbg_ref, dbv_ref, eg, ev, e2g, e2v):
        bb, i = pl.program_id(1), pl.program_id(2)
        last = i == ns - 1
        ups = []
        for p_ref, m_ref, n_ref, w_ref, b_ref, ext in ((gp_ref, g_ref, gn_ref, wg_ref, bg_ref, eg),
                                                       (vp_ref, v_ref, vn_ref, wv_ref, bv_ref, ev)):
            ext[0:HALO, :] = jnp.where(i > 0, p_ref[0], 0.0)
            ext[HALO:HALO + ts, :] = m_ref[0]
            ext[HALO + ts:2 * HALO + ts, :] = jnp.where(last, 0.0, n_ref[0])
            acc = b_ref[...] + w_ref[0:1, :] * ext[pl.ds(HALO - (K - 1), ts + HALO), :]
            for k in range(1, K):
                acc = acc + w_ref[k:k + 1, :] * ext[pl.ds(HALO - (K - 1) + k, ts + HALO), :]
            ups.append(acc)
        ug, uv = ups
        dg_e = uv * _dsilu(ug)
        dv_e = _silu(ug)
        d_main = d_ref[0]
        d_next = jnp.where(last, 0.0, dn_ref[0])
        dug = d_main * dg_e[0:ts]
        duv = d_main * dv_e[0:ts]
        e2g[0:ts, :] = dug
        e2g[ts:ts + HALO, :] = d_next * dg_e[ts:ts + HALO]
        e2v[0:ts, :] = duv
        e2v[ts:ts + HALO, :] = d_next * dv_e[ts:ts + HALO]

        @pl.when((bb == 0) & (i == 0))
        def _():
            for r in (dwg_ref, dwv_ref, dbg_ref, dbv_ref):
                r[...] = jnp.zeros_like(r)

        for w_ref, e2, ext, du, o_ref, dw_ref, db_ref in ((wg_ref, e2g, eg, dug, dg_ref, dwg_ref, dbg_ref),
                                                          (wv_ref, e2v, ev, duv, dv_ref, dwv_ref, dbv_ref)):
            dx = w_ref[0:1, :] * e2[pl.ds(K - 1, ts), :]
            for k in range(1, K):
                dx = dx + w_ref[k:k + 1, :] * e2[pl.ds(K - 1 - k, ts), :]
            o_ref[0] = dx.astype(BF16)
            for k in range(K):
                dw_ref[k:k + 1, :] += _csum(du * ext[pl.ds(HALO - (K - 1) + k, ts), :])
            db_ref[...] += _csum(du)

    nhb = S // HALO
    main = lambda off: pl.BlockSpec((1, ts, tc), lambda j, b, i: (b, i, off + j))
    prev = lambda off: pl.BlockSpec((1, HALO, tc), lambda j, b, i: _prev_map(ts, lambda jj: off + jj)(b, i, j))
    nxt = lambda off: pl.BlockSpec((1, HALO, tc), lambda j, b, i: _next_map(ts, nhb, lambda jj: off + jj)(b, i, j))
    wsp = lambda off: pl.BlockSpec((K, tc), lambda j, b, i: (0, off + j))
    bsp = lambda off: pl.BlockSpec((1, tc), lambda j, b, i: (0, off + j))
    outs = _pc(body, name=name, grid=(nj, Bl, ns),
               in_specs=[prev(0), main(0), nxt(0), prev(nj), main(nj), nxt(nj), main(0), nxt(0),
                         wsp(0), wsp(nj), bsp(0), bsp(nj)],
               out_specs=[main(0), main(0), wsp(0), wsp(0), bsp(0), bsp(0)],
               out_shape=[SDS((Bl, S, DFF), BF16), SDS((Bl, S, DFF), BF16), SDS((K, DFF), F32), SDS((K, DFF), F32),
                          SDS((1, DFF), F32), SDS((1, DFF), F32)],
               scratch=[pltpu.VMEM((2 * HALO + ts, tc), F32), pltpu.VMEM((2 * HALO + ts, tc), F32),
                        pltpu.VMEM((HALO + ts, tc), F32), pltpu.VMEM((HALO + ts, tc), F32)],
               )(pre3, pre3, pre3, pre3, pre3, pre3, dact, dact, w, w, b, b)
    return outs


PHALO = 16


def _pool_window_sums(ext, base, ts, step):
    s = ext[pl.ds(base, ts), :]
    out = []
    for i in range(1, PHALO):
        s = s + ext[pl.ds(base + step * i, ts), :]
        if i + 1 in POOL_WIN:
            out.append(s)
    return out


def _pick(g, vals):
    r = vals[-1]
    for k in range(len(vals) - 2, -1, -1):
        r = jnp.where(g == k, vals[k], r)
    return r


def _pool_count(g, i, ts, rows):
    t = (i * ts + lax.broadcasted_iota(jnp.int32, (rows, 1), 0) + 1).astype(F32)
    return jnp.minimum(t, _pick(g, [float(w) for w in POOL_WIN]))


def _pooled(up_ref, u_ref, ext, g, i, ts):
    ext[0:PHALO, :] = jnp.where(i > 0, up_ref[0], 0.0)
    u = u_ref[0]
    ext[PHALO:PHALO + ts, :] = u
    sums = _pool_window_sums(ext, PHALO, ts, -1)
    return _pick(g, sums) / _pool_count(g, i, ts, ts) - u


def pool_fwd(proj3, pool_w, scale, *, name):
    Bl, S, _ = proj3.shape
    ts = _tile(S, 512, 16)
    c0 = U0 // POOL_D

    def body(up_ref, u_ref, w_ref, s_ref, o_ref, ext):
        i, g = pl.program_id(1), pl.program_id(2)
        pooled = _pooled(up_ref, u_ref, ext, g, i, ts)
        o_ref[0] = (_nn(pooled, w_ref[0]) * s_ref[...]).astype(BF16)

    return _pc(body, name=name, grid=(Bl, S // ts, POOL_G),
               in_specs=[pl.BlockSpec((1, PHALO, POOL_D), lambda b, i, g: (b, jnp.maximum(i * (ts // PHALO) - 1, 0), c0 + g)),
                         pl.BlockSpec((1, ts, POOL_D), lambda b, i, g: (b, i, c0 + g)),
                         pl.BlockSpec((1, POOL_D, POOL_D), lambda b, i, g: (g, 0, 0)),
                         pl.BlockSpec((1, POOL_D), lambda b, i, g: (0, g))],
               out_specs=pl.BlockSpec((1, ts, POOL_D), lambda b, i, g: (b, i, g)),
               out_shape=SDS((Bl, S, POOL_W), BF16), scratch=[pltpu.VMEM((PHALO + ts, POOL_D), F32)],
               )(proj3, proj3, pool_w, scale)


def pool_bwd(proj3, pool_w, scale, dycat3, *, name):
    Bl, S, _ = proj3.shape
    ts = _tile(S, 512, 16)
    ns = S // ts
    c0 = U0 // POOL_D
    d0 = SSD_W // POOL_D
    nhb = S // PHALO

    def body(up_ref, u_ref, d_ref, dn_ref, w_ref, s_ref, du_ref, dw_ref, ds_ref, ext, ext2):
        g, bb, i = pl.program_id(0), pl.program_id(1), pl.program_id(2)
        last = i == ns - 1
        pooled = _pooled(up_ref, u_ref, ext, g, i, ts)
        wm = w_ref[0]
        sc = s_ref[...]
        dy = d_ref[0]
        dp_main = dy * sc
        dpool = _nt(dp_main, wm)
        dpool_n = _nt(jnp.where(last, 0.0, dn_ref[0]) * sc, wm)
        ext2[0:ts, :] = dpool / _pool_count(g, i, ts, ts)
        ext2[ts:ts + PHALO, :] = dpool_n / _pool_count(g, i + 1, ts, PHALO)
        sums = _pool_window_sums(ext2, 0, ts, 1)
        du_ref[0] = _pick(g, sums) - dpool

        @pl.when((bb == 0) & (i == 0))
        def _():
            dw_ref[...] = jnp.zeros_like(dw_ref)
            ds_ref[...] = jnp.zeros_like(ds_ref)

        dw_ref[0] += _tn(pooled, dp_main)
        ds_ref[...] += _csum(dy * _nn(pooled, wm))

    return _pc(body, name=name, grid=(POOL_G, Bl, ns),
               in_specs=[pl.BlockSpec((1, PHALO, POOL_D), lambda g, b, i: (b, jnp.maximum(i * (ts // PHALO) - 1, 0), c0 + g)),
                         pl.BlockSpec((1, ts, POOL_D), lambda g, b, i: (b, i, c0 + g)),
                         pl.BlockSpec((1, ts, POOL_D), lambda g, b, i: (b, i, d0 + g)),
                         pl.BlockSpec((1, PHALO, POOL_D), lambda g, b, i: (b, jnp.minimum((i + 1) * (ts // PHALO), nhb - 1), d0 + g)),
                         pl.BlockSpec((1, POOL_D, POOL_D), lambda g, b, i: (g, 0, 0)),
                         pl.BlockSpec((1, POOL_D), lambda g, b, i: (0, g))],
               out_specs=[pl.BlockSpec((1, ts, POOL_D), lambda g, b, i: (b, i, g)),
                          pl.BlockSpec((1, POOL_D, POOL_D), lambda g, b, i: (g, 0, 0)),
                          pl.BlockSpec((1, POOL_D), lambda g, b, i: (0, g))],
               out_shape=[SDS((Bl, S, POOL_W), F32), SDS((POOL_G, POOL_D, POOL_D), F32), SDS((1, POOL_W), F32)],
               scratch=[pltpu.VMEM((PHALO + ts, POOL_D), F32), pltpu.VMEM((PHALO + ts, POOL_D), F32)],
               )(proj3, proj3, dycat3, dycat3, pool_w, scale)


NPAIR = SSD_HEADS // 2


def _ssd_common(sm, bias, alog):
    L = SSD_L
    dt = jax.nn.softplus(sm + bias)
    a = -jnp.exp(alog)
    da = dt * a
    r = lax.broadcasted_iota(jnp.int32, (L, L), 0)
    c = lax.broadcasted_iota(jnp.int32, (L, L), 1)
    tri = (r >= c).astype(F32)
    cum = _dg(tri, da, 1, 0, lax.Precision.HIGHEST)
    return dt, a, cum, cum.T, r >= c


def _lanes(lo, hi, shape=(1, LANES)):
    lane = lax.broadcasted_iota(jnp.int32, shape, len(shape) - 1)
    return (lane >= lo) & (lane < hi)


def _onehot_lane(h):
    return (lax.broadcasted_iota(jnp.int32, (1, LANES), 1) == h).astype(F32)


def _split_nn(a, e):
    hi = a.astype(BF16)
    lo = (a - hi.astype(F32)).astype(BF16)
    return _dg(hi, e, 1, 0) + _dg(lo, e, 1, 0)


def _head_spread():
    r = lax.broadcasted_iota(jnp.int32, (LANES, SSD_W), 0)
    c = lax.broadcasted_iota(jnp.int32, (LANES, SSD_W), 1)
    return (c // SSD_P == r).astype(BF16)


def _pair_gather(j):
    r = lax.broadcasted_iota(jnp.int32, (LANES, LANES), 0)
    c = lax.broadcasted_iota(jnp.int32, (LANES, LANES), 1)
    return (c == 2 * j + (r >= SSD_P).astype(jnp.int32)).astype(BF16)


def ssd_fwd(xbc3, proj3, bias, alog, dskip, *, name):
    Bl, S, _ = xbc3.shape
    L = SSD_L
    nc = S // L

    def body(xbc_ref, sm_ref, bias_ref, alog_ref, d_ref, y_ref, hin_ref, H):
        c = pl.program_id(1)

        @pl.when(c == 0)
        def _():
            H[...] = jnp.zeros_like(H)

        dt, a, cum, cumT, mask = _ssd_common(sm_ref[0], bias_ref[...], alog_ref[...])
        lo = _lanes(0, SSD_P)
        rowlo = lax.broadcasted_iota(jnp.int32, (LANES, LANES), 0) < SSD_P
        spread = _head_spread()
        dt_x = _split_nn(dt, spread)
        el_x = _split_nn(jnp.exp(cum), spread)
        wl_x = _split_nn(jnp.exp(cum[L - 1:L, :] - cum), spread)
        cb = []
        for g in range(SSD_G):
            Bg = xbc_ref[0, :, SSD_W + g * SSD_N:SSD_W + (g + 1) * SSD_N]
            Cg = xbc_ref[0, :, SSD_W + SSD_G * SSD_N + g * SSD_N:SSD_W + SSD_G * SSD_N + (g + 1) * SSD_N]
            cb.append((Bg, Cg, _nt(Cg, Bg)))
        for j in range(NPAIR):
            h0, h1 = 2 * j, 2 * j + 1
            sl = slice(j * LANES, (j + 1) * LANES)
            Bg, Cg, CB = cb[j // (NPAIR // SSD_G)]
            X = xbc_ref[0, :, sl]
            c0, c1 = cum[:, h0:h0 + 1], cum[:, h1:h1 + 1]
            r0, r1 = cumT[h0:h0 + 1, :], cumT[h1:h1 + 1, :]
            cl0, cl1 = cum[L - 1:L, h0:h0 + 1], cum[L - 1:L, h1:h1 + 1]
            Xt = X * dt_x[:, sl]
            M0 = CB * jnp.exp(jnp.where(mask, c0 - r0, NEG))
            M1 = CB * jnp.exp(jnp.where(mask, c1 - r1, NEG))
            Yd = jnp.where(lo, _nn(M0, Xt), _nn(M1, Xt))
            Hp = H[j]
            hin_ref[0, 0, j] = Hp
            Z = _nt(Cg, Hp)
            y_ref[0, :, sl] = Yd + el_x[:, sl] * Z + X * d_ref[j:j + 1, :]
            H[j] = jnp.where(rowlo, jnp.exp(cl0), jnp.exp(cl1)) * Hp + _tn(wl_x[:, sl] * Xt, Bg)

    vec = pl.BlockSpec((1, LANES), lambda b, c: (0, 0))
    return _pc(body, name=name, grid=(Bl, nc),
               in_specs=[pl.BlockSpec((1, L, SSD_XBC), lambda b, c: (b, c, 0)),
                         pl.BlockSpec((1, L, LANES), lambda b, c: (b, c, DT0 // LANES)),
                         vec, vec, pl.BlockSpec((NPAIR, LANES), lambda b, c: (0, 0))],
               out_specs=[pl.BlockSpec((1, L, SSD_W), lambda b, c: (b, c, 0)),
                          pl.BlockSpec((1, 1, NPAIR, LANES, LANES), lambda b, c: (b, c, 0, 0, 0))],
               out_shape=[SDS((Bl, S, SSD_W), F32), SDS((Bl, nc, NPAIR, LANES, LANES), F32)],
               scratch=[pltpu.VMEM((NPAIR, LANES, LANES), F32)])(xbc3, proj3, bias, alog, dskip)


def ssd_bwd(xbc3, proj3, hin, dy3, bias, alog, dskip, *, name):
    Bl, S, _ = xbc3.shape
    L = SSD_L
    nc = S // L

    def body(xbc_ref, sm_ref, hin_ref, dy_ref, bias_ref, alog_ref, d_ref, dx_ref, ddt_ref, dpar_ref, dd_ref, dH, ddacc):
        bb, i = pl.program_id(0), pl.program_id(1)

        @pl.when(i == 0)
        def _():
            dH[...] = jnp.zeros_like(dH)

        @pl.when((bb == 0) & (i == 0))
        def _():
            dpar_ref[...] = jnp.zeros_like(dpar_ref)
            ddacc[...] = jnp.zeros_like(ddacc)

        sm = sm_ref[0]
        dt, a, cum, cumT, mask = _ssd_common(sm, bias_ref[...], alog_ref[...])
        maskf = mask.astype(F32)
        lo = _lanes(0, SSD_P)
        rowlo = lax.broadcasted_iota(jnp.int32, (LANES, LANES), 0) < SSD_P
        lastrow = (lax.broadcasted_iota(jnp.int32, (L, 1), 0) == L - 1).astype(F32)
        dcum = jnp.zeros((L, LANES), F32)
        dcum_t = jnp.zeros((LANES, L), F32)
        ddt = jnp.zeros((L, LANES), F32)
        spread = _head_spread()
        ones = jnp.ones((L, LANES), BF16)
        ecum = jnp.exp(cum)
        wall = jnp.exp(cum[L - 1:L, :] - cum)
        dt_x = _split_nn(dt, spread)
        el_x = _split_nn(ecum, spread)
        wl_x = _split_nn(wall, spread)
        headrow = lax.broadcasted_iota(jnp.int32, (LANES, 1), 0)
        grp = []
        for g in range(SSD_G):
            Bg = xbc_ref[0, :, SSD_W + g * SSD_N:SSD_W + (g + 1) * SSD_N]
            Cg = xbc_ref[0, :, SSD_W + SSD_G * SSD_N + g * SSD_N:SSD_W + SSD_G * SSD_N + (g + 1) * SSD_N]
            grp.append(dict(B=Bg, C=Cg, CB=_nt(Cg, Bg), dB=jnp.zeros((L, SSD_N), F32), dC=jnp.zeros((L, SSD_N), F32),
                            dCB=jnp.zeros((L, L), F32)))
        for j in range(NPAIR):
            h0, h1 = 2 * j, 2 * j + 1
            sl = slice(j * LANES, (j + 1) * LANES)
            G = grp[j // (NPAIR // SSD_G)]
            Bg, Cg, CB = G["B"], G["C"], G["CB"]
            X = xbc_ref[0, :, sl]
            dY = dy_ref[0, :, sl]
            c0, c1 = cum[:, h0:h0 + 1], cum[:, h1:h1 + 1]
            r0, r1 = cumT[h0:h0 + 1, :], cumT[h1:h1 + 1, :]
            cl0, cl1 = cum[L - 1:L, h0:h0 + 1], cum[L - 1:L, h1:h1 + 1]
            oh0, oh1 = _onehot_lane(h0), _onehot_lane(h1)
            gather = _pair_gather(j)
            dtl, el, wl = dt_x[:, sl], el_x[:, sl], wl_x[:, sl]
            Xt = X * dtl
            Hp = hin_ref[0, 0, j]
            dS = dH[j]
            dX = dY * d_ref[j:j + 1, :]
            ddacc[j:j + 1, :] += _csum(dY * X)
            Z = _nt(Cg, Hp)
            dZ = dY * el
            dcum = dcum + _split_nn(dY * Z, gather) * ecum
            G["dC"] = G["dC"] + _nn(dZ, Hp)
            dHy = _tn(dZ, Cg)
            Gm = _nt(Bg, dS)
            dXt = wl * Gm
            q = _split_nn(Xt * Gm, gather) * wall
            dcum = dcum + lastrow * _csum(q) - q
            G["dB"] = G["dB"] + _nn(wl * Xt, dS)
            g0, g1 = jnp.exp(cl0), jnp.exp(cl1)
            rowsum = _nn(dS * Hp, ones)
            dg0 = _csum(jnp.where(rowlo, rowsum, 0.0))
            dg1 = _csum(jnp.where(rowlo, 0.0, rowsum))
            dcum = dcum + lastrow * ((dg0 * g0) * oh0 + (dg1 * g1) * oh1)
            dH[j] = jnp.where(rowlo, g0, g1) * dS + dHy
            for h, ch, rh, mh, oh in ((h0, c0, r0, lo, oh0), (h1, c1, r1, jnp.logical_not(lo), oh1)):
                decay = jnp.exp(jnp.where(mask, ch - rh, NEG))
                Mh = CB * decay
                dM = _nt(jnp.where(mh, dY, 0.0), Xt) * maskf
                dXt = dXt + jnp.where(mh, _tn(Mh, dY), 0.0)
                G["dCB"] = G["dCB"] + dM * decay
                Q = dM * Mh
                dcum = dcum + _rsum(Q) * oh
                dcum_t = dcum_t + (headrow == h).astype(F32) * _csum(Q)
            dX = dX + dXt * dtl
            ddt = ddt + _split_nn(dXt * X, gather)
            dx_ref[0, :, sl] = dX
        dcum = dcum - dcum_t.T
        for g in range(SSD_G):
            G = grp[g]
            dC = G["dC"] + _nn(G["dCB"], G["B"])
            dB = G["dB"] + _tn(G["dCB"], G["C"])
            dx_ref[0, :, SSD_W + g * SSD_N:SSD_W + (g + 1) * SSD_N] = dB
            dx_ref[0, :, SSD_W + SSD_G * SSD_N + g * SSD_N:SSD_W + SSD_G * SSD_N + (g + 1) * SSD_N] = dC
        r = lax.broadcasted_iota(jnp.int32, (L, L), 0)
        c = lax.broadcasted_iota(jnp.int32, (L, L), 1)
        dda = _dg((c >= r).astype(F32), dcum, 1, 0, lax.Precision.HIGHEST)
        heads = _lanes(0, SSD_HEADS)
        ddt = ddt + dda * a
        draw = jnp.where(heads, ddt * _sig(sm + bias_ref[...]), 0.0)
        ddt_ref[0] = draw
        dpar_ref[0:1, :] += _csum(draw)
        dpar_ref[1:2, :] += _csum(jnp.where(heads, dda * dt * a, 0.0))

        @pl.when((bb == Bl - 1) & (i == nc - 1))
        def _():
            acc = ddacc[...]
            lane = lax.broadcasted_iota(jnp.int32, (NPAIR, LANES), 1)
            s0 = _rsum(jnp.where(lane < SSD_P, acc, 0.0))
            s1 = _rsum(jnp.where(lane < SSD_P, 0.0, acc))
            dd_ref[...] = jnp.where(lane == 0, s0, jnp.where(lane == 1, s1, 0.0))

    vec = pl.BlockSpec((1, LANES), lambda b, i: (0, 0))
    par = pl.BlockSpec((NPAIR, LANES), lambda b, i: (0, 0))
    return _pc(body, name=name, grid=(Bl, nc),
               in_specs=[pl.BlockSpec((1, L, SSD_XBC), lambda b, i: (b, nc - 1 - i, 0)),
                         pl.BlockSpec((1, L, LANES), lambda b, i: (b, nc - 1 - i, DT0 // LANES)),
                         pl.BlockSpec((1, 1, NPAIR, LANES, LANES), lambda b, i: (b, nc - 1 - i, 0, 0, 0)),
                         pl.BlockSpec((1, L, SSD_W), lambda b, i: (b, nc - 1 - i, 0)),
                         vec, vec, par],
               out_specs=[pl.BlockSpec((1, L, SSD_XBC), lambda b, i: (b, nc - 1 - i, 0)),
                          pl.BlockSpec((1, L, LANES), lambda b, i: (b, nc - 1 - i, 0)),
                          par, par],
               out_shape=[SDS((Bl, S, SSD_XBC), F32), SDS((Bl, S, LANES), F32), SDS((NPAIR, LANES), F32),
                          SDS((NPAIR, LANES), F32)],
               scratch=[pltpu.VMEM((NPAIR, LANES, LANES), F32), pltpu.VMEM((NPAIR, LANES), F32)],
               )(xbc3, proj3, hin, dy3, bias, alog, dskip)


PE_LO, PE_MID, PE_HI = MLA_NOPE, MLA_NOPE + MLA_ROPE // 2, MLA_NOPE + MLA_ROPE
ATT_SCALE = 1.0 / math.sqrt(MLA_QK)


def _swap(x):
    first, second = _lanes(PE_LO, PE_MID), _lanes(PE_MID, PE_HI)
    half = MLA_ROPE // 2
    return jnp.where(first, -pltpu.roll(x, LANES - half, 1), jnp.where(second, pltpu.roll(x, half, 1), 0.0))


def rope_tables(pos, invf, *, name):
    T = pos.shape[0]
    tm = _tile(T, 512, 8)

    def body(pos_ref, f_ref, c_ref, s_ref):
        ang = pos_ref[...] * f_ref[...]
        pe = _lanes(PE_LO, PE_HI)
        c_ref[...] = jnp.where(pe, jnp.cos(ang), 1.0)
        s_ref[...] = jnp.where(pe, jnp.sin(ang), 0.0)

    tile = pl.BlockSpec((tm, LANES), lambda i: (i, 0))
    return _pc(body, name=name, grid=(T // tm,),
               in_specs=[pl.BlockSpec((tm, 1), lambda i: (i, 0)), pl.BlockSpec((1, LANES), lambda i: (0, 0))],
               out_specs=[tile, tile], out_shape=[SDS((T, LANES), F32)] * 2)(pos, invf)


V_ONE = MLA_V


def mla_prep_fwd(qt, kvt, proj, cs, sn, *, name):
    T = qt.shape[0]
    tm = _tile(T, 512, 8)

    def body(q_ref, k_ref, v_ref, kpe_ref, c_ref, s_ref, qo_ref, ko_ref, vo_ref):
        c, s = c_ref[...], s_ref[...]
        q = q_ref[...]
        k = k_ref[...] + kpe_ref[...]
        qo_ref[...] = ((q * c + _swap(q) * s) * ATT_SCALE).astype(BF16)
        ko_ref[...] = (k * c + _swap(k) * s).astype(BF16)
        vo_ref[...] = jnp.where(_lanes(V_ONE, V_ONE + 1), 1.0, v_ref[...]).astype(BF16)

    tile = pl.BlockSpec((tm, LANES), lambda i, h: (i, h))
    tab = pl.BlockSpec((tm, LANES), lambda i, h: (i, 0))
    return _pc(body, name=name, grid=(T // tm, MLA_H),
               in_specs=[tile, tile, pl.BlockSpec((tm, LANES), lambda i, h: (i, MLA_H + h)),
                         pl.BlockSpec((tm, LANES), lambda i, h: (i, KPE0 // LANES)), tab, tab],
               out_specs=[tile, tile, tile], out_shape=[SDS((T, MLA_H * LANES), BF16)] * 3)(qt, kvt, kvt, proj, cs, sn)


def mla_prep_bwd(dqr, dkr, cs, sn, *, name):
    T = dqr.shape[0]
    tm = _tile(T, 512, 8)

    def body(dq_ref, dk_ref, c_ref, s_ref, qo_ref, ko_ref, kpe_ref):
        c, s = c_ref[...], s_ref[...]
        dq = dq_ref[...] * ATT_SCALE
        dk = dk_ref[...]
        qo_ref[...] = (dq * c - _swap(dq * s)).astype(BF16)
        dkk = dk * c - _swap(dk * s)
        pe = _lanes(PE_LO, PE_HI)
        ko_ref[...] = jnp.where(pe, 0.0, dkk).astype(BF16)

        @pl.when(pl.program_id(1) == 0)
        def _():
            kpe_ref[...] = jnp.zeros_like(kpe_ref)

        kpe_ref[...] += jnp.where(pe, dkk, 0.0)

    tile = pl.BlockSpec((tm, LANES), lambda i, h: (i, h))
    tab = pl.BlockSpec((tm, LANES), lambda i, h: (i, 0))
    return _pc(body, name=name, grid=(T // tm, MLA_H), in_specs=[tile, tile, tab, tab], out_specs=[tile, tile, tab],
               out_shape=[SDS((T, MLA_H * LANES), BF16), SDS((T, MLA_H * LANES), BF16), SDS((T, LANES), F32)],
               )(dqr, dkr, cs, sn)


def _att_tile(S):
    return _tile(S, 512, LANES)


def _rep(x, n):
    return x if n == 1 else jnp.concatenate([x] * n, axis=1)


def _diag_mask(t, transposed=False):
    r = lax.broadcasted_iota(jnp.int32, (t, t), 0)
    c = lax.broadcasted_iota(jnp.int32, (t, t), 1)
    return (c >= r) if transposed else (c <= r)


def flash_fwd(qr, kr, vr, Bl, *, name):
    T = qr.shape[0]
    S = T // Bl
    t = _att_tile(S)
    n = S // t
    nl = t // LANES

    def body(q_ref, k_ref, v_ref, o_ref, lse_ref, lset_ref, m, acc):
        qi = pl.program_id(2)
        q = q_ref[...]
        m[...] = jnp.full_like(m, NEG)
        acc[...] = jnp.zeros_like(acc)

        def block(kj, masked):
            off = pl.multiple_of(kj * t, t)
            s = _nt(q, k_ref[pl.ds(off, t), :])
            if masked:
                s = jnp.where(_diag_mask(t), s, NEG)
            mo = m[...]
            mn = jnp.maximum(mo, jnp.max(s, axis=1, keepdims=True))
            p = jnp.exp((s - _rep(mn, nl)).astype(BF16))
            acc[...] = jnp.exp(mo - mn) * acc[...] + _nn(p, v_ref[pl.ds(off, t), :])
            m[...] = mn

        def loop(kj, c):
            block(kj, False)
            return c

        lax.fori_loop(0, qi, loop, 0)
        block(qi, True)
        a = acc[...]
        l = a[:, V_ONE:V_ONE + 1]
        o_ref[...] = jnp.where(_lanes(0, MLA_V), a / l, 0.0).astype(BF16)
        lse = m[...] + jnp.log(l)
        lse_ref[...] = lse
        lset_ref[...] = lse.T[0:8, :]

    qs = pl.BlockSpec((t, LANES), lambda b, h, qi: (b * n + qi, h))
    seq = pl.BlockSpec((S, LANES), lambda b, h, qi: (b, h))
    return _pc(body, name=name, grid=(Bl, MLA_H, n), in_specs=[qs, seq, seq],
               out_specs=[qs, qs, pl.BlockSpec((8, t), lambda b, h, qi: (b * MLA_H + h, qi))],
               out_shape=[SDS((T, MLA_H * LANES), BF16), SDS((T, MLA_H * LANES), F32), SDS((Bl * MLA_H * 8, S), F32)],
               scratch=[pltpu.VMEM((t, LANES), F32), pltpu.VMEM((t, LANES), F32)])(qr, kr, vr)


def flash_bwd_dq(qr, kr, vr, o, lse, dycat, Bl, *, name):
    T = qr.shape[0]
    S = T // Bl
    t = _att_tile(S)
    n = S // t
    nl = t // LANES
    do0 = (SSD_W + POOL_W) // LANES

    def body(q_ref, k_ref, v_ref, o_ref, lse_ref, do_ref, dq_ref, dlt_ref, acc, dl):
        qi = pl.program_id(2)
        q = q_ref[...]
        do = do_ref[...]
        dob = do.astype(BF16)
        dl[...] = jnp.broadcast_to(_rsum(do * o_ref[...].astype(F32)), (t, LANES))
        acc[...] = jnp.zeros_like(acc)

        def block(kj, masked):
            off = pl.multiple_of(kj * t, t)
            k = k_ref[pl.ds(off, t), :]
            s = _nt(q, k)
            if masked:
                s = jnp.where(_diag_mask(t), s, NEG)
            p = jnp.exp((s - _rep(lse_ref[...], nl)).astype(BF16))
            dp = _nt(dob, v_ref[pl.ds(off, t), :])
            acc[...] += _nn(p * (dp - _rep(dl[...], nl)), k)

        def loop(kj, c):
            block(kj, False)
            return c

        lax.fori_loop(0, qi, loop, 0)
        block(qi, True)
        dq_ref[...] = acc[...]
        dlt_ref[...] = dl[...].T[0:8, :]

    qs = pl.BlockSpec((t, LANES), lambda b, h, qi: (b * n + qi, h))
    seq = pl.BlockSpec((S, LANES), lambda b, h, qi: (b, h))
    return _pc(body, name=name, grid=(Bl, MLA_H, n),
               in_specs=[qs, seq, seq, qs, qs, pl.BlockSpec((t, LANES), lambda b, h, qi: (b * n + qi, do0 + h))],
               out_specs=[qs, pl.BlockSpec((8, t), lambda b, h, qi: (b * MLA_H + h, qi))],
               out_shape=[SDS((T, MLA_H * LANES), F32), SDS((Bl * MLA_H * 8, S), F32)],
               scratch=[pltpu.VMEM((t, LANES), F32), pltpu.VMEM((t, LANES), F32)])(qr, kr, vr, o, lse, dycat)


def flash_bwd_dkv(qr, kr, vr, lset, dlt, dycat, Bl, *, name):
    T = qr.shape[0]
    S = T // Bl
    t = _att_tile(S)
    n = S // t
    do0 = (SSD_W + POOL_W) // LANES

    def body(q_ref, k_ref, v_ref, lset_ref, dlt_ref, do_ref, dk_ref, dv_ref, dka, dva):
        kj = pl.program_id(2)
        k = k_ref[...]
        v = v_ref[...]
        dka[...] = jnp.zeros_like(dka)
        dva[...] = jnp.zeros_like(dva)

        def block(qi, masked):
            off = pl.multiple_of(qi * t, t)
            q = q_ref[pl.ds(off, t), :]
            do = do_ref[pl.ds(off, t), :].astype(BF16)
            st = _nt(k, q)
            if masked:
                st = jnp.where(_diag_mask(t, True), st, NEG)
            pt = jnp.exp((st - lset_ref[0:1, pl.ds(off, t)]).astype(BF16))
            dst = pt * (_nt(v, do) - dlt_ref[0:1, pl.ds(off, t)])
            dva[...] += _nn(pt, do)
            dka[...] += _nn(dst, q)

        def loop(qi, c):
            block(qi, False)
            return c

        block(kj, True)
        lax.fori_loop(kj + 1, n, loop, 0)
        dk_ref[...] = dka[...]
        dv_ref[...] = dva[...].astype(BF16)

    ks = pl.BlockSpec((t, LANES), lambda b, h, kj: (b * n + kj, h))
    seq = pl.BlockSpec((S, LANES), lambda b, h, kj: (b, h))
    rows = pl.BlockSpec((8, S), lambda b, h, kj: (b * MLA_H + h, 0))
    return _pc(body, name=name, grid=(Bl, MLA_H, n),
               in_specs=[seq, ks, ks, rows, rows, pl.BlockSpec((S, LANES), lambda b, h, kj: (b, do0 + h))],
               out_specs=[ks, ks], out_shape=[SDS((T, MLA_H * LANES), F32), SDS((T, MLA_H * LANES), BF16)],
               scratch=[pltpu.VMEM((t, LANES), F32), pltpu.VMEM((t, LANES), F32)])(qr, kr, vr, lset, dlt, dycat)


def _rows2d(a):
    return a.reshape(-1, a.shape[-1])


def add2(a, b, *, name):
    shp = a.shape
    a2, b2 = _rows2d(a), _rows2d(b)
    R, C = a2.shape
    tm = _tile(R, 512, 8)

    def body(a_ref, b_ref, o_ref, ob_ref):
        s = a_ref[...] + b_ref[...]
        o_ref[...] = s
        ob_ref[...] = s.astype(BF16)

    blk = pl.BlockSpec((tm, C), lambda i: (i, 0))
    o, ob = _pc(body, name=name, grid=(R // tm,), in_specs=[blk, blk], out_specs=[blk, blk],
                out_shape=[SDS((R, C), F32), SDS((R, C), BF16)])(a2, b2)
    return o.reshape(shp), ob.reshape(shp)


def adamw(w, m, v, parts, *, name):
    shp = w.shape
    w2, m2, v2 = _rows2d(w), _rows2d(m), _rows2d(v)
    R, C = w2.shape
    p3 = [p.reshape(p.shape[0], R, C) for p in parts]
    tm = _tile(R, 256, 8)
    bc1 = 1.0 - ADAM_B1 ** ADAM_STEP
    bc2 = 1.0 - ADAM_B2 ** ADAM_STEP

    def body(w_ref, m_ref, v_ref, *refs):
        p_refs, (g_ref, d_ref, nm_ref, nv_ref) = refs[:len(p3)], refs[len(p3):]
        g = None
        for p_ref, p in zip(p_refs, p3):
            for k in range(p.shape[0]):
                term = p_ref[k].astype(F32)
                g = term if g is None else g + term
        mm_ = ADAM_B1 * m_ref[...] + (1.0 - ADAM_B1) * g
        vv = ADAM_B2 * v_ref[...] + (1.0 - ADAM_B2) * (g * g)
        g_ref[...] = g
        nm_ref[...] = mm_
        nv_ref[...] = vv
        d_ref[...] = -ADAM_LR * ((mm_ / bc1) / (jnp.sqrt(vv / bc2) + ADAM_EPS) + ADAM_WD * w_ref[...])

    blk = pl.BlockSpec((tm, C), lambda i: (i, 0))
    pspecs = [pl.BlockSpec((p.shape[0], tm, C), lambda i: (0, i, 0)) for p in p3]
    outs = _pc(body, name=name, grid=(R // tm,), in_specs=[blk, blk, blk] + pspecs,
               out_specs=[blk] * 4, out_shape=[SDS((R, C), F32)] * 4)(w2, m2, v2, *p3)
    return [o.reshape(shp) for o in outs]


ANY = pl.BlockSpec(memory_space=pl.ANY)


def _place():
    return lax.axis_index("x"), lax.axis_index("y"), lax.axis_index("c")


def all_gather_many(xs, *, name):
    n = len(xs)

    def body(*refs):
        x_refs, o_refs = refs[:n], refs[n:2 * n]
        send_sems, recv_sems, local_sems = refs[2 * n:]
        x, y, c = _place()
        me, sibling = (x, y, c), (x, y, 1 - c)
        chips = [(1 - x, y), (x, 1 - y), (1 - x, 1 - y)]

        def rows(a, p):
            return o_refs[a].at[4 * p[0] + 2 * p[1] + p[2]]

        def copy(a, k, block, to, src=None):
            return pltpu.make_async_remote_copy(
                src_ref=rows(a, block) if src is None else src, dst_ref=rows(a, block),
                send_sem=send_sems.at[7 * a + k], recv_sem=recv_sems.at[7 * a + k], device_id=to, device_id_type=MESH)

        mine = [pltpu.make_async_copy(x_refs[a], rows(a, me), local_sems.at[a]) for a in range(n)]
        for cp in mine:
            cp.start()
        first = []
        for a in range(n):
            first.append(copy(a, 0, me, sibling, src=x_refs[a]))
            first += [copy(a, 1 + j, me, (*chip, c), src=x_refs[a]) for j, chip in enumerate(chips)]
        for cp in first:
            cp.start()
        passed = []
        for j, chip in enumerate(chips):
            for a in range(n):
                copy(a, 1 + j, (*chip, c), me).wait_recv()
                cp = copy(a, 4 + j, (*chip, c), sibling)
                cp.start()
                passed.append(cp)
        for a in range(n):
            copy(a, 0, sibling, me).wait_recv()
            for j, chip in enumerate(chips):
                copy(a, 4 + j, (*chip, 1 - c), me).wait_recv()
        for cp in first + passed:
            cp.wait_send()
        for cp in mine:
            cp.wait()

    return pl.pallas_call(
        body, name=name, in_specs=[ANY] * n, out_specs=[ANY] * n,
        out_shape=[SDS((N_DEV,) + a.shape, a.dtype) for a in xs],
        scratch_shapes=[pltpu.SemaphoreType.DMA((7 * n,)), pltpu.SemaphoreType.DMA((7 * n,)), pltpu.SemaphoreType.DMA((n,))],
    )(*xs)


def exchange_sibling(gs, *, name):
    n = len(gs)

    def body(*refs):
        g_refs, o_refs = refs[:n], refs[n:2 * n]
        send_sems, recv_sems = refs[2 * n:]
        x, y, c = _place()
        cps = []
        for a in range(n):
            for k in range(4):
                cps.append(pltpu.make_async_remote_copy(
                    src_ref=g_refs[a].at[2 * k + (1 - c)], dst_ref=o_refs[a].at[k],
                    send_sem=send_sems.at[4 * a + k], recv_sem=recv_sems.at[4 * a + k], device_id=(x, y, 1 - c), device_id_type=MESH))
        for cp in cps:
            cp.start()
        for cp in cps:
            cp.wait()

    return pl.pallas_call(
        body, name=name, in_specs=[ANY] * n, out_specs=[ANY] * n,
        out_shape=[SDS((4,) + g.shape[1:], g.dtype) for g in gs],
        scratch_shapes=[pltpu.SemaphoreType.DMA((4 * n,)), pltpu.SemaphoreType.DMA((4 * n,))],
    )(*gs)


def exchange_chips(As, *, name):
    n = len(As)

    def body(*refs):
        a_refs, o_refs = refs[:n], refs[n:2 * n]
        send_sems, recv_sems = refs[2 * n:]
        x, y, c = _place()
        chips = [(1 - x, y), (x, 1 - y), (1 - x, 1 - y)]
        cps = []
        for a in range(n):
            for j, chip in enumerate(chips):
                cps.append(pltpu.make_async_remote_copy(
                    src_ref=a_refs[a].at[2 * chip[0] + chip[1]], dst_ref=o_refs[a].at[j],
                    send_sem=send_sems.at[3 * a + j], recv_sem=recv_sems.at[3 * a + j], device_id=(*chip, c), device_id_type=MESH))
        for cp in cps:
            cp.start()
        for cp in cps:
            cp.wait()

    return pl.pallas_call(
        body, name=name, in_specs=[ANY] * n, out_specs=[ANY] * n,
        out_shape=[SDS((3,) + a.shape[1:], a.dtype) for a in As],
        scratch_shapes=[pltpu.SemaphoreType.DMA((3 * n,)), pltpu.SemaphoreType.DMA((3 * n,))],
    )(*As)


def _owner_major(full, axis):
    shp = full.shape
    r = full.reshape(shp[:axis] + (N_DEV, shp[axis] // N_DEV) + shp[axis + 1:])
    return jnp.moveaxis(r, axis, 0)


def _from_owner_major(g8, axis):
    r = jnp.moveaxis(g8, 0, axis)
    shp = r.shape
    return r.reshape(shp[:axis] + (shp[axis] * shp[axis + 1],) + shp[axis + 2:])


def _perm_w_in(w):
    z = jnp.zeros((w.shape[0], LANES), w.dtype)
    dt = jnp.pad(w[:, 2560:2576], ((0, 0), (0, LANES - SSD_HEADS)))
    kpe = jnp.pad(w[:, 3728:3760], ((0, 0), (PE_LO, LANES - PE_HI)))
    return jnp.concatenate([w[:, 0:1024], w[:, 1024:2560], w[:, 2576:3088], w[:, 3088:3472], z, w[:, 3472:3728], dt, kpe], axis=1)


def _unperm_w_in(g):
    return jnp.concatenate([g[:, Z0:Z0 + 1024], g[:, XBC0:XBC0 + 1536], g[:, DT0:DT0 + SSD_HEADS], g[:, U0:U0 + 512],
                            g[:, CQ0:CQ0 + 384], g[:, CKV0:CKV0 + 256], g[:, KPE0 + PE_LO:KPE0 + PE_HI]], axis=1)


def _perm_w_uq(w):
    return jnp.pad(w.reshape(MLA_QR, MLA_H, MLA_QK), ((0, 0), (0, 0), (0, LANES - MLA_QK))).reshape(MLA_QR, MLA_H * LANES)


def _unperm_w_uq(g):
    return g.reshape(MLA_QR, MLA_H, LANES)[:, :, :MLA_QK].reshape(MLA_QR, MLA_H * MLA_QK)


def _perm_w_ukv(w):
    w3 = w.reshape(MLA_KVR, MLA_H, MLA_NOPE + MLA_V)
    pad = ((0, 0), (0, 0), (0, LANES - MLA_NOPE))
    k = jnp.pad(w3[:, :, :MLA_NOPE], pad).reshape(MLA_KVR, MLA_H * LANES)
    v = jnp.pad(w3[:, :, MLA_NOPE:], pad).reshape(MLA_KVR, MLA_H * LANES)
    return jnp.concatenate([k, v], axis=1)


def _unperm_w_ukv(g):
    k = g[:, :MLA_H * LANES].reshape(MLA_KVR, MLA_H, LANES)[:, :, :MLA_NOPE]
    v = g[:, MLA_H * LANES:].reshape(MLA_KVR, MLA_H, LANES)[:, :, :MLA_V]
    return jnp.concatenate([k, v], axis=2).reshape(MLA_KVR, MLA_H * (MLA_NOPE + MLA_V))


def _perm_w_out(w):
    m = jnp.pad(w[SSD_W + POOL_W:].reshape(MLA_H, MLA_V, D), ((0, 0), (0, LANES - MLA_V), (0, 0))).reshape(MLA_H * LANES, D)
    return jnp.concatenate([w[:SSD_W + POOL_W], m], axis=0)


def _unperm_w_out(g):
    m = g[SSD_W + POOL_W:].reshape(MLA_H, LANES, D)[:, :MLA_V].reshape(MLA_H * MLA_V, D)
    return jnp.concatenate([g[:SSD_W + POOL_W], m], axis=0)


def _lane_pad(v):
    return jnp.pad(v.reshape(1, -1), ((0, 0), (0, LANES - v.shape[-1])))


SMALL = ("attn_norm", "ssd_conv_b", "ssd_dt_bias", "ssd_a_log", "ssd_d", "ssd_norm", "pool_w", "pool_scale",
         "mla_q_norm", "mla_kv_norm", "ffn_norm", "ffn_conv_b", "final_norm")
SHARDED = {"w_in": 2, "ssd_conv_w": 2, "mla_w_uq": 2, "mla_w_ukv": 2, "w_out": 1, "ffn_w_up": 2, "ffn_conv_w": 2,
           "ffn_w_down": 1}
ALL_W = ("attn_norm", "w_in", "ssd_conv_w", "ssd_conv_b", "ssd_dt_bias", "ssd_a_log", "ssd_d", "ssd_norm", "pool_w",
         "pool_scale", "mla_q_norm", "mla_w_uq", "mla_kv_norm", "mla_w_ukv", "w_out", "ffn_norm", "ffn_w_up",
         "ffn_conv_w", "ffn_conv_b", "ffn_w_down", "final_norm")


def _pack_small(d):
    rows, layout = [], []
    for k in SMALL:
        a = d[k].reshape(-1)
        n = a.shape[0]
        r = -(-n // LANES)
        rows.append(jnp.pad(a, (0, r * LANES - n)).reshape(r, LANES))
        layout.append((k, n, r, d[k].shape))
    buf = jnp.concatenate(rows, axis=0)
    pad = (-buf.shape[0]) % 8
    return jnp.pad(buf, ((0, pad), (0, 0))), layout


def _unpack_small(buf, layout):
    out, r0 = {}, 0
    for k, n, r, shp in layout:
        out[k] = buf[r0:r0 + r].reshape(-1)[:n].reshape(shp)
        r0 += r
    return out


def _layer_fwd(l, x, W, cs, sn, Bl):
    T = x.shape[0]
    S = T // Bl
    n = f"l{l}_"
    h = rms_fwd(x, W["attn_norm"], name=n + "attn_norm")
    proj = mm(h, W["w_in"], name=n + "w_in")
    proj3 = proj.reshape(Bl, S, PW)
    xbc3 = conv_silu_fwd(proj3, W["ssd_conv_w"], W["ssd_conv_b"], name=n + "ssd_conv")
    y3, hin = ssd_fwd(xbc3, proj3, W["ssd_dt_bias"], W["ssd_a_log"], W["ssd_d"], name=n + "ssd_scan")
    y = y3.reshape(T, SSD_W)
    y_ssd = gated_rms_fwd(y, proj, W["ssd_norm"], name=n + "ssd_gate_norm")
    y_pool = pool_fwd(proj3, W["pool_w"], W["pool_scale"], name=n + "pool").reshape(T, POOL_W)
    qn = rms_fwd(proj, W["mla_q_norm"], col0=CQ0, width=MLA_QR, name=n + "q_norm")
    kvn = rms_fwd(proj, W["mla_kv_norm"], col0=CKV0, width=MLA_KVR, name=n + "kv_norm")
    qt = mm(qn, W["mla_w_uq"], name=n + "w_uq")
    kvt = mm(kvn, W["mla_w_ukv"], name=n + "w_ukv")
    qr, kr, vr = mla_prep_fwd(qt, kvt, proj, cs, sn, name=n + "rope")
    o, lse, lset = flash_fwd(qr, kr, vr, Bl, name=n + "attn")
    ycat = jnp.concatenate([y_ssd, y_pool, o], axis=1)
    x1 = mm(ycat, W["w_out"], add=x, name=n + "w_out")
    h2 = rms_fwd(x1, W["ffn_norm"], name=n + "ffn_norm")
    pre = mm(h2, W["ffn_w_up"], name=n + "w_up")
    pre3 = pre.reshape(Bl, S, 2 * DFF)
    act = ffn_act_fwd(pre3, W["ffn_conv_w"], W["ffn_conv_b"], name=n + "ffn_act").reshape(T, DFF)
    x2 = mm(act, W["ffn_w_down"], add=x1, name=n + "w_down")
    saved = dict(x=x, h=h, proj=proj, xbc3=xbc3, hin=hin, y=y, qn=qn, kvn=kvn, vr=vr, qr=qr, kr=kr, o=o, lse=lse, lset=lset,
                 ycat=ycat, x1=x1, h2=h2, pre3=pre3, act=act)
    return x2, saved


def _layer_bwd(l, dx2, dx2b, W, sv, cs, sn, Bl):
    T = dx2.shape[0]
    S = T // Bl
    n = f"l{l}_b_"
    g = {}
    g["ffn_w_down"] = mm(sv["act"], dx2b, ta=True, name=n + "dw_down")
    dact = mm(dx2b, W["ffn_w_down"], tb=True, name=n + "dact")
    dpg, dpv, dwg, dwv, dbg, dbv = ffn_act_bwd(sv["pre3"], W["ffn_conv_w"], W["ffn_conv_b"], dact.reshape(Bl, S, DFF),
                                               name=n + "ffn_act")
    g["ffn_conv_w"] = jnp.concatenate([dwg, dwv], axis=1)
    g["ffn_conv_b"] = jnp.concatenate([dbg, dbv], axis=1)
    dpg, dpv = dpg.reshape(T, DFF), dpv.reshape(T, DFF)
    g["ffn_w_up"] = jnp.concatenate([mm(sv["h2"], dpg, ta=True, name=n + "dw_up_g"),
                                     mm(sv["h2"], dpv, ta=True, name=n + "dw_up_v")], axis=1)
    dh2 = mm(dpg, W["ffn_w_up"], tb=True, name=n + "dh2_g")
    dh2 = mm(dpv, W["ffn_w_up"], tb=True, b_k0=DFF, add=dh2, name=n + "dh2_v")
    dx1, dx1b, g["ffn_norm"] = rms_bwd(sv["x1"], W["ffn_norm"], dh2, add=dx2, name=n + "ffn_norm")
    g["w_out"] = mm(sv["ycat"], dx1b, ta=True, name=n + "dw_out")
    dycat = mm(dx1b, W["w_out"], tb=True, name=n + "dycat")
    proj, proj3 = sv["proj"], sv["proj"].reshape(Bl, S, PW)
    dy, dz, g["ssd_norm"] = gated_rms_bwd(sv["y"], proj, W["ssd_norm"], dycat, name=n + "ssd_gate_norm")
    dxa, ddt, dpar, dd = ssd_bwd(sv["xbc3"], proj3, sv["hin"], dy.reshape(Bl, S, SSD_W), W["ssd_dt_bias"], W["ssd_a_log"],
                                 W["ssd_d"], name=n + "ssd_scan")
    g["ssd_dt_bias"] = dpar[0, :SSD_HEADS]
    g["ssd_a_log"] = dpar[1, :SSD_HEADS]
    g["ssd_d"] = dd[:, :2].reshape(SSD_HEADS)
    dxbc, g["ssd_conv_w"], g["ssd_conv_b"] = conv_silu_bwd(proj3, W["ssd_conv_w"], W["ssd_conv_b"], dxa, name=n + "ssd_conv")
    du, g["pool_w"], g["pool_scale"] = pool_bwd(proj3, W["pool_w"], W["pool_scale"], dycat.reshape(Bl, S, YCAT), name=n + "pool")
    dqr, dlt = flash_bwd_dq(sv["qr"], sv["kr"], sv["vr"], sv["o"], sv["lse"], dycat, Bl, name=n + "attn_dq")
    dkr, dv = flash_bwd_dkv(sv["qr"], sv["kr"], sv["vr"], sv["lset"], dlt, dycat, Bl, name=n + "attn_dkv")
    dqt, dkt, dkpe = mla_prep_bwd(dqr, dkr, cs, sn, name=n + "rope")
    g["mla_w_ukv"] = jnp.concatenate([mm(sv["kvn"], dkt, ta=True, name=n + "dw_uk"),
                                      mm(sv["kvn"], dv, ta=True, name=n + "dw_uv")], axis=1)
    dkvn = mm(dkt, W["mla_w_ukv"], tb=True, name=n + "dkvn_k")
    dkvn = mm(dv, W["mla_w_ukv"], tb=True, b_k0=MLA_H * LANES, add=dkvn, name=n + "dkvn_v")
    g["mla_w_uq"] = mm(sv["qn"], dqt, ta=True, name=n + "dw_uq")
    dqn = mm(dqt, W["mla_w_uq"], tb=True, name=n + "dqn")
    dcq, g["mla_q_norm"] = rms_bwd(proj, W["mla_q_norm"], dqn, col0=CQ0, width=MLA_QR, name=n + "q_norm")
    dckv, g["mla_kv_norm"] = rms_bwd(proj, W["mla_kv_norm"], dkvn, col0=CKV0, width=MLA_KVR, name=n + "kv_norm")
    dproj = jnp.concatenate([dz, dxbc.reshape(T, SSD_XBC), du.reshape(T, POOL_W), dcq, jnp.zeros((T, LANES), F32), dckv,
                             ddt.reshape(T, LANES), dkpe], axis=1).astype(BF16)
    g["w_in"] = mm(sv["h"], dproj, ta=True, name=n + "dw_in")
    dh = mm(dproj, W["w_in"], tb=True, name=n + "dh")
    dx, dxb, g["attn_norm"] = rms_bwd(sv["x"], W["attn_norm"], dh, add=dx1, name=n + "attn_norm")
    return dx, dxb, g


def kernel(x, positions, attn_norm, w_in, ssd_conv_w, ssd_conv_b, ssd_dt_bias, ssd_a_log, ssd_d, ssd_norm, pool_w, pool_scale, mla_q_norm, mla_w_uq, mla_kv_norm, mla_w_ukv, w_out, ffn_norm, ffn_w_up, ffn_conv_w, ffn_conv_b, ffn_w_down, final_norm, loss_target, m_attn_norm, m_w_in, m_ssd_conv_w, m_ssd_conv_b, m_ssd_dt_bias, m_ssd_a_log, m_ssd_d, m_ssd_norm, m_pool_w, m_pool_scale, m_mla_q_norm, m_mla_w_uq, m_mla_kv_norm, m_mla_w_ukv, m_w_out, m_ffn_norm, m_ffn_w_up, m_ffn_conv_w, m_ffn_conv_b, m_ffn_w_down, m_final_norm, v_attn_norm, v_w_in, v_ssd_conv_w, v_ssd_conv_b, v_ssd_dt_bias, v_ssd_a_log, v_ssd_d, v_ssd_norm, v_pool_w, v_pool_scale, v_mla_q_norm, v_mla_w_uq, v_mla_kv_norm, v_mla_w_ukv, v_w_out, v_ffn_norm, v_ffn_w_up, v_ffn_conv_w, v_ffn_conv_b, v_ffn_w_down, v_final_norm):
    a = locals()
    Wt = {k: a[k] for k in ALL_W}
    Mo = {k: a["m_" + k] for k in ALL_W}
    Vo = {k: a["v_" + k] for k in ALL_W}
    Bl, S, _ = x.shape
    T = Bl * S

    names = list(SHARDED)
    conv = ("ssd_conv_w", "ffn_conv_w")
    shards = [Wt[k] if k in conv else Wt[k].astype(BF16) for k in names]
    gathered = all_gather_many(shards, name="gather_weights")
    full = {k: _from_owner_major(g8, SHARDED[k]) for k, g8 in zip(names, gathered)}

    pos = positions.astype(F32).reshape(T, 1)
    inv_freq = ROPE_THETA ** (-jnp.arange(0, MLA_ROPE, 2, dtype=F32) / MLA_ROPE)
    invf = jnp.pad(jnp.concatenate([inv_freq, inv_freq]), (PE_LO, LANES - PE_HI)).reshape(1, LANES)

    layers = []
    for l in range(DEPTH):
        layers.append({
            "attn_norm": attn_norm[l].reshape(1, D), "w_in": _perm_w_in(full["w_in"][l]),
            "ssd_conv_w": full["ssd_conv_w"][l], "ssd_conv_b": ssd_conv_b[l].reshape(1, SSD_XBC),
            "ssd_dt_bias": _lane_pad(ssd_dt_bias[l]), "ssd_a_log": _lane_pad(ssd_a_log[l]),
            "ssd_d": jnp.repeat(ssd_d[l].reshape(NPAIR, 2), SSD_P, axis=1), "ssd_norm": ssd_norm[l].reshape(1, SSD_W),
            "pool_w": pool_w[l].astype(BF16), "pool_scale": pool_scale[l].reshape(1, POOL_W),
            "mla_q_norm": mla_q_norm[l].reshape(1, MLA_QR), "mla_w_uq": _perm_w_uq(full["mla_w_uq"][l]),
            "mla_kv_norm": mla_kv_norm[l].reshape(1, MLA_KVR), "mla_w_ukv": _perm_w_ukv(full["mla_w_ukv"][l]),
            "w_out": _perm_w_out(full["w_out"][l]), "ffn_norm": ffn_norm[l].reshape(1, D),
            "ffn_w_up": full["ffn_w_up"][l], "ffn_conv_w": full["ffn_conv_w"][l],
            "ffn_conv_b": ffn_conv_b[l].reshape(1, 2 * DFF), "ffn_w_down": full["ffn_w_down"][l]})

    cs, sn = rope_tables(pos, invf, name="rope_tables")
    xc = x.reshape(T, D)
    saved = []
    for l in range(DEPTH):
        xc, sv = _layer_fwd(l, xc, layers[l], cs, sn, Bl)
        saved.append(sv)
    dx, dxb, g_final, loss_part = final_loss(xc, final_norm.reshape(1, D), loss_target.reshape(T, D), name="final_loss")
    grads = [None] * DEPTH
    for l in reversed(range(DEPTH)):
        dx, dxb, grads[l] = _layer_bwd(l, dx, dxb, layers[l], saved[l], cs, sn, Bl)
    loss = lax.psum(loss_part[0, 0], AXES)

    unperm = {"w_in": _unperm_w_in, "mla_w_uq": _unperm_w_uq, "mla_w_ukv": _unperm_w_ukv, "w_out": _unperm_w_out}
    part = {}
    for k in ALL_W:
        if k == "final_norm":
            part[k] = g_final.reshape(D)
        else:
            shp = list(Wt[k].shape[1:])
            if k in SHARDED:
                shp[SHARDED[k] - 1] *= N_DEV
            part[k] = jnp.stack([unperm.get(k, lambda t: t)(grads[l][k]).reshape(shp) for l in range(DEPTH)])

    xi, yi, ci = _place()
    chip = 2 * xi + yi
    g8 = [_owner_major(part[k], SHARDED[k]) for k in names]
    r1 = exchange_sibling(g8, name="rs_sibling")
    keep = [lax.dynamic_index_in_dim(g.reshape((4, 2) + g.shape[1:]), ci, 1, keepdims=False) for g in g8]
    sums = [add2(kp, r, name="rs_add_" + k) for k, kp, r in zip(names, keep, r1)]
    r2 = exchange_chips([sb for _, sb in sums], name="rs_chips")
    out_g, out_d, out_m, out_v = {}, {}, {}, {}
    for k, (s32, _), r in zip(names, sums, r2):
        own = lax.dynamic_index_in_dim(s32, chip, 0, keepdims=True)
        out_g[k], out_d[k], out_m[k], out_v[k] = adamw(Wt[k], Mo[k], Vo[k], [own, r], name="adamw_" + k)

    pg, layout = _pack_small(part)
    pw, _ = _pack_small(Wt)
    pm, _ = _pack_small(Mo)
    pv, _ = _pack_small(Vo)
    (pg8,) = all_gather_many([pg], name="gather_small_grads")
    sg, sd, sm, sv_ = adamw(pw, pm, pv, [pg8], name="adamw_small")
    for dst, buf in ((out_g, sg), (out_d, sd), (out_m, sm), (out_v, sv_)):
        dst.update(_unpack_small(buf, layout))

    return (loss, dx.reshape(Bl, S, D), *[out_g[k] for k in ALL_W], *[out_d[k] for k in ALL_W],
            *[out_m[k] for k in ALL_W], *[out_v[k] for k in ALL_W])
```

```python
import functools
import math

import jax
import jax.numpy as jnp
from jax import lax
from jax.experimental import pallas as pl
from jax.experimental.pallas import tpu as pltpu

F32, BF16 = jnp.float32, jnp.bfloat16
SDS = jax.ShapeDtypeStruct
MESH = pl.DeviceIdType.MESH
AXES = ("x", "y", "c")
N_DEV = 8

D = 1024
EPS = 1e-6
SSD_HEADS, SSD_P, SSD_W, SSD_G, SSD_N, SSD_K, SSD_L, SSD_XBC = 16, 64, 1024, 2, 128, 4, 128, 1536
POOL_G, POOL_D, POOL_W, POOL_WIN = 4, 128, 512, (2, 4, 8, 16)
MLA_H, MLA_QR, MLA_KVR, MLA_NOPE, MLA_ROPE, MLA_V, MLA_QK = 8, 384, 256, 64, 32, 64, 96
ROPE_THETA = 10000.0
MIX = 2048
DFF, FFN_K = 2816, 3
DEPTH = 2
ADAM_LR, ADAM_B1, ADAM_B2, ADAM_EPS, ADAM_WD, ADAM_STEP = 0.001, 0.9, 0.999, 1e-08, 0.01, 10

Z0, XBC0, U0, CQ0, CKV0, DT0, KPE0, PW = 0, 1024, 2560, 3072, 3584, 3840, 3968, 4096
LANES = 128
YCAT = SSD_W + POOL_W + MLA_H * LANES
NEG = -1e30
VMEM_LIMIT = 56 * 1024 * 1024
MM_ROW_TILE, MM_LANE_TILE, MM_FULL_K = 1024, 1408, 2816


def _tile(n, pref, mult):
    if n <= pref:
        return n
    for d in range(pref, 0, -mult):
        if d % mult == 0 and n % d == 0:
            return d
    return n


def _dg(a, b, ca, cb, prec=None):
    return lax.dot_general(a, b, (((ca,), (cb,)), ((), ())), preferred_element_type=F32, precision=prec)


def _nn(a, b):
    return _dg(a.astype(BF16), b.astype(BF16), 1, 0)


def _nt(a, b):
    return _dg(a.astype(BF16), b.astype(BF16), 1, 1)


def _tn(a, b):
    return _dg(a.astype(BF16), b.astype(BF16), 0, 0)


def _sig(x):
    return jax.nn.sigmoid(x)


def _silu(x):
    return x * _sig(x)


def _dsilu(x):
    s = _sig(x)
    return s * (1.0 + x * (1.0 - s))


ANY = pl.BlockSpec(memory_space=pl.ANY)


def _pc(body, *, name, grid, in_specs, out_specs, out_shape, scratch=(), comm=None):
    params = pltpu.CompilerParams(vmem_limit_bytes=VMEM_LIMIT)
    if comm is None:
        return pl.pallas_call(body, name=name, grid=grid, in_specs=in_specs, out_specs=out_specs, out_shape=out_shape,
                              scratch_shapes=list(scratch), compiler_params=params)
    single = not isinstance(out_shape, (list, tuple))
    o_specs = [out_specs] if single else list(out_specs)
    o_shape = [out_shape] if single else list(out_shape)
    ni, no, ns = len(in_specs), len(o_specs), len(scratch)
    nci, nco = len(comm["ins"]), len(comm["out_shape"])

    def fused(*refs):
        ins, cins = refs[:ni], refs[ni:ni + nci]
        outs, couts = refs[ni + nci:ni + nci + no], refs[ni + nci + no:ni + nci + no + nco]
        scr = refs[ni + nci + no + nco:ni + nci + no + nco + ns]
        send_sems, recv_sems = refs[-2:]
        copies = comm["copies"](cins, couts, send_sems, recv_sems)
        first = functools.reduce(jnp.logical_and, [pl.program_id(d) == 0 for d in range(len(grid))])
        last = functools.reduce(jnp.logical_and, [pl.program_id(d) == grid[d] - 1 for d in range(len(grid))])

        @pl.when(first)
        def _():
            for cp in copies:
                cp.start()

        body(*ins, *outs, *scr)

        @pl.when(last)
        def _():
            for cp in copies:
                cp.wait()

    call = pl.pallas_call(
        fused, name=name, grid=grid, in_specs=list(in_specs) + [ANY] * nci, out_specs=o_specs + [ANY] * nco,
        out_shape=o_shape + list(comm["out_shape"]),
        scratch_shapes=list(scratch) + [pltpu.SemaphoreType.DMA((comm["sems"],)), pltpu.SemaphoreType.DMA((comm["sems"],))],
        input_output_aliases={ni + a: no + a for a in range(nci)} if comm.get("alias") else {},
        compiler_params=params)

    def run(*args):
        res = call(*args, *comm["ins"])
        comm["result"] = list(res[no:])
        return res[0] if single else list(res[:no])

    return run


def _rsum(x):
    return jnp.sum(x, axis=1, keepdims=True)


def _csum(x):
    return jnp.sum(x, axis=0, keepdims=True)


def mm(a, b, *, ta=False, tb=False, add=None, out_dtype=F32, b_k0=0, comm=None, name):
    M, K = (a.shape[1], a.shape[0]) if ta else a.shape
    N = b.shape[0] if tb else b.shape[1]
    assert tb or b_k0 == 0
    tm = _tile(M, MM_LANE_TILE, LANES) if ta else _tile(M, MM_ROW_TILE, 8)
    tn = _tile(N, MM_LANE_TILE, LANES)
    if ta:
        tk = _tile(K, MM_ROW_TILE, 8)
    else:
        tk = K if K <= MM_FULL_K else _tile(K, 2048, LANES)
    nk = K // tk

    def body(*refs):
        if add is None:
            a_ref, b_ref, o_ref = refs[:3]
        else:
            a_ref, b_ref, add_ref, o_ref = refs[:4]
        part = _dg(a_ref[...].astype(BF16), b_ref[...].astype(BF16), 0 if ta else 1, 1 if tb else 0)

        def finish(r):
            if add is not None:
                r = r + add_ref[...].astype(F32)
            o_ref[...] = r.astype(out_dtype)

        if nk == 1:
            finish(part)
            return
        acc = refs[-1]
        k = pl.program_id(2)

        @pl.when(k == 0)
        def _():
            acc[...] = part

        @pl.when(k > 0)
        def _():
            acc[...] += part

        @pl.when(k == nk - 1)
        def _():
            finish(acc[...])

    a_spec = pl.BlockSpec((tk, tm), lambda i, j, k: (k, i)) if ta else pl.BlockSpec((tm, tk), lambda i, j, k: (i, k))
    assert b_k0 % tk == 0
    kb0 = b_k0 // tk
    b_spec = pl.BlockSpec((tn, tk), lambda i, j, k: (j, kb0 + k)) if tb else pl.BlockSpec((tk, tn), lambda i, j, k: (k, j))
    o_spec = pl.BlockSpec((tm, tn), lambda i, j, k: (i, j))
    ins, specs = [a, b], [a_spec, b_spec]
    if add is not None:
        ins.append(add)
        specs.append(o_spec)
    return _pc(body, name=name, grid=(M // tm, N // tn, nk), in_specs=specs, out_specs=o_spec,
               out_shape=SDS((M, N), out_dtype), scratch=[pltpu.VMEM((tm, tn), F32)] if nk > 1 else [], comm=comm)(*ins)


def rms_fwd(x, g, *, col0=0, width=None, name):
    T = x.shape[0]
    W = width or x.shape[1]
    tm = _tile(T, 512, 8)

    def body(x_ref, g_ref, o_ref):
        v = x_ref[...]
        r = lax.rsqrt(jnp.mean(v * v, axis=1, keepdims=True) + EPS)
        o_ref[...] = ((v * r) * g_ref[...]).astype(BF16)

    return _pc(body, name=name, grid=(T // tm,),
               in_specs=[pl.BlockSpec((tm, W), lambda i: (i, col0 // W)), pl.BlockSpec((1, W), lambda i: (0, 0))],
               out_specs=pl.BlockSpec((tm, W), lambda i: (i, 0)), out_shape=SDS((T, W), BF16))(x, g)


def rms_bwd(x, g, dh, *, col0=0, width=None, add=None, name):
    T = x.shape[0]
    W = width or x.shape[1]
    tm = _tile(T, 512, 8)

    def body(*refs):
        if add is None:
            x_ref, g_ref, dh_ref, dx_ref, dg_ref = refs
        else:
            x_ref, g_ref, dh_ref, add_ref, dx_ref, dxb_ref, dg_ref = refs
        v = x_ref[...]
        r = lax.rsqrt(jnp.mean(v * v, axis=1, keepdims=True) + EPS)
        xh = v * r
        d = dh_ref[...].astype(F32)
        dxh = d * g_ref[...]
        dx = r * (dxh - xh * jnp.mean(dxh * xh, axis=1, keepdims=True))
        if add is not None:
            dx = dx + add_ref[...]
            dxb_ref[...] = dx.astype(BF16)
        dx_ref[...] = dx

        @pl.when(pl.program_id(0) == 0)
        def _():
            dg_ref[...] = jnp.zeros_like(dg_ref)

        dg_ref[...] += _csum(d * xh)

    row = pl.BlockSpec((tm, W), lambda i: (i, 0))
    vec = pl.BlockSpec((1, W), lambda i: (0, 0))
    ins = [x, g, dh] + ([] if add is None else [add])
    specs = [pl.BlockSpec((tm, W), lambda i: (i, col0 // W)), vec, row] + ([] if add is None else [row])
    if add is None:
        return _pc(body, name=name, grid=(T // tm,), in_specs=specs, out_specs=[row, vec],
                   out_shape=[SDS((T, W), F32), SDS((1, W), F32)])(*ins)
    return _pc(body, name=name, grid=(T // tm,), in_specs=specs, out_specs=[row, row, vec],
               out_shape=[SDS((T, W), F32), SDS((T, W), BF16), SDS((1, W), F32)])(*ins)


def gated_rms_fwd(y, proj, g, *, name):
    T = y.shape[0]
    tm = _tile(T, 512, 8)

    def body(y_ref, z_ref, g_ref, o_ref):
        v = y_ref[...] * _silu(z_ref[...])
        r = lax.rsqrt(jnp.mean(v * v, axis=1, keepdims=True) + EPS)
        o_ref[...] = ((v * r) * g_ref[...]).astype(BF16)

    row = pl.BlockSpec((tm, SSD_W), lambda i: (i, 0))
    return _pc(body, name=name, grid=(T // tm,), in_specs=[row, row, pl.BlockSpec((1, SSD_W), lambda i: (0, 0))],
               out_specs=row, out_shape=SDS((T, SSD_W), BF16))(y, proj, g)


def gated_rms_bwd(y, proj, g, dycat, *, name):
    T = y.shape[0]
    tm = _tile(T, 512, 8)

    def body(y_ref, z_ref, g_ref, d_ref, dy_ref, dz_ref, dg_ref):
        yv, z = y_ref[...], z_ref[...]
        sz = _silu(z)
        v = yv * sz
        r = lax.rsqrt(jnp.mean(v * v, axis=1, keepdims=True) + EPS)
        vh = v * r
        d = d_ref[...]
        dvh = d * g_ref[...]
        dv = r * (dvh - vh * jnp.mean(dvh * vh, axis=1, keepdims=True))
        dy_ref[...] = dv * sz
        dz_ref[...] = dv * yv * _dsilu(z)

        @pl.when(pl.program_id(0) == 0)
        def _():
            dg_ref[...] = jnp.zeros_like(dg_ref)

        dg_ref[...] += _csum(d * vh)

    row = pl.BlockSpec((tm, SSD_W), lambda i: (i, 0))
    vec = pl.BlockSpec((1, SSD_W), lambda i: (0, 0))
    return _pc(body, name=name, grid=(T // tm,), in_specs=[row, row, vec, row], out_specs=[row, row, vec],
               out_shape=[SDS((T, SSD_W), F32), SDS((T, SSD_W), F32), SDS((1, SSD_W), F32)])(y, proj, g, dycat)


def final_loss(x, g, tgt, *, name):
    T = x.shape[0]
    tm = _tile(T, 512, 8)

    def body(x_ref, g_ref, t_ref, dx_ref, dxb_ref, dg_ref, l_ref):
        v = x_ref[...]
        gg = g_ref[...]
        r = lax.rsqrt(jnp.mean(v * v, axis=1, keepdims=True) + EPS)
        xh = v * r
        err = xh * gg - t_ref[...]
        part = 0.5 * _csum(jnp.mean(err * err, axis=1, keepdims=True))
        d = err * (1.0 / D)
        dxh = d * gg
        dx = r * (dxh - xh * jnp.mean(dxh * xh, axis=1, keepdims=True))
        dx_ref[...] = dx
        dxb_ref[...] = dx.astype(BF16)

        @pl.when(pl.program_id(0) == 0)
        def _():
            dg_ref[...] = jnp.zeros_like(dg_ref)
            l_ref[...] = jnp.zeros_like(l_ref)

        dg_ref[...] += _csum(d * xh)
        l_ref[...] += jnp.broadcast_to(part, (1, LANES))

    row = pl.BlockSpec((tm, D), lambda i: (i, 0))
    vec = pl.BlockSpec((1, D), lambda i: (0, 0))
    return _pc(body, name=name, grid=(T // tm,), in_specs=[row, vec, row],
               out_specs=[row, row, vec, pl.BlockSpec((1, LANES), lambda i: (0, 0))],
               out_shape=[SDS((T, D), F32), SDS((T, D), BF16), SDS((1, D), F32), SDS((1, LANES), F32)])(x, g, tgt)


HALO = 8


def _prev_map(ts, col):
    return lambda b, i, j: (b, jnp.maximum(i * (ts // HALO) - 1, 0), col(j))


def _next_map(ts, n_halo_blocks, col):
    return lambda b, i, j: (b, jnp.minimum((i + 1) * (ts // HALO), n_halo_blocks - 1), col(j))


def conv_silu_fwd(proj3, w, b, *, name):
    Bl, S, _ = proj3.shape
    C, K = SSD_XBC, SSD_K
    ts, tc = _tile(S, 512, 8), 512
    c0 = XBC0 // tc

    def body(xp_ref, x_ref, w_ref, b_ref, o_ref, ext):
        i = pl.program_id(1)
        ext[0:HALO, :] = jnp.where(i > 0, xp_ref[0], 0.0)
        ext[HALO:HALO + ts, :] = x_ref[0]
        acc = b_ref[...] + w_ref[0:1, :] * ext[pl.ds(HALO - (K - 1), ts), :]
        for k in range(1, K):
            acc = acc + w_ref[k:k + 1, :] * ext[pl.ds(HALO - (K - 1) + k, ts), :]
        o_ref[0] = _silu(acc)

    return _pc(body, name=name, grid=(Bl, S // ts, C // tc),
               in_specs=[pl.BlockSpec((1, HALO, tc), _prev_map(ts, lambda j: c0 + j)),
                         pl.BlockSpec((1, ts, tc), lambda b, i, j: (b, i, c0 + j)),
                         pl.BlockSpec((K, tc), lambda b, i, j: (0, j)),
                         pl.BlockSpec((1, tc), lambda b, i, j: (0, j))],
               out_specs=pl.BlockSpec((1, ts, tc), lambda b, i, j: (b, i, j)),
               out_shape=SDS((Bl, S, C), F32), scratch=[pltpu.VMEM((HALO + ts, tc), F32)])(proj3, proj3, w, b)


def conv_silu_bwd(proj3, w, b, dact, *, name):
    Bl, S, _ = proj3.shape
    C, K = SSD_XBC, SSD_K
    ts, tc = _tile(S, 512, 8), 512
    c0 = XBC0 // tc
    ns = S // ts

    def body(xp_ref, x_ref, xn_ref, d_ref, dn_ref, w_ref, b_ref, dx_ref, dw_ref, db_ref, ext, ext2):
        bb, i = pl.program_id(1), pl.program_id(2)
        last = i == ns - 1
        ext[0:HALO, :] = jnp.where(i > 0, xp_ref[0], 0.0)
        ext[HALO:HALO + ts, :] = x_ref[0]
        ext[HALO + ts:2 * HALO + ts, :] = jnp.where(last, 0.0, xn_ref[0])
        taps = [ext[pl.ds(HALO - (K - 1) + k, ts + HALO), :] for k in range(K)]
        acc = b_ref[...] + w_ref[0:1, :] * taps[0]
        for k in range(1, K):
            acc = acc + w_ref[k:k + 1, :] * taps[k]
        dsl = _dsilu(acc)
        du = d_ref[0] * dsl[0:ts]
        ext2[0:ts, :] = du
        ext2[ts:ts + HALO, :] = jnp.where(last, 0.0, dn_ref[0]) * dsl[ts:ts + HALO]
        dx = w_ref[0:1, :] * ext2[pl.ds(K - 1, ts), :]
        for k in range(1, K):
            dx = dx + w_ref[k:k + 1, :] * ext2[pl.ds(K - 1 - k, ts), :]
        dx_ref[0] = dx

        @pl.when((bb == 0) & (i == 0))
        def _():
            dw_ref[...] = jnp.zeros_like(dw_ref)
            db_ref[...] = jnp.zeros_like(db_ref)

        for k in range(K):
            dw_ref[k:k + 1, :] += _csum(du * taps[k][0:ts])
        db_ref[...] += _csum(du)

    nhb = S // HALO
    cx = lambda j: c0 + j
    cj = lambda j: j
    return _pc(body, name=name, grid=(C // tc, Bl, ns),
               in_specs=[pl.BlockSpec((1, HALO, tc), lambda j, b, i: _prev_map(ts, cx)(b, i, j)),
                         pl.BlockSpec((1, ts, tc), lambda j, b, i: (b, i, c0 + j)),
                         pl.BlockSpec((1, HALO, tc), lambda j, b, i: _next_map(ts, nhb, cx)(b, i, j)),
                         pl.BlockSpec((1, ts, tc), lambda j, b, i: (b, i, j)),
                         pl.BlockSpec((1, HALO, tc), lambda j, b, i: _next_map(ts, nhb, cj)(b, i, j)),
                         pl.BlockSpec((K, tc), lambda j, b, i: (0, j)),
                         pl.BlockSpec((1, tc), lambda j, b, i: (0, j))],
               out_specs=[pl.BlockSpec((1, ts, tc), lambda j, b, i: (b, i, j)),
                          pl.BlockSpec((K, tc), lambda j, b, i: (0, j)),
                          pl.BlockSpec((1, tc), lambda j, b, i: (0, j))],
               out_shape=[SDS((Bl, S, C), F32), SDS((K, C), F32), SDS((1, C), F32)],
               scratch=[pltpu.VMEM((2 * HALO + ts, tc), F32), pltpu.VMEM((HALO + ts, tc), F32)],
               )(proj3, proj3, proj3, dact, dact, w, b)


def ffn_act_fwd(pre3, w, b, *, name):
    Bl, S, _ = pre3.shape
    K = FFN_K
    ts, tc = _tile(S, 512, 8), 256
    nj = DFF // tc

    def body(gp_ref, g_ref, vp_ref, v_ref, wg_ref, wv_ref, bg_ref, bv_ref, o_ref, eg, ev):
        i = pl.program_id(1)
        outs = []
        for p_ref, m_ref, w_ref, b_ref, ext in ((gp_ref, g_ref, wg_ref, bg_ref, eg), (vp_ref, v_ref, wv_ref, bv_ref, ev)):
            ext[0:HALO, :] = jnp.where(i > 0, p_ref[0], 0.0)
            ext[HALO:HALO + ts, :] = m_ref[0]
            acc = b_ref[...] + w_ref[0:1, :] * ext[pl.ds(HALO - (K - 1), ts), :]
            for k in range(1, K):
                acc = acc + w_ref[k:k + 1, :] * ext[pl.ds(HALO - (K - 1) + k, ts), :]
            outs.append(acc)
        o_ref[0] = (_silu(outs[0]) * outs[1]).astype(BF16)

    main = lambda off: pl.BlockSpec((1, ts, tc), lambda b, i, j: (b, i, off + j))
    prev = lambda off: pl.BlockSpec((1, HALO, tc), _prev_map(ts, lambda j: off + j))
    wsp = lambda off: pl.BlockSpec((K, tc), lambda b, i, j: (0, off + j))
    bsp = lambda off: pl.BlockSpec((1, tc), lambda b, i, j: (0, off + j))
    return _pc(body, name=name, grid=(Bl, S // ts, nj),
               in_specs=[prev(0), main(0), prev(nj), main(nj), wsp(0), wsp(nj), bsp(0), bsp(nj)],
               out_specs=pl.BlockSpec((1, ts, tc), lambda b, i, j: (b, i, j)),
               out_shape=SDS((Bl, S, DFF), BF16),
               scratch=[pltpu.VMEM((HALO + ts, tc), F32), pltpu.VMEM((HALO + ts, tc), F32)],
               )(pre3, pre3, pre3, pre3, w, w, b, b)


def ffn_act_bwd(pre3, w, b, dact, *, comm=None, name):
    Bl, S, _ = pre3.shape
    K = FFN_K
    ts, tc = _tile(S, 512, 8), 256
    nj = DFF // tc
    ns = S // ts

    def body(gp_ref, g_ref, gn_ref, vp_ref, v_ref, vn_ref, d_ref, dn_ref, wg_ref, wv_ref, bg_ref, bv_ref,
             dg_ref, dv_ref, dwg_ref, dwv_ref, dbg_ref, dbv_ref, eg, ev, e2g, e2v):
        bb, i = pl.program_id(1), pl.program_id(2)
        last = i == ns - 1
        ups, taps = [], []
        for p_ref, m_ref, n_ref, w_ref, b_ref, ext in ((gp_ref, g_ref, gn_ref, wg_ref, bg_ref, eg),
                                                       (vp_ref, v_ref, vn_ref, wv_ref, bv_ref, ev)):
            ext[0:HALO, :] = jnp.where(i > 0, p_ref[0], 0.0)
            ext[HALO:HALO + ts, :] = m_ref[0]
            ext[HALO + ts:2 * HALO + ts, :] = jnp.where(last, 0.0, n_ref[0])
            tp = [ext[pl.ds(HALO - (K - 1) + k, ts + HALO), :] for k in range(K)]
            acc = b_ref[...] + w_ref[0:1, :] * tp[0]
            for k in range(1, K):
                acc = acc + w_ref[k:k + 1, :] * tp[k]
            ups.append(acc)
            taps.append(tp)
        ug, uv = ups
        dg_e = uv * _dsilu(ug)
        dv_e = _silu(ug)
        d_main = d_ref[0]
        d_next = jnp.where(last, 0.0, dn_ref[0])
        dug = d_main * dg_e[0:ts]
        duv = d_main * dv_e[0:ts]
        e2g[0:ts, :] = dug
        e2g[ts:ts + HALO, :] = d_next * dg_e[ts:ts + HALO]
        e2v[0:ts, :] = duv
        e2v[ts:ts + HALO, :] = d_next * dv_e[ts:ts + HALO]

        @pl.when((bb == 0) & (i == 0))
        def _():
            for r in (dwg_ref, dwv_ref, dbg_ref, dbv_ref):
                r[...] = jnp.zeros_like(r)

        for w_ref, e2, tp, du, o_ref, dw_ref, db_ref in ((wg_ref, e2g, taps[0], dug, dg_ref, dwg_ref, dbg_ref),
                                                         (wv_ref, e2v, taps[1], duv, dv_ref, dwv_ref, dbv_ref)):
            dx = w_ref[0:1, :] * e2[pl.ds(K - 1, ts), :]
            for k in range(1, K):
                dx = dx + w_ref[k:k + 1, :] * e2[pl.ds(K - 1 - k, ts), :]
            o_ref[0] = dx.astype(BF16)
            for k in range(K):
                dw_ref[k:k + 1, :] += _csum(du * tp[k][0:ts])
            db_ref[...] += _csum(du)

    nhb = S // HALO
    main = lambda off: pl.BlockSpec((1, ts, tc), lambda j, b, i: (b, i, off + j))
    prev = lambda off: pl.BlockSpec((1, HALO, tc), lambda j, b, i: _prev_map(ts, lambda jj: off + jj)(b, i, j))
    nxt = lambda off: pl.BlockSpec((1, HALO, tc), lambda j, b, i: _next_map(ts, nhb, lambda jj: off + jj)(b, i, j))
    wsp = lambda off: pl.BlockSpec((K, tc), lambda j, b, i: (0, off + j))
    bsp = lambda off: pl.BlockSpec((1, tc), lambda j, b, i: (0, off + j))
    outs = _pc(body, name=name, grid=(nj, Bl, ns),
               in_specs=[prev(0), main(0), nxt(0), prev(nj), main(nj), nxt(nj), main(0), nxt(0),
                         wsp(0), wsp(nj), bsp(0), bsp(nj)],
               out_specs=[main(0), main(0), wsp(0), wsp(0), bsp(0), bsp(0)],
               out_shape=[SDS((Bl, S, DFF), BF16), SDS((Bl, S, DFF), BF16), SDS((K, DFF), F32), SDS((K, DFF), F32),
                          SDS((1, DFF), F32), SDS((1, DFF), F32)],
               scratch=[pltpu.VMEM((2 * HALO + ts, tc), F32), pltpu.VMEM((2 * HALO + ts, tc), F32),
                        pltpu.VMEM((HALO + ts, tc), F32), pltpu.VMEM((HALO + ts, tc), F32)],
               comm=comm)(pre3, pre3, pre3, pre3, pre3, pre3, dact, dact, w, w, b, b)
    return outs


PHALO = 16


def _pool_window_sums(ext, base, ts, step):
    s = ext[pl.ds(base, ts), :]
    out = []
    for i in range(1, PHALO):
        s = s + ext[pl.ds(base + step * i, ts), :]
        if i + 1 in POOL_WIN:
            out.append(s)
    return out


def _pick(g, vals):
    r = vals[-1]
    for k in range(len(vals) - 2, -1, -1):
        r = jnp.where(g == k, vals[k], r)
    return r


def _pool_count(g, i, ts, rows):
    t = (i * ts + lax.broadcasted_iota(jnp.int32, (rows, 1), 0) + 1).astype(F32)
    return jnp.minimum(t, _pick(g, [float(w) for w in POOL_WIN]))


def _pooled(up_ref, u_ref, ext, g, i, ts):
    ext[0:PHALO, :] = jnp.where(i > 0, up_ref[0], 0.0)
    u = u_ref[0]
    ext[PHALO:PHALO + ts, :] = u
    sums = _pool_window_sums(ext, PHALO, ts, -1)
    return _pick(g, sums) / _pool_count(g, i, ts, ts) - u


def pool_fwd(proj3, pool_w, scale, *, name):
    Bl, S, _ = proj3.shape
    ts = _tile(S, 512, 16)
    c0 = U0 // POOL_D

    def body(up_ref, u_ref, w_ref, s_ref, o_ref, ext):
        i, g = pl.program_id(1), pl.program_id(2)
        pooled = _pooled(up_ref, u_ref, ext, g, i, ts)
        o_ref[0] = (_nn(pooled, w_ref[0]) * s_ref[...]).astype(BF16)

    return _pc(body, name=name, grid=(Bl, S // ts, POOL_G),
               in_specs=[pl.BlockSpec((1, PHALO, POOL_D), lambda b, i, g: (b, jnp.maximum(i * (ts // PHALO) - 1, 0), c0 + g)),
                         pl.BlockSpec((1, ts, POOL_D), lambda b, i, g: (b, i, c0 + g)),
                         pl.BlockSpec((1, POOL_D, POOL_D), lambda b, i, g: (g, 0, 0)),
                         pl.BlockSpec((1, POOL_D), lambda b, i, g: (0, g))],
               out_specs=pl.BlockSpec((1, ts, POOL_D), lambda b, i, g: (b, i, g)),
               out_shape=SDS((Bl, S, POOL_W), BF16), scratch=[pltpu.VMEM((PHALO + ts, POOL_D), F32)],
               )(proj3, proj3, pool_w, scale)


def pool_bwd(proj3, pool_w, scale, dycat3, *, name):
    Bl, S, _ = proj3.shape
    ts = _tile(S, 512, 16)
    ns = S // ts
    c0 = U0 // POOL_D
    d0 = SSD_W // POOL_D
    nhb = S // PHALO

    def body(up_ref, u_ref, d_ref, dn_ref, w_ref, s_ref, du_ref, dw_ref, ds_ref, ext, ext2):
        g, bb, i = pl.program_id(0), pl.program_id(1), pl.program_id(2)
        last = i == ns - 1
        pooled = _pooled(up_ref, u_ref, ext, g, i, ts)
        wm = w_ref[0]
        sc = s_ref[...]
        dy = d_ref[0]
        dp_main = dy * sc
        dpool = _nt(dp_main, wm)
        dpool_n = _nt(jnp.where(last, 0.0, dn_ref[0]) * sc, wm)
        ext2[0:ts, :] = dpool / _pool_count(g, i, ts, ts)
        ext2[ts:ts + PHALO, :] = dpool_n / _pool_count(g, i + 1, ts, PHALO)
        sums = _pool_window_sums(ext2, 0, ts, 1)
        du_ref[0] = _pick(g, sums) - dpool

        @pl.when((bb == 0) & (i == 0))
        def _():
            dw_ref[...] = jnp.zeros_like(dw_ref)
            ds_ref[...] = jnp.zeros_like(ds_ref)

        dw_ref[0] += _tn(pooled, dp_main)
        ds_ref[...] += _csum(dy * _nn(pooled, wm))

    return _pc(body, name=name, grid=(POOL_G, Bl, ns),
               in_specs=[pl.BlockSpec((1, PHALO, POOL_D), lambda g, b, i: (b, jnp.maximum(i * (ts // PHALO) - 1, 0), c0 + g)),
                         pl.BlockSpec((1, ts, POOL_D), lambda g, b, i: (b, i, c0 + g)),
                         pl.BlockSpec((1, ts, POOL_D), lambda g, b, i: (b, i, d0 + g)),
                         pl.BlockSpec((1, PHALO, POOL_D), lambda g, b, i: (b, jnp.minimum((i + 1) * (ts // PHALO), nhb - 1), d0 + g)),
                         pl.BlockSpec((1, POOL_D, POOL_D), lambda g, b, i: (g, 0, 0)),
                         pl.BlockSpec((1, POOL_D), lambda g, b, i: (0, g))],
               out_specs=[pl.BlockSpec((1, ts, POOL_D), lambda g, b, i: (b, i, g)),
                          pl.BlockSpec((1, POOL_D, POOL_D), lambda g, b, i: (g, 0, 0)),
                          pl.BlockSpec((1, POOL_D), lambda g, b, i: (0, g))],
               out_shape=[SDS((Bl, S, POOL_W), F32), SDS((POOL_G, POOL_D, POOL_D), F32), SDS((1, POOL_W), F32)],
               scratch=[pltpu.VMEM((PHALO + ts, POOL_D), F32), pltpu.VMEM((PHALO + ts, POOL_D), F32)],
               )(proj3, proj3, dycat3, dycat3, pool_w, scale)


NPAIR = SSD_HEADS // 2


def _ssd_common(sm, bias, alog):
    L = SSD_L
    dt = jax.nn.softplus(sm + bias)
    a = -jnp.exp(alog)
    da = dt * a
    r = lax.broadcasted_iota(jnp.int32, (L, L), 0)
    c = lax.broadcasted_iota(jnp.int32, (L, L), 1)
    tri = (r >= c).astype(F32)
    cum = _dg(tri, da, 1, 0, lax.Precision.HIGHEST)
    return dt, a, cum, cum.T, r >= c


def _lanes(lo, hi, shape=(1, LANES)):
    lane = lax.broadcasted_iota(jnp.int32, shape, len(shape) - 1)
    return (lane >= lo) & (lane < hi)


def _onehot_lane(h):
    return (lax.broadcasted_iota(jnp.int32, (1, LANES), 1) == h).astype(F32)


def _split_nn(a, e):
    hi = a.astype(BF16)
    lo = (a - hi.astype(F32)).astype(BF16)
    return _dg(hi, e, 1, 0) + _dg(lo, e, 1, 0)


def _head_spread():
    r = lax.broadcasted_iota(jnp.int32, (LANES, SSD_W), 0)
    c = lax.broadcasted_iota(jnp.int32, (LANES, SSD_W), 1)
    return (c // SSD_P == r).astype(BF16)


def _pair_gather(j):
    r = lax.broadcasted_iota(jnp.int32, (LANES, LANES), 0)
    c = lax.broadcasted_iota(jnp.int32, (LANES, LANES), 1)
    return (c == 2 * j + (r >= SSD_P).astype(jnp.int32)).astype(BF16)


def ssd_fwd(xbc3, proj3, bias, alog, dskip, *, name):
    Bl, S, _ = xbc3.shape
    L = SSD_L
    nc = S // L

    def body(xbc_ref, sm_ref, bias_ref, alog_ref, d_ref, y_ref, hin_ref, H):
        c = pl.program_id(1)

        @pl.when(c == 0)
        def _():
            H[...] = jnp.zeros_like(H)

        dt, a, cum, cumT, mask = _ssd_common(sm_ref[0], bias_ref[...], alog_ref[...])
        lo = _lanes(0, SSD_P)
        rowlo = lax.broadcasted_iota(jnp.int32, (LANES, LANES), 0) < SSD_P
        spread = _head_spread()
        dt_x = _split_nn(dt, spread)
        el_x = _split_nn(jnp.exp(cum), spread)
        wl_x = _split_nn(jnp.exp(cum[L - 1:L, :] - cum), spread)
        cb = []
        for g in range(SSD_G):
            Bg = xbc_ref[0, :, SSD_W + g * SSD_N:SSD_W + (g + 1) * SSD_N]
            Cg = xbc_ref[0, :, SSD_W + SSD_G * SSD_N + g * SSD_N:SSD_W + SSD_G * SSD_N + (g + 1) * SSD_N]
            cb.append((Bg, Cg, _nt(Cg, Bg)))
        for j in range(NPAIR):
            h0, h1 = 2 * j, 2 * j + 1
            sl = slice(j * LANES, (j + 1) * LANES)
            Bg, Cg, CB = cb[j // (NPAIR // SSD_G)]
            X = xbc_ref[0, :, sl]
            c0, c1 = cum[:, h0:h0 + 1], cum[:, h1:h1 + 1]
            r0, r1 = cumT[h0:h0 + 1, :], cumT[h1:h1 + 1, :]
            cl0, cl1 = cum[L - 1:L, h0:h0 + 1], cum[L - 1:L, h1:h1 + 1]
            Xt = X * dt_x[:, sl]
            M0 = CB * jnp.exp(jnp.where(mask, c0 - r0, NEG))
            M1 = CB * jnp.exp(jnp.where(mask, c1 - r1, NEG))
            Yd = jnp.where(lo, _nn(M0, Xt), _nn(M1, Xt))
            Hp = H[j]
            hin_ref[0, 0, j] = Hp
            Z = _nt(Cg, Hp)
            y_ref[0, :, sl] = Yd + el_x[:, sl] * Z + X * d_ref[j:j + 1, :]
            H[j] = jnp.where(rowlo, jnp.exp(cl0), jnp.exp(cl1)) * Hp + _tn(wl_x[:, sl] * Xt, Bg)

    vec = pl.BlockSpec((1, LANES), lambda b, c: (0, 0))
    return _pc(body, name=name, grid=(Bl, nc),
               in_specs=[pl.BlockSpec((1, L, SSD_XBC), lambda b, c: (b, c, 0)),
                         pl.BlockSpec((1, L, LANES), lambda b, c: (b, c, DT0 // LANES)),
                         vec, vec, pl.BlockSpec((NPAIR, LANES), lambda b, c: (0, 0))],
               out_specs=[pl.BlockSpec((1, L, SSD_W), lambda b, c: (b, c, 0)),
                          pl.BlockSpec((1, 1, NPAIR, LANES, LANES), lambda b, c: (b, c, 0, 0, 0))],
               out_shape=[SDS((Bl, S, SSD_W), F32), SDS((Bl, nc, NPAIR, LANES, LANES), F32)],
               scratch=[pltpu.VMEM((NPAIR, LANES, LANES), F32)])(xbc3, proj3, bias, alog, dskip)


def ssd_bwd(xbc3, proj3, hin, dy3, bias, alog, dskip, *, comm=None, name):
    Bl, S, _ = xbc3.shape
    L = SSD_L
    nc = S // L

    def body(xbc_ref, sm_ref, hin_ref, dy_ref, bias_ref, alog_ref, d_ref, dx_ref, ddt_ref, dpar_ref, dd_ref, dH, ddacc):
        bb, i = pl.program_id(0), pl.program_id(1)

        @pl.when(i == 0)
        def _():
            dH[...] = jnp.zeros_like(dH)

        @pl.when((bb == 0) & (i == 0))
        def _():
            dpar_ref[...] = jnp.zeros_like(dpar_ref)
            ddacc[...] = jnp.zeros_like(ddacc)

        sm = sm_ref[0]
        dt, a, cum, cumT, mask = _ssd_common(sm, bias_ref[...], alog_ref[...])
        maskf = mask.astype(F32)
        lo = _lanes(0, SSD_P)
        rowlo = lax.broadcasted_iota(jnp.int32, (LANES, LANES), 0) < SSD_P
        lastrow = (lax.broadcasted_iota(jnp.int32, (L, 1), 0) == L - 1).astype(F32)
        dcum = jnp.zeros((L, LANES), F32)
        dcum_t = jnp.zeros((LANES, L), F32)
        ddt = jnp.zeros((L, LANES), F32)
        spread = _head_spread()
        ones = jnp.ones((L, LANES), BF16)
        ecum = jnp.exp(cum)
        wall = jnp.exp(cum[L - 1:L, :] - cum)
        dt_x = _split_nn(dt, spread)
        el_x = _split_nn(ecum, spread)
        wl_x = _split_nn(wall, spread)
        headrow = lax.broadcasted_iota(jnp.int32, (LANES, 1), 0)
        grp = []
        for g in range(SSD_G):
            Bg = xbc_ref[0, :, SSD_W + g * SSD_N:SSD_W + (g + 1) * SSD_N]
            Cg = xbc_ref[0, :, SSD_W + SSD_G * SSD_N + g * SSD_N:SSD_W + SSD_G * SSD_N + (g + 1) * SSD_N]
            grp.append(dict(B=Bg, C=Cg, CB=_nt(Cg, Bg), dB=jnp.zeros((L, SSD_N), F32), dC=jnp.zeros((L, SSD_N), F32),
                            dCB=jnp.zeros((L, L), F32)))
        for j in range(NPAIR):
            h0, h1 = 2 * j, 2 * j + 1
            sl = slice(j * LANES, (j + 1) * LANES)
            G = grp[j // (NPAIR // SSD_G)]
            Bg, Cg, CB = G["B"], G["C"], G["CB"]
            X = xbc_ref[0, :, sl]
            dY = dy_ref[0, :, sl]
            c0, c1 = cum[:, h0:h0 + 1], cum[:, h1:h1 + 1]
            r0, r1 = cumT[h0:h0 + 1, :], cumT[h1:h1 + 1, :]
            cl0, cl1 = cum[L - 1:L, h0:h0 + 1], cum[L - 1:L, h1:h1 + 1]
            oh0, oh1 = _onehot_lane(h0), _onehot_lane(h1)
            gather = _pair_gather(j)
            dtl, el, wl = dt_x[:, sl], el_x[:, sl], wl_x[:, sl]
            Xt = X * dtl
            Hp = hin_ref[0, 0, j]
            dS = dH[j]
            dX = dY * d_ref[j:j + 1, :]
            ddacc[j:j + 1, :] += _csum(dY * X)
            Z = _nt(Cg, Hp)
            dZ = dY * el
            dcum = dcum + _split_nn(dY * Z, gather) * ecum
            G["dC"] = G["dC"] + _nn(dZ, Hp)
            dHy = _tn(dZ, Cg)
            Gm = _nt(Bg, dS)
            dXt = wl * Gm
            q = _split_nn(Xt * Gm, gather) * wall
            dcum = dcum + lastrow * _csum(q) - q
            G["dB"] = G["dB"] + _nn(wl * Xt, dS)
            g0, g1 = jnp.exp(cl0), jnp.exp(cl1)
            rowsum = _nn(dS * Hp, ones)
            dg0 = _csum(jnp.where(rowlo, rowsum, 0.0))
            dg1 = _csum(jnp.where(rowlo, 0.0, rowsum))
            dcum = dcum + lastrow * ((dg0 * g0) * oh0 + (dg1 * g1) * oh1)
            dH[j] = jnp.where(rowlo, g0, g1) * dS + dHy
            for h, ch, rh, mh, oh in ((h0, c0, r0, lo, oh0), (h1, c1, r1, jnp.logical_not(lo), oh1)):
                decay = jnp.exp(jnp.where(mask, ch - rh, NEG))
                Mh = CB * decay
                dM = _nt(jnp.where(mh, dY, 0.0), Xt) * maskf
                dXt = dXt + jnp.where(mh, _tn(Mh, dY), 0.0)
                G["dCB"] = G["dCB"] + dM * decay
                Q = dM * Mh
                dcum = dcum + _rsum(Q) * oh
                dcum_t = dcum_t + (headrow == h).astype(F32) * _csum(Q)
            dX = dX + dXt * dtl
            ddt = ddt + _split_nn(dXt * X, gather)
            dx_ref[0, :, sl] = dX
        dcum = dcum - dcum_t.T
        for g in range(SSD_G):
            G = grp[g]
            dC = G["dC"] + _nn(G["dCB"], G["B"])
            dB = G["dB"] + _tn(G["dCB"], G["C"])
            dx_ref[0, :, SSD_W + g * SSD_N:SSD_W + (g + 1) * SSD_N] = dB
            dx_ref[0, :, SSD_W + SSD_G * SSD_N + g * SSD_N:SSD_W + SSD_G * SSD_N + (g + 1) * SSD_N] = dC
        r = lax.broadcasted_iota(jnp.int32, (L, L), 0)
        c = lax.broadcasted_iota(jnp.int32, (L, L), 1)
        dda = _dg((c >= r).astype(F32), dcum, 1, 0, lax.Precision.HIGHEST)
        heads = _lanes(0, SSD_HEADS)
        ddt = ddt + dda * a
        draw = jnp.where(heads, ddt * _sig(sm + bias_ref[...]), 0.0)
        ddt_ref[0] = draw
        dpar_ref[0:1, :] += _csum(draw)
        dpar_ref[1:2, :] += _csum(jnp.where(heads, dda * dt * a, 0.0))

        @pl.when((bb == Bl - 1) & (i == nc - 1))
        def _():
            acc = ddacc[...]
            lane = lax.broadcasted_iota(jnp.int32, (NPAIR, LANES), 1)
            s0 = _rsum(jnp.where(lane < SSD_P, acc, 0.0))
            s1 = _rsum(jnp.where(lane < SSD_P, 0.0, acc))
            dd_ref[...] = jnp.where(lane == 0, s0, jnp.where(lane == 1, s1, 0.0))

    vec = pl.BlockSpec((1, LANES), lambda b, i: (0, 0))
    par = pl.BlockSpec((NPAIR, LANES), lambda b, i: (0, 0))
    return _pc(body, name=name, grid=(Bl, nc),
               in_specs=[pl.BlockSpec((1, L, SSD_XBC), lambda b, i: (b, nc - 1 - i, 0)),
                         pl.BlockSpec((1, L, LANES), lambda b, i: (b, nc - 1 - i, DT0 // LANES)),
                         pl.BlockSpec((1, 1, NPAIR, LANES, LANES), lambda b, i: (b, nc - 1 - i, 0, 0, 0)),
                         pl.BlockSpec((1, L, SSD_W), lambda b, i: (b, nc - 1 - i, 0)),
                         vec, vec, par],
               out_specs=[pl.BlockSpec((1, L, SSD_XBC), lambda b, i: (b, nc - 1 - i, 0)),
                          pl.BlockSpec((1, L, LANES), lambda b, i: (b, nc - 1 - i, 0)),
                          par, par],
               out_shape=[SDS((Bl, S, SSD_XBC), F32), SDS((Bl, S, LANES), F32), SDS((NPAIR, LANES), F32),
                          SDS((NPAIR, LANES), F32)],
               scratch=[pltpu.VMEM((NPAIR, LANES, LANES), F32), pltpu.VMEM((NPAIR, LANES), F32)],
               comm=comm)(xbc3, proj3, hin, dy3, bias, alog, dskip)


PE_LO, PE_MID, PE_HI = MLA_NOPE, MLA_NOPE + MLA_ROPE // 2, MLA_NOPE + MLA_ROPE
ATT_SCALE = 1.0 / math.sqrt(MLA_QK)


def _swap_matrix():
    src = lax.broadcasted_iota(jnp.int32, (LANES, LANES), 0)
    dst = lax.broadcasted_iota(jnp.int32, (LANES, LANES), 1)
    half = MLA_ROPE // 2
    first = (dst >= PE_LO) & (dst < PE_MID) & (src == dst + half)
    second = (dst >= PE_MID) & (dst < PE_HI) & (src == dst - half)
    return (second.astype(F32) - first.astype(F32)).astype(BF16)


def rope_tables(pos, invf, *, name):
    T = pos.shape[0]
    tm = _tile(T, 512, 8)

    def body(pos_ref, f_ref, c_ref, s_ref):
        ang = pos_ref[...] * f_ref[...]
        pe = _lanes(PE_LO, PE_HI)
        c_ref[...] = jnp.where(pe, jnp.cos(ang), 1.0)
        s_ref[...] = jnp.where(pe, jnp.sin(ang), 0.0)

    tile = pl.BlockSpec((tm, LANES), lambda i: (i, 0))
    return _pc(body, name=name, grid=(T // tm,),
               in_specs=[pl.BlockSpec((tm, 1), lambda i: (i, 0)), pl.BlockSpec((1, LANES), lambda i: (0, 0))],
               out_specs=[tile, tile], out_shape=[SDS((T, LANES), F32)] * 2)(pos, invf)


V_ONE = MLA_V


def mla_prep_fwd(qt, kvt, proj, cs, sn, *, name):
    T = qt.shape[0]
    tm = _tile(T, 256, 8)
    HW = MLA_H * LANES

    def body(q_ref, k_ref, v_ref, kpe_ref, c_ref, s_ref, qo_ref, ko_ref, vo_ref):
        c, s = c_ref[...], s_ref[...]
        kpe = kpe_ref[...]
        sw = _swap_matrix()
        one = _lanes(V_ONE, V_ONE + 1)
        for h in range(MLA_H):
            sl = slice(h * LANES, (h + 1) * LANES)
            q = q_ref[:, sl]
            k = k_ref[:, sl] + kpe
            qo_ref[:, sl] = ((q * c + _split_nn(q, sw) * s) * ATT_SCALE).astype(BF16)
            ko_ref[:, sl] = (k * c + _split_nn(k, sw) * s).astype(BF16)
            vo_ref[:, sl] = jnp.where(one, 1.0, v_ref[:, sl]).astype(BF16)

    row = pl.BlockSpec((tm, HW), lambda i: (i, 0))
    tab = pl.BlockSpec((tm, LANES), lambda i: (i, 0))
    return _pc(body, name=name, grid=(T // tm,),
               in_specs=[row, row, pl.BlockSpec((tm, HW), lambda i: (i, 1)),
                         pl.BlockSpec((tm, LANES), lambda i: (i, KPE0 // LANES)), tab, tab],
               out_specs=[row, row, row], out_shape=[SDS((T, HW), BF16)] * 3)(qt, kvt, kvt, proj, cs, sn)


def mla_prep_bwd(dqr, dkr, cs, sn, *, name):
    T = dqr.shape[0]
    tm = _tile(T, 256, 8)
    HW = MLA_H * LANES

    def body(dq_ref, dk_ref, c_ref, s_ref, qo_ref, ko_ref, kpe_ref):
        c, s = c_ref[...], s_ref[...]
        sw = _swap_matrix()
        pe = _lanes(PE_LO, PE_HI)
        dkpe = jnp.zeros((tm, LANES), F32)
        for h in range(MLA_H):
            sl = slice(h * LANES, (h + 1) * LANES)
            dq = dq_ref[:, sl] * ATT_SCALE
            dk = dk_ref[:, sl]
            qo_ref[:, sl] = (dq * c - _split_nn(dq * s, sw)).astype(BF16)
            dkk = dk * c - _split_nn(dk * s, sw)
            ko_ref[:, sl] = jnp.where(pe, 0.0, dkk).astype(BF16)
            dkpe = dkpe + jnp.where(pe, dkk, 0.0)
        kpe_ref[...] = dkpe

    row = pl.BlockSpec((tm, HW), lambda i: (i, 0))
    tab = pl.BlockSpec((tm, LANES), lambda i: (i, 0))
    return _pc(body, name=name, grid=(T // tm,), in_specs=[row, row, tab, tab], out_specs=[row, row, tab],
               out_shape=[SDS((T, HW), BF16), SDS((T, HW), BF16), SDS((T, LANES), F32)])(dqr, dkr, cs, sn)


def _att_tile(S):
    return _tile(S, 512, LANES)


def _rep(x, n):
    return x if n == 1 else jnp.concatenate([x] * n, axis=1)


def _diag_mask(t, transposed=False):
    r = lax.broadcasted_iota(jnp.int32, (t, t), 0)
    c = lax.broadcasted_iota(jnp.int32, (t, t), 1)
    return (c >= r) if transposed else (c <= r)


def flash_fwd(qr, kr, vr, Bl, *, comm=None, name):
    T = qr.shape[0]
    S = T // Bl
    t = _att_tile(S)
    n = S // t
    nl = t // LANES

    def body(q_ref, k_ref, v_ref, o_ref, lse_ref, lset_ref, m, acc):
        qi = pl.program_id(2)
        q = q_ref[...]
        m[...] = jnp.full_like(m, NEG)
        acc[...] = jnp.zeros_like(acc)

        def block(kj, masked):
            off = pl.multiple_of(kj * t, t)
            s = _nt(q, k_ref[pl.ds(off, t), :])
            if masked:
                s = jnp.where(_diag_mask(t), s, NEG)
            mo = m[...]
            mn = jnp.maximum(mo, jnp.max(s, axis=1, keepdims=True))
            p = jnp.exp((s - _rep(mn, nl)).astype(BF16))
            acc[...] = jnp.exp(mo - mn) * acc[...] + _nn(p, v_ref[pl.ds(off, t), :])
            m[...] = mn

        def loop(kj, c):
            block(kj, False)
            return c

        lax.fori_loop(0, qi, loop, 0)
        block(qi, True)
        a = acc[...]
        l = a[:, V_ONE:V_ONE + 1]
        o_ref[...] = jnp.where(_lanes(0, MLA_V), a / l, 0.0).astype(BF16)
        lse = m[...] + jnp.log(l)
        lse_ref[...] = lse
        lset_ref[...] = lse.T[0:8, :]

    qs = pl.BlockSpec((t, LANES), lambda b, h, qi: (b * n + qi, h))
    seq = pl.BlockSpec((S, LANES), lambda b, h, qi: (b, h))
    return _pc(body, name=name, grid=(Bl, MLA_H, n), in_specs=[qs, seq, seq],
               out_specs=[qs, qs, pl.BlockSpec((8, t), lambda b, h, qi: (b * MLA_H + h, qi))],
               out_shape=[SDS((T, MLA_H * LANES), BF16), SDS((T, MLA_H * LANES), F32), SDS((Bl * MLA_H * 8, S), F32)],
               scratch=[pltpu.VMEM((t, LANES), F32), pltpu.VMEM((t, LANES), F32)], comm=comm)(qr, kr, vr)


def flash_bwd_dq(qr, kr, vr, o, lse, dycat, Bl, *, name):
    T = qr.shape[0]
    S = T // Bl
    t = _att_tile(S)
    n = S // t
    nl = t // LANES
    do0 = (SSD_W + POOL_W) // LANES

    def body(q_ref, k_ref, v_ref, o_ref, lse_ref, do_ref, dq_ref, dlt_ref, acc, dl):
        qi = pl.program_id(2)
        q = q_ref[...]
        do = do_ref[...]
        dob = do.astype(BF16)
        dl[...] = jnp.broadcast_to(_rsum(do * o_ref[...].astype(F32)), (t, LANES))
        acc[...] = jnp.zeros_like(acc)

        def block(kj, masked):
            off = pl.multiple_of(kj * t, t)
            k = k_ref[pl.ds(off, t), :]
            s = _nt(q, k)
            if masked:
                s = jnp.where(_diag_mask(t), s, NEG)
            p = jnp.exp((s - _rep(lse_ref[...], nl)).astype(BF16))
            dp = _nt(dob, v_ref[pl.ds(off, t), :])
            acc[...] += _nn(p * (dp - _rep(dl[...], nl)), k)

        def loop(kj, c):
            block(kj, False)
            return c

        lax.fori_loop(0, qi, loop, 0)
        block(qi, True)
        dq_ref[...] = acc[...]
        dlt_ref[...] = dl[...].T[0:8, :]

    qs = pl.BlockSpec((t, LANES), lambda b, h, qi: (b * n + qi, h))
    seq = pl.BlockSpec((S, LANES), lambda b, h, qi: (b, h))
    return _pc(body, name=name, grid=(Bl, MLA_H, n),
               in_specs=[qs, seq, seq, qs, qs, pl.BlockSpec((t, LANES), lambda b, h, qi: (b * n + qi, do0 + h))],
               out_specs=[qs, pl.BlockSpec((8, t), lambda b, h, qi: (b * MLA_H + h, qi))],
               out_shape=[SDS((T, MLA_H * LANES), F32), SDS((Bl * MLA_H * 8, S), F32)],
               scratch=[pltpu.VMEM((t, LANES), F32), pltpu.VMEM((t, LANES), F32)])(qr, kr, vr, o, lse, dycat)


def flash_bwd_dkv(qr, kr, vr, lset, dlt, dycat, Bl, *, name):
    T = qr.shape[0]
    S = T // Bl
    t = _att_tile(S)
    n = S // t
    do0 = (SSD_W + POOL_W) // LANES

    def body(q_ref, k_ref, v_ref, lset_ref, dlt_ref, do_ref, dk_ref, dv_ref, dka, dva):
        kj = pl.program_id(2)
        k = k_ref[...]
        v = v_ref[...]
        dka[...] = jnp.zeros_like(dka)
        dva[...] = jnp.zeros_like(dva)

        def block(qi, masked):
            off = pl.multiple_of(qi * t, t)
            q = q_ref[pl.ds(off, t), :]
            do = do_ref[pl.ds(off, t), :].astype(BF16)
            st = _nt(k, q)
            if masked:
                st = jnp.where(_diag_mask(t, True), st, NEG)
            pt = jnp.exp((st - lset_ref[0:1, pl.ds(off, t)]).astype(BF16))
            dst = pt * (_nt(v, do) - dlt_ref[0:1, pl.ds(off, t)])
            dva[...] += _nn(pt, do)
            dka[...] += _nn(dst, q)

        def loop(qi, c):
            block(qi, False)
            return c

        block(kj, True)
        lax.fori_loop(kj + 1, n, loop, 0)
        dk_ref[...] = dka[...]
        dv_ref[...] = dva[...].astype(BF16)

    ks = pl.BlockSpec((t, LANES), lambda b, h, kj: (b * n + kj, h))
    seq = pl.BlockSpec((S, LANES), lambda b, h, kj: (b, h))
    rows = pl.BlockSpec((8, S), lambda b, h, kj: (b * MLA_H + h, 0))
    return _pc(body, name=name, grid=(Bl, MLA_H, n),
               in_specs=[seq, ks, ks, rows, rows, pl.BlockSpec((S, LANES), lambda b, h, kj: (b, do0 + h))],
               out_specs=[ks, ks], out_shape=[SDS((T, MLA_H * LANES), F32), SDS((T, MLA_H * LANES), BF16)],
               scratch=[pltpu.VMEM((t, LANES), F32), pltpu.VMEM((t, LANES), F32)])(qr, kr, vr, lset, dlt, dycat)


def _rows2d(a):
    return a.reshape(-1, a.shape[-1])


def add2(a, b, *, name):
    shp = a.shape
    a2, b2 = _rows2d(a), _rows2d(b)
    R, C = a2.shape
    tm = _tile(R, 512, 8)

    def body(a_ref, b_ref, o_ref, ob_ref):
        s = a_ref[...] + b_ref[...]
        o_ref[...] = s
        ob_ref[...] = s.astype(BF16)

    blk = pl.BlockSpec((tm, C), lambda i: (i, 0))
    o, ob = _pc(body, name=name, grid=(R // tm,), in_specs=[blk, blk], out_specs=[blk, blk],
                out_shape=[SDS((R, C), F32), SDS((R, C), BF16)])(a2, b2)
    return o.reshape(shp), ob.reshape(shp)


def adamw(w, m, v, parts, *, name):
    shp = w.shape
    w2, m2, v2 = _rows2d(w), _rows2d(m), _rows2d(v)
    R, C = w2.shape
    p3 = [p.reshape(p.shape[0], R, C) for p in parts]
    tm = _tile(R, 256, 8)
    bc1 = 1.0 - ADAM_B1 ** ADAM_STEP
    bc2 = 1.0 - ADAM_B2 ** ADAM_STEP

    def body(w_ref, m_ref, v_ref, *refs):
        p_refs, (g_ref, d_ref, nm_ref, nv_ref) = refs[:len(p3)], refs[len(p3):]
        g = None
        for p_ref, p in zip(p_refs, p3):
            for k in range(p.shape[0]):
                term = p_ref[k].astype(F32)
                g = term if g is None else g + term
        mm_ = ADAM_B1 * m_ref[...] + (1.0 - ADAM_B1) * g
        vv = ADAM_B2 * v_ref[...] + (1.0 - ADAM_B2) * (g * g)
        g_ref[...] = g
        nm_ref[...] = mm_
        nv_ref[...] = vv
        d_ref[...] = -ADAM_LR * ((mm_ / bc1) / (jnp.sqrt(vv / bc2) + ADAM_EPS) + ADAM_WD * w_ref[...])

    blk = pl.BlockSpec((tm, C), lambda i: (i, 0))
    pspecs = [pl.BlockSpec((p.shape[0], tm, C), lambda i: (0, i, 0)) for p in p3]
    outs = _pc(body, name=name, grid=(R // tm,), in_specs=[blk, blk, blk] + pspecs,
               out_specs=[blk] * 4, out_shape=[SDS((R, C), F32)] * 4)(w2, m2, v2, *p3)
    return [o.reshape(shp) for o in outs]


def _place():
    return lax.axis_index("x"), lax.axis_index("y"), lax.axis_index("c")


def all_gather_many(xs, *, name):
    n = len(xs)

    def body(*refs):
        x_refs, o_refs = refs[:n], refs[n:2 * n]
        send_sems, recv_sems, local_sems = refs[2 * n:]
        x, y, c = _place()
        me, sibling = (x, y, c), (x, y, 1 - c)
        chips = [(1 - x, y), (x, 1 - y), (1 - x, 1 - y)]

        def rows(a, p):
            return o_refs[a].at[4 * p[0] + 2 * p[1] + p[2]]

        def copy(a, k, block, to, src=None):
            return pltpu.make_async_remote_copy(
                src_ref=rows(a, block) if src is None else src, dst_ref=rows(a, block),
                send_sem=send_sems.at[7 * a + k], recv_sem=recv_sems.at[7 * a + k], device_id=to, device_id_type=MESH)

        mine = [pltpu.make_async_copy(x_refs[a], rows(a, me), local_sems.at[a]) for a in range(n)]
        for cp in mine:
            cp.start()
        first = []
        for a in range(n):
            first.append(copy(a, 0, me, sibling, src=x_refs[a]))
            first += [copy(a, 1 + j, me, (*chip, c), src=x_refs[a]) for j, chip in enumerate(chips)]
        for cp in first:
            cp.start()
        passed = []
        for j, chip in enumerate(chips):
            for a in range(n):
                copy(a, 1 + j, (*chip, c), me).wait_recv()
                cp = copy(a, 4 + j, (*chip, c), sibling)
                cp.start()
                passed.append(cp)
        for a in range(n):
            copy(a, 0, sibling, me).wait_recv()
            for j, chip in enumerate(chips):
                copy(a, 4 + j, (*chip, 1 - c), me).wait_recv()
        for cp in first + passed:
            cp.wait_send()
        for cp in mine:
            cp.wait()

    return pl.pallas_call(
        body, name=name, in_specs=[ANY] * n, out_specs=[ANY] * n,
        out_shape=[SDS((N_DEV,) + a.shape, a.dtype) for a in xs],
        scratch_shapes=[pltpu.SemaphoreType.DMA((7 * n,)), pltpu.SemaphoreType.DMA((7 * n,)), pltpu.SemaphoreType.DMA((n,))],
    )(*xs)


def _stage(ins, out_shape, n_peers, copy_of):
    ins = list(ins)

    def copies(in_refs, out_refs, send_sems, recv_sems):
        place = _place()
        out = []
        for a in range(len(ins)):
            for k in range(n_peers):
                src, dst, peer = copy_of(in_refs[a], out_refs[a], k, place)
                out.append(pltpu.make_async_remote_copy(
                    src_ref=src, dst_ref=dst, send_sem=send_sems.at[n_peers * a + k], recv_sem=recv_sems.at[n_peers * a + k],
                    device_id=peer, device_id_type=MESH))
        return out

    return dict(ins=ins, out_shape=list(out_shape), sems=n_peers * len(ins), copies=copies)


def _other_chips(x, y):
    return [(1 - x, y), (x, 1 - y), (1 - x, 1 - y)]


def stage_gather_direct(blocks):
    def copy_of(src, dst, k, place):
        x, y, c = place
        peer = (x, y, 1 - c) if k == 0 else (*_other_chips(x, y)[k - 1], c)
        return src, dst.at[4 * x + 2 * y + c], peer

    return _stage(blocks, [SDS((N_DEV,) + b.shape, b.dtype) for b in blocks], 4, copy_of)


def stage_gather_forward(bufs):
    def copy_of(src, dst, k, place):
        x, y, c = place
        cx, cy = _other_chips(x, y)[k]
        slot = 4 * cx + 2 * cy + c
        return src.at[slot], dst.at[slot], (x, y, 1 - c)

    st = _stage(bufs, [SDS(b.shape, b.dtype) for b in bufs], 3, copy_of)
    st["alias"] = True
    return st


def stage_rs_sibling(g8s):
    def copy_of(src, dst, k, place):
        x, y, c = place
        return src.at[2 * k + (1 - c)], dst.at[k], (x, y, 1 - c)

    return _stage(g8s, [SDS((4,) + g.shape[1:], g.dtype) for g in g8s], 4, copy_of)


def stage_rs_chips(sums):
    def copy_of(src, dst, k, place):
        x, y, c = place
        chip = _other_chips(x, y)[k]
        return src.at[2 * chip[0] + chip[1]], dst.at[k], (*chip, c)

    return _stage(sums, [SDS((3,) + s.shape[1:], s.dtype) for s in sums], 3, copy_of)


def run_stage(stage, *, name):
    n_in, n_out = len(stage["ins"]), len(stage["out_shape"])

    def body(*refs):
        cps = stage["copies"](refs[:n_in], refs[n_in:n_in + n_out], refs[-2], refs[-1])
        for cp in cps:
            cp.start()
        for cp in cps:
            cp.wait()

    return pl.pallas_call(
        body, name=name, in_specs=[ANY] * n_in, out_specs=[ANY] * n_out, out_shape=stage["out_shape"],
        scratch_shapes=[pltpu.SemaphoreType.DMA((stage["sems"],)), pltpu.SemaphoreType.DMA((stage["sems"],))],
    )(*stage["ins"])


def with_own_block(buf, own):
    x, y, c = _place()
    return lax.dynamic_update_index_in_dim(buf, own, 4 * x + 2 * y + c, 0)


def _owner_major(full, axis):
    shp = full.shape
    r = full.reshape(shp[:axis] + (N_DEV, shp[axis] // N_DEV) + shp[axis + 1:])
    return jnp.moveaxis(r, axis, 0)


def _from_owner_major(g8, axis):
    r = jnp.moveaxis(g8, 0, axis)
    shp = r.shape
    return r.reshape(shp[:axis] + (shp[axis] * shp[axis + 1],) + shp[axis + 2:])


def _perm_w_in(w):
    z = jnp.zeros((w.shape[0], LANES), w.dtype)
    dt = jnp.pad(w[:, 2560:2576], ((0, 0), (0, LANES - SSD_HEADS)))
    kpe = jnp.pad(w[:, 3728:3760], ((0, 0), (PE_LO, LANES - PE_HI)))
    return jnp.concatenate([w[:, 0:1024], w[:, 1024:2560], w[:, 2576:3088], w[:, 3088:3472], z, w[:, 3472:3728], dt, kpe], axis=1)


def _unperm_w_in(g):
    return jnp.concatenate([g[:, Z0:Z0 + 1024], g[:, XBC0:XBC0 + 1536], g[:, DT0:DT0 + SSD_HEADS], g[:, U0:U0 + 512],
                            g[:, CQ0:CQ0 + 384], g[:, CKV0:CKV0 + 256], g[:, KPE0 + PE_LO:KPE0 + PE_HI]], axis=1)


def _perm_w_uq(w):
    return jnp.pad(w.reshape(MLA_QR, MLA_H, MLA_QK), ((0, 0), (0, 0), (0, LANES - MLA_QK))).reshape(MLA_QR, MLA_H * LANES)


def _unperm_w_uq(g):
    return g.reshape(MLA_QR, MLA_H, LANES)[:, :, :MLA_QK].reshape(MLA_QR, MLA_H * MLA_QK)


def _perm_w_ukv(w):
    w3 = w.reshape(MLA_KVR, MLA_H, MLA_NOPE + MLA_V)
    pad = ((0, 0), (0, 0), (0, LANES - MLA_NOPE))
    k = jnp.pad(w3[:, :, :MLA_NOPE], pad).reshape(MLA_KVR, MLA_H * LANES)
    v = jnp.pad(w3[:, :, MLA_NOPE:], pad).reshape(MLA_KVR, MLA_H * LANES)
    return jnp.concatenate([k, v], axis=1)


def _unperm_w_ukv(g):
    k = g[:, :MLA_H * LANES].reshape(MLA_KVR, MLA_H, LANES)[:, :, :MLA_NOPE]
    v = g[:, MLA_H * LANES:].reshape(MLA_KVR, MLA_H, LANES)[:, :, :MLA_V]
    return jnp.concatenate([k, v], axis=2).reshape(MLA_KVR, MLA_H * (MLA_NOPE + MLA_V))


def _perm_w_out(w):
    m = jnp.pad(w[SSD_W + POOL_W:].reshape(MLA_H, MLA_V, D), ((0, 0), (0, LANES - MLA_V), (0, 0))).reshape(MLA_H * LANES, D)
    return jnp.concatenate([w[:SSD_W + POOL_W], m], axis=0)


def _unperm_w_out(g):
    m = g[SSD_W + POOL_W:].reshape(MLA_H, LANES, D)[:, :MLA_V].reshape(MLA_H * MLA_V, D)
    return jnp.concatenate([g[:SSD_W + POOL_W], m], axis=0)


def _lane_pad(v):
    return jnp.pad(v.reshape(1, -1), ((0, 0), (0, LANES - v.shape[-1])))


SMALL = ("attn_norm", "ssd_conv_b", "ssd_dt_bias", "ssd_a_log", "ssd_d", "ssd_norm", "pool_w", "pool_scale",
         "mla_q_norm", "mla_kv_norm", "ffn_norm", "ffn_conv_b", "final_norm")
SHARDED = {"w_in": 2, "ssd_conv_w": 2, "mla_w_uq": 2, "mla_w_ukv": 2, "w_out": 1, "ffn_w_up": 2, "ffn_conv_w": 2,
           "ffn_w_down": 1}
ALL_W = ("attn_norm", "w_in", "ssd_conv_w", "ssd_conv_b", "ssd_dt_bias", "ssd_a_log", "ssd_d", "ssd_norm", "pool_w",
         "pool_scale", "mla_q_norm", "mla_w_uq", "mla_kv_norm", "mla_w_ukv", "w_out", "ffn_norm", "ffn_w_up",
         "ffn_conv_w", "ffn_conv_b", "ffn_w_down", "final_norm")


def _pack_small(d):
    rows, layout = [], []
    for k in SMALL:
        a = d[k].reshape(-1)
        n = a.shape[0]
        r = -(-n // LANES)
        rows.append(jnp.pad(a, (0, r * LANES - n)).reshape(r, LANES))
        layout.append((k, n, r, d[k].shape))
    buf = jnp.concatenate(rows, axis=0)
    pad = (-buf.shape[0]) % 8
    return jnp.pad(buf, ((0, pad), (0, 0))), layout


def _unpack_small(buf, layout):
    out, r0 = {}, 0
    for k, n, r, shp in layout:
        out[k] = buf[r0:r0 + r].reshape(-1)[:n].reshape(shp)
        r0 += r
    return out


def _layer_fwd(l, x, W, cs, sn, Bl, next_blocks=None):
    T = x.shape[0]
    S = T // Bl
    n = f"l{l}_"
    h = rms_fwd(x, W["attn_norm"], name=n + "attn_norm")
    proj = mm(h, W["w_in"], name=n + "w_in")
    proj3 = proj.reshape(Bl, S, PW)
    xbc3 = conv_silu_fwd(proj3, W["ssd_conv_w"], W["ssd_conv_b"], name=n + "ssd_conv")
    y3, hin = ssd_fwd(xbc3, proj3, W["ssd_dt_bias"], W["ssd_a_log"], W["ssd_d"], name=n + "ssd_scan")
    y = y3.reshape(T, SSD_W)
    y_ssd = gated_rms_fwd(y, proj, W["ssd_norm"], name=n + "ssd_gate_norm")
    y_pool = pool_fwd(proj3, W["pool_w"], W["pool_scale"], name=n + "pool").reshape(T, POOL_W)
    qn = rms_fwd(proj, W["mla_q_norm"], col0=CQ0, width=MLA_QR, name=n + "q_norm")
    kvn = rms_fwd(proj, W["mla_kv_norm"], col0=CKV0, width=MLA_KVR, name=n + "kv_norm")
    qt = mm(qn, W["mla_w_uq"], name=n + "w_uq")
    kvt = mm(kvn, W["mla_w_ukv"], name=n + "w_ukv")
    qr, kr, vr = mla_prep_fwd(qt, kvt, proj, cs, sn, name=n + "rope")
    direct = stage_gather_direct(next_blocks) if next_blocks is not None else None
    o, lse, lset = flash_fwd(qr, kr, vr, Bl, comm=direct, name=n + "attn")
    ycat = jnp.concatenate([y_ssd, y_pool, o], axis=1)
    x1 = mm(ycat, W["w_out"], add=x, name=n + "w_out")
    h2 = rms_fwd(x1, W["ffn_norm"], name=n + "ffn_norm")
    forward = stage_gather_forward(direct["result"]) if direct else None
    pre = mm(h2, W["ffn_w_up"], comm=forward, name=n + "w_up")
    gathered = [with_own_block(buf, b) for buf, b in zip(forward["result"], next_blocks)] if direct else None
    pre3 = pre.reshape(Bl, S, 2 * DFF)
    act = ffn_act_fwd(pre3, W["ffn_conv_w"], W["ffn_conv_b"], name=n + "ffn_act").reshape(T, DFF)
    x2 = mm(act, W["ffn_w_down"], add=x1, name=n + "w_down")
    saved = dict(x=x, h=h, proj=proj, xbc3=xbc3, hin=hin, y=y, qn=qn, kvn=kvn, vr=vr, qr=qr, kr=kr, o=o, lse=lse, lset=lset,
                 ycat=ycat, x1=x1, h2=h2, pre3=pre3, act=act)
    return x2, saved, gathered


def _keep_column(g8):
    ci = lax.axis_index("c")
    return lax.dynamic_index_in_dim(g8.reshape((4, 2) + g8.shape[1:]), ci, 1, keepdims=False)


def _layer_bwd(l, dx2, dx2b, W, sv, cs, sn, Bl, later_g8=None):
    T = dx2.shape[0]
    S = T // Bl
    n = f"l{l}_b_"
    g = {}
    g["ffn_w_down"] = mm(sv["act"], dx2b, ta=True, name=n + "dw_down")
    dact = mm(dx2b, W["ffn_w_down"], tb=True, name=n + "dact")
    to_sibling = stage_rs_sibling(later_g8) if later_g8 is not None else None
    dpg, dpv, dwg, dwv, dbg, dbv = ffn_act_bwd(sv["pre3"], W["ffn_conv_w"], W["ffn_conv_b"], dact.reshape(Bl, S, DFF),
                                               comm=to_sibling, name=n + "ffn_act")
    to_chips = sums = None
    if to_sibling:
        sums = [add2(_keep_column(g8), r, name=f"{n}rs_add{a}") for a, (g8, r) in enumerate(zip(later_g8, to_sibling["result"]))]
        to_chips = stage_rs_chips([sb for _, sb in sums])
    g["ffn_conv_w"] = jnp.concatenate([dwg, dwv], axis=1)
    g["ffn_conv_b"] = jnp.concatenate([dbg, dbv], axis=1)
    dpg, dpv = dpg.reshape(T, DFF), dpv.reshape(T, DFF)
    g["ffn_w_up"] = jnp.concatenate([mm(sv["h2"], dpg, ta=True, name=n + "dw_up_g"),
                                     mm(sv["h2"], dpv, ta=True, name=n + "dw_up_v")], axis=1)
    dh2 = mm(dpg, W["ffn_w_up"], tb=True, name=n + "dh2_g")
    dh2 = mm(dpv, W["ffn_w_up"], tb=True, b_k0=DFF, add=dh2, name=n + "dh2_v")
    dx1, dx1b, g["ffn_norm"] = rms_bwd(sv["x1"], W["ffn_norm"], dh2, add=dx2, name=n + "ffn_norm")
    g["w_out"] = mm(sv["ycat"], dx1b, ta=True, name=n + "dw_out")
    dycat = mm(dx1b, W["w_out"], tb=True, name=n + "dycat")
    proj, proj3 = sv["proj"], sv["proj"].reshape(Bl, S, PW)
    dy, dz, g["ssd_norm"] = gated_rms_bwd(sv["y"], proj, W["ssd_norm"], dycat, name=n + "ssd_gate_norm")
    dxa, ddt, dpar, dd = ssd_bwd(sv["xbc3"], proj3, sv["hin"], dy.reshape(Bl, S, SSD_W), W["ssd_dt_bias"], W["ssd_a_log"],
                                 W["ssd_d"], comm=to_chips, name=n + "ssd_scan")
    reduced = ([s32 for s32, _ in sums], to_chips["result"]) if to_chips else None
    g["ssd_dt_bias"] = dpar[0, :SSD_HEADS]
    g["ssd_a_log"] = dpar[1, :SSD_HEADS]
    g["ssd_d"] = dd[:, :2].reshape(SSD_HEADS)
    dxbc, g["ssd_conv_w"], g["ssd_conv_b"] = conv_silu_bwd(proj3, W["ssd_conv_w"], W["ssd_conv_b"], dxa, name=n + "ssd_conv")
    du, g["pool_w"], g["pool_scale"] = pool_bwd(proj3, W["pool_w"], W["pool_scale"], dycat.reshape(Bl, S, YCAT), name=n + "pool")
    dqr, dlt = flash_bwd_dq(sv["qr"], sv["kr"], sv["vr"], sv["o"], sv["lse"], dycat, Bl, name=n + "attn_dq")
    dkr, dv = flash_bwd_dkv(sv["qr"], sv["kr"], sv["vr"], sv["lset"], dlt, dycat, Bl, name=n + "attn_dkv")
    dqt, dkt, dkpe = mla_prep_bwd(dqr, dkr, cs, sn, name=n + "rope")
    g["mla_w_ukv"] = jnp.concatenate([mm(sv["kvn"], dkt, ta=True, name=n + "dw_uk"),
                                      mm(sv["kvn"], dv, ta=True, name=n + "dw_uv")], axis=1)
    dkvn = mm(dkt, W["mla_w_ukv"], tb=True, name=n + "dkvn_k")
    dkvn = mm(dv, W["mla_w_ukv"], tb=True, b_k0=MLA_H * LANES, add=dkvn, name=n + "dkvn_v")
    g["mla_w_uq"] = mm(sv["qn"], dqt, ta=True, name=n + "dw_uq")
    dqn = mm(dqt, W["mla_w_uq"], tb=True, name=n + "dqn")
    dcq, g["mla_q_norm"] = rms_bwd(proj, W["mla_q_norm"], dqn, col0=CQ0, width=MLA_QR, name=n + "q_norm")
    dckv, g["mla_kv_norm"] = rms_bwd(proj, W["mla_kv_norm"], dkvn, col0=CKV0, width=MLA_KVR, name=n + "kv_norm")
    dproj = jnp.concatenate([dz, dxbc.reshape(T, SSD_XBC), du.reshape(T, POOL_W), dcq, jnp.zeros((T, LANES), F32), dckv,
                             ddt.reshape(T, LANES), dkpe], axis=1).astype(BF16)
    g["w_in"] = mm(sv["h"], dproj, ta=True, name=n + "dw_in")
    dh = mm(dproj, W["w_in"], tb=True, name=n + "dh")
    dx, dxb, g["attn_norm"] = rms_bwd(sv["x"], W["attn_norm"], dh, add=dx1, name=n + "attn_norm")
    return dx, dxb, g, reduced


def kernel(x, positions, attn_norm, w_in, ssd_conv_w, ssd_conv_b, ssd_dt_bias, ssd_a_log, ssd_d, ssd_norm, pool_w, pool_scale, mla_q_norm, mla_w_uq, mla_kv_norm, mla_w_ukv, w_out, ffn_norm, ffn_w_up, ffn_conv_w, ffn_conv_b, ffn_w_down, final_norm, loss_target, m_attn_norm, m_w_in, m_ssd_conv_w, m_ssd_conv_b, m_ssd_dt_bias, m_ssd_a_log, m_ssd_d, m_ssd_norm, m_pool_w, m_pool_scale, m_mla_q_norm, m_mla_w_uq, m_mla_kv_norm, m_mla_w_ukv, m_w_out, m_ffn_norm, m_ffn_w_up, m_ffn_conv_w, m_ffn_conv_b, m_ffn_w_down, m_final_norm, v_attn_norm, v_w_in, v_ssd_conv_w, v_ssd_conv_b, v_ssd_dt_bias, v_ssd_a_log, v_ssd_d, v_ssd_norm, v_pool_w, v_pool_scale, v_mla_q_norm, v_mla_w_uq, v_mla_kv_norm, v_mla_w_ukv, v_w_out, v_ffn_norm, v_ffn_w_up, v_ffn_conv_w, v_ffn_conv_b, v_ffn_w_down, v_final_norm):
    a = locals()
    Wt = {k: a[k] for k in ALL_W}
    Mo = {k: a["m_" + k] for k in ALL_W}
    Vo = {k: a["v_" + k] for k in ALL_W}
    Bl, S, _ = x.shape
    T = Bl * S

    names = list(SHARDED)
    conv = ("ssd_conv_w", "ffn_conv_w")

    def blocks_of(l):
        return [Wt[k][l] if k in conv else Wt[k][l].astype(BF16) for k in names]

    def layer_weights(l, gathered):
        full = {k: _from_owner_major(g8, SHARDED[k] - 1) for k, g8 in zip(names, gathered)}
        return {
            "attn_norm": attn_norm[l].reshape(1, D), "w_in": _perm_w_in(full["w_in"]),
            "ssd_conv_w": full["ssd_conv_w"], "ssd_conv_b": ssd_conv_b[l].reshape(1, SSD_XBC),
            "ssd_dt_bias": _lane_pad(ssd_dt_bias[l]), "ssd_a_log": _lane_pad(ssd_a_log[l]),
            "ssd_d": jnp.repeat(ssd_d[l].reshape(NPAIR, 2), SSD_P, axis=1), "ssd_norm": ssd_norm[l].reshape(1, SSD_W),
            "pool_w": pool_w[l].astype(BF16), "pool_scale": pool_scale[l].reshape(1, POOL_W),
            "mla_q_norm": mla_q_norm[l].reshape(1, MLA_QR), "mla_w_uq": _perm_w_uq(full["mla_w_uq"]),
            "mla_kv_norm": mla_kv_norm[l].reshape(1, MLA_KVR), "mla_w_ukv": _perm_w_ukv(full["mla_w_ukv"]),
            "w_out": _perm_w_out(full["w_out"]), "ffn_norm": ffn_norm[l].reshape(1, D),
            "ffn_w_up": full["ffn_w_up"], "ffn_conv_w": full["ffn_conv_w"],
            "ffn_conv_b": ffn_conv_b[l].reshape(1, 2 * DFF), "ffn_w_down": full["ffn_w_down"]}

    pos = positions.astype(F32).reshape(T, 1)
    inv_freq = ROPE_THETA ** (-jnp.arange(0, MLA_ROPE, 2, dtype=F32) / MLA_ROPE)
    invf = jnp.pad(jnp.concatenate([inv_freq, inv_freq]), (PE_LO, LANES - PE_HI)).reshape(1, LANES)
    cs, sn = rope_tables(pos, invf, name="rope_tables")

    layers = [layer_weights(0, all_gather_many(blocks_of(0), name="gather_weights_l0"))]
    xc = x.reshape(T, D)
    saved = []
    for l in range(DEPTH):
        xc, sv, gathered = _layer_fwd(l, xc, layers[l], cs, sn, Bl, next_blocks=blocks_of(l + 1) if l + 1 < DEPTH else None)
        saved.append(sv)
        if gathered is not None:
            layers.append(layer_weights(l + 1, gathered))
    dx, dxb, g_final, loss_part = final_loss(xc, final_norm.reshape(1, D), loss_target.reshape(T, D), name="final_loss")

    unperm = {"w_in": _unperm_w_in, "mla_w_uq": _unperm_w_uq, "mla_w_ukv": _unperm_w_ukv, "w_out": _unperm_w_out}

    def by_owner(g):
        out = []
        for k in names:
            shp = list(Wt[k].shape[1:])
            shp[SHARDED[k] - 1] *= N_DEV
            out.append(_owner_major(unperm.get(k, lambda t: t)(g[k]).reshape(shp), SHARDED[k] - 1))
        return out

    grads, reduced = [None] * DEPTH, [None] * DEPTH
    later_g8 = None
    for l in reversed(range(DEPTH)):
        dx, dxb, grads[l], red = _layer_bwd(l, dx, dxb, layers[l], saved[l], cs, sn, Bl, later_g8=later_g8)
        if red is not None:
            reduced[l + 1] = red
        later_g8 = by_owner(grads[l])
    loss = lax.psum(loss_part[0, 0], AXES)
    from_sibling = run_stage(stage_rs_sibling(later_g8), name="rs_sibling_l0")
    sums = [add2(_keep_column(g8), r, name=f"rs_add_l0_{a}") for a, (g8, r) in enumerate(zip(later_g8, from_sibling))]
    reduced[0] = ([s32 for s32, _ in sums], run_stage(stage_rs_chips([sb for _, sb in sums]), name="rs_chips_l0"))

    xi, yi, _ = _place()
    chip = 2 * xi + yi
    out_g, out_d, out_m, out_v = {}, {}, {}, {}
    for a, k in enumerate(names):
        per_layer = []
        for l in range(DEPTH):
            own = lax.dynamic_index_in_dim(reduced[l][0][a], chip, 0, keepdims=True)
            per_layer.append(adamw(Wt[k][l], Mo[k][l], Vo[k][l], [own, reduced[l][1][a]], name=f"adamw_l{l}_{k}"))
        out_g[k], out_d[k], out_m[k], out_v[k] = (jnp.stack([per_layer[l][i] for l in range(DEPTH)]) for i in range(4))

    part = {k: g_final.reshape(D) if k == "final_norm" else
            jnp.stack([grads[l][k].reshape(Wt[k].shape[1:]) for l in range(DEPTH)]) for k in SMALL}
    pg, layout = _pack_small(part)
    pw, _ = _pack_small(Wt)
    pm, _ = _pack_small(Mo)
    pv, _ = _pack_small(Vo)
    (pg8,) = all_gather_many([pg], name="gather_small_grads")
    sg, sd, sm, sv_ = adamw(pw, pm, pv, [pg8], name="adamw_small")
    for dst, buf in ((out_g, sg), (out_d, sd), (out_m, sm), (out_v, sv_)):
        dst.update(_unpack_small(buf, layout))

    return (loss, dx.reshape(Bl, S, D), *[out_g[k] for k in ALL_W], *[out_d[k] for k in ALL_W],
            *[out_m[k] for k in ALL_W], *[out_v[k] for k in ALL_W])
```

```python
import functools
import math

import jax
import jax.numpy as jnp
from jax import lax
from jax.experimental import pallas as pl
from jax.experimental.pallas import tpu as pltpu

F32, BF16 = jnp.float32, jnp.bfloat16
SDS = jax.ShapeDtypeStruct
MESH = pl.DeviceIdType.MESH
AXES = ("x", "y", "c")
N_DEV = 8

D = 1024
EPS = 1e-6
SSD_HEADS, SSD_P, SSD_W, SSD_G, SSD_N, SSD_K, SSD_L, SSD_XBC = 16, 64, 1024, 2, 128, 4, 128, 1536
POOL_G, POOL_D, POOL_W, POOL_WIN = 4, 128, 512, (2, 4, 8, 16)
MLA_H, MLA_QR, MLA_KVR, MLA_NOPE, MLA_ROPE, MLA_V, MLA_QK = 8, 384, 256, 64, 32, 64, 96
ROPE_THETA = 10000.0
MIX = 2048
DFF, FFN_K = 2816, 3
DEPTH = 2
ADAM_LR, ADAM_B1, ADAM_B2, ADAM_EPS, ADAM_WD, ADAM_STEP = 0.001, 0.9, 0.999, 1e-08, 0.01, 10

Z0, XBC0, U0, CQ0, CKV0, DT0, KPE0, PW = 0, 1024, 2560, 3072, 3584, 3840, 3968, 4096
LANES = 128
YCAT = SSD_W + POOL_W + MLA_H * LANES
NEG = -1e30
VMEM_LIMIT = 56 * 1024 * 1024
MM_ROW_TILE, MM_LANE_TILE, MM_FULL_K = 1024, 1408, 2816


def _tile(n, pref, mult):
    if n <= pref:
        return n
    for d in range(pref, 0, -mult):
        if d % mult == 0 and n % d == 0:
            return d
    return n


def _dg(a, b, ca, cb, prec=None):
    return lax.dot_general(a, b, (((ca,), (cb,)), ((), ())), preferred_element_type=F32, precision=prec)


def _nn(a, b):
    return _dg(a.astype(BF16), b.astype(BF16), 1, 0)


def _nt(a, b):
    return _dg(a.astype(BF16), b.astype(BF16), 1, 1)


def _tn(a, b):
    return _dg(a.astype(BF16), b.astype(BF16), 0, 0)


def _sig(x):
    return jax.nn.sigmoid(x)


def _silu(x):
    return x * _sig(x)


def _dsilu(x):
    s = _sig(x)
    return s * (1.0 + x * (1.0 - s))


ANY = pl.BlockSpec(memory_space=pl.ANY)


def _pc(body, *, name, grid, in_specs, out_specs, out_shape, scratch=(), comm=None):
    params = pltpu.CompilerParams(vmem_limit_bytes=VMEM_LIMIT)
    if comm is None:
        return pl.pallas_call(body, name=name, grid=grid, in_specs=in_specs, out_specs=out_specs, out_shape=out_shape,
                              scratch_shapes=list(scratch), compiler_params=params)
    single = not isinstance(out_shape, (list, tuple))
    o_specs = [out_specs] if single else list(out_specs)
    o_shape = [out_shape] if single else list(out_shape)
    ni, no, ns = len(in_specs), len(o_specs), len(scratch)
    nci, nco = len(comm["ins"]), len(comm["out_shape"])

    def fused(*refs):
        ins, cins = refs[:ni], refs[ni:ni + nci]
        outs, couts = refs[ni + nci:ni + nci + no], refs[ni + nci + no:ni + nci + no + nco]
        scr = refs[ni + nci + no + nco:ni + nci + no + nco + ns]
        send_sems, recv_sems = refs[-2:]
        copies = comm["copies"](cins, couts, send_sems, recv_sems)
        first = functools.reduce(jnp.logical_and, [pl.program_id(d) == 0 for d in range(len(grid))])
        last = functools.reduce(jnp.logical_and, [pl.program_id(d) == grid[d] - 1 for d in range(len(grid))])

        @pl.when(first)
        def _():
            for cp in copies:
                cp.start()

        body(*ins, *outs, *scr)

        @pl.when(last)
        def _():
            for cp in copies:
                cp.wait()

    call = pl.pallas_call(
        fused, name=name, grid=grid, in_specs=list(in_specs) + [ANY] * nci, out_specs=o_specs + [ANY] * nco,
        out_shape=o_shape + list(comm["out_shape"]),
        scratch_shapes=list(scratch) + [pltpu.SemaphoreType.DMA((comm["sems"],)), pltpu.SemaphoreType.DMA((comm["sems"],))],
        input_output_aliases={ni + a: no + a for a in range(nci)} if comm.get("alias") else {},
        compiler_params=params)

    def run(*args):
        res = call(*args, *comm["ins"])
        comm["result"] = list(res[no:])
        return res[0] if single else list(res[:no])

    return run


def _rsum(x):
    return jnp.sum(x, axis=1, keepdims=True)


def _csum(x):
    return jnp.sum(x, axis=0, keepdims=True)


def mm(a, b, *, ta=False, tb=False, add=None, out_dtype=F32, b_k0=0, comm=None, name):
    M, K = (a.shape[1], a.shape[0]) if ta else a.shape
    N = b.shape[0] if tb else b.shape[1]
    assert tb or b_k0 == 0
    tm = _tile(M, MM_LANE_TILE, LANES) if ta else _tile(M, MM_ROW_TILE, 8)
    tn = _tile(N, MM_LANE_TILE, LANES)
    if ta:
        tk = _tile(K, MM_ROW_TILE, 8)
    else:
        tk = K if K <= MM_FULL_K else _tile(K, 2048, LANES)
    nk = K // tk

    def body(*refs):
        if add is None:
            a_ref, b_ref, o_ref = refs[:3]
        else:
            a_ref, b_ref, add_ref, o_ref = refs[:4]
        part = _dg(a_ref[...].astype(BF16), b_ref[...].astype(BF16), 0 if ta else 1, 1 if tb else 0)

        def finish(r):
            if add is not None:
                r = r + add_ref[...].astype(F32)
            o_ref[...] = r.astype(out_dtype)

        if nk == 1:
            finish(part)
            return
        acc = refs[-1]
        k = pl.program_id(2)

        @pl.when(k == 0)
        def _():
            acc[...] = part

        @pl.when(k > 0)
        def _():
            acc[...] += part

        @pl.when(k == nk - 1)
        def _():
            finish(acc[...])

    a_spec = pl.BlockSpec((tk, tm), lambda i, j, k: (k, i)) if ta else pl.BlockSpec((tm, tk), lambda i, j, k: (i, k))
    assert b_k0 % tk == 0
    kb0 = b_k0 // tk
    b_spec = pl.BlockSpec((tn, tk), lambda i, j, k: (j, kb0 + k)) if tb else pl.BlockSpec((tk, tn), lambda i, j, k: (k, j))
    o_spec = pl.BlockSpec((tm, tn), lambda i, j, k: (i, j))
    ins, specs = [a, b], [a_spec, b_spec]
    if add is not None:
        ins.append(add)
        specs.append(o_spec)
    return _pc(body, name=name, grid=(M // tm, N // tn, nk), in_specs=specs, out_specs=o_spec,
               out_shape=SDS((M, N), out_dtype), scratch=[pltpu.VMEM((tm, tn), F32)] if nk > 1 else [], comm=comm)(*ins)


def rms_fwd(x, g, *, col0=0, width=None, name):
    T = x.shape[0]
    W = width or x.shape[1]
    tm = _tile(T, 512, 8)

    def body(x_ref, g_ref, o_ref):
        v = x_ref[...]
        r = lax.rsqrt(jnp.mean(v * v, axis=1, keepdims=True) + EPS)
        o_ref[...] = ((v * r) * g_ref[...]).astype(BF16)

    return _pc(body, name=name, grid=(T // tm,),
               in_specs=[pl.BlockSpec((tm, W), lambda i: (i, col0 // W)), pl.BlockSpec((1, W), lambda i: (0, 0))],
               out_specs=pl.BlockSpec((tm, W), lambda i: (i, 0)), out_shape=SDS((T, W), BF16))(x, g)


def rms_bwd(x, g, dh, *, col0=0, width=None, add=None, name):
    T = x.shape[0]
    W = width or x.shape[1]
    tm = _tile(T, 512, 8)

    def body(*refs):
        if add is None:
            x_ref, g_ref, dh_ref, dx_ref, dg_ref = refs
        else:
            x_ref, g_ref, dh_ref, add_ref, dx_ref, dxb_ref, dg_ref = refs
        v = x_ref[...]
        r = lax.rsqrt(jnp.mean(v * v, axis=1, keepdims=True) + EPS)
        xh = v * r
        d = dh_ref[...].astype(F32)
        dxh = d * g_ref[...]
        dx = r * (dxh - xh * jnp.mean(dxh * xh, axis=1, keepdims=True))
        if add is not None:
            dx = dx + add_ref[...]
            dxb_ref[...] = dx.astype(BF16)
        dx_ref[...] = dx.astype(dx_ref.dtype)

        @pl.when(pl.program_id(0) == 0)
        def _():
            dg_ref[...] = jnp.zeros_like(dg_ref)

        dg_ref[...] += _csum(d * xh)

    row = pl.BlockSpec((tm, W), lambda i: (i, 0))
    vec = pl.BlockSpec((1, W), lambda i: (0, 0))
    ins = [x, g, dh] + ([] if add is None else [add])
    specs = [pl.BlockSpec((tm, W), lambda i: (i, col0 // W)), vec, row] + ([] if add is None else [row])
    if add is None:
        return _pc(body, name=name, grid=(T // tm,), in_specs=specs, out_specs=[row, vec],
                   out_shape=[SDS((T, W), BF16), SDS((1, W), F32)])(*ins)
    return _pc(body, name=name, grid=(T // tm,), in_specs=specs, out_specs=[row, row, vec],
               out_shape=[SDS((T, W), F32), SDS((T, W), BF16), SDS((1, W), F32)])(*ins)


def gated_rms_fwd(y, proj, g, *, name):
    T = y.shape[0]
    tm = _tile(T, 512, 8)

    def body(y_ref, z_ref, g_ref, o_ref):
        v = y_ref[...] * _silu(z_ref[...])
        r = lax.rsqrt(jnp.mean(v * v, axis=1, keepdims=True) + EPS)
        o_ref[...] = ((v * r) * g_ref[...]).astype(BF16)

    row = pl.BlockSpec((tm, SSD_W), lambda i: (i, 0))
    return _pc(body, name=name, grid=(T // tm,), in_specs=[row, row, pl.BlockSpec((1, SSD_W), lambda i: (0, 0))],
               out_specs=row, out_shape=SDS((T, SSD_W), BF16))(y, proj, g)


def gated_rms_bwd(y, proj, g, dycat, *, name):
    T = y.shape[0]
    tm = _tile(T, 512, 8)

    def body(y_ref, z_ref, g_ref, d_ref, dy_ref, dz_ref, dg_ref):
        yv, z = y_ref[...], z_ref[...]
        sz = _silu(z)
        v = yv * sz
        r = lax.rsqrt(jnp.mean(v * v, axis=1, keepdims=True) + EPS)
        vh = v * r
        d = d_ref[...]
        dvh = d * g_ref[...]
        dv = r * (dvh - vh * jnp.mean(dvh * vh, axis=1, keepdims=True))
        dy_ref[...] = dv * sz
        dz_ref[...] = (dv * yv * _dsilu(z)).astype(BF16)

        @pl.when(pl.program_id(0) == 0)
        def _():
            dg_ref[...] = jnp.zeros_like(dg_ref)

        dg_ref[...] += _csum(d * vh)

    row = pl.BlockSpec((tm, SSD_W), lambda i: (i, 0))
    vec = pl.BlockSpec((1, SSD_W), lambda i: (0, 0))
    return _pc(body, name=name, grid=(T // tm,), in_specs=[row, row, vec, row], out_specs=[row, row, vec],
               out_shape=[SDS((T, SSD_W), F32), SDS((T, SSD_W), BF16), SDS((1, SSD_W), F32)])(y, proj, g, dycat)


def final_loss(x, g, tgt, *, name):
    T = x.shape[0]
    tm = _tile(T, 512, 8)

    def body(x_ref, g_ref, t_ref, dx_ref, dxb_ref, dg_ref, l_ref):
        v = x_ref[...]
        gg = g_ref[...]
        r = lax.rsqrt(jnp.mean(v * v, axis=1, keepdims=True) + EPS)
        xh = v * r
        err = xh * gg - t_ref[...]
        part = 0.5 * _csum(jnp.mean(err * err, axis=1, keepdims=True))
        d = err * (1.0 / D)
        dxh = d * gg
        dx = r * (dxh - xh * jnp.mean(dxh * xh, axis=1, keepdims=True))
        dx_ref[...] = dx
        dxb_ref[...] = dx.astype(BF16)

        @pl.when(pl.program_id(0) == 0)
        def _():
            dg_ref[...] = jnp.zeros_like(dg_ref)
            l_ref[...] = jnp.zeros_like(l_ref)

        dg_ref[...] += _csum(d * xh)
        l_ref[...] += jnp.broadcast_to(part, (1, LANES))

    row = pl.BlockSpec((tm, D), lambda i: (i, 0))
    vec = pl.BlockSpec((1, D), lambda i: (0, 0))
    return _pc(body, name=name, grid=(T // tm,), in_specs=[row, vec, row],
               out_specs=[row, row, vec, pl.BlockSpec((1, LANES), lambda i: (0, 0))],
               out_shape=[SDS((T, D), F32), SDS((T, D), BF16), SDS((1, D), F32), SDS((1, LANES), F32)])(x, g, tgt)


HALO = 8


def _prev_map(ts, col):
    return lambda b, i, j: (b, jnp.maximum(i * (ts // HALO) - 1, 0), col(j))


def _next_map(ts, n_halo_blocks, col):
    return lambda b, i, j: (b, jnp.minimum((i + 1) * (ts // HALO), n_halo_blocks - 1), col(j))


def conv_silu_fwd(proj3, w, b, *, name):
    Bl, S, _ = proj3.shape
    C, K = SSD_XBC, SSD_K
    ts, tc = _tile(S, 512, 8), 512
    c0 = XBC0 // tc

    def body(xp_ref, x_ref, w_ref, b_ref, o_ref, ext):
        i = pl.program_id(1)
        ext[0:HALO, :] = jnp.where(i > 0, xp_ref[0], 0.0)
        ext[HALO:HALO + ts, :] = x_ref[0]
        acc = b_ref[...] + w_ref[0:1, :] * ext[pl.ds(HALO - (K - 1), ts), :]
        for k in range(1, K):
            acc = acc + w_ref[k:k + 1, :] * ext[pl.ds(HALO - (K - 1) + k, ts), :]
        o_ref[0] = _silu(acc)

    return _pc(body, name=name, grid=(Bl, S // ts, C // tc),
               in_specs=[pl.BlockSpec((1, HALO, tc), _prev_map(ts, lambda j: c0 + j)),
                         pl.BlockSpec((1, ts, tc), lambda b, i, j: (b, i, c0 + j)),
                         pl.BlockSpec((K, tc), lambda b, i, j: (0, j)),
                         pl.BlockSpec((1, tc), lambda b, i, j: (0, j))],
               out_specs=pl.BlockSpec((1, ts, tc), lambda b, i, j: (b, i, j)),
               out_shape=SDS((Bl, S, C), F32), scratch=[pltpu.VMEM((HALO + ts, tc), F32)])(proj3, proj3, w, b)


def conv_silu_bwd(proj3, w, b, dact, *, name):
    Bl, S, _ = proj3.shape
    C, K = SSD_XBC, SSD_K
    ts, tc = _tile(S, 512, 8), 512
    c0 = XBC0 // tc
    ns = S // ts

    def body(xp_ref, x_ref, xn_ref, d_ref, dn_ref, w_ref, b_ref, dx_ref, dw_ref, db_ref, ext, ext2):
        bb, i = pl.program_id(1), pl.program_id(2)
        last = i == ns - 1
        ext[0:HALO, :] = jnp.where(i > 0, xp_ref[0], 0.0)
        ext[HALO:HALO + ts, :] = x_ref[0]
        ext[HALO + ts:2 * HALO + ts, :] = jnp.where(last, 0.0, xn_ref[0])
        taps = [ext[pl.ds(HALO - (K - 1) + k, ts + HALO), :] for k in range(K)]
        acc = b_ref[...] + w_ref[0:1, :] * taps[0]
        for k in range(1, K):
            acc = acc + w_ref[k:k + 1, :] * taps[k]
        dsl = _dsilu(acc)
        du = d_ref[0] * dsl[0:ts]
        ext2[0:ts, :] = du
        ext2[ts:ts + HALO, :] = jnp.where(last, 0.0, dn_ref[0]) * dsl[ts:ts + HALO]
        dx = w_ref[0:1, :] * ext2[pl.ds(K - 1, ts), :]
        for k in range(1, K):
            dx = dx + w_ref[k:k + 1, :] * ext2[pl.ds(K - 1 - k, ts), :]
        dx_ref[0] = dx.astype(BF16)

        @pl.when((bb == 0) & (i == 0))
        def _():
            dw_ref[...] = jnp.zeros_like(dw_ref)
            db_ref[...] = jnp.zeros_like(db_ref)

        for k in range(K):
            dw_ref[k:k + 1, :] += _csum(du * taps[k][0:ts])
        db_ref[...] += _csum(du)

    nhb = S // HALO
    cx = lambda j: c0 + j
    cj = lambda j: j
    return _pc(body, name=name, grid=(C // tc, Bl, ns),
               in_specs=[pl.BlockSpec((1, HALO, tc), lambda j, b, i: _prev_map(ts, cx)(b, i, j)),
                         pl.BlockSpec((1, ts, tc), lambda j, b, i: (b, i, c0 + j)),
                         pl.BlockSpec((1, HALO, tc), lambda j, b, i: _next_map(ts, nhb, cx)(b, i, j)),
                         pl.BlockSpec((1, ts, tc), lambda j, b, i: (b, i, j)),
                         pl.BlockSpec((1, HALO, tc), lambda j, b, i: _next_map(ts, nhb, cj)(b, i, j)),
                         pl.BlockSpec((K, tc), lambda j, b, i: (0, j)),
                         pl.BlockSpec((1, tc), lambda j, b, i: (0, j))],
               out_specs=[pl.BlockSpec((1, ts, tc), lambda j, b, i: (b, i, j)),
                          pl.BlockSpec((K, tc), lambda j, b, i: (0, j)),
                          pl.BlockSpec((1, tc), lambda j, b, i: (0, j))],
               out_shape=[SDS((Bl, S, C), BF16), SDS((K, C), F32), SDS((1, C), F32)],
               scratch=[pltpu.VMEM((2 * HALO + ts, tc), F32), pltpu.VMEM((HALO + ts, tc), F32)],
               )(proj3, proj3, proj3, dact, dact, w, b)


def ffn_act_fwd(pre3, w, b, *, name):
    Bl, S, _ = pre3.shape
    K = FFN_K
    ts, tc = _tile(S, 512, 8), 256
    nj = DFF // tc

    def body(gp_ref, g_ref, vp_ref, v_ref, wg_ref, wv_ref, bg_ref, bv_ref, o_ref, eg, ev):
        i = pl.program_id(1)
        outs = []
        for p_ref, m_ref, w_ref, b_ref, ext in ((gp_ref, g_ref, wg_ref, bg_ref, eg), (vp_ref, v_ref, wv_ref, bv_ref, ev)):
            ext[0:HALO, :] = jnp.where(i > 0, p_ref[0], 0.0)
            ext[HALO:HALO + ts, :] = m_ref[0]
            acc = b_ref[...] + w_ref[0:1, :] * ext[pl.ds(HALO - (K - 1), ts), :]
            for k in range(1, K):
                acc = acc + w_ref[k:k + 1, :] * ext[pl.ds(HALO - (K - 1) + k, ts), :]
            outs.append(acc)
        o_ref[0] = (_silu(outs[0]) * outs[1]).astype(BF16)

    main = lambda off: pl.BlockSpec((1, ts, tc), lambda b, i, j: (b, i, off + j))
    prev = lambda off: pl.BlockSpec((1, HALO, tc), _prev_map(ts, lambda j: off + j))
    wsp = lambda off: pl.BlockSpec((K, tc), lambda b, i, j: (0, off + j))
    bsp = lambda off: pl.BlockSpec((1, tc), lambda b, i, j: (0, off + j))
    return _pc(body, name=name, grid=(Bl, S // ts, nj),
               in_specs=[prev(0), main(0), prev(nj), main(nj), wsp(0), wsp(nj), bsp(0), bsp(nj)],
               out_specs=pl.BlockSpec((1, ts, tc), lambda b, i, j: (b, i, j)),
               out_shape=SDS((Bl, S, DFF), BF16),
               scratch=[pltpu.VMEM((HALO + ts, tc), F32), pltpu.VMEM((HALO + ts, tc), F32)],
               )(pre3, pre3, pre3, pre3, w, w, b, b)


def ffn_act_bwd(pre3, w, b, dact, *, comm=None, name):
    Bl, S, _ = pre3.shape
    K = FFN_K
    ts, tc = _tile(S, 512, 8), 256
    nj = DFF // tc
    ns = S // ts

    def body(gp_ref, g_ref, gn_ref, vp_ref, v_ref, vn_ref, d_ref, dn_ref, wg_ref, wv_ref, bg_ref, bv_ref,
             dg_ref, dv_ref, dwg_ref, dwv_ref, dbg_ref, dbv_ref, eg, ev, e2g, e2v):
        bb, i = pl.program_id(1), pl.program_id(2)
        last = i == ns - 1
        ups, taps = [], []
        for p_ref, m_ref, n_ref, w_ref, b_ref, ext in ((gp_ref, g_ref, gn_ref, wg_ref, bg_ref, eg),
                                                       (vp_ref, v_ref, vn_ref, wv_ref, bv_ref, ev)):
            ext[0:HALO, :] = jnp.where(i > 0, p_ref[0], 0.0)
            ext[HALO:HALO + ts, :] = m_ref[0]
            ext[HALO + ts:2 * HALO + ts, :] = jnp.where(last, 0.0, n_ref[0])
            tp = [ext[pl.ds(HALO - (K - 1) + k, ts + HALO), :] for k in range(K)]
            acc = b_ref[...] + w_ref[0:1, :] * tp[0]
            for k in range(1, K):
                acc = acc + w_ref[k:k + 1, :] * tp[k]
            ups.append(acc)
            taps.append(tp)
        ug, uv = ups
        dg_e = uv * _dsilu(ug)
        dv_e = _silu(ug)
        d_main = d_ref[0]
        d_next = jnp.where(last, 0.0, dn_ref[0])
        dug = d_main * dg_e[0:ts]
        duv = d_main * dv_e[0:ts]
        e2g[0:ts, :] = dug
        e2g[ts:ts + HALO, :] = d_next * dg_e[ts:ts + HALO]
        e2v[0:ts, :] = duv
        e2v[ts:ts + HALO, :] = d_next * dv_e[ts:ts + HALO]

        @pl.when((bb == 0) & (i == 0))
        def _():
            for r in (dwg_ref, dwv_ref, dbg_ref, dbv_ref):
                r[...] = jnp.zeros_like(r)

        for w_ref, e2, tp, du, o_ref, dw_ref, db_ref in ((wg_ref, e2g, taps[0], dug, dg_ref, dwg_ref, dbg_ref),
                                                         (wv_ref, e2v, taps[1], duv, dv_ref, dwv_ref, dbv_ref)):
            dx = w_ref[0:1, :] * e2[pl.ds(K - 1, ts), :]
            for k in range(1, K):
                dx = dx + w_ref[k:k + 1, :] * e2[pl.ds(K - 1 - k, ts), :]
            o_ref[0] = dx.astype(BF16)
            for k in range(K):
                dw_ref[k:k + 1, :] += _csum(du * tp[k][0:ts])
            db_ref[...] += _csum(du)

    nhb = S // HALO
    main = lambda off: pl.BlockSpec((1, ts, tc), lambda j, b, i: (b, i, off + j))
    prev = lambda off: pl.BlockSpec((1, HALO, tc), lambda j, b, i: _prev_map(ts, lambda jj: off + jj)(b, i, j))
    nxt = lambda off: pl.BlockSpec((1, HALO, tc), lambda j, b, i: _next_map(ts, nhb, lambda jj: off + jj)(b, i, j))
    wsp = lambda off: pl.BlockSpec((K, tc), lambda j, b, i: (0, off + j))
    bsp = lambda off: pl.BlockSpec((1, tc), lambda j, b, i: (0, off + j))
    outs = _pc(body, name=name, grid=(nj, Bl, ns),
               in_specs=[prev(0), main(0), nxt(0), prev(nj), main(nj), nxt(nj), main(0), nxt(0),
                         wsp(0), wsp(nj), bsp(0), bsp(nj)],
               out_specs=[main(0), main(0), wsp(0), wsp(0), bsp(0), bsp(0)],
               out_shape=[SDS((Bl, S, DFF), BF16), SDS((Bl, S, DFF), BF16), SDS((K, DFF), F32), SDS((K, DFF), F32),
                          SDS((1, DFF), F32), SDS((1, DFF), F32)],
               scratch=[pltpu.VMEM((2 * HALO + ts, tc), F32), pltpu.VMEM((2 * HALO + ts, tc), F32),
                        pltpu.VMEM((HALO + ts, tc), F32), pltpu.VMEM((HALO + ts, tc), F32)],
               comm=comm)(pre3, pre3, pre3, pre3, pre3, pre3, dact, dact, w, w, b, b)
    return outs


PHALO = 16


def _pool_window_sums(ext, base, ts, step):
    s = ext[pl.ds(base, ts), :]
    out = []
    for i in range(1, PHALO):
        s = s + ext[pl.ds(base + step * i, ts), :]
        if i + 1 in POOL_WIN:
            out.append(s)
    return out


def _pick(g, vals):
    r = vals[-1]
    for k in range(len(vals) - 2, -1, -1):
        r = jnp.where(g == k, vals[k], r)
    return r


def _pool_count(g, i, ts, rows):
    t = (i * ts + lax.broadcasted_iota(jnp.int32, (rows, 1), 0) + 1).astype(F32)
    return jnp.minimum(t, _pick(g, [float(w) for w in POOL_WIN]))


def _pooled(up_ref, u_ref, ext, g, i, ts):
    ext[0:PHALO, :] = jnp.where(i > 0, up_ref[0], 0.0)
    u = u_ref[0]
    ext[PHALO:PHALO + ts, :] = u
    sums = _pool_window_sums(ext, PHALO, ts, -1)
    return _pick(g, sums) / _pool_count(g, i, ts, ts) - u


def pool_fwd(proj3, pool_w, scale, *, name):
    Bl, S, _ = proj3.shape
    ts = _tile(S, 512, 16)
    c0 = U0 // POOL_D

    def body(up_ref, u_ref, w_ref, s_ref, o_ref, ext):
        i, g = pl.program_id(1), pl.program_id(2)
        pooled = _pooled(up_ref, u_ref, ext, g, i, ts)
        o_ref[0] = (_nn(pooled, w_ref[0]) * s_ref[...]).astype(BF16)

    return _pc(body, name=name, grid=(Bl, S // ts, POOL_G),
               in_specs=[pl.BlockSpec((1, PHALO, POOL_D), lambda b, i, g: (b, jnp.maximum(i * (ts // PHALO) - 1, 0), c0 + g)),
                         pl.BlockSpec((1, ts, POOL_D), lambda b, i, g: (b, i, c0 + g)),
                         pl.BlockSpec((1, POOL_D, POOL_D), lambda b, i, g: (g, 0, 0)),
                         pl.BlockSpec((1, POOL_D), lambda b, i, g: (0, g))],
               out_specs=pl.BlockSpec((1, ts, POOL_D), lambda b, i, g: (b, i, g)),
               out_shape=SDS((Bl, S, POOL_W), BF16), scratch=[pltpu.VMEM((PHALO + ts, POOL_D), F32)],
               )(proj3, proj3, pool_w, scale)


def pool_bwd(proj3, pool_w, scale, dycat3, *, name):
    Bl, S, _ = proj3.shape
    ts = _tile(S, 512, 16)
    ns = S // ts
    c0 = U0 // POOL_D
    d0 = SSD_W // POOL_D
    nhb = S // PHALO

    def body(up_ref, u_ref, d_ref, dn_ref, w_ref, s_ref, du_ref, dw_ref, ds_ref, ext, ext2):
        g, bb, i = pl.program_id(0), pl.program_id(1), pl.program_id(2)
        last = i == ns - 1
        pooled = _pooled(up_ref, u_ref, ext, g, i, ts)
        wm = w_ref[0]
        sc = s_ref[...]
        dy = d_ref[0]
        dp_main = dy * sc
        dpool = _nt(dp_main, wm)
        dpool_n = _nt(jnp.where(last, 0.0, dn_ref[0]) * sc, wm)
        ext2[0:ts, :] = dpool / _pool_count(g, i, ts, ts)
        ext2[ts:ts + PHALO, :] = dpool_n / _pool_count(g, i + 1, ts, PHALO)
        sums = _pool_window_sums(ext2, 0, ts, 1)
        du_ref[0] = (_pick(g, sums) - dpool).astype(BF16)

        @pl.when((bb == 0) & (i == 0))
        def _():
            dw_ref[...] = jnp.zeros_like(dw_ref)
            ds_ref[...] = jnp.zeros_like(ds_ref)

        dw_ref[0] += _tn(pooled, dp_main)
        ds_ref[...] += _csum(dy * _nn(pooled, wm))

    return _pc(body, name=name, grid=(POOL_G, Bl, ns),
               in_specs=[pl.BlockSpec((1, PHALO, POOL_D), lambda g, b, i: (b, jnp.maximum(i * (ts // PHALO) - 1, 0), c0 + g)),
                         pl.BlockSpec((1, ts, POOL_D), lambda g, b, i: (b, i, c0 + g)),
                         pl.BlockSpec((1, ts, POOL_D), lambda g, b, i: (b, i, d0 + g)),
                         pl.BlockSpec((1, PHALO, POOL_D), lambda g, b, i: (b, jnp.minimum((i + 1) * (ts // PHALO), nhb - 1), d0 + g)),
                         pl.BlockSpec((1, POOL_D, POOL_D), lambda g, b, i: (g, 0, 0)),
                         pl.BlockSpec((1, POOL_D), lambda g, b, i: (0, g))],
               out_specs=[pl.BlockSpec((1, ts, POOL_D), lambda g, b, i: (b, i, g)),
                          pl.BlockSpec((1, POOL_D, POOL_D), lambda g, b, i: (g, 0, 0)),
                          pl.BlockSpec((1, POOL_D), lambda g, b, i: (0, g))],
               out_shape=[SDS((Bl, S, POOL_W), BF16), SDS((POOL_G, POOL_D, POOL_D), F32), SDS((1, POOL_W), F32)],
               scratch=[pltpu.VMEM((PHALO + ts, POOL_D), F32), pltpu.VMEM((PHALO + ts, POOL_D), F32)],
               )(proj3, proj3, dycat3, dycat3, pool_w, scale)


NPAIR = SSD_HEADS // 2


def _ssd_common(sm, bias, alog):
    L = SSD_L
    dt = jax.nn.softplus(sm + bias)
    a = -jnp.exp(alog)
    da = dt * a
    r = lax.broadcasted_iota(jnp.int32, (L, L), 0)
    c = lax.broadcasted_iota(jnp.int32, (L, L), 1)
    tri = (r >= c).astype(F32)
    cum = _dg(tri, da, 1, 0, lax.Precision.HIGHEST)
    return dt, a, cum, cum.T, r >= c


def _lanes(lo, hi, shape=(1, LANES)):
    lane = lax.broadcasted_iota(jnp.int32, shape, len(shape) - 1)
    return (lane >= lo) & (lane < hi)


def _onehot_lane(h):
    return (lax.broadcasted_iota(jnp.int32, (1, LANES), 1) == h).astype(F32)


def _split_nn(a, e):
    hi = a.astype(BF16)
    lo = (a - hi.astype(F32)).astype(BF16)
    return _dg(hi, e, 1, 0) + _dg(lo, e, 1, 0)


def _head_spread():
    r = lax.broadcasted_iota(jnp.int32, (LANES, SSD_W), 0)
    c = lax.broadcasted_iota(jnp.int32, (LANES, SSD_W), 1)
    return (c // SSD_P == r).astype(BF16)


def _pair_gather(j):
    r = lax.broadcasted_iota(jnp.int32, (LANES, LANES), 0)
    c = lax.broadcasted_iota(jnp.int32, (LANES, LANES), 1)
    return (c == 2 * j + (r >= SSD_P).astype(jnp.int32)).astype(BF16)


def ssd_fwd(xbc3, proj3, bias, alog, dskip, *, name):
    Bl, S, _ = xbc3.shape
    L = SSD_L
    nc = S // L

    def body(xbc_ref, sm_ref, bias_ref, alog_ref, d_ref, y_ref, hin_ref, H):
        c = pl.program_id(1)

        @pl.when(c == 0)
        def _():
            H[...] = jnp.zeros_like(H)

        dt, a, cum, cumT, mask = _ssd_common(sm_ref[0], bias_ref[...], alog_ref[...])
        lo = _lanes(0, SSD_P)
        rowlo = lax.broadcasted_iota(jnp.int32, (LANES, LANES), 0) < SSD_P
        spread = _head_spread()
        dt_x = _split_nn(dt, spread)
        el_x = _split_nn(jnp.exp(cum), spread)
        wl_x = _split_nn(jnp.exp(cum[L - 1:L, :] - cum), spread)
        cb = []
        for g in range(SSD_G):
            Bg = xbc_ref[0, :, SSD_W + g * SSD_N:SSD_W + (g + 1) * SSD_N]
            Cg = xbc_ref[0, :, SSD_W + SSD_G * SSD_N + g * SSD_N:SSD_W + SSD_G * SSD_N + (g + 1) * SSD_N]
            cb.append((Bg, Cg, _nt(Cg, Bg)))
        for j in range(NPAIR):
            h0, h1 = 2 * j, 2 * j + 1
            sl = slice(j * LANES, (j + 1) * LANES)
            Bg, Cg, CB = cb[j // (NPAIR // SSD_G)]
            X = xbc_ref[0, :, sl]
            c0, c1 = cum[:, h0:h0 + 1], cum[:, h1:h1 + 1]
            r0, r1 = cumT[h0:h0 + 1, :], cumT[h1:h1 + 1, :]
            cl0, cl1 = cum[L - 1:L, h0:h0 + 1], cum[L - 1:L, h1:h1 + 1]
            Xt = X * dt_x[:, sl]
            M0 = CB * jnp.exp(jnp.where(mask, c0 - r0, NEG))
            M1 = CB * jnp.exp(jnp.where(mask, c1 - r1, NEG))
            Yd = jnp.where(lo, _nn(M0, Xt), _nn(M1, Xt))
            Hp = H[j]
            hin_ref[0, 0, j] = Hp
            Z = _nt(Cg, Hp)
            y_ref[0, :, sl] = Yd + el_x[:, sl] * Z + X * d_ref[j:j + 1, :]
            H[j] = jnp.where(rowlo, jnp.exp(cl0), jnp.exp(cl1)) * Hp + _tn(wl_x[:, sl] * Xt, Bg)

    vec = pl.BlockSpec((1, LANES), lambda b, c: (0, 0))
    return _pc(body, name=name, grid=(Bl, nc),
               in_specs=[pl.BlockSpec((1, L, SSD_XBC), lambda b, c: (b, c, 0)),
                         pl.BlockSpec((1, L, LANES), lambda b, c: (b, c, DT0 // LANES)),
                         vec, vec, pl.BlockSpec((NPAIR, LANES), lambda b, c: (0, 0))],
               out_specs=[pl.BlockSpec((1, L, SSD_W), lambda b, c: (b, c, 0)),
                          pl.BlockSpec((1, 1, NPAIR, LANES, LANES), lambda b, c: (b, c, 0, 0, 0))],
               out_shape=[SDS((Bl, S, SSD_W), F32), SDS((Bl, nc, NPAIR, LANES, LANES), F32)],
               scratch=[pltpu.VMEM((NPAIR, LANES, LANES), F32)])(xbc3, proj3, bias, alog, dskip)


def ssd_bwd(xbc3, proj3, hin, dy3, bias, alog, dskip, *, comm=None, name):
    Bl, S, _ = xbc3.shape
    L = SSD_L
    nc = S // L

    def body(xbc_ref, sm_ref, hin_ref, dy_ref, bias_ref, alog_ref, d_ref, dx_ref, ddt_ref, dpar_ref, dd_ref, dH, ddacc):
        bb, i = pl.program_id(0), pl.program_id(1)

        @pl.when(i == 0)
        def _():
            dH[...] = jnp.zeros_like(dH)

        @pl.when((bb == 0) & (i == 0))
        def _():
            dpar_ref[...] = jnp.zeros_like(dpar_ref)
            ddacc[...] = jnp.zeros_like(ddacc)

        sm = sm_ref[0]
        dt, a, cum, cumT, mask = _ssd_common(sm, bias_ref[...], alog_ref[...])
        maskf = mask.astype(F32)
        lo = _lanes(0, SSD_P)
        rowlo = lax.broadcasted_iota(jnp.int32, (LANES, LANES), 0) < SSD_P
        lastrow = (lax.broadcasted_iota(jnp.int32, (L, 1), 0) == L - 1).astype(F32)
        dcum = jnp.zeros((L, LANES), F32)
        dcum_t = jnp.zeros((LANES, L), F32)
        ddt = jnp.zeros((L, LANES), F32)
        spread = _head_spread()
        ones = jnp.ones((L, LANES), BF16)
        ecum = jnp.exp(cum)
        wall = jnp.exp(cum[L - 1:L, :] - cum)
        dt_x = _split_nn(dt, spread)
        el_x = _split_nn(ecum, spread)
        wl_x = _split_nn(wall, spread)
        headrow = lax.broadcasted_iota(jnp.int32, (LANES, 1), 0)
        grp = []
        for g in range(SSD_G):
            Bg = xbc_ref[0, :, SSD_W + g * SSD_N:SSD_W + (g + 1) * SSD_N]
            Cg = xbc_ref[0, :, SSD_W + SSD_G * SSD_N + g * SSD_N:SSD_W + SSD_G * SSD_N + (g + 1) * SSD_N]
            grp.append(dict(B=Bg, C=Cg, CB=_nt(Cg, Bg), dB=jnp.zeros((L, SSD_N), F32), dC=jnp.zeros((L, SSD_N), F32),
                            dCB=jnp.zeros((L, L), F32)))
        for j in range(NPAIR):
            h0, h1 = 2 * j, 2 * j + 1
            sl = slice(j * LANES, (j + 1) * LANES)
            G = grp[j // (NPAIR // SSD_G)]
            Bg, Cg, CB = G["B"], G["C"], G["CB"]
            X = xbc_ref[0, :, sl]
            dY = dy_ref[0, :, sl]
            c0, c1 = cum[:, h0:h0 + 1], cum[:, h1:h1 + 1]
            r0, r1 = cumT[h0:h0 + 1, :], cumT[h1:h1 + 1, :]
            cl0, cl1 = cum[L - 1:L, h0:h0 + 1], cum[L - 1:L, h1:h1 + 1]
            oh0, oh1 = _onehot_lane(h0), _onehot_lane(h1)
            gather = _pair_gather(j)
            dtl, el, wl = dt_x[:, sl], el_x[:, sl], wl_x[:, sl]
            Xt = X * dtl
            Hp = hin_ref[0, 0, j]
            dS = dH[j]
            dX = dY * d_ref[j:j + 1, :]
            ddacc[j:j + 1, :] += _csum(dY * X)
            Z = _nt(Cg, Hp)
            dZ = dY * el
            dcum = dcum + _split_nn(dY * Z, gather) * ecum
            G["dC"] = G["dC"] + _nn(dZ, Hp)
            dHy = _tn(dZ, Cg)
            Gm = _nt(Bg, dS)
            dXt = wl * Gm
            q = _split_nn(Xt * Gm, gather) * wall
            dcum = dcum + lastrow * _csum(q) - q
            G["dB"] = G["dB"] + _nn(wl * Xt, dS)
            g0, g1 = jnp.exp(cl0), jnp.exp(cl1)
            rowsum = _nn(dS * Hp, ones)
            dg0 = _csum(jnp.where(rowlo, rowsum, 0.0))
            dg1 = _csum(jnp.where(rowlo, 0.0, rowsum))
            dcum = dcum + lastrow * ((dg0 * g0) * oh0 + (dg1 * g1) * oh1)
            dH[j] = jnp.where(rowlo, g0, g1) * dS + dHy
            for h, ch, rh, mh, oh in ((h0, c0, r0, lo, oh0), (h1, c1, r1, jnp.logical_not(lo), oh1)):
                decay = jnp.exp(jnp.where(mask, ch - rh, NEG))
                Mh = CB * decay
                dM = _nt(jnp.where(mh, dY, 0.0), Xt) * maskf
                dXt = dXt + jnp.where(mh, _tn(Mh, dY), 0.0)
                G["dCB"] = G["dCB"] + dM * decay
                Q = dM * Mh
                dcum = dcum + _rsum(Q) * oh
                dcum_t = dcum_t + (headrow == h).astype(F32) * _csum(Q)
            dX = dX + dXt * dtl
            ddt = ddt + _split_nn(dXt * X, gather)
            dx_ref[0, :, sl] = dX
        dcum = dcum - dcum_t.T
        for g in range(SSD_G):
            G = grp[g]
            dC = G["dC"] + _nn(G["dCB"], G["B"])
            dB = G["dB"] + _tn(G["dCB"], G["C"])
            dx_ref[0, :, SSD_W + g * SSD_N:SSD_W + (g + 1) * SSD_N] = dB
            dx_ref[0, :, SSD_W + SSD_G * SSD_N + g * SSD_N:SSD_W + SSD_G * SSD_N + (g + 1) * SSD_N] = dC
        r = lax.broadcasted_iota(jnp.int32, (L, L), 0)
        c = lax.broadcasted_iota(jnp.int32, (L, L), 1)
        dda = _dg((c >= r).astype(F32), dcum, 1, 0, lax.Precision.HIGHEST)
        heads = _lanes(0, SSD_HEADS)
        ddt = ddt + dda * a
        draw = jnp.where(heads, ddt * _sig(sm + bias_ref[...]), 0.0)
        ddt_ref[0] = draw.astype(BF16)
        dpar_ref[0:1, :] += _csum(draw)
        dpar_ref[1:2, :] += _csum(jnp.where(heads, dda * dt * a, 0.0))

        @pl.when((bb == Bl - 1) & (i == nc - 1))
        def _():
            acc = ddacc[...]
            lane = lax.broadcasted_iota(jnp.int32, (NPAIR, LANES), 1)
            s0 = _rsum(jnp.where(lane < SSD_P, acc, 0.0))
            s1 = _rsum(jnp.where(lane < SSD_P, 0.0, acc))
            dd_ref[...] = jnp.where(lane == 0, s0, jnp.where(lane == 1, s1, 0.0))

    vec = pl.BlockSpec((1, LANES), lambda b, i: (0, 0))
    par = pl.BlockSpec((NPAIR, LANES), lambda b, i: (0, 0))
    return _pc(body, name=name, grid=(Bl, nc),
               in_specs=[pl.BlockSpec((1, L, SSD_XBC), lambda b, i: (b, nc - 1 - i, 0)),
                         pl.BlockSpec((1, L, LANES), lambda b, i: (b, nc - 1 - i, DT0 // LANES)),
                         pl.BlockSpec((1, 1, NPAIR, LANES, LANES), lambda b, i: (b, nc - 1 - i, 0, 0, 0)),
                         pl.BlockSpec((1, L, SSD_W), lambda b, i: (b, nc - 1 - i, 0)),
                         vec, vec, par],
               out_specs=[pl.BlockSpec((1, L, SSD_XBC), lambda b, i: (b, nc - 1 - i, 0)),
                          pl.BlockSpec((1, L, LANES), lambda b, i: (b, nc - 1 - i, 0)),
                          par, par],
               out_shape=[SDS((Bl, S, SSD_XBC), F32), SDS((Bl, S, LANES), BF16), SDS((NPAIR, LANES), F32),
                          SDS((NPAIR, LANES), F32)],
               scratch=[pltpu.VMEM((NPAIR, LANES, LANES), F32), pltpu.VMEM((NPAIR, LANES), F32)],
               comm=comm)(xbc3, proj3, hin, dy3, bias, alog, dskip)


PE_LO, PE_MID, PE_HI = MLA_NOPE, MLA_NOPE + MLA_ROPE // 2, MLA_NOPE + MLA_ROPE
ATT_SCALE = 1.0 / math.sqrt(MLA_QK)


def _swap_matrix():
    src = lax.broadcasted_iota(jnp.int32, (LANES, LANES), 0)
    dst = lax.broadcasted_iota(jnp.int32, (LANES, LANES), 1)
    half = MLA_ROPE // 2
    first = (dst >= PE_LO) & (dst < PE_MID) & (src == dst + half)
    second = (dst >= PE_MID) & (dst < PE_HI) & (src == dst - half)
    return (second.astype(F32) - first.astype(F32)).astype(BF16)


def rope_tables(pos, invf, *, name):
    T = pos.shape[0]
    tm = _tile(T, 512, 8)

    def body(pos_ref, f_ref, c_ref, s_ref):
        ang = pos_ref[...] * f_ref[...]
        pe = _lanes(PE_LO, PE_HI)
        c_ref[...] = jnp.where(pe, jnp.cos(ang), 1.0)
        s_ref[...] = jnp.where(pe, jnp.sin(ang), 0.0)

    tile = pl.BlockSpec((tm, LANES), lambda i: (i, 0))
    return _pc(body, name=name, grid=(T // tm,),
               in_specs=[pl.BlockSpec((tm, 1), lambda i: (i, 0)), pl.BlockSpec((1, LANES), lambda i: (0, 0))],
               out_specs=[tile, tile], out_shape=[SDS((T, LANES), F32)] * 2)(pos, invf)


V_ONE = MLA_V


def mla_prep_fwd(qt, kvt, proj, cs, sn, *, name):
    T = qt.shape[0]
    tm = _tile(T, 256, 8)
    HW = MLA_H * LANES

    def body(q_ref, k_ref, v_ref, kpe_ref, c_ref, s_ref, qo_ref, ko_ref, vo_ref):
        c, s = c_ref[...], s_ref[...]
        kpe = kpe_ref[...]
        sw = _swap_matrix()
        one = _lanes(V_ONE, V_ONE + 1)
        for h in range(MLA_H):
            sl = slice(h * LANES, (h + 1) * LANES)
            q = q_ref[:, sl]
            k = k_ref[:, sl] + kpe
            qo_ref[:, sl] = ((q * c + _split_nn(q, sw) * s) * ATT_SCALE).astype(BF16)
            ko_ref[:, sl] = (k * c + _split_nn(k, sw) * s).astype(BF16)
            vo_ref[:, sl] = jnp.where(one, 1.0, v_ref[:, sl]).astype(BF16)

    row = pl.BlockSpec((tm, HW), lambda i: (i, 0))
    tab = pl.BlockSpec((tm, LANES), lambda i: (i, 0))
    return _pc(body, name=name, grid=(T // tm,),
               in_specs=[row, row, pl.BlockSpec((tm, HW), lambda i: (i, 1)),
                         pl.BlockSpec((tm, LANES), lambda i: (i, KPE0 // LANES)), tab, tab],
               out_specs=[row, row, row], out_shape=[SDS((T, HW), BF16)] * 3)(qt, kvt, kvt, proj, cs, sn)


def mla_prep_bwd(dqr, dkr, cs, sn, *, name):
    T = dqr.shape[0]
    tm = _tile(T, 256, 8)
    HW = MLA_H * LANES

    def body(dq_ref, dk_ref, c_ref, s_ref, qo_ref, ko_ref, kpe_ref):
        c, s = c_ref[...], s_ref[...]
        sw = _swap_matrix()
        pe = _lanes(PE_LO, PE_HI)
        dkpe = jnp.zeros((tm, LANES), F32)
        for h in range(MLA_H):
            sl = slice(h * LANES, (h + 1) * LANES)
            dq = dq_ref[:, sl] * ATT_SCALE
            dk = dk_ref[:, sl]
            qo_ref[:, sl] = (dq * c - _split_nn(dq * s, sw)).astype(BF16)
            dkk = dk * c - _split_nn(dk * s, sw)
            ko_ref[:, sl] = jnp.where(pe, 0.0, dkk).astype(BF16)
            dkpe = dkpe + jnp.where(pe, dkk, 0.0)
        kpe_ref[...] = dkpe.astype(BF16)

    row = pl.BlockSpec((tm, HW), lambda i: (i, 0))
    tab = pl.BlockSpec((tm, LANES), lambda i: (i, 0))
    return _pc(body, name=name, grid=(T // tm,), in_specs=[row, row, tab, tab], out_specs=[row, row, tab],
               out_shape=[SDS((T, HW), BF16), SDS((T, HW), BF16), SDS((T, LANES), BF16)])(dqr, dkr, cs, sn)


def _att_tile(S):
    return _tile(S, 512, LANES)


def _rep(x, n):
    return x if n == 1 else jnp.concatenate([x] * n, axis=1)


def _diag_mask(t, transposed=False):
    r = lax.broadcasted_iota(jnp.int32, (t, t), 0)
    c = lax.broadcasted_iota(jnp.int32, (t, t), 1)
    return (c >= r) if transposed else (c <= r)


def flash_fwd(qr, kr, vr, Bl, *, comm=None, name):
    T = qr.shape[0]
    S = T // Bl
    t = _att_tile(S)
    n = S // t
    nl = t // LANES

    def body(q_ref, k_ref, v_ref, o_ref, lse_ref, lset_ref, m, acc):
        qi = pl.program_id(2)
        q = q_ref[...]
        m[...] = jnp.full_like(m, NEG)
        acc[...] = jnp.zeros_like(acc)

        def block(kj, masked):
            off = pl.multiple_of(kj * t, t)
            s = _nt(q, k_ref[pl.ds(off, t), :])
            if masked:
                s = jnp.where(_diag_mask(t), s, NEG)
            mo = m[...]
            mn = jnp.maximum(mo, jnp.max(s, axis=1, keepdims=True))
            p = jnp.exp((s - _rep(mn, nl)).astype(BF16))
            acc[...] = jnp.exp(mo - mn) * acc[...] + _nn(p, v_ref[pl.ds(off, t), :])
            m[...] = mn

        def loop(kj, c):
            block(kj, False)
            return c

        lax.fori_loop(0, qi, loop, 0)
        block(qi, True)
        a = acc[...]
        l = a[:, V_ONE:V_ONE + 1]
        o_ref[...] = jnp.where(_lanes(0, MLA_V), a / l, 0.0).astype(BF16)
        lse = m[...] + jnp.log(l)
        lse_ref[...] = lse
        lset_ref[...] = lse.T[0:8, :]

    qs = pl.BlockSpec((t, LANES), lambda b, h, qi: (b * n + qi, h))
    seq = pl.BlockSpec((S, LANES), lambda b, h, qi: (b, h))
    return _pc(body, name=name, grid=(Bl, MLA_H, n), in_specs=[qs, seq, seq],
               out_specs=[qs, qs, pl.BlockSpec((8, t), lambda b, h, qi: (b * MLA_H + h, qi))],
               out_shape=[SDS((T, MLA_H * LANES), BF16), SDS((T, MLA_H * LANES), F32), SDS((Bl * MLA_H * 8, S), F32)],
               scratch=[pltpu.VMEM((t, LANES), F32), pltpu.VMEM((t, LANES), F32)], comm=comm)(qr, kr, vr)


def flash_bwd_dq(qr, kr, vr, o, lse, dycat, Bl, *, comm=None, name):
    T = qr.shape[0]
    S = T // Bl
    t = _att_tile(S)
    n = S // t
    nl = t // LANES
    do0 = (SSD_W + POOL_W) // LANES

    def body(q_ref, k_ref, v_ref, o_ref, lse_ref, do_ref, dq_ref, dlt_ref, acc, dl):
        qi = pl.program_id(2)
        q = q_ref[...]
        do = do_ref[...]
        dob = do.astype(BF16)
        dl[...] = jnp.broadcast_to(_rsum(do * o_ref[...].astype(F32)), (t, LANES))
        acc[...] = jnp.zeros_like(acc)

        def block(kj, masked):
            off = pl.multiple_of(kj * t, t)
            k = k_ref[pl.ds(off, t), :]
            s = _nt(q, k)
            if masked:
                s = jnp.where(_diag_mask(t), s, NEG)
            p = jnp.exp((s - _rep(lse_ref[...], nl)).astype(BF16))
            dp = _nt(dob, v_ref[pl.ds(off, t), :])
            acc[...] += _nn(p * (dp - _rep(dl[...], nl)), k)

        def loop(kj, c):
            block(kj, False)
            return c

        lax.fori_loop(0, qi, loop, 0)
        block(qi, True)
        dq_ref[...] = acc[...]
        dlt_ref[...] = dl[...].T[0:8, :]

    qs = pl.BlockSpec((t, LANES), lambda b, h, qi: (b * n + qi, h))
    seq = pl.BlockSpec((S, LANES), lambda b, h, qi: (b, h))
    return _pc(body, name=name, grid=(Bl, MLA_H, n),
               in_specs=[qs, seq, seq, qs, qs, pl.BlockSpec((t, LANES), lambda b, h, qi: (b * n + qi, do0 + h))],
               out_specs=[qs, pl.BlockSpec((8, t), lambda b, h, qi: (b * MLA_H + h, qi))],
               out_shape=[SDS((T, MLA_H * LANES), F32), SDS((Bl * MLA_H * 8, S), F32)],
               scratch=[pltpu.VMEM((t, LANES), F32), pltpu.VMEM((t, LANES), F32)], comm=comm)(qr, kr, vr, o, lse, dycat)


def flash_bwd_dkv(qr, kr, vr, lset, dlt, dycat, Bl, *, comm=None, name):
    T = qr.shape[0]
    S = T // Bl
    t = _att_tile(S)
    n = S // t
    do0 = (SSD_W + POOL_W) // LANES

    def body(q_ref, k_ref, v_ref, lset_ref, dlt_ref, do_ref, dk_ref, dv_ref, dka, dva):
        kj = pl.program_id(2)
        k = k_ref[...]
        v = v_ref[...]
        dka[...] = jnp.zeros_like(dka)
        dva[...] = jnp.zeros_like(dva)

        def block(qi, masked):
            off = pl.multiple_of(qi * t, t)
            q = q_ref[pl.ds(off, t), :]
            do = do_ref[pl.ds(off, t), :].astype(BF16)
            st = _nt(k, q)
            if masked:
                st = jnp.where(_diag_mask(t, True), st, NEG)
            pt = jnp.exp((st - lset_ref[0:1, pl.ds(off, t)]).astype(BF16))
            dst = pt * (_nt(v, do) - dlt_ref[0:1, pl.ds(off, t)])
            dva[...] += _nn(pt, do)
            dka[...] += _nn(dst, q)

        def loop(qi, c):
            block(qi, False)
            return c

        block(kj, True)
        lax.fori_loop(kj + 1, n, loop, 0)
        dk_ref[...] = dka[...]
        dv_ref[...] = dva[...].astype(BF16)

    ks = pl.BlockSpec((t, LANES), lambda b, h, kj: (b * n + kj, h))
    seq = pl.BlockSpec((S, LANES), lambda b, h, kj: (b, h))
    rows = pl.BlockSpec((8, S), lambda b, h, kj: (b * MLA_H + h, 0))
    return _pc(body, name=name, grid=(Bl, MLA_H, n),
               in_specs=[seq, ks, ks, rows, rows, pl.BlockSpec((S, LANES), lambda b, h, kj: (b, do0 + h))],
               out_specs=[ks, ks], out_shape=[SDS((T, MLA_H * LANES), F32), SDS((T, MLA_H * LANES), BF16)],
               scratch=[pltpu.VMEM((t, LANES), F32), pltpu.VMEM((t, LANES), F32)], comm=comm)(qr, kr, vr, lset, dlt, dycat)


def _rows2d(a):
    return a.reshape(-1, a.shape[-1])


def add2(a, b, *, name):
    shp = a.shape
    a2, b2 = _rows2d(a), _rows2d(b)
    R, C = a2.shape
    tm = _tile(R, 512, 8)

    def body(a_ref, b_ref, o_ref, ob_ref):
        s = a_ref[...] + b_ref[...]
        o_ref[...] = s
        ob_ref[...] = s.astype(BF16)

    blk = pl.BlockSpec((tm, C), lambda i: (i, 0))
    o, ob = _pc(body, name=name, grid=(R // tm,), in_specs=[blk, blk], out_specs=[blk, blk],
                out_shape=[SDS((R, C), F32), SDS((R, C), BF16)])(a2, b2)
    return o.reshape(shp), ob.reshape(shp)


def adamw(w, m, v, parts, *, name):
    shp = w.shape
    w2, m2, v2 = _rows2d(w), _rows2d(m), _rows2d(v)
    R, C = w2.shape
    p3 = [p.reshape(p.shape[0], R, C) for p in parts]
    tm = _tile(R, 256, 8)
    bc1 = 1.0 - ADAM_B1 ** ADAM_STEP
    bc2 = 1.0 - ADAM_B2 ** ADAM_STEP

    def body(w_ref, m_ref, v_ref, *refs):
        p_refs, (g_ref, d_ref, nm_ref, nv_ref) = refs[:len(p3)], refs[len(p3):]
        g = None
        for p_ref, p in zip(p_refs, p3):
            for k in range(p.shape[0]):
                term = p_ref[k].astype(F32)
                g = term if g is None else g + term
        mm_ = ADAM_B1 * m_ref[...] + (1.0 - ADAM_B1) * g
        vv = ADAM_B2 * v_ref[...] + (1.0 - ADAM_B2) * (g * g)
        g_ref[...] = g
        nm_ref[...] = mm_
        nv_ref[...] = vv
        d_ref[...] = -ADAM_LR * ((mm_ / bc1) / (jnp.sqrt(vv / bc2) + ADAM_EPS) + ADAM_WD * w_ref[...])

    blk = pl.BlockSpec((tm, C), lambda i: (i, 0))
    pspecs = [pl.BlockSpec((p.shape[0], tm, C), lambda i: (0, i, 0)) for p in p3]
    outs = _pc(body, name=name, grid=(R // tm,), in_specs=[blk, blk, blk] + pspecs,
               out_specs=[blk] * 4, out_shape=[SDS((R, C), F32)] * 4)(w2, m2, v2, *p3)
    return [o.reshape(shp) for o in outs]


def _place():
    return lax.axis_index("x"), lax.axis_index("y"), lax.axis_index("c")


def all_gather_many(xs, *, name):
    n = len(xs)

    def body(*refs):
        x_refs, o_refs = refs[:n], refs[n:2 * n]
        send_sems, recv_sems, local_sems = refs[2 * n:]
        x, y, c = _place()
        me, sibling = (x, y, c), (x, y, 1 - c)
        chips = [(1 - x, y), (x, 1 - y), (1 - x, 1 - y)]

        def rows(a, p):
            return o_refs[a].at[4 * p[0] + 2 * p[1] + p[2]]

        def copy(a, k, block, to, src=None):
            return pltpu.make_async_remote_copy(
                src_ref=rows(a, block) if src is None else src, dst_ref=rows(a, block),
                send_sem=send_sems.at[7 * a + k], recv_sem=recv_sems.at[7 * a + k], device_id=to, device_id_type=MESH)

        mine = [pltpu.make_async_copy(x_refs[a], rows(a, me), local_sems.at[a]) for a in range(n)]
        for cp in mine:
            cp.start()
        first = []
        for a in range(n):
            first.append(copy(a, 0, me, sibling, src=x_refs[a]))
            first += [copy(a, 1 + j, me, (*chip, c), src=x_refs[a]) for j, chip in enumerate(chips)]
        for cp in first:
            cp.start()
        passed = []
        for j, chip in enumerate(chips):
            for a in range(n):
                copy(a, 1 + j, (*chip, c), me).wait_recv()
                cp = copy(a, 4 + j, (*chip, c), sibling)
                cp.start()
                passed.append(cp)
        for a in range(n):
            copy(a, 0, sibling, me).wait_recv()
            for j, chip in enumerate(chips):
                copy(a, 4 + j, (*chip, 1 - c), me).wait_recv()
        for cp in first + passed:
            cp.wait_send()
        for cp in mine:
            cp.wait()

    return pl.pallas_call(
        body, name=name, in_specs=[ANY] * n, out_specs=[ANY] * n,
        out_shape=[SDS((N_DEV,) + a.shape, a.dtype) for a in xs],
        scratch_shapes=[pltpu.SemaphoreType.DMA((7 * n,)), pltpu.SemaphoreType.DMA((7 * n,)), pltpu.SemaphoreType.DMA((n,))],
    )(*xs)


def _stage(ins, out_shape, n_peers, copy_of):
    ins = list(ins)

    def copies(in_refs, out_refs, send_sems, recv_sems):
        place = _place()
        out = []
        for a in range(len(ins)):
            for k in range(n_peers):
                src, dst, peer = copy_of(in_refs[a], out_refs[a], k, place)
                out.append(pltpu.make_async_remote_copy(
                    src_ref=src, dst_ref=dst, send_sem=send_sems.at[n_peers * a + k], recv_sem=recv_sems.at[n_peers * a + k],
                    device_id=peer, device_id_type=MESH))
        return out

    return dict(ins=ins, out_shape=list(out_shape), sems=n_peers * len(ins), copies=copies)


def _other_chips(x, y):
    return [(1 - x, y), (x, 1 - y), (1 - x, 1 - y)]


def stage_gather_direct(blocks):
    def copy_of(src, dst, k, place):
        x, y, c = place
        peer = (x, y, 1 - c) if k == 0 else (*_other_chips(x, y)[k - 1], c)
        return src, dst.at[4 * x + 2 * y + c], peer

    return _stage(blocks, [SDS((N_DEV,) + b.shape, b.dtype) for b in blocks], 4, copy_of)


def stage_gather_forward(bufs):
    def copy_of(src, dst, k, place):
        x, y, c = place
        cx, cy = _other_chips(x, y)[k]
        slot = 4 * cx + 2 * cy + c
        return src.at[slot], dst.at[slot], (x, y, 1 - c)

    st = _stage(bufs, [SDS(b.shape, b.dtype) for b in bufs], 3, copy_of)
    st["alias"] = True
    return st


def stage_rs_sibling(g8s):
    def copy_of(src, dst, k, place):
        x, y, c = place
        return src.at[2 * k + (1 - c)], dst.at[k], (x, y, 1 - c)

    return _stage(g8s, [SDS((4,) + g.shape[1:], g.dtype) for g in g8s], 4, copy_of)


def stage_rs_chips(sums):
    def copy_of(src, dst, k, place):
        x, y, c = place
        chip = _other_chips(x, y)[k]
        return src.at[2 * chip[0] + chip[1]], dst.at[k], (*chip, c)

    return _stage(sums, [SDS((3,) + s.shape[1:], s.dtype) for s in sums], 3, copy_of)


def run_stage(stage, *, name):
    n_in, n_out = len(stage["ins"]), len(stage["out_shape"])

    def body(*refs):
        cps = stage["copies"](refs[:n_in], refs[n_in:n_in + n_out], refs[-2], refs[-1])
        for cp in cps:
            cp.start()
        for cp in cps:
            cp.wait()

    return pl.pallas_call(
        body, name=name, in_specs=[ANY] * n_in, out_specs=[ANY] * n_out, out_shape=stage["out_shape"],
        scratch_shapes=[pltpu.SemaphoreType.DMA((stage["sems"],)), pltpu.SemaphoreType.DMA((stage["sems"],))],
    )(*stage["ins"])


def with_own_block(buf, own):
    x, y, c = _place()
    return lax.dynamic_update_index_in_dim(buf, own, 4 * x + 2 * y + c, 0)


def _owner_major(full, axis):
    shp = full.shape
    r = full.reshape(shp[:axis] + (N_DEV, shp[axis] // N_DEV) + shp[axis + 1:])
    return jnp.moveaxis(r, axis, 0)


def _from_owner_major(g8, axis):
    r = jnp.moveaxis(g8, 0, axis)
    shp = r.shape
    return r.reshape(shp[:axis] + (shp[axis] * shp[axis + 1],) + shp[axis + 2:])


def _perm_w_in(w):
    z = jnp.zeros((w.shape[0], LANES), w.dtype)
    dt = jnp.pad(w[:, 2560:2576], ((0, 0), (0, LANES - SSD_HEADS)))
    kpe = jnp.pad(w[:, 3728:3760], ((0, 0), (PE_LO, LANES - PE_HI)))
    return jnp.concatenate([w[:, 0:1024], w[:, 1024:2560], w[:, 2576:3088], w[:, 3088:3472], z, w[:, 3472:3728], dt, kpe], axis=1)


def _unperm_w_in(g):
    return jnp.concatenate([g[:, Z0:Z0 + 1024], g[:, XBC0:XBC0 + 1536], g[:, DT0:DT0 + SSD_HEADS], g[:, U0:U0 + 512],
                            g[:, CQ0:CQ0 + 384], g[:, CKV0:CKV0 + 256], g[:, KPE0 + PE_LO:KPE0 + PE_HI]], axis=1)


def _perm_w_uq(w):
    return jnp.pad(w.reshape(MLA_QR, MLA_H, MLA_QK), ((0, 0), (0, 0), (0, LANES - MLA_QK))).reshape(MLA_QR, MLA_H * LANES)


def _unperm_w_uq(g):
    return g.reshape(MLA_QR, MLA_H, LANES)[:, :, :MLA_QK].reshape(MLA_QR, MLA_H * MLA_QK)


def _perm_w_ukv(w):
    w3 = w.reshape(MLA_KVR, MLA_H, MLA_NOPE + MLA_V)
    pad = ((0, 0), (0, 0), (0, LANES - MLA_NOPE))
    k = jnp.pad(w3[:, :, :MLA_NOPE], pad).reshape(MLA_KVR, MLA_H * LANES)
    v = jnp.pad(w3[:, :, MLA_NOPE:], pad).reshape(MLA_KVR, MLA_H * LANES)
    return jnp.concatenate([k, v], axis=1)


def _unperm_w_ukv(g):
    k = g[:, :MLA_H * LANES].reshape(MLA_KVR, MLA_H, LANES)[:, :, :MLA_NOPE]
    v = g[:, MLA_H * LANES:].reshape(MLA_KVR, MLA_H, LANES)[:, :, :MLA_V]
    return jnp.concatenate([k, v], axis=2).reshape(MLA_KVR, MLA_H * (MLA_NOPE + MLA_V))


def _perm_w_out(w):
    m = jnp.pad(w[SSD_W + POOL_W:].reshape(MLA_H, MLA_V, D), ((0, 0), (0, LANES - MLA_V), (0, 0))).reshape(MLA_H * LANES, D)
    return jnp.concatenate([w[:SSD_W + POOL_W], m], axis=0)


def _unperm_w_out(g):
    m = g[SSD_W + POOL_W:].reshape(MLA_H, LANES, D)[:, :MLA_V].reshape(MLA_H * MLA_V, D)
    return jnp.concatenate([g[:SSD_W + POOL_W], m], axis=0)


def _lane_pad(v):
    return jnp.pad(v.reshape(1, -1), ((0, 0), (0, LANES - v.shape[-1])))


SMALL = ("attn_norm", "ssd_conv_b", "ssd_dt_bias", "ssd_a_log", "ssd_d", "ssd_norm", "pool_w", "pool_scale",
         "mla_q_norm", "mla_kv_norm", "ffn_norm", "ffn_conv_b", "final_norm")
SHARDED = {"w_in": 2, "ssd_conv_w": 2, "mla_w_uq": 2, "mla_w_ukv": 2, "w_out": 1, "ffn_w_up": 2, "ffn_conv_w": 2,
           "ffn_w_down": 1}
ALL_W = ("attn_norm", "w_in", "ssd_conv_w", "ssd_conv_b", "ssd_dt_bias", "ssd_a_log", "ssd_d", "ssd_norm", "pool_w",
         "pool_scale", "mla_q_norm", "mla_w_uq", "mla_kv_norm", "mla_w_ukv", "w_out", "ffn_norm", "ffn_w_up",
         "ffn_conv_w", "ffn_conv_b", "ffn_w_down", "final_norm")


def _pack_small(d):
    rows, layout = [], []
    for k in SMALL:
        a = d[k].reshape(-1)
        n = a.shape[0]
        r = -(-n // LANES)
        rows.append(jnp.pad(a, (0, r * LANES - n)).reshape(r, LANES))
        layout.append((k, n, r, d[k].shape))
    buf = jnp.concatenate(rows, axis=0)
    pad = (-buf.shape[0]) % 8
    return jnp.pad(buf, ((0, pad), (0, 0))), layout


def _unpack_small(buf, layout):
    out, r0 = {}, 0
    for k, n, r, shp in layout:
        out[k] = buf[r0:r0 + r].reshape(-1)[:n].reshape(shp)
        r0 += r
    return out


def _layer_fwd(l, x, W, cs, sn, Bl, next_blocks=None):
    T = x.shape[0]
    S = T // Bl
    n = f"l{l}_"
    h = rms_fwd(x, W["attn_norm"], name=n + "attn_norm")
    proj = mm(h, W["w_in"], name=n + "w_in")
    proj3 = proj.reshape(Bl, S, PW)
    xbc3 = conv_silu_fwd(proj3, W["ssd_conv_w"], W["ssd_conv_b"], name=n + "ssd_conv")
    y3, hin = ssd_fwd(xbc3, proj3, W["ssd_dt_bias"], W["ssd_a_log"], W["ssd_d"], name=n + "ssd_scan")
    y = y3.reshape(T, SSD_W)
    y_ssd = gated_rms_fwd(y, proj, W["ssd_norm"], name=n + "ssd_gate_norm")
    y_pool = pool_fwd(proj3, W["pool_w"], W["pool_scale"], name=n + "pool").reshape(T, POOL_W)
    qn = rms_fwd(proj, W["mla_q_norm"], col0=CQ0, width=MLA_QR, name=n + "q_norm")
    kvn = rms_fwd(proj, W["mla_kv_norm"], col0=CKV0, width=MLA_KVR, name=n + "kv_norm")
    qt = mm(qn, W["mla_w_uq"], name=n + "w_uq")
    kvt = mm(kvn, W["mla_w_ukv"], name=n + "w_ukv")
    qr, kr, vr = mla_prep_fwd(qt, kvt, proj, cs, sn, name=n + "rope")
    direct = stage_gather_direct(next_blocks) if next_blocks is not None else None
    o, lse, lset = flash_fwd(qr, kr, vr, Bl, comm=direct, name=n + "attn")
    ycat = jnp.concatenate([y_ssd, y_pool, o], axis=1)
    x1 = mm(ycat, W["w_out"], add=x, name=n + "w_out")
    h2 = rms_fwd(x1, W["ffn_norm"], name=n + "ffn_norm")
    forward = stage_gather_forward(direct["result"]) if direct else None
    pre = mm(h2, W["ffn_w_up"], comm=forward, name=n + "w_up")
    gathered = [with_own_block(buf, b) for buf, b in zip(forward["result"], next_blocks)] if direct else None
    pre3 = pre.reshape(Bl, S, 2 * DFF)
    act = ffn_act_fwd(pre3, W["ffn_conv_w"], W["ffn_conv_b"], name=n + "ffn_act").reshape(T, DFF)
    x2 = mm(act, W["ffn_w_down"], add=x1, name=n + "w_down")
    saved = dict(x=x, h=h, proj=proj, xbc3=xbc3, hin=hin, y=y, qn=qn, kvn=kvn, vr=vr, qr=qr, kr=kr, o=o, lse=lse, lset=lset,
                 ycat=ycat, x1=x1, h2=h2, pre3=pre3, act=act)
    return x2, saved, gathered


def _keep_column(g8):
    ci = lax.axis_index("c")
    return lax.dynamic_index_in_dim(g8.reshape((4, 2) + g8.shape[1:]), ci, 1, keepdims=False)


EARLY = ("ffn_w_up", "ffn_conv_w", "ffn_w_down", "w_out")
LATE = tuple(k for k in SHARDED if k not in EARLY)
_UNPERM = {"w_in": _unperm_w_in, "mla_w_uq": _unperm_w_uq, "mla_w_ukv": _unperm_w_ukv, "w_out": _unperm_w_out}


def _by_owner(g, keys):
    return [_owner_major(_UNPERM.get(k, lambda t: t)(g[k]), SHARDED[k] - 1) for k in keys]


def _chip_sums(g8s, from_sibling, tag):
    return [add2(_keep_column(g8), r, name=f"{tag}{a}") for a, (g8, r) in enumerate(zip(g8s, from_sibling))]


def _layer_bwd(l, dx2, dx2b, W, sv, cs, sn, Bl, later_g8=None):
    T = dx2.shape[0]
    S = T // Bl
    n = f"l{l}_b_"
    g = {}
    g["ffn_w_down"] = mm(sv["act"], dx2b, ta=True, name=n + "dw_down")
    dact = mm(dx2b, W["ffn_w_down"], tb=True, name=n + "dact")
    to_sibling = stage_rs_sibling(later_g8) if later_g8 is not None else None
    dpg, dpv, dwg, dwv, dbg, dbv = ffn_act_bwd(sv["pre3"], W["ffn_conv_w"], W["ffn_conv_b"], dact.reshape(Bl, S, DFF),
                                               comm=to_sibling, name=n + "ffn_act")
    to_chips = sums = None
    if to_sibling:
        sums = _chip_sums(later_g8, to_sibling["result"], n + "rs_late_add")
        to_chips = stage_rs_chips([sb for _, sb in sums])
    g["ffn_conv_w"] = jnp.concatenate([dwg, dwv], axis=1)
    g["ffn_conv_b"] = jnp.concatenate([dbg, dbv], axis=1)
    dpg, dpv = dpg.reshape(T, DFF), dpv.reshape(T, DFF)
    g["ffn_w_up"] = jnp.concatenate([mm(sv["h2"], dpg, ta=True, name=n + "dw_up_g"),
                                     mm(sv["h2"], dpv, ta=True, name=n + "dw_up_v")], axis=1)
    dh2 = mm(dpg, W["ffn_w_up"], tb=True, name=n + "dh2_g")
    dh2 = mm(dpv, W["ffn_w_up"], tb=True, b_k0=DFF, add=dh2, name=n + "dh2_v")
    dx1, dx1b, g["ffn_norm"] = rms_bwd(sv["x1"], W["ffn_norm"], dh2, add=dx2, name=n + "ffn_norm")
    g["w_out"] = mm(sv["ycat"], dx1b, ta=True, name=n + "dw_out")
    dycat = mm(dx1b, W["w_out"], tb=True, name=n + "dycat")
    proj, proj3 = sv["proj"], sv["proj"].reshape(Bl, S, PW)
    dy, dz, g["ssd_norm"] = gated_rms_bwd(sv["y"], proj, W["ssd_norm"], dycat, name=n + "ssd_gate_norm")
    dxa, ddt, dpar, dd = ssd_bwd(sv["xbc3"], proj3, sv["hin"], dy.reshape(Bl, S, SSD_W), W["ssd_dt_bias"], W["ssd_a_log"],
                                 W["ssd_d"], comm=to_chips, name=n + "ssd_scan")
    reduced_late = ([s32 for s32, _ in sums], to_chips["result"]) if to_chips else None
    g["ssd_dt_bias"] = dpar[0, :SSD_HEADS]
    g["ssd_a_log"] = dpar[1, :SSD_HEADS]
    g["ssd_d"] = dd[:, :2].reshape(SSD_HEADS)
    dxbc, g["ssd_conv_w"], g["ssd_conv_b"] = conv_silu_bwd(proj3, W["ssd_conv_w"], W["ssd_conv_b"], dxa, name=n + "ssd_conv")
    du, g["pool_w"], g["pool_scale"] = pool_bwd(proj3, W["pool_w"], W["pool_scale"], dycat.reshape(Bl, S, YCAT), name=n + "pool")
    early_g8 = _by_owner(g, EARLY)
    early_sibling = stage_rs_sibling(early_g8)
    dqr, dlt = flash_bwd_dq(sv["qr"], sv["kr"], sv["vr"], sv["o"], sv["lse"], dycat, Bl, comm=early_sibling, name=n + "attn_dq")
    early_sums = _chip_sums(early_g8, early_sibling["result"], n + "rs_early_add")
    early_chips = stage_rs_chips([sb for _, sb in early_sums])
    dkr, dv = flash_bwd_dkv(sv["qr"], sv["kr"], sv["vr"], sv["lset"], dlt, dycat, Bl, comm=early_chips, name=n + "attn_dkv")
    reduced_early = ([s32 for s32, _ in early_sums], early_chips["result"])
    dqt, dkt, dkpe = mla_prep_bwd(dqr, dkr, cs, sn, name=n + "rope")
    g["mla_w_ukv"] = jnp.concatenate([mm(sv["kvn"], dkt, ta=True, name=n + "dw_uk"),
                                      mm(sv["kvn"], dv, ta=True, name=n + "dw_uv")], axis=1)
    dkvn = mm(dkt, W["mla_w_ukv"], tb=True, name=n + "dkvn_k")
    dkvn = mm(dv, W["mla_w_ukv"], tb=True, b_k0=MLA_H * LANES, add=dkvn, name=n + "dkvn_v")
    g["mla_w_uq"] = mm(sv["qn"], dqt, ta=True, name=n + "dw_uq")
    dqn = mm(dqt, W["mla_w_uq"], tb=True, name=n + "dqn")
    dcq, g["mla_q_norm"] = rms_bwd(proj, W["mla_q_norm"], dqn, col0=CQ0, width=MLA_QR, name=n + "q_norm")
    dckv, g["mla_kv_norm"] = rms_bwd(proj, W["mla_kv_norm"], dkvn, col0=CKV0, width=MLA_KVR, name=n + "kv_norm")
    dproj = jnp.concatenate([dz, dxbc.reshape(T, SSD_XBC), du.reshape(T, POOL_W), dcq, jnp.zeros((T, LANES), BF16), dckv,
                             ddt.reshape(T, LANES), dkpe], axis=1)
    g["w_in"] = mm(sv["h"], dproj, ta=True, name=n + "dw_in")
    dh = mm(dproj, W["w_in"], tb=True, name=n + "dh")
    dx, dxb, g["attn_norm"] = rms_bwd(sv["x"], W["attn_norm"], dh, add=dx1, name=n + "attn_norm")
    return dx, dxb, g, reduced_late, reduced_early


def kernel(x, positions, attn_norm, w_in, ssd_conv_w, ssd_conv_b, ssd_dt_bias, ssd_a_log, ssd_d, ssd_norm, pool_w, pool_scale, mla_q_norm, mla_w_uq, mla_kv_norm, mla_w_ukv, w_out, ffn_norm, ffn_w_up, ffn_conv_w, ffn_conv_b, ffn_w_down, final_norm, loss_target, m_attn_norm, m_w_in, m_ssd_conv_w, m_ssd_conv_b, m_ssd_dt_bias, m_ssd_a_log, m_ssd_d, m_ssd_norm, m_pool_w, m_pool_scale, m_mla_q_norm, m_mla_w_uq, m_mla_kv_norm, m_mla_w_ukv, m_w_out, m_ffn_norm, m_ffn_w_up, m_ffn_conv_w, m_ffn_conv_b, m_ffn_w_down, m_final_norm, v_attn_norm, v_w_in, v_ssd_conv_w, v_ssd_conv_b, v_ssd_dt_bias, v_ssd_a_log, v_ssd_d, v_ssd_norm, v_pool_w, v_pool_scale, v_mla_q_norm, v_mla_w_uq, v_mla_kv_norm, v_mla_w_ukv, v_w_out, v_ffn_norm, v_ffn_w_up, v_ffn_conv_w, v_ffn_conv_b, v_ffn_w_down, v_final_norm):
    a = locals()
    Wt = {k: a[k] for k in ALL_W}
    Mo = {k: a["m_" + k] for k in ALL_W}
    Vo = {k: a["v_" + k] for k in ALL_W}
    Bl, S, _ = x.shape
    T = Bl * S

    names = list(SHARDED)
    conv = ("ssd_conv_w", "ffn_conv_w")

    def blocks_of(l):
        return [Wt[k][l] if k in conv else Wt[k][l].astype(BF16) for k in names]

    def layer_weights(l, gathered):
        full = {k: _from_owner_major(g8, SHARDED[k] - 1) for k, g8 in zip(names, gathered)}
        return {
            "attn_norm": attn_norm[l].reshape(1, D), "w_in": _perm_w_in(full["w_in"]),
            "ssd_conv_w": full["ssd_conv_w"], "ssd_conv_b": ssd_conv_b[l].reshape(1, SSD_XBC),
            "ssd_dt_bias": _lane_pad(ssd_dt_bias[l]), "ssd_a_log": _lane_pad(ssd_a_log[l]),
            "ssd_d": jnp.repeat(ssd_d[l].reshape(NPAIR, 2), SSD_P, axis=1), "ssd_norm": ssd_norm[l].reshape(1, SSD_W),
            "pool_w": pool_w[l].astype(BF16), "pool_scale": pool_scale[l].reshape(1, POOL_W),
            "mla_q_norm": mla_q_norm[l].reshape(1, MLA_QR), "mla_w_uq": _perm_w_uq(full["mla_w_uq"]),
            "mla_kv_norm": mla_kv_norm[l].reshape(1, MLA_KVR), "mla_w_ukv": _perm_w_ukv(full["mla_w_ukv"]),
            "w_out": _perm_w_out(full["w_out"]), "ffn_norm": ffn_norm[l].reshape(1, D),
            "ffn_w_up": full["ffn_w_up"], "ffn_conv_w": full["ffn_conv_w"],
            "ffn_conv_b": ffn_conv_b[l].reshape(1, 2 * DFF), "ffn_w_down": full["ffn_w_down"]}

    pos = positions.astype(F32).reshape(T, 1)
    inv_freq = ROPE_THETA ** (-jnp.arange(0, MLA_ROPE, 2, dtype=F32) / MLA_ROPE)
    invf = jnp.pad(jnp.concatenate([inv_freq, inv_freq]), (PE_LO, LANES - PE_HI)).reshape(1, LANES)
    cs, sn = rope_tables(pos, invf, name="rope_tables")

    layers = [layer_weights(0, all_gather_many(blocks_of(0), name="gather_weights_l0"))]
    xc = x.reshape(T, D)
    saved = []
    for l in range(DEPTH):
        xc, sv, gathered = _layer_fwd(l, xc, layers[l], cs, sn, Bl, next_blocks=blocks_of(l + 1) if l + 1 < DEPTH else None)
        saved.append(sv)
        if gathered is not None:
            layers.append(layer_weights(l + 1, gathered))
    dx, dxb, g_final, loss_part = final_loss(xc, final_norm.reshape(1, D), loss_target.reshape(T, D), name="final_loss")

    grads = [None] * DEPTH
    reduced = {}

    def record(l, keys, red):
        for a, k in enumerate(keys):
            reduced[(l, k)] = (red[0][a], red[1][a])

    later_g8 = None
    for l in reversed(range(DEPTH)):
        dx, dxb, grads[l], red_late, red_early = _layer_bwd(l, dx, dxb, layers[l], saved[l], cs, sn, Bl, later_g8=later_g8)
        if red_late is not None:
            record(l + 1, LATE, red_late)
        record(l, EARLY, red_early)
        later_g8 = _by_owner(grads[l], LATE)
    loss = lax.psum(loss_part[0, 0], AXES)
    sums = _chip_sums(later_g8, run_stage(stage_rs_sibling(later_g8), name="rs_sibling_l0"), "rs_add_l0_")
    record(0, LATE, ([s32 for s32, _ in sums], run_stage(stage_rs_chips([sb for _, sb in sums]), name="rs_chips_l0")))

    xi, yi, _ = _place()
    chip = 2 * xi + yi
    out_g, out_d, out_m, out_v = {}, {}, {}, {}
    for k in names:
        per_layer = []
        for l in range(DEPTH):
            s32, recv = reduced[(l, k)]
            own = lax.dynamic_index_in_dim(s32, chip, 0, keepdims=True)
            per_layer.append(adamw(Wt[k][l], Mo[k][l], Vo[k][l], [own, recv], name=f"adamw_l{l}_{k}"))
        out_g[k], out_d[k], out_m[k], out_v[k] = (jnp.stack([per_layer[l][i] for l in range(DEPTH)]) for i in range(4))

    part = {k: g_final.reshape(D) if k == "final_norm" else
            jnp.stack([grads[l][k].reshape(Wt[k].shape[1:]) for l in range(DEPTH)]) for k in SMALL}
    pg, layout = _pack_small(part)
    pw, _ = _pack_small(Wt)
    pm, _ = _pack_small(Mo)
    pv, _ = _pack_small(Vo)
    (pg8,) = all_gather_many([pg], name="gather_small_grads")
    sg, sd, sm, sv_ = adamw(pw, pm, pv, [pg8], name="adamw_small")
    for dst, buf in ((out_g, sg), (out_d, sd), (out_m, sm), (out_v, sv_)):
        dst.update(_unpack_small(buf, layout))

    return (loss, dx.reshape(Bl, S, D), *[out_g[k] for k in ALL_W], *[out_d[k] for k in ALL_W],
            *[out_m[k] for k in ALL_W], *[out_v[k] for k in ALL_W])
```

```python
import functools
import math

import jax
import jax.numpy as jnp
from jax import lax
from jax.experimental import pallas as pl
from jax.experimental.pallas import tpu as pltpu

F32, BF16 = jnp.float32, jnp.bfloat16
SDS = jax.ShapeDtypeStruct
MESH = pl.DeviceIdType.MESH
AXES = ("x", "y", "c")
N_DEV = 8

D = 1024
EPS = 1e-6
SSD_HEADS, SSD_P, SSD_W, SSD_G, SSD_N, SSD_K, SSD_L, SSD_XBC = 16, 64, 1024, 2, 128, 4, 128, 1536
POOL_G, POOL_D, POOL_W, POOL_WIN = 4, 128, 512, (2, 4, 8, 16)
MLA_H, MLA_QR, MLA_KVR, MLA_NOPE, MLA_ROPE, MLA_V, MLA_QK = 8, 384, 256, 64, 32, 64, 96
ROPE_THETA = 10000.0
MIX = 2048
DFF, FFN_K = 2816, 3
DEPTH = 2
ADAM_LR, ADAM_B1, ADAM_B2, ADAM_EPS, ADAM_WD, ADAM_STEP = 0.001, 0.9, 0.999, 1e-08, 0.01, 10

Z0, XBC0, U0, CQ0, CKV0, DT0, KPE0, PW = 0, 1024, 2560, 3072, 3584, 3840, 3968, 4096
LANES = 128
YCAT = SSD_W + POOL_W + MLA_H * LANES
NEG = -1e30
VMEM_LIMIT = 56 * 1024 * 1024
MM_ROW_TILE, MM_LANE_TILE, MM_FULL_K = 1024, 1408, 2816


def _tile(n, pref, mult):
    if n <= pref:
        return n
    for d in range(pref, 0, -mult):
        if d % mult == 0 and n % d == 0:
            return d
    return n


def _dg(a, b, ca, cb, prec=None):
    return lax.dot_general(a, b, (((ca,), (cb,)), ((), ())), preferred_element_type=F32, precision=prec)


def _nn(a, b):
    return _dg(a.astype(BF16), b.astype(BF16), 1, 0)


def _nt(a, b):
    return _dg(a.astype(BF16), b.astype(BF16), 1, 1)


def _tn(a, b):
    return _dg(a.astype(BF16), b.astype(BF16), 0, 0)


def _sig(x):
    return jax.nn.sigmoid(x)


def _silu(x):
    return x * _sig(x)


def _dsilu(x):
    s = _sig(x)
    return s * (1.0 + x * (1.0 - s))


ANY = pl.BlockSpec(memory_space=pl.ANY)


def _pc(body, *, name, grid, in_specs, out_specs, out_shape, scratch=(), comm=None):
    params = pltpu.CompilerParams(vmem_limit_bytes=VMEM_LIMIT)
    if comm is None:
        return pl.pallas_call(body, name=name, grid=grid, in_specs=in_specs, out_specs=out_specs, out_shape=out_shape,
                              scratch_shapes=list(scratch), compiler_params=params)
    single = not isinstance(out_shape, (list, tuple))
    o_specs = [out_specs] if single else list(out_specs)
    o_shape = [out_shape] if single else list(out_shape)
    ni, no, ns = len(in_specs), len(o_specs), len(scratch)
    nci, nco = len(comm["ins"]), len(comm["out_shape"])

    def fused(*refs):
        ins, cins = refs[:ni], refs[ni:ni + nci]
        outs, couts = refs[ni + nci:ni + nci + no], refs[ni + nci + no:ni + nci + no + nco]
        scr = refs[ni + nci + no + nco:ni + nci + no + nco + ns]
        send_sems, recv_sems = refs[-2:]
        copies = comm["copies"](cins, couts, send_sems, recv_sems)
        first = functools.reduce(jnp.logical_and, [pl.program_id(d) == 0 for d in range(len(grid))])
        last = functools.reduce(jnp.logical_and, [pl.program_id(d) == grid[d] - 1 for d in range(len(grid))])

        @pl.when(first)
        def _():
            for cp in copies:
                cp.start()

        body(*ins, *outs, *scr)

        @pl.when(last)
        def _():
            for cp in copies:
                cp.wait()

    call = pl.pallas_call(
        fused, name=name, grid=grid, in_specs=list(in_specs) + [ANY] * nci, out_specs=o_specs + [ANY] * nco,
        out_shape=o_shape + list(comm["out_shape"]),
        scratch_shapes=list(scratch) + [pltpu.SemaphoreType.DMA((comm["sems"],)), pltpu.SemaphoreType.DMA((comm["sems"],))],
        input_output_aliases={ni + a: no + a for a in range(nci)} if comm.get("alias") else {},
        compiler_params=params)

    def run(*args):
        res = call(*args, *comm["ins"])
        comm["result"] = list(res[no:])
        return res[0] if single else list(res[:no])

    return run


def _rsum(x):
    return jnp.sum(x, axis=1, keepdims=True)


def _csum(x):
    return jnp.sum(x, axis=0, keepdims=True)


def mm(a, b, *, ta=False, tb=False, add=None, out_dtype=F32, b_k0=0, comm=None, name):
    M, K = (a.shape[1], a.shape[0]) if ta else a.shape
    N = b.shape[0] if tb else b.shape[1]
    assert tb or b_k0 == 0
    tm = _tile(M, MM_LANE_TILE, LANES) if ta else _tile(M, MM_ROW_TILE, 8)
    tn = _tile(N, MM_LANE_TILE, LANES)
    if ta:
        tk = _tile(K, MM_ROW_TILE, 8)
    else:
        tk = K if K <= MM_FULL_K else _tile(K, 2048, LANES)
    nk = K // tk

    def body(*refs):
        if add is None:
            a_ref, b_ref, o_ref = refs[:3]
        else:
            a_ref, b_ref, add_ref, o_ref = refs[:4]
        part = _dg(a_ref[...].astype(BF16), b_ref[...].astype(BF16), 0 if ta else 1, 1 if tb else 0)

        def finish(r):
            if add is not None:
                r = r + add_ref[...].astype(F32)
            o_ref[...] = r.astype(out_dtype)

        if nk == 1:
            finish(part)
            return
        acc = refs[-1]
        k = pl.program_id(2)

        @pl.when(k == 0)
        def _():
            acc[...] = part

        @pl.when(k > 0)
        def _():
            acc[...] += part

        @pl.when(k == nk - 1)
        def _():
            finish(acc[...])

    a_spec = pl.BlockSpec((tk, tm), lambda i, j, k: (k, i)) if ta else pl.BlockSpec((tm, tk), lambda i, j, k: (i, k))
    assert b_k0 % tk == 0
    kb0 = b_k0 // tk
    b_spec = pl.BlockSpec((tn, tk), lambda i, j, k: (j, kb0 + k)) if tb else pl.BlockSpec((tk, tn), lambda i, j, k: (k, j))
    o_spec = pl.BlockSpec((tm, tn), lambda i, j, k: (i, j))
    ins, specs = [a, b], [a_spec, b_spec]
    if add is not None:
        ins.append(add)
        specs.append(o_spec)
    return _pc(body, name=name, grid=(M // tm, N // tn, nk), in_specs=specs, out_specs=o_spec,
               out_shape=SDS((M, N), out_dtype), scratch=[pltpu.VMEM((tm, tn), F32)] if nk > 1 else [], comm=comm)(*ins)


def rms_fwd(x, g, *, col0=0, width=None, name):
    T = x.shape[0]
    W = width or x.shape[1]
    tm = _tile(T, 512, 8)

    def body(x_ref, g_ref, o_ref):
        v = x_ref[...]
        r = lax.rsqrt(jnp.mean(v * v, axis=1, keepdims=True) + EPS)
        o_ref[...] = ((v * r) * g_ref[...]).astype(BF16)

    return _pc(body, name=name, grid=(T // tm,),
               in_specs=[pl.BlockSpec((tm, W), lambda i: (i, col0 // W)), pl.BlockSpec((1, W), lambda i: (0, 0))],
               out_specs=pl.BlockSpec((tm, W), lambda i: (i, 0)), out_shape=SDS((T, W), BF16))(x, g)


def rms_bwd(x, g, dh, *, col0=0, width=None, add=None, name):
    T = x.shape[0]
    W = width or x.shape[1]
    tm = _tile(T, 512, 8)

    def body(*refs):
        if add is None:
            x_ref, g_ref, dh_ref, dx_ref, dg_ref = refs
        else:
            x_ref, g_ref, dh_ref, add_ref, dx_ref, dxb_ref, dg_ref = refs
        v = x_ref[...]
        r = lax.rsqrt(jnp.mean(v * v, axis=1, keepdims=True) + EPS)
        xh = v * r
        d = dh_ref[...].astype(F32)
        dxh = d * g_ref[...]
        dx = r * (dxh - xh * jnp.mean(dxh * xh, axis=1, keepdims=True))
        if add is not None:
            dx = dx + add_ref[...]
            dxb_ref[...] = dx.astype(BF16)
        dx_ref[...] = dx.astype(dx_ref.dtype)

        @pl.when(pl.program_id(0) == 0)
        def _():
            dg_ref[...] = jnp.zeros_like(dg_ref)

        dg_ref[...] += _csum(d * xh)

    row = pl.BlockSpec((tm, W), lambda i: (i, 0))
    vec = pl.BlockSpec((1, W), lambda i: (0, 0))
    ins = [x, g, dh] + ([] if add is None else [add])
    specs = [pl.BlockSpec((tm, W), lambda i: (i, col0 // W)), vec, row] + ([] if add is None else [row])
    if add is None:
        return _pc(body, name=name, grid=(T // tm,), in_specs=specs, out_specs=[row, vec],
                   out_shape=[SDS((T, W), BF16), SDS((1, W), F32)])(*ins)
    return _pc(body, name=name, grid=(T // tm,), in_specs=specs, out_specs=[row, row, vec],
               out_shape=[SDS((T, W), F32), SDS((T, W), BF16), SDS((1, W), F32)])(*ins)


def gated_rms_fwd(y, proj, g, *, name):
    T = y.shape[0]
    tm = _tile(T, 512, 8)

    def body(y_ref, z_ref, g_ref, o_ref):
        v = y_ref[...] * _silu(z_ref[...])
        r = lax.rsqrt(jnp.mean(v * v, axis=1, keepdims=True) + EPS)
        o_ref[...] = ((v * r) * g_ref[...]).astype(BF16)

    row = pl.BlockSpec((tm, SSD_W), lambda i: (i, 0))
    return _pc(body, name=name, grid=(T // tm,), in_specs=[row, row, pl.BlockSpec((1, SSD_W), lambda i: (0, 0))],
               out_specs=row, out_shape=SDS((T, SSD_W), BF16))(y, proj, g)


def gated_rms_bwd(y, proj, g, dycat, *, name):
    T = y.shape[0]
    tm = _tile(T, 512, 8)

    def body(y_ref, z_ref, g_ref, d_ref, dy_ref, dz_ref, dg_ref):
        yv, z = y_ref[...], z_ref[...]
        sz = _silu(z)
        v = yv * sz
        r = lax.rsqrt(jnp.mean(v * v, axis=1, keepdims=True) + EPS)
        vh = v * r
        d = d_ref[...]
        dvh = d * g_ref[...]
        dv = r * (dvh - vh * jnp.mean(dvh * vh, axis=1, keepdims=True))
        dy_ref[...] = dv * sz
        dz_ref[...] = (dv * yv * _dsilu(z)).astype(BF16)

        @pl.when(pl.program_id(0) == 0)
        def _():
            dg_ref[...] = jnp.zeros_like(dg_ref)

        dg_ref[...] += _csum(d * vh)

    row = pl.BlockSpec((tm, SSD_W), lambda i: (i, 0))
    vec = pl.BlockSpec((1, SSD_W), lambda i: (0, 0))
    return _pc(body, name=name, grid=(T // tm,), in_specs=[row, row, vec, row], out_specs=[row, row, vec],
               out_shape=[SDS((T, SSD_W), F32), SDS((T, SSD_W), BF16), SDS((1, SSD_W), F32)])(y, proj, g, dycat)


def final_loss(x, g, tgt, *, name):
    T = x.shape[0]
    tm = _tile(T, 512, 8)

    def body(x_ref, g_ref, t_ref, dx_ref, dxb_ref, dg_ref, l_ref):
        v = x_ref[...]
        gg = g_ref[...]
        r = lax.rsqrt(jnp.mean(v * v, axis=1, keepdims=True) + EPS)
        xh = v * r
        err = xh * gg - t_ref[...]
        part = 0.5 * _csum(jnp.mean(err * err, axis=1, keepdims=True))
        d = err * (1.0 / D)
        dxh = d * gg
        dx = r * (dxh - xh * jnp.mean(dxh * xh, axis=1, keepdims=True))
        dx_ref[...] = dx
        dxb_ref[...] = dx.astype(BF16)

        @pl.when(pl.program_id(0) == 0)
        def _():
            dg_ref[...] = jnp.zeros_like(dg_ref)
            l_ref[...] = jnp.zeros_like(l_ref)

        dg_ref[...] += _csum(d * xh)
        l_ref[...] += jnp.broadcast_to(part, (1, LANES))

    row = pl.BlockSpec((tm, D), lambda i: (i, 0))
    vec = pl.BlockSpec((1, D), lambda i: (0, 0))
    return _pc(body, name=name, grid=(T // tm,), in_specs=[row, vec, row],
               out_specs=[row, row, vec, pl.BlockSpec((1, LANES), lambda i: (0, 0))],
               out_shape=[SDS((T, D), F32), SDS((T, D), BF16), SDS((1, D), F32), SDS((1, LANES), F32)])(x, g, tgt)


HALO = 8


def _prev_map(ts, col):
    return lambda b, i, j: (b, jnp.maximum(i * (ts // HALO) - 1, 0), col(j))


def _next_map(ts, n_halo_blocks, col):
    return lambda b, i, j: (b, jnp.minimum((i + 1) * (ts // HALO), n_halo_blocks - 1), col(j))


def conv_silu_fwd(proj3, w, b, *, name):
    Bl, S, _ = proj3.shape
    C, K = SSD_XBC, SSD_K
    ts, tc = _tile(S, 512, 8), 512
    c0 = XBC0 // tc

    def body(xp_ref, x_ref, w_ref, b_ref, o_ref, ext):
        i = pl.program_id(1)
        ext[0:HALO, :] = jnp.where(i > 0, xp_ref[0], 0.0)
        ext[HALO:HALO + ts, :] = x_ref[0]
        acc = b_ref[...] + w_ref[0:1, :] * ext[pl.ds(HALO - (K - 1), ts), :]
        for k in range(1, K):
            acc = acc + w_ref[k:k + 1, :] * ext[pl.ds(HALO - (K - 1) + k, ts), :]
        o_ref[0] = _silu(acc)

    return _pc(body, name=name, grid=(Bl, S // ts, C // tc),
               in_specs=[pl.BlockSpec((1, HALO, tc), _prev_map(ts, lambda j: c0 + j)),
                         pl.BlockSpec((1, ts, tc), lambda b, i, j: (b, i, c0 + j)),
                         pl.BlockSpec((K, tc), lambda b, i, j: (0, j)),
                         pl.BlockSpec((1, tc), lambda b, i, j: (0, j))],
               out_specs=pl.BlockSpec((1, ts, tc), lambda b, i, j: (b, i, j)),
               out_shape=SDS((Bl, S, C), F32), scratch=[pltpu.VMEM((HALO + ts, tc), F32)])(proj3, proj3, w, b)


def conv_silu_bwd(proj3, w, b, dact, *, name):
    Bl, S, _ = proj3.shape
    C, K = SSD_XBC, SSD_K
    ts, tc = _tile(S, 512, 8), 512
    c0 = XBC0 // tc
    ns = S // ts

    def body(xp_ref, x_ref, xn_ref, d_ref, dn_ref, w_ref, b_ref, dx_ref, dw_ref, db_ref, ext, ext2):
        bb, i = pl.program_id(1), pl.program_id(2)
        last = i == ns - 1
        ext[0:HALO, :] = jnp.where(i > 0, xp_ref[0], 0.0)
        ext[HALO:HALO + ts, :] = x_ref[0]
        ext[HALO + ts:2 * HALO + ts, :] = jnp.where(last, 0.0, xn_ref[0])
        taps = [ext[pl.ds(HALO - (K - 1) + k, ts + HALO), :] for k in range(K)]
        acc = b_ref[...] + w_ref[0:1, :] * taps[0]
        for k in range(1, K):
            acc = acc + w_ref[k:k + 1, :] * taps[k]
        dsl = _dsilu(acc)
        du = d_ref[0] * dsl[0:ts]
        ext2[0:ts, :] = du
        ext2[ts:ts + HALO, :] = jnp.where(last, 0.0, dn_ref[0]) * dsl[ts:ts + HALO]
        dx = w_ref[0:1, :] * ext2[pl.ds(K - 1, ts), :]
        for k in range(1, K):
            dx = dx + w_ref[k:k + 1, :] * ext2[pl.ds(K - 1 - k, ts), :]
        dx_ref[0] = dx.astype(BF16)

        @pl.when((bb == 0) & (i == 0))
        def _():
            dw_ref[...] = jnp.zeros_like(dw_ref)
            db_ref[...] = jnp.zeros_like(db_ref)

        for k in range(K):
            dw_ref[k:k + 1, :] += _csum(du * taps[k][0:ts])
        db_ref[...] += _csum(du)

    nhb = S // HALO
    cx = lambda j: c0 + j
    cj = lambda j: j
    return _pc(body, name=name, grid=(C // tc, Bl, ns),
               in_specs=[pl.BlockSpec((1, HALO, tc), lambda j, b, i: _prev_map(ts, cx)(b, i, j)),
                         pl.BlockSpec((1, ts, tc), lambda j, b, i: (b, i, c0 + j)),
                         pl.BlockSpec((1, HALO, tc), lambda j, b, i: _next_map(ts, nhb, cx)(b, i, j)),
                         pl.BlockSpec((1, ts, tc), lambda j, b, i: (b, i, j)),
                         pl.BlockSpec((1, HALO, tc), lambda j, b, i: _next_map(ts, nhb, cj)(b, i, j)),
                         pl.BlockSpec((K, tc), lambda j, b, i: (0, j)),
                         pl.BlockSpec((1, tc), lambda j, b, i: (0, j))],
               out_specs=[pl.BlockSpec((1, ts, tc), lambda j, b, i: (b, i, j)),
                          pl.BlockSpec((K, tc), lambda j, b, i: (0, j)),
                          pl.BlockSpec((1, tc), lambda j, b, i: (0, j))],
               out_shape=[SDS((Bl, S, C), BF16), SDS((K, C), F32), SDS((1, C), F32)],
               scratch=[pltpu.VMEM((2 * HALO + ts, tc), F32), pltpu.VMEM((HALO + ts, tc), F32)],
               )(proj3, proj3, proj3, dact, dact, w, b)


def ffn_act_fwd(pre3, w, b, *, name):
    Bl, S, _ = pre3.shape
    K = FFN_K
    ts, tc = _tile(S, 512, 8), 256
    nj = DFF // tc

    def body(gp_ref, g_ref, vp_ref, v_ref, wg_ref, wv_ref, bg_ref, bv_ref, o_ref, eg, ev):
        i = pl.program_id(1)
        outs = []
        for p_ref, m_ref, w_ref, b_ref, ext in ((gp_ref, g_ref, wg_ref, bg_ref, eg), (vp_ref, v_ref, wv_ref, bv_ref, ev)):
            ext[0:HALO, :] = jnp.where(i > 0, p_ref[0], 0.0)
            ext[HALO:HALO + ts, :] = m_ref[0]
            acc = b_ref[...] + w_ref[0:1, :] * ext[pl.ds(HALO - (K - 1), ts), :]
            for k in range(1, K):
                acc = acc + w_ref[k:k + 1, :] * ext[pl.ds(HALO - (K - 1) + k, ts), :]
            outs.append(acc)
        o_ref[0] = (_silu(outs[0]) * outs[1]).astype(BF16)

    main = lambda off: pl.BlockSpec((1, ts, tc), lambda b, i, j: (b, i, off + j))
    prev = lambda off: pl.BlockSpec((1, HALO, tc), _prev_map(ts, lambda j: off + j))
    wsp = lambda off: pl.BlockSpec((K, tc), lambda b, i, j: (0, off + j))
    bsp = lambda off: pl.BlockSpec((1, tc), lambda b, i, j: (0, off + j))
    return _pc(body, name=name, grid=(Bl, S // ts, nj),
               in_specs=[prev(0), main(0), prev(nj), main(nj), wsp(0), wsp(nj), bsp(0), bsp(nj)],
               out_specs=pl.BlockSpec((1, ts, tc), lambda b, i, j: (b, i, j)),
               out_shape=SDS((Bl, S, DFF), BF16),
               scratch=[pltpu.VMEM((HALO + ts, tc), F32), pltpu.VMEM((HALO + ts, tc), F32)],
               )(pre3, pre3, pre3, pre3, w, w, b, b)


def ffn_act_bwd(pre3, w, b, dact, *, comm=None, name):
    Bl, S, _ = pre3.shape
    K = FFN_K
    ts, tc = _tile(S, 512, 8), 256
    nj = DFF // tc
    ns = S // ts

    def body(gp_ref, g_ref, gn_ref, vp_ref, v_ref, vn_ref, d_ref, dn_ref, wg_ref, wv_ref, bg_ref, bv_ref,
             dg_ref, dv_ref, dwg_ref, dwv_ref, dbg_ref, dbv_ref, eg, ev, e2g, e2v):
        bb, i = pl.program_id(1), pl.program_id(2)
        last = i == ns - 1
        ups, taps = [], []
        for p_ref, m_ref, n_ref, w_ref, b_ref, ext in ((gp_ref, g_ref, gn_ref, wg_ref, bg_ref, eg),
                                                       (vp_ref, v_ref, vn_ref, wv_ref, bv_ref, ev)):
            ext[0:HALO, :] = jnp.where(i > 0, p_ref[0], 0.0)
            ext[HALO:HALO + ts, :] = m_ref[0]
            ext[HALO + ts:2 * HALO + ts, :] = jnp.where(last, 0.0, n_ref[0])
            tp = [ext[pl.ds(HALO - (K - 1) + k, ts + HALO), :] for k in range(K)]
            acc = b_ref[...] + w_ref[0:1, :] * tp[0]
            for k in range(1, K):
                acc = acc + w_ref[k:k + 1, :] * tp[k]
            ups.append(acc)
            taps.append(tp)
        ug, uv = ups
        dg_e = uv * _dsilu(ug)
        dv_e = _silu(ug)
        d_main = d_ref[0]
        d_next = jnp.where(last, 0.0, dn_ref[0])
        dug = d_main * dg_e[0:ts]
        duv = d_main * dv_e[0:ts]
        e2g[0:ts, :] = dug
        e2g[ts:ts + HALO, :] = d_next * dg_e[ts:ts + HALO]
        e2v[0:ts, :] = duv
        e2v[ts:ts + HALO, :] = d_next * dv_e[ts:ts + HALO]

        @pl.when((bb == 0) & (i == 0))
        def _():
            for r in (dwg_ref, dwv_ref, dbg_ref, dbv_ref):
                r[...] = jnp.zeros_like(r)

        for w_ref, e2, tp, du, o_ref, dw_ref, db_ref in ((wg_ref, e2g, taps[0], dug, dg_ref, dwg_ref, dbg_ref),
                                                         (wv_ref, e2v, taps[1], duv, dv_ref, dwv_ref, dbv_ref)):
            dx = w_ref[0:1, :] * e2[pl.ds(K - 1, ts), :]
            for k in range(1, K):
                dx = dx + w_ref[k:k + 1, :] * e2[pl.ds(K - 1 - k, ts), :]
            o_ref[0] = dx.astype(BF16)
            for k in range(K):
                dw_ref[k:k + 1, :] += _csum(du * tp[k][0:ts])
            db_ref[...] += _csum(du)

    nhb = S // HALO
    main = lambda off: pl.BlockSpec((1, ts, tc), lambda j, b, i: (b, i, off + j))
    prev = lambda off: pl.BlockSpec((1, HALO, tc), lambda j, b, i: _prev_map(ts, lambda jj: off + jj)(b, i, j))
    nxt = lambda off: pl.BlockSpec((1, HALO, tc), lambda j, b, i: _next_map(ts, nhb, lambda jj: off + jj)(b, i, j))
    wsp = lambda off: pl.BlockSpec((K, tc), lambda j, b, i: (0, off + j))
    bsp = lambda off: pl.BlockSpec((1, tc), lambda j, b, i: (0, off + j))
    outs = _pc(body, name=name, grid=(nj, Bl, ns),
               in_specs=[prev(0), main(0), nxt(0), prev(nj), main(nj), nxt(nj), main(0), nxt(0),
                         wsp(0), wsp(nj), bsp(0), bsp(nj)],
               out_specs=[main(0), main(0), wsp(0), wsp(0), bsp(0), bsp(0)],
               out_shape=[SDS((Bl, S, DFF), BF16), SDS((Bl, S, DFF), BF16), SDS((K, DFF), F32), SDS((K, DFF), F32),
                          SDS((1, DFF), F32), SDS((1, DFF), F32)],
               scratch=[pltpu.VMEM((2 * HALO + ts, tc), F32), pltpu.VMEM((2 * HALO + ts, tc), F32),
                        pltpu.VMEM((HALO + ts, tc), F32), pltpu.VMEM((HALO + ts, tc), F32)],
               comm=comm)(pre3, pre3, pre3, pre3, pre3, pre3, dact, dact, w, w, b, b)
    return outs


PHALO = 16


def _pool_window_sums(ext, base, ts, step):
    s = ext[pl.ds(base, ts), :]
    out = []
    for i in range(1, PHALO):
        s = s + ext[pl.ds(base + step * i, ts), :]
        if i + 1 in POOL_WIN:
            out.append(s)
    return out


def _pick(g, vals):
    r = vals[-1]
    for k in range(len(vals) - 2, -1, -1):
        r = jnp.where(g == k, vals[k], r)
    return r


def _pool_count(g, i, ts, rows):
    t = (i * ts + lax.broadcasted_iota(jnp.int32, (rows, 1), 0) + 1).astype(F32)
    return jnp.minimum(t, _pick(g, [float(w) for w in POOL_WIN]))


def _pooled(up_ref, u_ref, ext, g, i, ts):
    ext[0:PHALO, :] = jnp.where(i > 0, up_ref[0], 0.0)
    u = u_ref[0]
    ext[PHALO:PHALO + ts, :] = u
    sums = _pool_window_sums(ext, PHALO, ts, -1)
    return _pick(g, sums) / _pool_count(g, i, ts, ts) - u


def pool_fwd(proj3, pool_w, scale, *, name):
    Bl, S, _ = proj3.shape
    ts = _tile(S, 512, 16)
    c0 = U0 // POOL_D

    def body(up_ref, u_ref, w_ref, s_ref, o_ref, ext):
        i, g = pl.program_id(1), pl.program_id(2)
        pooled = _pooled(up_ref, u_ref, ext, g, i, ts)
        o_ref[0] = (_nn(pooled, w_ref[0]) * s_ref[...]).astype(BF16)

    return _pc(body, name=name, grid=(Bl, S // ts, POOL_G),
               in_specs=[pl.BlockSpec((1, PHALO, POOL_D), lambda b, i, g: (b, jnp.maximum(i * (ts // PHALO) - 1, 0), c0 + g)),
                         pl.BlockSpec((1, ts, POOL_D), lambda b, i, g: (b, i, c0 + g)),
                         pl.BlockSpec((1, POOL_D, POOL_D), lambda b, i, g: (g, 0, 0)),
                         pl.BlockSpec((1, POOL_D), lambda b, i, g: (0, g))],
               out_specs=pl.BlockSpec((1, ts, POOL_D), lambda b, i, g: (b, i, g)),
               out_shape=SDS((Bl, S, POOL_W), BF16), scratch=[pltpu.VMEM((PHALO + ts, POOL_D), F32)],
               )(proj3, proj3, pool_w, scale)


def pool_bwd(proj3, pool_w, scale, dycat3, *, name):
    Bl, S, _ = proj3.shape
    ts = _tile(S, 512, 16)
    ns = S // ts
    c0 = U0 // POOL_D
    d0 = SSD_W // POOL_D
    nhb = S // PHALO

    def body(up_ref, u_ref, d_ref, dn_ref, w_ref, s_ref, du_ref, dw_ref, ds_ref, ext, ext2):
        g, bb, i = pl.program_id(0), pl.program_id(1), pl.program_id(2)
        last = i == ns - 1
        pooled = _pooled(up_ref, u_ref, ext, g, i, ts)
        wm = w_ref[0]
        sc = s_ref[...]
        dy = d_ref[0]
        dp_main = dy * sc
        dpool = _nt(dp_main, wm)
        dpool_n = _nt(jnp.where(last, 0.0, dn_ref[0]) * sc, wm)
        ext2[0:ts, :] = dpool / _pool_count(g, i, ts, ts)
        ext2[ts:ts + PHALO, :] = dpool_n / _pool_count(g, i + 1, ts, PHALO)
        sums = _pool_window_sums(ext2, 0, ts, 1)
        du_ref[0] = (_pick(g, sums) - dpool).astype(BF16)

        @pl.when((bb == 0) & (i == 0))
        def _():
            dw_ref[...] = jnp.zeros_like(dw_ref)
            ds_ref[...] = jnp.zeros_like(ds_ref)

        dw_ref[0] += _tn(pooled, dp_main)
        ds_ref[...] += _csum(dy * _nn(pooled, wm))

    return _pc(body, name=name, grid=(POOL_G, Bl, ns),
               in_specs=[pl.BlockSpec((1, PHALO, POOL_D), lambda g, b, i: (b, jnp.maximum(i * (ts // PHALO) - 1, 0), c0 + g)),
                         pl.BlockSpec((1, ts, POOL_D), lambda g, b, i: (b, i, c0 + g)),
                         pl.BlockSpec((1, ts, POOL_D), lambda g, b, i: (b, i, d0 + g)),
                         pl.BlockSpec((1, PHALO, POOL_D), lambda g, b, i: (b, jnp.minimum((i + 1) * (ts // PHALO), nhb - 1), d0 + g)),
                         pl.BlockSpec((1, POOL_D, POOL_D), lambda g, b, i: (g, 0, 0)),
                         pl.BlockSpec((1, POOL_D), lambda g, b, i: (0, g))],
               out_specs=[pl.BlockSpec((1, ts, POOL_D), lambda g, b, i: (b, i, g)),
                          pl.BlockSpec((1, POOL_D, POOL_D), lambda g, b, i: (g, 0, 0)),
                          pl.BlockSpec((1, POOL_D), lambda g, b, i: (0, g))],
               out_shape=[SDS((Bl, S, POOL_W), BF16), SDS((POOL_G, POOL_D, POOL_D), F32), SDS((1, POOL_W), F32)],
               scratch=[pltpu.VMEM((PHALO + ts, POOL_D), F32), pltpu.VMEM((PHALO + ts, POOL_D), F32)],
               )(proj3, proj3, dycat3, dycat3, pool_w, scale)


NPAIR = SSD_HEADS // 2


def _ssd_common(sm, bias, alog):
    L = SSD_L
    dt = jax.nn.softplus(sm + bias)
    a = -jnp.exp(alog)
    da = dt * a
    r = lax.broadcasted_iota(jnp.int32, (L, L), 0)
    c = lax.broadcasted_iota(jnp.int32, (L, L), 1)
    tri = (r >= c).astype(F32)
    cum = _dg(tri, da, 1, 0, lax.Precision.HIGHEST)
    return dt, a, cum, cum.T, r >= c


def _lanes(lo, hi, shape=(1, LANES)):
    lane = lax.broadcasted_iota(jnp.int32, shape, len(shape) - 1)
    return (lane >= lo) & (lane < hi)


def _onehot_lane(h):
    return (lax.broadcasted_iota(jnp.int32, (1, LANES), 1) == h).astype(F32)


def _split_nn(a, e):
    hi = a.astype(BF16)
    lo = (a - hi.astype(F32)).astype(BF16)
    return _dg(hi, e, 1, 0) + _dg(lo, e, 1, 0)


def _head_spread():
    r = lax.broadcasted_iota(jnp.int32, (LANES, SSD_W), 0)
    c = lax.broadcasted_iota(jnp.int32, (LANES, SSD_W), 1)
    return (c // SSD_P == r).astype(BF16)


def _pair_gather(j):
    r = lax.broadcasted_iota(jnp.int32, (LANES, LANES), 0)
    c = lax.broadcasted_iota(jnp.int32, (LANES, LANES), 1)
    return (c == 2 * j + (r >= SSD_P).astype(jnp.int32)).astype(BF16)


def ssd_fwd(xbc3, proj3, bias, alog, dskip, *, name):
    Bl, S, _ = xbc3.shape
    L = SSD_L
    nc = S // L

    def body(xbc_ref, sm_ref, bias_ref, alog_ref, d_ref, y_ref, hin_ref, H):
        c = pl.program_id(1)

        @pl.when(c == 0)
        def _():
            H[...] = jnp.zeros_like(H)

        dt, a, cum, cumT, mask = _ssd_common(sm_ref[0], bias_ref[...], alog_ref[...])
        lo = _lanes(0, SSD_P)
        rowlo = lax.broadcasted_iota(jnp.int32, (LANES, LANES), 0) < SSD_P
        spread = _head_spread()
        dt_x = _split_nn(dt, spread)
        el_x = _split_nn(jnp.exp(cum), spread)
        wl_x = _split_nn(jnp.exp(cum[L - 1:L, :] - cum), spread)
        cb = []
        for g in range(SSD_G):
            Bg = xbc_ref[0, :, SSD_W + g * SSD_N:SSD_W + (g + 1) * SSD_N]
            Cg = xbc_ref[0, :, SSD_W + SSD_G * SSD_N + g * SSD_N:SSD_W + SSD_G * SSD_N + (g + 1) * SSD_N]
            cb.append((Bg, Cg, _nt(Cg, Bg)))
        for j in range(NPAIR):
            h0, h1 = 2 * j, 2 * j + 1
            sl = slice(j * LANES, (j + 1) * LANES)
            Bg, Cg, CB = cb[j // (NPAIR // SSD_G)]
            X = xbc_ref[0, :, sl]
            c0, c1 = cum[:, h0:h0 + 1], cum[:, h1:h1 + 1]
            r0, r1 = cumT[h0:h0 + 1, :], cumT[h1:h1 + 1, :]
            cl0, cl1 = cum[L - 1:L, h0:h0 + 1], cum[L - 1:L, h1:h1 + 1]
            Xt = X * dt_x[:, sl]
            M0 = CB * jnp.exp(jnp.where(mask, c0 - r0, NEG))
            M1 = CB * jnp.exp(jnp.where(mask, c1 - r1, NEG))
            Yd = jnp.where(lo, _nn(M0, Xt), _nn(M1, Xt))
            Hp = H[j]
            hin_ref[0, 0, j] = Hp
            Z = _nt(Cg, Hp)
            y_ref[0, :, sl] = Yd + el_x[:, sl] * Z + X * d_ref[j:j + 1, :]
            H[j] = jnp.where(rowlo, jnp.exp(cl0), jnp.exp(cl1)) * Hp + _tn(wl_x[:, sl] * Xt, Bg)

    vec = pl.BlockSpec((1, LANES), lambda b, c: (0, 0))
    return _pc(body, name=name, grid=(Bl, nc),
               in_specs=[pl.BlockSpec((1, L, SSD_XBC), lambda b, c: (b, c, 0)),
                         pl.BlockSpec((1, L, LANES), lambda b, c: (b, c, DT0 // LANES)),
                         vec, vec, pl.BlockSpec((NPAIR, LANES), lambda b, c: (0, 0))],
               out_specs=[pl.BlockSpec((1, L, SSD_W), lambda b, c: (b, c, 0)),
                          pl.BlockSpec((1, 1, NPAIR, LANES, LANES), lambda b, c: (b, c, 0, 0, 0))],
               out_shape=[SDS((Bl, S, SSD_W), F32), SDS((Bl, nc, NPAIR, LANES, LANES), F32)],
               scratch=[pltpu.VMEM((NPAIR, LANES, LANES), F32)])(xbc3, proj3, bias, alog, dskip)


def ssd_bwd(xbc3, proj3, hin, dy3, bias, alog, dskip, *, comm=None, name):
    Bl, S, _ = xbc3.shape
    L = SSD_L
    nc = S // L

    def body(xbc_ref, sm_ref, hin_ref, dy_ref, bias_ref, alog_ref, d_ref, dx_ref, ddt_ref, dpar_ref, dd_ref, dH, ddacc):
        bb, i = pl.program_id(0), pl.program_id(1)

        @pl.when(i == 0)
        def _():
            dH[...] = jnp.zeros_like(dH)

        @pl.when((bb == 0) & (i == 0))
        def _():
            dpar_ref[...] = jnp.zeros_like(dpar_ref)
            ddacc[...] = jnp.zeros_like(ddacc)

        sm = sm_ref[0]
        dt, a, cum, cumT, mask = _ssd_common(sm, bias_ref[...], alog_ref[...])
        maskf = mask.astype(F32)
        lo = _lanes(0, SSD_P)
        rowlo = lax.broadcasted_iota(jnp.int32, (LANES, LANES), 0) < SSD_P
        lastrow = (lax.broadcasted_iota(jnp.int32, (L, 1), 0) == L - 1).astype(F32)
        dcum = jnp.zeros((L, LANES), F32)
        dcum_t = jnp.zeros((LANES, L), F32)
        ddt = jnp.zeros((L, LANES), F32)
        spread = _head_spread()
        ones = jnp.ones((L, LANES), BF16)
        ecum = jnp.exp(cum)
        wall = jnp.exp(cum[L - 1:L, :] - cum)
        dt_x = _split_nn(dt, spread)
        el_x = _split_nn(ecum, spread)
        wl_x = _split_nn(wall, spread)
        headrow = lax.broadcasted_iota(jnp.int32, (LANES, 1), 0)
        grp = []
        for g in range(SSD_G):
            Bg = xbc_ref[0, :, SSD_W + g * SSD_N:SSD_W + (g + 1) * SSD_N]
            Cg = xbc_ref[0, :, SSD_W + SSD_G * SSD_N + g * SSD_N:SSD_W + SSD_G * SSD_N + (g + 1) * SSD_N]
            grp.append(dict(B=Bg, C=Cg, CB=_nt(Cg, Bg), dB=jnp.zeros((L, SSD_N), F32), dC=jnp.zeros((L, SSD_N), F32),
                            dCB=jnp.zeros((L, L), F32)))
        for j in range(NPAIR):
            h0, h1 = 2 * j, 2 * j + 1
            sl = slice(j * LANES, (j + 1) * LANES)
            G = grp[j // (NPAIR // SSD_G)]
            Bg, Cg, CB = G["B"], G["C"], G["CB"]
            X = xbc_ref[0, :, sl]
            dY = dy_ref[0, :, sl]
            c0, c1 = cum[:, h0:h0 + 1], cum[:, h1:h1 + 1]
            r0, r1 = cumT[h0:h0 + 1, :], cumT[h1:h1 + 1, :]
            cl0, cl1 = cum[L - 1:L, h0:h0 + 1], cum[L - 1:L, h1:h1 + 1]
            oh0, oh1 = _onehot_lane(h0), _onehot_lane(h1)
            gather = _pair_gather(j)
            dtl, el, wl = dt_x[:, sl], el_x[:, sl], wl_x[:, sl]
            Xt = X * dtl
            Hp = hin_ref[0, 0, j]
            dS = dH[j]
            dX = dY * d_ref[j:j + 1, :]
            ddacc[j:j + 1, :] += _csum(dY * X)
            Z = _nt(Cg, Hp)
            dZ = dY * el
            dcum = dcum + _split_nn(dY * Z, gather) * ecum
            G["dC"] = G["dC"] + _nn(dZ, Hp)
            dHy = _tn(dZ, Cg)
            Gm = _nt(Bg, dS)
            dXt = wl * Gm
            q = _split_nn(Xt * Gm, gather) * wall
            dcum = dcum + lastrow * _csum(q) - q
            G["dB"] = G["dB"] + _nn(wl * Xt, dS)
            g0, g1 = jnp.exp(cl0), jnp.exp(cl1)
            rowsum = _nn(dS * Hp, ones)
            dg0 = _csum(jnp.where(rowlo, rowsum, 0.0))
            dg1 = _csum(jnp.where(rowlo, 0.0, rowsum))
            dcum = dcum + lastrow * ((dg0 * g0) * oh0 + (dg1 * g1) * oh1)
            dH[j] = jnp.where(rowlo, g0, g1) * dS + dHy
            for h, ch, rh, mh, oh in ((h0, c0, r0, lo, oh0), (h1, c1, r1, jnp.logical_not(lo), oh1)):
                decay = jnp.exp(jnp.where(mask, ch - rh, NEG))
                Mh = CB * decay
                dM = _nt(jnp.where(mh, dY, 0.0), Xt) * maskf
                dXt = dXt + jnp.where(mh, _tn(Mh, dY), 0.0)
                G["dCB"] = G["dCB"] + dM * decay
                Q = dM * Mh
                dcum = dcum + _rsum(Q) * oh
                dcum_t = dcum_t + (headrow == h).astype(F32) * _csum(Q)
            dX = dX + dXt * dtl
            ddt = ddt + _split_nn(dXt * X, gather)
            dx_ref[0, :, sl] = dX
        dcum = dcum - dcum_t.T
        for g in range(SSD_G):
            G = grp[g]
            dC = G["dC"] + _nn(G["dCB"], G["B"])
            dB = G["dB"] + _tn(G["dCB"], G["C"])
            dx_ref[0, :, SSD_W + g * SSD_N:SSD_W + (g + 1) * SSD_N] = dB
            dx_ref[0, :, SSD_W + SSD_G * SSD_N + g * SSD_N:SSD_W + SSD_G * SSD_N + (g + 1) * SSD_N] = dC
        r = lax.broadcasted_iota(jnp.int32, (L, L), 0)
        c = lax.broadcasted_iota(jnp.int32, (L, L), 1)
        dda = _dg((c >= r).astype(F32), dcum, 1, 0, lax.Precision.HIGHEST)
        heads = _lanes(0, SSD_HEADS)
        ddt = ddt + dda * a
        draw = jnp.where(heads, ddt * _sig(sm + bias_ref[...]), 0.0)
        ddt_ref[0] = draw.astype(BF16)
        dpar_ref[0:1, :] += _csum(draw)
        dpar_ref[1:2, :] += _csum(jnp.where(heads, dda * dt * a, 0.0))

        @pl.when((bb == Bl - 1) & (i == nc - 1))
        def _():
            acc = ddacc[...]
            lane = lax.broadcasted_iota(jnp.int32, (NPAIR, LANES), 1)
            s0 = _rsum(jnp.where(lane < SSD_P, acc, 0.0))
            s1 = _rsum(jnp.where(lane < SSD_P, 0.0, acc))
            dd_ref[...] = jnp.where(lane == 0, s0, jnp.where(lane == 1, s1, 0.0))

    vec = pl.BlockSpec((1, LANES), lambda b, i: (0, 0))
    par = pl.BlockSpec((NPAIR, LANES), lambda b, i: (0, 0))
    return _pc(body, name=name, grid=(Bl, nc),
               in_specs=[pl.BlockSpec((1, L, SSD_XBC), lambda b, i: (b, nc - 1 - i, 0)),
                         pl.BlockSpec((1, L, LANES), lambda b, i: (b, nc - 1 - i, DT0 // LANES)),
                         pl.BlockSpec((1, 1, NPAIR, LANES, LANES), lambda b, i: (b, nc - 1 - i, 0, 0, 0)),
                         pl.BlockSpec((1, L, SSD_W), lambda b, i: (b, nc - 1 - i, 0)),
                         vec, vec, par],
               out_specs=[pl.BlockSpec((1, L, SSD_XBC), lambda b, i: (b, nc - 1 - i, 0)),
                          pl.BlockSpec((1, L, LANES), lambda b, i: (b, nc - 1 - i, 0)),
                          par, par],
               out_shape=[SDS((Bl, S, SSD_XBC), F32), SDS((Bl, S, LANES), BF16), SDS((NPAIR, LANES), F32),
                          SDS((NPAIR, LANES), F32)],
               scratch=[pltpu.VMEM((NPAIR, LANES, LANES), F32), pltpu.VMEM((NPAIR, LANES), F32)],
               comm=comm)(xbc3, proj3, hin, dy3, bias, alog, dskip)


PE_LO, PE_MID, PE_HI = MLA_NOPE, MLA_NOPE + MLA_ROPE // 2, MLA_NOPE + MLA_ROPE
ATT_SCALE = 1.0 / math.sqrt(MLA_QK)


def _swap_matrix():
    src = lax.broadcasted_iota(jnp.int32, (LANES, LANES), 0)
    dst = lax.broadcasted_iota(jnp.int32, (LANES, LANES), 1)
    half = MLA_ROPE // 2
    first = (dst >= PE_LO) & (dst < PE_MID) & (src == dst + half)
    second = (dst >= PE_MID) & (dst < PE_HI) & (src == dst - half)
    return (second.astype(F32) - first.astype(F32)).astype(BF16)


def rope_tables(pos, invf, *, name):
    T = pos.shape[0]
    tm = _tile(T, 512, 8)

    def body(pos_ref, f_ref, c_ref, s_ref):
        ang = pos_ref[...] * f_ref[...]
        pe = _lanes(PE_LO, PE_HI)
        c_ref[...] = jnp.where(pe, jnp.cos(ang), 1.0)
        s_ref[...] = jnp.where(pe, jnp.sin(ang), 0.0)

    tile = pl.BlockSpec((tm, LANES), lambda i: (i, 0))
    return _pc(body, name=name, grid=(T // tm,),
               in_specs=[pl.BlockSpec((tm, 1), lambda i: (i, 0)), pl.BlockSpec((1, LANES), lambda i: (0, 0))],
               out_specs=[tile, tile], out_shape=[SDS((T, LANES), F32)] * 2)(pos, invf)


V_ONE = MLA_V


def mla_prep_fwd(qt, kvt, proj, cs, sn, *, name):
    T = qt.shape[0]
    tm = _tile(T, 256, 8)
    HW = MLA_H * LANES

    def body(q_ref, k_ref, v_ref, kpe_ref, c_ref, s_ref, qo_ref, ko_ref, vo_ref):
        c, s = c_ref[...], s_ref[...]
        kpe = kpe_ref[...]
        sw = _swap_matrix()
        one = _lanes(V_ONE, V_ONE + 1)
        for h in range(MLA_H):
            sl = slice(h * LANES, (h + 1) * LANES)
            q = q_ref[:, sl]
            k = k_ref[:, sl] + kpe
            qo_ref[:, sl] = ((q * c + _split_nn(q, sw) * s) * ATT_SCALE).astype(BF16)
            ko_ref[:, sl] = (k * c + _split_nn(k, sw) * s).astype(BF16)
            vo_ref[:, sl] = jnp.where(one, 1.0, v_ref[:, sl]).astype(BF16)

    row = pl.BlockSpec((tm, HW), lambda i: (i, 0))
    tab = pl.BlockSpec((tm, LANES), lambda i: (i, 0))
    return _pc(body, name=name, grid=(T // tm,),
               in_specs=[row, row, pl.BlockSpec((tm, HW), lambda i: (i, 1)),
                         pl.BlockSpec((tm, LANES), lambda i: (i, KPE0 // LANES)), tab, tab],
               out_specs=[row, row, row], out_shape=[SDS((T, HW), BF16)] * 3)(qt, kvt, kvt, proj, cs, sn)


def mla_prep_bwd(dqr, dkr, cs, sn, *, name):
    T = dqr.shape[0]
    tm = _tile(T, 256, 8)
    HW = MLA_H * LANES

    def body(dq_ref, dk_ref, c_ref, s_ref, qo_ref, ko_ref, kpe_ref):
        c, s = c_ref[...], s_ref[...]
        sw = _swap_matrix()
        pe = _lanes(PE_LO, PE_HI)
        dkpe = jnp.zeros((tm, LANES), F32)
        for h in range(MLA_H):
            sl = slice(h * LANES, (h + 1) * LANES)
            dq = dq_ref[:, sl] * ATT_SCALE
            dk = dk_ref[:, sl]
            qo_ref[:, sl] = (dq * c - _split_nn(dq * s, sw)).astype(BF16)
            dkk = dk * c - _split_nn(dk * s, sw)
            ko_ref[:, sl] = jnp.where(pe, 0.0, dkk).astype(BF16)
            dkpe = dkpe + jnp.where(pe, dkk, 0.0)
        kpe_ref[...] = dkpe.astype(BF16)

    row = pl.BlockSpec((tm, HW), lambda i: (i, 0))
    tab = pl.BlockSpec((tm, LANES), lambda i: (i, 0))
    return _pc(body, name=name, grid=(T // tm,), in_specs=[row, row, tab, tab], out_specs=[row, row, tab],
               out_shape=[SDS((T, HW), BF16), SDS((T, HW), BF16), SDS((T, LANES), BF16)])(dqr, dkr, cs, sn)


def _att_tile(S):
    return _tile(S, 512, LANES)


def _rep(x, n):
    return x if n == 1 else jnp.concatenate([x] * n, axis=1)


def _diag_mask(t, transposed=False):
    r = lax.broadcasted_iota(jnp.int32, (t, t), 0)
    c = lax.broadcasted_iota(jnp.int32, (t, t), 1)
    return (c >= r) if transposed else (c <= r)


def flash_fwd(qr, kr, vr, Bl, *, comm=None, name):
    T = qr.shape[0]
    S = T // Bl
    t = _att_tile(S)
    n = S // t
    nl = t // LANES

    def body(q_ref, k_ref, v_ref, o_ref, lse_ref, lset_ref, m, acc):
        qi = pl.program_id(2)
        q = q_ref[...]
        m[...] = jnp.full_like(m, NEG)
        acc[...] = jnp.zeros_like(acc)

        def block(kj, masked):
            off = pl.multiple_of(kj * t, t)
            s = _nt(q, k_ref[pl.ds(off, t), :])
            if masked:
                s = jnp.where(_diag_mask(t), s, NEG)
            mo = m[...]
            mn = jnp.maximum(mo, jnp.max(s, axis=1, keepdims=True))
            p = jnp.exp((s - _rep(mn, nl)).astype(BF16))
            acc[...] = jnp.exp(mo - mn) * acc[...] + _nn(p, v_ref[pl.ds(off, t), :])
            m[...] = mn

        def loop(kj, c):
            block(kj, False)
            return c

        lax.fori_loop(0, qi, loop, 0)
        block(qi, True)
        a = acc[...]
        l = a[:, V_ONE:V_ONE + 1]
        o_ref[...] = jnp.where(_lanes(0, MLA_V), a / l, 0.0).astype(BF16)
        lse = m[...] + jnp.log(l)
        lse_ref[...] = lse
        lset_ref[...] = lse.T[0:8, :]

    qs = pl.BlockSpec((t, LANES), lambda b, h, qi: (b * n + qi, h))
    seq = pl.BlockSpec((S, LANES), lambda b, h, qi: (b, h))
    return _pc(body, name=name, grid=(Bl, MLA_H, n), in_specs=[qs, seq, seq],
               out_specs=[qs, qs, pl.BlockSpec((8, t), lambda b, h, qi: (b * MLA_H + h, qi))],
               out_shape=[SDS((T, MLA_H * LANES), BF16), SDS((T, MLA_H * LANES), F32), SDS((Bl * MLA_H * 8, S), F32)],
               scratch=[pltpu.VMEM((t, LANES), F32), pltpu.VMEM((t, LANES), F32)], comm=comm)(qr, kr, vr)


def flash_bwd_dq(qr, kr, vr, o, lse, dycat, Bl, *, comm=None, name):
    T = qr.shape[0]
    S = T // Bl
    t = _att_tile(S)
    n = S // t
    nl = t // LANES
    do0 = (SSD_W + POOL_W) // LANES

    def body(q_ref, k_ref, v_ref, o_ref, lse_ref, do_ref, dq_ref, dlt_ref, acc, dl):
        qi = pl.program_id(2)
        q = q_ref[...]
        do = do_ref[...]
        dob = do.astype(BF16)
        dl[...] = jnp.broadcast_to(_rsum(do * o_ref[...].astype(F32)), (t, LANES))
        acc[...] = jnp.zeros_like(acc)

        def block(kj, masked):
            off = pl.multiple_of(kj * t, t)
            k = k_ref[pl.ds(off, t), :]
            s = _nt(q, k)
            if masked:
                s = jnp.where(_diag_mask(t), s, NEG)
            p = jnp.exp((s - _rep(lse_ref[...], nl)).astype(BF16))
            dp = _nt(dob, v_ref[pl.ds(off, t), :])
            acc[...] += _nn(p * (dp - _rep(dl[...], nl)), k)

        def loop(kj, c):
            block(kj, False)
            return c

        lax.fori_loop(0, qi, loop, 0)
        block(qi, True)
        dq_ref[...] = acc[...]
        dlt_ref[...] = dl[...].T[0:8, :]

    qs = pl.BlockSpec((t, LANES), lambda b, h, qi: (b * n + qi, h))
    seq = pl.BlockSpec((S, LANES), lambda b, h, qi: (b, h))
    return _pc(body, name=name, grid=(Bl, MLA_H, n),
               in_specs=[qs, seq, seq, qs, qs, pl.BlockSpec((t, LANES), lambda b, h, qi: (b * n + qi, do0 + h))],
               out_specs=[qs, pl.BlockSpec((8, t), lambda b, h, qi: (b * MLA_H + h, qi))],
               out_shape=[SDS((T, MLA_H * LANES), F32), SDS((Bl * MLA_H * 8, S), F32)],
               scratch=[pltpu.VMEM((t, LANES), F32), pltpu.VMEM((t, LANES), F32)], comm=comm)(qr, kr, vr, o, lse, dycat)


def flash_bwd_dkv(qr, kr, vr, lset, dlt, dycat, Bl, *, comm=None, name):
    T = qr.shape[0]
    S = T // Bl
    t = _att_tile(S)
    n = S // t
    do0 = (SSD_W + POOL_W) // LANES

    def body(q_ref, k_ref, v_ref, lset_ref, dlt_ref, do_ref, dk_ref, dv_ref, dka, dva):
        kj = pl.program_id(2)
        k = k_ref[...]
        v = v_ref[...]
        dka[...] = jnp.zeros_like(dka)
        dva[...] = jnp.zeros_like(dva)

        def block(qi, masked):
            off = pl.multiple_of(qi * t, t)
            q = q_ref[pl.ds(off, t), :]
            do = do_ref[pl.ds(off, t), :].astype(BF16)
            st = _nt(k, q)
            if masked:
                st = jnp.where(_diag_mask(t, True), st, NEG)
            pt = jnp.exp((st - lset_ref[0:1, pl.ds(off, t)]).astype(BF16))
            dst = pt * (_nt(v, do) - dlt_ref[0:1, pl.ds(off, t)])
            dva[...] += _nn(pt, do)
            dka[...] += _nn(dst, q)

        def loop(qi, c):
            block(qi, False)
            return c

        block(kj, True)
        lax.fori_loop(kj + 1, n, loop, 0)
        dk_ref[...] = dka[...]
        dv_ref[...] = dva[...].astype(BF16)

    ks = pl.BlockSpec((t, LANES), lambda b, h, kj: (b * n + kj, h))
    seq = pl.BlockSpec((S, LANES), lambda b, h, kj: (b, h))
    rows = pl.BlockSpec((8, S), lambda b, h, kj: (b * MLA_H + h, 0))
    return _pc(body, name=name, grid=(Bl, MLA_H, n),
               in_specs=[seq, ks, ks, rows, rows, pl.BlockSpec((S, LANES), lambda b, h, kj: (b, do0 + h))],
               out_specs=[ks, ks], out_shape=[SDS((T, MLA_H * LANES), F32), SDS((T, MLA_H * LANES), BF16)],
               scratch=[pltpu.VMEM((t, LANES), F32), pltpu.VMEM((t, LANES), F32)], comm=comm)(qr, kr, vr, lset, dlt, dycat)


def _rows2d(a):
    return a.reshape(-1, a.shape[-1])


def _scalar(i):
    return jnp.reshape(i, (1,)).astype(jnp.int32)


def chip_sum(g8, from_sibling, *, name):
    blk = g8.shape[1:]
    R, C = math.prod(blk[:-1]), blk[-1]
    tm = _tile(R, 512, 16)

    def body(c_ref, a_ref, b_ref, o_ref, ob_ref):
        s = a_ref[0, 0] + b_ref[0]
        o_ref[0] = s
        ob_ref[0] = s.astype(BF16)

    row = pl.BlockSpec((1, tm, C), lambda k, i, c: (k, i, 0))
    spec = pltpu.PrefetchScalarGridSpec(
        num_scalar_prefetch=1, grid=(4, R // tm),
        in_specs=[pl.BlockSpec((1, 1, tm, C), lambda k, i, c: (k, c[0], i, 0)), row], out_specs=[row, row])
    o, ob = pl.pallas_call(body, name=name, grid_spec=spec, out_shape=[SDS((4, R, C), F32), SDS((4, R, C), BF16)],
                           compiler_params=pltpu.CompilerParams(vmem_limit_bytes=VMEM_LIMIT),
                           )(_scalar(lax.axis_index("c")), g8.reshape(4, 2, R, C), from_sibling.reshape(4, R, C))
    return o.reshape((4,) + blk), ob.reshape((4,) + blk)


def adamw_sharded(w, m, v, sums, recv, layer, prev, *, name):
    blk = w.shape[1:]
    R, C = math.prod(blk[:-1]), blk[-1]
    tm = _tile(R, 256, 16)
    bc1 = 1.0 - ADAM_B1 ** ADAM_STEP
    bc2 = 1.0 - ADAM_B2 ** ADAM_STEP
    n_prev = 0 if prev is None else 4

    def body(chip_ref, w_ref, m_ref, v_ref, s_ref, r_ref, *rest):
        g_ref, d_ref, nm_ref, nv_ref = rest[n_prev:]
        g = s_ref[0] + r_ref[0].astype(F32) + r_ref[1].astype(F32) + r_ref[2].astype(F32)
        mm_ = ADAM_B1 * m_ref[0] + (1.0 - ADAM_B1) * g
        vv = ADAM_B2 * v_ref[0] + (1.0 - ADAM_B2) * (g * g)
        g_ref[0] = g
        nm_ref[0] = mm_
        nv_ref[0] = vv
        d_ref[0] = -ADAM_LR * ((mm_ / bc1) / (jnp.sqrt(vv / bc2) + ADAM_EPS) + ADAM_WD * w_ref[0])

    lay = pl.BlockSpec((1, tm, C), lambda i, c: (layer, i, 0))
    spec = pltpu.PrefetchScalarGridSpec(
        num_scalar_prefetch=1, grid=(R // tm,),
        in_specs=[lay, lay, lay, pl.BlockSpec((1, tm, C), lambda i, c: (c[0], i, 0)),
                  pl.BlockSpec((3, tm, C), lambda i, c: (0, i, 0))] + [ANY] * n_prev,
        out_specs=[lay] * 4)
    xi, yi, _ = _place()
    d3 = (w.shape[0], R, C)
    outs = pl.pallas_call(
        body, name=name, grid_spec=spec, out_shape=[SDS(d3, F32)] * 4,
        input_output_aliases={6 + i: i for i in range(n_prev)},
        compiler_params=pltpu.CompilerParams(vmem_limit_bytes=VMEM_LIMIT),
    )(_scalar(2 * xi + yi), w.reshape(d3), m.reshape(d3), v.reshape(d3), sums.reshape(4, R, C), recv.reshape(3, R, C),
      *([] if prev is None else prev))
    return list(outs)


def adamw(w, m, v, parts, *, name):
    shp = w.shape
    w2, m2, v2 = _rows2d(w), _rows2d(m), _rows2d(v)
    R, C = w2.shape
    p3 = [p.reshape(p.shape[0], R, C) for p in parts]
    tm = _tile(R, 256, 8)
    bc1 = 1.0 - ADAM_B1 ** ADAM_STEP
    bc2 = 1.0 - ADAM_B2 ** ADAM_STEP

    def body(w_ref, m_ref, v_ref, *refs):
        p_refs, (g_ref, d_ref, nm_ref, nv_ref) = refs[:len(p3)], refs[len(p3):]
        g = None
        for p_ref, p in zip(p_refs, p3):
            for k in range(p.shape[0]):
                term = p_ref[k].astype(F32)
                g = term if g is None else g + term
        mm_ = ADAM_B1 * m_ref[...] + (1.0 - ADAM_B1) * g
        vv = ADAM_B2 * v_ref[...] + (1.0 - ADAM_B2) * (g * g)
        g_ref[...] = g
        nm_ref[...] = mm_
        nv_ref[...] = vv
        d_ref[...] = -ADAM_LR * ((mm_ / bc1) / (jnp.sqrt(vv / bc2) + ADAM_EPS) + ADAM_WD * w_ref[...])

    blk = pl.BlockSpec((tm, C), lambda i: (i, 0))
    pspecs = [pl.BlockSpec((p.shape[0], tm, C), lambda i: (0, i, 0)) for p in p3]
    outs = _pc(body, name=name, grid=(R // tm,), in_specs=[blk, blk, blk] + pspecs,
               out_specs=[blk] * 4, out_shape=[SDS((R, C), F32)] * 4)(w2, m2, v2, *p3)
    return [o.reshape(shp) for o in outs]


def _place():
    return lax.axis_index("x"), lax.axis_index("y"), lax.axis_index("c")


def all_gather_many(xs, *, name):
    n = len(xs)

    def body(*refs):
        x_refs, o_refs = refs[:n], refs[n:2 * n]
        send_sems, recv_sems, local_sems = refs[2 * n:]
        x, y, c = _place()
        me, sibling = (x, y, c), (x, y, 1 - c)
        chips = [(1 - x, y), (x, 1 - y), (1 - x, 1 - y)]

        def rows(a, p):
            return o_refs[a].at[4 * p[0] + 2 * p[1] + p[2]]

        def copy(a, k, block, to, src=None):
            return pltpu.make_async_remote_copy(
                src_ref=rows(a, block) if src is None else src, dst_ref=rows(a, block),
                send_sem=send_sems.at[7 * a + k], recv_sem=recv_sems.at[7 * a + k], device_id=to, device_id_type=MESH)

        mine = [pltpu.make_async_copy(x_refs[a], rows(a, me), local_sems.at[a]) for a in range(n)]
        for cp in mine:
            cp.start()
        first = []
        for a in range(n):
            first.append(copy(a, 0, me, sibling, src=x_refs[a]))
            first += [copy(a, 1 + j, me, (*chip, c), src=x_refs[a]) for j, chip in enumerate(chips)]
        for cp in first:
            cp.start()
        passed = []
        for j, chip in enumerate(chips):
            for a in range(n):
                copy(a, 1 + j, (*chip, c), me).wait_recv()
                cp = copy(a, 4 + j, (*chip, c), sibling)
                cp.start()
                passed.append(cp)
        for a in range(n):
            copy(a, 0, sibling, me).wait_recv()
            for j, chip in enumerate(chips):
                copy(a, 4 + j, (*chip, 1 - c), me).wait_recv()
        for cp in first + passed:
            cp.wait_send()
        for cp in mine:
            cp.wait()

    return pl.pallas_call(
        body, name=name, in_specs=[ANY] * n, out_specs=[ANY] * n,
        out_shape=[SDS((N_DEV,) + a.shape, a.dtype) for a in xs],
        scratch_shapes=[pltpu.SemaphoreType.DMA((7 * n,)), pltpu.SemaphoreType.DMA((7 * n,)), pltpu.SemaphoreType.DMA((n,))],
    )(*xs)


def _stage(ins, out_shape, n_peers, copy_of):
    ins = list(ins)

    def copies(in_refs, out_refs, send_sems, recv_sems):
        place = _place()
        out = []
        for a in range(len(ins)):
            for k in range(n_peers):
                src, dst, peer = copy_of(in_refs[a], out_refs[a], k, place)
                out.append(pltpu.make_async_remote_copy(
                    src_ref=src, dst_ref=dst, send_sem=send_sems.at[n_peers * a + k], recv_sem=recv_sems.at[n_peers * a + k],
                    device_id=peer, device_id_type=MESH))
        return out

    return dict(ins=ins, out_shape=list(out_shape), sems=n_peers * len(ins), copies=copies)


def _other_chips(x, y):
    return [(1 - x, y), (x, 1 - y), (1 - x, 1 - y)]


def stage_gather_direct(blocks):
    def copy_of(src, dst, k, place):
        x, y, c = place
        peer = (x, y, 1 - c) if k == 0 else (*_other_chips(x, y)[k - 1], c)
        return src, dst.at[4 * x + 2 * y + c], peer

    return _stage(blocks, [SDS((N_DEV,) + b.shape, b.dtype) for b in blocks], 4, copy_of)


def stage_gather_forward(bufs):
    def copy_of(src, dst, k, place):
        x, y, c = place
        cx, cy = _other_chips(x, y)[k]
        slot = 4 * cx + 2 * cy + c
        return src.at[slot], dst.at[slot], (x, y, 1 - c)

    st = _stage(bufs, [SDS(b.shape, b.dtype) for b in bufs], 3, copy_of)
    st["alias"] = True
    return st


def stage_rs_sibling(g8s):
    def copy_of(src, dst, k, place):
        x, y, c = place
        return src.at[2 * k + (1 - c)], dst.at[k], (x, y, 1 - c)

    return _stage(g8s, [SDS((4,) + g.shape[1:], g.dtype) for g in g8s], 4, copy_of)


def stage_rs_chips(sums):
    def copy_of(src, dst, k, place):
        x, y, c = place
        chip = _other_chips(x, y)[k]
        return src.at[2 * chip[0] + chip[1]], dst.at[k], (*chip, c)

    return _stage(sums, [SDS((3,) + s.shape[1:], s.dtype) for s in sums], 3, copy_of)


def run_stage(stage, *, name):
    n_in, n_out = len(stage["ins"]), len(stage["out_shape"])

    def body(*refs):
        cps = stage["copies"](refs[:n_in], refs[n_in:n_in + n_out], refs[-2], refs[-1])
        for cp in cps:
            cp.start()
        for cp in cps:
            cp.wait()

    return pl.pallas_call(
        body, name=name, in_specs=[ANY] * n_in, out_specs=[ANY] * n_out, out_shape=stage["out_shape"],
        scratch_shapes=[pltpu.SemaphoreType.DMA((stage["sems"],)), pltpu.SemaphoreType.DMA((stage["sems"],))],
    )(*stage["ins"])


def with_own_block(buf, own):
    x, y, c = _place()
    return lax.dynamic_update_index_in_dim(buf, own, 4 * x + 2 * y + c, 0)


def _owner_major(full, axis):
    shp = full.shape
    r = full.reshape(shp[:axis] + (N_DEV, shp[axis] // N_DEV) + shp[axis + 1:])
    return jnp.moveaxis(r, axis, 0)


def _from_owner_major(g8, axis):
    r = jnp.moveaxis(g8, 0, axis)
    shp = r.shape
    return r.reshape(shp[:axis] + (shp[axis] * shp[axis + 1],) + shp[axis + 2:])


def _perm_w_in(w):
    z = jnp.zeros((w.shape[0], LANES), w.dtype)
    dt = jnp.pad(w[:, 2560:2576], ((0, 0), (0, LANES - SSD_HEADS)))
    kpe = jnp.pad(w[:, 3728:3760], ((0, 0), (PE_LO, LANES - PE_HI)))
    return jnp.concatenate([w[:, 0:1024], w[:, 1024:2560], w[:, 2576:3088], w[:, 3088:3472], z, w[:, 3472:3728], dt, kpe], axis=1)


def _unperm_w_in(g):
    return jnp.concatenate([g[:, Z0:Z0 + 1024], g[:, XBC0:XBC0 + 1536], g[:, DT0:DT0 + SSD_HEADS], g[:, U0:U0 + 512],
                            g[:, CQ0:CQ0 + 384], g[:, CKV0:CKV0 + 256], g[:, KPE0 + PE_LO:KPE0 + PE_HI]], axis=1)


def _perm_w_uq(w):
    return jnp.pad(w.reshape(MLA_QR, MLA_H, MLA_QK), ((0, 0), (0, 0), (0, LANES - MLA_QK))).reshape(MLA_QR, MLA_H * LANES)


def _unperm_w_uq(g):
    return g.reshape(MLA_QR, MLA_H, LANES)[:, :, :MLA_QK].reshape(MLA_QR, MLA_H * MLA_QK)


def _perm_w_ukv(w):
    w3 = w.reshape(MLA_KVR, MLA_H, MLA_NOPE + MLA_V)
    pad = ((0, 0), (0, 0), (0, LANES - MLA_NOPE))
    k = jnp.pad(w3[:, :, :MLA_NOPE], pad).reshape(MLA_KVR, MLA_H * LANES)
    v = jnp.pad(w3[:, :, MLA_NOPE:], pad).reshape(MLA_KVR, MLA_H * LANES)
    return jnp.concatenate([k, v], axis=1)


def _unperm_w_ukv(g):
    k = g[:, :MLA_H * LANES].reshape(MLA_KVR, MLA_H, LANES)[:, :, :MLA_NOPE]
    v = g[:, MLA_H * LANES:].reshape(MLA_KVR, MLA_H, LANES)[:, :, :MLA_V]
    return jnp.concatenate([k, v], axis=2).reshape(MLA_KVR, MLA_H * (MLA_NOPE + MLA_V))


def _perm_w_out(w):
    m = jnp.pad(w[SSD_W + POOL_W:].reshape(MLA_H, MLA_V, D), ((0, 0), (0, LANES - MLA_V), (0, 0))).reshape(MLA_H * LANES, D)
    return jnp.concatenate([w[:SSD_W + POOL_W], m], axis=0)


def _unperm_w_out(g):
    m = g[SSD_W + POOL_W:].reshape(MLA_H, LANES, D)[:, :MLA_V].reshape(MLA_H * MLA_V, D)
    return jnp.concatenate([g[:SSD_W + POOL_W], m], axis=0)


def _lane_pad(v):
    return jnp.pad(v.reshape(1, -1), ((0, 0), (0, LANES - v.shape[-1])))


SMALL = ("attn_norm", "ssd_conv_b", "ssd_dt_bias", "ssd_a_log", "ssd_d", "ssd_norm", "pool_w", "pool_scale",
         "mla_q_norm", "mla_kv_norm", "ffn_norm", "ffn_conv_b", "final_norm")
SHARDED = {"w_in": 2, "ssd_conv_w": 2, "mla_w_uq": 2, "mla_w_ukv": 2, "w_out": 1, "ffn_w_up": 2, "ffn_conv_w": 2,
           "ffn_w_down": 1}
ALL_W = ("attn_norm", "w_in", "ssd_conv_w", "ssd_conv_b", "ssd_dt_bias", "ssd_a_log", "ssd_d", "ssd_norm", "pool_w",
         "pool_scale", "mla_q_norm", "mla_w_uq", "mla_kv_norm", "mla_w_ukv", "w_out", "ffn_norm", "ffn_w_up",
         "ffn_conv_w", "ffn_conv_b", "ffn_w_down", "final_norm")


def _pack_small(d):
    rows, layout = [], []
    for k in SMALL:
        a = d[k].reshape(-1)
        n = a.shape[0]
        r = -(-n // LANES)
        rows.append(jnp.pad(a, (0, r * LANES - n)).reshape(r, LANES))
        layout.append((k, n, r, d[k].shape))
    buf = jnp.concatenate(rows, axis=0)
    pad = (-buf.shape[0]) % 8
    return jnp.pad(buf, ((0, pad), (0, 0))), layout


def _unpack_small(buf, layout):
    out, r0 = {}, 0
    for k, n, r, shp in layout:
        out[k] = buf[r0:r0 + r].reshape(-1)[:n].reshape(shp)
        r0 += r
    return out


def _layer_fwd(l, x, W, cs, sn, Bl, next_blocks=None):
    T = x.shape[0]
    S = T // Bl
    n = f"l{l}_"
    h = rms_fwd(x, W["attn_norm"], name=n + "attn_norm")
    proj = mm(h, W["w_in"], name=n + "w_in")
    proj3 = proj.reshape(Bl, S, PW)
    xbc3 = conv_silu_fwd(proj3, W["ssd_conv_w"], W["ssd_conv_b"], name=n + "ssd_conv")
    y3, hin = ssd_fwd(xbc3, proj3, W["ssd_dt_bias"], W["ssd_a_log"], W["ssd_d"], name=n + "ssd_scan")
    y = y3.reshape(T, SSD_W)
    y_ssd = gated_rms_fwd(y, proj, W["ssd_norm"], name=n + "ssd_gate_norm")
    y_pool = pool_fwd(proj3, W["pool_w"], W["pool_scale"], name=n + "pool").reshape(T, POOL_W)
    qn = rms_fwd(proj, W["mla_q_norm"], col0=CQ0, width=MLA_QR, name=n + "q_norm")
    kvn = rms_fwd(proj, W["mla_kv_norm"], col0=CKV0, width=MLA_KVR, name=n + "kv_norm")
    qt = mm(qn, W["mla_w_uq"], name=n + "w_uq")
    kvt = mm(kvn, W["mla_w_ukv"], name=n + "w_ukv")
    qr, kr, vr = mla_prep_fwd(qt, kvt, proj, cs, sn, name=n + "rope")
    direct = stage_gather_direct(next_blocks) if next_blocks is not None else None
    o, lse, lset = flash_fwd(qr, kr, vr, Bl, comm=direct, name=n + "attn")
    ycat = jnp.concatenate([y_ssd, y_pool, o], axis=1)
    x1 = mm(ycat, W["w_out"], add=x, name=n + "w_out")
    h2 = rms_fwd(x1, W["ffn_norm"], name=n + "ffn_norm")
    forward = stage_gather_forward(direct["result"]) if direct else None
    pre = mm(h2, W["ffn_w_up"], comm=forward, name=n + "w_up")
    gathered = [with_own_block(buf, b) for buf, b in zip(forward["result"], next_blocks)] if direct else None
    pre3 = pre.reshape(Bl, S, 2 * DFF)
    act = ffn_act_fwd(pre3, W["ffn_conv_w"], W["ffn_conv_b"], name=n + "ffn_act").reshape(T, DFF)
    x2 = mm(act, W["ffn_w_down"], add=x1, name=n + "w_down")
    saved = dict(x=x, h=h, proj=proj, xbc3=xbc3, hin=hin, y=y, qn=qn, kvn=kvn, vr=vr, qr=qr, kr=kr, o=o, lse=lse, lset=lset,
                 ycat=ycat, x1=x1, h2=h2, pre3=pre3, act=act)
    return x2, saved, gathered


EARLY = ("ffn_w_up", "ffn_conv_w", "ffn_w_down", "w_out")
LATE = tuple(k for k in SHARDED if k not in EARLY)
_UNPERM = {"w_in": _unperm_w_in, "mla_w_uq": _unperm_w_uq, "mla_w_ukv": _unperm_w_ukv, "w_out": _unperm_w_out}


def _by_owner(g, keys):
    return [_owner_major(_UNPERM.get(k, lambda t: t)(g[k]), SHARDED[k] - 1) for k in keys]


def _chip_sums(g8s, from_sibling, tag):
    return [chip_sum(g8, r, name=f"{tag}{a}") for a, (g8, r) in enumerate(zip(g8s, from_sibling))]


def _layer_bwd(l, dx2, dx2b, W, sv, cs, sn, Bl, later_g8=None):
    T = dx2.shape[0]
    S = T // Bl
    n = f"l{l}_b_"
    g = {}
    g["ffn_w_down"] = mm(sv["act"], dx2b, ta=True, name=n + "dw_down")
    dact = mm(dx2b, W["ffn_w_down"], tb=True, name=n + "dact")
    to_sibling = stage_rs_sibling(later_g8) if later_g8 is not None else None
    dpg, dpv, dwg, dwv, dbg, dbv = ffn_act_bwd(sv["pre3"], W["ffn_conv_w"], W["ffn_conv_b"], dact.reshape(Bl, S, DFF),
                                               comm=to_sibling, name=n + "ffn_act")
    to_chips = sums = None
    if to_sibling:
        sums = _chip_sums(later_g8, to_sibling["result"], n + "rs_late_add")
        to_chips = stage_rs_chips([sb for _, sb in sums])
    g["ffn_conv_w"] = jnp.concatenate([dwg, dwv], axis=1)
    g["ffn_conv_b"] = jnp.concatenate([dbg, dbv], axis=1)
    dpg, dpv = dpg.reshape(T, DFF), dpv.reshape(T, DFF)
    g["ffn_w_up"] = jnp.concatenate([mm(sv["h2"], dpg, ta=True, name=n + "dw_up_g"),
                                     mm(sv["h2"], dpv, ta=True, name=n + "dw_up_v")], axis=1)
    dh2 = mm(dpg, W["ffn_w_up"], tb=True, name=n + "dh2_g")
    dh2 = mm(dpv, W["ffn_w_up"], tb=True, b_k0=DFF, add=dh2, name=n + "dh2_v")
    dx1, dx1b, g["ffn_norm"] = rms_bwd(sv["x1"], W["ffn_norm"], dh2, add=dx2, name=n + "ffn_norm")
    g["w_out"] = mm(sv["ycat"], dx1b, ta=True, name=n + "dw_out")
    dycat = mm(dx1b, W["w_out"], tb=True, name=n + "dycat")
    proj, proj3 = sv["proj"], sv["proj"].reshape(Bl, S, PW)
    dy, dz, g["ssd_norm"] = gated_rms_bwd(sv["y"], proj, W["ssd_norm"], dycat, name=n + "ssd_gate_norm")
    dxa, ddt, dpar, dd = ssd_bwd(sv["xbc3"], proj3, sv["hin"], dy.reshape(Bl, S, SSD_W), W["ssd_dt_bias"], W["ssd_a_log"],
                                 W["ssd_d"], comm=to_chips, name=n + "ssd_scan")
    reduced_late = ([s32 for s32, _ in sums], to_chips["result"]) if to_chips else None
    g["ssd_dt_bias"] = dpar[0, :SSD_HEADS]
    g["ssd_a_log"] = dpar[1, :SSD_HEADS]
    g["ssd_d"] = dd[:, :2].reshape(SSD_HEADS)
    dxbc, g["ssd_conv_w"], g["ssd_conv_b"] = conv_silu_bwd(proj3, W["ssd_conv_w"], W["ssd_conv_b"], dxa, name=n + "ssd_conv")
    du, g["pool_w"], g["pool_scale"] = pool_bwd(proj3, W["pool_w"], W["pool_scale"], dycat.reshape(Bl, S, YCAT), name=n + "pool")
    early_g8 = _by_owner(g, EARLY)
    early_sibling = stage_rs_sibling(early_g8)
    dqr, dlt = flash_bwd_dq(sv["qr"], sv["kr"], sv["vr"], sv["o"], sv["lse"], dycat, Bl, comm=early_sibling, name=n + "attn_dq")
    early_sums = _chip_sums(early_g8, early_sibling["result"], n + "rs_early_add")
    early_chips = stage_rs_chips([sb for _, sb in early_sums])
    dkr, dv = flash_bwd_dkv(sv["qr"], sv["kr"], sv["vr"], sv["lset"], dlt, dycat, Bl, comm=early_chips, name=n + "attn_dkv")
    reduced_early = ([s32 for s32, _ in early_sums], early_chips["result"])
    dqt, dkt, dkpe = mla_prep_bwd(dqr, dkr, cs, sn, name=n + "rope")
    g["mla_w_ukv"] = jnp.concatenate([mm(sv["kvn"], dkt, ta=True, name=n + "dw_uk"),
                                      mm(sv["kvn"], dv, ta=True, name=n + "dw_uv")], axis=1)
    dkvn = mm(dkt, W["mla_w_ukv"], tb=True, name=n + "dkvn_k")
    dkvn = mm(dv, W["mla_w_ukv"], tb=True, b_k0=MLA_H * LANES, add=dkvn, name=n + "dkvn_v")
    g["mla_w_uq"] = mm(sv["qn"], dqt, ta=True, name=n + "dw_uq")
    dqn = mm(dqt, W["mla_w_uq"], tb=True, name=n + "dqn")
    dcq, g["mla_q_norm"] = rms_bwd(proj, W["mla_q_norm"], dqn, col0=CQ0, width=MLA_QR, name=n + "q_norm")
    dckv, g["mla_kv_norm"] = rms_bwd(proj, W["mla_kv_norm"], dkvn, col0=CKV0, width=MLA_KVR, name=n + "kv_norm")
    dproj = jnp.concatenate([dz, dxbc.reshape(T, SSD_XBC), du.reshape(T, POOL_W), dcq, jnp.zeros((T, LANES), BF16), dckv,
                             ddt.reshape(T, LANES), dkpe], axis=1)
    g["w_in"] = mm(sv["h"], dproj, ta=True, name=n + "dw_in")
    dh = mm(dproj, W["w_in"], tb=True, name=n + "dh")
    dx, dxb, g["attn_norm"] = rms_bwd(sv["x"], W["attn_norm"], dh, add=dx1, name=n + "attn_norm")
    return dx, dxb, g, reduced_late, reduced_early


def kernel(x, positions, attn_norm, w_in, ssd_conv_w, ssd_conv_b, ssd_dt_bias, ssd_a_log, ssd_d, ssd_norm, pool_w, pool_scale, mla_q_norm, mla_w_uq, mla_kv_norm, mla_w_ukv, w_out, ffn_norm, ffn_w_up, ffn_conv_w, ffn_conv_b, ffn_w_down, final_norm, loss_target, m_attn_norm, m_w_in, m_ssd_conv_w, m_ssd_conv_b, m_ssd_dt_bias, m_ssd_a_log, m_ssd_d, m_ssd_norm, m_pool_w, m_pool_scale, m_mla_q_norm, m_mla_w_uq, m_mla_kv_norm, m_mla_w_ukv, m_w_out, m_ffn_norm, m_ffn_w_up, m_ffn_conv_w, m_ffn_conv_b, m_ffn_w_down, m_final_norm, v_attn_norm, v_w_in, v_ssd_conv_w, v_ssd_conv_b, v_ssd_dt_bias, v_ssd_a_log, v_ssd_d, v_ssd_norm, v_pool_w, v_pool_scale, v_mla_q_norm, v_mla_w_uq, v_mla_kv_norm, v_mla_w_ukv, v_w_out, v_ffn_norm, v_ffn_w_up, v_ffn_conv_w, v_ffn_conv_b, v_ffn_w_down, v_final_norm):
    a = locals()
    Wt = {k: a[k] for k in ALL_W}
    Mo = {k: a["m_" + k] for k in ALL_W}
    Vo = {k: a["v_" + k] for k in ALL_W}
    Bl, S, _ = x.shape
    T = Bl * S

    names = list(SHARDED)
    conv = ("ssd_conv_w", "ffn_conv_w")

    def blocks_of(l):
        return [Wt[k][l] if k in conv else Wt[k][l].astype(BF16) for k in names]

    def layer_weights(l, gathered):
        full = {k: _from_owner_major(g8, SHARDED[k] - 1) for k, g8 in zip(names, gathered)}
        return {
            "attn_norm": attn_norm[l].reshape(1, D), "w_in": _perm_w_in(full["w_in"]),
            "ssd_conv_w": full["ssd_conv_w"], "ssd_conv_b": ssd_conv_b[l].reshape(1, SSD_XBC),
            "ssd_dt_bias": _lane_pad(ssd_dt_bias[l]), "ssd_a_log": _lane_pad(ssd_a_log[l]),
            "ssd_d": jnp.repeat(ssd_d[l].reshape(NPAIR, 2), SSD_P, axis=1), "ssd_norm": ssd_norm[l].reshape(1, SSD_W),
            "pool_w": pool_w[l].astype(BF16), "pool_scale": pool_scale[l].reshape(1, POOL_W),
            "mla_q_norm": mla_q_norm[l].reshape(1, MLA_QR), "mla_w_uq": _perm_w_uq(full["mla_w_uq"]),
            "mla_kv_norm": mla_kv_norm[l].reshape(1, MLA_KVR), "mla_w_ukv": _perm_w_ukv(full["mla_w_ukv"]),
            "w_out": _perm_w_out(full["w_out"]), "ffn_norm": ffn_norm[l].reshape(1, D),
            "ffn_w_up": full["ffn_w_up"], "ffn_conv_w": full["ffn_conv_w"],
            "ffn_conv_b": ffn_conv_b[l].reshape(1, 2 * DFF), "ffn_w_down": full["ffn_w_down"]}

    pos = positions.astype(F32).reshape(T, 1)
    inv_freq = ROPE_THETA ** (-jnp.arange(0, MLA_ROPE, 2, dtype=F32) / MLA_ROPE)
    invf = jnp.pad(jnp.concatenate([inv_freq, inv_freq]), (PE_LO, LANES - PE_HI)).reshape(1, LANES)
    cs, sn = rope_tables(pos, invf, name="rope_tables")

    layers = [layer_weights(0, all_gather_many(blocks_of(0), name="gather_weights_l0"))]
    xc = x.reshape(T, D)
    saved = []
    for l in range(DEPTH):
        xc, sv, gathered = _layer_fwd(l, xc, layers[l], cs, sn, Bl, next_blocks=blocks_of(l + 1) if l + 1 < DEPTH else None)
        saved.append(sv)
        if gathered is not None:
            layers.append(layer_weights(l + 1, gathered))
    dx, dxb, g_final, loss_part = final_loss(xc, final_norm.reshape(1, D), loss_target.reshape(T, D), name="final_loss")

    grads = [None] * DEPTH
    reduced = {}

    def record(l, keys, red):
        for a, k in enumerate(keys):
            reduced[(l, k)] = (red[0][a], red[1][a])

    later_g8 = None
    for l in reversed(range(DEPTH)):
        dx, dxb, grads[l], red_late, red_early = _layer_bwd(l, dx, dxb, layers[l], saved[l], cs, sn, Bl, later_g8=later_g8)
        if red_late is not None:
            record(l + 1, LATE, red_late)
        record(l, EARLY, red_early)
        later_g8 = _by_owner(grads[l], LATE)
    loss = lax.psum(loss_part[0, 0], AXES)
    sums = _chip_sums(later_g8, run_stage(stage_rs_sibling(later_g8), name="rs_sibling_l0"), "rs_add_l0_")
    record(0, LATE, ([s32 for s32, _ in sums], run_stage(stage_rs_chips([sb for _, sb in sums]), name="rs_chips_l0")))

    out_g, out_d, out_m, out_v = {}, {}, {}, {}
    for k in names:
        outs = None
        for l in reversed(range(DEPTH)):
            s32, recv = reduced[(l, k)]
            outs = adamw_sharded(Wt[k], Mo[k], Vo[k], s32, recv, l, outs, name=f"adamw_l{l}_{k}")
        out_g[k], out_d[k], out_m[k], out_v[k] = (o.reshape(Wt[k].shape) for o in outs)

    part = {k: g_final.reshape(D) if k == "final_norm" else
            jnp.stack([grads[l][k].reshape(Wt[k].shape[1:]) for l in range(DEPTH)]) for k in SMALL}
    pg, layout = _pack_small(part)
    pw, _ = _pack_small(Wt)
    pm, _ = _pack_small(Mo)
    pv, _ = _pack_small(Vo)
    (pg8,) = all_gather_many([pg], name="gather_small_grads")
    sg, sd, sm, sv_ = adamw(pw, pm, pv, [pg8], name="adamw_small")
    for dst, buf in ((out_g, sg), (out_d, sd), (out_m, sm), (out_v, sv_)):
        dst.update(_unpack_small(buf, layout))

    return (loss, dx.reshape(Bl, S, D), *[out_g[k] for k in ALL_W], *[out_d[k] for k in ALL_W],
            *[out_m[k] for k in ALL_W], *[out_v[k] for k in ALL_W])
```

```python
import functools
import math

import jax
import jax.numpy as jnp
from jax import lax
from jax.experimental import pallas as pl
from jax.experimental.pallas import tpu as pltpu

F32, BF16 = jnp.float32, jnp.bfloat16
SDS = jax.ShapeDtypeStruct
MESH = pl.DeviceIdType.MESH
AXES = ("x", "y", "c")
N_DEV = 8

D = 1024
EPS = 1e-6
SSD_HEADS, SSD_P, SSD_W, SSD_G, SSD_N, SSD_K, SSD_L, SSD_XBC = 16, 64, 1024, 2, 128, 4, 128, 1536
POOL_G, POOL_D, POOL_W, POOL_WIN = 4, 128, 512, (2, 4, 8, 16)
MLA_H, MLA_QR, MLA_KVR, MLA_NOPE, MLA_ROPE, MLA_V, MLA_QK = 8, 384, 256, 64, 32, 64, 96
ROPE_THETA = 10000.0
MIX = 2048
DFF, FFN_K = 2816, 3
DEPTH = 2
ADAM_LR, ADAM_B1, ADAM_B2, ADAM_EPS, ADAM_WD, ADAM_STEP = 0.001, 0.9, 0.999, 1e-08, 0.01, 10

Z0, XBC0, U0, CQ0, CKV0, DT0, KPE0, PW = 0, 1024, 2560, 3072, 3584, 3840, 3968, 4096
LANES = 128
YCAT = SSD_W + POOL_W + MLA_H * LANES
NEG = -1e30
VMEM_LIMIT = 56 * 1024 * 1024
MM_ROW_TILE, MM_LANE_TILE, MM_FULL_K = 1024, 1408, 2816


def _tile(n, pref, mult):
    if n <= pref:
        return n
    for d in range(pref, 0, -mult):
        if d % mult == 0 and n % d == 0:
            return d
    return n


def _dg(a, b, ca, cb, prec=None):
    return lax.dot_general(a, b, (((ca,), (cb,)), ((), ())), preferred_element_type=F32, precision=prec)


def _nn(a, b):
    return _dg(a.astype(BF16), b.astype(BF16), 1, 0)


def _nt(a, b):
    return _dg(a.astype(BF16), b.astype(BF16), 1, 1)


def _tn(a, b):
    return _dg(a.astype(BF16), b.astype(BF16), 0, 0)


def _sig(x):
    return jax.nn.sigmoid(x)


def _silu(x):
    return x * _sig(x)


def _dsilu(x):
    s = _sig(x)
    return s * (1.0 + x * (1.0 - s))


ANY = pl.BlockSpec(memory_space=pl.ANY)


def _pc(body, *, name, grid, in_specs, out_specs, out_shape, scratch=(), comm=None):
    params = pltpu.CompilerParams(vmem_limit_bytes=VMEM_LIMIT)
    if comm is None:
        return pl.pallas_call(body, name=name, grid=grid, in_specs=in_specs, out_specs=out_specs, out_shape=out_shape,
                              scratch_shapes=list(scratch), compiler_params=params)
    single = not isinstance(out_shape, (list, tuple))
    o_specs = [out_specs] if single else list(out_specs)
    o_shape = [out_shape] if single else list(out_shape)
    ni, no, ns = len(in_specs), len(o_specs), len(scratch)
    nci, nco = len(comm["ins"]), len(comm["out_shape"])

    def fused(*refs):
        ins, cins = refs[:ni], refs[ni:ni + nci]
        outs, couts = refs[ni + nci:ni + nci + no], refs[ni + nci + no:ni + nci + no + nco]
        scr = refs[ni + nci + no + nco:ni + nci + no + nco + ns]
        send_sems, recv_sems = refs[-2:]
        copies = comm["copies"](cins, couts, send_sems, recv_sems)
        first = functools.reduce(jnp.logical_and, [pl.program_id(d) == 0 for d in range(len(grid))])
        last = functools.reduce(jnp.logical_and, [pl.program_id(d) == grid[d] - 1 for d in range(len(grid))])

        @pl.when(first)
        def _():
            for cp in copies:
                cp.start()

        body(*ins, *outs, *scr)

        @pl.when(last)
        def _():
            for cp in copies:
                cp.wait()

    call = pl.pallas_call(
        fused, name=name, grid=grid, in_specs=list(in_specs) + [ANY] * nci, out_specs=o_specs + [ANY] * nco,
        out_shape=o_shape + list(comm["out_shape"]),
        scratch_shapes=list(scratch) + [pltpu.SemaphoreType.DMA((comm["sems"],)), pltpu.SemaphoreType.DMA((comm["sems"],))],
        input_output_aliases={ni + a: no + a for a in range(nci)} if comm.get("alias") else {},
        compiler_params=params)

    def run(*args):
        res = call(*args, *comm["ins"])
        comm["result"] = list(res[no:])
        return res[0] if single else list(res[:no])

    return run


def _rsum(x):
    return jnp.sum(x, axis=1, keepdims=True)


def _csum(x):
    return jnp.sum(x, axis=0, keepdims=True)


def mm(a, b, *, ta=False, tb=False, add=None, out_dtype=F32, b_k0=0, comm=None, name):
    M, K = (a.shape[1], a.shape[0]) if ta else a.shape
    N = b.shape[0] if tb else b.shape[1]
    assert tb or b_k0 == 0
    tm = _tile(M, MM_LANE_TILE, LANES) if ta else _tile(M, MM_ROW_TILE, 8)
    tn = _tile(N, MM_LANE_TILE, LANES)
    if ta:
        tk = _tile(K, MM_ROW_TILE, 8)
    else:
        tk = K if K <= MM_FULL_K else _tile(K, 2048, LANES)
    nk = K // tk

    def body(*refs):
        if add is None:
            a_ref, b_ref, o_ref = refs[:3]
        else:
            a_ref, b_ref, add_ref, o_ref = refs[:4]
        part = _dg(a_ref[...].astype(BF16), b_ref[...].astype(BF16), 0 if ta else 1, 1 if tb else 0)

        def finish(r):
            if add is not None:
                r = r + add_ref[...].astype(F32)
            o_ref[...] = r.astype(out_dtype)

        if nk == 1:
            finish(part)
            return
        acc = refs[-1]
        k = pl.program_id(2)

        @pl.when(k == 0)
        def _():
            acc[...] = part

        @pl.when(k > 0)
        def _():
            acc[...] += part

        @pl.when(k == nk - 1)
        def _():
            finish(acc[...])

    a_spec = pl.BlockSpec((tk, tm), lambda i, j, k: (k, i)) if ta else pl.BlockSpec((tm, tk), lambda i, j, k: (i, k))
    assert b_k0 % tk == 0
    kb0 = b_k0 // tk
    b_spec = pl.BlockSpec((tn, tk), lambda i, j, k: (j, kb0 + k)) if tb else pl.BlockSpec((tk, tn), lambda i, j, k: (k, j))
    o_spec = pl.BlockSpec((tm, tn), lambda i, j, k: (i, j))
    ins, specs = [a, b], [a_spec, b_spec]
    if add is not None:
        ins.append(add)
        specs.append(o_spec)
    return _pc(body, name=name, grid=(M // tm, N // tn, nk), in_specs=specs, out_specs=o_spec,
               out_shape=SDS((M, N), out_dtype), scratch=[pltpu.VMEM((tm, tn), F32)] if nk > 1 else [], comm=comm)(*ins)


def rms_fwd(x, g, *, col0=0, width=None, name):
    T = x.shape[0]
    W = width or x.shape[1]
    tm = _tile(T, 512, 8)

    def body(x_ref, g_ref, o_ref):
        v = x_ref[...]
        r = lax.rsqrt(jnp.mean(v * v, axis=1, keepdims=True) + EPS)
        o_ref[...] = ((v * r) * g_ref[...]).astype(BF16)

    return _pc(body, name=name, grid=(T // tm,),
               in_specs=[pl.BlockSpec((tm, W), lambda i: (i, col0 // W)), pl.BlockSpec((1, W), lambda i: (0, 0))],
               out_specs=pl.BlockSpec((tm, W), lambda i: (i, 0)), out_shape=SDS((T, W), BF16))(x, g)


def rms_bwd(x, g, dh, *, col0=0, width=None, add=None, name):
    T = x.shape[0]
    W = width or x.shape[1]
    tm = _tile(T, 512, 8)

    def body(*refs):
        if add is None:
            x_ref, g_ref, dh_ref, dx_ref, dg_ref = refs
        else:
            x_ref, g_ref, dh_ref, add_ref, dx_ref, dxb_ref, dg_ref = refs
        v = x_ref[...]
        r = lax.rsqrt(jnp.mean(v * v, axis=1, keepdims=True) + EPS)
        xh = v * r
        d = dh_ref[...].astype(F32)
        dxh = d * g_ref[...]
        dx = r * (dxh - xh * jnp.mean(dxh * xh, axis=1, keepdims=True))
        if add is not None:
            dx = dx + add_ref[...]
            dxb_ref[...] = dx.astype(BF16)
        dx_ref[...] = dx.astype(dx_ref.dtype)

        @pl.when(pl.program_id(0) == 0)
        def _():
            dg_ref[...] = jnp.zeros_like(dg_ref)

        dg_ref[...] += _csum(d * xh)

    row = pl.BlockSpec((tm, W), lambda i: (i, 0))
    vec = pl.BlockSpec((1, W), lambda i: (0, 0))
    ins = [x, g, dh] + ([] if add is None else [add])
    specs = [pl.BlockSpec((tm, W), lambda i: (i, col0 // W)), vec, row] + ([] if add is None else [row])
    if add is None:
        return _pc(body, name=name, grid=(T // tm,), in_specs=specs, out_specs=[row, vec],
                   out_shape=[SDS((T, W), BF16), SDS((1, W), F32)])(*ins)
    return _pc(body, name=name, grid=(T // tm,), in_specs=specs, out_specs=[row, row, vec],
               out_shape=[SDS((T, W), F32), SDS((T, W), BF16), SDS((1, W), F32)])(*ins)


def gated_rms_fwd(y, proj, g, *, name):
    T = y.shape[0]
    tm = _tile(T, 512, 8)

    def body(y_ref, z_ref, g_ref, o_ref):
        v = y_ref[...] * _silu(z_ref[...])
        r = lax.rsqrt(jnp.mean(v * v, axis=1, keepdims=True) + EPS)
        o_ref[...] = ((v * r) * g_ref[...]).astype(BF16)

    row = pl.BlockSpec((tm, SSD_W), lambda i: (i, 0))
    return _pc(body, name=name, grid=(T // tm,), in_specs=[row, row, pl.BlockSpec((1, SSD_W), lambda i: (0, 0))],
               out_specs=row, out_shape=SDS((T, SSD_W), BF16))(y, proj, g)


def gated_rms_bwd(y, proj, g, dycat, *, name):
    T = y.shape[0]
    tm = _tile(T, 512, 8)

    def body(y_ref, z_ref, g_ref, d_ref, dy_ref, dz_ref, dg_ref):
        yv, z = y_ref[...], z_ref[...]
        sz = _silu(z)
        v = yv * sz
        r = lax.rsqrt(jnp.mean(v * v, axis=1, keepdims=True) + EPS)
        vh = v * r
        d = d_ref[...]
        dvh = d * g_ref[...]
        dv = r * (dvh - vh * jnp.mean(dvh * vh, axis=1, keepdims=True))
        dy_ref[...] = dv * sz
        dz_ref[...] = (dv * yv * _dsilu(z)).astype(BF16)

        @pl.when(pl.program_id(0) == 0)
        def _():
            dg_ref[...] = jnp.zeros_like(dg_ref)

        dg_ref[...] += _csum(d * vh)

    row = pl.BlockSpec((tm, SSD_W), lambda i: (i, 0))
    vec = pl.BlockSpec((1, SSD_W), lambda i: (0, 0))
    return _pc(body, name=name, grid=(T // tm,), in_specs=[row, row, vec, row], out_specs=[row, row, vec],
               out_shape=[SDS((T, SSD_W), F32), SDS((T, SSD_W), BF16), SDS((1, SSD_W), F32)])(y, proj, g, dycat)


def final_loss(x, g, tgt, *, name):
    T = x.shape[0]
    tm = _tile(T, 512, 8)

    def body(x_ref, g_ref, t_ref, dx_ref, dxb_ref, dg_ref, l_ref):
        v = x_ref[...]
        gg = g_ref[...]
        r = lax.rsqrt(jnp.mean(v * v, axis=1, keepdims=True) + EPS)
        xh = v * r
        err = xh * gg - t_ref[...]
        part = 0.5 * _csum(jnp.mean(err * err, axis=1, keepdims=True))
        d = err * (1.0 / D)
        dxh = d * gg
        dx = r * (dxh - xh * jnp.mean(dxh * xh, axis=1, keepdims=True))
        dx_ref[...] = dx
        dxb_ref[...] = dx.astype(BF16)

        @pl.when(pl.program_id(0) == 0)
        def _():
            dg_ref[...] = jnp.zeros_like(dg_ref)
            l_ref[...] = jnp.zeros_like(l_ref)

        dg_ref[...] += _csum(d * xh)
        l_ref[...] += jnp.broadcast_to(part, (1, LANES))

    row = pl.BlockSpec((tm, D), lambda i: (i, 0))
    vec = pl.BlockSpec((1, D), lambda i: (0, 0))
    return _pc(body, name=name, grid=(T // tm,), in_specs=[row, vec, row],
               out_specs=[row, row, vec, pl.BlockSpec((1, LANES), lambda i: (0, 0))],
               out_shape=[SDS((T, D), F32), SDS((T, D), BF16), SDS((1, D), F32), SDS((1, LANES), F32)])(x, g, tgt)


HALO = 8


def _prev_map(ts, col):
    return lambda b, i, j: (b, jnp.maximum(i * (ts // HALO) - 1, 0), col(j))


def _next_map(ts, n_halo_blocks, col):
    return lambda b, i, j: (b, jnp.minimum((i + 1) * (ts // HALO), n_halo_blocks - 1), col(j))


def _row_chunks(ts, rows):
    rows = min(rows, ts)
    return [(r, rows) for r in range(0, ts, rows)]


def _conv_rows(ext, w_ref, b_ref, r0, n, K):
    win = ext[r0:r0 + HALO + n, :]
    taps = [pltpu.roll(win, K - 1 - k, 0)[HALO:HALO + n] if k < K - 1 else win[HALO:HALO + n] for k in range(K)]
    acc = b_ref[...] + w_ref[0:1, :] * taps[0]
    for k in range(1, K):
        acc = acc + w_ref[k:k + 1, :] * taps[k]
    return acc, taps


def _conv_t_rows(ext2, w_ref, r0, n, K):
    win = ext2[r0:r0 + n + HALO, :]
    dx = w_ref[K - 1:K, :] * win[0:n]
    for k in range(K - 1):
        dx = dx + w_ref[k:k + 1, :] * pltpu.roll(win, n + HALO - (K - 1 - k), 0)[0:n]
    return dx


def _sum8(x):
    acc = x[0:8]
    for r in range(8, x.shape[0], 8):
        acc = acc + x[r:r + 8]
    return acc


def conv_silu_fwd(proj3, w, b, *, name):
    Bl, S, _ = proj3.shape
    C, K = SSD_XBC, SSD_K
    ts, tc = _tile(S, 512, 8), 512
    c0 = XBC0 // tc

    def body(xp_ref, x_ref, w_ref, b_ref, o_ref, ext):
        i = pl.program_id(1)
        ext[0:HALO, :] = jnp.where(i > 0, xp_ref[0], 0.0)
        ext[HALO:HALO + ts, :] = x_ref[0]
        for r0, n in _row_chunks(ts, 32):
            acc, _ = _conv_rows(ext, w_ref, b_ref, r0, n, K)
            o_ref[0, r0:r0 + n, :] = _silu(acc)

    return _pc(body, name=name, grid=(Bl, S // ts, C // tc),
               in_specs=[pl.BlockSpec((1, HALO, tc), _prev_map(ts, lambda j: c0 + j)),
                         pl.BlockSpec((1, ts, tc), lambda b, i, j: (b, i, c0 + j)),
                         pl.BlockSpec((K, tc), lambda b, i, j: (0, j)),
                         pl.BlockSpec((1, tc), lambda b, i, j: (0, j))],
               out_specs=pl.BlockSpec((1, ts, tc), lambda b, i, j: (b, i, j)),
               out_shape=SDS((Bl, S, C), F32), scratch=[pltpu.VMEM((HALO + ts, tc), F32)])(proj3, proj3, w, b)


def conv_silu_bwd(proj3, w, b, dact, *, name):
    Bl, S, _ = proj3.shape
    C, K = SSD_XBC, SSD_K
    ts, tc = _tile(S, 512, 8), 512
    c0 = XBC0 // tc
    ns = S // ts

    def body(xp_ref, x_ref, xn_ref, d_ref, dn_ref, w_ref, b_ref, dx_ref, dw_ref, db_ref, ext, ext2):
        bb, i = pl.program_id(1), pl.program_id(2)
        last = i == ns - 1
        ext[0:HALO, :] = jnp.where(i > 0, xp_ref[0], 0.0)
        ext[HALO:HALO + ts, :] = x_ref[0]
        ext[HALO + ts:2 * HALO + ts, :] = jnp.where(last, 0.0, xn_ref[0])
        dw = [jnp.zeros((8, tc), F32) for _ in range(K)]
        db = jnp.zeros((8, tc), F32)
        for r0, n in _row_chunks(ts, 16) + [(ts, HALO)]:
            acc, taps = _conv_rows(ext, w_ref, b_ref, r0, n, K)
            d = d_ref[0, r0:r0 + n, :] if r0 < ts else jnp.where(last, 0.0, dn_ref[0])
            du = d * _dsilu(acc)
            ext2[r0:r0 + n, :] = du
            if r0 < ts:
                dw = [a + _sum8(du * t) for a, t in zip(dw, taps)]
                db = db + _sum8(du)
        for r0, n in _row_chunks(ts, 32):
            dx_ref[0, r0:r0 + n, :] = _conv_t_rows(ext2, w_ref, r0, n, K).astype(BF16)

        @pl.when((bb == 0) & (i == 0))
        def _():
            dw_ref[...] = jnp.zeros_like(dw_ref)
            db_ref[...] = jnp.zeros_like(db_ref)

        for k in range(K):
            dw_ref[k:k + 1, :] += _csum(dw[k])
        db_ref[...] += _csum(db)

    nhb = S // HALO
    cx = lambda j: c0 + j
    cj = lambda j: j
    return _pc(body, name=name, grid=(C // tc, Bl, ns),
               in_specs=[pl.BlockSpec((1, HALO, tc), lambda j, b, i: _prev_map(ts, cx)(b, i, j)),
                         pl.BlockSpec((1, ts, tc), lambda j, b, i: (b, i, c0 + j)),
                         pl.BlockSpec((1, HALO, tc), lambda j, b, i: _next_map(ts, nhb, cx)(b, i, j)),
                         pl.BlockSpec((1, ts, tc), lambda j, b, i: (b, i, j)),
                         pl.BlockSpec((1, HALO, tc), lambda j, b, i: _next_map(ts, nhb, cj)(b, i, j)),
                         pl.BlockSpec((K, tc), lambda j, b, i: (0, j)),
                         pl.BlockSpec((1, tc), lambda j, b, i: (0, j))],
               out_specs=[pl.BlockSpec((1, ts, tc), lambda j, b, i: (b, i, j)),
                          pl.BlockSpec((K, tc), lambda j, b, i: (0, j)),
                          pl.BlockSpec((1, tc), lambda j, b, i: (0, j))],
               out_shape=[SDS((Bl, S, C), BF16), SDS((K, C), F32), SDS((1, C), F32)],
               scratch=[pltpu.VMEM((2 * HALO + ts, tc), F32), pltpu.VMEM((HALO + ts, tc), F32)],
               )(proj3, proj3, proj3, dact, dact, w, b)


def ffn_act_fwd(pre3, w, b, *, name):
    Bl, S, _ = pre3.shape
    K = FFN_K
    ts, tc = _tile(S, 512, 8), 256
    nj = DFF // tc

    def body(gp_ref, g_ref, vp_ref, v_ref, wg_ref, wv_ref, bg_ref, bv_ref, o_ref, eg, ev):
        i = pl.program_id(1)
        for p_ref, m_ref, ext in ((gp_ref, g_ref, eg), (vp_ref, v_ref, ev)):
            ext[0:HALO, :] = jnp.where(i > 0, p_ref[0], 0.0)
            ext[HALO:HALO + ts, :] = m_ref[0]
        for r0, n in _row_chunks(ts, 64):
            ug, _ = _conv_rows(eg, wg_ref, bg_ref, r0, n, K)
            uv, _ = _conv_rows(ev, wv_ref, bv_ref, r0, n, K)
            o_ref[0, r0:r0 + n, :] = (_silu(ug) * uv).astype(BF16)

    main = lambda off: pl.BlockSpec((1, ts, tc), lambda b, i, j: (b, i, off + j))
    prev = lambda off: pl.BlockSpec((1, HALO, tc), _prev_map(ts, lambda j: off + j))
    wsp = lambda off: pl.BlockSpec((K, tc), lambda b, i, j: (0, off + j))
    bsp = lambda off: pl.BlockSpec((1, tc), lambda b, i, j: (0, off + j))
    return _pc(body, name=name, grid=(Bl, S // ts, nj),
               in_specs=[prev(0), main(0), prev(nj), main(nj), wsp(0), wsp(nj), bsp(0), bsp(nj)],
               out_specs=pl.BlockSpec((1, ts, tc), lambda b, i, j: (b, i, j)),
               out_shape=SDS((Bl, S, DFF), BF16),
               scratch=[pltpu.VMEM((HALO + ts, tc), F32), pltpu.VMEM((HALO + ts, tc), F32)],
               )(pre3, pre3, pre3, pre3, w, w, b, b)


def ffn_act_bwd(pre3, w, b, dact, *, comm=None, name):
    Bl, S, _ = pre3.shape
    K = FFN_K
    ts, tc = _tile(S, 512, 8), 256
    nj = DFF // tc
    ns = S // ts

    def body(gp_ref, g_ref, gn_ref, vp_ref, v_ref, vn_ref, d_ref, dn_ref, wg_ref, wv_ref, bg_ref, bv_ref,
             dg_ref, dv_ref, dwg_ref, dwv_ref, dbg_ref, dbv_ref, eg, ev, e2g, e2v):
        bb, i = pl.program_id(1), pl.program_id(2)
        last = i == ns - 1
        for p_ref, m_ref, n_ref, ext in ((gp_ref, g_ref, gn_ref, eg), (vp_ref, v_ref, vn_ref, ev)):
            ext[0:HALO, :] = jnp.where(i > 0, p_ref[0], 0.0)
            ext[HALO:HALO + ts, :] = m_ref[0]
            ext[HALO + ts:2 * HALO + ts, :] = jnp.where(last, 0.0, n_ref[0])
        zero8 = jnp.zeros((8, tc), F32)
        dwg, dwv, dbg, dbv = [zero8] * K, [zero8] * K, zero8, zero8
        for r0, n in _row_chunks(ts, 32) + [(ts, HALO)]:
            ug, tg = _conv_rows(eg, wg_ref, bg_ref, r0, n, K)
            uv, tv = _conv_rows(ev, wv_ref, bv_ref, r0, n, K)
            d = d_ref[0, r0:r0 + n, :] if r0 < ts else jnp.where(last, 0.0, dn_ref[0])
            dug = d * uv * _dsilu(ug)
            duv = d * _silu(ug)
            e2g[r0:r0 + n, :] = dug
            e2v[r0:r0 + n, :] = duv
            if r0 < ts:
                dwg = [a + _sum8(dug * t) for a, t in zip(dwg, tg)]
                dwv = [a + _sum8(duv * t) for a, t in zip(dwv, tv)]
                dbg, dbv = dbg + _sum8(dug), dbv + _sum8(duv)
        for w_ref, e2, o_ref in ((wg_ref, e2g, dg_ref), (wv_ref, e2v, dv_ref)):
            for r0, n in _row_chunks(ts, 64):
                o_ref[0, r0:r0 + n, :] = _conv_t_rows(e2, w_ref, r0, n, K).astype(BF16)

        @pl.when((bb == 0) & (i == 0))
        def _():
            for r in (dwg_ref, dwv_ref, dbg_ref, dbv_ref):
                r[...] = jnp.zeros_like(r)

        for dw_ref, dw, db_ref, db in ((dwg_ref, dwg, dbg_ref, dbg), (dwv_ref, dwv, dbv_ref, dbv)):
            for k in range(K):
                dw_ref[k:k + 1, :] += _csum(dw[k])
            db_ref[...] += _csum(db)

    nhb = S // HALO
    main = lambda off: pl.BlockSpec((1, ts, tc), lambda j, b, i: (b, i, off + j))
    prev = lambda off: pl.BlockSpec((1, HALO, tc), lambda j, b, i: _prev_map(ts, lambda jj: off + jj)(b, i, j))
    nxt = lambda off: pl.BlockSpec((1, HALO, tc), lambda j, b, i: _next_map(ts, nhb, lambda jj: off + jj)(b, i, j))
    wsp = lambda off: pl.BlockSpec((K, tc), lambda j, b, i: (0, off + j))
    bsp = lambda off: pl.BlockSpec((1, tc), lambda j, b, i: (0, off + j))
    outs = _pc(body, name=name, grid=(nj, Bl, ns),
               in_specs=[prev(0), main(0), nxt(0), prev(nj), main(nj), nxt(nj), main(0), nxt(0),
                         wsp(0), wsp(nj), bsp(0), bsp(nj)],
               out_specs=[main(0), main(0), wsp(0), wsp(0), bsp(0), bsp(0)],
               out_shape=[SDS((Bl, S, DFF), BF16), SDS((Bl, S, DFF), BF16), SDS((K, DFF), F32), SDS((K, DFF), F32),
                          SDS((1, DFF), F32), SDS((1, DFF), F32)],
               scratch=[pltpu.VMEM((2 * HALO + ts, tc), F32), pltpu.VMEM((2 * HALO + ts, tc), F32),
                        pltpu.VMEM((HALO + ts, tc), F32), pltpu.VMEM((HALO + ts, tc), F32)],
               comm=comm)(pre3, pre3, pre3, pre3, pre3, pre3, dact, dact, w, w, b, b)
    return outs


PHALO = 16


def _pool_window_sums(ext, base, ts, step):
    s = ext[pl.ds(base, ts), :]
    out = []
    for i in range(1, PHALO):
        s = s + ext[pl.ds(base + step * i, ts), :]
        if i + 1 in POOL_WIN:
            out.append(s)
    return out


def _pick(g, vals):
    r = vals[-1]
    for k in range(len(vals) - 2, -1, -1):
        r = jnp.where(g == k, vals[k], r)
    return r


def _pool_count(g, i, ts, rows):
    t = (i * ts + lax.broadcasted_iota(jnp.int32, (rows, 1), 0) + 1).astype(F32)
    return jnp.minimum(t, _pick(g, [float(w) for w in POOL_WIN]))


def _pooled(up_ref, u_ref, ext, g, i, ts):
    ext[0:PHALO, :] = jnp.where(i > 0, up_ref[0], 0.0)
    u = u_ref[0]
    ext[PHALO:PHALO + ts, :] = u
    sums = _pool_window_sums(ext, PHALO, ts, -1)
    return _pick(g, sums) / _pool_count(g, i, ts, ts) - u


def pool_fwd(proj3, pool_w, scale, *, name):
    Bl, S, _ = proj3.shape
    ts = _tile(S, 512, 16)
    c0 = U0 // POOL_D

    def body(up_ref, u_ref, w_ref, s_ref, o_ref, ext):
        i, g = pl.program_id(1), pl.program_id(2)
        pooled = _pooled(up_ref, u_ref, ext, g, i, ts)
        o_ref[0] = (_nn(pooled, w_ref[0]) * s_ref[...]).astype(BF16)

    return _pc(body, name=name, grid=(Bl, S // ts, POOL_G),
               in_specs=[pl.BlockSpec((1, PHALO, POOL_D), lambda b, i, g: (b, jnp.maximum(i * (ts // PHALO) - 1, 0), c0 + g)),
                         pl.BlockSpec((1, ts, POOL_D), lambda b, i, g: (b, i, c0 + g)),
                         pl.BlockSpec((1, POOL_D, POOL_D), lambda b, i, g: (g, 0, 0)),
                         pl.BlockSpec((1, POOL_D), lambda b, i, g: (0, g))],
               out_specs=pl.BlockSpec((1, ts, POOL_D), lambda b, i, g: (b, i, g)),
               out_shape=SDS((Bl, S, POOL_W), BF16), scratch=[pltpu.VMEM((PHALO + ts, POOL_D), F32)],
               )(proj3, proj3, pool_w, scale)


def pool_bwd(proj3, pool_w, scale, dycat3, *, name):
    Bl, S, _ = proj3.shape
    ts = _tile(S, 512, 16)
    ns = S // ts
    c0 = U0 // POOL_D
    d0 = SSD_W // POOL_D
    nhb = S // PHALO

    def body(up_ref, u_ref, d_ref, dn_ref, w_ref, s_ref, du_ref, dw_ref, ds_ref, ext, ext2):
        g, bb, i = pl.program_id(0), pl.program_id(1), pl.program_id(2)
        last = i == ns - 1
        pooled = _pooled(up_ref, u_ref, ext, g, i, ts)
        wm = w_ref[0]
        sc = s_ref[...]
        dy = d_ref[0]
        dp_main = dy * sc
        dpool = _nt(dp_main, wm)
        dpool_n = _nt(jnp.where(last, 0.0, dn_ref[0]) * sc, wm)
        ext2[0:ts, :] = dpool / _pool_count(g, i, ts, ts)
        ext2[ts:ts + PHALO, :] = dpool_n / _pool_count(g, i + 1, ts, PHALO)
        sums = _pool_window_sums(ext2, 0, ts, 1)
        du_ref[0] = (_pick(g, sums) - dpool).astype(BF16)

        @pl.when((bb == 0) & (i == 0))
        def _():
            dw_ref[...] = jnp.zeros_like(dw_ref)
            ds_ref[...] = jnp.zeros_like(ds_ref)

        dw_ref[0] += _tn(pooled, dp_main)
        ds_ref[...] += _csum(dy * _nn(pooled, wm))

    return _pc(body, name=name, grid=(POOL_G, Bl, ns),
               in_specs=[pl.BlockSpec((1, PHALO, POOL_D), lambda g, b, i: (b, jnp.maximum(i * (ts // PHALO) - 1, 0), c0 + g)),
                         pl.BlockSpec((1, ts, POOL_D), lambda g, b, i: (b, i, c0 + g)),
                         pl.BlockSpec((1, ts, POOL_D), lambda g, b, i: (b, i, d0 + g)),
                         pl.BlockSpec((1, PHALO, POOL_D), lambda g, b, i: (b, jnp.minimum((i + 1) * (ts // PHALO), nhb - 1), d0 + g)),
                         pl.BlockSpec((1, POOL_D, POOL_D), lambda g, b, i: (g, 0, 0)),
                         pl.BlockSpec((1, POOL_D), lambda g, b, i: (0, g))],
               out_specs=[pl.BlockSpec((1, ts, POOL_D), lambda g, b, i: (b, i, g)),
                          pl.BlockSpec((1, POOL_D, POOL_D), lambda g, b, i: (g, 0, 0)),
                          pl.BlockSpec((1, POOL_D), lambda g, b, i: (0, g))],
               out_shape=[SDS((Bl, S, POOL_W), BF16), SDS((POOL_G, POOL_D, POOL_D), F32), SDS((1, POOL_W), F32)],
               scratch=[pltpu.VMEM((PHALO + ts, POOL_D), F32), pltpu.VMEM((PHALO + ts, POOL_D), F32)],
               )(proj3, proj3, dycat3, dycat3, pool_w, scale)


NPAIR = SSD_HEADS // 2


def _ssd_common(sm, bias, alog):
    L = SSD_L
    dt = jax.nn.softplus(sm + bias)
    a = -jnp.exp(alog)
    da = dt * a
    r = lax.broadcasted_iota(jnp.int32, (L, L), 0)
    c = lax.broadcasted_iota(jnp.int32, (L, L), 1)
    tri = (r >= c).astype(F32)
    cum = _dg(tri, da, 1, 0, lax.Precision.HIGHEST)
    return dt, a, cum, cum.T, r >= c


def _lanes(lo, hi, shape=(1, LANES)):
    lane = lax.broadcasted_iota(jnp.int32, shape, len(shape) - 1)
    return (lane >= lo) & (lane < hi)


def _onehot_lane(h):
    return (lax.broadcasted_iota(jnp.int32, (1, LANES), 1) == h).astype(F32)


def _split_nn(a, e):
    hi = a.astype(BF16)
    lo = (a - hi.astype(F32)).astype(BF16)
    return _dg(hi, e, 1, 0) + _dg(lo, e, 1, 0)


def _head_spread():
    r = lax.broadcasted_iota(jnp.int32, (LANES, SSD_W), 0)
    c = lax.broadcasted_iota(jnp.int32, (LANES, SSD_W), 1)
    return (c // SSD_P == r).astype(BF16)


def _pair_gather(j):
    r = lax.broadcasted_iota(jnp.int32, (LANES, LANES), 0)
    c = lax.broadcasted_iota(jnp.int32, (LANES, LANES), 1)
    return (c == 2 * j + (r >= SSD_P).astype(jnp.int32)).astype(BF16)


def ssd_fwd(xbc3, proj3, bias, alog, dskip, *, name):
    Bl, S, _ = xbc3.shape
    L = SSD_L
    nc = S // L

    def body(xbc_ref, sm_ref, bias_ref, alog_ref, d_ref, y_ref, hin_ref, H):
        c = pl.program_id(1)

        @pl.when(c == 0)
        def _():
            H[...] = jnp.zeros_like(H)

        dt, a, cum, cumT, mask = _ssd_common(sm_ref[0], bias_ref[...], alog_ref[...])
        lo = _lanes(0, SSD_P)
        rowlo = lax.broadcasted_iota(jnp.int32, (LANES, LANES), 0) < SSD_P
        spread = _head_spread()
        dt_x = _split_nn(dt, spread)
        el_x = _split_nn(jnp.exp(cum), spread)
        wl_x = _split_nn(jnp.exp(cum[L - 1:L, :] - cum), spread)
        cb = []
        for g in range(SSD_G):
            Bg = xbc_ref[0, :, SSD_W + g * SSD_N:SSD_W + (g + 1) * SSD_N]
            Cg = xbc_ref[0, :, SSD_W + SSD_G * SSD_N + g * SSD_N:SSD_W + SSD_G * SSD_N + (g + 1) * SSD_N]
            cb.append((Bg, Cg, _nt(Cg, Bg)))
        for j in range(NPAIR):
            h0, h1 = 2 * j, 2 * j + 1
            sl = slice(j * LANES, (j + 1) * LANES)
            Bg, Cg, CB = cb[j // (NPAIR // SSD_G)]
            X = xbc_ref[0, :, sl]
            c0, c1 = cum[:, h0:h0 + 1], cum[:, h1:h1 + 1]
            r0, r1 = cumT[h0:h0 + 1, :], cumT[h1:h1 + 1, :]
            cl0, cl1 = cum[L - 1:L, h0:h0 + 1], cum[L - 1:L, h1:h1 + 1]
            Xt = X * dt_x[:, sl]
            M0 = CB * jnp.exp(jnp.where(mask, c0 - r0, NEG))
            M1 = CB * jnp.exp(jnp.where(mask, c1 - r1, NEG))
            Yd = jnp.where(lo, _nn(M0, Xt), _nn(M1, Xt))
            Hp = H[j]
            hin_ref[0, 0, j] = Hp
            Z = _nt(Cg, Hp)
            y_ref[0, :, sl] = Yd + el_x[:, sl] * Z + X * d_ref[j:j + 1, :]
            H[j] = jnp.where(rowlo, jnp.exp(cl0), jnp.exp(cl1)) * Hp + _tn(wl_x[:, sl] * Xt, Bg)

    vec = pl.BlockSpec((1, LANES), lambda b, c: (0, 0))
    return _pc(body, name=name, grid=(Bl, nc),
               in_specs=[pl.BlockSpec((1, L, SSD_XBC), lambda b, c: (b, c, 0)),
                         pl.BlockSpec((1, L, LANES), lambda b, c: (b, c, DT0 // LANES)),
                         vec, vec, pl.BlockSpec((NPAIR, LANES), lambda b, c: (0, 0))],
               out_specs=[pl.BlockSpec((1, L, SSD_W), lambda b, c: (b, c, 0)),
                          pl.BlockSpec((1, 1, NPAIR, LANES, LANES), lambda b, c: (b, c, 0, 0, 0))],
               out_shape=[SDS((Bl, S, SSD_W), F32), SDS((Bl, nc, NPAIR, LANES, LANES), F32)],
               scratch=[pltpu.VMEM((NPAIR, LANES, LANES), F32)])(xbc3, proj3, bias, alog, dskip)


def ssd_bwd(xbc3, proj3, hin, dy3, bias, alog, dskip, *, comm=None, name):
    Bl, S, _ = xbc3.shape
    L = SSD_L
    nc = S // L

    def body(xbc_ref, sm_ref, hin_ref, dy_ref, bias_ref, alog_ref, d_ref, dx_ref, ddt_ref, dpar_ref, dd_ref, dH, ddacc):
        bb, i = pl.program_id(0), pl.program_id(1)

        @pl.when(i == 0)
        def _():
            dH[...] = jnp.zeros_like(dH)

        @pl.when((bb == 0) & (i == 0))
        def _():
            dpar_ref[...] = jnp.zeros_like(dpar_ref)
            ddacc[...] = jnp.zeros_like(ddacc)

        sm = sm_ref[0]
        dt, a, cum, cumT, mask = _ssd_common(sm, bias_ref[...], alog_ref[...])
        maskf = mask.astype(F32)
        lo = _lanes(0, SSD_P)
        rowlo = lax.broadcasted_iota(jnp.int32, (LANES, LANES), 0) < SSD_P
        lastrow = (lax.broadcasted_iota(jnp.int32, (L, 1), 0) == L - 1).astype(F32)
        dcum = jnp.zeros((L, LANES), F32)
        dcum_t = jnp.zeros((LANES, L), F32)
        ddt = jnp.zeros((L, LANES), F32)
        spread = _head_spread()
        ones = jnp.ones((L, LANES), BF16)
        ecum = jnp.exp(cum)
        wall = jnp.exp(cum[L - 1:L, :] - cum)
        dt_x = _split_nn(dt, spread)
        el_x = _split_nn(ecum, spread)
        wl_x = _split_nn(wall, spread)
        headrow = lax.broadcasted_iota(jnp.int32, (LANES, 1), 0)
        grp = []
        for g in range(SSD_G):
            Bg = xbc_ref[0, :, SSD_W + g * SSD_N:SSD_W + (g + 1) * SSD_N]
            Cg = xbc_ref[0, :, SSD_W + SSD_G * SSD_N + g * SSD_N:SSD_W + SSD_G * SSD_N + (g + 1) * SSD_N]
            grp.append(dict(B=Bg, C=Cg, CB=_nt(Cg, Bg), dB=jnp.zeros((L, SSD_N), F32), dC=jnp.zeros((L, SSD_N), F32),
                            dCB=jnp.zeros((L, L), F32)))
        for j in range(NPAIR):
            h0, h1 = 2 * j, 2 * j + 1
            sl = slice(j * LANES, (j + 1) * LANES)
            G = grp[j // (NPAIR // SSD_G)]
            Bg, Cg, CB = G["B"], G["C"], G["CB"]
            X = xbc_ref[0, :, sl]
            dY = dy_ref[0, :, sl]
            c0, c1 = cum[:, h0:h0 + 1], cum[:, h1:h1 + 1]
            r0, r1 = cumT[h0:h0 + 1, :], cumT[h1:h1 + 1, :]
            cl0, cl1 = cum[L - 1:L, h0:h0 + 1], cum[L - 1:L, h1:h1 + 1]
            oh0, oh1 = _onehot_lane(h0), _onehot_lane(h1)
            gather = _pair_gather(j)
            dtl, el, wl = dt_x[:, sl], el_x[:, sl], wl_x[:, sl]
            Xt = X * dtl
            Hp = hin_ref[0, 0, j]
            dS = dH[j]
            dX = dY * d_ref[j:j + 1, :]
            ddacc[j:j + 1, :] += _csum(dY * X)
            Z = _nt(Cg, Hp)
            dZ = dY * el
            dcum = dcum + _split_nn(dY * Z, gather) * ecum
            G["dC"] = G["dC"] + _nn(dZ, Hp)
            dHy = _tn(dZ, Cg)
            Gm = _nt(Bg, dS)
            dXt = wl * Gm
            q = _split_nn(Xt * Gm, gather) * wall
            dcum = dcum + lastrow * _csum(q) - q
            G["dB"] = G["dB"] + _nn(wl * Xt, dS)
            g0, g1 = jnp.exp(cl0), jnp.exp(cl1)
            rowsum = _nn(dS * Hp, ones)
            dg0 = _csum(jnp.where(rowlo, rowsum, 0.0))
            dg1 = _csum(jnp.where(rowlo, 0.0, rowsum))
            dcum = dcum + lastrow * ((dg0 * g0) * oh0 + (dg1 * g1) * oh1)
            dH[j] = jnp.where(rowlo, g0, g1) * dS + dHy
            for h, ch, rh, mh, oh in ((h0, c0, r0, lo, oh0), (h1, c1, r1, jnp.logical_not(lo), oh1)):
                decay = jnp.exp(jnp.where(mask, ch - rh, NEG))
                Mh = CB * decay
                dM = _nt(jnp.where(mh, dY, 0.0), Xt) * maskf
                dXt = dXt + jnp.where(mh, _tn(Mh, dY), 0.0)
                G["dCB"] = G["dCB"] + dM * decay
                Q = dM * Mh
                dcum = dcum + _rsum(Q) * oh
                dcum_t = dcum_t + (headrow == h).astype(F32) * _csum(Q)
            dX = dX + dXt * dtl
            ddt = ddt + _split_nn(dXt * X, gather)
            dx_ref[0, :, sl] = dX
        dcum = dcum - dcum_t.T
        for g in range(SSD_G):
            G = grp[g]
            dC = G["dC"] + _nn(G["dCB"], G["B"])
            dB = G["dB"] + _tn(G["dCB"], G["C"])
            dx_ref[0, :, SSD_W + g * SSD_N:SSD_W + (g + 1) * SSD_N] = dB
            dx_ref[0, :, SSD_W + SSD_G * SSD_N + g * SSD_N:SSD_W + SSD_G * SSD_N + (g + 1) * SSD_N] = dC
        r = lax.broadcasted_iota(jnp.int32, (L, L), 0)
        c = lax.broadcasted_iota(jnp.int32, (L, L), 1)
        dda = _dg((c >= r).astype(F32), dcum, 1, 0, lax.Precision.HIGHEST)
        heads = _lanes(0, SSD_HEADS)
        ddt = ddt + dda * a
        draw = jnp.where(heads, ddt * _sig(sm + bias_ref[...]), 0.0)
        ddt_ref[0] = draw.astype(BF16)
        dpar_ref[0:1, :] += _csum(draw)
        dpar_ref[1:2, :] += _csum(jnp.where(heads, dda * dt * a, 0.0))

        @pl.when((bb == Bl - 1) & (i == nc - 1))
        def _():
            acc = ddacc[...]
            lane = lax.broadcasted_iota(jnp.int32, (NPAIR, LANES), 1)
            s0 = _rsum(jnp.where(lane < SSD_P, acc, 0.0))
            s1 = _rsum(jnp.where(lane < SSD_P, 0.0, acc))
            dd_ref[...] = jnp.where(lane == 0, s0, jnp.where(lane == 1, s1, 0.0))

    vec = pl.BlockSpec((1, LANES), lambda b, i: (0, 0))
    par = pl.BlockSpec((NPAIR, LANES), lambda b, i: (0, 0))
    return _pc(body, name=name, grid=(Bl, nc),
               in_specs=[pl.BlockSpec((1, L, SSD_XBC), lambda b, i: (b, nc - 1 - i, 0)),
                         pl.BlockSpec((1, L, LANES), lambda b, i: (b, nc - 1 - i, DT0 // LANES)),
                         pl.BlockSpec((1, 1, NPAIR, LANES, LANES), lambda b, i: (b, nc - 1 - i, 0, 0, 0)),
                         pl.BlockSpec((1, L, SSD_W), lambda b, i: (b, nc - 1 - i, 0)),
                         vec, vec, par],
               out_specs=[pl.BlockSpec((1, L, SSD_XBC), lambda b, i: (b, nc - 1 - i, 0)),
                          pl.BlockSpec((1, L, LANES), lambda b, i: (b, nc - 1 - i, 0)),
                          par, par],
               out_shape=[SDS((Bl, S, SSD_XBC), F32), SDS((Bl, S, LANES), BF16), SDS((NPAIR, LANES), F32),
                          SDS((NPAIR, LANES), F32)],
               scratch=[pltpu.VMEM((NPAIR, LANES, LANES), F32), pltpu.VMEM((NPAIR, LANES), F32)],
               comm=comm)(xbc3, proj3, hin, dy3, bias, alog, dskip)


PE_LO, PE_MID, PE_HI = MLA_NOPE, MLA_NOPE + MLA_ROPE // 2, MLA_NOPE + MLA_ROPE
ATT_SCALE = 1.0 / math.sqrt(MLA_QK)


def _swap_matrix():
    src = lax.broadcasted_iota(jnp.int32, (LANES, LANES), 0)
    dst = lax.broadcasted_iota(jnp.int32, (LANES, LANES), 1)
    half = MLA_ROPE // 2
    first = (dst >= PE_LO) & (dst < PE_MID) & (src == dst + half)
    second = (dst >= PE_MID) & (dst < PE_HI) & (src == dst - half)
    return (second.astype(F32) - first.astype(F32)).astype(BF16)


def rope_tables(pos, invf, *, name):
    T = pos.shape[0]
    tm = _tile(T, 512, 8)

    def body(pos_ref, f_ref, c_ref, s_ref):
        ang = pos_ref[...] * f_ref[...]
        pe = _lanes(PE_LO, PE_HI)
        c_ref[...] = jnp.where(pe, jnp.cos(ang), 1.0)
        s_ref[...] = jnp.where(pe, jnp.sin(ang), 0.0)

    tile = pl.BlockSpec((tm, LANES), lambda i: (i, 0))
    return _pc(body, name=name, grid=(T // tm,),
               in_specs=[pl.BlockSpec((tm, 1), lambda i: (i, 0)), pl.BlockSpec((1, LANES), lambda i: (0, 0))],
               out_specs=[tile, tile], out_shape=[SDS((T, LANES), F32)] * 2)(pos, invf)


V_ONE = MLA_V


def mla_prep_fwd(qt, kvt, proj, cs, sn, *, name):
    T = qt.shape[0]
    tm = _tile(T, 256, 8)
    HW = MLA_H * LANES

    def body(q_ref, k_ref, v_ref, kpe_ref, c_ref, s_ref, qo_ref, ko_ref, vo_ref):
        c, s = c_ref[...], s_ref[...]
        kpe = kpe_ref[...]
        sw = _swap_matrix()
        one = _lanes(V_ONE, V_ONE + 1)
        for h in range(MLA_H):
            sl = slice(h * LANES, (h + 1) * LANES)
            q = q_ref[:, sl]
            k = k_ref[:, sl] + kpe
            qo_ref[:, sl] = ((q * c + _split_nn(q, sw) * s) * ATT_SCALE).astype(BF16)
            ko_ref[:, sl] = (k * c + _split_nn(k, sw) * s).astype(BF16)
            vo_ref[:, sl] = jnp.where(one, 1.0, v_ref[:, sl]).astype(BF16)

    row = pl.BlockSpec((tm, HW), lambda i: (i, 0))
    tab = pl.BlockSpec((tm, LANES), lambda i: (i, 0))
    return _pc(body, name=name, grid=(T // tm,),
               in_specs=[row, row, pl.BlockSpec((tm, HW), lambda i: (i, 1)),
                         pl.BlockSpec((tm, LANES), lambda i: (i, KPE0 // LANES)), tab, tab],
               out_specs=[row, row, row], out_shape=[SDS((T, HW), BF16)] * 3)(qt, kvt, kvt, proj, cs, sn)


def mla_prep_bwd(dqr, dkr, cs, sn, *, name):
    T = dqr.shape[0]
    tm = _tile(T, 256, 8)
    HW = MLA_H * LANES

    def body(dq_ref, dk_ref, c_ref, s_ref, qo_ref, ko_ref, kpe_ref):
        c, s = c_ref[...], s_ref[...]
        sw = _swap_matrix()
        pe = _lanes(PE_LO, PE_HI)
        dkpe = jnp.zeros((tm, LANES), F32)
        for h in range(MLA_H):
            sl = slice(h * LANES, (h + 1) * LANES)
            dq = dq_ref[:, sl] * ATT_SCALE
            dk = dk_ref[:, sl]
            qo_ref[:, sl] = (dq * c - _split_nn(dq * s, sw)).astype(BF16)
            dkk = dk * c - _split_nn(dk * s, sw)
            ko_ref[:, sl] = jnp.where(pe, 0.0, dkk).astype(BF16)
            dkpe = dkpe + jnp.where(pe, dkk, 0.0)
        kpe_ref[...] = dkpe.astype(BF16)

    row = pl.BlockSpec((tm, HW), lambda i: (i, 0))
    tab = pl.BlockSpec((tm, LANES), lambda i: (i, 0))
    return _pc(body, name=name, grid=(T // tm,), in_specs=[row, row, tab, tab], out_specs=[row, row, tab],
               out_shape=[SDS((T, HW), BF16), SDS((T, HW), BF16), SDS((T, LANES), BF16)])(dqr, dkr, cs, sn)


def _att_tile(S):
    return _tile(S, 512, LANES)


def _rep(x, n):
    return x if n == 1 else jnp.concatenate([x] * n, axis=1)


def _diag_mask(t, transposed=False):
    r = lax.broadcasted_iota(jnp.int32, (t, t), 0)
    c = lax.broadcasted_iota(jnp.int32, (t, t), 1)
    return (c >= r) if transposed else (c <= r)


def flash_fwd(qr, kr, vr, Bl, *, comm=None, name):
    T = qr.shape[0]
    S = T // Bl
    t = _att_tile(S)
    n = S // t
    nl = t // LANES

    def body(q_ref, k_ref, v_ref, o_ref, lse_ref, lset_ref, m, acc):
        qi = pl.program_id(2)
        q = q_ref[...]
        m[...] = jnp.full_like(m, NEG)
        acc[...] = jnp.zeros_like(acc)

        def block(kj, masked):
            off = pl.multiple_of(kj * t, t)
            s = _nt(q, k_ref[pl.ds(off, t), :])
            if masked:
                s = jnp.where(_diag_mask(t), s, NEG)
            mo = m[...]
            mn = jnp.maximum(mo, jnp.max(s, axis=1, keepdims=True))
            p = jnp.exp((s - _rep(mn, nl)).astype(BF16))
            acc[...] = jnp.exp(mo - mn) * acc[...] + _nn(p, v_ref[pl.ds(off, t), :])
            m[...] = mn

        def loop(kj, c):
            block(kj, False)
            return c

        lax.fori_loop(0, qi, loop, 0)
        block(qi, True)
        a = acc[...]
        l = a[:, V_ONE:V_ONE + 1]
        o_ref[...] = jnp.where(_lanes(0, MLA_V), a / l, 0.0).astype(BF16)
        lse = m[...] + jnp.log(l)
        lse_ref[...] = lse
        lset_ref[...] = lse.T[0:8, :]

    qs = pl.BlockSpec((t, LANES), lambda b, h, qi: (b * n + qi, h))
    seq = pl.BlockSpec((S, LANES), lambda b, h, qi: (b, h))
    return _pc(body, name=name, grid=(Bl, MLA_H, n), in_specs=[qs, seq, seq],
               out_specs=[qs, qs, pl.BlockSpec((8, t), lambda b, h, qi: (b * MLA_H + h, qi))],
               out_shape=[SDS((T, MLA_H * LANES), BF16), SDS((T, MLA_H * LANES), F32), SDS((Bl * MLA_H * 8, S), F32)],
               scratch=[pltpu.VMEM((t, LANES), F32), pltpu.VMEM((t, LANES), F32)], comm=comm)(qr, kr, vr)


def flash_bwd_dq(qr, kr, vr, o, lse, dycat, Bl, *, comm=None, name):
    T = qr.shape[0]
    S = T // Bl
    t = _att_tile(S)
    n = S // t
    nl = t // LANES
    do0 = (SSD_W + POOL_W) // LANES

    def body(q_ref, k_ref, v_ref, o_ref, lse_ref, do_ref, dq_ref, dlt_ref, acc, dl):
        qi = pl.program_id(2)
        q = q_ref[...]
        do = do_ref[...]
        dob = do.astype(BF16)
        dl[...] = jnp.broadcast_to(_rsum(do * o_ref[...].astype(F32)), (t, LANES))
        acc[...] = jnp.zeros_like(acc)

        def block(kj, masked):
            off = pl.multiple_of(kj * t, t)
            k = k_ref[pl.ds(off, t), :]
            s = _nt(q, k)
            if masked:
                s = jnp.where(_diag_mask(t), s, NEG)
            p = jnp.exp((s - _rep(lse_ref[...], nl)).astype(BF16))
            dp = _nt(dob, v_ref[pl.ds(off, t), :])
            acc[...] += _nn(p * (dp - _rep(dl[...], nl)), k)

        def loop(kj, c):
            block(kj, False)
            return c

        lax.fori_loop(0, qi, loop, 0)
        block(qi, True)
        dq_ref[...] = acc[...]
        dlt_ref[...] = dl[...].T[0:8, :]

    qs = pl.BlockSpec((t, LANES), lambda b, h, qi: (b * n + qi, h))
    seq = pl.BlockSpec((S, LANES), lambda b, h, qi: (b, h))
    return _pc(body, name=name, grid=(Bl, MLA_H, n),
               in_specs=[qs, seq, seq, qs, qs, pl.BlockSpec((t, LANES), lambda b, h, qi: (b * n + qi, do0 + h))],
               out_specs=[qs, pl.BlockSpec((8, t), lambda b, h, qi: (b * MLA_H + h, qi))],
               out_shape=[SDS((T, MLA_H * LANES), F32), SDS((Bl * MLA_H * 8, S), F32)],
               scratch=[pltpu.VMEM((t, LANES), F32), pltpu.VMEM((t, LANES), F32)], comm=comm)(qr, kr, vr, o, lse, dycat)


def flash_bwd_dkv(qr, kr, vr, lset, dlt, dycat, Bl, *, comm=None, name):
    T = qr.shape[0]
    S = T // Bl
    t = _att_tile(S)
    n = S // t
    do0 = (SSD_W + POOL_W) // LANES

    def body(q_ref, k_ref, v_ref, lset_ref, dlt_ref, do_ref, dk_ref, dv_ref, dka, dva):
        kj = pl.program_id(2)
        k = k_ref[...]
        v = v_ref[...]
        dka[...] = jnp.zeros_like(dka)
        dva[...] = jnp.zeros_like(dva)

        def block(qi, masked):
            off = pl.multiple_of(qi * t, t)
            q = q_ref[pl.ds(off, t), :]
            do = do_ref[pl.ds(off, t), :].astype(BF16)
            st = _nt(k, q)
            if masked:
                st = jnp.where(_diag_mask(t, True), st, NEG)
            pt = jnp.exp((st - lset_ref[0:1, pl.ds(off, t)]).astype(BF16))
            dst = pt * (_nt(v, do) - dlt_ref[0:1, pl.ds(off, t)])
            dva[...] += _nn(pt, do)
            dka[...] += _nn(dst, q)

        def loop(qi, c):
            block(qi, False)
            return c

        block(kj, True)
        lax.fori_loop(kj + 1, n, loop, 0)
        dk_ref[...] = dka[...]
        dv_ref[...] = dva[...].astype(BF16)

    ks = pl.BlockSpec((t, LANES), lambda b, h, kj: (b * n + kj, h))
    seq = pl.BlockSpec((S, LANES), lambda b, h, kj: (b, h))
    rows = pl.BlockSpec((8, S), lambda b, h, kj: (b * MLA_H + h, 0))
    return _pc(body, name=name, grid=(Bl, MLA_H, n),
               in_specs=[seq, ks, ks, rows, rows, pl.BlockSpec((S, LANES), lambda b, h, kj: (b, do0 + h))],
               out_specs=[ks, ks], out_shape=[SDS((T, MLA_H * LANES), F32), SDS((T, MLA_H * LANES), BF16)],
               scratch=[pltpu.VMEM((t, LANES), F32), pltpu.VMEM((t, LANES), F32)], comm=comm)(qr, kr, vr, lset, dlt, dycat)


def _rows2d(a):
    return a.reshape(-1, a.shape[-1])


def _scalar(i):
    return jnp.reshape(i, (1,)).astype(jnp.int32)


def chip_sum(g8, from_sibling, *, name):
    blk = g8.shape[1:]
    R, C = math.prod(blk[:-1]), blk[-1]
    tm = _tile(R, 512, 16)

    def body(c_ref, a_ref, b_ref, o_ref, ob_ref):
        s = a_ref[0, 0] + b_ref[0]
        o_ref[0] = s
        ob_ref[0] = s.astype(BF16)

    row = pl.BlockSpec((1, tm, C), lambda k, i, c: (k, i, 0))
    spec = pltpu.PrefetchScalarGridSpec(
        num_scalar_prefetch=1, grid=(4, R // tm),
        in_specs=[pl.BlockSpec((1, 1, tm, C), lambda k, i, c: (k, c[0], i, 0)), row], out_specs=[row, row])
    o, ob = pl.pallas_call(body, name=name, grid_spec=spec, out_shape=[SDS((4, R, C), F32), SDS((4, R, C), BF16)],
                           compiler_params=pltpu.CompilerParams(vmem_limit_bytes=VMEM_LIMIT),
                           )(_scalar(lax.axis_index("c")), g8.reshape(4, 2, R, C), from_sibling.reshape(4, R, C))
    return o.reshape((4,) + blk), ob.reshape((4,) + blk)


def adamw_sharded(w, m, v, sums, recv, layer, prev, *, name):
    blk = w.shape[1:]
    R, C = math.prod(blk[:-1]), blk[-1]
    tm = _tile(R, 256, 16)
    bc1 = 1.0 - ADAM_B1 ** ADAM_STEP
    bc2 = 1.0 - ADAM_B2 ** ADAM_STEP
    n_prev = 0 if prev is None else 4

    def body(chip_ref, w_ref, m_ref, v_ref, s_ref, r_ref, *rest):
        g_ref, d_ref, nm_ref, nv_ref = rest[n_prev:]
        g = s_ref[0] + r_ref[0].astype(F32) + r_ref[1].astype(F32) + r_ref[2].astype(F32)
        mm_ = ADAM_B1 * m_ref[0] + (1.0 - ADAM_B1) * g
        vv = ADAM_B2 * v_ref[0] + (1.0 - ADAM_B2) * (g * g)
        g_ref[0] = g
        nm_ref[0] = mm_
        nv_ref[0] = vv
        d_ref[0] = -ADAM_LR * ((mm_ / bc1) / (jnp.sqrt(vv / bc2) + ADAM_EPS) + ADAM_WD * w_ref[0])

    lay = pl.BlockSpec((1, tm, C), lambda i, c: (layer, i, 0))
    spec = pltpu.PrefetchScalarGridSpec(
        num_scalar_prefetch=1, grid=(R // tm,),
        in_specs=[lay, lay, lay, pl.BlockSpec((1, tm, C), lambda i, c: (c[0], i, 0)),
                  pl.BlockSpec((3, tm, C), lambda i, c: (0, i, 0))] + [ANY] * n_prev,
        out_specs=[lay] * 4)
    xi, yi, _ = _place()
    d3 = (w.shape[0], R, C)
    outs = pl.pallas_call(
        body, name=name, grid_spec=spec, out_shape=[SDS(d3, F32)] * 4,
        input_output_aliases={6 + i: i for i in range(n_prev)},
        compiler_params=pltpu.CompilerParams(vmem_limit_bytes=VMEM_LIMIT),
    )(_scalar(2 * xi + yi), w.reshape(d3), m.reshape(d3), v.reshape(d3), sums.reshape(4, R, C), recv.reshape(3, R, C),
      *([] if prev is None else prev))
    return list(outs)


def adamw(w, m, v, parts, *, name):
    shp = w.shape
    w2, m2, v2 = _rows2d(w), _rows2d(m), _rows2d(v)
    R, C = w2.shape
    p3 = [p.reshape(p.shape[0], R, C) for p in parts]
    tm = _tile(R, 256, 8)
    bc1 = 1.0 - ADAM_B1 ** ADAM_STEP
    bc2 = 1.0 - ADAM_B2 ** ADAM_STEP

    def body(w_ref, m_ref, v_ref, *refs):
        p_refs, (g_ref, d_ref, nm_ref, nv_ref) = refs[:len(p3)], refs[len(p3):]
        g = None
        for p_ref, p in zip(p_refs, p3):
            for k in range(p.shape[0]):
                term = p_ref[k].astype(F32)
                g = term if g is None else g + term
        mm_ = ADAM_B1 * m_ref[...] + (1.0 - ADAM_B1) * g
        vv = ADAM_B2 * v_ref[...] + (1.0 - ADAM_B2) * (g * g)
        g_ref[...] = g
        nm_ref[...] = mm_
        nv_ref[...] = vv
        d_ref[...] = -ADAM_LR * ((mm_ / bc1) / (jnp.sqrt(vv / bc2) + ADAM_EPS) + ADAM_WD * w_ref[...])

    blk = pl.BlockSpec((tm, C), lambda i: (i, 0))
    pspecs = [pl.BlockSpec((p.shape[0], tm, C), lambda i: (0, i, 0)) for p in p3]
    outs = _pc(body, name=name, grid=(R // tm,), in_specs=[blk, blk, blk] + pspecs,
               out_specs=[blk] * 4, out_shape=[SDS((R, C), F32)] * 4)(w2, m2, v2, *p3)
    return [o.reshape(shp) for o in outs]


def _place():
    return lax.axis_index("x"), lax.axis_index("y"), lax.axis_index("c")


def all_gather_many(xs, *, name):
    n = len(xs)

    def body(*refs):
        x_refs, o_refs = refs[:n], refs[n:2 * n]
        send_sems, recv_sems, local_sems = refs[2 * n:]
        x, y, c = _place()
        me, sibling = (x, y, c), (x, y, 1 - c)
        chips = [(1 - x, y), (x, 1 - y), (1 - x, 1 - y)]

        def rows(a, p):
            return o_refs[a].at[4 * p[0] + 2 * p[1] + p[2]]

        def copy(a, k, block, to, src=None):
            return pltpu.make_async_remote_copy(
                src_ref=rows(a, block) if src is None else src, dst_ref=rows(a, block),
                send_sem=send_sems.at[7 * a + k], recv_sem=recv_sems.at[7 * a + k], device_id=to, device_id_type=MESH)

        mine = [pltpu.make_async_copy(x_refs[a], rows(a, me), local_sems.at[a]) for a in range(n)]
        for cp in mine:
            cp.start()
        first = []
        for a in range(n):
            first.append(copy(a, 0, me, sibling, src=x_refs[a]))
            first += [copy(a, 1 + j, me, (*chip, c), src=x_refs[a]) for j, chip in enumerate(chips)]
        for cp in first:
            cp.start()
        passed = []
        for j, chip in enumerate(chips):
            for a in range(n):
                copy(a, 1 + j, (*chip, c), me).wait_recv()
                cp = copy(a, 4 + j, (*chip, c), sibling)
                cp.start()
                passed.append(cp)
        for a in range(n):
            copy(a, 0, sibling, me).wait_recv()
            for j, chip in enumerate(chips):
                copy(a, 4 + j, (*chip, 1 - c), me).wait_recv()
        for cp in first + passed:
            cp.wait_send()
        for cp in mine:
            cp.wait()

    return pl.pallas_call(
        body, name=name, in_specs=[ANY] * n, out_specs=[ANY] * n,
        out_shape=[SDS((N_DEV,) + a.shape, a.dtype) for a in xs],
        scratch_shapes=[pltpu.SemaphoreType.DMA((7 * n,)), pltpu.SemaphoreType.DMA((7 * n,)), pltpu.SemaphoreType.DMA((n,))],
    )(*xs)


def _stage(ins, out_shape, n_peers, copy_of):
    ins = list(ins)

    def copies(in_refs, out_refs, send_sems, recv_sems):
        place = _place()
        out = []
        for a in range(len(ins)):
            for k in range(n_peers):
                src, dst, peer = copy_of(in_refs[a], out_refs[a], k, place)
                out.append(pltpu.make_async_remote_copy(
                    src_ref=src, dst_ref=dst, send_sem=send_sems.at[n_peers * a + k], recv_sem=recv_sems.at[n_peers * a + k],
                    device_id=peer, device_id_type=MESH))
        return out

    return dict(ins=ins, out_shape=list(out_shape), sems=n_peers * len(ins), copies=copies)


def _other_chips(x, y):
    return [(1 - x, y), (x, 1 - y), (1 - x, 1 - y)]


def stage_gather_direct(blocks):
    def copy_of(src, dst, k, place):
        x, y, c = place
        peer = (x, y, 1 - c) if k == 0 else (*_other_chips(x, y)[k - 1], c)
        return src, dst.at[4 * x + 2 * y + c], peer

    return _stage(blocks, [SDS((N_DEV,) + b.shape, b.dtype) for b in blocks], 4, copy_of)


def stage_gather_forward(bufs):
    def copy_of(src, dst, k, place):
        x, y, c = place
        cx, cy = _other_chips(x, y)[k]
        slot = 4 * cx + 2 * cy + c
        return src.at[slot], dst.at[slot], (x, y, 1 - c)

    st = _stage(bufs, [SDS(b.shape, b.dtype) for b in bufs], 3, copy_of)
    st["alias"] = True
    return st


def stage_rs_sibling(g8s):
    def copy_of(src, dst, k, place):
        x, y, c = place
        return src.at[2 * k + (1 - c)], dst.at[k], (x, y, 1 - c)

    return _stage(g8s, [SDS((4,) + g.shape[1:], g.dtype) for g in g8s], 4, copy_of)


def stage_rs_chips(sums):
    def copy_of(src, dst, k, place):
        x, y, c = place
        chip = _other_chips(x, y)[k]
        return src.at[2 * chip[0] + chip[1]], dst.at[k], (*chip, c)

    return _stage(sums, [SDS((3,) + s.shape[1:], s.dtype) for s in sums], 3, copy_of)


def run_stage(stage, *, name):
    n_in, n_out = len(stage["ins"]), len(stage["out_shape"])

    def body(*refs):
        cps = stage["copies"](refs[:n_in], refs[n_in:n_in + n_out], refs[-2], refs[-1])
        for cp in cps:
            cp.start()
        for cp in cps:
            cp.wait()

    return pl.pallas_call(
        body, name=name, in_specs=[ANY] * n_in, out_specs=[ANY] * n_out, out_shape=stage["out_shape"],
        scratch_shapes=[pltpu.SemaphoreType.DMA((stage["sems"],)), pltpu.SemaphoreType.DMA((stage["sems"],))],
    )(*stage["ins"])


def with_own_block(buf, own):
    x, y, c = _place()
    return lax.dynamic_update_index_in_dim(buf, own, 4 * x + 2 * y + c, 0)


def _owner_major(full, axis):
    shp = full.shape
    r = full.reshape(shp[:axis] + (N_DEV, shp[axis] // N_DEV) + shp[axis + 1:])
    return jnp.moveaxis(r, axis, 0)


def _from_owner_major(g8, axis):
    r = jnp.moveaxis(g8, 0, axis)
    shp = r.shape
    return r.reshape(shp[:axis] + (shp[axis] * shp[axis + 1],) + shp[axis + 2:])


def _perm_w_in(w):
    z = jnp.zeros((w.shape[0], LANES), w.dtype)
    dt = jnp.pad(w[:, 2560:2576], ((0, 0), (0, LANES - SSD_HEADS)))
    kpe = jnp.pad(w[:, 3728:3760], ((0, 0), (PE_LO, LANES - PE_HI)))
    return jnp.concatenate([w[:, 0:1024], w[:, 1024:2560], w[:, 2576:3088], w[:, 3088:3472], z, w[:, 3472:3728], dt, kpe], axis=1)


def _unperm_w_in(g):
    return jnp.concatenate([g[:, Z0:Z0 + 1024], g[:, XBC0:XBC0 + 1536], g[:, DT0:DT0 + SSD_HEADS], g[:, U0:U0 + 512],
                            g[:, CQ0:CQ0 + 384], g[:, CKV0:CKV0 + 256], g[:, KPE0 + PE_LO:KPE0 + PE_HI]], axis=1)


def _perm_w_uq(w):
    return jnp.pad(w.reshape(MLA_QR, MLA_H, MLA_QK), ((0, 0), (0, 0), (0, LANES - MLA_QK))).reshape(MLA_QR, MLA_H * LANES)


def _unperm_w_uq(g):
    return g.reshape(MLA_QR, MLA_H, LANES)[:, :, :MLA_QK].reshape(MLA_QR, MLA_H * MLA_QK)


def _perm_w_ukv(w):
    w3 = w.reshape(MLA_KVR, MLA_H, MLA_NOPE + MLA_V)
    pad = ((0, 0), (0, 0), (0, LANES - MLA_NOPE))
    k = jnp.pad(w3[:, :, :MLA_NOPE], pad).reshape(MLA_KVR, MLA_H * LANES)
    v = jnp.pad(w3[:, :, MLA_NOPE:], pad).reshape(MLA_KVR, MLA_H * LANES)
    return jnp.concatenate([k, v], axis=1)


def _unperm_w_ukv(g):
    k = g[:, :MLA_H * LANES].reshape(MLA_KVR, MLA_H, LANES)[:, :, :MLA_NOPE]
    v = g[:, MLA_H * LANES:].reshape(MLA_KVR, MLA_H, LANES)[:, :, :MLA_V]
    return jnp.concatenate([k, v], axis=2).reshape(MLA_KVR, MLA_H * (MLA_NOPE + MLA_V))


def _perm_w_out(w):
    m = jnp.pad(w[SSD_W + POOL_W:].reshape(MLA_H, MLA_V, D), ((0, 0), (0, LANES - MLA_V), (0, 0))).reshape(MLA_H * LANES, D)
    return jnp.concatenate([w[:SSD_W + POOL_W], m], axis=0)


def _unperm_w_out(g):
    m = g[SSD_W + POOL_W:].reshape(MLA_H, LANES, D)[:, :MLA_V].reshape(MLA_H * MLA_V, D)
    return jnp.concatenate([g[:SSD_W + POOL_W], m], axis=0)


def _lane_pad(v):
    return jnp.pad(v.reshape(1, -1), ((0, 0), (0, LANES - v.shape[-1])))


SMALL = ("attn_norm", "ssd_conv_b", "ssd_dt_bias", "ssd_a_log", "ssd_d", "ssd_norm", "pool_w", "pool_scale",
         "mla_q_norm", "mla_kv_norm", "ffn_norm", "ffn_conv_b", "final_norm")
SHARDED = {"w_in": 2, "ssd_conv_w": 2, "mla_w_uq": 2, "mla_w_ukv": 2, "w_out": 1, "ffn_w_up": 2, "ffn_conv_w": 2,
           "ffn_w_down": 1}
ALL_W = ("attn_norm", "w_in", "ssd_conv_w", "ssd_conv_b", "ssd_dt_bias", "ssd_a_log", "ssd_d", "ssd_norm", "pool_w",
         "pool_scale", "mla_q_norm", "mla_w_uq", "mla_kv_norm", "mla_w_ukv", "w_out", "ffn_norm", "ffn_w_up",
         "ffn_conv_w", "ffn_conv_b", "ffn_w_down", "final_norm")


def _pack_small(d):
    rows, layout = [], []
    for k in SMALL:
        a = d[k].reshape(-1)
        n = a.shape[0]
        r = -(-n // LANES)
        rows.append(jnp.pad(a, (0, r * LANES - n)).reshape(r, LANES))
        layout.append((k, n, r, d[k].shape))
    buf = jnp.concatenate(rows, axis=0)
    pad = (-buf.shape[0]) % 8
    return jnp.pad(buf, ((0, pad), (0, 0))), layout


def _unpack_small(buf, layout):
    out, r0 = {}, 0
    for k, n, r, shp in layout:
        out[k] = buf[r0:r0 + r].reshape(-1)[:n].reshape(shp)
        r0 += r
    return out


def _layer_fwd(l, x, W, cs, sn, Bl, next_blocks=None):
    T = x.shape[0]
    S = T // Bl
    n = f"l{l}_"
    h = rms_fwd(x, W["attn_norm"], name=n + "attn_norm")
    proj = mm(h, W["w_in"], name=n + "w_in")
    proj3 = proj.reshape(Bl, S, PW)
    xbc3 = conv_silu_fwd(proj3, W["ssd_conv_w"], W["ssd_conv_b"], name=n + "ssd_conv")
    y3, hin = ssd_fwd(xbc3, proj3, W["ssd_dt_bias"], W["ssd_a_log"], W["ssd_d"], name=n + "ssd_scan")
    y = y3.reshape(T, SSD_W)
    y_ssd = gated_rms_fwd(y, proj, W["ssd_norm"], name=n + "ssd_gate_norm")
    y_pool = pool_fwd(proj3, W["pool_w"], W["pool_scale"], name=n + "pool").reshape(T, POOL_W)
    qn = rms_fwd(proj, W["mla_q_norm"], col0=CQ0, width=MLA_QR, name=n + "q_norm")
    kvn = rms_fwd(proj, W["mla_kv_norm"], col0=CKV0, width=MLA_KVR, name=n + "kv_norm")
    qt = mm(qn, W["mla_w_uq"], name=n + "w_uq")
    kvt = mm(kvn, W["mla_w_ukv"], name=n + "w_ukv")
    qr, kr, vr = mla_prep_fwd(qt, kvt, proj, cs, sn, name=n + "rope")
    direct = stage_gather_direct(next_blocks) if next_blocks is not None else None
    o, lse, lset = flash_fwd(qr, kr, vr, Bl, comm=direct, name=n + "attn")
    ycat = jnp.concatenate([y_ssd, y_pool, o], axis=1)
    x1 = mm(ycat, W["w_out"], add=x, name=n + "w_out")
    h2 = rms_fwd(x1, W["ffn_norm"], name=n + "ffn_norm")
    forward = stage_gather_forward(direct["result"]) if direct else None
    pre = mm(h2, W["ffn_w_up"], comm=forward, name=n + "w_up")
    gathered = [with_own_block(buf, b) for buf, b in zip(forward["result"], next_blocks)] if direct else None
    pre3 = pre.reshape(Bl, S, 2 * DFF)
    act = ffn_act_fwd(pre3, W["ffn_conv_w"], W["ffn_conv_b"], name=n + "ffn_act").reshape(T, DFF)
    x2 = mm(act, W["ffn_w_down"], add=x1, name=n + "w_down")
    saved = dict(x=x, h=h, proj=proj, xbc3=xbc3, hin=hin, y=y, qn=qn, kvn=kvn, vr=vr, qr=qr, kr=kr, o=o, lse=lse, lset=lset,
                 ycat=ycat, x1=x1, h2=h2, pre3=pre3, act=act)
    return x2, saved, gathered


EARLY = ("ffn_w_up", "ffn_conv_w", "ffn_w_down", "w_out")
LATE = tuple(k for k in SHARDED if k not in EARLY)
_UNPERM = {"w_in": _unperm_w_in, "mla_w_uq": _unperm_w_uq, "mla_w_ukv": _unperm_w_ukv, "w_out": _unperm_w_out}


def _by_owner(g, keys):
    return [_owner_major(_UNPERM.get(k, lambda t: t)(g[k]), SHARDED[k] - 1) for k in keys]


def _chip_sums(g8s, from_sibling, tag):
    return [chip_sum(g8, r, name=f"{tag}{a}") for a, (g8, r) in enumerate(zip(g8s, from_sibling))]


def _layer_bwd(l, dx2, dx2b, W, sv, cs, sn, Bl, later_g8=None):
    T = dx2.shape[0]
    S = T // Bl
    n = f"l{l}_b_"
    g = {}
    g["ffn_w_down"] = mm(sv["act"], dx2b, ta=True, name=n + "dw_down")
    dact = mm(dx2b, W["ffn_w_down"], tb=True, name=n + "dact")
    to_sibling = stage_rs_sibling(later_g8) if later_g8 is not None else None
    dpg, dpv, dwg, dwv, dbg, dbv = ffn_act_bwd(sv["pre3"], W["ffn_conv_w"], W["ffn_conv_b"], dact.reshape(Bl, S, DFF),
                                               comm=to_sibling, name=n + "ffn_act")
    to_chips = sums = None
    if to_sibling:
        sums = _chip_sums(later_g8, to_sibling["result"], n + "rs_late_add")
        to_chips = stage_rs_chips([sb for _, sb in sums])
    g["ffn_conv_w"] = jnp.concatenate([dwg, dwv], axis=1)
    g["ffn_conv_b"] = jnp.concatenate([dbg, dbv], axis=1)
    dpg, dpv = dpg.reshape(T, DFF), dpv.reshape(T, DFF)
    g["ffn_w_up"] = jnp.concatenate([mm(sv["h2"], dpg, ta=True, name=n + "dw_up_g"),
                                     mm(sv["h2"], dpv, ta=True, name=n + "dw_up_v")], axis=1)
    dh2 = mm(dpg, W["ffn_w_up"], tb=True, name=n + "dh2_g")
    dh2 = mm(dpv, W["ffn_w_up"], tb=True, b_k0=DFF, add=dh2, name=n + "dh2_v")
    dx1, dx1b, g["ffn_norm"] = rms_bwd(sv["x1"], W["ffn_norm"], dh2, add=dx2, name=n + "ffn_norm")
    g["w_out"] = mm(sv["ycat"], dx1b, ta=True, name=n + "dw_out")
    dycat = mm(dx1b, W["w_out"], tb=True, name=n + "dycat")
    proj, proj3 = sv["proj"], sv["proj"].reshape(Bl, S, PW)
    dy, dz, g["ssd_norm"] = gated_rms_bwd(sv["y"], proj, W["ssd_norm"], dycat, name=n + "ssd_gate_norm")
    dxa, ddt, dpar, dd = ssd_bwd(sv["xbc3"], proj3, sv["hin"], dy.reshape(Bl, S, SSD_W), W["ssd_dt_bias"], W["ssd_a_log"],
                                 W["ssd_d"], comm=to_chips, name=n + "ssd_scan")
    reduced_late = ([s32 for s32, _ in sums], to_chips["result"]) if to_chips else None
    g["ssd_dt_bias"] = dpar[0, :SSD_HEADS]
    g["ssd_a_log"] = dpar[1, :SSD_HEADS]
    g["ssd_d"] = dd[:, :2].reshape(SSD_HEADS)
    dxbc, g["ssd_conv_w"], g["ssd_conv_b"] = conv_silu_bwd(proj3, W["ssd_conv_w"], W["ssd_conv_b"], dxa, name=n + "ssd_conv")
    du, g["pool_w"], g["pool_scale"] = pool_bwd(proj3, W["pool_w"], W["pool_scale"], dycat.reshape(Bl, S, YCAT), name=n + "pool")
    early_g8 = _by_owner(g, EARLY)
    early_sibling = stage_rs_sibling(early_g8)
    dqr, dlt = flash_bwd_dq(sv["qr"], sv["kr"], sv["vr"], sv["o"], sv["lse"], dycat, Bl, comm=early_sibling, name=n + "attn_dq")
    early_sums = _chip_sums(early_g8, early_sibling["result"], n + "rs_early_add")
    early_chips = stage_rs_chips([sb for _, sb in early_sums])
    dkr, dv = flash_bwd_dkv(sv["qr"], sv["kr"], sv["vr"], sv["lset"], dlt, dycat, Bl, comm=early_chips, name=n + "attn_dkv")
    reduced_early = ([s32 for s32, _ in early_sums], early_chips["result"])
    dqt, dkt, dkpe = mla_prep_bwd(dqr, dkr, cs, sn, name=n + "rope")
    g["mla_w_ukv"] = jnp.concatenate([mm(sv["kvn"], dkt, ta=True, name=n + "dw_uk"),
                                      mm(sv["kvn"], dv, ta=True, name=n + "dw_uv")], axis=1)
    dkvn = mm(dkt, W["mla_w_ukv"], tb=True, name=n + "dkvn_k")
    dkvn = mm(dv, W["mla_w_ukv"], tb=True, b_k0=MLA_H * LANES, add=dkvn, name=n + "dkvn_v")
    g["mla_w_uq"] = mm(sv["qn"], dqt, ta=True, name=n + "dw_uq")
    dqn = mm(dqt, W["mla_w_uq"], tb=True, name=n + "dqn")
    dcq, g["mla_q_norm"] = rms_bwd(proj, W["mla_q_norm"], dqn, col0=CQ0, width=MLA_QR, name=n + "q_norm")
    dckv, g["mla_kv_norm"] = rms_bwd(proj, W["mla_kv_norm"], dkvn, col0=CKV0, width=MLA_KVR, name=n + "kv_norm")
    dproj = jnp.concatenate([dz, dxbc.reshape(T, SSD_XBC), du.reshape(T, POOL_W), dcq, jnp.zeros((T, LANES), BF16), dckv,
                             ddt.reshape(T, LANES), dkpe], axis=1)
    g["w_in"] = mm(sv["h"], dproj, ta=True, name=n + "dw_in")
    dh = mm(dproj, W["w_in"], tb=True, name=n + "dh")
    dx, dxb, g["attn_norm"] = rms_bwd(sv["x"], W["attn_norm"], dh, add=dx1, name=n + "attn_norm")
    return dx, dxb, g, reduced_late, reduced_early


def kernel(x, positions, attn_norm, w_in, ssd_conv_w, ssd_conv_b, ssd_dt_bias, ssd_a_log, ssd_d, ssd_norm, pool_w, pool_scale, mla_q_norm, mla_w_uq, mla_kv_norm, mla_w_ukv, w_out, ffn_norm, ffn_w_up, ffn_conv_w, ffn_conv_b, ffn_w_down, final_norm, loss_target, m_attn_norm, m_w_in, m_ssd_conv_w, m_ssd_conv_b, m_ssd_dt_bias, m_ssd_a_log, m_ssd_d, m_ssd_norm, m_pool_w, m_pool_scale, m_mla_q_norm, m_mla_w_uq, m_mla_kv_norm, m_mla_w_ukv, m_w_out, m_ffn_norm, m_ffn_w_up, m_ffn_conv_w, m_ffn_conv_b, m_ffn_w_down, m_final_norm, v_attn_norm, v_w_in, v_ssd_conv_w, v_ssd_conv_b, v_ssd_dt_bias, v_ssd_a_log, v_ssd_d, v_ssd_norm, v_pool_w, v_pool_scale, v_mla_q_norm, v_mla_w_uq, v_mla_kv_norm, v_mla_w_ukv, v_w_out, v_ffn_norm, v_ffn_w_up, v_ffn_conv_w, v_ffn_conv_b, v_ffn_w_down, v_final_norm):
    a = locals()
    Wt = {k: a[k] for k in ALL_W}
    Mo = {k: a["m_" + k] for k in ALL_W}
    Vo = {k: a["v_" + k] for k in ALL_W}
    Bl, S, _ = x.shape
    T = Bl * S

    names = list(SHARDED)
    conv = ("ssd_conv_w", "ffn_conv_w")

    def blocks_of(l):
        return [Wt[k][l] if k in conv else Wt[k][l].astype(BF16) for k in names]

    def layer_weights(l, gathered):
        full = {k: _from_owner_major(g8, SHARDED[k] - 1) for k, g8 in zip(names, gathered)}
        return {
            "attn_norm": attn_norm[l].reshape(1, D), "w_in": _perm_w_in(full["w_in"]),
            "ssd_conv_w": full["ssd_conv_w"], "ssd_conv_b": ssd_conv_b[l].reshape(1, SSD_XBC),
            "ssd_dt_bias": _lane_pad(ssd_dt_bias[l]), "ssd_a_log": _lane_pad(ssd_a_log[l]),
            "ssd_d": jnp.repeat(ssd_d[l].reshape(NPAIR, 2), SSD_P, axis=1), "ssd_norm": ssd_norm[l].reshape(1, SSD_W),
            "pool_w": pool_w[l].astype(BF16), "pool_scale": pool_scale[l].reshape(1, POOL_W),
            "mla_q_norm": mla_q_norm[l].reshape(1, MLA_QR), "mla_w_uq": _perm_w_uq(full["mla_w_uq"]),
            "mla_kv_norm": mla_kv_norm[l].reshape(1, MLA_KVR), "mla_w_ukv": _perm_w_ukv(full["mla_w_ukv"]),
            "w_out": _perm_w_out(full["w_out"]), "ffn_norm": ffn_norm[l].reshape(1, D),
            "ffn_w_up": full["ffn_w_up"], "ffn_conv_w": full["ffn_conv_w"],
            "ffn_conv_b": ffn_conv_b[l].reshape(1, 2 * DFF), "ffn_w_down": full["ffn_w_down"]}

    pos = positions.astype(F32).reshape(T, 1)
    inv_freq = ROPE_THETA ** (-jnp.arange(0, MLA_ROPE, 2, dtype=F32) / MLA_ROPE)
    invf = jnp.pad(jnp.concatenate([inv_freq, inv_freq]), (PE_LO, LANES - PE_HI)).reshape(1, LANES)
    cs, sn = rope_tables(pos, invf, name="rope_tables")

    layers = [layer_weights(0, all_gather_many(blocks_of(0), name="gather_weights_l0"))]
    xc = x.reshape(T, D)
    saved = []
    for l in range(DEPTH):
        xc, sv, gathered = _layer_fwd(l, xc, layers[l], cs, sn, Bl, next_blocks=blocks_of(l + 1) if l + 1 < DEPTH else None)
        saved.append(sv)
        if gathered is not None:
            layers.append(layer_weights(l + 1, gathered))
    dx, dxb, g_final, loss_part = final_loss(xc, final_norm.reshape(1, D), loss_target.reshape(T, D), name="final_loss")

    grads = [None] * DEPTH
    reduced = {}

    def record(l, keys, red):
        for a, k in enumerate(keys):
            reduced[(l, k)] = (red[0][a], red[1][a])

    later_g8 = None
    for l in reversed(range(DEPTH)):
        dx, dxb, grads[l], red_late, red_early = _layer_bwd(l, dx, dxb, layers[l], saved[l], cs, sn, Bl, later_g8=later_g8)
        if red_late is not None:
            record(l + 1, LATE, red_late)
        record(l, EARLY, red_early)
        later_g8 = _by_owner(grads[l], LATE)
    loss = lax.psum(loss_part[0, 0], AXES)
    sums = _chip_sums(later_g8, run_stage(stage_rs_sibling(later_g8), name="rs_sibling_l0"), "rs_add_l0_")
    record(0, LATE, ([s32 for s32, _ in sums], run_stage(stage_rs_chips([sb for _, sb in sums]), name="rs_chips_l0")))

    out_g, out_d, out_m, out_v = {}, {}, {}, {}
    for k in names:
        outs = None
        for l in reversed(range(DEPTH)):
            s32, recv = reduced[(l, k)]
            outs = adamw_sharded(Wt[k], Mo[k], Vo[k], s32, recv, l, outs, name=f"adamw_l{l}_{k}")
        out_g[k], out_d[k], out_m[k], out_v[k] = (o.reshape(Wt[k].shape) for o in outs)

    part = {k: g_final.reshape(D) if k == "final_norm" else
            jnp.stack([grads[l][k].reshape(Wt[k].shape[1:]) for l in range(DEPTH)]) for k in SMALL}
    pg, layout = _pack_small(part)
    pw, _ = _pack_small(Wt)
    pm, _ = _pack_small(Mo)
    pv, _ = _pack_small(Vo)
    (pg8,) = all_gather_many([pg], name="gather_small_grads")
    sg, sd, sm, sv_ = adamw(pw, pm, pv, [pg8], name="adamw_small")
    for dst, buf in ((out_g, sg), (out_d, sd), (out_m, sm), (out_v, sv_)):
        dst.update(_unpack_small(buf, layout))

    return (loss, dx.reshape(Bl, S, D), *[out_g[k] for k in ALL_W], *[out_d[k] for k in ALL_W],
            *[out_m[k] for k in ALL_W], *[out_v[k] for k in ALL_W])
```

```python
import functools
import math

import jax
import jax.numpy as jnp
from jax import lax
from jax.experimental import pallas as pl
from jax.experimental.pallas import tpu as pltpu

F32, BF16 = jnp.float32, jnp.bfloat16
SDS = jax.ShapeDtypeStruct
MESH = pl.DeviceIdType.MESH
AXES = ("x", "y", "c")
N_DEV = 8

D = 1024
EPS = 1e-6
SSD_HEADS, SSD_P, SSD_W, SSD_G, SSD_N, SSD_K, SSD_L, SSD_XBC = 16, 64, 1024, 2, 128, 4, 128, 1536
POOL_G, POOL_D, POOL_W, POOL_WIN = 4, 128, 512, (2, 4, 8, 16)
MLA_H, MLA_QR, MLA_KVR, MLA_NOPE, MLA_ROPE, MLA_V, MLA_QK = 8, 384, 256, 64, 32, 64, 96
ROPE_THETA = 10000.0
MIX = 2048
DFF, FFN_K = 2816, 3
DEPTH = 2
ADAM_LR, ADAM_B1, ADAM_B2, ADAM_EPS, ADAM_WD, ADAM_STEP = 0.001, 0.9, 0.999, 1e-08, 0.01, 10

Z0, XBC0, U0, CQ0, CKV0, DT0, KPE0, PW = 0, 1024, 2560, 3072, 3584, 3840, 3968, 4096
LANES = 128
YCAT = SSD_W + POOL_W + MLA_H * LANES
NEG = -1e30
VMEM_LIMIT = 56 * 1024 * 1024
MM_ROW_TILE, MM_LANE_TILE, MM_FULL_K = 1024, 1408, 2816


def _tile(n, pref, mult):
    if n <= pref:
        return n
    for d in range(pref, 0, -mult):
        if d % mult == 0 and n % d == 0:
            return d
    return n


def _dg(a, b, ca, cb, prec=None):
    return lax.dot_general(a, b, (((ca,), (cb,)), ((), ())), preferred_element_type=F32, precision=prec)


def _nn(a, b):
    return _dg(a.astype(BF16), b.astype(BF16), 1, 0)


def _nt(a, b):
    return _dg(a.astype(BF16), b.astype(BF16), 1, 1)


def _tn(a, b):
    return _dg(a.astype(BF16), b.astype(BF16), 0, 0)


def _sig(x):
    return jax.nn.sigmoid(x)


def _silu(x):
    return x * _sig(x)


def _dsilu(x):
    s = _sig(x)
    return s * (1.0 + x * (1.0 - s))


ANY = pl.BlockSpec(memory_space=pl.ANY)


def _pc(body, *, name, grid, in_specs, out_specs, out_shape, scratch=(), comm=None):
    params = pltpu.CompilerParams(vmem_limit_bytes=VMEM_LIMIT)
    if comm is None:
        return pl.pallas_call(body, name=name, grid=grid, in_specs=in_specs, out_specs=out_specs, out_shape=out_shape,
                              scratch_shapes=list(scratch), compiler_params=params)
    single = not isinstance(out_shape, (list, tuple))
    o_specs = [out_specs] if single else list(out_specs)
    o_shape = [out_shape] if single else list(out_shape)
    ni, no, ns = len(in_specs), len(o_specs), len(scratch)
    nci, nco = len(comm["ins"]), len(comm["out_shape"])

    def fused(*refs):
        ins, cins = refs[:ni], refs[ni:ni + nci]
        outs, couts = refs[ni + nci:ni + nci + no], refs[ni + nci + no:ni + nci + no + nco]
        scr = refs[ni + nci + no + nco:ni + nci + no + nco + ns]
        send_sems, recv_sems = refs[-2:]
        copies = comm["copies"](cins, couts, send_sems, recv_sems)
        first = functools.reduce(jnp.logical_and, [pl.program_id(d) == 0 for d in range(len(grid))])
        last = functools.reduce(jnp.logical_and, [pl.program_id(d) == grid[d] - 1 for d in range(len(grid))])

        @pl.when(first)
        def _():
            for cp in copies:
                cp.start()

        body(*ins, *outs, *scr)

        @pl.when(last)
        def _():
            for cp in copies:
                cp.wait()

    call = pl.pallas_call(
        fused, name=name, grid=grid, in_specs=list(in_specs) + [ANY] * nci, out_specs=o_specs + [ANY] * nco,
        out_shape=o_shape + list(comm["out_shape"]),
        scratch_shapes=list(scratch) + [pltpu.SemaphoreType.DMA((comm["sems"],)), pltpu.SemaphoreType.DMA((comm["sems"],))],
        input_output_aliases={ni + a: no + a for a in range(nci)} if comm.get("alias") else {},
        compiler_params=params)

    def run(*args):
        res = call(*args, *comm["ins"])
        comm["result"] = list(res[no:])
        return res[0] if single else list(res[:no])

    return run


def _rsum(x):
    return jnp.sum(x, axis=1, keepdims=True)


def _csum(x):
    return jnp.sum(x, axis=0, keepdims=True)


def mm(a, b, *, ta=False, tb=False, add=None, out_dtype=F32, b_k0=0, comm=None, name):
    M, K = (a.shape[1], a.shape[0]) if ta else a.shape
    N = b.shape[0] if tb else b.shape[1]
    assert tb or b_k0 == 0
    tm = _tile(M, MM_LANE_TILE, LANES) if ta else _tile(M, MM_ROW_TILE, 8)
    tn = _tile(N, MM_LANE_TILE, LANES)
    if ta:
        tk = _tile(K, MM_ROW_TILE, 8)
    else:
        tk = K if K <= MM_FULL_K else _tile(K, 2048, LANES)
    nk = K // tk

    def body(*refs):
        if add is None:
            a_ref, b_ref, o_ref = refs[:3]
        else:
            a_ref, b_ref, add_ref, o_ref = refs[:4]
        part = _dg(a_ref[...].astype(BF16), b_ref[...].astype(BF16), 0 if ta else 1, 1 if tb else 0)

        def finish(r):
            if add is not None:
                r = r + add_ref[...].astype(F32)
            o_ref[...] = r.astype(out_dtype)

        if nk == 1:
            finish(part)
            return
        acc = refs[-1]
        k = pl.program_id(2)

        @pl.when(k == 0)
        def _():
            acc[...] = part

        @pl.when(k > 0)
        def _():
            acc[...] += part

        @pl.when(k == nk - 1)
        def _():
            finish(acc[...])

    a_spec = pl.BlockSpec((tk, tm), lambda i, j, k: (k, i)) if ta else pl.BlockSpec((tm, tk), lambda i, j, k: (i, k))
    assert b_k0 % tk == 0
    kb0 = b_k0 // tk
    b_spec = pl.BlockSpec((tn, tk), lambda i, j, k: (j, kb0 + k)) if tb else pl.BlockSpec((tk, tn), lambda i, j, k: (k, j))
    o_spec = pl.BlockSpec((tm, tn), lambda i, j, k: (i, j))
    ins, specs = [a, b], [a_spec, b_spec]
    if add is not None:
        ins.append(add)
        specs.append(o_spec)
    return _pc(body, name=name, grid=(M // tm, N // tn, nk), in_specs=specs, out_specs=o_spec,
               out_shape=SDS((M, N), out_dtype), scratch=[pltpu.VMEM((tm, tn), F32)] if nk > 1 else [], comm=comm)(*ins)


def rms_fwd(x, g, *, col0=0, width=None, name):
    T = x.shape[0]
    W = width or x.shape[1]
    tm = _tile(T, 512, 8)

    def body(x_ref, g_ref, o_ref):
        v = x_ref[...]
        r = lax.rsqrt(jnp.mean(v * v, axis=1, keepdims=True) + EPS)
        o_ref[...] = ((v * r) * g_ref[...]).astype(BF16)

    return _pc(body, name=name, grid=(T // tm,),
               in_specs=[pl.BlockSpec((tm, W), lambda i: (i, col0 // W)), pl.BlockSpec((1, W), lambda i: (0, 0))],
               out_specs=pl.BlockSpec((tm, W), lambda i: (i, 0)), out_shape=SDS((T, W), BF16))(x, g)


def rms_bwd(x, g, dh, *, col0=0, width=None, add=None, name):
    T = x.shape[0]
    W = width or x.shape[1]
    tm = _tile(T, 512, 8)

    def body(*refs):
        if add is None:
            x_ref, g_ref, dh_ref, dx_ref, dg_ref = refs
        else:
            x_ref, g_ref, dh_ref, add_ref, dx_ref, dxb_ref, dg_ref = refs
        v = x_ref[...]
        r = lax.rsqrt(jnp.mean(v * v, axis=1, keepdims=True) + EPS)
        xh = v * r
        d = dh_ref[...].astype(F32)
        dxh = d * g_ref[...]
        dx = r * (dxh - xh * jnp.mean(dxh * xh, axis=1, keepdims=True))
        if add is not None:
            dx = dx + add_ref[...]
            dxb_ref[...] = dx.astype(BF16)
        dx_ref[...] = dx.astype(dx_ref.dtype)

        @pl.when(pl.program_id(0) == 0)
        def _():
            dg_ref[...] = jnp.zeros_like(dg_ref)

        dg_ref[...] += _csum(d * xh)

    row = pl.BlockSpec((tm, W), lambda i: (i, 0))
    vec = pl.BlockSpec((1, W), lambda i: (0, 0))
    ins = [x, g, dh] + ([] if add is None else [add])
    specs = [pl.BlockSpec((tm, W), lambda i: (i, col0 // W)), vec, row] + ([] if add is None else [row])
    if add is None:
        return _pc(body, name=name, grid=(T // tm,), in_specs=specs, out_specs=[row, vec],
                   out_shape=[SDS((T, W), BF16), SDS((1, W), F32)])(*ins)
    return _pc(body, name=name, grid=(T // tm,), in_specs=specs, out_specs=[row, row, vec],
               out_shape=[SDS((T, W), F32), SDS((T, W), BF16), SDS((1, W), F32)])(*ins)


def gated_rms_fwd(y, proj, g, *, name):
    T = y.shape[0]
    tm = _tile(T, 512, 8)

    def body(y_ref, z_ref, g_ref, o_ref):
        v = y_ref[...] * _silu(z_ref[...])
        r = lax.rsqrt(jnp.mean(v * v, axis=1, keepdims=True) + EPS)
        o_ref[...] = ((v * r) * g_ref[...]).astype(BF16)

    row = pl.BlockSpec((tm, SSD_W), lambda i: (i, 0))
    return _pc(body, name=name, grid=(T // tm,), in_specs=[row, row, pl.BlockSpec((1, SSD_W), lambda i: (0, 0))],
               out_specs=row, out_shape=SDS((T, SSD_W), BF16))(y, proj, g)


def gated_rms_bwd(y, proj, g, dycat, *, name):
    T = y.shape[0]
    tm = _tile(T, 512, 8)

    def body(y_ref, z_ref, g_ref, d_ref, dy_ref, dz_ref, dg_ref):
        yv, z = y_ref[...], z_ref[...]
        sz = _silu(z)
        v = yv * sz
        r = lax.rsqrt(jnp.mean(v * v, axis=1, keepdims=True) + EPS)
        vh = v * r
        d = d_ref[...]
        dvh = d * g_ref[...]
        dv = r * (dvh - vh * jnp.mean(dvh * vh, axis=1, keepdims=True))
        dy_ref[...] = dv * sz
        dz_ref[...] = (dv * yv * _dsilu(z)).astype(BF16)

        @pl.when(pl.program_id(0) == 0)
        def _():
            dg_ref[...] = jnp.zeros_like(dg_ref)

        dg_ref[...] += _csum(d * vh)

    row = pl.BlockSpec((tm, SSD_W), lambda i: (i, 0))
    vec = pl.BlockSpec((1, SSD_W), lambda i: (0, 0))
    return _pc(body, name=name, grid=(T // tm,), in_specs=[row, row, vec, row], out_specs=[row, row, vec],
               out_shape=[SDS((T, SSD_W), F32), SDS((T, SSD_W), BF16), SDS((1, SSD_W), F32)])(y, proj, g, dycat)


def final_loss(x, g, tgt, *, name):
    T = x.shape[0]
    tm = _tile(T, 512, 8)

    def body(x_ref, g_ref, t_ref, dx_ref, dxb_ref, dg_ref, l_ref):
        v = x_ref[...]
        gg = g_ref[...]
        r = lax.rsqrt(jnp.mean(v * v, axis=1, keepdims=True) + EPS)
        xh = v * r
        err = xh * gg - t_ref[...]
        part = 0.5 * _csum(jnp.mean(err * err, axis=1, keepdims=True))
        d = err * (1.0 / D)
        dxh = d * gg
        dx = r * (dxh - xh * jnp.mean(dxh * xh, axis=1, keepdims=True))
        dx_ref[...] = dx
        dxb_ref[...] = dx.astype(BF16)

        @pl.when(pl.program_id(0) == 0)
        def _():
            dg_ref[...] = jnp.zeros_like(dg_ref)
            l_ref[...] = jnp.zeros_like(l_ref)

        dg_ref[...] += _csum(d * xh)
        l_ref[...] += jnp.broadcast_to(part, (1, LANES))

    row = pl.BlockSpec((tm, D), lambda i: (i, 0))
    vec = pl.BlockSpec((1, D), lambda i: (0, 0))
    return _pc(body, name=name, grid=(T // tm,), in_specs=[row, vec, row],
               out_specs=[row, row, vec, pl.BlockSpec((1, LANES), lambda i: (0, 0))],
               out_shape=[SDS((T, D), F32), SDS((T, D), BF16), SDS((1, D), F32), SDS((1, LANES), F32)])(x, g, tgt)


HALO = 8


def _prev_map(ts, col):
    return lambda b, i, j: (b, jnp.maximum(i * (ts // HALO) - 1, 0), col(j))


def _next_map(ts, n_halo_blocks, col):
    return lambda b, i, j: (b, jnp.minimum((i + 1) * (ts // HALO), n_halo_blocks - 1), col(j))


def _row_chunks(ts, rows):
    rows = min(rows, ts)
    return [(r, rows) for r in range(0, ts, rows)]


def _conv_rows(ext, w_ref, b_ref, r0, n, K):
    win = ext[r0:r0 + HALO + n, :]
    taps = [pltpu.roll(win, K - 1 - k, 0)[HALO:HALO + n] if k < K - 1 else win[HALO:HALO + n] for k in range(K)]
    acc = b_ref[...] + w_ref[0:1, :] * taps[0]
    for k in range(1, K):
        acc = acc + w_ref[k:k + 1, :] * taps[k]
    return acc, taps


def _conv_t_rows(ext2, w_ref, r0, n, K):
    win = ext2[r0:r0 + n + HALO, :]
    dx = w_ref[K - 1:K, :] * win[0:n]
    for k in range(K - 1):
        dx = dx + w_ref[k:k + 1, :] * pltpu.roll(win, n + HALO - (K - 1 - k), 0)[0:n]
    return dx


def _sum8(x):
    acc = x[0:8]
    for r in range(8, x.shape[0], 8):
        acc = acc + x[r:r + 8]
    return acc


def conv_silu_fwd(proj3, w, b, *, name):
    Bl, S, _ = proj3.shape
    C, K = SSD_XBC, SSD_K
    ts, tc = _tile(S, 512, 8), 512
    c0 = XBC0 // tc

    def body(xp_ref, x_ref, w_ref, b_ref, o_ref, ext):
        i = pl.program_id(1)
        ext[0:HALO, :] = jnp.where(i > 0, xp_ref[0], 0.0)
        ext[HALO:HALO + ts, :] = x_ref[0]
        for r0, n in _row_chunks(ts, 32):
            acc, _ = _conv_rows(ext, w_ref, b_ref, r0, n, K)
            o_ref[0, r0:r0 + n, :] = _silu(acc)

    return _pc(body, name=name, grid=(Bl, S // ts, C // tc),
               in_specs=[pl.BlockSpec((1, HALO, tc), _prev_map(ts, lambda j: c0 + j)),
                         pl.BlockSpec((1, ts, tc), lambda b, i, j: (b, i, c0 + j)),
                         pl.BlockSpec((K, tc), lambda b, i, j: (0, j)),
                         pl.BlockSpec((1, tc), lambda b, i, j: (0, j))],
               out_specs=pl.BlockSpec((1, ts, tc), lambda b, i, j: (b, i, j)),
               out_shape=SDS((Bl, S, C), F32), scratch=[pltpu.VMEM((HALO + ts, tc), F32)])(proj3, proj3, w, b)


def conv_silu_bwd(proj3, w, b, dact, *, name):
    Bl, S, _ = proj3.shape
    C, K = SSD_XBC, SSD_K
    ts, tc = _tile(S, 512, 8), 512
    c0 = XBC0 // tc
    ns = S // ts

    def body(xp_ref, x_ref, xn_ref, d_ref, dn_ref, w_ref, b_ref, dx_ref, dw_ref, db_ref, ext, ext2):
        bb, i = pl.program_id(1), pl.program_id(2)
        last = i == ns - 1
        ext[0:HALO, :] = jnp.where(i > 0, xp_ref[0], 0.0)
        ext[HALO:HALO + ts, :] = x_ref[0]
        ext[HALO + ts:2 * HALO + ts, :] = jnp.where(last, 0.0, xn_ref[0])
        dw = [jnp.zeros((8, tc), F32) for _ in range(K)]
        db = jnp.zeros((8, tc), F32)
        for r0, n in _row_chunks(ts, 16) + [(ts, HALO)]:
            acc, taps = _conv_rows(ext, w_ref, b_ref, r0, n, K)
            d = d_ref[0, r0:r0 + n, :] if r0 < ts else jnp.where(last, 0.0, dn_ref[0])
            du = d * _dsilu(acc)
            ext2[r0:r0 + n, :] = du
            if r0 < ts:
                dw = [a + _sum8(du * t) for a, t in zip(dw, taps)]
                db = db + _sum8(du)
        for r0, n in _row_chunks(ts, 32):
            dx_ref[0, r0:r0 + n, :] = _conv_t_rows(ext2, w_ref, r0, n, K).astype(BF16)

        @pl.when((bb == 0) & (i == 0))
        def _():
            dw_ref[...] = jnp.zeros_like(dw_ref)
            db_ref[...] = jnp.zeros_like(db_ref)

        for k in range(K):
            dw_ref[k:k + 1, :] += _csum(dw[k])
        db_ref[...] += _csum(db)

    nhb = S // HALO
    cx = lambda j: c0 + j
    cj = lambda j: j
    return _pc(body, name=name, grid=(C // tc, Bl, ns),
               in_specs=[pl.BlockSpec((1, HALO, tc), lambda j, b, i: _prev_map(ts, cx)(b, i, j)),
                         pl.BlockSpec((1, ts, tc), lambda j, b, i: (b, i, c0 + j)),
                         pl.BlockSpec((1, HALO, tc), lambda j, b, i: _next_map(ts, nhb, cx)(b, i, j)),
                         pl.BlockSpec((1, ts, tc), lambda j, b, i: (b, i, j)),
                         pl.BlockSpec((1, HALO, tc), lambda j, b, i: _next_map(ts, nhb, cj)(b, i, j)),
                         pl.BlockSpec((K, tc), lambda j, b, i: (0, j)),
                         pl.BlockSpec((1, tc), lambda j, b, i: (0, j))],
               out_specs=[pl.BlockSpec((1, ts, tc), lambda j, b, i: (b, i, j)),
                          pl.BlockSpec((K, tc), lambda j, b, i: (0, j)),
                          pl.BlockSpec((1, tc), lambda j, b, i: (0, j))],
               out_shape=[SDS((Bl, S, C), BF16), SDS((K, C), F32), SDS((1, C), F32)],
               scratch=[pltpu.VMEM((2 * HALO + ts, tc), F32), pltpu.VMEM((HALO + ts, tc), F32)],
               )(proj3, proj3, proj3, dact, dact, w, b)


def ffn_act_fwd(pre3, w, b, *, name):
    Bl, S, _ = pre3.shape
    K = FFN_K
    ts, tc = _tile(S, 512, 8), 256
    nj = DFF // tc

    def body(gp_ref, g_ref, vp_ref, v_ref, wg_ref, wv_ref, bg_ref, bv_ref, o_ref, eg, ev):
        i = pl.program_id(1)
        for p_ref, m_ref, ext in ((gp_ref, g_ref, eg), (vp_ref, v_ref, ev)):
            ext[0:HALO, :] = jnp.where(i > 0, p_ref[0], 0.0)
            ext[HALO:HALO + ts, :] = m_ref[0]
        for r0, n in _row_chunks(ts, 64):
            ug, _ = _conv_rows(eg, wg_ref, bg_ref, r0, n, K)
            uv, _ = _conv_rows(ev, wv_ref, bv_ref, r0, n, K)
            o_ref[0, r0:r0 + n, :] = (_silu(ug) * uv).astype(BF16)

    main = lambda off: pl.BlockSpec((1, ts, tc), lambda b, i, j: (b, i, off + j))
    prev = lambda off: pl.BlockSpec((1, HALO, tc), _prev_map(ts, lambda j: off + j))
    wsp = lambda off: pl.BlockSpec((K, tc), lambda b, i, j: (0, off + j))
    bsp = lambda off: pl.BlockSpec((1, tc), lambda b, i, j: (0, off + j))
    return _pc(body, name=name, grid=(Bl, S // ts, nj),
               in_specs=[prev(0), main(0), prev(nj), main(nj), wsp(0), wsp(nj), bsp(0), bsp(nj)],
               out_specs=pl.BlockSpec((1, ts, tc), lambda b, i, j: (b, i, j)),
               out_shape=SDS((Bl, S, DFF), BF16),
               scratch=[pltpu.VMEM((HALO + ts, tc), F32), pltpu.VMEM((HALO + ts, tc), F32)],
               )(pre3, pre3, pre3, pre3, w, w, b, b)


def ffn_act_bwd(pre3, w, b, dact, *, comm=None, name):
    Bl, S, _ = pre3.shape
    K = FFN_K
    ts, tc = _tile(S, 512, 8), 256
    nj = DFF // tc
    ns = S // ts

    def body(gp_ref, g_ref, gn_ref, vp_ref, v_ref, vn_ref, d_ref, dn_ref, wg_ref, wv_ref, bg_ref, bv_ref,
             dg_ref, dv_ref, dwg_ref, dwv_ref, dbg_ref, dbv_ref, eg, ev, e2g, e2v):
        bb, i = pl.program_id(1), pl.program_id(2)
        last = i == ns - 1
        for p_ref, m_ref, n_ref, ext in ((gp_ref, g_ref, gn_ref, eg), (vp_ref, v_ref, vn_ref, ev)):
            ext[0:HALO, :] = jnp.where(i > 0, p_ref[0], 0.0)
            ext[HALO:HALO + ts, :] = m_ref[0]
            ext[HALO + ts:2 * HALO + ts, :] = jnp.where(last, 0.0, n_ref[0])
        zero8 = jnp.zeros((8, tc), F32)
        dwg, dwv, dbg, dbv = [zero8] * K, [zero8] * K, zero8, zero8
        for r0, n in _row_chunks(ts, 32) + [(ts, HALO)]:
            ug, tg = _conv_rows(eg, wg_ref, bg_ref, r0, n, K)
            uv, tv = _conv_rows(ev, wv_ref, bv_ref, r0, n, K)
            d = d_ref[0, r0:r0 + n, :] if r0 < ts else jnp.where(last, 0.0, dn_ref[0])
            dug = d * uv * _dsilu(ug)
            duv = d * _silu(ug)
            e2g[r0:r0 + n, :] = dug
            e2v[r0:r0 + n, :] = duv
            if r0 < ts:
                dwg = [a + _sum8(dug * t) for a, t in zip(dwg, tg)]
                dwv = [a + _sum8(duv * t) for a, t in zip(dwv, tv)]
                dbg, dbv = dbg + _sum8(dug), dbv + _sum8(duv)
        for w_ref, e2, o_ref in ((wg_ref, e2g, dg_ref), (wv_ref, e2v, dv_ref)):
            for r0, n in _row_chunks(ts, 64):
                o_ref[0, r0:r0 + n, :] = _conv_t_rows(e2, w_ref, r0, n, K).astype(BF16)

        @pl.when((bb == 0) & (i == 0))
        def _():
            for r in (dwg_ref, dwv_ref, dbg_ref, dbv_ref):
                r[...] = jnp.zeros_like(r)

        for dw_ref, dw, db_ref, db in ((dwg_ref, dwg, dbg_ref, dbg), (dwv_ref, dwv, dbv_ref, dbv)):
            for k in range(K):
                dw_ref[k:k + 1, :] += _csum(dw[k])
            db_ref[...] += _csum(db)

    nhb = S // HALO
    main = lambda off: pl.BlockSpec((1, ts, tc), lambda j, b, i: (b, i, off + j))
    prev = lambda off: pl.BlockSpec((1, HALO, tc), lambda j, b, i: _prev_map(ts, lambda jj: off + jj)(b, i, j))
    nxt = lambda off: pl.BlockSpec((1, HALO, tc), lambda j, b, i: _next_map(ts, nhb, lambda jj: off + jj)(b, i, j))
    wsp = lambda off: pl.BlockSpec((K, tc), lambda j, b, i: (0, off + j))
    bsp = lambda off: pl.BlockSpec((1, tc), lambda j, b, i: (0, off + j))
    outs = _pc(body, name=name, grid=(nj, Bl, ns),
               in_specs=[prev(0), main(0), nxt(0), prev(nj), main(nj), nxt(nj), main(0), nxt(0),
                         wsp(0), wsp(nj), bsp(0), bsp(nj)],
               out_specs=[main(0), main(0), wsp(0), wsp(0), bsp(0), bsp(0)],
               out_shape=[SDS((Bl, S, DFF), BF16), SDS((Bl, S, DFF), BF16), SDS((K, DFF), F32), SDS((K, DFF), F32),
                          SDS((1, DFF), F32), SDS((1, DFF), F32)],
               scratch=[pltpu.VMEM((2 * HALO + ts, tc), F32), pltpu.VMEM((2 * HALO + ts, tc), F32),
                        pltpu.VMEM((HALO + ts, tc), F32), pltpu.VMEM((HALO + ts, tc), F32)],
               comm=comm)(pre3, pre3, pre3, pre3, pre3, pre3, dact, dact, w, w, b, b)
    return outs


PHALO = 16


def _pool_window_sums(win, trailing):
    rows = win.shape[0]
    out, s = [], win
    for w in POOL_WIN:
        half = w // 2
        s = s + pltpu.roll(s, half if trailing else rows - half, 0)
        out.append(s)
    return out


def _pick(g, vals):
    r = vals[-1]
    for k in range(len(vals) - 2, -1, -1):
        r = jnp.where(g == k, vals[k], r)
    return r


def _pool_count(g, i, ts, rows, r0=0):
    t = (i * ts + r0 + lax.broadcasted_iota(jnp.int32, (rows, 1), 0) + 1).astype(F32)
    return jnp.minimum(t, _pick(g, [float(w) for w in POOL_WIN]))


def _fill_pool_ext(up_ref, u_ref, ext, i, ts):
    ext[0:PHALO, :] = jnp.where(i > 0, up_ref[0], 0.0)
    ext[PHALO:PHALO + ts, :] = u_ref[0]


def _pooled_rows(ext, g, i, ts, r0, n):
    win = ext[r0:r0 + n + PHALO, :]
    sums = _pool_window_sums(win, True)
    return _pick(g, sums)[PHALO:PHALO + n] / _pool_count(g, i, ts, n, r0) - win[PHALO:PHALO + n]


def pool_fwd(proj3, pool_w, scale, *, name):
    Bl, S, _ = proj3.shape
    ts = _tile(S, 512, 16)
    c0 = U0 // POOL_D

    def body(up_ref, u_ref, w_ref, s_ref, o_ref, ext):
        i, g = pl.program_id(1), pl.program_id(2)
        _fill_pool_ext(up_ref, u_ref, ext, i, ts)
        wm, sc = w_ref[0], s_ref[...]
        for r0, n in _row_chunks(ts, 128):
            o_ref[0, r0:r0 + n, :] = (_nn(_pooled_rows(ext, g, i, ts, r0, n), wm) * sc).astype(BF16)

    return _pc(body, name=name, grid=(Bl, S // ts, POOL_G),
               in_specs=[pl.BlockSpec((1, PHALO, POOL_D), lambda b, i, g: (b, jnp.maximum(i * (ts // PHALO) - 1, 0), c0 + g)),
                         pl.BlockSpec((1, ts, POOL_D), lambda b, i, g: (b, i, c0 + g)),
                         pl.BlockSpec((1, POOL_D, POOL_D), lambda b, i, g: (g, 0, 0)),
                         pl.BlockSpec((1, POOL_D), lambda b, i, g: (0, g))],
               out_specs=pl.BlockSpec((1, ts, POOL_D), lambda b, i, g: (b, i, g)),
               out_shape=SDS((Bl, S, POOL_W), BF16), scratch=[pltpu.VMEM((PHALO + ts, POOL_D), F32)],
               )(proj3, proj3, pool_w, scale)


def pool_bwd(proj3, pool_w, scale, dycat3, *, name):
    Bl, S, _ = proj3.shape
    ts = _tile(S, 512, 16)
    ns = S // ts
    c0 = U0 // POOL_D
    d0 = SSD_W // POOL_D
    nhb = S // PHALO

    def body(up_ref, u_ref, d_ref, dn_ref, w_ref, s_ref, du_ref, dw_ref, ds_ref, ext, ext2):
        g, bb, i = pl.program_id(0), pl.program_id(1), pl.program_id(2)
        last = i == ns - 1
        _fill_pool_ext(up_ref, u_ref, ext, i, ts)
        wm = w_ref[0]
        sc = s_ref[...]
        dwa = jnp.zeros((POOL_D, POOL_D), F32)
        dsa = jnp.zeros((8, POOL_D), F32)
        dpools = []
        for r0, n in _row_chunks(ts, 128):
            pooled = _pooled_rows(ext, g, i, ts, r0, n)
            dy = d_ref[0, r0:r0 + n, :]
            dp = dy * sc
            dpool = _nt(dp, wm)
            dpools.append(dpool)
            ext2[r0:r0 + n, :] = dpool / _pool_count(g, i, ts, n, r0)
            dwa = dwa + _tn(pooled, dp)
            dsa = dsa + _sum8(dy * _nn(pooled, wm))
        dpool_n = _nt(jnp.where(last, 0.0, dn_ref[0]) * sc, wm)
        ext2[ts:ts + PHALO, :] = dpool_n / _pool_count(g, i + 1, ts, PHALO)
        for (r0, n), dpool in zip(_row_chunks(ts, 128), dpools):
            sums = _pool_window_sums(ext2[r0:r0 + n + PHALO, :], False)
            du_ref[0, r0:r0 + n, :] = (_pick(g, sums)[0:n] - dpool).astype(BF16)

        @pl.when((bb == 0) & (i == 0))
        def _():
            dw_ref[...] = jnp.zeros_like(dw_ref)
            ds_ref[...] = jnp.zeros_like(ds_ref)

        dw_ref[0] += dwa
        ds_ref[...] += _csum(dsa)

    return _pc(body, name=name, grid=(POOL_G, Bl, ns),
               in_specs=[pl.BlockSpec((1, PHALO, POOL_D), lambda g, b, i: (b, jnp.maximum(i * (ts // PHALO) - 1, 0), c0 + g)),
                         pl.BlockSpec((1, ts, POOL_D), lambda g, b, i: (b, i, c0 + g)),
                         pl.BlockSpec((1, ts, POOL_D), lambda g, b, i: (b, i, d0 + g)),
                         pl.BlockSpec((1, PHALO, POOL_D), lambda g, b, i: (b, jnp.minimum((i + 1) * (ts // PHALO), nhb - 1), d0 + g)),
                         pl.BlockSpec((1, POOL_D, POOL_D), lambda g, b, i: (g, 0, 0)),
                         pl.BlockSpec((1, POOL_D), lambda g, b, i: (0, g))],
               out_specs=[pl.BlockSpec((1, ts, POOL_D), lambda g, b, i: (b, i, g)),
                          pl.BlockSpec((1, POOL_D, POOL_D), lambda g, b, i: (g, 0, 0)),
                          pl.BlockSpec((1, POOL_D), lambda g, b, i: (0, g))],
               out_shape=[SDS((Bl, S, POOL_W), BF16), SDS((POOL_G, POOL_D, POOL_D), F32), SDS((1, POOL_W), F32)],
               scratch=[pltpu.VMEM((PHALO + ts, POOL_D), F32), pltpu.VMEM((PHALO + ts, POOL_D), F32)],
               )(proj3, proj3, dycat3, dycat3, pool_w, scale)


NPAIR = SSD_HEADS // 2


def _ssd_common(sm, bias, alog):
    L = SSD_L
    dt = jax.nn.softplus(sm + bias)
    a = -jnp.exp(alog)
    da = dt * a
    r = lax.broadcasted_iota(jnp.int32, (L, L), 0)
    c = lax.broadcasted_iota(jnp.int32, (L, L), 1)
    tri = (r >= c).astype(F32)
    cum = _dg(tri, da, 1, 0, lax.Precision.HIGHEST)
    return dt, a, cum, cum.T, r >= c


def _lanes(lo, hi, shape=(1, LANES)):
    lane = lax.broadcasted_iota(jnp.int32, shape, len(shape) - 1)
    return (lane >= lo) & (lane < hi)


def _onehot_lane(h):
    return (lax.broadcasted_iota(jnp.int32, (1, LANES), 1) == h).astype(F32)


def _split_nn(a, e):
    hi = a.astype(BF16)
    lo = (a - hi.astype(F32)).astype(BF16)
    return _dg(hi, e, 1, 0) + _dg(lo, e, 1, 0)


def _head_spread():
    r = lax.broadcasted_iota(jnp.int32, (LANES, SSD_W), 0)
    c = lax.broadcasted_iota(jnp.int32, (LANES, SSD_W), 1)
    return (c // SSD_P == r).astype(BF16)


def _pair_gather(j):
    r = lax.broadcasted_iota(jnp.int32, (LANES, LANES), 0)
    c = lax.broadcasted_iota(jnp.int32, (LANES, LANES), 1)
    return (c == 2 * j + (r >= SSD_P).astype(jnp.int32)).astype(BF16)


def ssd_fwd(xbc3, proj3, bias, alog, dskip, *, comm=None, name):
    Bl, S, _ = xbc3.shape
    L = SSD_L
    nc = S // L

    def body(xbc_ref, sm_ref, bias_ref, alog_ref, d_ref, y_ref, hin_ref, H):
        c = pl.program_id(1)

        @pl.when(c == 0)
        def _():
            H[...] = jnp.zeros_like(H)

        dt, a, cum, cumT, mask = _ssd_common(sm_ref[0], bias_ref[...], alog_ref[...])
        lo = _lanes(0, SSD_P)
        rowlo = lax.broadcasted_iota(jnp.int32, (LANES, LANES), 0) < SSD_P
        spread = _head_spread()
        dt_x = _split_nn(dt, spread)
        el_x = _split_nn(jnp.exp(cum), spread)
        wl_x = _split_nn(jnp.exp(cum[L - 1:L, :] - cum), spread)
        cb = []
        for g in range(SSD_G):
            Bg = xbc_ref[0, :, SSD_W + g * SSD_N:SSD_W + (g + 1) * SSD_N]
            Cg = xbc_ref[0, :, SSD_W + SSD_G * SSD_N + g * SSD_N:SSD_W + SSD_G * SSD_N + (g + 1) * SSD_N]
            cb.append((Bg, Cg, _nt(Cg, Bg)))
        for j in range(NPAIR):
            h0, h1 = 2 * j, 2 * j + 1
            sl = slice(j * LANES, (j + 1) * LANES)
            Bg, Cg, CB = cb[j // (NPAIR // SSD_G)]
            X = xbc_ref[0, :, sl]
            c0, c1 = cum[:, h0:h0 + 1], cum[:, h1:h1 + 1]
            r0, r1 = cumT[h0:h0 + 1, :], cumT[h1:h1 + 1, :]
            cl0, cl1 = cum[L - 1:L, h0:h0 + 1], cum[L - 1:L, h1:h1 + 1]
            Xt = X * dt_x[:, sl]
            M0 = CB * jnp.exp(jnp.where(mask, c0 - r0, NEG))
            M1 = CB * jnp.exp(jnp.where(mask, c1 - r1, NEG))
            Yd = jnp.where(lo, _nn(M0, Xt), _nn(M1, Xt))
            Hp = H[j]
            hin_ref[0, 0, j] = Hp
            Z = _nt(Cg, Hp)
            y_ref[0, :, sl] = Yd + el_x[:, sl] * Z + X * d_ref[j:j + 1, :]
            H[j] = jnp.where(rowlo, jnp.exp(cl0), jnp.exp(cl1)) * Hp + _tn(wl_x[:, sl] * Xt, Bg)

    vec = pl.BlockSpec((1, LANES), lambda b, c: (0, 0))
    return _pc(body, name=name, grid=(Bl, nc),
               in_specs=[pl.BlockSpec((1, L, SSD_XBC), lambda b, c: (b, c, 0)),
                         pl.BlockSpec((1, L, LANES), lambda b, c: (b, c, DT0 // LANES)),
                         vec, vec, pl.BlockSpec((NPAIR, LANES), lambda b, c: (0, 0))],
               out_specs=[pl.BlockSpec((1, L, SSD_W), lambda b, c: (b, c, 0)),
                          pl.BlockSpec((1, 1, NPAIR, LANES, LANES), lambda b, c: (b, c, 0, 0, 0))],
               out_shape=[SDS((Bl, S, SSD_W), F32), SDS((Bl, nc, NPAIR, LANES, LANES), F32)],
               scratch=[pltpu.VMEM((NPAIR, LANES, LANES), F32)], comm=comm)(xbc3, proj3, bias, alog, dskip)


def ssd_bwd(xbc3, proj3, hin, dy3, bias, alog, dskip, *, comm=None, name):
    Bl, S, _ = xbc3.shape
    L = SSD_L
    nc = S // L

    def body(xbc_ref, sm_ref, hin_ref, dy_ref, bias_ref, alog_ref, d_ref, dx_ref, ddt_ref, dpar_ref, dd_ref, dH, ddacc):
        bb, i = pl.program_id(0), pl.program_id(1)

        @pl.when(i == 0)
        def _():
            dH[...] = jnp.zeros_like(dH)

        @pl.when((bb == 0) & (i == 0))
        def _():
            dpar_ref[...] = jnp.zeros_like(dpar_ref)
            ddacc[...] = jnp.zeros_like(ddacc)

        sm = sm_ref[0]
        dt, a, cum, cumT, mask = _ssd_common(sm, bias_ref[...], alog_ref[...])
        maskf = mask.astype(F32)
        lo = _lanes(0, SSD_P)
        rowlo = lax.broadcasted_iota(jnp.int32, (LANES, LANES), 0) < SSD_P
        lastrow = (lax.broadcasted_iota(jnp.int32, (L, 1), 0) == L - 1).astype(F32)
        dcum = jnp.zeros((L, LANES), F32)
        dcum_t = jnp.zeros((LANES, L), F32)
        ddt = jnp.zeros((L, LANES), F32)
        spread = _head_spread()
        ones = jnp.ones((L, LANES), BF16)
        ecum = jnp.exp(cum)
        wall = jnp.exp(cum[L - 1:L, :] - cum)
        dt_x = _split_nn(dt, spread)
        el_x = _split_nn(ecum, spread)
        wl_x = _split_nn(wall, spread)
        headrow = lax.broadcasted_iota(jnp.int32, (LANES, 1), 0)
        grp = []
        for g in range(SSD_G):
            Bg = xbc_ref[0, :, SSD_W + g * SSD_N:SSD_W + (g + 1) * SSD_N]
            Cg = xbc_ref[0, :, SSD_W + SSD_G * SSD_N + g * SSD_N:SSD_W + SSD_G * SSD_N + (g + 1) * SSD_N]
            grp.append(dict(B=Bg, C=Cg, CB=_nt(Cg, Bg), dB=jnp.zeros((L, SSD_N), F32), dC=jnp.zeros((L, SSD_N), F32),
                            dCB=jnp.zeros((L, L), F32)))
        for j in range(NPAIR):
            h0, h1 = 2 * j, 2 * j + 1
            sl = slice(j * LANES, (j + 1) * LANES)
            G = grp[j // (NPAIR // SSD_G)]
            Bg, Cg, CB = G["B"], G["C"], G["CB"]
            X = xbc_ref[0, :, sl]
            dY = dy_ref[0, :, sl]
            c0, c1 = cum[:, h0:h0 + 1], cum[:, h1:h1 + 1]
            r0, r1 = cumT[h0:h0 + 1, :], cumT[h1:h1 + 1, :]
            cl0, cl1 = cum[L - 1:L, h0:h0 + 1], cum[L - 1:L, h1:h1 + 1]
            oh0, oh1 = _onehot_lane(h0), _onehot_lane(h1)
            gather = _pair_gather(j)
            dtl, el, wl = dt_x[:, sl], el_x[:, sl], wl_x[:, sl]
            Xt = X * dtl
            Hp = hin_ref[0, 0, j]
            dS = dH[j]
            dX = dY * d_ref[j:j + 1, :]
            ddacc[j:j + 1, :] += _csum(dY * X)
            Z = _nt(Cg, Hp)
            dZ = dY * el
            dcum = dcum + _split_nn(dY * Z, gather) * ecum
            G["dC"] = G["dC"] + _nn(dZ, Hp)
            dHy = _tn(dZ, Cg)
            Gm = _nt(Bg, dS)
            dXt = wl * Gm
            q = _split_nn(Xt * Gm, gather) * wall
            dcum = dcum + lastrow * _csum(q) - q
            G["dB"] = G["dB"] + _nn(wl * Xt, dS)
            g0, g1 = jnp.exp(cl0), jnp.exp(cl1)
            rowsum = _nn(dS * Hp, ones)
            dg0 = _csum(jnp.where(rowlo, rowsum, 0.0))
            dg1 = _csum(jnp.where(rowlo, 0.0, rowsum))
            dcum = dcum + lastrow * ((dg0 * g0) * oh0 + (dg1 * g1) * oh1)
            dH[j] = jnp.where(rowlo, g0, g1) * dS + dHy
            for h, ch, rh, mh, oh in ((h0, c0, r0, lo, oh0), (h1, c1, r1, jnp.logical_not(lo), oh1)):
                decay = jnp.exp(jnp.where(mask, ch - rh, NEG))
                Mh = CB * decay
                dM = _nt(jnp.where(mh, dY, 0.0), Xt) * maskf
                dXt = dXt + jnp.where(mh, _tn(Mh, dY), 0.0)
                G["dCB"] = G["dCB"] + dM * decay
                Q = dM * Mh
                dcum = dcum + _rsum(Q) * oh
                dcum_t = dcum_t + (headrow == h).astype(F32) * _csum(Q)
            dX = dX + dXt * dtl
            ddt = ddt + _split_nn(dXt * X, gather)
            dx_ref[0, :, sl] = dX
        dcum = dcum - dcum_t.T
        for g in range(SSD_G):
            G = grp[g]
            dC = G["dC"] + _nn(G["dCB"], G["B"])
            dB = G["dB"] + _tn(G["dCB"], G["C"])
            dx_ref[0, :, SSD_W + g * SSD_N:SSD_W + (g + 1) * SSD_N] = dB
            dx_ref[0, :, SSD_W + SSD_G * SSD_N + g * SSD_N:SSD_W + SSD_G * SSD_N + (g + 1) * SSD_N] = dC
        r = lax.broadcasted_iota(jnp.int32, (L, L), 0)
        c = lax.broadcasted_iota(jnp.int32, (L, L), 1)
        dda = _dg((c >= r).astype(F32), dcum, 1, 0, lax.Precision.HIGHEST)
        heads = _lanes(0, SSD_HEADS)
        ddt = ddt + dda * a
        draw = jnp.where(heads, ddt * _sig(sm + bias_ref[...]), 0.0)
        ddt_ref[0] = draw.astype(BF16)
        dpar_ref[0:1, :] += _csum(draw)
        dpar_ref[1:2, :] += _csum(jnp.where(heads, dda * dt * a, 0.0))

        @pl.when((bb == Bl - 1) & (i == nc - 1))
        def _():
            acc = ddacc[...]
            lane = lax.broadcasted_iota(jnp.int32, (NPAIR, LANES), 1)
            s0 = _rsum(jnp.where(lane < SSD_P, acc, 0.0))
            s1 = _rsum(jnp.where(lane < SSD_P, 0.0, acc))
            dd_ref[...] = jnp.where(lane == 0, s0, jnp.where(lane == 1, s1, 0.0))

    vec = pl.BlockSpec((1, LANES), lambda b, i: (0, 0))
    par = pl.BlockSpec((NPAIR, LANES), lambda b, i: (0, 0))
    return _pc(body, name=name, grid=(Bl, nc),
               in_specs=[pl.BlockSpec((1, L, SSD_XBC), lambda b, i: (b, nc - 1 - i, 0)),
                         pl.BlockSpec((1, L, LANES), lambda b, i: (b, nc - 1 - i, DT0 // LANES)),
                         pl.BlockSpec((1, 1, NPAIR, LANES, LANES), lambda b, i: (b, nc - 1 - i, 0, 0, 0)),
                         pl.BlockSpec((1, L, SSD_W), lambda b, i: (b, nc - 1 - i, 0)),
                         vec, vec, par],
               out_specs=[pl.BlockSpec((1, L, SSD_XBC), lambda b, i: (b, nc - 1 - i, 0)),
                          pl.BlockSpec((1, L, LANES), lambda b, i: (b, nc - 1 - i, 0)),
                          par, par],
               out_shape=[SDS((Bl, S, SSD_XBC), F32), SDS((Bl, S, LANES), BF16), SDS((NPAIR, LANES), F32),
                          SDS((NPAIR, LANES), F32)],
               scratch=[pltpu.VMEM((NPAIR, LANES, LANES), F32), pltpu.VMEM((NPAIR, LANES), F32)],
               comm=comm)(xbc3, proj3, hin, dy3, bias, alog, dskip)


PE_LO, PE_MID, PE_HI = MLA_NOPE, MLA_NOPE + MLA_ROPE // 2, MLA_NOPE + MLA_ROPE
ATT_SCALE = 1.0 / math.sqrt(MLA_QK)


def _swap_matrix():
    src = lax.broadcasted_iota(jnp.int32, (LANES, LANES), 0)
    dst = lax.broadcasted_iota(jnp.int32, (LANES, LANES), 1)
    half = MLA_ROPE // 2
    first = (dst >= PE_LO) & (dst < PE_MID) & (src == dst + half)
    second = (dst >= PE_MID) & (dst < PE_HI) & (src == dst - half)
    return (second.astype(F32) - first.astype(F32)).astype(BF16)


def rope_tables(pos, invf, *, name):
    T = pos.shape[0]
    tm = _tile(T, 512, 8)

    def body(pos_ref, f_ref, c_ref, s_ref):
        ang = pos_ref[...] * f_ref[...]
        pe = _lanes(PE_LO, PE_HI)
        c_ref[...] = jnp.where(pe, jnp.cos(ang), 1.0)
        s_ref[...] = jnp.where(pe, jnp.sin(ang), 0.0)

    tile = pl.BlockSpec((tm, LANES), lambda i: (i, 0))
    return _pc(body, name=name, grid=(T // tm,),
               in_specs=[pl.BlockSpec((tm, 1), lambda i: (i, 0)), pl.BlockSpec((1, LANES), lambda i: (0, 0))],
               out_specs=[tile, tile], out_shape=[SDS((T, LANES), F32)] * 2)(pos, invf)


V_ONE = MLA_V


def mla_prep_fwd(qt, kvt, proj, cs, sn, *, name):
    T = qt.shape[0]
    tm = _tile(T, 256, 8)
    HW = MLA_H * LANES

    def body(q_ref, k_ref, v_ref, kpe_ref, c_ref, s_ref, qo_ref, ko_ref, vo_ref):
        c, s = c_ref[...], s_ref[...]
        kpe = kpe_ref[...]
        sw = _swap_matrix()
        one = _lanes(V_ONE, V_ONE + 1)
        for h in range(MLA_H):
            sl = slice(h * LANES, (h + 1) * LANES)
            q = q_ref[:, sl]
            k = k_ref[:, sl] + kpe
            qo_ref[:, sl] = ((q * c + _split_nn(q, sw) * s) * ATT_SCALE).astype(BF16)
            ko_ref[:, sl] = (k * c + _split_nn(k, sw) * s).astype(BF16)
            vo_ref[:, sl] = jnp.where(one, 1.0, v_ref[:, sl]).astype(BF16)

    row = pl.BlockSpec((tm, HW), lambda i: (i, 0))
    tab = pl.BlockSpec((tm, LANES), lambda i: (i, 0))
    return _pc(body, name=name, grid=(T // tm,),
               in_specs=[row, row, pl.BlockSpec((tm, HW), lambda i: (i, 1)),
                         pl.BlockSpec((tm, LANES), lambda i: (i, KPE0 // LANES)), tab, tab],
               out_specs=[row, row, row], out_shape=[SDS((T, HW), BF16)] * 3)(qt, kvt, kvt, proj, cs, sn)


def mla_prep_bwd(dqr, dkr, cs, sn, *, name):
    T = dqr.shape[0]
    tm = _tile(T, 256, 8)
    HW = MLA_H * LANES

    def body(dq_ref, dk_ref, c_ref, s_ref, qo_ref, ko_ref, kpe_ref):
        c, s = c_ref[...], s_ref[...]
        sw = _swap_matrix()
        pe = _lanes(PE_LO, PE_HI)
        dkpe = jnp.zeros((tm, LANES), F32)
        for h in range(MLA_H):
            sl = slice(h * LANES, (h + 1) * LANES)
            dq = dq_ref[:, sl] * ATT_SCALE
            dk = dk_ref[:, sl]
            qo_ref[:, sl] = (dq * c - _split_nn(dq * s, sw)).astype(BF16)
            dkk = dk * c - _split_nn(dk * s, sw)
            ko_ref[:, sl] = jnp.where(pe, 0.0, dkk).astype(BF16)
            dkpe = dkpe + jnp.where(pe, dkk, 0.0)
        kpe_ref[...] = dkpe.astype(BF16)

    row = pl.BlockSpec((tm, HW), lambda i: (i, 0))
    tab = pl.BlockSpec((tm, LANES), lambda i: (i, 0))
    return _pc(body, name=name, grid=(T // tm,), in_specs=[row, row, tab, tab], out_specs=[row, row, tab],
               out_shape=[SDS((T, HW), BF16), SDS((T, HW), BF16), SDS((T, LANES), BF16)])(dqr, dkr, cs, sn)


def _att_tile(S):
    return _tile(S, 512, LANES)


def _rep(x, n):
    return x if n == 1 else jnp.concatenate([x] * n, axis=1)


def _diag_mask(t, transposed=False):
    r = lax.broadcasted_iota(jnp.int32, (t, t), 0)
    c = lax.broadcasted_iota(jnp.int32, (t, t), 1)
    return (c >= r) if transposed else (c <= r)


def flash_fwd(qr, kr, vr, Bl, *, comm=None, name):
    T = qr.shape[0]
    S = T // Bl
    t = _att_tile(S)
    n = S // t
    nl = t // LANES

    def body(q_ref, k_ref, v_ref, o_ref, lse_ref, lset_ref, m, acc):
        qi = pl.program_id(2)
        q = q_ref[...]
        m[...] = jnp.full_like(m, NEG)
        acc[...] = jnp.zeros_like(acc)

        def block(kj, masked):
            off = pl.multiple_of(kj * t, t)
            s = _nt(q, k_ref[pl.ds(off, t), :])
            if masked:
                s = jnp.where(_diag_mask(t), s, NEG)
            mo = m[...]
            mn = jnp.maximum(mo, jnp.max(s, axis=1, keepdims=True))
            p = jnp.exp((s - _rep(mn, nl)).astype(BF16))
            acc[...] = jnp.exp(mo - mn) * acc[...] + _nn(p, v_ref[pl.ds(off, t), :])
            m[...] = mn

        def loop(kj, c):
            block(kj, False)
            return c

        lax.fori_loop(0, qi, loop, 0)
        block(qi, True)
        a = acc[...]
        l = a[:, V_ONE:V_ONE + 1]
        o_ref[...] = jnp.where(_lanes(0, MLA_V), a / l, 0.0).astype(BF16)
        lse = m[...] + jnp.log(l)
        lse_ref[...] = lse
        lset_ref[...] = lse.T[0:8, :]

    qs = pl.BlockSpec((t, LANES), lambda b, h, qi: (b * n + qi, h))
    seq = pl.BlockSpec((S, LANES), lambda b, h, qi: (b, h))
    return _pc(body, name=name, grid=(Bl, MLA_H, n), in_specs=[qs, seq, seq],
               out_specs=[qs, qs, pl.BlockSpec((8, t), lambda b, h, qi: (b * MLA_H + h, qi))],
               out_shape=[SDS((T, MLA_H * LANES), BF16), SDS((T, MLA_H * LANES), F32), SDS((Bl * MLA_H * 8, S), F32)],
               scratch=[pltpu.VMEM((t, LANES), F32), pltpu.VMEM((t, LANES), F32)], comm=comm)(qr, kr, vr)


def flash_bwd_dq(qr, kr, vr, o, lse, dycat, Bl, *, comm=None, name):
    T = qr.shape[0]
    S = T // Bl
    t = _att_tile(S)
    n = S // t
    nl = t // LANES
    do0 = (SSD_W + POOL_W) // LANES

    def body(q_ref, k_ref, v_ref, o_ref, lse_ref, do_ref, dq_ref, dlt_ref, acc, dl):
        qi = pl.program_id(2)
        q = q_ref[...]
        do = do_ref[...]
        dob = do.astype(BF16)
        dl[...] = jnp.broadcast_to(_rsum(do * o_ref[...].astype(F32)), (t, LANES))
        acc[...] = jnp.zeros_like(acc)

        def block(kj, masked):
            off = pl.multiple_of(kj * t, t)
            k = k_ref[pl.ds(off, t), :]
            s = _nt(q, k)
            if masked:
                s = jnp.where(_diag_mask(t), s, NEG)
            p = jnp.exp((s - _rep(lse_ref[...], nl)).astype(BF16))
            dp = _nt(dob, v_ref[pl.ds(off, t), :])
            acc[...] += _nn(p * (dp - _rep(dl[...], nl)), k)

        def loop(kj, c):
            block(kj, False)
            return c

        lax.fori_loop(0, qi, loop, 0)
        block(qi, True)
        dq_ref[...] = acc[...]
        dlt_ref[...] = dl[...].T[0:8, :]

    qs = pl.BlockSpec((t, LANES), lambda b, h, qi: (b * n + qi, h))
    seq = pl.BlockSpec((S, LANES), lambda b, h, qi: (b, h))
    return _pc(body, name=name, grid=(Bl, MLA_H, n),
               in_specs=[qs, seq, seq, qs, qs, pl.BlockSpec((t, LANES), lambda b, h, qi: (b * n + qi, do0 + h))],
               out_specs=[qs, pl.BlockSpec((8, t), lambda b, h, qi: (b * MLA_H + h, qi))],
               out_shape=[SDS((T, MLA_H * LANES), F32), SDS((Bl * MLA_H * 8, S), F32)],
               scratch=[pltpu.VMEM((t, LANES), F32), pltpu.VMEM((t, LANES), F32)], comm=comm)(qr, kr, vr, o, lse, dycat)


def flash_bwd_dkv(qr, kr, vr, lset, dlt, dycat, Bl, *, comm=None, name):
    T = qr.shape[0]
    S = T // Bl
    t = _att_tile(S)
    n = S // t
    do0 = (SSD_W + POOL_W) // LANES

    def body(q_ref, k_ref, v_ref, lset_ref, dlt_ref, do_ref, dk_ref, dv_ref, dka, dva):
        kj = pl.program_id(2)
        k = k_ref[...]
        v = v_ref[...]
        dka[...] = jnp.zeros_like(dka)
        dva[...] = jnp.zeros_like(dva)

        def block(qi, masked):
            off = pl.multiple_of(qi * t, t)
            q = q_ref[pl.ds(off, t), :]
            do = do_ref[pl.ds(off, t), :].astype(BF16)
            st = _nt(k, q)
            if masked:
                st = jnp.where(_diag_mask(t, True), st, NEG)
            pt = jnp.exp((st - lset_ref[0:1, pl.ds(off, t)]).astype(BF16))
            dst = pt * (_nt(v, do) - dlt_ref[0:1, pl.ds(off, t)])
            dva[...] += _nn(pt, do)
            dka[...] += _nn(dst, q)

        def loop(qi, c):
            block(qi, False)
            return c

        block(kj, True)
        lax.fori_loop(kj + 1, n, loop, 0)
        dk_ref[...] = dka[...]
        dv_ref[...] = dva[...].astype(BF16)

    ks = pl.BlockSpec((t, LANES), lambda b, h, kj: (b * n + kj, h))
    seq = pl.BlockSpec((S, LANES), lambda b, h, kj: (b, h))
    rows = pl.BlockSpec((8, S), lambda b, h, kj: (b * MLA_H + h, 0))
    return _pc(body, name=name, grid=(Bl, MLA_H, n),
               in_specs=[seq, ks, ks, rows, rows, pl.BlockSpec((S, LANES), lambda b, h, kj: (b, do0 + h))],
               out_specs=[ks, ks], out_shape=[SDS((T, MLA_H * LANES), F32), SDS((T, MLA_H * LANES), BF16)],
               scratch=[pltpu.VMEM((t, LANES), F32), pltpu.VMEM((t, LANES), F32)], comm=comm)(qr, kr, vr, lset, dlt, dycat)


def _rows2d(a):
    return a.reshape(-1, a.shape[-1])


def _scalar(i):
    return jnp.reshape(i, (1,)).astype(jnp.int32)


def chip_sum(g8, from_sibling, *, name):
    blk = g8.shape[1:]
    R, C = math.prod(blk[:-1]), blk[-1]
    tm = _tile(R, 512, 16)

    def body(c_ref, a_ref, b_ref, o_ref, ob_ref):
        s = a_ref[0, 0] + b_ref[0]
        o_ref[0] = s
        ob_ref[0] = s.astype(BF16)

    row = pl.BlockSpec((1, tm, C), lambda k, i, c: (k, i, 0))
    spec = pltpu.PrefetchScalarGridSpec(
        num_scalar_prefetch=1, grid=(4, R // tm),
        in_specs=[pl.BlockSpec((1, 1, tm, C), lambda k, i, c: (k, c[0], i, 0)), row], out_specs=[row, row])
    o, ob = pl.pallas_call(body, name=name, grid_spec=spec, out_shape=[SDS((4, R, C), F32), SDS((4, R, C), BF16)],
                           compiler_params=pltpu.CompilerParams(vmem_limit_bytes=VMEM_LIMIT),
                           )(_scalar(lax.axis_index("c")), g8.reshape(4, 2, R, C), from_sibling.reshape(4, R, C))
    return o.reshape((4,) + blk), ob.reshape((4,) + blk)


def adamw_sharded(w, m, v, sums, recv, layer, prev, *, name):
    blk = w.shape[1:]
    R, C = math.prod(blk[:-1]), blk[-1]
    tm = _tile(R, 256, 16)
    bc1 = 1.0 - ADAM_B1 ** ADAM_STEP
    bc2 = 1.0 - ADAM_B2 ** ADAM_STEP
    n_prev = 0 if prev is None else 4

    def body(chip_ref, w_ref, m_ref, v_ref, s_ref, r_ref, *rest):
        g_ref, d_ref, nm_ref, nv_ref = rest[n_prev:]
        g = s_ref[0] + r_ref[0].astype(F32) + r_ref[1].astype(F32) + r_ref[2].astype(F32)
        mm_ = ADAM_B1 * m_ref[0] + (1.0 - ADAM_B1) * g
        vv = ADAM_B2 * v_ref[0] + (1.0 - ADAM_B2) * (g * g)
        g_ref[0] = g
        nm_ref[0] = mm_
        nv_ref[0] = vv
        d_ref[0] = -ADAM_LR * ((mm_ / bc1) / (jnp.sqrt(vv / bc2) + ADAM_EPS) + ADAM_WD * w_ref[0])

    lay = pl.BlockSpec((1, tm, C), lambda i, c: (layer, i, 0))
    spec = pltpu.PrefetchScalarGridSpec(
        num_scalar_prefetch=1, grid=(R // tm,),
        in_specs=[lay, lay, lay, pl.BlockSpec((1, tm, C), lambda i, c: (c[0], i, 0)),
                  pl.BlockSpec((3, tm, C), lambda i, c: (0, i, 0))] + [ANY] * n_prev,
        out_specs=[lay] * 4)
    xi, yi, _ = _place()
    d3 = (w.shape[0], R, C)
    outs = pl.pallas_call(
        body, name=name, grid_spec=spec, out_shape=[SDS(d3, F32)] * 4,
        input_output_aliases={6 + i: i for i in range(n_prev)},
        compiler_params=pltpu.CompilerParams(vmem_limit_bytes=VMEM_LIMIT),
    )(_scalar(2 * xi + yi), w.reshape(d3), m.reshape(d3), v.reshape(d3), sums.reshape(4, R, C), recv.reshape(3, R, C),
      *([] if prev is None else prev))
    return list(outs)


def adamw(w, m, v, parts, *, name):
    shp = w.shape
    w2, m2, v2 = _rows2d(w), _rows2d(m), _rows2d(v)
    R, C = w2.shape
    p3 = [p.reshape(p.shape[0], R, C) for p in parts]
    tm = _tile(R, 256, 8)
    bc1 = 1.0 - ADAM_B1 ** ADAM_STEP
    bc2 = 1.0 - ADAM_B2 ** ADAM_STEP

    def body(w_ref, m_ref, v_ref, *refs):
        p_refs, (g_ref, d_ref, nm_ref, nv_ref) = refs[:len(p3)], refs[len(p3):]
        g = None
        for p_ref, p in zip(p_refs, p3):
            for k in range(p.shape[0]):
                term = p_ref[k].astype(F32)
                g = term if g is None else g + term
        mm_ = ADAM_B1 * m_ref[...] + (1.0 - ADAM_B1) * g
        vv = ADAM_B2 * v_ref[...] + (1.0 - ADAM_B2) * (g * g)
        g_ref[...] = g
        nm_ref[...] = mm_
        nv_ref[...] = vv
        d_ref[...] = -ADAM_LR * ((mm_ / bc1) / (jnp.sqrt(vv / bc2) + ADAM_EPS) + ADAM_WD * w_ref[...])

    blk = pl.BlockSpec((tm, C), lambda i: (i, 0))
    pspecs = [pl.BlockSpec((p.shape[0], tm, C), lambda i: (0, i, 0)) for p in p3]
    outs = _pc(body, name=name, grid=(R // tm,), in_specs=[blk, blk, blk] + pspecs,
               out_specs=[blk] * 4, out_shape=[SDS((R, C), F32)] * 4)(w2, m2, v2, *p3)
    return [o.reshape(shp) for o in outs]


def _place():
    return lax.axis_index("x"), lax.axis_index("y"), lax.axis_index("c")


def all_gather_many(xs, *, name):
    n = len(xs)

    def body(*refs):
        x_refs, o_refs = refs[:n], refs[n:2 * n]
        send_sems, recv_sems, local_sems = refs[2 * n:]
        x, y, c = _place()
        me, sibling = (x, y, c), (x, y, 1 - c)
        chips = [(1 - x, y), (x, 1 - y), (1 - x, 1 - y)]

        def rows(a, p):
            return o_refs[a].at[4 * p[0] + 2 * p[1] + p[2]]

        def copy(a, k, block, to, src=None):
            return pltpu.make_async_remote_copy(
                src_ref=rows(a, block) if src is None else src, dst_ref=rows(a, block),
                send_sem=send_sems.at[7 * a + k], recv_sem=recv_sems.at[7 * a + k], device_id=to, device_id_type=MESH)

        mine = [pltpu.make_async_copy(x_refs[a], rows(a, me), local_sems.at[a]) for a in range(n)]
        for cp in mine:
            cp.start()
        first = []
        for a in range(n):
            first.append(copy(a, 0, me, sibling, src=x_refs[a]))
            first += [copy(a, 1 + j, me, (*chip, c), src=x_refs[a]) for j, chip in enumerate(chips)]
        for cp in first:
            cp.start()
        passed = []
        for j, chip in enumerate(chips):
            for a in range(n):
                copy(a, 1 + j, (*chip, c), me).wait_recv()
                cp = copy(a, 4 + j, (*chip, c), sibling)
                cp.start()
                passed.append(cp)
        for a in range(n):
            copy(a, 0, sibling, me).wait_recv()
            for j, chip in enumerate(chips):
                copy(a, 4 + j, (*chip, 1 - c), me).wait_recv()
        for cp in first + passed:
            cp.wait_send()
        for cp in mine:
            cp.wait()

    return pl.pallas_call(
        body, name=name, in_specs=[ANY] * n, out_specs=[ANY] * n,
        out_shape=[SDS((N_DEV,) + a.shape, a.dtype) for a in xs],
        scratch_shapes=[pltpu.SemaphoreType.DMA((7 * n,)), pltpu.SemaphoreType.DMA((7 * n,)), pltpu.SemaphoreType.DMA((n,))],
    )(*xs)


def _stage(ins, out_shape, n_peers, copy_of):
    ins = list(ins)

    def copies(in_refs, out_refs, send_sems, recv_sems):
        place = _place()
        out = []
        for a in range(len(ins)):
            for k in range(n_peers):
                src, dst, peer = copy_of(in_refs[a], out_refs[a], k, place)
                out.append(pltpu.make_async_remote_copy(
                    src_ref=src, dst_ref=dst, send_sem=send_sems.at[n_peers * a + k], recv_sem=recv_sems.at[n_peers * a + k],
                    device_id=peer, device_id_type=MESH))
        return out

    return dict(ins=ins, out_shape=list(out_shape), sems=n_peers * len(ins), copies=copies)


def _other_chips(x, y):
    return [(1 - x, y), (x, 1 - y), (1 - x, 1 - y)]


def stage_gather_direct(blocks):
    def copy_of(src, dst, k, place):
        x, y, c = place
        peer = (x, y, 1 - c) if k == 0 else (*_other_chips(x, y)[k - 1], c)
        return src, dst.at[4 * x + 2 * y + c], peer

    return _stage(blocks, [SDS((N_DEV,) + b.shape, b.dtype) for b in blocks], 4, copy_of)


def stage_gather_forward(bufs):
    def copy_of(src, dst, k, place):
        x, y, c = place
        cx, cy = _other_chips(x, y)[k]
        slot = 4 * cx + 2 * cy + c
        return src.at[slot], dst.at[slot], (x, y, 1 - c)

    st = _stage(bufs, [SDS(b.shape, b.dtype) for b in bufs], 3, copy_of)
    st["alias"] = True
    return st


def stage_rs_sibling(g8s):
    def copy_of(src, dst, k, place):
        x, y, c = place
        return src.at[2 * k + (1 - c)], dst.at[k], (x, y, 1 - c)

    return _stage(g8s, [SDS((4,) + g.shape[1:], g.dtype) for g in g8s], 4, copy_of)


def stage_rs_chips(sums):
    def copy_of(src, dst, k, place):
        x, y, c = place
        chip = _other_chips(x, y)[k]
        return src.at[2 * chip[0] + chip[1]], dst.at[k], (*chip, c)

    return _stage(sums, [SDS((3,) + s.shape[1:], s.dtype) for s in sums], 3, copy_of)


def run_stage(stage, *, name):
    n_in, n_out = len(stage["ins"]), len(stage["out_shape"])

    def body(*refs):
        cps = stage["copies"](refs[:n_in], refs[n_in:n_in + n_out], refs[-2], refs[-1])
        for cp in cps:
            cp.start()
        for cp in cps:
            cp.wait()

    return pl.pallas_call(
        body, name=name, in_specs=[ANY] * n_in, out_specs=[ANY] * n_out, out_shape=stage["out_shape"],
        scratch_shapes=[pltpu.SemaphoreType.DMA((stage["sems"],)), pltpu.SemaphoreType.DMA((stage["sems"],))],
    )(*stage["ins"])


def with_own_block(buf, own):
    x, y, c = _place()
    return lax.dynamic_update_index_in_dim(buf, own, 4 * x + 2 * y + c, 0)


def _owner_major(full, axis):
    shp = full.shape
    r = full.reshape(shp[:axis] + (N_DEV, shp[axis] // N_DEV) + shp[axis + 1:])
    return jnp.moveaxis(r, axis, 0)


def _from_owner_major(g8, axis):
    r = jnp.moveaxis(g8, 0, axis)
    shp = r.shape
    return r.reshape(shp[:axis] + (shp[axis] * shp[axis + 1],) + shp[axis + 2:])


def _perm_w_in(w):
    z = jnp.zeros((w.shape[0], LANES), w.dtype)
    dt = jnp.pad(w[:, 2560:2576], ((0, 0), (0, LANES - SSD_HEADS)))
    kpe = jnp.pad(w[:, 3728:3760], ((0, 0), (PE_LO, LANES - PE_HI)))
    return jnp.concatenate([w[:, 0:1024], w[:, 1024:2560], w[:, 2576:3088], w[:, 3088:3472], z, w[:, 3472:3728], dt, kpe], axis=1)


def _unperm_w_in(g):
    return jnp.concatenate([g[:, Z0:Z0 + 1024], g[:, XBC0:XBC0 + 1536], g[:, DT0:DT0 + SSD_HEADS], g[:, U0:U0 + 512],
                            g[:, CQ0:CQ0 + 384], g[:, CKV0:CKV0 + 256], g[:, KPE0 + PE_LO:KPE0 + PE_HI]], axis=1)


def _perm_w_uq(w):
    return jnp.pad(w.reshape(MLA_QR, MLA_H, MLA_QK), ((0, 0), (0, 0), (0, LANES - MLA_QK))).reshape(MLA_QR, MLA_H * LANES)


def _unperm_w_uq(g):
    return g.reshape(MLA_QR, MLA_H, LANES)[:, :, :MLA_QK].reshape(MLA_QR, MLA_H * MLA_QK)


def _perm_w_ukv(w):
    w3 = w.reshape(MLA_KVR, MLA_H, MLA_NOPE + MLA_V)
    pad = ((0, 0), (0, 0), (0, LANES - MLA_NOPE))
    k = jnp.pad(w3[:, :, :MLA_NOPE], pad).reshape(MLA_KVR, MLA_H * LANES)
    v = jnp.pad(w3[:, :, MLA_NOPE:], pad).reshape(MLA_KVR, MLA_H * LANES)
    return jnp.concatenate([k, v], axis=1)


def _unperm_w_ukv(g):
    k = g[:, :MLA_H * LANES].reshape(MLA_KVR, MLA_H, LANES)[:, :, :MLA_NOPE]
    v = g[:, MLA_H * LANES:].reshape(MLA_KVR, MLA_H, LANES)[:, :, :MLA_V]
    return jnp.concatenate([k, v], axis=2).reshape(MLA_KVR, MLA_H * (MLA_NOPE + MLA_V))


def _perm_w_out(w):
    m = jnp.pad(w[SSD_W + POOL_W:].reshape(MLA_H, MLA_V, D), ((0, 0), (0, LANES - MLA_V), (0, 0))).reshape(MLA_H * LANES, D)
    return jnp.concatenate([w[:SSD_W + POOL_W], m], axis=0)


def _unperm_w_out(g):
    m = g[SSD_W + POOL_W:].reshape(MLA_H, LANES, D)[:, :MLA_V].reshape(MLA_H * MLA_V, D)
    return jnp.concatenate([g[:SSD_W + POOL_W], m], axis=0)


def _lane_pad(v):
    return jnp.pad(v.reshape(1, -1), ((0, 0), (0, LANES - v.shape[-1])))


SMALL = ("attn_norm", "ssd_conv_b", "ssd_dt_bias", "ssd_a_log", "ssd_d", "ssd_norm", "pool_w", "pool_scale",
         "mla_q_norm", "mla_kv_norm", "ffn_norm", "ffn_conv_b", "final_norm")
SHARDED = {"w_in": 2, "ssd_conv_w": 2, "mla_w_uq": 2, "mla_w_ukv": 2, "w_out": 1, "ffn_w_up": 2, "ffn_conv_w": 2,
           "ffn_w_down": 1}
ALL_W = ("attn_norm", "w_in", "ssd_conv_w", "ssd_conv_b", "ssd_dt_bias", "ssd_a_log", "ssd_d", "ssd_norm", "pool_w",
         "pool_scale", "mla_q_norm", "mla_w_uq", "mla_kv_norm", "mla_w_ukv", "w_out", "ffn_norm", "ffn_w_up",
         "ffn_conv_w", "ffn_conv_b", "ffn_w_down", "final_norm")


def _pack_small(d):
    rows, layout = [], []
    for k in SMALL:
        a = d[k].reshape(-1)
        n = a.shape[0]
        r = -(-n // LANES)
        rows.append(jnp.pad(a, (0, r * LANES - n)).reshape(r, LANES))
        layout.append((k, n, r, d[k].shape))
    buf = jnp.concatenate(rows, axis=0)
    pad = (-buf.shape[0]) % 8
    return jnp.pad(buf, ((0, pad), (0, 0))), layout


def _unpack_small(buf, layout):
    out, r0 = {}, 0
    for k, n, r, shp in layout:
        out[k] = buf[r0:r0 + r].reshape(-1)[:n].reshape(shp)
        r0 += r
    return out


_PERM = {"w_in": _perm_w_in, "mla_w_uq": _perm_w_uq, "mla_w_ukv": _perm_w_ukv, "w_out": _perm_w_out}
FIRST = ("w_in", "ssd_conv_w")
REST = tuple(k for k in SHARDED if k not in FIRST)


def _sharded_entries(keys, gathered):
    return {k: _PERM.get(k, lambda t: t)(_from_owner_major(g8, SHARDED[k] - 1)) for k, g8 in zip(keys, gathered)}


def _layer_fwd(l, x, W, cs, sn, Bl, next_blocks=None, pending=None):
    T = x.shape[0]
    S = T // Bl
    n = f"l{l}_"
    h = rms_fwd(x, W["attn_norm"], name=n + "attn_norm")
    own_direct = stage_gather_direct(pending) if pending is not None else None
    proj = mm(h, W["w_in"], comm=own_direct, name=n + "w_in")
    proj3 = proj.reshape(Bl, S, PW)
    xbc3 = conv_silu_fwd(proj3, W["ssd_conv_w"], W["ssd_conv_b"], name=n + "ssd_conv")
    own_forward = stage_gather_forward(own_direct["result"]) if own_direct else None
    y3, hin = ssd_fwd(xbc3, proj3, W["ssd_dt_bias"], W["ssd_a_log"], W["ssd_d"], comm=own_forward, name=n + "ssd_scan")
    if own_direct:
        W = {**W, **_sharded_entries(REST, [with_own_block(buf, b) for buf, b in zip(own_forward["result"], pending)])}
    y = y3.reshape(T, SSD_W)
    y_ssd = gated_rms_fwd(y, proj, W["ssd_norm"], name=n + "ssd_gate_norm")
    y_pool = pool_fwd(proj3, W["pool_w"], W["pool_scale"], name=n + "pool").reshape(T, POOL_W)
    qn = rms_fwd(proj, W["mla_q_norm"], col0=CQ0, width=MLA_QR, name=n + "q_norm")
    kvn = rms_fwd(proj, W["mla_kv_norm"], col0=CKV0, width=MLA_KVR, name=n + "kv_norm")
    qt = mm(qn, W["mla_w_uq"], name=n + "w_uq")
    kvt = mm(kvn, W["mla_w_ukv"], name=n + "w_ukv")
    qr, kr, vr = mla_prep_fwd(qt, kvt, proj, cs, sn, name=n + "rope")
    direct = stage_gather_direct(next_blocks) if next_blocks is not None else None
    o, lse, lset = flash_fwd(qr, kr, vr, Bl, comm=direct, name=n + "attn")
    ycat = jnp.concatenate([y_ssd, y_pool, o], axis=1)
    x1 = mm(ycat, W["w_out"], add=x, name=n + "w_out")
    h2 = rms_fwd(x1, W["ffn_norm"], name=n + "ffn_norm")
    forward = stage_gather_forward(direct["result"]) if direct else None
    pre = mm(h2, W["ffn_w_up"], comm=forward, name=n + "w_up")
    gathered = [with_own_block(buf, b) for buf, b in zip(forward["result"], next_blocks)] if direct else None
    pre3 = pre.reshape(Bl, S, 2 * DFF)
    act = ffn_act_fwd(pre3, W["ffn_conv_w"], W["ffn_conv_b"], name=n + "ffn_act").reshape(T, DFF)
    x2 = mm(act, W["ffn_w_down"], add=x1, name=n + "w_down")
    saved = dict(x=x, h=h, proj=proj, xbc3=xbc3, hin=hin, y=y, qn=qn, kvn=kvn, vr=vr, qr=qr, kr=kr, o=o, lse=lse, lset=lset,
                 ycat=ycat, x1=x1, h2=h2, pre3=pre3, act=act)
    return x2, saved, gathered, W


EARLY = ("ffn_w_up", "ffn_conv_w", "ffn_w_down", "w_out")
LATE = tuple(k for k in SHARDED if k not in EARLY)
_UNPERM = {"w_in": _unperm_w_in, "mla_w_uq": _unperm_w_uq, "mla_w_ukv": _unperm_w_ukv, "w_out": _unperm_w_out}


def _by_owner(g, keys):
    return [_owner_major(_UNPERM.get(k, lambda t: t)(g[k]), SHARDED[k] - 1) for k in keys]


def _chip_sums(g8s, from_sibling, tag):
    return [chip_sum(g8, r, name=f"{tag}{a}") for a, (g8, r) in enumerate(zip(g8s, from_sibling))]


def _layer_bwd(l, dx2, dx2b, W, sv, cs, sn, Bl, later_g8=None):
    T = dx2.shape[0]
    S = T // Bl
    n = f"l{l}_b_"
    g = {}
    g["ffn_w_down"] = mm(sv["act"], dx2b, ta=True, name=n + "dw_down")
    dact = mm(dx2b, W["ffn_w_down"], tb=True, name=n + "dact")
    to_sibling = stage_rs_sibling(later_g8) if later_g8 is not None else None
    dpg, dpv, dwg, dwv, dbg, dbv = ffn_act_bwd(sv["pre3"], W["ffn_conv_w"], W["ffn_conv_b"], dact.reshape(Bl, S, DFF),
                                               comm=to_sibling, name=n + "ffn_act")
    to_chips = sums = None
    if to_sibling:
        sums = _chip_sums(later_g8, to_sibling["result"], n + "rs_late_add")
        to_chips = stage_rs_chips([sb for _, sb in sums])
    g["ffn_conv_w"] = jnp.concatenate([dwg, dwv], axis=1)
    g["ffn_conv_b"] = jnp.concatenate([dbg, dbv], axis=1)
    dpg, dpv = dpg.reshape(T, DFF), dpv.reshape(T, DFF)
    g["ffn_w_up"] = jnp.concatenate([mm(sv["h2"], dpg, ta=True, name=n + "dw_up_g"),
                                     mm(sv["h2"], dpv, ta=True, name=n + "dw_up_v")], axis=1)
    dh2 = mm(dpg, W["ffn_w_up"], tb=True, name=n + "dh2_g")
    dh2 = mm(dpv, W["ffn_w_up"], tb=True, b_k0=DFF, add=dh2, name=n + "dh2_v")
    dx1, dx1b, g["ffn_norm"] = rms_bwd(sv["x1"], W["ffn_norm"], dh2, add=dx2, name=n + "ffn_norm")
    g["w_out"] = mm(sv["ycat"], dx1b, ta=True, name=n + "dw_out")
    dycat = mm(dx1b, W["w_out"], tb=True, name=n + "dycat")
    proj, proj3 = sv["proj"], sv["proj"].reshape(Bl, S, PW)
    dy, dz, g["ssd_norm"] = gated_rms_bwd(sv["y"], proj, W["ssd_norm"], dycat, name=n + "ssd_gate_norm")
    dxa, ddt, dpar, dd = ssd_bwd(sv["xbc3"], proj3, sv["hin"], dy.reshape(Bl, S, SSD_W), W["ssd_dt_bias"], W["ssd_a_log"],
                                 W["ssd_d"], comm=to_chips, name=n + "ssd_scan")
    reduced_late = ([s32 for s32, _ in sums], to_chips["result"]) if to_chips else None
    g["ssd_dt_bias"] = dpar[0, :SSD_HEADS]
    g["ssd_a_log"] = dpar[1, :SSD_HEADS]
    g["ssd_d"] = dd[:, :2].reshape(SSD_HEADS)
    dxbc, g["ssd_conv_w"], g["ssd_conv_b"] = conv_silu_bwd(proj3, W["ssd_conv_w"], W["ssd_conv_b"], dxa, name=n + "ssd_conv")
    du, g["pool_w"], g["pool_scale"] = pool_bwd(proj3, W["pool_w"], W["pool_scale"], dycat.reshape(Bl, S, YCAT), name=n + "pool")
    early_g8 = _by_owner(g, EARLY)
    early_sibling = stage_rs_sibling(early_g8)
    dqr, dlt = flash_bwd_dq(sv["qr"], sv["kr"], sv["vr"], sv["o"], sv["lse"], dycat, Bl, comm=early_sibling, name=n + "attn_dq")
    early_sums = _chip_sums(early_g8, early_sibling["result"], n + "rs_early_add")
    early_chips = stage_rs_chips([sb for _, sb in early_sums])
    dkr, dv = flash_bwd_dkv(sv["qr"], sv["kr"], sv["vr"], sv["lset"], dlt, dycat, Bl, comm=early_chips, name=n + "attn_dkv")
    reduced_early = ([s32 for s32, _ in early_sums], early_chips["result"])
    dqt, dkt, dkpe = mla_prep_bwd(dqr, dkr, cs, sn, name=n + "rope")
    g["mla_w_ukv"] = jnp.concatenate([mm(sv["kvn"], dkt, ta=True, name=n + "dw_uk"),
                                      mm(sv["kvn"], dv, ta=True, name=n + "dw_uv")], axis=1)
    dkvn = mm(dkt, W["mla_w_ukv"], tb=True, name=n + "dkvn_k")
    dkvn = mm(dv, W["mla_w_ukv"], tb=True, b_k0=MLA_H * LANES, add=dkvn, name=n + "dkvn_v")
    g["mla_w_uq"] = mm(sv["qn"], dqt, ta=True, name=n + "dw_uq")
    dqn = mm(dqt, W["mla_w_uq"], tb=True, name=n + "dqn")
    dcq, g["mla_q_norm"] = rms_bwd(proj, W["mla_q_norm"], dqn, col0=CQ0, width=MLA_QR, name=n + "q_norm")
    dckv, g["mla_kv_norm"] = rms_bwd(proj, W["mla_kv_norm"], dkvn, col0=CKV0, width=MLA_KVR, name=n + "kv_norm")
    dproj = jnp.concatenate([dz, dxbc.reshape(T, SSD_XBC), du.reshape(T, POOL_W), dcq, jnp.zeros((T, LANES), BF16), dckv,
                             ddt.reshape(T, LANES), dkpe], axis=1)
    g["w_in"] = mm(sv["h"], dproj, ta=True, name=n + "dw_in")
    dh = mm(dproj, W["w_in"], tb=True, name=n + "dh")
    dx, dxb, g["attn_norm"] = rms_bwd(sv["x"], W["attn_norm"], dh, add=dx1, name=n + "attn_norm")
    return dx, dxb, g, reduced_late, reduced_early


def kernel(x, positions, attn_norm, w_in, ssd_conv_w, ssd_conv_b, ssd_dt_bias, ssd_a_log, ssd_d, ssd_norm, pool_w, pool_scale, mla_q_norm, mla_w_uq, mla_kv_norm, mla_w_ukv, w_out, ffn_norm, ffn_w_up, ffn_conv_w, ffn_conv_b, ffn_w_down, final_norm, loss_target, m_attn_norm, m_w_in, m_ssd_conv_w, m_ssd_conv_b, m_ssd_dt_bias, m_ssd_a_log, m_ssd_d, m_ssd_norm, m_pool_w, m_pool_scale, m_mla_q_norm, m_mla_w_uq, m_mla_kv_norm, m_mla_w_ukv, m_w_out, m_ffn_norm, m_ffn_w_up, m_ffn_conv_w, m_ffn_conv_b, m_ffn_w_down, m_final_norm, v_attn_norm, v_w_in, v_ssd_conv_w, v_ssd_conv_b, v_ssd_dt_bias, v_ssd_a_log, v_ssd_d, v_ssd_norm, v_pool_w, v_pool_scale, v_mla_q_norm, v_mla_w_uq, v_mla_kv_norm, v_mla_w_ukv, v_w_out, v_ffn_norm, v_ffn_w_up, v_ffn_conv_w, v_ffn_conv_b, v_ffn_w_down, v_final_norm):
    a = locals()
    Wt = {k: a[k] for k in ALL_W}
    Mo = {k: a["m_" + k] for k in ALL_W}
    Vo = {k: a["v_" + k] for k in ALL_W}
    Bl, S, _ = x.shape
    T = Bl * S

    names = list(SHARDED)
    conv = ("ssd_conv_w", "ffn_conv_w")

    def blocks_of(l, keys=names):
        return [Wt[k][l] if k in conv else Wt[k][l].astype(BF16) for k in keys]

    def replicated(l):
        return {
            "attn_norm": attn_norm[l].reshape(1, D), "ssd_conv_b": ssd_conv_b[l].reshape(1, SSD_XBC),
            "ssd_dt_bias": _lane_pad(ssd_dt_bias[l]), "ssd_a_log": _lane_pad(ssd_a_log[l]),
            "ssd_d": jnp.repeat(ssd_d[l].reshape(NPAIR, 2), SSD_P, axis=1), "ssd_norm": ssd_norm[l].reshape(1, SSD_W),
            "pool_w": pool_w[l].astype(BF16), "pool_scale": pool_scale[l].reshape(1, POOL_W),
            "mla_q_norm": mla_q_norm[l].reshape(1, MLA_QR), "mla_kv_norm": mla_kv_norm[l].reshape(1, MLA_KVR),
            "ffn_norm": ffn_norm[l].reshape(1, D), "ffn_conv_b": ffn_conv_b[l].reshape(1, 2 * DFF)}

    pos = positions.astype(F32).reshape(T, 1)
    inv_freq = ROPE_THETA ** (-jnp.arange(0, MLA_ROPE, 2, dtype=F32) / MLA_ROPE)
    invf = jnp.pad(jnp.concatenate([inv_freq, inv_freq]), (PE_LO, LANES - PE_HI)).reshape(1, LANES)
    cs, sn = rope_tables(pos, invf, name="rope_tables")

    first = all_gather_many(blocks_of(0, FIRST), name="gather_weights_l0")
    layers = [{**replicated(0), **_sharded_entries(FIRST, first)}]
    xc = x.reshape(T, D)
    saved = []
    for l in range(DEPTH):
        xc, sv, gathered, layers[l] = _layer_fwd(l, xc, layers[l], cs, sn, Bl,
                                                 next_blocks=blocks_of(l + 1) if l + 1 < DEPTH else None,
                                                 pending=blocks_of(0, REST) if l == 0 else None)
        saved.append(sv)
        if gathered is not None:
            layers.append({**replicated(l + 1), **_sharded_entries(names, gathered)})
    dx, dxb, g_final, loss_part = final_loss(xc, final_norm.reshape(1, D), loss_target.reshape(T, D), name="final_loss")

    grads = [None] * DEPTH
    reduced = {}

    def record(l, keys, red):
        for a, k in enumerate(keys):
            reduced[(l, k)] = (red[0][a], red[1][a])

    later_g8 = None
    for l in reversed(range(DEPTH)):
        dx, dxb, grads[l], red_late, red_early = _layer_bwd(l, dx, dxb, layers[l], saved[l], cs, sn, Bl, later_g8=later_g8)
        if red_late is not None:
            record(l + 1, LATE, red_late)
        record(l, EARLY, red_early)
        later_g8 = _by_owner(grads[l], LATE)
    loss = lax.psum(loss_part[0, 0], AXES)
    sums = _chip_sums(later_g8, run_stage(stage_rs_sibling(later_g8), name="rs_sibling_l0"), "rs_add_l0_")
    record(0, LATE, ([s32 for s32, _ in sums], run_stage(stage_rs_chips([sb for _, sb in sums]), name="rs_chips_l0")))

    out_g, out_d, out_m, out_v = {}, {}, {}, {}
    for k in names:
        outs = None
        for l in reversed(range(DEPTH)):
            s32, recv = reduced[(l, k)]
            outs = adamw_sharded(Wt[k], Mo[k], Vo[k], s32, recv, l, outs, name=f"adamw_l{l}_{k}")
        out_g[k], out_d[k], out_m[k], out_v[k] = (o.reshape(Wt[k].shape) for o in outs)

    part = {k: g_final.reshape(D) if k == "final_norm" else
            jnp.stack([grads[l][k].reshape(Wt[k].shape[1:]) for l in range(DEPTH)]) for k in SMALL}
    pg, layout = _pack_small(part)
    pw, _ = _pack_small(Wt)
    pm, _ = _pack_small(Mo)
    pv, _ = _pack_small(Vo)
    (pg8,) = all_gather_many([pg], name="gather_small_grads")
    sg, sd, sm, sv_ = adamw(pw, pm, pv, [pg8], name="adamw_small")
    for dst, buf in ((out_g, sg), (out_d, sd), (out_m, sm), (out_v, sv_)):
        dst.update(_unpack_small(buf, layout))

    return (loss, dx.reshape(Bl, S, D), *[out_g[k] for k in ALL_W], *[out_d[k] for k in ALL_W],
            *[out_m[k] for k in ALL_W], *[out_v[k] for k in ALL_W])
```

```python
import functools
import math

import jax
import jax.numpy as jnp
from jax import lax
from jax.experimental import pallas as pl
from jax.experimental.pallas import tpu as pltpu

F32, BF16 = jnp.float32, jnp.bfloat16
SDS = jax.ShapeDtypeStruct
MESH = pl.DeviceIdType.MESH
AXES = ("x", "y", "c")
N_DEV = 8

D = 1024
EPS = 1e-6
SSD_HEADS, SSD_P, SSD_W, SSD_G, SSD_N, SSD_K, SSD_L, SSD_XBC = 16, 64, 1024, 2, 128, 4, 128, 1536
POOL_G, POOL_D, POOL_W, POOL_WIN = 4, 128, 512, (2, 4, 8, 16)
MLA_H, MLA_QR, MLA_KVR, MLA_NOPE, MLA_ROPE, MLA_V, MLA_QK = 8, 384, 256, 64, 32, 64, 96
ROPE_THETA = 10000.0
MIX = 2048
DFF, FFN_K = 2816, 3
DEPTH = 2
ADAM_LR, ADAM_B1, ADAM_B2, ADAM_EPS, ADAM_WD, ADAM_STEP = 0.001, 0.9, 0.999, 1e-08, 0.01, 10

Z0, XBC0, U0, CQ0, CKV0, DT0, KPE0, PW = 0, 1024, 2560, 3072, 3584, 3840, 3968, 4096
LANES = 128
YCAT = SSD_W + POOL_W + MLA_H * LANES
NEG = -1e30
VMEM_LIMIT = 56 * 1024 * 1024
MM_ROW_TILE, MM_LANE_TILE, MM_FULL_K = 1024, 1408, 2816


def _tile(n, pref, mult):
    if n <= pref:
        return n
    for d in range(pref, 0, -mult):
        if d % mult == 0 and n % d == 0:
            return d
    return n


def _dg(a, b, ca, cb, prec=None):
    return lax.dot_general(a, b, (((ca,), (cb,)), ((), ())), preferred_element_type=F32, precision=prec)


def _nn(a, b):
    return _dg(a.astype(BF16), b.astype(BF16), 1, 0)


def _nt(a, b):
    return _dg(a.astype(BF16), b.astype(BF16), 1, 1)


def _tn(a, b):
    return _dg(a.astype(BF16), b.astype(BF16), 0, 0)


def _sig(x):
    return jax.nn.sigmoid(x)


def _silu(x):
    return x * _sig(x)


def _dsilu(x):
    s = _sig(x)
    return s * (1.0 + x * (1.0 - s))


ANY = pl.BlockSpec(memory_space=pl.ANY)


def _pc(body, *, name, grid, in_specs, out_specs, out_shape, scratch=(), comm=None):
    params = pltpu.CompilerParams(vmem_limit_bytes=VMEM_LIMIT)
    if comm is None:
        return pl.pallas_call(body, name=name, grid=grid, in_specs=in_specs, out_specs=out_specs, out_shape=out_shape,
                              scratch_shapes=list(scratch), compiler_params=params)
    single = not isinstance(out_shape, (list, tuple))
    o_specs = [out_specs] if single else list(out_specs)
    o_shape = [out_shape] if single else list(out_shape)
    ni, no, ns = len(in_specs), len(o_specs), len(scratch)
    nci, nco = len(comm["ins"]), len(comm["out_shape"])

    def fused(*refs):
        ins, cins = refs[:ni], refs[ni:ni + nci]
        outs, couts = refs[ni + nci:ni + nci + no], refs[ni + nci + no:ni + nci + no + nco]
        scr = refs[ni + nci + no + nco:ni + nci + no + nco + ns]
        send_sems, recv_sems = refs[-2:]
        copies = comm["copies"](cins, couts, send_sems, recv_sems)
        first = functools.reduce(jnp.logical_and, [pl.program_id(d) == 0 for d in range(len(grid))])
        last = functools.reduce(jnp.logical_and, [pl.program_id(d) == grid[d] - 1 for d in range(len(grid))])

        @pl.when(first)
        def _():
            for cp in copies:
                cp.start()

        body(*ins, *outs, *scr)

        @pl.when(last)
        def _():
            for cp in copies:
                cp.wait()

    call = pl.pallas_call(
        fused, name=name, grid=grid, in_specs=list(in_specs) + [ANY] * nci, out_specs=o_specs + [ANY] * nco,
        out_shape=o_shape + list(comm["out_shape"]),
        scratch_shapes=list(scratch) + [pltpu.SemaphoreType.DMA((comm["sems"],)), pltpu.SemaphoreType.DMA((comm["sems"],))],
        input_output_aliases={ni + a: no + a for a in range(comm.get("alias", 0))},
        compiler_params=params)

    def run(*args):
        res = call(*args, *comm["ins"])
        comm["result"] = list(res[no:])
        return res[0] if single else list(res[:no])

    return run


def _rsum(x):
    return jnp.sum(x, axis=1, keepdims=True)


def _csum(x):
    return jnp.sum(x, axis=0, keepdims=True)


def mm(a, b, *, ta=False, tb=False, add=None, out_dtype=F32, b_k0=0, comm=None, name):
    M, K = (a.shape[1], a.shape[0]) if ta else a.shape
    N = b.shape[0] if tb else b.shape[1]
    assert tb or b_k0 == 0
    tm = _tile(M, MM_LANE_TILE, LANES) if ta else _tile(M, MM_ROW_TILE, 8)
    tn = _tile(N, MM_LANE_TILE, LANES)
    if ta:
        tk = _tile(K, MM_ROW_TILE, 8)
    else:
        tk = K if K <= MM_FULL_K else _tile(K, 2048, LANES)
    nk = K // tk

    def body(*refs):
        if add is None:
            a_ref, b_ref, o_ref = refs[:3]
        else:
            a_ref, b_ref, add_ref, o_ref = refs[:4]
        part = _dg(a_ref[...].astype(BF16), b_ref[...].astype(BF16), 0 if ta else 1, 1 if tb else 0)

        def finish(r):
            if add is not None:
                r = r + add_ref[...].astype(F32)
            o_ref[...] = r.astype(out_dtype)

        if nk == 1:
            finish(part)
            return
        acc = refs[-1]
        k = pl.program_id(2)

        @pl.when(k == 0)
        def _():
            acc[...] = part

        @pl.when(k > 0)
        def _():
            acc[...] += part

        @pl.when(k == nk - 1)
        def _():
            finish(acc[...])

    a_spec = pl.BlockSpec((tk, tm), lambda i, j, k: (k, i)) if ta else pl.BlockSpec((tm, tk), lambda i, j, k: (i, k))
    assert b_k0 % tk == 0
    kb0 = b_k0 // tk
    b_spec = pl.BlockSpec((tn, tk), lambda i, j, k: (j, kb0 + k)) if tb else pl.BlockSpec((tk, tn), lambda i, j, k: (k, j))
    o_spec = pl.BlockSpec((tm, tn), lambda i, j, k: (i, j))
    ins, specs = [a, b], [a_spec, b_spec]
    if add is not None:
        ins.append(add)
        specs.append(o_spec)
    return _pc(body, name=name, grid=(M // tm, N // tn, nk), in_specs=specs, out_specs=o_spec,
               out_shape=SDS((M, N), out_dtype), scratch=[pltpu.VMEM((tm, tn), F32)] if nk > 1 else [], comm=comm)(*ins)


def rms_fwd(x, g, *, col0=0, width=None, name):
    T = x.shape[0]
    W = width or x.shape[1]
    tm = _tile(T, 512, 8)

    def body(x_ref, g_ref, o_ref):
        v = x_ref[...]
        r = lax.rsqrt(jnp.mean(v * v, axis=1, keepdims=True) + EPS)
        o_ref[...] = ((v * r) * g_ref[...]).astype(BF16)

    return _pc(body, name=name, grid=(T // tm,),
               in_specs=[pl.BlockSpec((tm, W), lambda i: (i, col0 // W)), pl.BlockSpec((1, W), lambda i: (0, 0))],
               out_specs=pl.BlockSpec((tm, W), lambda i: (i, 0)), out_shape=SDS((T, W), BF16))(x, g)


def rms_bwd(x, g, dh, *, col0=0, width=None, add=None, name):
    T = x.shape[0]
    W = width or x.shape[1]
    tm = _tile(T, 512, 8)

    def body(*refs):
        if add is None:
            x_ref, g_ref, dh_ref, dx_ref, dg_ref = refs
        else:
            x_ref, g_ref, dh_ref, add_ref, dx_ref, dxb_ref, dg_ref = refs
        v = x_ref[...]
        r = lax.rsqrt(jnp.mean(v * v, axis=1, keepdims=True) + EPS)
        xh = v * r
        d = dh_ref[...].astype(F32)
        dxh = d * g_ref[...]
        dx = r * (dxh - xh * jnp.mean(dxh * xh, axis=1, keepdims=True))
        if add is not None:
            dx = dx + add_ref[...]
            dxb_ref[...] = dx.astype(BF16)
        dx_ref[...] = dx.astype(dx_ref.dtype)

        @pl.when(pl.program_id(0) == 0)
        def _():
            dg_ref[...] = jnp.zeros_like(dg_ref)

        dg_ref[...] += _csum(d * xh)

    row = pl.BlockSpec((tm, W), lambda i: (i, 0))
    vec = pl.BlockSpec((1, W), lambda i: (0, 0))
    ins = [x, g, dh] + ([] if add is None else [add])
    specs = [pl.BlockSpec((tm, W), lambda i: (i, col0 // W)), vec, row] + ([] if add is None else [row])
    if add is None:
        return _pc(body, name=name, grid=(T // tm,), in_specs=specs, out_specs=[row, vec],
                   out_shape=[SDS((T, W), BF16), SDS((1, W), F32)])(*ins)
    return _pc(body, name=name, grid=(T // tm,), in_specs=specs, out_specs=[row, row, vec],
               out_shape=[SDS((T, W), F32), SDS((T, W), BF16), SDS((1, W), F32)])(*ins)


def gated_rms_fwd(y, proj, g, *, name):
    T = y.shape[0]
    tm = _tile(T, 512, 8)

    def body(y_ref, z_ref, g_ref, o_ref):
        v = y_ref[...] * _silu(z_ref[...])
        r = lax.rsqrt(jnp.mean(v * v, axis=1, keepdims=True) + EPS)
        o_ref[...] = ((v * r) * g_ref[...]).astype(BF16)

    row = pl.BlockSpec((tm, SSD_W), lambda i: (i, 0))
    return _pc(body, name=name, grid=(T // tm,), in_specs=[row, row, pl.BlockSpec((1, SSD_W), lambda i: (0, 0))],
               out_specs=row, out_shape=SDS((T, SSD_W), BF16))(y, proj, g)


def gated_rms_bwd(y, proj, g, dycat, *, name):
    T = y.shape[0]
    tm = _tile(T, 512, 8)

    def body(y_ref, z_ref, g_ref, d_ref, dy_ref, dz_ref, dg_ref):
        yv, z = y_ref[...], z_ref[...]
        sz = _silu(z)
        v = yv * sz
        r = lax.rsqrt(jnp.mean(v * v, axis=1, keepdims=True) + EPS)
        vh = v * r
        d = d_ref[...]
        dvh = d * g_ref[...]
        dv = r * (dvh - vh * jnp.mean(dvh * vh, axis=1, keepdims=True))
        dy_ref[...] = dv * sz
        dz_ref[...] = (dv * yv * _dsilu(z)).astype(BF16)

        @pl.when(pl.program_id(0) == 0)
        def _():
            dg_ref[...] = jnp.zeros_like(dg_ref)

        dg_ref[...] += _csum(d * vh)

    row = pl.BlockSpec((tm, SSD_W), lambda i: (i, 0))
    vec = pl.BlockSpec((1, SSD_W), lambda i: (0, 0))
    return _pc(body, name=name, grid=(T // tm,), in_specs=[row, row, vec, row], out_specs=[row, row, vec],
               out_shape=[SDS((T, SSD_W), F32), SDS((T, SSD_W), BF16), SDS((1, SSD_W), F32)])(y, proj, g, dycat)


def final_loss(x, g, tgt, *, name):
    T = x.shape[0]
    tm = _tile(T, 512, 8)

    def body(x_ref, g_ref, t_ref, dx_ref, dxb_ref, dg_ref, l_ref):
        v = x_ref[...]
        gg = g_ref[...]
        r = lax.rsqrt(jnp.mean(v * v, axis=1, keepdims=True) + EPS)
        xh = v * r
        err = xh * gg - t_ref[...]
        part = 0.5 * _csum(jnp.mean(err * err, axis=1, keepdims=True))
        d = err * (1.0 / D)
        dxh = d * gg
        dx = r * (dxh - xh * jnp.mean(dxh * xh, axis=1, keepdims=True))
        dx_ref[...] = dx
        dxb_ref[...] = dx.astype(BF16)

        @pl.when(pl.program_id(0) == 0)
        def _():
            dg_ref[...] = jnp.zeros_like(dg_ref)
            l_ref[...] = jnp.zeros_like(l_ref)

        dg_ref[...] += _csum(d * xh)
        l_ref[...] += jnp.broadcast_to(part, (1, LANES))

    row = pl.BlockSpec((tm, D), lambda i: (i, 0))
    vec = pl.BlockSpec((1, D), lambda i: (0, 0))
    return _pc(body, name=name, grid=(T // tm,), in_specs=[row, vec, row],
               out_specs=[row, row, vec, pl.BlockSpec((1, LANES), lambda i: (0, 0))],
               out_shape=[SDS((T, D), F32), SDS((T, D), BF16), SDS((1, D), F32), SDS((1, LANES), F32)])(x, g, tgt)


HALO = 8


def _prev_map(ts, col):
    return lambda b, i, j: (b, jnp.maximum(i * (ts // HALO) - 1, 0), col(j))


def _next_map(ts, n_halo_blocks, col):
    return lambda b, i, j: (b, jnp.minimum((i + 1) * (ts // HALO), n_halo_blocks - 1), col(j))


def _row_chunks(ts, rows):
    rows = min(rows, ts)
    return [(r, rows) for r in range(0, ts, rows)]


def _conv_rows(ext, w_ref, b_ref, r0, n, K):
    win = ext[r0:r0 + HALO + n, :]
    taps = [pltpu.roll(win, K - 1 - k, 0)[HALO:HALO + n] if k < K - 1 else win[HALO:HALO + n] for k in range(K)]
    acc = b_ref[...] + w_ref[0:1, :] * taps[0]
    for k in range(1, K):
        acc = acc + w_ref[k:k + 1, :] * taps[k]
    return acc, taps


def _conv_t_rows(ext2, w_ref, r0, n, K):
    win = ext2[r0:r0 + n + HALO, :]
    dx = w_ref[K - 1:K, :] * win[0:n]
    for k in range(K - 1):
        dx = dx + w_ref[k:k + 1, :] * pltpu.roll(win, n + HALO - (K - 1 - k), 0)[0:n]
    return dx


def _sum8(x):
    acc = x[0:8]
    for r in range(8, x.shape[0], 8):
        acc = acc + x[r:r + 8]
    return acc


def conv_silu_fwd(proj3, w, b, *, name):
    Bl, S, _ = proj3.shape
    C, K = SSD_XBC, SSD_K
    ts, tc = _tile(S, 512, 8), 512
    c0 = XBC0 // tc

    def body(xp_ref, x_ref, w_ref, b_ref, o_ref, ext):
        i = pl.program_id(1)
        ext[0:HALO, :] = jnp.where(i > 0, xp_ref[0], 0.0)
        ext[HALO:HALO + ts, :] = x_ref[0]
        for r0, n in _row_chunks(ts, 32):
            acc, _ = _conv_rows(ext, w_ref, b_ref, r0, n, K)
            o_ref[0, r0:r0 + n, :] = _silu(acc)

    return _pc(body, name=name, grid=(Bl, S // ts, C // tc),
               in_specs=[pl.BlockSpec((1, HALO, tc), _prev_map(ts, lambda j: c0 + j)),
                         pl.BlockSpec((1, ts, tc), lambda b, i, j: (b, i, c0 + j)),
                         pl.BlockSpec((K, tc), lambda b, i, j: (0, j)),
                         pl.BlockSpec((1, tc), lambda b, i, j: (0, j))],
               out_specs=pl.BlockSpec((1, ts, tc), lambda b, i, j: (b, i, j)),
               out_shape=SDS((Bl, S, C), F32), scratch=[pltpu.VMEM((HALO + ts, tc), F32)])(proj3, proj3, w, b)


def conv_silu_bwd(proj3, w, b, dact, *, name):
    Bl, S, _ = proj3.shape
    C, K = SSD_XBC, SSD_K
    ts, tc = _tile(S, 512, 8), 512
    c0 = XBC0 // tc
    ns = S // ts

    def body(xp_ref, x_ref, xn_ref, d_ref, dn_ref, w_ref, b_ref, dx_ref, dw_ref, db_ref, ext, ext2):
        bb, i = pl.program_id(1), pl.program_id(2)
        last = i == ns - 1
        ext[0:HALO, :] = jnp.where(i > 0, xp_ref[0], 0.0)
        ext[HALO:HALO + ts, :] = x_ref[0]
        ext[HALO + ts:2 * HALO + ts, :] = jnp.where(last, 0.0, xn_ref[0])
        dw = [jnp.zeros((8, tc), F32) for _ in range(K)]
        db = jnp.zeros((8, tc), F32)
        for r0, n in _row_chunks(ts, 16) + [(ts, HALO)]:
            acc, taps = _conv_rows(ext, w_ref, b_ref, r0, n, K)
            d = d_ref[0, r0:r0 + n, :] if r0 < ts else jnp.where(last, 0.0, dn_ref[0])
            du = d * _dsilu(acc)
            ext2[r0:r0 + n, :] = du
            if r0 < ts:
                dw = [a + _sum8(du * t) for a, t in zip(dw, taps)]
                db = db + _sum8(du)
        for r0, n in _row_chunks(ts, 32):
            dx_ref[0, r0:r0 + n, :] = _conv_t_rows(ext2, w_ref, r0, n, K).astype(BF16)

        @pl.when((bb == 0) & (i == 0))
        def _():
            dw_ref[...] = jnp.zeros_like(dw_ref)
            db_ref[...] = jnp.zeros_like(db_ref)

        for k in range(K):
            dw_ref[k:k + 1, :] += _csum(dw[k])
        db_ref[...] += _csum(db)

    nhb = S // HALO
    cx = lambda j: c0 + j
    cj = lambda j: j
    return _pc(body, name=name, grid=(C // tc, Bl, ns),
               in_specs=[pl.BlockSpec((1, HALO, tc), lambda j, b, i: _prev_map(ts, cx)(b, i, j)),
                         pl.BlockSpec((1, ts, tc), lambda j, b, i: (b, i, c0 + j)),
                         pl.BlockSpec((1, HALO, tc), lambda j, b, i: _next_map(ts, nhb, cx)(b, i, j)),
                         pl.BlockSpec((1, ts, tc), lambda j, b, i: (b, i, j)),
                         pl.BlockSpec((1, HALO, tc), lambda j, b, i: _next_map(ts, nhb, cj)(b, i, j)),
                         pl.BlockSpec((K, tc), lambda j, b, i: (0, j)),
                         pl.BlockSpec((1, tc), lambda j, b, i: (0, j))],
               out_specs=[pl.BlockSpec((1, ts, tc), lambda j, b, i: (b, i, j)),
                          pl.BlockSpec((K, tc), lambda j, b, i: (0, j)),
                          pl.BlockSpec((1, tc), lambda j, b, i: (0, j))],
               out_shape=[SDS((Bl, S, C), BF16), SDS((K, C), F32), SDS((1, C), F32)],
               scratch=[pltpu.VMEM((2 * HALO + ts, tc), F32), pltpu.VMEM((HALO + ts, tc), F32)],
               )(proj3, proj3, proj3, dact, dact, w, b)


def ffn_act_fwd(pre3, w, b, *, name):
    Bl, S, _ = pre3.shape
    K = FFN_K
    ts, tc = _tile(S, 512, 8), 256
    nj = DFF // tc

    def body(gp_ref, g_ref, vp_ref, v_ref, wg_ref, wv_ref, bg_ref, bv_ref, o_ref, eg, ev):
        i = pl.program_id(1)
        for p_ref, m_ref, ext in ((gp_ref, g_ref, eg), (vp_ref, v_ref, ev)):
            ext[0:HALO, :] = jnp.where(i > 0, p_ref[0], 0.0)
            ext[HALO:HALO + ts, :] = m_ref[0]
        for r0, n in _row_chunks(ts, 64):
            ug, _ = _conv_rows(eg, wg_ref, bg_ref, r0, n, K)
            uv, _ = _conv_rows(ev, wv_ref, bv_ref, r0, n, K)
            o_ref[0, r0:r0 + n, :] = (_silu(ug) * uv).astype(BF16)

    main = lambda off: pl.BlockSpec((1, ts, tc), lambda b, i, j: (b, i, off + j))
    prev = lambda off: pl.BlockSpec((1, HALO, tc), _prev_map(ts, lambda j: off + j))
    wsp = lambda off: pl.BlockSpec((K, tc), lambda b, i, j: (0, off + j))
    bsp = lambda off: pl.BlockSpec((1, tc), lambda b, i, j: (0, off + j))
    return _pc(body, name=name, grid=(Bl, S // ts, nj),
               in_specs=[prev(0), main(0), prev(nj), main(nj), wsp(0), wsp(nj), bsp(0), bsp(nj)],
               out_specs=pl.BlockSpec((1, ts, tc), lambda b, i, j: (b, i, j)),
               out_shape=SDS((Bl, S, DFF), BF16),
               scratch=[pltpu.VMEM((HALO + ts, tc), F32), pltpu.VMEM((HALO + ts, tc), F32)],
               )(pre3, pre3, pre3, pre3, w, w, b, b)


def ffn_act_bwd(pre3, w, b, dact, *, comm=None, name):
    Bl, S, _ = pre3.shape
    K = FFN_K
    ts, tc = _tile(S, 512, 8), 256
    nj = DFF // tc
    ns = S // ts

    def body(gp_ref, g_ref, gn_ref, vp_ref, v_ref, vn_ref, d_ref, dn_ref, wg_ref, wv_ref, bg_ref, bv_ref,
             dg_ref, dv_ref, dwg_ref, dwv_ref, dbg_ref, dbv_ref, eg, ev, e2g, e2v):
        bb, i = pl.program_id(1), pl.program_id(2)
        last = i == ns - 1
        for p_ref, m_ref, n_ref, ext in ((gp_ref, g_ref, gn_ref, eg), (vp_ref, v_ref, vn_ref, ev)):
            ext[0:HALO, :] = jnp.where(i > 0, p_ref[0], 0.0)
            ext[HALO:HALO + ts, :] = m_ref[0]
            ext[HALO + ts:2 * HALO + ts, :] = jnp.where(last, 0.0, n_ref[0])
        zero8 = jnp.zeros((8, tc), F32)
        dwg, dwv, dbg, dbv = [zero8] * K, [zero8] * K, zero8, zero8
        for r0, n in _row_chunks(ts, 32) + [(ts, HALO)]:
            ug, tg = _conv_rows(eg, wg_ref, bg_ref, r0, n, K)
            uv, tv = _conv_rows(ev, wv_ref, bv_ref, r0, n, K)
            d = d_ref[0, r0:r0 + n, :] if r0 < ts else jnp.where(last, 0.0, dn_ref[0])
            dug = d * uv * _dsilu(ug)
            duv = d * _silu(ug)
            e2g[r0:r0 + n, :] = dug
            e2v[r0:r0 + n, :] = duv
            if r0 < ts:
                dwg = [a + _sum8(dug * t) for a, t in zip(dwg, tg)]
                dwv = [a + _sum8(duv * t) for a, t in zip(dwv, tv)]
                dbg, dbv = dbg + _sum8(dug), dbv + _sum8(duv)
        for w_ref, e2, o_ref in ((wg_ref, e2g, dg_ref), (wv_ref, e2v, dv_ref)):
            for r0, n in _row_chunks(ts, 64):
                o_ref[0, r0:r0 + n, :] = _conv_t_rows(e2, w_ref, r0, n, K).astype(BF16)

        @pl.when((bb == 0) & (i == 0))
        def _():
            for r in (dwg_ref, dwv_ref, dbg_ref, dbv_ref):
                r[...] = jnp.zeros_like(r)

        for dw_ref, dw, db_ref, db in ((dwg_ref, dwg, dbg_ref, dbg), (dwv_ref, dwv, dbv_ref, dbv)):
            for k in range(K):
                dw_ref[k:k + 1, :] += _csum(dw[k])
            db_ref[...] += _csum(db)

    nhb = S // HALO
    main = lambda off: pl.BlockSpec((1, ts, tc), lambda j, b, i: (b, i, off + j))
    prev = lambda off: pl.BlockSpec((1, HALO, tc), lambda j, b, i: _prev_map(ts, lambda jj: off + jj)(b, i, j))
    nxt = lambda off: pl.BlockSpec((1, HALO, tc), lambda j, b, i: _next_map(ts, nhb, lambda jj: off + jj)(b, i, j))
    wsp = lambda off: pl.BlockSpec((K, tc), lambda j, b, i: (0, off + j))
    bsp = lambda off: pl.BlockSpec((1, tc), lambda j, b, i: (0, off + j))
    outs = _pc(body, name=name, grid=(nj, Bl, ns),
               in_specs=[prev(0), main(0), nxt(0), prev(nj), main(nj), nxt(nj), main(0), nxt(0),
                         wsp(0), wsp(nj), bsp(0), bsp(nj)],
               out_specs=[main(0), main(0), wsp(0), wsp(0), bsp(0), bsp(0)],
               out_shape=[SDS((Bl, S, DFF), BF16), SDS((Bl, S, DFF), BF16), SDS((K, DFF), F32), SDS((K, DFF), F32),
                          SDS((1, DFF), F32), SDS((1, DFF), F32)],
               scratch=[pltpu.VMEM((2 * HALO + ts, tc), F32), pltpu.VMEM((2 * HALO + ts, tc), F32),
                        pltpu.VMEM((HALO + ts, tc), F32), pltpu.VMEM((HALO + ts, tc), F32)],
               comm=comm)(pre3, pre3, pre3, pre3, pre3, pre3, dact, dact, w, w, b, b)
    return outs


PHALO = 16


def _pool_window_sums(win, trailing):
    rows = win.shape[0]
    out, s = [], win
    for w in POOL_WIN:
        half = w // 2
        s = s + pltpu.roll(s, half if trailing else rows - half, 0)
        out.append(s)
    return out


def _pick(g, vals):
    r = vals[-1]
    for k in range(len(vals) - 2, -1, -1):
        r = jnp.where(g == k, vals[k], r)
    return r


def _pool_count(g, i, ts, rows, r0=0):
    t = (i * ts + r0 + lax.broadcasted_iota(jnp.int32, (rows, 1), 0) + 1).astype(F32)
    return jnp.minimum(t, _pick(g, [float(w) for w in POOL_WIN]))


def _fill_pool_ext(up_ref, u_ref, ext, i, ts):
    ext[0:PHALO, :] = jnp.where(i > 0, up_ref[0], 0.0)
    ext[PHALO:PHALO + ts, :] = u_ref[0]


def _pooled_rows(ext, g, i, ts, r0, n):
    win = ext[r0:r0 + n + PHALO, :]
    sums = _pool_window_sums(win, True)
    return _pick(g, sums)[PHALO:PHALO + n] / _pool_count(g, i, ts, n, r0) - win[PHALO:PHALO + n]


def pool_fwd(proj3, pool_w, scale, *, name):
    Bl, S, _ = proj3.shape
    ts = _tile(S, 512, 16)
    c0 = U0 // POOL_D

    def body(up_ref, u_ref, w_ref, s_ref, o_ref, ext):
        i, g = pl.program_id(1), pl.program_id(2)
        _fill_pool_ext(up_ref, u_ref, ext, i, ts)
        wm, sc = w_ref[0], s_ref[...]
        for r0, n in _row_chunks(ts, 128):
            o_ref[0, r0:r0 + n, :] = (_nn(_pooled_rows(ext, g, i, ts, r0, n), wm) * sc).astype(BF16)

    return _pc(body, name=name, grid=(Bl, S // ts, POOL_G),
               in_specs=[pl.BlockSpec((1, PHALO, POOL_D), lambda b, i, g: (b, jnp.maximum(i * (ts // PHALO) - 1, 0), c0 + g)),
                         pl.BlockSpec((1, ts, POOL_D), lambda b, i, g: (b, i, c0 + g)),
                         pl.BlockSpec((1, POOL_D, POOL_D), lambda b, i, g: (g, 0, 0)),
                         pl.BlockSpec((1, POOL_D), lambda b, i, g: (0, g))],
               out_specs=pl.BlockSpec((1, ts, POOL_D), lambda b, i, g: (b, i, g)),
               out_shape=SDS((Bl, S, POOL_W), BF16), scratch=[pltpu.VMEM((PHALO + ts, POOL_D), F32)],
               )(proj3, proj3, pool_w, scale)


def pool_bwd(proj3, pool_w, scale, dycat3, *, name):
    Bl, S, _ = proj3.shape
    ts = _tile(S, 512, 16)
    ns = S // ts
    c0 = U0 // POOL_D
    d0 = SSD_W // POOL_D
    nhb = S // PHALO

    def body(up_ref, u_ref, d_ref, dn_ref, w_ref, s_ref, du_ref, dw_ref, ds_ref, ext, ext2):
        g, bb, i = pl.program_id(0), pl.program_id(1), pl.program_id(2)
        last = i == ns - 1
        _fill_pool_ext(up_ref, u_ref, ext, i, ts)
        wm = w_ref[0]
        sc = s_ref[...]
        dwa = jnp.zeros((POOL_D, POOL_D), F32)
        dsa = jnp.zeros((8, POOL_D), F32)
        dpools = []
        for r0, n in _row_chunks(ts, 128):
            pooled = _pooled_rows(ext, g, i, ts, r0, n)
            dy = d_ref[0, r0:r0 + n, :]
            dp = dy * sc
            dpool = _nt(dp, wm)
            dpools.append(dpool)
            ext2[r0:r0 + n, :] = dpool / _pool_count(g, i, ts, n, r0)
            dwa = dwa + _tn(pooled, dp)
            dsa = dsa + _sum8(dy * _nn(pooled, wm))
        dpool_n = _nt(jnp.where(last, 0.0, dn_ref[0]) * sc, wm)
        ext2[ts:ts + PHALO, :] = dpool_n / _pool_count(g, i + 1, ts, PHALO)
        for (r0, n), dpool in zip(_row_chunks(ts, 128), dpools):
            sums = _pool_window_sums(ext2[r0:r0 + n + PHALO, :], False)
            du_ref[0, r0:r0 + n, :] = (_pick(g, sums)[0:n] - dpool).astype(BF16)

        @pl.when((bb == 0) & (i == 0))
        def _():
            dw_ref[...] = jnp.zeros_like(dw_ref)
            ds_ref[...] = jnp.zeros_like(ds_ref)

        dw_ref[0] += dwa
        ds_ref[...] += _csum(dsa)

    return _pc(body, name=name, grid=(POOL_G, Bl, ns),
               in_specs=[pl.BlockSpec((1, PHALO, POOL_D), lambda g, b, i: (b, jnp.maximum(i * (ts // PHALO) - 1, 0), c0 + g)),
                         pl.BlockSpec((1, ts, POOL_D), lambda g, b, i: (b, i, c0 + g)),
                         pl.BlockSpec((1, ts, POOL_D), lambda g, b, i: (b, i, d0 + g)),
                         pl.BlockSpec((1, PHALO, POOL_D), lambda g, b, i: (b, jnp.minimum((i + 1) * (ts // PHALO), nhb - 1), d0 + g)),
                         pl.BlockSpec((1, POOL_D, POOL_D), lambda g, b, i: (g, 0, 0)),
                         pl.BlockSpec((1, POOL_D), lambda g, b, i: (0, g))],
               out_specs=[pl.BlockSpec((1, ts, POOL_D), lambda g, b, i: (b, i, g)),
                          pl.BlockSpec((1, POOL_D, POOL_D), lambda g, b, i: (g, 0, 0)),
                          pl.BlockSpec((1, POOL_D), lambda g, b, i: (0, g))],
               out_shape=[SDS((Bl, S, POOL_W), BF16), SDS((POOL_G, POOL_D, POOL_D), F32), SDS((1, POOL_W), F32)],
               scratch=[pltpu.VMEM((PHALO + ts, POOL_D), F32), pltpu.VMEM((PHALO + ts, POOL_D), F32)],
               )(proj3, proj3, dycat3, dycat3, pool_w, scale)


NPAIR = SSD_HEADS // 2


def _ssd_common(sm, bias, alog):
    L = SSD_L
    dt = jax.nn.softplus(sm + bias)
    a = -jnp.exp(alog)
    da = dt * a
    r = lax.broadcasted_iota(jnp.int32, (L, L), 0)
    c = lax.broadcasted_iota(jnp.int32, (L, L), 1)
    tri = (r >= c).astype(F32)
    cum = _dg(tri, da, 1, 0, lax.Precision.HIGHEST)
    return dt, a, cum, cum.T, r >= c


def _lanes(lo, hi, shape=(1, LANES)):
    lane = lax.broadcasted_iota(jnp.int32, shape, len(shape) - 1)
    return (lane >= lo) & (lane < hi)


def _onehot_lane(h):
    return (lax.broadcasted_iota(jnp.int32, (1, LANES), 1) == h).astype(F32)


def _split_nn(a, e):
    hi = a.astype(BF16)
    lo = (a - hi.astype(F32)).astype(BF16)
    return _dg(hi, e, 1, 0) + _dg(lo, e, 1, 0)


def _head_spread():
    r = lax.broadcasted_iota(jnp.int32, (LANES, SSD_W), 0)
    c = lax.broadcasted_iota(jnp.int32, (LANES, SSD_W), 1)
    return (c // SSD_P == r).astype(BF16)


def _pair_gather(j):
    r = lax.broadcasted_iota(jnp.int32, (LANES, LANES), 0)
    c = lax.broadcasted_iota(jnp.int32, (LANES, LANES), 1)
    return (c == 2 * j + (r >= SSD_P).astype(jnp.int32)).astype(BF16)


def ssd_fwd(xbc3, proj3, bias, alog, dskip, *, comm=None, name):
    Bl, S, _ = xbc3.shape
    L = SSD_L
    nc = S // L

    def body(xbc_ref, sm_ref, bias_ref, alog_ref, d_ref, y_ref, hin_ref, H):
        c = pl.program_id(1)

        @pl.when(c == 0)
        def _():
            H[...] = jnp.zeros_like(H)

        dt, a, cum, cumT, mask = _ssd_common(sm_ref[0], bias_ref[...], alog_ref[...])
        lo = _lanes(0, SSD_P)
        rowlo = lax.broadcasted_iota(jnp.int32, (LANES, LANES), 0) < SSD_P
        spread = _head_spread()
        dt_x = _split_nn(dt, spread)
        el_x = _split_nn(jnp.exp(cum), spread)
        wl_x = _split_nn(jnp.exp(cum[L - 1:L, :] - cum), spread)
        cb = []
        for g in range(SSD_G):
            Bg = xbc_ref[0, :, SSD_W + g * SSD_N:SSD_W + (g + 1) * SSD_N]
            Cg = xbc_ref[0, :, SSD_W + SSD_G * SSD_N + g * SSD_N:SSD_W + SSD_G * SSD_N + (g + 1) * SSD_N]
            cb.append((Bg, Cg, _nt(Cg, Bg)))
        for j in range(NPAIR):
            h0, h1 = 2 * j, 2 * j + 1
            sl = slice(j * LANES, (j + 1) * LANES)
            Bg, Cg, CB = cb[j // (NPAIR // SSD_G)]
            X = xbc_ref[0, :, sl]
            c0, c1 = cum[:, h0:h0 + 1], cum[:, h1:h1 + 1]
            r0, r1 = cumT[h0:h0 + 1, :], cumT[h1:h1 + 1, :]
            cl0, cl1 = cum[L - 1:L, h0:h0 + 1], cum[L - 1:L, h1:h1 + 1]
            Xt = X * dt_x[:, sl]
            M0 = CB * jnp.exp(jnp.where(mask, c0 - r0, NEG))
            M1 = CB * jnp.exp(jnp.where(mask, c1 - r1, NEG))
            Yd = jnp.where(lo, _nn(M0, Xt), _nn(M1, Xt))
            Hp = H[j]
            hin_ref[0, 0, j] = Hp
            Z = _nt(Cg, Hp)
            y_ref[0, :, sl] = Yd + el_x[:, sl] * Z + X * d_ref[j:j + 1, :]
            H[j] = jnp.where(rowlo, jnp.exp(cl0), jnp.exp(cl1)) * Hp + _tn(wl_x[:, sl] * Xt, Bg)

    vec = pl.BlockSpec((1, LANES), lambda b, c: (0, 0))
    return _pc(body, name=name, grid=(Bl, nc),
               in_specs=[pl.BlockSpec((1, L, SSD_XBC), lambda b, c: (b, c, 0)),
                         pl.BlockSpec((1, L, LANES), lambda b, c: (b, c, DT0 // LANES)),
                         vec, vec, pl.BlockSpec((NPAIR, LANES), lambda b, c: (0, 0))],
               out_specs=[pl.BlockSpec((1, L, SSD_W), lambda b, c: (b, c, 0)),
                          pl.BlockSpec((1, 1, NPAIR, LANES, LANES), lambda b, c: (b, c, 0, 0, 0))],
               out_shape=[SDS((Bl, S, SSD_W), F32), SDS((Bl, nc, NPAIR, LANES, LANES), F32)],
               scratch=[pltpu.VMEM((NPAIR, LANES, LANES), F32)], comm=comm)(xbc3, proj3, bias, alog, dskip)


def ssd_bwd(xbc3, proj3, hin, dy3, bias, alog, dskip, *, comm=None, name):
    Bl, S, _ = xbc3.shape
    L = SSD_L
    nc = S // L

    def body(xbc_ref, sm_ref, hin_ref, dy_ref, bias_ref, alog_ref, d_ref, dx_ref, ddt_ref, dpar_ref, dd_ref, dH, ddacc):
        bb, i = pl.program_id(0), pl.program_id(1)

        @pl.when(i == 0)
        def _():
            dH[...] = jnp.zeros_like(dH)

        @pl.when((bb == 0) & (i == 0))
        def _():
            dpar_ref[...] = jnp.zeros_like(dpar_ref)
            ddacc[...] = jnp.zeros_like(ddacc)

        sm = sm_ref[0]
        dt, a, cum, cumT, mask = _ssd_common(sm, bias_ref[...], alog_ref[...])
        maskf = mask.astype(F32)
        lo = _lanes(0, SSD_P)
        rowlo = lax.broadcasted_iota(jnp.int32, (LANES, LANES), 0) < SSD_P
        lastrow = (lax.broadcasted_iota(jnp.int32, (L, 1), 0) == L - 1).astype(F32)
        dcum = jnp.zeros((L, LANES), F32)
        dcum_t = jnp.zeros((LANES, L), F32)
        ddt = jnp.zeros((L, LANES), F32)
        spread = _head_spread()
        ones = jnp.ones((L, LANES), BF16)
        ecum = jnp.exp(cum)
        wall = jnp.exp(cum[L - 1:L, :] - cum)
        dt_x = _split_nn(dt, spread)
        el_x = _split_nn(ecum, spread)
        wl_x = _split_nn(wall, spread)
        headrow = lax.broadcasted_iota(jnp.int32, (LANES, 1), 0)
        grp = []
        for g in range(SSD_G):
            Bg = xbc_ref[0, :, SSD_W + g * SSD_N:SSD_W + (g + 1) * SSD_N]
            Cg = xbc_ref[0, :, SSD_W + SSD_G * SSD_N + g * SSD_N:SSD_W + SSD_G * SSD_N + (g + 1) * SSD_N]
            grp.append(dict(B=Bg, C=Cg, CB=_nt(Cg, Bg), dB=jnp.zeros((L, SSD_N), F32), dC=jnp.zeros((L, SSD_N), F32),
                            dCB=jnp.zeros((L, L), F32)))
        for j in range(NPAIR):
            h0, h1 = 2 * j, 2 * j + 1
            sl = slice(j * LANES, (j + 1) * LANES)
            G = grp[j // (NPAIR // SSD_G)]
            Bg, Cg, CB = G["B"], G["C"], G["CB"]
            X = xbc_ref[0, :, sl]
            dY = dy_ref[0, :, sl]
            c0, c1 = cum[:, h0:h0 + 1], cum[:, h1:h1 + 1]
            r0, r1 = cumT[h0:h0 + 1, :], cumT[h1:h1 + 1, :]
            cl0, cl1 = cum[L - 1:L, h0:h0 + 1], cum[L - 1:L, h1:h1 + 1]
            oh0, oh1 = _onehot_lane(h0), _onehot_lane(h1)
            gather = _pair_gather(j)
            dtl, el, wl = dt_x[:, sl], el_x[:, sl], wl_x[:, sl]
            Xt = X * dtl
            Hp = hin_ref[0, 0, j]
            dS = dH[j]
            dX = dY * d_ref[j:j + 1, :]
            ddacc[j:j + 1, :] += _csum(dY * X)
            Z = _nt(Cg, Hp)
            dZ = dY * el
            dcum = dcum + _split_nn(dY * Z, gather) * ecum
            G["dC"] = G["dC"] + _nn(dZ, Hp)
            dHy = _tn(dZ, Cg)
            Gm = _nt(Bg, dS)
            dXt = wl * Gm
            q = _split_nn(Xt * Gm, gather) * wall
            dcum = dcum + lastrow * _csum(q) - q
            G["dB"] = G["dB"] + _nn(wl * Xt, dS)
            g0, g1 = jnp.exp(cl0), jnp.exp(cl1)
            rowsum = _nn(dS * Hp, ones)
            dg0 = _csum(jnp.where(rowlo, rowsum, 0.0))
            dg1 = _csum(jnp.where(rowlo, 0.0, rowsum))
            dcum = dcum + lastrow * ((dg0 * g0) * oh0 + (dg1 * g1) * oh1)
            dH[j] = jnp.where(rowlo, g0, g1) * dS + dHy
            for h, ch, rh, mh, oh in ((h0, c0, r0, lo, oh0), (h1, c1, r1, jnp.logical_not(lo), oh1)):
                decay = jnp.exp(jnp.where(mask, ch - rh, NEG))
                Mh = CB * decay
                dM = _nt(jnp.where(mh, dY, 0.0), Xt) * maskf
                dXt = dXt + jnp.where(mh, _tn(Mh, dY), 0.0)
                G["dCB"] = G["dCB"] + dM * decay
                Q = dM * Mh
                dcum = dcum + _rsum(Q) * oh
                dcum_t = dcum_t + (headrow == h).astype(F32) * _csum(Q)
            dX = dX + dXt * dtl
            ddt = ddt + _split_nn(dXt * X, gather)
            dx_ref[0, :, sl] = dX
        dcum = dcum - dcum_t.T
        for g in range(SSD_G):
            G = grp[g]
            dC = G["dC"] + _nn(G["dCB"], G["B"])
            dB = G["dB"] + _tn(G["dCB"], G["C"])
            dx_ref[0, :, SSD_W + g * SSD_N:SSD_W + (g + 1) * SSD_N] = dB
            dx_ref[0, :, SSD_W + SSD_G * SSD_N + g * SSD_N:SSD_W + SSD_G * SSD_N + (g + 1) * SSD_N] = dC
        r = lax.broadcasted_iota(jnp.int32, (L, L), 0)
        c = lax.broadcasted_iota(jnp.int32, (L, L), 1)
        dda = _dg((c >= r).astype(F32), dcum, 1, 0, lax.Precision.HIGHEST)
        heads = _lanes(0, SSD_HEADS)
        ddt = ddt + dda * a
        draw = jnp.where(heads, ddt * _sig(sm + bias_ref[...]), 0.0)
        ddt_ref[0] = draw.astype(BF16)
        dpar_ref[0:1, :] += _csum(draw)
        dpar_ref[1:2, :] += _csum(jnp.where(heads, dda * dt * a, 0.0))

        @pl.when((bb == Bl - 1) & (i == nc - 1))
        def _():
            acc = ddacc[...]
            lane = lax.broadcasted_iota(jnp.int32, (NPAIR, LANES), 1)
            s0 = _rsum(jnp.where(lane < SSD_P, acc, 0.0))
            s1 = _rsum(jnp.where(lane < SSD_P, 0.0, acc))
            dd_ref[...] = jnp.where(lane == 0, s0, jnp.where(lane == 1, s1, 0.0))

    vec = pl.BlockSpec((1, LANES), lambda b, i: (0, 0))
    par = pl.BlockSpec((NPAIR, LANES), lambda b, i: (0, 0))
    return _pc(body, name=name, grid=(Bl, nc),
               in_specs=[pl.BlockSpec((1, L, SSD_XBC), lambda b, i: (b, nc - 1 - i, 0)),
                         pl.BlockSpec((1, L, LANES), lambda b, i: (b, nc - 1 - i, DT0 // LANES)),
                         pl.BlockSpec((1, 1, NPAIR, LANES, LANES), lambda b, i: (b, nc - 1 - i, 0, 0, 0)),
                         pl.BlockSpec((1, L, SSD_W), lambda b, i: (b, nc - 1 - i, 0)),
                         vec, vec, par],
               out_specs=[pl.BlockSpec((1, L, SSD_XBC), lambda b, i: (b, nc - 1 - i, 0)),
                          pl.BlockSpec((1, L, LANES), lambda b, i: (b, nc - 1 - i, 0)),
                          par, par],
               out_shape=[SDS((Bl, S, SSD_XBC), F32), SDS((Bl, S, LANES), BF16), SDS((NPAIR, LANES), F32),
                          SDS((NPAIR, LANES), F32)],
               scratch=[pltpu.VMEM((NPAIR, LANES, LANES), F32), pltpu.VMEM((NPAIR, LANES), F32)],
               comm=comm)(xbc3, proj3, hin, dy3, bias, alog, dskip)


PE_LO, PE_MID, PE_HI = MLA_NOPE, MLA_NOPE + MLA_ROPE // 2, MLA_NOPE + MLA_ROPE
ATT_SCALE = 1.0 / math.sqrt(MLA_QK)


def _swap_matrix():
    src = lax.broadcasted_iota(jnp.int32, (LANES, LANES), 0)
    dst = lax.broadcasted_iota(jnp.int32, (LANES, LANES), 1)
    half = MLA_ROPE // 2
    first = (dst >= PE_LO) & (dst < PE_MID) & (src == dst + half)
    second = (dst >= PE_MID) & (dst < PE_HI) & (src == dst - half)
    return (second.astype(F32) - first.astype(F32)).astype(BF16)


def rope_tables(pos, invf, *, name):
    T = pos.shape[0]
    tm = _tile(T, 512, 8)

    def body(pos_ref, f_ref, c_ref, s_ref):
        ang = pos_ref[...] * f_ref[...]
        pe = _lanes(PE_LO, PE_HI)
        c_ref[...] = jnp.where(pe, jnp.cos(ang), 1.0)
        s_ref[...] = jnp.where(pe, jnp.sin(ang), 0.0)

    tile = pl.BlockSpec((tm, LANES), lambda i: (i, 0))
    return _pc(body, name=name, grid=(T // tm,),
               in_specs=[pl.BlockSpec((tm, 1), lambda i: (i, 0)), pl.BlockSpec((1, LANES), lambda i: (0, 0))],
               out_specs=[tile, tile], out_shape=[SDS((T, LANES), F32)] * 2)(pos, invf)


V_ONE = MLA_V


def mla_prep_fwd(qt, kvt, proj, cs, sn, *, name):
    T = qt.shape[0]
    tm = _tile(T, 256, 8)
    HW = MLA_H * LANES

    def body(q_ref, k_ref, v_ref, kpe_ref, c_ref, s_ref, qo_ref, ko_ref, vo_ref):
        c, s = c_ref[...], s_ref[...]
        kpe = kpe_ref[...]
        sw = _swap_matrix()
        one = _lanes(V_ONE, V_ONE + 1)
        for h in range(MLA_H):
            sl = slice(h * LANES, (h + 1) * LANES)
            q = q_ref[:, sl]
            k = k_ref[:, sl] + kpe
            qo_ref[:, sl] = ((q * c + _split_nn(q, sw) * s) * ATT_SCALE).astype(BF16)
            ko_ref[:, sl] = (k * c + _split_nn(k, sw) * s).astype(BF16)
            vo_ref[:, sl] = jnp.where(one, 1.0, v_ref[:, sl]).astype(BF16)

    row = pl.BlockSpec((tm, HW), lambda i: (i, 0))
    tab = pl.BlockSpec((tm, LANES), lambda i: (i, 0))
    return _pc(body, name=name, grid=(T // tm,),
               in_specs=[row, row, pl.BlockSpec((tm, HW), lambda i: (i, 1)),
                         pl.BlockSpec((tm, LANES), lambda i: (i, KPE0 // LANES)), tab, tab],
               out_specs=[row, row, row], out_shape=[SDS((T, HW), BF16)] * 3)(qt, kvt, kvt, proj, cs, sn)


def mla_prep_bwd(dqr, dkr, cs, sn, *, name):
    T = dqr.shape[0]
    tm = _tile(T, 256, 8)
    HW = MLA_H * LANES

    def body(dq_ref, dk_ref, c_ref, s_ref, qo_ref, ko_ref, kpe_ref):
        c, s = c_ref[...], s_ref[...]
        sw = _swap_matrix()
        pe = _lanes(PE_LO, PE_HI)
        dkpe = jnp.zeros((tm, LANES), F32)
        for h in range(MLA_H):
            sl = slice(h * LANES, (h + 1) * LANES)
            dq = dq_ref[:, sl] * ATT_SCALE
            dk = dk_ref[:, sl]
            qo_ref[:, sl] = (dq * c - _split_nn(dq * s, sw)).astype(BF16)
            dkk = dk * c - _split_nn(dk * s, sw)
            ko_ref[:, sl] = jnp.where(pe, 0.0, dkk).astype(BF16)
            dkpe = dkpe + jnp.where(pe, dkk, 0.0)
        kpe_ref[...] = dkpe.astype(BF16)

    row = pl.BlockSpec((tm, HW), lambda i: (i, 0))
    tab = pl.BlockSpec((tm, LANES), lambda i: (i, 0))
    return _pc(body, name=name, grid=(T // tm,), in_specs=[row, row, tab, tab], out_specs=[row, row, tab],
               out_shape=[SDS((T, HW), BF16), SDS((T, HW), BF16), SDS((T, LANES), BF16)])(dqr, dkr, cs, sn)


def _att_tile(S):
    return _tile(S, 512, LANES)


def _rep(x, n):
    return x if n == 1 else jnp.concatenate([x] * n, axis=1)


def _diag_mask(t, transposed=False):
    r = lax.broadcasted_iota(jnp.int32, (t, t), 0)
    c = lax.broadcasted_iota(jnp.int32, (t, t), 1)
    return (c >= r) if transposed else (c <= r)


def flash_fwd(qr, kr, vr, Bl, *, comm=None, name):
    T = qr.shape[0]
    S = T // Bl
    t = _att_tile(S)
    n = S // t
    nl = t // LANES

    def body(q_ref, k_ref, v_ref, o_ref, lse_ref, lset_ref, m, acc):
        qi = pl.program_id(2)
        q = q_ref[...]
        m[...] = jnp.full_like(m, NEG)
        acc[...] = jnp.zeros_like(acc)

        def block(kj, masked):
            off = pl.multiple_of(kj * t, t)
            s = _nt(q, k_ref[pl.ds(off, t), :])
            if masked:
                s = jnp.where(_diag_mask(t), s, NEG)
            mo = m[...]
            mn = jnp.maximum(mo, jnp.max(s, axis=1, keepdims=True))
            p = jnp.exp((s - _rep(mn, nl)).astype(BF16))
            acc[...] = jnp.exp(mo - mn) * acc[...] + _nn(p, v_ref[pl.ds(off, t), :])
            m[...] = mn

        def loop(kj, c):
            block(kj, False)
            return c

        lax.fori_loop(0, qi, loop, 0)
        block(qi, True)
        a = acc[...]
        l = a[:, V_ONE:V_ONE + 1]
        o_ref[...] = jnp.where(_lanes(0, MLA_V), a / l, 0.0).astype(BF16)
        lse = m[...] + jnp.log(l)
        lse_ref[...] = lse
        lset_ref[...] = lse.T[0:8, :]

    qs = pl.BlockSpec((t, LANES), lambda b, h, qi: (b * n + qi, h))
    seq = pl.BlockSpec((S, LANES), lambda b, h, qi: (b, h))
    return _pc(body, name=name, grid=(Bl, MLA_H, n), in_specs=[qs, seq, seq],
               out_specs=[qs, qs, pl.BlockSpec((8, t), lambda b, h, qi: (b * MLA_H + h, qi))],
               out_shape=[SDS((T, MLA_H * LANES), BF16), SDS((T, MLA_H * LANES), F32), SDS((Bl * MLA_H * 8, S), F32)],
               scratch=[pltpu.VMEM((t, LANES), F32), pltpu.VMEM((t, LANES), F32)], comm=comm)(qr, kr, vr)


def flash_bwd_dq(qr, kr, vr, o, lse, dycat, Bl, *, comm=None, name):
    T = qr.shape[0]
    S = T // Bl
    t = _att_tile(S)
    n = S // t
    nl = t // LANES
    do0 = (SSD_W + POOL_W) // LANES

    def body(q_ref, k_ref, v_ref, o_ref, lse_ref, do_ref, dq_ref, dlt_ref, acc, dl):
        qi = pl.program_id(2)
        q = q_ref[...]
        do = do_ref[...]
        dob = do.astype(BF16)
        dl[...] = jnp.broadcast_to(_rsum(do * o_ref[...].astype(F32)), (t, LANES))
        acc[...] = jnp.zeros_like(acc)

        def block(kj, masked):
            off = pl.multiple_of(kj * t, t)
            k = k_ref[pl.ds(off, t), :]
            s = _nt(q, k)
            if masked:
                s = jnp.where(_diag_mask(t), s, NEG)
            p = jnp.exp((s - _rep(lse_ref[...], nl)).astype(BF16))
            dp = _nt(dob, v_ref[pl.ds(off, t), :])
            acc[...] += _nn(p * (dp - _rep(dl[...], nl)), k)

        def loop(kj, c):
            block(kj, False)
            return c

        lax.fori_loop(0, qi, loop, 0)
        block(qi, True)
        dq_ref[...] = acc[...]
        dlt_ref[...] = dl[...].T[0:8, :]

    qs = pl.BlockSpec((t, LANES), lambda b, h, qi: (b * n + qi, h))
    seq = pl.BlockSpec((S, LANES), lambda b, h, qi: (b, h))
    return _pc(body, name=name, grid=(Bl, MLA_H, n),
               in_specs=[qs, seq, seq, qs, qs, pl.BlockSpec((t, LANES), lambda b, h, qi: (b * n + qi, do0 + h))],
               out_specs=[qs, pl.BlockSpec((8, t), lambda b, h, qi: (b * MLA_H + h, qi))],
               out_shape=[SDS((T, MLA_H * LANES), F32), SDS((Bl * MLA_H * 8, S), F32)],
               scratch=[pltpu.VMEM((t, LANES), F32), pltpu.VMEM((t, LANES), F32)], comm=comm)(qr, kr, vr, o, lse, dycat)


def flash_bwd_dkv(qr, kr, vr, lset, dlt, dycat, Bl, *, comm=None, name):
    T = qr.shape[0]
    S = T // Bl
    t = _att_tile(S)
    n = S // t
    do0 = (SSD_W + POOL_W) // LANES

    def body(q_ref, k_ref, v_ref, lset_ref, dlt_ref, do_ref, dk_ref, dv_ref, dka, dva):
        kj = pl.program_id(2)
        k = k_ref[...]
        v = v_ref[...]
        dka[...] = jnp.zeros_like(dka)
        dva[...] = jnp.zeros_like(dva)

        def block(qi, masked):
            off = pl.multiple_of(qi * t, t)
            q = q_ref[pl.ds(off, t), :]
            do = do_ref[pl.ds(off, t), :].astype(BF16)
            st = _nt(k, q)
            if masked:
                st = jnp.where(_diag_mask(t, True), st, NEG)
            pt = jnp.exp((st - lset_ref[0:1, pl.ds(off, t)]).astype(BF16))
            dst = pt * (_nt(v, do) - dlt_ref[0:1, pl.ds(off, t)])
            dva[...] += _nn(pt, do)
            dka[...] += _nn(dst, q)

        def loop(qi, c):
            block(qi, False)
            return c

        block(kj, True)
        lax.fori_loop(kj + 1, n, loop, 0)
        dk_ref[...] = dka[...]
        dv_ref[...] = dva[...].astype(BF16)

    ks = pl.BlockSpec((t, LANES), lambda b, h, kj: (b * n + kj, h))
    seq = pl.BlockSpec((S, LANES), lambda b, h, kj: (b, h))
    rows = pl.BlockSpec((8, S), lambda b, h, kj: (b * MLA_H + h, 0))
    return _pc(body, name=name, grid=(Bl, MLA_H, n),
               in_specs=[seq, ks, ks, rows, rows, pl.BlockSpec((S, LANES), lambda b, h, kj: (b, do0 + h))],
               out_specs=[ks, ks], out_shape=[SDS((T, MLA_H * LANES), F32), SDS((T, MLA_H * LANES), BF16)],
               scratch=[pltpu.VMEM((t, LANES), F32), pltpu.VMEM((t, LANES), F32)], comm=comm)(qr, kr, vr, lset, dlt, dycat)


def _rows2d(a):
    return a.reshape(-1, a.shape[-1])


def _scalar(i):
    return jnp.reshape(i, (1,)).astype(jnp.int32)


def chip_sum(g8, from_sibling, *, name):
    blk = g8.shape[1:]
    R, C = math.prod(blk[:-1]), blk[-1]
    tm = _tile(R, 512, 16)

    def body(c_ref, a_ref, b_ref, o_ref, ob_ref):
        s = a_ref[0, 0] + b_ref[0]
        o_ref[0] = s
        ob_ref[0] = s.astype(BF16)

    row = pl.BlockSpec((1, tm, C), lambda k, i, c: (k, i, 0))
    spec = pltpu.PrefetchScalarGridSpec(
        num_scalar_prefetch=1, grid=(4, R // tm),
        in_specs=[pl.BlockSpec((1, 1, tm, C), lambda k, i, c: (k, c[0], i, 0)), row], out_specs=[row, row])
    o, ob = pl.pallas_call(body, name=name, grid_spec=spec, out_shape=[SDS((4, R, C), F32), SDS((4, R, C), BF16)],
                           compiler_params=pltpu.CompilerParams(vmem_limit_bytes=VMEM_LIMIT),
                           )(_scalar(lax.axis_index("c")), g8.reshape(4, 2, R, C), from_sibling.reshape(4, R, C))
    return o.reshape((4,) + blk), ob.reshape((4,) + blk)


def adamw_sharded(w, m, v, sums, recv, layer, prev, *, name):
    blk = w.shape[1:]
    R, C = math.prod(blk[:-1]), blk[-1]
    tm = _tile(R, 256, 16)
    bc1 = 1.0 - ADAM_B1 ** ADAM_STEP
    bc2 = 1.0 - ADAM_B2 ** ADAM_STEP
    n_prev = 0 if prev is None else 4

    def body(chip_ref, w_ref, m_ref, v_ref, s_ref, r_ref, *rest):
        g_ref, d_ref, nm_ref, nv_ref = rest[n_prev:]
        g = s_ref[0] + r_ref[0].astype(F32) + r_ref[1].astype(F32) + r_ref[2].astype(F32)
        mm_ = ADAM_B1 * m_ref[0] + (1.0 - ADAM_B1) * g
        vv = ADAM_B2 * v_ref[0] + (1.0 - ADAM_B2) * (g * g)
        g_ref[0] = g
        nm_ref[0] = mm_
        nv_ref[0] = vv
        d_ref[0] = -ADAM_LR * ((mm_ / bc1) / (jnp.sqrt(vv / bc2) + ADAM_EPS) + ADAM_WD * w_ref[0])

    lay = pl.BlockSpec((1, tm, C), lambda i, c: (layer, i, 0))
    spec = pltpu.PrefetchScalarGridSpec(
        num_scalar_prefetch=1, grid=(R // tm,),
        in_specs=[lay, lay, lay, pl.BlockSpec((1, tm, C), lambda i, c: (c[0], i, 0)),
                  pl.BlockSpec((3, tm, C), lambda i, c: (0, i, 0))] + [ANY] * n_prev,
        out_specs=[lay] * 4)
    xi, yi, _ = _place()
    d3 = (w.shape[0], R, C)
    outs = pl.pallas_call(
        body, name=name, grid_spec=spec, out_shape=[SDS(d3, F32)] * 4,
        input_output_aliases={6 + i: i for i in range(n_prev)},
        compiler_params=pltpu.CompilerParams(vmem_limit_bytes=VMEM_LIMIT),
    )(_scalar(2 * xi + yi), w.reshape(d3), m.reshape(d3), v.reshape(d3), sums.reshape(4, R, C), recv.reshape(3, R, C),
      *([] if prev is None else prev))
    return list(outs)


def adamw(w, m, v, parts, *, name):
    shp = w.shape
    w2, m2, v2 = _rows2d(w), _rows2d(m), _rows2d(v)
    R, C = w2.shape
    p3 = [p.reshape(p.shape[0], R, C) for p in parts]
    tm = _tile(R, 256, 8)
    bc1 = 1.0 - ADAM_B1 ** ADAM_STEP
    bc2 = 1.0 - ADAM_B2 ** ADAM_STEP

    def body(w_ref, m_ref, v_ref, *refs):
        p_refs, (g_ref, d_ref, nm_ref, nv_ref) = refs[:len(p3)], refs[len(p3):]
        g = None
        for p_ref, p in zip(p_refs, p3):
            for k in range(p.shape[0]):
                term = p_ref[k].astype(F32)
                g = term if g is None else g + term
        mm_ = ADAM_B1 * m_ref[...] + (1.0 - ADAM_B1) * g
        vv = ADAM_B2 * v_ref[...] + (1.0 - ADAM_B2) * (g * g)
        g_ref[...] = g
        nm_ref[...] = mm_
        nv_ref[...] = vv
        d_ref[...] = -ADAM_LR * ((mm_ / bc1) / (jnp.sqrt(vv / bc2) + ADAM_EPS) + ADAM_WD * w_ref[...])

    blk = pl.BlockSpec((tm, C), lambda i: (i, 0))
    pspecs = [pl.BlockSpec((p.shape[0], tm, C), lambda i: (0, i, 0)) for p in p3]
    outs = _pc(body, name=name, grid=(R // tm,), in_specs=[blk, blk, blk] + pspecs,
               out_specs=[blk] * 4, out_shape=[SDS((R, C), F32)] * 4)(w2, m2, v2, *p3)
    return [o.reshape(shp) for o in outs]


def _place():
    return lax.axis_index("x"), lax.axis_index("y"), lax.axis_index("c")


def all_gather_many(xs, *, name):
    n = len(xs)

    def body(*refs):
        x_refs, o_refs = refs[:n], refs[n:2 * n]
        send_sems, recv_sems, local_sems = refs[2 * n:]
        x, y, c = _place()
        me, sibling = (x, y, c), (x, y, 1 - c)
        chips = [(1 - x, y), (x, 1 - y), (1 - x, 1 - y)]

        def rows(a, p):
            return o_refs[a].at[4 * p[0] + 2 * p[1] + p[2]]

        def copy(a, k, block, to, src=None):
            return pltpu.make_async_remote_copy(
                src_ref=rows(a, block) if src is None else src, dst_ref=rows(a, block),
                send_sem=send_sems.at[7 * a + k], recv_sem=recv_sems.at[7 * a + k], device_id=to, device_id_type=MESH)

        mine = [pltpu.make_async_copy(x_refs[a], rows(a, me), local_sems.at[a]) for a in range(n)]
        for cp in mine:
            cp.start()
        first = []
        for a in range(n):
            first.append(copy(a, 0, me, sibling, src=x_refs[a]))
            first += [copy(a, 1 + j, me, (*chip, c), src=x_refs[a]) for j, chip in enumerate(chips)]
        for cp in first:
            cp.start()
        passed = []
        for j, chip in enumerate(chips):
            for a in range(n):
                copy(a, 1 + j, (*chip, c), me).wait_recv()
                cp = copy(a, 4 + j, (*chip, c), sibling)
                cp.start()
                passed.append(cp)
        for a in range(n):
            copy(a, 0, sibling, me).wait_recv()
            for j, chip in enumerate(chips):
                copy(a, 4 + j, (*chip, 1 - c), me).wait_recv()
        for cp in first + passed:
            cp.wait_send()
        for cp in mine:
            cp.wait()

    return pl.pallas_call(
        body, name=name, in_specs=[ANY] * n, out_specs=[ANY] * n,
        out_shape=[SDS((N_DEV,) + a.shape, a.dtype) for a in xs],
        scratch_shapes=[pltpu.SemaphoreType.DMA((7 * n,)), pltpu.SemaphoreType.DMA((7 * n,)), pltpu.SemaphoreType.DMA((n,))],
    )(*xs)


def _stage(ins, out_shape, n_peers, copy_of, n_arrays=None, local_of=None):
    ins = list(ins)
    n = len(ins) if n_arrays is None else n_arrays

    def copies(in_refs, out_refs, send_sems, recv_sems):
        place = _place()
        out = []
        for a in range(n):
            for k in range(n_peers):
                src, dst, peer = copy_of(in_refs[a], out_refs[a], k, place)
                out.append(pltpu.make_async_remote_copy(
                    src_ref=src, dst_ref=dst, send_sem=send_sems.at[n_peers * a + k], recv_sem=recv_sems.at[n_peers * a + k],
                    device_id=peer, device_id_type=MESH))
        if local_of is not None:
            for i, (src, dst) in enumerate(local_of(in_refs, out_refs, place)):
                out.append(pltpu.make_async_copy(src, dst, send_sems.at[n_peers * n + i]))
        return out

    return dict(ins=ins, out_shape=list(out_shape), sems=n_peers * n + (n if local_of is not None else 0), copies=copies)


def _other_chips(x, y):
    return [(1 - x, y), (x, 1 - y), (1 - x, 1 - y)]


def stage_gather_direct(blocks):
    def copy_of(src, dst, k, place):
        x, y, c = place
        peer = (x, y, 1 - c) if k == 0 else (*_other_chips(x, y)[k - 1], c)
        return src, dst.at[4 * x + 2 * y + c], peer

    return _stage(blocks, [SDS((N_DEV,) + b.shape, b.dtype) for b in blocks], 4, copy_of)


def stage_gather_forward(bufs, own):
    n = len(bufs)

    def copy_of(src, dst, k, place):
        x, y, c = place
        cx, cy = _other_chips(x, y)[k]
        slot = 4 * cx + 2 * cy + c
        return src.at[slot], dst.at[slot], (x, y, 1 - c)

    def local_of(in_refs, out_refs, place):
        x, y, c = place
        return [(in_refs[n + a], out_refs[a].at[4 * x + 2 * y + c]) for a in range(n)]

    st = _stage(list(bufs) + list(own), [SDS(b.shape, b.dtype) for b in bufs], 3, copy_of, n_arrays=n, local_of=local_of)
    st["alias"] = n
    return st


def stage_rs_sibling(g8s):
    def copy_of(src, dst, k, place):
        x, y, c = place
        return src.at[2 * k + (1 - c)], dst.at[k], (x, y, 1 - c)

    return _stage(g8s, [SDS((4,) + g.shape[1:], g.dtype) for g in g8s], 4, copy_of)


def stage_rs_chips(sums):
    def copy_of(src, dst, k, place):
        x, y, c = place
        chip = _other_chips(x, y)[k]
        return src.at[2 * chip[0] + chip[1]], dst.at[k], (*chip, c)

    return _stage(sums, [SDS((3,) + s.shape[1:], s.dtype) for s in sums], 3, copy_of)


def run_stage(stage, *, name):
    n_in, n_out = len(stage["ins"]), len(stage["out_shape"])

    def body(*refs):
        cps = stage["copies"](refs[:n_in], refs[n_in:n_in + n_out], refs[-2], refs[-1])
        for cp in cps:
            cp.start()
        for cp in cps:
            cp.wait()

    return pl.pallas_call(
        body, name=name, in_specs=[ANY] * n_in, out_specs=[ANY] * n_out, out_shape=stage["out_shape"],
        scratch_shapes=[pltpu.SemaphoreType.DMA((stage["sems"],)), pltpu.SemaphoreType.DMA((stage["sems"],))],
    )(*stage["ins"])


def _owner_major(full, axis):
    shp = full.shape
    r = full.reshape(shp[:axis] + (N_DEV, shp[axis] // N_DEV) + shp[axis + 1:])
    return jnp.moveaxis(r, axis, 0)


def _from_owner_major(g8, axis):
    r = jnp.moveaxis(g8, 0, axis)
    shp = r.shape
    return r.reshape(shp[:axis] + (shp[axis] * shp[axis + 1],) + shp[axis + 2:])


def _perm_w_in(w):
    z = jnp.zeros((w.shape[0], LANES), w.dtype)
    dt = jnp.pad(w[:, 2560:2576], ((0, 0), (0, LANES - SSD_HEADS)))
    kpe = jnp.pad(w[:, 3728:3760], ((0, 0), (PE_LO, LANES - PE_HI)))
    return jnp.concatenate([w[:, 0:1024], w[:, 1024:2560], w[:, 2576:3088], w[:, 3088:3472], z, w[:, 3472:3728], dt, kpe], axis=1)


def _unperm_w_in(g):
    return jnp.concatenate([g[:, Z0:Z0 + 1024], g[:, XBC0:XBC0 + 1536], g[:, DT0:DT0 + SSD_HEADS], g[:, U0:U0 + 512],
                            g[:, CQ0:CQ0 + 384], g[:, CKV0:CKV0 + 256], g[:, KPE0 + PE_LO:KPE0 + PE_HI]], axis=1)


def _perm_w_uq(w):
    return jnp.pad(w.reshape(MLA_QR, MLA_H, MLA_QK), ((0, 0), (0, 0), (0, LANES - MLA_QK))).reshape(MLA_QR, MLA_H * LANES)


def _unperm_w_uq(g):
    return g.reshape(MLA_QR, MLA_H, LANES)[:, :, :MLA_QK].reshape(MLA_QR, MLA_H * MLA_QK)


def _perm_w_ukv(w):
    w3 = w.reshape(MLA_KVR, MLA_H, MLA_NOPE + MLA_V)
    pad = ((0, 0), (0, 0), (0, LANES - MLA_NOPE))
    k = jnp.pad(w3[:, :, :MLA_NOPE], pad).reshape(MLA_KVR, MLA_H * LANES)
    v = jnp.pad(w3[:, :, MLA_NOPE:], pad).reshape(MLA_KVR, MLA_H * LANES)
    return jnp.concatenate([k, v], axis=1)


def _unperm_w_ukv(g):
    k = g[:, :MLA_H * LANES].reshape(MLA_KVR, MLA_H, LANES)[:, :, :MLA_NOPE]
    v = g[:, MLA_H * LANES:].reshape(MLA_KVR, MLA_H, LANES)[:, :, :MLA_V]
    return jnp.concatenate([k, v], axis=2).reshape(MLA_KVR, MLA_H * (MLA_NOPE + MLA_V))


def _perm_w_out(w):
    m = jnp.pad(w[SSD_W + POOL_W:].reshape(MLA_H, MLA_V, D), ((0, 0), (0, LANES - MLA_V), (0, 0))).reshape(MLA_H * LANES, D)
    return jnp.concatenate([w[:SSD_W + POOL_W], m], axis=0)


def _unperm_w_out(g):
    m = g[SSD_W + POOL_W:].reshape(MLA_H, LANES, D)[:, :MLA_V].reshape(MLA_H * MLA_V, D)
    return jnp.concatenate([g[:SSD_W + POOL_W], m], axis=0)


def _lane_pad(v):
    return jnp.pad(v.reshape(1, -1), ((0, 0), (0, LANES - v.shape[-1])))


SMALL = ("attn_norm", "ssd_conv_b", "ssd_dt_bias", "ssd_a_log", "ssd_d", "ssd_norm", "pool_w", "pool_scale",
         "mla_q_norm", "mla_kv_norm", "ffn_norm", "ffn_conv_b", "final_norm")
SHARDED = {"w_in": 2, "ssd_conv_w": 2, "mla_w_uq": 2, "mla_w_ukv": 2, "w_out": 1, "ffn_w_up": 2, "ffn_conv_w": 2,
           "ffn_w_down": 1}
ALL_W = ("attn_norm", "w_in", "ssd_conv_w", "ssd_conv_b", "ssd_dt_bias", "ssd_a_log", "ssd_d", "ssd_norm", "pool_w",
         "pool_scale", "mla_q_norm", "mla_w_uq", "mla_kv_norm", "mla_w_ukv", "w_out", "ffn_norm", "ffn_w_up",
         "ffn_conv_w", "ffn_conv_b", "ffn_w_down", "final_norm")


def _pack_small(d):
    rows, layout = [], []
    for k in SMALL:
        a = d[k].reshape(-1)
        n = a.shape[0]
        r = -(-n // LANES)
        rows.append(jnp.pad(a, (0, r * LANES - n)).reshape(r, LANES))
        layout.append((k, n, r, d[k].shape))
    buf = jnp.concatenate(rows, axis=0)
    pad = (-buf.shape[0]) % 8
    return jnp.pad(buf, ((0, pad), (0, 0))), layout


def _unpack_small(buf, layout):
    out, r0 = {}, 0
    for k, n, r, shp in layout:
        out[k] = buf[r0:r0 + r].reshape(-1)[:n].reshape(shp)
        r0 += r
    return out


_PERM = {"w_in": _perm_w_in, "mla_w_uq": _perm_w_uq, "mla_w_ukv": _perm_w_ukv, "w_out": _perm_w_out}
FIRST = ("w_in", "ssd_conv_w")
REST = tuple(k for k in SHARDED if k not in FIRST)


def _sharded_entries(keys, gathered):
    return {k: _PERM.get(k, lambda t: t)(_from_owner_major(g8, SHARDED[k] - 1)) for k, g8 in zip(keys, gathered)}


def _layer_fwd(l, x, W, cs, sn, Bl, next_blocks=None, pending=None):
    T = x.shape[0]
    S = T // Bl
    n = f"l{l}_"
    h = rms_fwd(x, W["attn_norm"], name=n + "attn_norm")
    own_direct = stage_gather_direct(pending) if pending is not None else None
    proj = mm(h, W["w_in"], comm=own_direct, name=n + "w_in")
    proj3 = proj.reshape(Bl, S, PW)
    xbc3 = conv_silu_fwd(proj3, W["ssd_conv_w"], W["ssd_conv_b"], name=n + "ssd_conv")
    own_forward = stage_gather_forward(own_direct["result"], pending) if own_direct else None
    y3, hin = ssd_fwd(xbc3, proj3, W["ssd_dt_bias"], W["ssd_a_log"], W["ssd_d"], comm=own_forward, name=n + "ssd_scan")
    if own_direct:
        W = {**W, **_sharded_entries(REST, own_forward["result"])}
    y = y3.reshape(T, SSD_W)
    y_ssd = gated_rms_fwd(y, proj, W["ssd_norm"], name=n + "ssd_gate_norm")
    y_pool = pool_fwd(proj3, W["pool_w"], W["pool_scale"], name=n + "pool").reshape(T, POOL_W)
    qn = rms_fwd(proj, W["mla_q_norm"], col0=CQ0, width=MLA_QR, name=n + "q_norm")
    kvn = rms_fwd(proj, W["mla_kv_norm"], col0=CKV0, width=MLA_KVR, name=n + "kv_norm")
    qt = mm(qn, W["mla_w_uq"], name=n + "w_uq")
    kvt = mm(kvn, W["mla_w_ukv"], name=n + "w_ukv")
    qr, kr, vr = mla_prep_fwd(qt, kvt, proj, cs, sn, name=n + "rope")
    direct = stage_gather_direct(next_blocks) if next_blocks is not None else None
    o, lse, lset = flash_fwd(qr, kr, vr, Bl, comm=direct, name=n + "attn")
    ycat = jnp.concatenate([y_ssd, y_pool, o], axis=1)
    x1 = mm(ycat, W["w_out"], add=x, name=n + "w_out")
    h2 = rms_fwd(x1, W["ffn_norm"], name=n + "ffn_norm")
    forward = stage_gather_forward(direct["result"], next_blocks) if direct else None
    pre = mm(h2, W["ffn_w_up"], comm=forward, name=n + "w_up")
    gathered = forward["result"] if direct else None
    pre3 = pre.reshape(Bl, S, 2 * DFF)
    act = ffn_act_fwd(pre3, W["ffn_conv_w"], W["ffn_conv_b"], name=n + "ffn_act").reshape(T, DFF)
    x2 = mm(act, W["ffn_w_down"], add=x1, name=n + "w_down")
    saved = dict(x=x, h=h, proj=proj, xbc3=xbc3, hin=hin, y=y, qn=qn, kvn=kvn, vr=vr, qr=qr, kr=kr, o=o, lse=lse, lset=lset,
                 ycat=ycat, x1=x1, h2=h2, pre3=pre3, act=act)
    return x2, saved, gathered, W


EARLY = ("ffn_w_up", "ffn_conv_w", "ffn_w_down", "w_out")
LATE = tuple(k for k in SHARDED if k not in EARLY)
_UNPERM = {"w_in": _unperm_w_in, "mla_w_uq": _unperm_w_uq, "mla_w_ukv": _unperm_w_ukv, "w_out": _unperm_w_out}


def _by_owner(g, keys):
    return [_owner_major(_UNPERM.get(k, lambda t: t)(g[k]), SHARDED[k] - 1) for k in keys]


def _chip_sums(g8s, from_sibling, tag):
    return [chip_sum(g8, r, name=f"{tag}{a}") for a, (g8, r) in enumerate(zip(g8s, from_sibling))]


def _layer_bwd(l, dx2, dx2b, W, sv, cs, sn, Bl, later_g8=None):
    T = dx2.shape[0]
    S = T // Bl
    n = f"l{l}_b_"
    g = {}
    g["ffn_w_down"] = mm(sv["act"], dx2b, ta=True, name=n + "dw_down")
    dact = mm(dx2b, W["ffn_w_down"], tb=True, name=n + "dact")
    to_sibling = stage_rs_sibling(later_g8) if later_g8 is not None else None
    dpg, dpv, dwg, dwv, dbg, dbv = ffn_act_bwd(sv["pre3"], W["ffn_conv_w"], W["ffn_conv_b"], dact.reshape(Bl, S, DFF),
                                               comm=to_sibling, name=n + "ffn_act")
    to_chips = sums = None
    if to_sibling:
        sums = _chip_sums(later_g8, to_sibling["result"], n + "rs_late_add")
        to_chips = stage_rs_chips([sb for _, sb in sums])
    g["ffn_conv_w"] = jnp.concatenate([dwg, dwv], axis=1)
    g["ffn_conv_b"] = jnp.concatenate([dbg, dbv], axis=1)
    dpg, dpv = dpg.reshape(T, DFF), dpv.reshape(T, DFF)
    g["ffn_w_up"] = jnp.concatenate([mm(sv["h2"], dpg, ta=True, name=n + "dw_up_g"),
                                     mm(sv["h2"], dpv, ta=True, name=n + "dw_up_v")], axis=1)
    dh2 = mm(dpg, W["ffn_w_up"], tb=True, name=n + "dh2_g")
    dh2 = mm(dpv, W["ffn_w_up"], tb=True, b_k0=DFF, add=dh2, name=n + "dh2_v")
    dx1, dx1b, g["ffn_norm"] = rms_bwd(sv["x1"], W["ffn_norm"], dh2, add=dx2, name=n + "ffn_norm")
    g["w_out"] = mm(sv["ycat"], dx1b, ta=True, name=n + "dw_out")
    dycat = mm(dx1b, W["w_out"], tb=True, out_dtype=BF16, name=n + "dycat")
    proj, proj3 = sv["proj"], sv["proj"].reshape(Bl, S, PW)
    dy, dz, g["ssd_norm"] = gated_rms_bwd(sv["y"], proj, W["ssd_norm"], dycat, name=n + "ssd_gate_norm")
    dxa, ddt, dpar, dd = ssd_bwd(sv["xbc3"], proj3, sv["hin"], dy.reshape(Bl, S, SSD_W), W["ssd_dt_bias"], W["ssd_a_log"],
                                 W["ssd_d"], comm=to_chips, name=n + "ssd_scan")
    reduced_late = ([s32 for s32, _ in sums], to_chips["result"]) if to_chips else None
    g["ssd_dt_bias"] = dpar[0, :SSD_HEADS]
    g["ssd_a_log"] = dpar[1, :SSD_HEADS]
    g["ssd_d"] = dd[:, :2].reshape(SSD_HEADS)
    dxbc, g["ssd_conv_w"], g["ssd_conv_b"] = conv_silu_bwd(proj3, W["ssd_conv_w"], W["ssd_conv_b"], dxa, name=n + "ssd_conv")
    du, g["pool_w"], g["pool_scale"] = pool_bwd(proj3, W["pool_w"], W["pool_scale"], dycat.reshape(Bl, S, YCAT), name=n + "pool")
    early_g8 = _by_owner(g, EARLY)
    early_sibling = stage_rs_sibling(early_g8)
    dqr, dlt = flash_bwd_dq(sv["qr"], sv["kr"], sv["vr"], sv["o"], sv["lse"], dycat, Bl, comm=early_sibling, name=n + "attn_dq")
    early_sums = _chip_sums(early_g8, early_sibling["result"], n + "rs_early_add")
    early_chips = stage_rs_chips([sb for _, sb in early_sums])
    dkr, dv = flash_bwd_dkv(sv["qr"], sv["kr"], sv["vr"], sv["lset"], dlt, dycat, Bl, comm=early_chips, name=n + "attn_dkv")
    reduced_early = ([s32 for s32, _ in early_sums], early_chips["result"])
    dqt, dkt, dkpe = mla_prep_bwd(dqr, dkr, cs, sn, name=n + "rope")
    g["mla_w_ukv"] = jnp.concatenate([mm(sv["kvn"], dkt, ta=True, name=n + "dw_uk"),
                                      mm(sv["kvn"], dv, ta=True, name=n + "dw_uv")], axis=1)
    dkvn = mm(dkt, W["mla_w_ukv"], tb=True, name=n + "dkvn_k")
    dkvn = mm(dv, W["mla_w_ukv"], tb=True, b_k0=MLA_H * LANES, add=dkvn, name=n + "dkvn_v")
    g["mla_w_uq"] = mm(sv["qn"], dqt, ta=True, name=n + "dw_uq")
    dqn = mm(dqt, W["mla_w_uq"], tb=True, name=n + "dqn")
    dcq, g["mla_q_norm"] = rms_bwd(proj, W["mla_q_norm"], dqn, col0=CQ0, width=MLA_QR, name=n + "q_norm")
    dckv, g["mla_kv_norm"] = rms_bwd(proj, W["mla_kv_norm"], dkvn, col0=CKV0, width=MLA_KVR, name=n + "kv_norm")
    dproj = jnp.concatenate([dz, dxbc.reshape(T, SSD_XBC), du.reshape(T, POOL_W), dcq, jnp.zeros((T, LANES), BF16), dckv,
                             ddt.reshape(T, LANES), dkpe], axis=1)
    g["w_in"] = mm(sv["h"], dproj, ta=True, name=n + "dw_in")
    dh = mm(dproj, W["w_in"], tb=True, name=n + "dh")
    dx, dxb, g["attn_norm"] = rms_bwd(sv["x"], W["attn_norm"], dh, add=dx1, name=n + "attn_norm")
    return dx, dxb, g, reduced_late, reduced_early


def kernel(x, positions, attn_norm, w_in, ssd_conv_w, ssd_conv_b, ssd_dt_bias, ssd_a_log, ssd_d, ssd_norm, pool_w, pool_scale, mla_q_norm, mla_w_uq, mla_kv_norm, mla_w_ukv, w_out, ffn_norm, ffn_w_up, ffn_conv_w, ffn_conv_b, ffn_w_down, final_norm, loss_target, m_attn_norm, m_w_in, m_ssd_conv_w, m_ssd_conv_b, m_ssd_dt_bias, m_ssd_a_log, m_ssd_d, m_ssd_norm, m_pool_w, m_pool_scale, m_mla_q_norm, m_mla_w_uq, m_mla_kv_norm, m_mla_w_ukv, m_w_out, m_ffn_norm, m_ffn_w_up, m_ffn_conv_w, m_ffn_conv_b, m_ffn_w_down, m_final_norm, v_attn_norm, v_w_in, v_ssd_conv_w, v_ssd_conv_b, v_ssd_dt_bias, v_ssd_a_log, v_ssd_d, v_ssd_norm, v_pool_w, v_pool_scale, v_mla_q_norm, v_mla_w_uq, v_mla_kv_norm, v_mla_w_ukv, v_w_out, v_ffn_norm, v_ffn_w_up, v_ffn_conv_w, v_ffn_conv_b, v_ffn_w_down, v_final_norm):
    a = locals()
    Wt = {k: a[k] for k in ALL_W}
    Mo = {k: a["m_" + k] for k in ALL_W}
    Vo = {k: a["v_" + k] for k in ALL_W}
    Bl, S, _ = x.shape
    T = Bl * S

    names = list(SHARDED)
    conv = ("ssd_conv_w", "ffn_conv_w")

    def blocks_of(l, keys=names):
        return [Wt[k][l] if k in conv else Wt[k][l].astype(BF16) for k in keys]

    def replicated(l):
        return {
            "attn_norm": attn_norm[l].reshape(1, D), "ssd_conv_b": ssd_conv_b[l].reshape(1, SSD_XBC),
            "ssd_dt_bias": _lane_pad(ssd_dt_bias[l]), "ssd_a_log": _lane_pad(ssd_a_log[l]),
            "ssd_d": jnp.repeat(ssd_d[l].reshape(NPAIR, 2), SSD_P, axis=1), "ssd_norm": ssd_norm[l].reshape(1, SSD_W),
            "pool_w": pool_w[l].astype(BF16), "pool_scale": pool_scale[l].reshape(1, POOL_W),
            "mla_q_norm": mla_q_norm[l].reshape(1, MLA_QR), "mla_kv_norm": mla_kv_norm[l].reshape(1, MLA_KVR),
            "ffn_norm": ffn_norm[l].reshape(1, D), "ffn_conv_b": ffn_conv_b[l].reshape(1, 2 * DFF)}

    pos = positions.astype(F32).reshape(T, 1)
    inv_freq = ROPE_THETA ** (-jnp.arange(0, MLA_ROPE, 2, dtype=F32) / MLA_ROPE)
    invf = jnp.pad(jnp.concatenate([inv_freq, inv_freq]), (PE_LO, LANES - PE_HI)).reshape(1, LANES)
    cs, sn = rope_tables(pos, invf, name="rope_tables")

    first = all_gather_many(blocks_of(0, FIRST), name="gather_weights_l0")
    layers = [{**replicated(0), **_sharded_entries(FIRST, first)}]
    xc = x.reshape(T, D)
    saved = []
    for l in range(DEPTH):
        xc, sv, gathered, layers[l] = _layer_fwd(l, xc, layers[l], cs, sn, Bl,
                                                 next_blocks=blocks_of(l + 1) if l + 1 < DEPTH else None,
                                                 pending=blocks_of(0, REST) if l == 0 else None)
        saved.append(sv)
        if gathered is not None:
            layers.append({**replicated(l + 1), **_sharded_entries(names, gathered)})
    dx, dxb, g_final, loss_part = final_loss(xc, final_norm.reshape(1, D), loss_target.reshape(T, D), name="final_loss")

    grads = [None] * DEPTH
    reduced = {}

    def record(l, keys, red):
        for a, k in enumerate(keys):
            reduced[(l, k)] = (red[0][a], red[1][a])

    later_g8 = None
    for l in reversed(range(DEPTH)):
        dx, dxb, grads[l], red_late, red_early = _layer_bwd(l, dx, dxb, layers[l], saved[l], cs, sn, Bl, later_g8=later_g8)
        if red_late is not None:
            record(l + 1, LATE, red_late)
        record(l, EARLY, red_early)
        later_g8 = _by_owner(grads[l], LATE)
    loss = lax.psum(loss_part[0, 0], AXES)
    sums = _chip_sums(later_g8, run_stage(stage_rs_sibling(later_g8), name="rs_sibling_l0"), "rs_add_l0_")
    record(0, LATE, ([s32 for s32, _ in sums], run_stage(stage_rs_chips([sb for _, sb in sums]), name="rs_chips_l0")))

    out_g, out_d, out_m, out_v = {}, {}, {}, {}
    for k in names:
        outs = None
        for l in reversed(range(DEPTH)):
            s32, recv = reduced[(l, k)]
            outs = adamw_sharded(Wt[k], Mo[k], Vo[k], s32, recv, l, outs, name=f"adamw_l{l}_{k}")
        out_g[k], out_d[k], out_m[k], out_v[k] = (o.reshape(Wt[k].shape) for o in outs)

    part = {k: g_final.reshape(D) if k == "final_norm" else
            jnp.stack([grads[l][k].reshape(Wt[k].shape[1:]) for l in range(DEPTH)]) for k in SMALL}
    pg, layout = _pack_small(part)
    pw, _ = _pack_small(Wt)
    pm, _ = _pack_small(Mo)
    pv, _ = _pack_small(Vo)
    (pg8,) = all_gather_many([pg], name="gather_small_grads")
    sg, sd, sm, sv_ = adamw(pw, pm, pv, [pg8], name="adamw_small")
    for dst, buf in ((out_g, sg), (out_d, sd), (out_m, sm), (out_v, sv_)):
        dst.update(_unpack_small(buf, layout))

    return (loss, dx.reshape(Bl, S, D), *[out_g[k] for k in ALL_W], *[out_d[k] for k in ALL_W],
            *[out_m[k] for k in ALL_W], *[out_v[k] for k in ALL_W])
```

```python
import functools
import math

import jax
import jax.numpy as jnp
from jax import lax
from jax.experimental import pallas as pl
from jax.experimental.pallas import tpu as pltpu

F32, BF16 = jnp.float32, jnp.bfloat16
SDS = jax.ShapeDtypeStruct
MESH = pl.DeviceIdType.MESH
AXES = ("x", "y", "c")
N_DEV = 8

D = 1024
EPS = 1e-6
SSD_HEADS, SSD_P, SSD_W, SSD_G, SSD_N, SSD_K, SSD_L, SSD_XBC = 16, 64, 1024, 2, 128, 4, 128, 1536
POOL_G, POOL_D, POOL_W, POOL_WIN = 4, 128, 512, (2, 4, 8, 16)
MLA_H, MLA_QR, MLA_KVR, MLA_NOPE, MLA_ROPE, MLA_V, MLA_QK = 8, 384, 256, 64, 32, 64, 96
ROPE_THETA = 10000.0
MIX = 2048
DFF, FFN_K = 2816, 3
DEPTH = 2
ADAM_LR, ADAM_B1, ADAM_B2, ADAM_EPS, ADAM_WD, ADAM_STEP = 0.001, 0.9, 0.999, 1e-08, 0.01, 10

Z0, XBC0, U0, CQ0, CKV0, DT0, KPE0, PW = 0, 1024, 2560, 3072, 3584, 3840, 3968, 4096
LANES = 128
YCAT = SSD_W + POOL_W + MLA_H * LANES
NEG = -1e30
VMEM_LIMIT = 56 * 1024 * 1024
MM_ROW_TILE, MM_LANE_TILE, MM_FULL_K = 1024, 1408, 2816


def _tile(n, pref, mult):
    if n <= pref:
        return n
    for d in range(pref, 0, -mult):
        if d % mult == 0 and n % d == 0:
            return d
    return n


def _dg(a, b, ca, cb, prec=None):
    return lax.dot_general(a, b, (((ca,), (cb,)), ((), ())), preferred_element_type=F32, precision=prec)


def _nn(a, b):
    return _dg(a.astype(BF16), b.astype(BF16), 1, 0)


def _nt(a, b):
    return _dg(a.astype(BF16), b.astype(BF16), 1, 1)


def _tn(a, b):
    return _dg(a.astype(BF16), b.astype(BF16), 0, 0)


def _sig(x):
    return jax.nn.sigmoid(x)


def _silu(x):
    return x * _sig(x)


def _dsilu(x):
    s = _sig(x)
    return s * (1.0 + x * (1.0 - s))


ANY = pl.BlockSpec(memory_space=pl.ANY)


def _pc(body, *, name, grid, in_specs, out_specs, out_shape, scratch=(), comm=None):
    params = pltpu.CompilerParams(vmem_limit_bytes=VMEM_LIMIT)
    if comm is None:
        return pl.pallas_call(body, name=name, grid=grid, in_specs=in_specs, out_specs=out_specs, out_shape=out_shape,
                              scratch_shapes=list(scratch), compiler_params=params)
    single = not isinstance(out_shape, (list, tuple))
    o_specs = [out_specs] if single else list(out_specs)
    o_shape = [out_shape] if single else list(out_shape)
    ni, no, ns = len(in_specs), len(o_specs), len(scratch)
    nci, nco = len(comm["ins"]), len(comm["out_shape"])

    def fused(*refs):
        ins, cins = refs[:ni], refs[ni:ni + nci]
        outs, couts = refs[ni + nci:ni + nci + no], refs[ni + nci + no:ni + nci + no + nco]
        scr = refs[ni + nci + no + nco:ni + nci + no + nco + ns]
        send_sems, recv_sems = refs[-2:]
        copies = comm["copies"](cins, couts, send_sems, recv_sems)
        first = functools.reduce(jnp.logical_and, [pl.program_id(d) == 0 for d in range(len(grid))])
        last = functools.reduce(jnp.logical_and, [pl.program_id(d) == grid[d] - 1 for d in range(len(grid))])

        @pl.when(first)
        def _():
            for cp in copies:
                cp.start()

        body(*ins, *outs, *scr)

        @pl.when(last)
        def _():
            for cp in copies:
                cp.wait()

    call = pl.pallas_call(
        fused, name=name, grid=grid, in_specs=list(in_specs) + [ANY] * nci, out_specs=o_specs + [ANY] * nco,
        out_shape=o_shape + list(comm["out_shape"]),
        scratch_shapes=list(scratch) + [pltpu.SemaphoreType.DMA((comm["sems"],)), pltpu.SemaphoreType.DMA((comm["sems"],))],
        input_output_aliases={ni + a: no + a for a in range(comm.get("alias", 0))},
        compiler_params=params)

    def run(*args):
        res = call(*args, *comm["ins"])
        comm["result"] = list(res[no:])
        return res[0] if single else list(res[:no])

    return run


def _rsum(x):
    return jnp.sum(x, axis=1, keepdims=True)


def _csum(x):
    return jnp.sum(x, axis=0, keepdims=True)


def mm(a, b, *, ta=False, tb=False, add=None, out_dtype=F32, b_k0=0, comm=None, name):
    M, K = (a.shape[1], a.shape[0]) if ta else a.shape
    N = b.shape[0] if tb else b.shape[1]
    assert tb or b_k0 == 0
    tm = _tile(M, MM_LANE_TILE, LANES) if ta else _tile(M, MM_ROW_TILE, 8)
    tn = _tile(N, MM_LANE_TILE, LANES)
    if ta:
        tk = _tile(K, MM_ROW_TILE, 8)
    else:
        tk = K if K <= MM_FULL_K else _tile(K, 2048, LANES)
    nk = K // tk

    def body(*refs):
        if add is None:
            a_ref, b_ref, o_ref = refs[:3]
        else:
            a_ref, b_ref, add_ref, o_ref = refs[:4]
        part = _dg(a_ref[...].astype(BF16), b_ref[...].astype(BF16), 0 if ta else 1, 1 if tb else 0)

        def finish(r):
            if add is not None:
                r = r + add_ref[...].astype(F32)
            o_ref[...] = r.astype(out_dtype)

        if nk == 1:
            finish(part)
            return
        acc = refs[-1]
        k = pl.program_id(2)

        @pl.when(k == 0)
        def _():
            acc[...] = part

        @pl.when(k > 0)
        def _():
            acc[...] += part

        @pl.when(k == nk - 1)
        def _():
            finish(acc[...])

    a_spec = pl.BlockSpec((tk, tm), lambda i, j, k: (k, i)) if ta else pl.BlockSpec((tm, tk), lambda i, j, k: (i, k))
    assert b_k0 % tk == 0
    kb0 = b_k0 // tk
    b_spec = pl.BlockSpec((tn, tk), lambda i, j, k: (j, kb0 + k)) if tb else pl.BlockSpec((tk, tn), lambda i, j, k: (k, j))
    o_spec = pl.BlockSpec((tm, tn), lambda i, j, k: (i, j))
    ins, specs = [a, b], [a_spec, b_spec]
    if add is not None:
        ins.append(add)
        specs.append(o_spec)
    return _pc(body, name=name, grid=(M // tm, N // tn, nk), in_specs=specs, out_specs=o_spec,
               out_shape=SDS((M, N), out_dtype), scratch=[pltpu.VMEM((tm, tn), F32)] if nk > 1 else [], comm=comm)(*ins)


def rms_fwd(x, g, *, col0=0, width=None, name):
    T = x.shape[0]
    W = width or x.shape[1]
    tm = _tile(T, 512, 8)

    def body(x_ref, g_ref, o_ref):
        v = x_ref[...]
        r = lax.rsqrt(jnp.mean(v * v, axis=1, keepdims=True) + EPS)
        o_ref[...] = ((v * r) * g_ref[...]).astype(BF16)

    return _pc(body, name=name, grid=(T // tm,),
               in_specs=[pl.BlockSpec((tm, W), lambda i: (i, col0 // W)), pl.BlockSpec((1, W), lambda i: (0, 0))],
               out_specs=pl.BlockSpec((tm, W), lambda i: (i, 0)), out_shape=SDS((T, W), BF16))(x, g)


def rms_bwd(x, g, dh, *, col0=0, width=None, add=None, name):
    T = x.shape[0]
    W = width or x.shape[1]
    tm = _tile(T, 512, 8)

    def body(*refs):
        if add is None:
            x_ref, g_ref, dh_ref, dx_ref, dg_ref = refs
        else:
            x_ref, g_ref, dh_ref, add_ref, dx_ref, dxb_ref, dg_ref = refs
        v = x_ref[...]
        r = lax.rsqrt(jnp.mean(v * v, axis=1, keepdims=True) + EPS)
        xh = v * r
        d = dh_ref[...].astype(F32)
        dxh = d * g_ref[...]
        dx = r * (dxh - xh * jnp.mean(dxh * xh, axis=1, keepdims=True))
        if add is not None:
            dx = dx + add_ref[...]
            dxb_ref[...] = dx.astype(BF16)
        dx_ref[...] = dx.astype(dx_ref.dtype)

        @pl.when(pl.program_id(0) == 0)
        def _():
            dg_ref[...] = jnp.zeros_like(dg_ref)

        dg_ref[...] += _csum(d * xh)

    row = pl.BlockSpec((tm, W), lambda i: (i, 0))
    vec = pl.BlockSpec((1, W), lambda i: (0, 0))
    ins = [x, g, dh] + ([] if add is None else [add])
    specs = [pl.BlockSpec((tm, W), lambda i: (i, col0 // W)), vec, row] + ([] if add is None else [row])
    if add is None:
        return _pc(body, name=name, grid=(T // tm,), in_specs=specs, out_specs=[row, vec],
                   out_shape=[SDS((T, W), BF16), SDS((1, W), F32)])(*ins)
    return _pc(body, name=name, grid=(T // tm,), in_specs=specs, out_specs=[row, row, vec],
               out_shape=[SDS((T, W), F32), SDS((T, W), BF16), SDS((1, W), F32)])(*ins)


def gated_rms_fwd(y, proj, g, *, name):
    T = y.shape[0]
    tm = _tile(T, 512, 8)

    def body(y_ref, z_ref, g_ref, o_ref):
        v = y_ref[...] * _silu(z_ref[...])
        r = lax.rsqrt(jnp.mean(v * v, axis=1, keepdims=True) + EPS)
        o_ref[...] = ((v * r) * g_ref[...]).astype(BF16)

    row = pl.BlockSpec((tm, SSD_W), lambda i: (i, 0))
    return _pc(body, name=name, grid=(T // tm,), in_specs=[row, row, pl.BlockSpec((1, SSD_W), lambda i: (0, 0))],
               out_specs=row, out_shape=SDS((T, SSD_W), BF16))(y, proj, g)


def gated_rms_bwd(y, proj, g, dycat, *, name):
    T = y.shape[0]
    tm = _tile(T, 512, 8)

    def body(y_ref, z_ref, g_ref, d_ref, dy_ref, dz_ref, dg_ref):
        yv, z = y_ref[...], z_ref[...]
        sz = _silu(z)
        v = yv * sz
        r = lax.rsqrt(jnp.mean(v * v, axis=1, keepdims=True) + EPS)
        vh = v * r
        d = d_ref[...]
        dvh = d * g_ref[...]
        dv = r * (dvh - vh * jnp.mean(dvh * vh, axis=1, keepdims=True))
        dy_ref[...] = dv * sz
        dz_ref[...] = (dv * yv * _dsilu(z)).astype(BF16)

        @pl.when(pl.program_id(0) == 0)
        def _():
            dg_ref[...] = jnp.zeros_like(dg_ref)

        dg_ref[...] += _csum(d * vh)

    row = pl.BlockSpec((tm, SSD_W), lambda i: (i, 0))
    vec = pl.BlockSpec((1, SSD_W), lambda i: (0, 0))
    return _pc(body, name=name, grid=(T // tm,), in_specs=[row, row, vec, row], out_specs=[row, row, vec],
               out_shape=[SDS((T, SSD_W), F32), SDS((T, SSD_W), BF16), SDS((1, SSD_W), F32)])(y, proj, g, dycat)


def final_loss(x, g, tgt, *, name):
    T = x.shape[0]
    tm = _tile(T, 512, 8)

    def body(x_ref, g_ref, t_ref, dx_ref, dxb_ref, dg_ref, l_ref):
        v = x_ref[...]
        gg = g_ref[...]
        r = lax.rsqrt(jnp.mean(v * v, axis=1, keepdims=True) + EPS)
        xh = v * r
        err = xh * gg - t_ref[...]
        part = 0.5 * _csum(jnp.mean(err * err, axis=1, keepdims=True))
        d = err * (1.0 / D)
        dxh = d * gg
        dx = r * (dxh - xh * jnp.mean(dxh * xh, axis=1, keepdims=True))
        dx_ref[...] = dx
        dxb_ref[...] = dx.astype(BF16)

        @pl.when(pl.program_id(0) == 0)
        def _():
            dg_ref[...] = jnp.zeros_like(dg_ref)
            l_ref[...] = jnp.zeros_like(l_ref)

        dg_ref[...] += _csum(d * xh)
        l_ref[...] += jnp.broadcast_to(part, (1, LANES))

    row = pl.BlockSpec((tm, D), lambda i: (i, 0))
    vec = pl.BlockSpec((1, D), lambda i: (0, 0))
    return _pc(body, name=name, grid=(T // tm,), in_specs=[row, vec, row],
               out_specs=[row, row, vec, pl.BlockSpec((1, LANES), lambda i: (0, 0))],
               out_shape=[SDS((T, D), F32), SDS((T, D), BF16), SDS((1, D), F32), SDS((1, LANES), F32)])(x, g, tgt)


HALO = 8


def _prev_map(ts, col):
    return lambda b, i, j: (b, jnp.maximum(i * (ts // HALO) - 1, 0), col(j))


def _next_map(ts, n_halo_blocks, col):
    return lambda b, i, j: (b, jnp.minimum((i + 1) * (ts // HALO), n_halo_blocks - 1), col(j))


def _row_chunks(ts, rows):
    rows = min(rows, ts)
    return [(r, rows) for r in range(0, ts, rows)]


def _conv_rows(ext, w_ref, b_ref, r0, n, K):
    win = ext[r0:r0 + HALO + n, :]
    taps = [pltpu.roll(win, K - 1 - k, 0)[HALO:HALO + n] if k < K - 1 else win[HALO:HALO + n] for k in range(K)]
    acc = b_ref[...] + w_ref[0:1, :] * taps[0]
    for k in range(1, K):
        acc = acc + w_ref[k:k + 1, :] * taps[k]
    return acc, taps


def _conv_t_rows(ext2, w_ref, r0, n, K):
    win = ext2[r0:r0 + n + HALO, :]
    dx = w_ref[K - 1:K, :] * win[0:n]
    for k in range(K - 1):
        dx = dx + w_ref[k:k + 1, :] * pltpu.roll(win, n + HALO - (K - 1 - k), 0)[0:n]
    return dx


def _sum8(x):
    acc = x[0:8]
    for r in range(8, x.shape[0], 8):
        acc = acc + x[r:r + 8]
    return acc


def conv_silu_fwd(proj3, w, b, *, name):
    Bl, S, _ = proj3.shape
    C, K = SSD_XBC, SSD_K
    ts, tc = _tile(S, 512, 8), 512
    c0 = XBC0 // tc

    def body(xp_ref, x_ref, w_ref, b_ref, o_ref, ext):
        i = pl.program_id(1)
        ext[0:HALO, :] = jnp.where(i > 0, xp_ref[0], 0.0)
        ext[HALO:HALO + ts, :] = x_ref[0]
        for r0, n in _row_chunks(ts, 32):
            acc, _ = _conv_rows(ext, w_ref, b_ref, r0, n, K)
            o_ref[0, r0:r0 + n, :] = _silu(acc)

    return _pc(body, name=name, grid=(Bl, S // ts, C // tc),
               in_specs=[pl.BlockSpec((1, HALO, tc), _prev_map(ts, lambda j: c0 + j)),
                         pl.BlockSpec((1, ts, tc), lambda b, i, j: (b, i, c0 + j)),
                         pl.BlockSpec((K, tc), lambda b, i, j: (0, j)),
                         pl.BlockSpec((1, tc), lambda b, i, j: (0, j))],
               out_specs=pl.BlockSpec((1, ts, tc), lambda b, i, j: (b, i, j)),
               out_shape=SDS((Bl, S, C), F32), scratch=[pltpu.VMEM((HALO + ts, tc), F32)])(proj3, proj3, w, b)


def conv_silu_bwd(proj3, w, b, dact, *, comm=None, name):
    Bl, S, _ = proj3.shape
    C, K = SSD_XBC, SSD_K
    ts, tc = _tile(S, 512, 8), 512
    c0 = XBC0 // tc
    ns = S // ts

    def body(xp_ref, x_ref, xn_ref, d_ref, dn_ref, w_ref, b_ref, dx_ref, dw_ref, db_ref, ext, ext2):
        bb, i = pl.program_id(1), pl.program_id(2)
        last = i == ns - 1
        ext[0:HALO, :] = jnp.where(i > 0, xp_ref[0], 0.0)
        ext[HALO:HALO + ts, :] = x_ref[0]
        ext[HALO + ts:2 * HALO + ts, :] = jnp.where(last, 0.0, xn_ref[0])
        dw = [jnp.zeros((8, tc), F32) for _ in range(K)]
        db = jnp.zeros((8, tc), F32)
        for r0, n in _row_chunks(ts, 16) + [(ts, HALO)]:
            acc, taps = _conv_rows(ext, w_ref, b_ref, r0, n, K)
            d = d_ref[0, r0:r0 + n, :] if r0 < ts else jnp.where(last, 0.0, dn_ref[0])
            du = d * _dsilu(acc)
            ext2[r0:r0 + n, :] = du
            if r0 < ts:
                dw = [a + _sum8(du * t) for a, t in zip(dw, taps)]
                db = db + _sum8(du)
        for r0, n in _row_chunks(ts, 32):
            dx_ref[0, r0:r0 + n, :] = _conv_t_rows(ext2, w_ref, r0, n, K).astype(BF16)

        @pl.when((bb == 0) & (i == 0))
        def _():
            dw_ref[...] = jnp.zeros_like(dw_ref)
            db_ref[...] = jnp.zeros_like(db_ref)

        for k in range(K):
            dw_ref[k:k + 1, :] += _csum(dw[k])
        db_ref[...] += _csum(db)

    nhb = S // HALO
    cx = lambda j: c0 + j
    cj = lambda j: j
    return _pc(body, name=name, grid=(C // tc, Bl, ns),
               in_specs=[pl.BlockSpec((1, HALO, tc), lambda j, b, i: _prev_map(ts, cx)(b, i, j)),
                         pl.BlockSpec((1, ts, tc), lambda j, b, i: (b, i, c0 + j)),
                         pl.BlockSpec((1, HALO, tc), lambda j, b, i: _next_map(ts, nhb, cx)(b, i, j)),
                         pl.BlockSpec((1, ts, tc), lambda j, b, i: (b, i, j)),
                         pl.BlockSpec((1, HALO, tc), lambda j, b, i: _next_map(ts, nhb, cj)(b, i, j)),
                         pl.BlockSpec((K, tc), lambda j, b, i: (0, j)),
                         pl.BlockSpec((1, tc), lambda j, b, i: (0, j))],
               out_specs=[pl.BlockSpec((1, ts, tc), lambda j, b, i: (b, i, j)),
                          pl.BlockSpec((K, tc), lambda j, b, i: (0, j)),
                          pl.BlockSpec((1, tc), lambda j, b, i: (0, j))],
               out_shape=[SDS((Bl, S, C), BF16), SDS((K, C), F32), SDS((1, C), F32)],
               scratch=[pltpu.VMEM((2 * HALO + ts, tc), F32), pltpu.VMEM((HALO + ts, tc), F32)],
               comm=comm)(proj3, proj3, proj3, dact, dact, w, b)


def ffn_act_fwd(pre3, w, b, *, name):
    Bl, S, _ = pre3.shape
    K = FFN_K
    ts, tc = _tile(S, 512, 8), 256
    nj = DFF // tc

    def body(gp_ref, g_ref, vp_ref, v_ref, wg_ref, wv_ref, bg_ref, bv_ref, o_ref, eg, ev):
        i = pl.program_id(1)
        for p_ref, m_ref, ext in ((gp_ref, g_ref, eg), (vp_ref, v_ref, ev)):
            ext[0:HALO, :] = jnp.where(i > 0, p_ref[0], 0.0)
            ext[HALO:HALO + ts, :] = m_ref[0]
        for r0, n in _row_chunks(ts, 64):
            ug, _ = _conv_rows(eg, wg_ref, bg_ref, r0, n, K)
            uv, _ = _conv_rows(ev, wv_ref, bv_ref, r0, n, K)
            o_ref[0, r0:r0 + n, :] = (_silu(ug) * uv).astype(BF16)

    main = lambda off: pl.BlockSpec((1, ts, tc), lambda b, i, j: (b, i, off + j))
    prev = lambda off: pl.BlockSpec((1, HALO, tc), _prev_map(ts, lambda j: off + j))
    wsp = lambda off: pl.BlockSpec((K, tc), lambda b, i, j: (0, off + j))
    bsp = lambda off: pl.BlockSpec((1, tc), lambda b, i, j: (0, off + j))
    return _pc(body, name=name, grid=(Bl, S // ts, nj),
               in_specs=[prev(0), main(0), prev(nj), main(nj), wsp(0), wsp(nj), bsp(0), bsp(nj)],
               out_specs=pl.BlockSpec((1, ts, tc), lambda b, i, j: (b, i, j)),
               out_shape=SDS((Bl, S, DFF), BF16),
               scratch=[pltpu.VMEM((HALO + ts, tc), F32), pltpu.VMEM((HALO + ts, tc), F32)],
               )(pre3, pre3, pre3, pre3, w, w, b, b)


def ffn_act_bwd(pre3, w, b, dact, *, comm=None, name):
    Bl, S, _ = pre3.shape
    K = FFN_K
    ts, tc = _tile(S, 512, 8), 256
    nj = DFF // tc
    ns = S // ts

    def body(gp_ref, g_ref, gn_ref, vp_ref, v_ref, vn_ref, d_ref, dn_ref, wg_ref, wv_ref, bg_ref, bv_ref,
             dg_ref, dv_ref, dwg_ref, dwv_ref, dbg_ref, dbv_ref, eg, ev, e2g, e2v):
        bb, i = pl.program_id(1), pl.program_id(2)
        last = i == ns - 1
        for p_ref, m_ref, n_ref, ext in ((gp_ref, g_ref, gn_ref, eg), (vp_ref, v_ref, vn_ref, ev)):
            ext[0:HALO, :] = jnp.where(i > 0, p_ref[0], 0.0)
            ext[HALO:HALO + ts, :] = m_ref[0]
            ext[HALO + ts:2 * HALO + ts, :] = jnp.where(last, 0.0, n_ref[0])
        zero8 = jnp.zeros((8, tc), F32)
        dwg, dwv, dbg, dbv = [zero8] * K, [zero8] * K, zero8, zero8
        for r0, n in _row_chunks(ts, 32) + [(ts, HALO)]:
            ug, tg = _conv_rows(eg, wg_ref, bg_ref, r0, n, K)
            uv, tv = _conv_rows(ev, wv_ref, bv_ref, r0, n, K)
            d = d_ref[0, r0:r0 + n, :] if r0 < ts else jnp.where(last, 0.0, dn_ref[0])
            dug = d * uv * _dsilu(ug)
            duv = d * _silu(ug)
            e2g[r0:r0 + n, :] = dug
            e2v[r0:r0 + n, :] = duv
            if r0 < ts:
                dwg = [a + _sum8(dug * t) for a, t in zip(dwg, tg)]
                dwv = [a + _sum8(duv * t) for a, t in zip(dwv, tv)]
                dbg, dbv = dbg + _sum8(dug), dbv + _sum8(duv)
        for w_ref, e2, o_ref in ((wg_ref, e2g, dg_ref), (wv_ref, e2v, dv_ref)):
            for r0, n in _row_chunks(ts, 64):
                o_ref[0, r0:r0 + n, :] = _conv_t_rows(e2, w_ref, r0, n, K).astype(BF16)

        @pl.when((bb == 0) & (i == 0))
        def _():
            for r in (dwg_ref, dwv_ref, dbg_ref, dbv_ref):
                r[...] = jnp.zeros_like(r)

        for dw_ref, dw, db_ref, db in ((dwg_ref, dwg, dbg_ref, dbg), (dwv_ref, dwv, dbv_ref, dbv)):
            for k in range(K):
                dw_ref[k:k + 1, :] += _csum(dw[k])
            db_ref[...] += _csum(db)

    nhb = S // HALO
    main = lambda off: pl.BlockSpec((1, ts, tc), lambda j, b, i: (b, i, off + j))
    prev = lambda off: pl.BlockSpec((1, HALO, tc), lambda j, b, i: _prev_map(ts, lambda jj: off + jj)(b, i, j))
    nxt = lambda off: pl.BlockSpec((1, HALO, tc), lambda j, b, i: _next_map(ts, nhb, lambda jj: off + jj)(b, i, j))
    wsp = lambda off: pl.BlockSpec((K, tc), lambda j, b, i: (0, off + j))
    bsp = lambda off: pl.BlockSpec((1, tc), lambda j, b, i: (0, off + j))
    outs = _pc(body, name=name, grid=(nj, Bl, ns),
               in_specs=[prev(0), main(0), nxt(0), prev(nj), main(nj), nxt(nj), main(0), nxt(0),
                         wsp(0), wsp(nj), bsp(0), bsp(nj)],
               out_specs=[main(0), main(0), wsp(0), wsp(0), bsp(0), bsp(0)],
               out_shape=[SDS((Bl, S, DFF), BF16), SDS((Bl, S, DFF), BF16), SDS((K, DFF), F32), SDS((K, DFF), F32),
                          SDS((1, DFF), F32), SDS((1, DFF), F32)],
               scratch=[pltpu.VMEM((2 * HALO + ts, tc), F32), pltpu.VMEM((2 * HALO + ts, tc), F32),
                        pltpu.VMEM((HALO + ts, tc), F32), pltpu.VMEM((HALO + ts, tc), F32)],
               comm=comm)(pre3, pre3, pre3, pre3, pre3, pre3, dact, dact, w, w, b, b)
    return outs


PHALO = 16


def _pool_window_sums(win, trailing):
    rows = win.shape[0]
    out, s = [], win
    for w in POOL_WIN:
        half = w // 2
        s = s + pltpu.roll(s, half if trailing else rows - half, 0)
        out.append(s)
    return out


def _pick(g, vals):
    r = vals[-1]
    for k in range(len(vals) - 2, -1, -1):
        r = jnp.where(g == k, vals[k], r)
    return r


def _pool_count(g, i, ts, rows, r0=0):
    t = (i * ts + r0 + lax.broadcasted_iota(jnp.int32, (rows, 1), 0) + 1).astype(F32)
    return jnp.minimum(t, _pick(g, [float(w) for w in POOL_WIN]))


def _fill_pool_ext(up_ref, u_ref, ext, i, ts):
    ext[0:PHALO, :] = jnp.where(i > 0, up_ref[0], 0.0)
    ext[PHALO:PHALO + ts, :] = u_ref[0]


def _pooled_rows(ext, g, i, ts, r0, n):
    win = ext[r0:r0 + n + PHALO, :]
    sums = _pool_window_sums(win, True)
    return _pick(g, sums)[PHALO:PHALO + n] / _pool_count(g, i, ts, n, r0) - win[PHALO:PHALO + n]


def pool_fwd(proj3, pool_w, scale, *, name):
    Bl, S, _ = proj3.shape
    ts = _tile(S, 512, 16)
    c0 = U0 // POOL_D

    def body(up_ref, u_ref, w_ref, s_ref, o_ref, ext):
        i, g = pl.program_id(1), pl.program_id(2)
        _fill_pool_ext(up_ref, u_ref, ext, i, ts)
        wm, sc = w_ref[0], s_ref[...]
        for r0, n in _row_chunks(ts, 128):
            o_ref[0, r0:r0 + n, :] = (_nn(_pooled_rows(ext, g, i, ts, r0, n), wm) * sc).astype(BF16)

    return _pc(body, name=name, grid=(Bl, S // ts, POOL_G),
               in_specs=[pl.BlockSpec((1, PHALO, POOL_D), lambda b, i, g: (b, jnp.maximum(i * (ts // PHALO) - 1, 0), c0 + g)),
                         pl.BlockSpec((1, ts, POOL_D), lambda b, i, g: (b, i, c0 + g)),
                         pl.BlockSpec((1, POOL_D, POOL_D), lambda b, i, g: (g, 0, 0)),
                         pl.BlockSpec((1, POOL_D), lambda b, i, g: (0, g))],
               out_specs=pl.BlockSpec((1, ts, POOL_D), lambda b, i, g: (b, i, g)),
               out_shape=SDS((Bl, S, POOL_W), BF16), scratch=[pltpu.VMEM((PHALO + ts, POOL_D), F32)],
               )(proj3, proj3, pool_w, scale)


def pool_bwd(proj3, pool_w, scale, dycat3, *, name):
    Bl, S, _ = proj3.shape
    ts = _tile(S, 512, 16)
    ns = S // ts
    c0 = U0 // POOL_D
    d0 = SSD_W // POOL_D
    nhb = S // PHALO

    def body(up_ref, u_ref, d_ref, dn_ref, w_ref, s_ref, du_ref, dw_ref, ds_ref, ext, ext2):
        g, bb, i = pl.program_id(0), pl.program_id(1), pl.program_id(2)
        last = i == ns - 1
        _fill_pool_ext(up_ref, u_ref, ext, i, ts)
        wm = w_ref[0]
        sc = s_ref[...]
        dwa = jnp.zeros((POOL_D, POOL_D), F32)
        dsa = jnp.zeros((8, POOL_D), F32)
        dpools = []
        for r0, n in _row_chunks(ts, 128):
            pooled = _pooled_rows(ext, g, i, ts, r0, n)
            dy = d_ref[0, r0:r0 + n, :]
            dp = dy * sc
            dpool = _nt(dp, wm)
            dpools.append(dpool)
            ext2[r0:r0 + n, :] = dpool / _pool_count(g, i, ts, n, r0)
            dwa = dwa + _tn(pooled, dp)
            dsa = dsa + _sum8(dy * _nn(pooled, wm))
        dpool_n = _nt(jnp.where(last, 0.0, dn_ref[0]) * sc, wm)
        ext2[ts:ts + PHALO, :] = dpool_n / _pool_count(g, i + 1, ts, PHALO)
        for (r0, n), dpool in zip(_row_chunks(ts, 128), dpools):
            sums = _pool_window_sums(ext2[r0:r0 + n + PHALO, :], False)
            du_ref[0, r0:r0 + n, :] = (_pick(g, sums)[0:n] - dpool).astype(BF16)

        @pl.when((bb == 0) & (i == 0))
        def _():
            dw_ref[...] = jnp.zeros_like(dw_ref)
            ds_ref[...] = jnp.zeros_like(ds_ref)

        dw_ref[0] += dwa
        ds_ref[...] += _csum(dsa)

    return _pc(body, name=name, grid=(POOL_G, Bl, ns),
               in_specs=[pl.BlockSpec((1, PHALO, POOL_D), lambda g, b, i: (b, jnp.maximum(i * (ts // PHALO) - 1, 0), c0 + g)),
                         pl.BlockSpec((1, ts, POOL_D), lambda g, b, i: (b, i, c0 + g)),
                         pl.BlockSpec((1, ts, POOL_D), lambda g, b, i: (b, i, d0 + g)),
                         pl.BlockSpec((1, PHALO, POOL_D), lambda g, b, i: (b, jnp.minimum((i + 1) * (ts // PHALO), nhb - 1), d0 + g)),
                         pl.BlockSpec((1, POOL_D, POOL_D), lambda g, b, i: (g, 0, 0)),
                         pl.BlockSpec((1, POOL_D), lambda g, b, i: (0, g))],
               out_specs=[pl.BlockSpec((1, ts, POOL_D), lambda g, b, i: (b, i, g)),
                          pl.BlockSpec((1, POOL_D, POOL_D), lambda g, b, i: (g, 0, 0)),
                          pl.BlockSpec((1, POOL_D), lambda g, b, i: (0, g))],
               out_shape=[SDS((Bl, S, POOL_W), BF16), SDS((POOL_G, POOL_D, POOL_D), F32), SDS((1, POOL_W), F32)],
               scratch=[pltpu.VMEM((PHALO + ts, POOL_D), F32), pltpu.VMEM((PHALO + ts, POOL_D), F32)],
               )(proj3, proj3, dycat3, dycat3, pool_w, scale)


NPAIR = SSD_HEADS // 2


def _ssd_common(sm, bias, alog):
    L = SSD_L
    dt = jax.nn.softplus(sm + bias)
    a = -jnp.exp(alog)
    da = dt * a
    r = lax.broadcasted_iota(jnp.int32, (L, L), 0)
    c = lax.broadcasted_iota(jnp.int32, (L, L), 1)
    tri = (r >= c).astype(F32)
    cum = _dg(tri, da, 1, 0, lax.Precision.HIGHEST)
    return dt, a, cum, cum.T, r >= c


def _lanes(lo, hi, shape=(1, LANES)):
    lane = lax.broadcasted_iota(jnp.int32, shape, len(shape) - 1)
    return (lane >= lo) & (lane < hi)


def _onehot_lane(h):
    return (lax.broadcasted_iota(jnp.int32, (1, LANES), 1) == h).astype(F32)


def _split_nn(a, e):
    hi = a.astype(BF16)
    lo = (a - hi.astype(F32)).astype(BF16)
    return _dg(hi, e, 1, 0) + _dg(lo, e, 1, 0)


def _head_spread():
    r = lax.broadcasted_iota(jnp.int32, (LANES, SSD_W), 0)
    c = lax.broadcasted_iota(jnp.int32, (LANES, SSD_W), 1)
    return (c // SSD_P == r).astype(BF16)


def _pair_gather(j):
    r = lax.broadcasted_iota(jnp.int32, (LANES, LANES), 0)
    c = lax.broadcasted_iota(jnp.int32, (LANES, LANES), 1)
    return (c == 2 * j + (r >= SSD_P).astype(jnp.int32)).astype(BF16)


def ssd_fwd(xbc3, proj3, bias, alog, dskip, *, comm=None, name):
    Bl, S, _ = xbc3.shape
    L = SSD_L
    nc = S // L

    def body(xbc_ref, sm_ref, bias_ref, alog_ref, d_ref, y_ref, hin_ref, H):
        c = pl.program_id(1)

        @pl.when(c == 0)
        def _():
            H[...] = jnp.zeros_like(H)

        dt, a, cum, cumT, mask = _ssd_common(sm_ref[0], bias_ref[...], alog_ref[...])
        lo = _lanes(0, SSD_P)
        rowlo = lax.broadcasted_iota(jnp.int32, (LANES, LANES), 0) < SSD_P
        spread = _head_spread()
        dt_x = _split_nn(dt, spread)
        el_x = _split_nn(jnp.exp(cum), spread)
        wl_x = _split_nn(jnp.exp(cum[L - 1:L, :] - cum), spread)
        cb = []
        for g in range(SSD_G):
            Bg = xbc_ref[0, :, SSD_W + g * SSD_N:SSD_W + (g + 1) * SSD_N]
            Cg = xbc_ref[0, :, SSD_W + SSD_G * SSD_N + g * SSD_N:SSD_W + SSD_G * SSD_N + (g + 1) * SSD_N]
            cb.append((Bg, Cg, _nt(Cg, Bg)))
        for j in range(NPAIR):
            h0, h1 = 2 * j, 2 * j + 1
            sl = slice(j * LANES, (j + 1) * LANES)
            Bg, Cg, CB = cb[j // (NPAIR // SSD_G)]
            X = xbc_ref[0, :, sl]
            c0, c1 = cum[:, h0:h0 + 1], cum[:, h1:h1 + 1]
            r0, r1 = cumT[h0:h0 + 1, :], cumT[h1:h1 + 1, :]
            cl0, cl1 = cum[L - 1:L, h0:h0 + 1], cum[L - 1:L, h1:h1 + 1]
            Xt = X * dt_x[:, sl]
            M0 = CB * jnp.exp(jnp.where(mask, c0 - r0, NEG))
            M1 = CB * jnp.exp(jnp.where(mask, c1 - r1, NEG))
            Yd = jnp.where(lo, _nn(M0, Xt), _nn(M1, Xt))
            Hp = H[j]
            hin_ref[0, 0, j] = Hp
            Z = _nt(Cg, Hp)
            y_ref[0, :, sl] = Yd + el_x[:, sl] * Z + X * d_ref[j:j + 1, :]
            H[j] = jnp.where(rowlo, jnp.exp(cl0), jnp.exp(cl1)) * Hp + _tn(wl_x[:, sl] * Xt, Bg)

    vec = pl.BlockSpec((1, LANES), lambda b, c: (0, 0))
    return _pc(body, name=name, grid=(Bl, nc),
               in_specs=[pl.BlockSpec((1, L, SSD_XBC), lambda b, c: (b, c, 0)),
                         pl.BlockSpec((1, L, LANES), lambda b, c: (b, c, DT0 // LANES)),
                         vec, vec, pl.BlockSpec((NPAIR, LANES), lambda b, c: (0, 0))],
               out_specs=[pl.BlockSpec((1, L, SSD_W), lambda b, c: (b, c, 0)),
                          pl.BlockSpec((1, 1, NPAIR, LANES, LANES), lambda b, c: (b, c, 0, 0, 0))],
               out_shape=[SDS((Bl, S, SSD_W), F32), SDS((Bl, nc, NPAIR, LANES, LANES), F32)],
               scratch=[pltpu.VMEM((NPAIR, LANES, LANES), F32)], comm=comm)(xbc3, proj3, bias, alog, dskip)


def ssd_bwd(xbc3, proj3, hin, dy3, bias, alog, dskip, *, comm=None, name):
    Bl, S, _ = xbc3.shape
    L = SSD_L
    nc = S // L

    def body(xbc_ref, sm_ref, hin_ref, dy_ref, bias_ref, alog_ref, d_ref, dx_ref, ddt_ref, dpar_ref, dd_ref, dH, ddacc):
        bb, i = pl.program_id(0), pl.program_id(1)

        @pl.when(i == 0)
        def _():
            dH[...] = jnp.zeros_like(dH)

        @pl.when((bb == 0) & (i == 0))
        def _():
            dpar_ref[...] = jnp.zeros_like(dpar_ref)
            ddacc[...] = jnp.zeros_like(ddacc)

        sm = sm_ref[0]
        dt, a, cum, cumT, mask = _ssd_common(sm, bias_ref[...], alog_ref[...])
        maskf = mask.astype(F32)
        lo = _lanes(0, SSD_P)
        rowlo = lax.broadcasted_iota(jnp.int32, (LANES, LANES), 0) < SSD_P
        lastrow = (lax.broadcasted_iota(jnp.int32, (L, 1), 0) == L - 1).astype(F32)
        dcum = jnp.zeros((L, LANES), F32)
        dcum_t = jnp.zeros((LANES, L), F32)
        ddt = jnp.zeros((L, LANES), F32)
        spread = _head_spread()
        ones = jnp.ones((L, LANES), BF16)
        ecum = jnp.exp(cum)
        wall = jnp.exp(cum[L - 1:L, :] - cum)
        dt_x = _split_nn(dt, spread)
        el_x = _split_nn(ecum, spread)
        wl_x = _split_nn(wall, spread)
        headrow = lax.broadcasted_iota(jnp.int32, (LANES, 1), 0)
        grp = []
        for g in range(SSD_G):
            Bg = xbc_ref[0, :, SSD_W + g * SSD_N:SSD_W + (g + 1) * SSD_N]
            Cg = xbc_ref[0, :, SSD_W + SSD_G * SSD_N + g * SSD_N:SSD_W + SSD_G * SSD_N + (g + 1) * SSD_N]
            grp.append(dict(B=Bg, C=Cg, CB=_nt(Cg, Bg), dB=jnp.zeros((L, SSD_N), F32), dC=jnp.zeros((L, SSD_N), F32),
                            dCB=jnp.zeros((L, L), F32)))
        for j in range(NPAIR):
            h0, h1 = 2 * j, 2 * j + 1
            sl = slice(j * LANES, (j + 1) * LANES)
            G = grp[j // (NPAIR // SSD_G)]
            Bg, Cg, CB = G["B"], G["C"], G["CB"]
            X = xbc_ref[0, :, sl]
            dY = dy_ref[0, :, sl]
            c0, c1 = cum[:, h0:h0 + 1], cum[:, h1:h1 + 1]
            r0, r1 = cumT[h0:h0 + 1, :], cumT[h1:h1 + 1, :]
            cl0, cl1 = cum[L - 1:L, h0:h0 + 1], cum[L - 1:L, h1:h1 + 1]
            oh0, oh1 = _onehot_lane(h0), _onehot_lane(h1)
            gather = _pair_gather(j)
            dtl, el, wl = dt_x[:, sl], el_x[:, sl], wl_x[:, sl]
            Xt = X * dtl
            Hp = hin_ref[0, 0, j]
            dS = dH[j]
            dX = dY * d_ref[j:j + 1, :]
            ddacc[j:j + 1, :] += _csum(dY * X)
            Z = _nt(Cg, Hp)
            dZ = dY * el
            dcum = dcum + _split_nn(dY * Z, gather) * ecum
            G["dC"] = G["dC"] + _nn(dZ, Hp)
            dHy = _tn(dZ, Cg)
            Gm = _nt(Bg, dS)
            dXt = wl * Gm
            q = _split_nn(Xt * Gm, gather) * wall
            dcum = dcum + lastrow * _csum(q) - q
            G["dB"] = G["dB"] + _nn(wl * Xt, dS)
            g0, g1 = jnp.exp(cl0), jnp.exp(cl1)
            rowsum = _nn(dS * Hp, ones)
            dg0 = _csum(jnp.where(rowlo, rowsum, 0.0))
            dg1 = _csum(jnp.where(rowlo, 0.0, rowsum))
            dcum = dcum + lastrow * ((dg0 * g0) * oh0 + (dg1 * g1) * oh1)
            dH[j] = jnp.where(rowlo, g0, g1) * dS + dHy
            for h, ch, rh, mh, oh in ((h0, c0, r0, lo, oh0), (h1, c1, r1, jnp.logical_not(lo), oh1)):
                decay = jnp.exp(jnp.where(mask, ch - rh, NEG))
                Mh = CB * decay
                dM = _nt(jnp.where(mh, dY, 0.0), Xt) * maskf
                dXt = dXt + jnp.where(mh, _tn(Mh, dY), 0.0)
                G["dCB"] = G["dCB"] + dM * decay
                Q = dM * Mh
                dcum = dcum + _rsum(Q) * oh
                dcum_t = dcum_t + (headrow == h).astype(F32) * _csum(Q)
            dX = dX + dXt * dtl
            ddt = ddt + _split_nn(dXt * X, gather)
            dx_ref[0, :, sl] = dX
        dcum = dcum - dcum_t.T
        for g in range(SSD_G):
            G = grp[g]
            dC = G["dC"] + _nn(G["dCB"], G["B"])
            dB = G["dB"] + _tn(G["dCB"], G["C"])
            dx_ref[0, :, SSD_W + g * SSD_N:SSD_W + (g + 1) * SSD_N] = dB
            dx_ref[0, :, SSD_W + SSD_G * SSD_N + g * SSD_N:SSD_W + SSD_G * SSD_N + (g + 1) * SSD_N] = dC
        r = lax.broadcasted_iota(jnp.int32, (L, L), 0)
        c = lax.broadcasted_iota(jnp.int32, (L, L), 1)
        dda = _dg((c >= r).astype(F32), dcum, 1, 0, lax.Precision.HIGHEST)
        heads = _lanes(0, SSD_HEADS)
        ddt = ddt + dda * a
        draw = jnp.where(heads, ddt * _sig(sm + bias_ref[...]), 0.0)
        ddt_ref[0] = draw.astype(BF16)
        dpar_ref[0:1, :] += _csum(draw)
        dpar_ref[1:2, :] += _csum(jnp.where(heads, dda * dt * a, 0.0))

        @pl.when((bb == Bl - 1) & (i == nc - 1))
        def _():
            acc = ddacc[...]
            lane = lax.broadcasted_iota(jnp.int32, (NPAIR, LANES), 1)
            s0 = _rsum(jnp.where(lane < SSD_P, acc, 0.0))
            s1 = _rsum(jnp.where(lane < SSD_P, 0.0, acc))
            dd_ref[...] = jnp.where(lane == 0, s0, jnp.where(lane == 1, s1, 0.0))

    vec = pl.BlockSpec((1, LANES), lambda b, i: (0, 0))
    par = pl.BlockSpec((NPAIR, LANES), lambda b, i: (0, 0))
    return _pc(body, name=name, grid=(Bl, nc),
               in_specs=[pl.BlockSpec((1, L, SSD_XBC), lambda b, i: (b, nc - 1 - i, 0)),
                         pl.BlockSpec((1, L, LANES), lambda b, i: (b, nc - 1 - i, DT0 // LANES)),
                         pl.BlockSpec((1, 1, NPAIR, LANES, LANES), lambda b, i: (b, nc - 1 - i, 0, 0, 0)),
                         pl.BlockSpec((1, L, SSD_W), lambda b, i: (b, nc - 1 - i, 0)),
                         vec, vec, par],
               out_specs=[pl.BlockSpec((1, L, SSD_XBC), lambda b, i: (b, nc - 1 - i, 0)),
                          pl.BlockSpec((1, L, LANES), lambda b, i: (b, nc - 1 - i, 0)),
                          par, par],
               out_shape=[SDS((Bl, S, SSD_XBC), F32), SDS((Bl, S, LANES), BF16), SDS((NPAIR, LANES), F32),
                          SDS((NPAIR, LANES), F32)],
               scratch=[pltpu.VMEM((NPAIR, LANES, LANES), F32), pltpu.VMEM((NPAIR, LANES), F32)],
               comm=comm)(xbc3, proj3, hin, dy3, bias, alog, dskip)


PE_LO, PE_MID, PE_HI = MLA_NOPE, MLA_NOPE + MLA_ROPE // 2, MLA_NOPE + MLA_ROPE
ATT_SCALE = 1.0 / math.sqrt(MLA_QK)


def _swap_matrix():
    src = lax.broadcasted_iota(jnp.int32, (LANES, LANES), 0)
    dst = lax.broadcasted_iota(jnp.int32, (LANES, LANES), 1)
    half = MLA_ROPE // 2
    first = (dst >= PE_LO) & (dst < PE_MID) & (src == dst + half)
    second = (dst >= PE_MID) & (dst < PE_HI) & (src == dst - half)
    return (second.astype(F32) - first.astype(F32)).astype(BF16)


def rope_tables(pos, invf, *, name):
    T = pos.shape[0]
    tm = _tile(T, 512, 8)

    def body(pos_ref, f_ref, c_ref, s_ref):
        ang = pos_ref[...] * f_ref[...]
        pe = _lanes(PE_LO, PE_HI)
        c_ref[...] = jnp.where(pe, jnp.cos(ang), 1.0)
        s_ref[...] = jnp.where(pe, jnp.sin(ang), 0.0)

    tile = pl.BlockSpec((tm, LANES), lambda i: (i, 0))
    return _pc(body, name=name, grid=(T // tm,),
               in_specs=[pl.BlockSpec((tm, 1), lambda i: (i, 0)), pl.BlockSpec((1, LANES), lambda i: (0, 0))],
               out_specs=[tile, tile], out_shape=[SDS((T, LANES), F32)] * 2)(pos, invf)


V_ONE = MLA_V


def mla_prep_fwd(qt, kvt, proj, cs, sn, *, name):
    T = qt.shape[0]
    tm = _tile(T, 256, 8)
    HW = MLA_H * LANES

    def body(q_ref, k_ref, v_ref, kpe_ref, c_ref, s_ref, qo_ref, ko_ref, vo_ref):
        c, s = c_ref[...], s_ref[...]
        kpe = kpe_ref[...]
        sw = _swap_matrix()
        one = _lanes(V_ONE, V_ONE + 1)
        for h in range(MLA_H):
            sl = slice(h * LANES, (h + 1) * LANES)
            q = q_ref[:, sl]
            k = k_ref[:, sl] + kpe
            qo_ref[:, sl] = ((q * c + _split_nn(q, sw) * s) * ATT_SCALE).astype(BF16)
            ko_ref[:, sl] = (k * c + _split_nn(k, sw) * s).astype(BF16)
            vo_ref[:, sl] = jnp.where(one, 1.0, v_ref[:, sl]).astype(BF16)

    row = pl.BlockSpec((tm, HW), lambda i: (i, 0))
    tab = pl.BlockSpec((tm, LANES), lambda i: (i, 0))
    return _pc(body, name=name, grid=(T // tm,),
               in_specs=[row, row, pl.BlockSpec((tm, HW), lambda i: (i, 1)),
                         pl.BlockSpec((tm, LANES), lambda i: (i, KPE0 // LANES)), tab, tab],
               out_specs=[row, row, row], out_shape=[SDS((T, HW), BF16)] * 3)(qt, kvt, kvt, proj, cs, sn)


def mla_prep_bwd(dqr, dkr, cs, sn, *, name):
    T = dqr.shape[0]
    tm = _tile(T, 256, 8)
    HW = MLA_H * LANES

    def body(dq_ref, dk_ref, c_ref, s_ref, qo_ref, ko_ref, kpe_ref):
        c, s = c_ref[...], s_ref[...]
        sw = _swap_matrix()
        pe = _lanes(PE_LO, PE_HI)
        dkpe = jnp.zeros((tm, LANES), F32)
        for h in range(MLA_H):
            sl = slice(h * LANES, (h + 1) * LANES)
            dq = dq_ref[:, sl] * ATT_SCALE
            dk = dk_ref[:, sl]
            qo_ref[:, sl] = (dq * c - _split_nn(dq * s, sw)).astype(BF16)
            dkk = dk * c - _split_nn(dk * s, sw)
            ko_ref[:, sl] = jnp.where(pe, 0.0, dkk).astype(BF16)
            dkpe = dkpe + jnp.where(pe, dkk, 0.0)
        kpe_ref[...] = dkpe.astype(BF16)

    row = pl.BlockSpec((tm, HW), lambda i: (i, 0))
    tab = pl.BlockSpec((tm, LANES), lambda i: (i, 0))
    return _pc(body, name=name, grid=(T // tm,), in_specs=[row, row, tab, tab], out_specs=[row, row, tab],
               out_shape=[SDS((T, HW), BF16), SDS((T, HW), BF16), SDS((T, LANES), BF16)])(dqr, dkr, cs, sn)


def _att_tile(S):
    return _tile(S, 512, LANES)


def _rep(x, n):
    return x if n == 1 else jnp.concatenate([x] * n, axis=1)


def _diag_mask(t, transposed=False):
    r = lax.broadcasted_iota(jnp.int32, (t, t), 0)
    c = lax.broadcasted_iota(jnp.int32, (t, t), 1)
    return (c >= r) if transposed else (c <= r)


def flash_fwd(qr, kr, vr, Bl, *, comm=None, name):
    T = qr.shape[0]
    S = T // Bl
    t = _att_tile(S)
    n = S // t
    nl = t // LANES

    def body(q_ref, k_ref, v_ref, o_ref, lse_ref, lset_ref, m, acc):
        qi = pl.program_id(2)
        q = q_ref[...]
        m[...] = jnp.full_like(m, NEG)
        acc[...] = jnp.zeros_like(acc)

        def block(kj, masked):
            off = pl.multiple_of(kj * t, t)
            s = _nt(q, k_ref[pl.ds(off, t), :])
            if masked:
                s = jnp.where(_diag_mask(t), s, NEG)
            mo = m[...]
            mn = jnp.maximum(mo, jnp.max(s, axis=1, keepdims=True))
            p = jnp.exp((s - _rep(mn, nl)).astype(BF16))
            acc[...] = jnp.exp(mo - mn) * acc[...] + _nn(p, v_ref[pl.ds(off, t), :])
            m[...] = mn

        def loop(kj, c):
            block(kj, False)
            return c

        lax.fori_loop(0, qi, loop, 0)
        block(qi, True)
        a = acc[...]
        l = a[:, V_ONE:V_ONE + 1]
        o_ref[...] = jnp.where(_lanes(0, MLA_V), a / l, 0.0).astype(BF16)
        lse = m[...] + jnp.log(l)
        lse_ref[...] = lse
        lset_ref[...] = lse.T[0:8, :]

    qs = pl.BlockSpec((t, LANES), lambda b, h, qi: (b * n + qi, h))
    seq = pl.BlockSpec((S, LANES), lambda b, h, qi: (b, h))
    return _pc(body, name=name, grid=(Bl, MLA_H, n), in_specs=[qs, seq, seq],
               out_specs=[qs, qs, pl.BlockSpec((8, t), lambda b, h, qi: (b * MLA_H + h, qi))],
               out_shape=[SDS((T, MLA_H * LANES), BF16), SDS((T, MLA_H * LANES), F32), SDS((Bl * MLA_H * 8, S), F32)],
               scratch=[pltpu.VMEM((t, LANES), F32), pltpu.VMEM((t, LANES), F32)], comm=comm)(qr, kr, vr)


def attn_delta(o, dycat, Bl, *, name):
    T = o.shape[0]
    S = T // Bl
    t = _att_tile(S)
    n = S // t
    do0 = (SSD_W + POOL_W) // LANES

    def body(o_ref, do_ref, dlt_ref):
        d = _rsum(do_ref[...].astype(F32) * o_ref[...].astype(F32))
        dlt_ref[...] = jnp.broadcast_to(d, (t, LANES)).T[0:8, :]

    return _pc(body, name=name, grid=(Bl, MLA_H, n),
               in_specs=[pl.BlockSpec((t, LANES), lambda b, h, qi: (b * n + qi, h)),
                         pl.BlockSpec((t, LANES), lambda b, h, qi: (b * n + qi, do0 + h))],
               out_specs=pl.BlockSpec((8, t), lambda b, h, qi: (b * MLA_H + h, qi)),
               out_shape=SDS((Bl * MLA_H * 8, S), F32))(o, dycat)


def flash_bwd(qr, kr, vr, lset, dlt, dycat, Bl, *, comm=None, name):
    T = qr.shape[0]
    S = T // Bl
    t = _att_tile(S)
    n = S // t
    do0 = (SSD_W + POOL_W) // LANES

    def body(q_ref, k_ref, v_ref, lset_ref, dlt_ref, do_ref, dq_ref, dk_ref, dv_ref, dka, dva):
        kj = pl.program_id(2)

        @pl.when(kj == 0)
        def _():
            dq_ref[...] = jnp.zeros_like(dq_ref)

        k = k_ref[...]
        v = v_ref[...]
        dka[...] = jnp.zeros_like(dka)
        dva[...] = jnp.zeros_like(dva)

        def block(qi, masked):
            off = pl.multiple_of(qi * t, t)
            q = q_ref[pl.ds(off, t), :]
            do = do_ref[pl.ds(off, t), :].astype(BF16)
            st = _nt(k, q)
            if masked:
                st = jnp.where(_diag_mask(t, True), st, NEG)
            pt = jnp.exp((st - lset_ref[0:1, pl.ds(off, t)]).astype(BF16))
            dst = pt * (_nt(v, do) - dlt_ref[0:1, pl.ds(off, t)])
            dva[...] += _nn(pt, do)
            dka[...] += _nn(dst, q)
            dq_ref[pl.ds(off, t), :] += _tn(dst, k)

        def loop(qi, c):
            block(qi, False)
            return c

        block(kj, True)
        lax.fori_loop(kj + 1, n, loop, 0)
        dk_ref[...] = dka[...]
        dv_ref[...] = dva[...].astype(BF16)

    ks = pl.BlockSpec((t, LANES), lambda b, h, kj: (b * n + kj, h))
    seq = pl.BlockSpec((S, LANES), lambda b, h, kj: (b, h))
    rows = pl.BlockSpec((8, S), lambda b, h, kj: (b * MLA_H + h, 0))
    return _pc(body, name=name, grid=(Bl, MLA_H, n),
               in_specs=[seq, ks, ks, rows, rows, pl.BlockSpec((S, LANES), lambda b, h, kj: (b, do0 + h))],
               out_specs=[seq, ks, ks],
               out_shape=[SDS((T, MLA_H * LANES), F32), SDS((T, MLA_H * LANES), F32), SDS((T, MLA_H * LANES), BF16)],
               scratch=[pltpu.VMEM((t, LANES), F32), pltpu.VMEM((t, LANES), F32)], comm=comm)(qr, kr, vr, lset, dlt, dycat)


def _rows2d(a):
    return a.reshape(-1, a.shape[-1])


def _scalar(i):
    return jnp.reshape(i, (1,)).astype(jnp.int32)


def chip_sum(g8, from_sibling, *, name):
    blk = g8.shape[1:]
    R, C = math.prod(blk[:-1]), blk[-1]
    tm = _tile(R, 512, 16)

    def body(c_ref, a_ref, b_ref, o_ref, ob_ref):
        s = a_ref[0, 0] + b_ref[0]
        o_ref[0] = s
        ob_ref[0] = s.astype(BF16)

    row = pl.BlockSpec((1, tm, C), lambda k, i, c: (k, i, 0))
    spec = pltpu.PrefetchScalarGridSpec(
        num_scalar_prefetch=1, grid=(4, R // tm),
        in_specs=[pl.BlockSpec((1, 1, tm, C), lambda k, i, c: (k, c[0], i, 0)), row], out_specs=[row, row])
    o, ob = pl.pallas_call(body, name=name, grid_spec=spec, out_shape=[SDS((4, R, C), F32), SDS((4, R, C), BF16)],
                           compiler_params=pltpu.CompilerParams(vmem_limit_bytes=VMEM_LIMIT),
                           )(_scalar(lax.axis_index("c")), g8.reshape(4, 2, R, C), from_sibling.reshape(4, R, C))
    return o.reshape((4,) + blk), ob.reshape((4,) + blk)


def adamw_sharded(w, m, v, sums, recv, layer, prev, *, name):
    blk = w.shape[1:]
    R, C = math.prod(blk[:-1]), blk[-1]
    tm = _tile(R, 256, 16)
    bc1 = 1.0 - ADAM_B1 ** ADAM_STEP
    bc2 = 1.0 - ADAM_B2 ** ADAM_STEP
    n_prev = 0 if prev is None else 4

    def body(chip_ref, w_ref, m_ref, v_ref, s_ref, r_ref, *rest):
        g_ref, d_ref, nm_ref, nv_ref = rest[n_prev:]
        g = s_ref[0] + r_ref[0].astype(F32) + r_ref[1].astype(F32) + r_ref[2].astype(F32)
        mm_ = ADAM_B1 * m_ref[0] + (1.0 - ADAM_B1) * g
        vv = ADAM_B2 * v_ref[0] + (1.0 - ADAM_B2) * (g * g)
        g_ref[0] = g
        nm_ref[0] = mm_
        nv_ref[0] = vv
        d_ref[0] = -ADAM_LR * ((mm_ / bc1) / (jnp.sqrt(vv / bc2) + ADAM_EPS) + ADAM_WD * w_ref[0])

    lay = pl.BlockSpec((1, tm, C), lambda i, c: (layer, i, 0))
    spec = pltpu.PrefetchScalarGridSpec(
        num_scalar_prefetch=1, grid=(R // tm,),
        in_specs=[lay, lay, lay, pl.BlockSpec((1, tm, C), lambda i, c: (c[0], i, 0)),
                  pl.BlockSpec((3, tm, C), lambda i, c: (0, i, 0))] + [ANY] * n_prev,
        out_specs=[lay] * 4)
    xi, yi, _ = _place()
    d3 = (w.shape[0], R, C)
    outs = pl.pallas_call(
        body, name=name, grid_spec=spec, out_shape=[SDS(d3, F32)] * 4,
        input_output_aliases={6 + i: i for i in range(n_prev)},
        compiler_params=pltpu.CompilerParams(vmem_limit_bytes=VMEM_LIMIT),
    )(_scalar(2 * xi + yi), w.reshape(d3), m.reshape(d3), v.reshape(d3), sums.reshape(4, R, C), recv.reshape(3, R, C),
      *([] if prev is None else prev))
    return list(outs)


def adamw(w, m, v, parts, *, name):
    shp = w.shape
    w2, m2, v2 = _rows2d(w), _rows2d(m), _rows2d(v)
    R, C = w2.shape
    p3 = [p.reshape(p.shape[0], R, C) for p in parts]
    tm = _tile(R, 256, 8)
    bc1 = 1.0 - ADAM_B1 ** ADAM_STEP
    bc2 = 1.0 - ADAM_B2 ** ADAM_STEP

    def body(w_ref, m_ref, v_ref, *refs):
        p_refs, (g_ref, d_ref, nm_ref, nv_ref) = refs[:len(p3)], refs[len(p3):]
        g = None
        for p_ref, p in zip(p_refs, p3):
            for k in range(p.shape[0]):
                term = p_ref[k].astype(F32)
                g = term if g is None else g + term
        mm_ = ADAM_B1 * m_ref[...] + (1.0 - ADAM_B1) * g
        vv = ADAM_B2 * v_ref[...] + (1.0 - ADAM_B2) * (g * g)
        g_ref[...] = g
        nm_ref[...] = mm_
        nv_ref[...] = vv
        d_ref[...] = -ADAM_LR * ((mm_ / bc1) / (jnp.sqrt(vv / bc2) + ADAM_EPS) + ADAM_WD * w_ref[...])

    blk = pl.BlockSpec((tm, C), lambda i: (i, 0))
    pspecs = [pl.BlockSpec((p.shape[0], tm, C), lambda i: (0, i, 0)) for p in p3]
    outs = _pc(body, name=name, grid=(R // tm,), in_specs=[blk, blk, blk] + pspecs,
               out_specs=[blk] * 4, out_shape=[SDS((R, C), F32)] * 4)(w2, m2, v2, *p3)
    return [o.reshape(shp) for o in outs]


def _place():
    return lax.axis_index("x"), lax.axis_index("y"), lax.axis_index("c")


def all_gather_many(xs, *, name):
    n = len(xs)

    def body(*refs):
        x_refs, o_refs = refs[:n], refs[n:2 * n]
        send_sems, recv_sems, local_sems = refs[2 * n:]
        x, y, c = _place()
        me, sibling = (x, y, c), (x, y, 1 - c)
        chips = [(1 - x, y), (x, 1 - y), (1 - x, 1 - y)]

        def rows(a, p):
            return o_refs[a].at[4 * p[0] + 2 * p[1] + p[2]]

        def copy(a, k, block, to, src=None):
            return pltpu.make_async_remote_copy(
                src_ref=rows(a, block) if src is None else src, dst_ref=rows(a, block),
                send_sem=send_sems.at[7 * a + k], recv_sem=recv_sems.at[7 * a + k], device_id=to, device_id_type=MESH)

        mine = [pltpu.make_async_copy(x_refs[a], rows(a, me), local_sems.at[a]) for a in range(n)]
        for cp in mine:
            cp.start()
        first = []
        for a in range(n):
            first.append(copy(a, 0, me, sibling, src=x_refs[a]))
            first += [copy(a, 1 + j, me, (*chip, c), src=x_refs[a]) for j, chip in enumerate(chips)]
        for cp in first:
            cp.start()
        passed = []
        for j, chip in enumerate(chips):
            for a in range(n):
                copy(a, 1 + j, (*chip, c), me).wait_recv()
                cp = copy(a, 4 + j, (*chip, c), sibling)
                cp.start()
                passed.append(cp)
        for a in range(n):
            copy(a, 0, sibling, me).wait_recv()
            for j, chip in enumerate(chips):
                copy(a, 4 + j, (*chip, 1 - c), me).wait_recv()
        for cp in first + passed:
            cp.wait_send()
        for cp in mine:
            cp.wait()

    return pl.pallas_call(
        body, name=name, in_specs=[ANY] * n, out_specs=[ANY] * n,
        out_shape=[SDS((N_DEV,) + a.shape, a.dtype) for a in xs],
        scratch_shapes=[pltpu.SemaphoreType.DMA((7 * n,)), pltpu.SemaphoreType.DMA((7 * n,)), pltpu.SemaphoreType.DMA((n,))],
    )(*xs)


def _stage(ins, out_shape, n_peers, copy_of, n_arrays=None, local_of=None):
    ins = list(ins)
    n = len(ins) if n_arrays is None else n_arrays

    def copies(in_refs, out_refs, send_sems, recv_sems):
        place = _place()
        out = []
        for a in range(n):
            for k in range(n_peers):
                src, dst, peer = copy_of(in_refs[a], out_refs[a], k, place)
                out.append(pltpu.make_async_remote_copy(
                    src_ref=src, dst_ref=dst, send_sem=send_sems.at[n_peers * a + k], recv_sem=recv_sems.at[n_peers * a + k],
                    device_id=peer, device_id_type=MESH))
        if local_of is not None:
            for i, (src, dst) in enumerate(local_of(in_refs, out_refs, place)):
                out.append(pltpu.make_async_copy(src, dst, send_sems.at[n_peers * n + i]))
        return out

    return dict(ins=ins, out_shape=list(out_shape), sems=n_peers * n + (n if local_of is not None else 0), copies=copies)


def _other_chips(x, y):
    return [(1 - x, y), (x, 1 - y), (1 - x, 1 - y)]


def stage_gather_direct(blocks):
    def copy_of(src, dst, k, place):
        x, y, c = place
        peer = (x, y, 1 - c) if k == 0 else (*_other_chips(x, y)[k - 1], c)
        return src, dst.at[4 * x + 2 * y + c], peer

    return _stage(blocks, [SDS((N_DEV,) + b.shape, b.dtype) for b in blocks], 4, copy_of)


def stage_gather_forward(bufs, own):
    n = len(bufs)

    def copy_of(src, dst, k, place):
        x, y, c = place
        cx, cy = _other_chips(x, y)[k]
        slot = 4 * cx + 2 * cy + c
        return src.at[slot], dst.at[slot], (x, y, 1 - c)

    def local_of(in_refs, out_refs, place):
        x, y, c = place
        return [(in_refs[n + a], out_refs[a].at[4 * x + 2 * y + c]) for a in range(n)]

    st = _stage(list(bufs) + list(own), [SDS(b.shape, b.dtype) for b in bufs], 3, copy_of, n_arrays=n, local_of=local_of)
    st["alias"] = n
    return st


def stage_rs_sibling(g8s):
    def copy_of(src, dst, k, place):
        x, y, c = place
        return src.at[2 * k + (1 - c)], dst.at[k], (x, y, 1 - c)

    return _stage(g8s, [SDS((4,) + g.shape[1:], g.dtype) for g in g8s], 4, copy_of)


def stage_rs_chips(sums):
    def copy_of(src, dst, k, place):
        x, y, c = place
        chip = _other_chips(x, y)[k]
        return src.at[2 * chip[0] + chip[1]], dst.at[k], (*chip, c)

    return _stage(sums, [SDS((3,) + s.shape[1:], s.dtype) for s in sums], 3, copy_of)


def run_stage(stage, *, name):
    n_in, n_out = len(stage["ins"]), len(stage["out_shape"])

    def body(*refs):
        cps = stage["copies"](refs[:n_in], refs[n_in:n_in + n_out], refs[-2], refs[-1])
        for cp in cps:
            cp.start()
        for cp in cps:
            cp.wait()

    return pl.pallas_call(
        body, name=name, in_specs=[ANY] * n_in, out_specs=[ANY] * n_out, out_shape=stage["out_shape"],
        scratch_shapes=[pltpu.SemaphoreType.DMA((stage["sems"],)), pltpu.SemaphoreType.DMA((stage["sems"],))],
    )(*stage["ins"])


def _owner_major(full, axis):
    shp = full.shape
    r = full.reshape(shp[:axis] + (N_DEV, shp[axis] // N_DEV) + shp[axis + 1:])
    return jnp.moveaxis(r, axis, 0)


def _from_owner_major(g8, axis):
    r = jnp.moveaxis(g8, 0, axis)
    shp = r.shape
    return r.reshape(shp[:axis] + (shp[axis] * shp[axis + 1],) + shp[axis + 2:])


def _perm_w_in(w):
    z = jnp.zeros((w.shape[0], LANES), w.dtype)
    dt = jnp.pad(w[:, 2560:2576], ((0, 0), (0, LANES - SSD_HEADS)))
    kpe = jnp.pad(w[:, 3728:3760], ((0, 0), (PE_LO, LANES - PE_HI)))
    return jnp.concatenate([w[:, 0:1024], w[:, 1024:2560], w[:, 2576:3088], w[:, 3088:3472], z, w[:, 3472:3728], dt, kpe], axis=1)


def _unperm_w_in(g):
    return jnp.concatenate([g[:, Z0:Z0 + 1024], g[:, XBC0:XBC0 + 1536], g[:, DT0:DT0 + SSD_HEADS], g[:, U0:U0 + 512],
                            g[:, CQ0:CQ0 + 384], g[:, CKV0:CKV0 + 256], g[:, KPE0 + PE_LO:KPE0 + PE_HI]], axis=1)


def _perm_w_uq(w):
    return jnp.pad(w.reshape(MLA_QR, MLA_H, MLA_QK), ((0, 0), (0, 0), (0, LANES - MLA_QK))).reshape(MLA_QR, MLA_H * LANES)


def _unperm_w_uq(g):
    return g.reshape(MLA_QR, MLA_H, LANES)[:, :, :MLA_QK].reshape(MLA_QR, MLA_H * MLA_QK)


def _perm_w_ukv(w):
    w3 = w.reshape(MLA_KVR, MLA_H, MLA_NOPE + MLA_V)
    pad = ((0, 0), (0, 0), (0, LANES - MLA_NOPE))
    k = jnp.pad(w3[:, :, :MLA_NOPE], pad).reshape(MLA_KVR, MLA_H * LANES)
    v = jnp.pad(w3[:, :, MLA_NOPE:], pad).reshape(MLA_KVR, MLA_H * LANES)
    return jnp.concatenate([k, v], axis=1)


def _unperm_w_ukv(g):
    k = g[:, :MLA_H * LANES].reshape(MLA_KVR, MLA_H, LANES)[:, :, :MLA_NOPE]
    v = g[:, MLA_H * LANES:].reshape(MLA_KVR, MLA_H, LANES)[:, :, :MLA_V]
    return jnp.concatenate([k, v], axis=2).reshape(MLA_KVR, MLA_H * (MLA_NOPE + MLA_V))


def _perm_w_out(w):
    m = jnp.pad(w[SSD_W + POOL_W:].reshape(MLA_H, MLA_V, D), ((0, 0), (0, LANES - MLA_V), (0, 0))).reshape(MLA_H * LANES, D)
    return jnp.concatenate([w[:SSD_W + POOL_W], m], axis=0)


def _unperm_w_out(g):
    m = g[SSD_W + POOL_W:].reshape(MLA_H, LANES, D)[:, :MLA_V].reshape(MLA_H * MLA_V, D)
    return jnp.concatenate([g[:SSD_W + POOL_W], m], axis=0)


def _lane_pad(v):
    return jnp.pad(v.reshape(1, -1), ((0, 0), (0, LANES - v.shape[-1])))


SMALL = ("attn_norm", "ssd_conv_b", "ssd_dt_bias", "ssd_a_log", "ssd_d", "ssd_norm", "pool_w", "pool_scale",
         "mla_q_norm", "mla_kv_norm", "ffn_norm", "ffn_conv_b", "final_norm")
SHARDED = {"w_in": 2, "ssd_conv_w": 2, "mla_w_uq": 2, "mla_w_ukv": 2, "w_out": 1, "ffn_w_up": 2, "ffn_conv_w": 2,
           "ffn_w_down": 1}
ALL_W = ("attn_norm", "w_in", "ssd_conv_w", "ssd_conv_b", "ssd_dt_bias", "ssd_a_log", "ssd_d", "ssd_norm", "pool_w",
         "pool_scale", "mla_q_norm", "mla_w_uq", "mla_kv_norm", "mla_w_ukv", "w_out", "ffn_norm", "ffn_w_up",
         "ffn_conv_w", "ffn_conv_b", "ffn_w_down", "final_norm")


def _pack_small(d):
    rows, layout = [], []
    for k in SMALL:
        a = d[k].reshape(-1)
        n = a.shape[0]
        r = -(-n // LANES)
        rows.append(jnp.pad(a, (0, r * LANES - n)).reshape(r, LANES))
        layout.append((k, n, r, d[k].shape))
    buf = jnp.concatenate(rows, axis=0)
    pad = (-buf.shape[0]) % 8
    return jnp.pad(buf, ((0, pad), (0, 0))), layout


def _unpack_small(buf, layout):
    out, r0 = {}, 0
    for k, n, r, shp in layout:
        out[k] = buf[r0:r0 + r].reshape(-1)[:n].reshape(shp)
        r0 += r
    return out


_PERM = {"w_in": _perm_w_in, "mla_w_uq": _perm_w_uq, "mla_w_ukv": _perm_w_ukv, "w_out": _perm_w_out}
FIRST = ("w_in", "ssd_conv_w")
REST = tuple(k for k in SHARDED if k not in FIRST)


def _sharded_entries(keys, gathered):
    return {k: _PERM.get(k, lambda t: t)(_from_owner_major(g8, SHARDED[k] - 1)) for k, g8 in zip(keys, gathered)}


def _layer_fwd(l, x, W, cs, sn, Bl, next_blocks=None, pending=None):
    T = x.shape[0]
    S = T // Bl
    n = f"l{l}_"
    h = rms_fwd(x, W["attn_norm"], name=n + "attn_norm")
    own_direct = stage_gather_direct(pending) if pending is not None else None
    proj = mm(h, W["w_in"], comm=own_direct, name=n + "w_in")
    proj3 = proj.reshape(Bl, S, PW)
    xbc3 = conv_silu_fwd(proj3, W["ssd_conv_w"], W["ssd_conv_b"], name=n + "ssd_conv")
    own_forward = stage_gather_forward(own_direct["result"], pending) if own_direct else None
    y3, hin = ssd_fwd(xbc3, proj3, W["ssd_dt_bias"], W["ssd_a_log"], W["ssd_d"], comm=own_forward, name=n + "ssd_scan")
    if own_direct:
        W = {**W, **_sharded_entries(REST, own_forward["result"])}
    y = y3.reshape(T, SSD_W)
    y_ssd = gated_rms_fwd(y, proj, W["ssd_norm"], name=n + "ssd_gate_norm")
    y_pool = pool_fwd(proj3, W["pool_w"], W["pool_scale"], name=n + "pool").reshape(T, POOL_W)
    qn = rms_fwd(proj, W["mla_q_norm"], col0=CQ0, width=MLA_QR, name=n + "q_norm")
    kvn = rms_fwd(proj, W["mla_kv_norm"], col0=CKV0, width=MLA_KVR, name=n + "kv_norm")
    qt = mm(qn, W["mla_w_uq"], name=n + "w_uq")
    kvt = mm(kvn, W["mla_w_ukv"], name=n + "w_ukv")
    qr, kr, vr = mla_prep_fwd(qt, kvt, proj, cs, sn, name=n + "rope")
    direct = stage_gather_direct(next_blocks) if next_blocks is not None else None
    o, lse, lset = flash_fwd(qr, kr, vr, Bl, comm=direct, name=n + "attn")
    ycat = jnp.concatenate([y_ssd, y_pool, o], axis=1)
    x1 = mm(ycat, W["w_out"], add=x, name=n + "w_out")
    h2 = rms_fwd(x1, W["ffn_norm"], name=n + "ffn_norm")
    forward = stage_gather_forward(direct["result"], next_blocks) if direct else None
    pre = mm(h2, W["ffn_w_up"], comm=forward, name=n + "w_up")
    gathered = forward["result"] if direct else None
    pre3 = pre.reshape(Bl, S, 2 * DFF)
    act = ffn_act_fwd(pre3, W["ffn_conv_w"], W["ffn_conv_b"], name=n + "ffn_act").reshape(T, DFF)
    x2 = mm(act, W["ffn_w_down"], add=x1, name=n + "w_down")
    saved = dict(x=x, h=h, proj=proj, xbc3=xbc3, hin=hin, y=y, qn=qn, kvn=kvn, vr=vr, qr=qr, kr=kr, o=o, lse=lse, lset=lset,
                 ycat=ycat, x1=x1, h2=h2, pre3=pre3, act=act)
    return x2, saved, gathered, W


EARLY = ("ffn_w_up", "ffn_conv_w", "ffn_w_down", "w_out")
LATE = tuple(k for k in SHARDED if k not in EARLY)
_UNPERM = {"w_in": _unperm_w_in, "mla_w_uq": _unperm_w_uq, "mla_w_ukv": _unperm_w_ukv, "w_out": _unperm_w_out}


def _by_owner(g, keys):
    return [_owner_major(_UNPERM.get(k, lambda t: t)(g[k]), SHARDED[k] - 1) for k in keys]


def _chip_sums(g8s, from_sibling, tag):
    return [chip_sum(g8, r, name=f"{tag}{a}") for a, (g8, r) in enumerate(zip(g8s, from_sibling))]


def _layer_bwd(l, dx2, dx2b, W, sv, cs, sn, Bl, later_g8=None):
    T = dx2.shape[0]
    S = T // Bl
    n = f"l{l}_b_"
    g = {}
    g["ffn_w_down"] = mm(sv["act"], dx2b, ta=True, name=n + "dw_down")
    dact = mm(dx2b, W["ffn_w_down"], tb=True, name=n + "dact")
    to_sibling = stage_rs_sibling(later_g8) if later_g8 is not None else None
    dpg, dpv, dwg, dwv, dbg, dbv = ffn_act_bwd(sv["pre3"], W["ffn_conv_w"], W["ffn_conv_b"], dact.reshape(Bl, S, DFF),
                                               comm=to_sibling, name=n + "ffn_act")
    to_chips = sums = None
    if to_sibling:
        sums = _chip_sums(later_g8, to_sibling["result"], n + "rs_late_add")
        to_chips = stage_rs_chips([sb for _, sb in sums])
    g["ffn_conv_w"] = jnp.concatenate([dwg, dwv], axis=1)
    g["ffn_conv_b"] = jnp.concatenate([dbg, dbv], axis=1)
    dpg, dpv = dpg.reshape(T, DFF), dpv.reshape(T, DFF)
    g["ffn_w_up"] = jnp.concatenate([mm(sv["h2"], dpg, ta=True, name=n + "dw_up_g"),
                                     mm(sv["h2"], dpv, ta=True, name=n + "dw_up_v")], axis=1)
    dh2 = mm(dpg, W["ffn_w_up"], tb=True, name=n + "dh2_g")
    dh2 = mm(dpv, W["ffn_w_up"], tb=True, b_k0=DFF, add=dh2, name=n + "dh2_v")
    dx1, dx1b, g["ffn_norm"] = rms_bwd(sv["x1"], W["ffn_norm"], dh2, add=dx2, name=n + "ffn_norm")
    g["w_out"] = mm(sv["ycat"], dx1b, ta=True, name=n + "dw_out")
    dycat = mm(dx1b, W["w_out"], tb=True, out_dtype=BF16, name=n + "dycat")
    proj, proj3 = sv["proj"], sv["proj"].reshape(Bl, S, PW)
    dy, dz, g["ssd_norm"] = gated_rms_bwd(sv["y"], proj, W["ssd_norm"], dycat, name=n + "ssd_gate_norm")
    dxa, ddt, dpar, dd = ssd_bwd(sv["xbc3"], proj3, sv["hin"], dy.reshape(Bl, S, SSD_W), W["ssd_dt_bias"], W["ssd_a_log"],
                                 W["ssd_d"], comm=to_chips, name=n + "ssd_scan")
    reduced_late = ([s32 for s32, _ in sums], to_chips["result"]) if to_chips else None
    g["ssd_dt_bias"] = dpar[0, :SSD_HEADS]
    g["ssd_a_log"] = dpar[1, :SSD_HEADS]
    g["ssd_d"] = dd[:, :2].reshape(SSD_HEADS)
    early_g8 = _by_owner(g, EARLY)
    early_sibling = stage_rs_sibling(early_g8)
    dxbc, g["ssd_conv_w"], g["ssd_conv_b"] = conv_silu_bwd(proj3, W["ssd_conv_w"], W["ssd_conv_b"], dxa, comm=early_sibling,
                                                           name=n + "ssd_conv")
    du, g["pool_w"], g["pool_scale"] = pool_bwd(proj3, W["pool_w"], W["pool_scale"], dycat.reshape(Bl, S, YCAT), name=n + "pool")
    early_sums = _chip_sums(early_g8, early_sibling["result"], n + "rs_early_add")
    early_chips = stage_rs_chips([sb for _, sb in early_sums])
    dlt = attn_delta(sv["o"], dycat, Bl, name=n + "attn_delta")
    dqr, dkr, dv = flash_bwd(sv["qr"], sv["kr"], sv["vr"], sv["lset"], dlt, dycat, Bl, comm=early_chips, name=n + "attn_bwd")
    reduced_early = ([s32 for s32, _ in early_sums], early_chips["result"])
    dqt, dkt, dkpe = mla_prep_bwd(dqr, dkr, cs, sn, name=n + "rope")
    g["mla_w_ukv"] = jnp.concatenate([mm(sv["kvn"], dkt, ta=True, name=n + "dw_uk"),
                                      mm(sv["kvn"], dv, ta=True, name=n + "dw_uv")], axis=1)
    dkvn = mm(dkt, W["mla_w_ukv"], tb=True, name=n + "dkvn_k")
    dkvn = mm(dv, W["mla_w_ukv"], tb=True, b_k0=MLA_H * LANES, add=dkvn, name=n + "dkvn_v")
    g["mla_w_uq"] = mm(sv["qn"], dqt, ta=True, name=n + "dw_uq")
    dqn = mm(dqt, W["mla_w_uq"], tb=True, name=n + "dqn")
    dcq, g["mla_q_norm"] = rms_bwd(proj, W["mla_q_norm"], dqn, col0=CQ0, width=MLA_QR, name=n + "q_norm")
    dckv, g["mla_kv_norm"] = rms_bwd(proj, W["mla_kv_norm"], dkvn, col0=CKV0, width=MLA_KVR, name=n + "kv_norm")
    dproj = jnp.concatenate([dz, dxbc.reshape(T, SSD_XBC), du.reshape(T, POOL_W), dcq, jnp.zeros((T, LANES), BF16), dckv,
                             ddt.reshape(T, LANES), dkpe], axis=1)
    g["w_in"] = mm(sv["h"], dproj, ta=True, name=n + "dw_in")
    dh = mm(dproj, W["w_in"], tb=True, name=n + "dh")
    dx, dxb, g["attn_norm"] = rms_bwd(sv["x"], W["attn_norm"], dh, add=dx1, name=n + "attn_norm")
    return dx, dxb, g, reduced_late, reduced_early


def kernel(x, positions, attn_norm, w_in, ssd_conv_w, ssd_conv_b, ssd_dt_bias, ssd_a_log, ssd_d, ssd_norm, pool_w, pool_scale, mla_q_norm, mla_w_uq, mla_kv_norm, mla_w_ukv, w_out, ffn_norm, ffn_w_up, ffn_conv_w, ffn_conv_b, ffn_w_down, final_norm, loss_target, m_attn_norm, m_w_in, m_ssd_conv_w, m_ssd_conv_b, m_ssd_dt_bias, m_ssd_a_log, m_ssd_d, m_ssd_norm, m_pool_w, m_pool_scale, m_mla_q_norm, m_mla_w_uq, m_mla_kv_norm, m_mla_w_ukv, m_w_out, m_ffn_norm, m_ffn_w_up, m_ffn_conv_w, m_ffn_conv_b, m_ffn_w_down, m_final_norm, v_attn_norm, v_w_in, v_ssd_conv_w, v_ssd_conv_b, v_ssd_dt_bias, v_ssd_a_log, v_ssd_d, v_ssd_norm, v_pool_w, v_pool_scale, v_mla_q_norm, v_mla_w_uq, v_mla_kv_norm, v_mla_w_ukv, v_w_out, v_ffn_norm, v_ffn_w_up, v_ffn_conv_w, v_ffn_conv_b, v_ffn_w_down, v_final_norm):
    a = locals()
    Wt = {k: a[k] for k in ALL_W}
    Mo = {k: a["m_" + k] for k in ALL_W}
    Vo = {k: a["v_" + k] for k in ALL_W}
    Bl, S, _ = x.shape
    T = Bl * S

    names = list(SHARDED)
    conv = ("ssd_conv_w", "ffn_conv_w")

    def blocks_of(l, keys=names):
        return [Wt[k][l] if k in conv else Wt[k][l].astype(BF16) for k in keys]

    def replicated(l):
        return {
            "attn_norm": attn_norm[l].reshape(1, D), "ssd_conv_b": ssd_conv_b[l].reshape(1, SSD_XBC),
            "ssd_dt_bias": _lane_pad(ssd_dt_bias[l]), "ssd_a_log": _lane_pad(ssd_a_log[l]),
            "ssd_d": jnp.repeat(ssd_d[l].reshape(NPAIR, 2), SSD_P, axis=1), "ssd_norm": ssd_norm[l].reshape(1, SSD_W),
            "pool_w": pool_w[l].astype(BF16), "pool_scale": pool_scale[l].reshape(1, POOL_W),
            "mla_q_norm": mla_q_norm[l].reshape(1, MLA_QR), "mla_kv_norm": mla_kv_norm[l].reshape(1, MLA_KVR),
            "ffn_norm": ffn_norm[l].reshape(1, D), "ffn_conv_b": ffn_conv_b[l].reshape(1, 2 * DFF)}

    pos = positions.astype(F32).reshape(T, 1)
    inv_freq = ROPE_THETA ** (-jnp.arange(0, MLA_ROPE, 2, dtype=F32) / MLA_ROPE)
    invf = jnp.pad(jnp.concatenate([inv_freq, inv_freq]), (PE_LO, LANES - PE_HI)).reshape(1, LANES)
    cs, sn = rope_tables(pos, invf, name="rope_tables")

    first = all_gather_many(blocks_of(0, FIRST), name="gather_weights_l0")
    layers = [{**replicated(0), **_sharded_entries(FIRST, first)}]
    xc = x.reshape(T, D)
    saved = []
    for l in range(DEPTH):
        xc, sv, gathered, layers[l] = _layer_fwd(l, xc, layers[l], cs, sn, Bl,
                                                 next_blocks=blocks_of(l + 1) if l + 1 < DEPTH else None,
                                                 pending=blocks_of(0, REST) if l == 0 else None)
        saved.append(sv)
        if gathered is not None:
            layers.append({**replicated(l + 1), **_sharded_entries(names, gathered)})
    dx, dxb, g_final, loss_part = final_loss(xc, final_norm.reshape(1, D), loss_target.reshape(T, D), name="final_loss")

    grads = [None] * DEPTH
    reduced = {}

    def record(l, keys, red):
        for a, k in enumerate(keys):
            reduced[(l, k)] = (red[0][a], red[1][a])

    later_g8 = None
    for l in reversed(range(DEPTH)):
        dx, dxb, grads[l], red_late, red_early = _layer_bwd(l, dx, dxb, layers[l], saved[l], cs, sn, Bl, later_g8=later_g8)
        if red_late is not None:
            record(l + 1, LATE, red_late)
        record(l, EARLY, red_early)
        later_g8 = _by_owner(grads[l], LATE)
    loss = lax.psum(loss_part[0, 0], AXES)
    sums = _chip_sums(later_g8, run_stage(stage_rs_sibling(later_g8), name="rs_sibling_l0"), "rs_add_l0_")
    record(0, LATE, ([s32 for s32, _ in sums], run_stage(stage_rs_chips([sb for _, sb in sums]), name="rs_chips_l0")))

    out_g, out_d, out_m, out_v = {}, {}, {}, {}
    for k in names:
        outs = None
        for l in reversed(range(DEPTH)):
            s32, recv = reduced[(l, k)]
            outs = adamw_sharded(Wt[k], Mo[k], Vo[k], s32, recv, l, outs, name=f"adamw_l{l}_{k}")
        out_g[k], out_d[k], out_m[k], out_v[k] = (o.reshape(Wt[k].shape) for o in outs)

    part = {k: g_final.reshape(D) if k == "final_norm" else
            jnp.stack([grads[l][k].reshape(Wt[k].shape[1:]) for l in range(DEPTH)]) for k in SMALL}
    pg, layout = _pack_small(part)
    pw, _ = _pack_small(Wt)
    pm, _ = _pack_small(Mo)
    pv, _ = _pack_small(Vo)
    (pg8,) = all_gather_many([pg], name="gather_small_grads")
    sg, sd, sm, sv_ = adamw(pw, pm, pv, [pg8], name="adamw_small")
    for dst, buf in ((out_g, sg), (out_d, sd), (out_m, sm), (out_v, sv_)):
        dst.update(_unpack_small(buf, layout))

    return (loss, dx.reshape(Bl, S, D), *[out_g[k] for k in ALL_W], *[out_d[k] for k in ALL_W],
            *[out_m[k] for k in ALL_W], *[out_v[k] for k in ALL_W])
```

```python
import functools
import math

import jax
import jax.numpy as jnp
from jax import lax
from jax.experimental import pallas as pl
from jax.experimental.pallas import tpu as pltpu

F32, BF16 = jnp.float32, jnp.bfloat16
SDS = jax.ShapeDtypeStruct
MESH = pl.DeviceIdType.MESH
AXES = ("x", "y", "c")
N_DEV = 8

D = 1024
EPS = 1e-6
SSD_HEADS, SSD_P, SSD_W, SSD_G, SSD_N, SSD_K, SSD_L, SSD_XBC = 16, 64, 1024, 2, 128, 4, 128, 1536
POOL_G, POOL_D, POOL_W, POOL_WIN = 4, 128, 512, (2, 4, 8, 16)
MLA_H, MLA_QR, MLA_KVR, MLA_NOPE, MLA_ROPE, MLA_V, MLA_QK = 8, 384, 256, 64, 32, 64, 96
ROPE_THETA = 10000.0
MIX = 2048
DFF, FFN_K = 2816, 3
DEPTH = 2
ADAM_LR, ADAM_B1, ADAM_B2, ADAM_EPS, ADAM_WD, ADAM_STEP = 0.001, 0.9, 0.999, 1e-08, 0.01, 10

Z0, XBC0, U0, CQ0, CKV0, DT0, KPE0, PW = 0, 1024, 2560, 3072, 3584, 3840, 3968, 4096
LANES = 128
YCAT = SSD_W + POOL_W + MLA_H * LANES
NEG = -1e30
VMEM_LIMIT = 56 * 1024 * 1024
MM_ROW_TILE, MM_LANE_TILE, MM_FULL_K = 1024, 1408, 2816


def _tile(n, pref, mult):
    if n <= pref:
        return n
    for d in range(pref, 0, -mult):
        if d % mult == 0 and n % d == 0:
            return d
    return n


def _dg(a, b, ca, cb, prec=None):
    return lax.dot_general(a, b, (((ca,), (cb,)), ((), ())), preferred_element_type=F32, precision=prec)


def _nn(a, b):
    return _dg(a.astype(BF16), b.astype(BF16), 1, 0)


def _nt(a, b):
    return _dg(a.astype(BF16), b.astype(BF16), 1, 1)


def _tn(a, b):
    return _dg(a.astype(BF16), b.astype(BF16), 0, 0)


def _sig(x):
    return jax.nn.sigmoid(x)


def _silu(x):
    return x * _sig(x)


def _dsilu(x):
    s = _sig(x)
    return s * (1.0 + x * (1.0 - s))


ANY = pl.BlockSpec(memory_space=pl.ANY)


def _pc(body, *, name, grid, in_specs, out_specs, out_shape, scratch=(), comm=None):
    params = pltpu.CompilerParams(vmem_limit_bytes=VMEM_LIMIT)
    if comm is None:
        return pl.pallas_call(body, name=name, grid=grid, in_specs=in_specs, out_specs=out_specs, out_shape=out_shape,
                              scratch_shapes=list(scratch), compiler_params=params)
    single = not isinstance(out_shape, (list, tuple))
    o_specs = [out_specs] if single else list(out_specs)
    o_shape = [out_shape] if single else list(out_shape)
    ni, no, ns = len(in_specs), len(o_specs), len(scratch)
    nci, nco = len(comm["ins"]), len(comm["out_shape"])

    def fused(*refs):
        ins, cins = refs[:ni], refs[ni:ni + nci]
        outs, couts = refs[ni + nci:ni + nci + no], refs[ni + nci + no:ni + nci + no + nco]
        scr = refs[ni + nci + no + nco:ni + nci + no + nco + ns]
        send_sems, recv_sems = refs[-2:]
        copies = comm["copies"](cins, couts, send_sems, recv_sems)
        first = functools.reduce(jnp.logical_and, [pl.program_id(d) == 0 for d in range(len(grid))])
        last = functools.reduce(jnp.logical_and, [pl.program_id(d) == grid[d] - 1 for d in range(len(grid))])

        @pl.when(first)
        def _():
            for cp in copies:
                cp.start()

        body(*ins, *outs, *scr)

        @pl.when(last)
        def _():
            for cp in copies:
                cp.wait()

    call = pl.pallas_call(
        fused, name=name, grid=grid, in_specs=list(in_specs) + [ANY] * nci, out_specs=o_specs + [ANY] * nco,
        out_shape=o_shape + list(comm["out_shape"]),
        scratch_shapes=list(scratch) + [pltpu.SemaphoreType.DMA((comm["sems"],)), pltpu.SemaphoreType.DMA((comm["sems"],))],
        input_output_aliases={ni + a: no + a for a in range(comm.get("alias", 0))},
        compiler_params=params)

    def run(*args):
        res = call(*args, *comm["ins"])
        comm["result"] = list(res[no:])
        return res[0] if single else list(res[:no])

    return run


def _rsum(x):
    return jnp.sum(x, axis=1, keepdims=True)


def _csum(x):
    return jnp.sum(x, axis=0, keepdims=True)


def mm(a, b, *, ta=False, tb=False, add=None, out_dtype=F32, b_k0=0, comm=None, name):
    M, K = (a.shape[1], a.shape[0]) if ta else a.shape
    N = b.shape[0] if tb else b.shape[1]
    assert tb or b_k0 == 0
    tm = _tile(M, MM_LANE_TILE, LANES) if ta else _tile(M, MM_ROW_TILE, 8)
    tn = _tile(N, MM_LANE_TILE, LANES)
    if ta:
        tk = _tile(K, MM_ROW_TILE, 8)
    else:
        tk = K if K <= MM_FULL_K else _tile(K, 2048, LANES)
    nk = K // tk

    def body(*refs):
        if add is None:
            a_ref, b_ref, o_ref = refs[:3]
        else:
            a_ref, b_ref, add_ref, o_ref = refs[:4]
        part = _dg(a_ref[...].astype(BF16), b_ref[...].astype(BF16), 0 if ta else 1, 1 if tb else 0)

        def finish(r):
            if add is not None:
                r = r + add_ref[...].astype(F32)
            o_ref[...] = r.astype(out_dtype)

        if nk == 1:
            finish(part)
            return
        acc = refs[-1]
        k = pl.program_id(2)

        @pl.when(k == 0)
        def _():
            acc[...] = part

        @pl.when(k > 0)
        def _():
            acc[...] += part

        @pl.when(k == nk - 1)
        def _():
            finish(acc[...])

    a_spec = pl.BlockSpec((tk, tm), lambda i, j, k: (k, i)) if ta else pl.BlockSpec((tm, tk), lambda i, j, k: (i, k))
    assert b_k0 % tk == 0
    kb0 = b_k0 // tk
    b_spec = pl.BlockSpec((tn, tk), lambda i, j, k: (j, kb0 + k)) if tb else pl.BlockSpec((tk, tn), lambda i, j, k: (k, j))
    o_spec = pl.BlockSpec((tm, tn), lambda i, j, k: (i, j))
    ins, specs = [a, b], [a_spec, b_spec]
    if add is not None:
        ins.append(add)
        specs.append(o_spec)
    return _pc(body, name=name, grid=(M // tm, N // tn, nk), in_specs=specs, out_specs=o_spec,
               out_shape=SDS((M, N), out_dtype), scratch=[pltpu.VMEM((tm, tn), F32)] if nk > 1 else [], comm=comm)(*ins)


def rms_fwd(x, g, *, col0=0, width=None, name):
    T = x.shape[0]
    W = width or x.shape[1]
    tm = _tile(T, 512, 8)

    def body(x_ref, g_ref, o_ref):
        v = x_ref[...]
        r = lax.rsqrt(jnp.mean(v * v, axis=1, keepdims=True) + EPS)
        o_ref[...] = ((v * r) * g_ref[...]).astype(BF16)

    return _pc(body, name=name, grid=(T // tm,),
               in_specs=[pl.BlockSpec((tm, W), lambda i: (i, col0 // W)), pl.BlockSpec((1, W), lambda i: (0, 0))],
               out_specs=pl.BlockSpec((tm, W), lambda i: (i, 0)), out_shape=SDS((T, W), BF16))(x, g)


def rms_bwd(x, g, dh, *, col0=0, width=None, add=None, name):
    T = x.shape[0]
    W = width or x.shape[1]
    tm = _tile(T, 512, 8)

    def body(*refs):
        if add is None:
            x_ref, g_ref, dh_ref, dx_ref, dg_ref = refs
        else:
            x_ref, g_ref, dh_ref, add_ref, dx_ref, dxb_ref, dg_ref = refs
        v = x_ref[...]
        r = lax.rsqrt(jnp.mean(v * v, axis=1, keepdims=True) + EPS)
        xh = v * r
        d = dh_ref[...].astype(F32)
        dxh = d * g_ref[...]
        dx = r * (dxh - xh * jnp.mean(dxh * xh, axis=1, keepdims=True))
        if add is not None:
            dx = dx + add_ref[...]
            dxb_ref[...] = dx.astype(BF16)
        dx_ref[...] = dx.astype(dx_ref.dtype)

        @pl.when(pl.program_id(0) == 0)
        def _():
            dg_ref[...] = jnp.zeros_like(dg_ref)

        dg_ref[...] += _csum(d * xh)

    row = pl.BlockSpec((tm, W), lambda i: (i, 0))
    vec = pl.BlockSpec((1, W), lambda i: (0, 0))
    ins = [x, g, dh] + ([] if add is None else [add])
    specs = [pl.BlockSpec((tm, W), lambda i: (i, col0 // W)), vec, row] + ([] if add is None else [row])
    if add is None:
        return _pc(body, name=name, grid=(T // tm,), in_specs=specs, out_specs=[row, vec],
                   out_shape=[SDS((T, W), BF16), SDS((1, W), F32)])(*ins)
    return _pc(body, name=name, grid=(T // tm,), in_specs=specs, out_specs=[row, row, vec],
               out_shape=[SDS((T, W), F32), SDS((T, W), BF16), SDS((1, W), F32)])(*ins)


def gated_rms_fwd(y, proj, g, *, name):
    T = y.shape[0]
    tm = _tile(T, 512, 8)

    def body(y_ref, z_ref, g_ref, o_ref):
        v = y_ref[...] * _silu(z_ref[...])
        r = lax.rsqrt(jnp.mean(v * v, axis=1, keepdims=True) + EPS)
        o_ref[...] = ((v * r) * g_ref[...]).astype(BF16)

    row = pl.BlockSpec((tm, SSD_W), lambda i: (i, 0))
    return _pc(body, name=name, grid=(T // tm,), in_specs=[row, row, pl.BlockSpec((1, SSD_W), lambda i: (0, 0))],
               out_specs=row, out_shape=SDS((T, SSD_W), BF16))(y, proj, g)


def gated_rms_bwd(y, proj, g, dycat, *, name):
    T = y.shape[0]
    tm = _tile(T, 512, 8)

    def body(y_ref, z_ref, g_ref, d_ref, dy_ref, dz_ref, dg_ref):
        yv, z = y_ref[...], z_ref[...]
        sz = _silu(z)
        v = yv * sz
        r = lax.rsqrt(jnp.mean(v * v, axis=1, keepdims=True) + EPS)
        vh = v * r
        d = d_ref[...]
        dvh = d * g_ref[...]
        dv = r * (dvh - vh * jnp.mean(dvh * vh, axis=1, keepdims=True))
        dy_ref[...] = dv * sz
        dz_ref[...] = (dv * yv * _dsilu(z)).astype(BF16)

        @pl.when(pl.program_id(0) == 0)
        def _():
            dg_ref[...] = jnp.zeros_like(dg_ref)

        dg_ref[...] += _csum(d * vh)

    row = pl.BlockSpec((tm, SSD_W), lambda i: (i, 0))
    vec = pl.BlockSpec((1, SSD_W), lambda i: (0, 0))
    return _pc(body, name=name, grid=(T // tm,), in_specs=[row, row, vec, row], out_specs=[row, row, vec],
               out_shape=[SDS((T, SSD_W), F32), SDS((T, SSD_W), BF16), SDS((1, SSD_W), F32)])(y, proj, g, dycat)


def final_loss(x, g, tgt, *, name):
    T = x.shape[0]
    tm = _tile(T, 512, 8)

    def body(x_ref, g_ref, t_ref, dx_ref, dxb_ref, dg_ref, l_ref):
        v = x_ref[...]
        gg = g_ref[...]
        r = lax.rsqrt(jnp.mean(v * v, axis=1, keepdims=True) + EPS)
        xh = v * r
        err = xh * gg - t_ref[...]
        part = 0.5 * _csum(jnp.mean(err * err, axis=1, keepdims=True))
        d = err * (1.0 / D)
        dxh = d * gg
        dx = r * (dxh - xh * jnp.mean(dxh * xh, axis=1, keepdims=True))
        dx_ref[...] = dx
        dxb_ref[...] = dx.astype(BF16)

        @pl.when(pl.program_id(0) == 0)
        def _():
            dg_ref[...] = jnp.zeros_like(dg_ref)
            l_ref[...] = jnp.zeros_like(l_ref)

        dg_ref[...] += _csum(d * xh)
        l_ref[...] += jnp.broadcast_to(part, (1, LANES))

    row = pl.BlockSpec((tm, D), lambda i: (i, 0))
    vec = pl.BlockSpec((1, D), lambda i: (0, 0))
    return _pc(body, name=name, grid=(T // tm,), in_specs=[row, vec, row],
               out_specs=[row, row, vec, pl.BlockSpec((1, LANES), lambda i: (0, 0))],
               out_shape=[SDS((T, D), F32), SDS((T, D), BF16), SDS((1, D), F32), SDS((1, LANES), F32)])(x, g, tgt)


HALO = 8


def _prev_map(ts, col):
    return lambda b, i, j: (b, jnp.maximum(i * (ts // HALO) - 1, 0), col(j))


def _next_map(ts, n_halo_blocks, col):
    return lambda b, i, j: (b, jnp.minimum((i + 1) * (ts // HALO), n_halo_blocks - 1), col(j))


def _row_chunks(ts, rows):
    rows = min(rows, ts)
    return [(r, rows) for r in range(0, ts, rows)]


def _conv_rows(ext, w_ref, b_ref, r0, n, K):
    win = ext[r0:r0 + HALO + n, :]
    taps = [pltpu.roll(win, K - 1 - k, 0)[HALO:HALO + n] if k < K - 1 else win[HALO:HALO + n] for k in range(K)]
    acc = b_ref[...] + w_ref[0:1, :] * taps[0]
    for k in range(1, K):
        acc = acc + w_ref[k:k + 1, :] * taps[k]
    return acc, taps


def _conv_t_rows(ext2, w_ref, r0, n, K):
    win = ext2[r0:r0 + n + HALO, :]
    dx = w_ref[K - 1:K, :] * win[0:n]
    for k in range(K - 1):
        dx = dx + w_ref[k:k + 1, :] * pltpu.roll(win, n + HALO - (K - 1 - k), 0)[0:n]
    return dx


def _sum8(x):
    acc = x[0:8]
    for r in range(8, x.shape[0], 8):
        acc = acc + x[r:r + 8]
    return acc


def conv_silu_fwd(proj3, w, b, *, name):
    Bl, S, _ = proj3.shape
    C, K = SSD_XBC, SSD_K
    ts, tc = _tile(S, 512, 8), 512
    c0 = XBC0 // tc

    def body(xp_ref, x_ref, w_ref, b_ref, o_ref, ext):
        i = pl.program_id(1)
        ext[0:HALO, :] = jnp.where(i > 0, xp_ref[0], 0.0)
        ext[HALO:HALO + ts, :] = x_ref[0]
        for r0, n in _row_chunks(ts, 32):
            acc, _ = _conv_rows(ext, w_ref, b_ref, r0, n, K)
            o_ref[0, r0:r0 + n, :] = _silu(acc)

    return _pc(body, name=name, grid=(Bl, S // ts, C // tc),
               in_specs=[pl.BlockSpec((1, HALO, tc), _prev_map(ts, lambda j: c0 + j)),
                         pl.BlockSpec((1, ts, tc), lambda b, i, j: (b, i, c0 + j)),
                         pl.BlockSpec((K, tc), lambda b, i, j: (0, j)),
                         pl.BlockSpec((1, tc), lambda b, i, j: (0, j))],
               out_specs=pl.BlockSpec((1, ts, tc), lambda b, i, j: (b, i, j)),
               out_shape=SDS((Bl, S, C), F32), scratch=[pltpu.VMEM((HALO + ts, tc), F32)])(proj3, proj3, w, b)


def conv_silu_bwd(proj3, w, b, dact, *, comm=None, name):
    Bl, S, _ = proj3.shape
    C, K = SSD_XBC, SSD_K
    ts, tc = _tile(S, 512, 8), 512
    c0 = XBC0 // tc
    ns = S // ts

    def body(xp_ref, x_ref, xn_ref, d_ref, dn_ref, w_ref, b_ref, dx_ref, dw_ref, db_ref, ext, ext2):
        bb, i = pl.program_id(1), pl.program_id(2)
        last = i == ns - 1
        ext[0:HALO, :] = jnp.where(i > 0, xp_ref[0], 0.0)
        ext[HALO:HALO + ts, :] = x_ref[0]
        ext[HALO + ts:2 * HALO + ts, :] = jnp.where(last, 0.0, xn_ref[0])
        dw = [jnp.zeros((8, tc), F32) for _ in range(K)]
        db = jnp.zeros((8, tc), F32)
        for r0, n in _row_chunks(ts, 16) + [(ts, HALO)]:
            acc, taps = _conv_rows(ext, w_ref, b_ref, r0, n, K)
            d = d_ref[0, r0:r0 + n, :] if r0 < ts else jnp.where(last, 0.0, dn_ref[0])
            du = d * _dsilu(acc)
            ext2[r0:r0 + n, :] = du
            if r0 < ts:
                dw = [a + _sum8(du * t) for a, t in zip(dw, taps)]
                db = db + _sum8(du)
        for r0, n in _row_chunks(ts, 32):
            dx_ref[0, r0:r0 + n, :] = _conv_t_rows(ext2, w_ref, r0, n, K).astype(BF16)

        @pl.when((bb == 0) & (i == 0))
        def _():
            dw_ref[...] = jnp.zeros_like(dw_ref)
            db_ref[...] = jnp.zeros_like(db_ref)

        for k in range(K):
            dw_ref[k:k + 1, :] += _csum(dw[k])
        db_ref[...] += _csum(db)

    nhb = S // HALO
    cx = lambda j: c0 + j
    cj = lambda j: j
    return _pc(body, name=name, grid=(C // tc, Bl, ns),
               in_specs=[pl.BlockSpec((1, HALO, tc), lambda j, b, i: _prev_map(ts, cx)(b, i, j)),
                         pl.BlockSpec((1, ts, tc), lambda j, b, i: (b, i, c0 + j)),
                         pl.BlockSpec((1, HALO, tc), lambda j, b, i: _next_map(ts, nhb, cx)(b, i, j)),
                         pl.BlockSpec((1, ts, tc), lambda j, b, i: (b, i, j)),
                         pl.BlockSpec((1, HALO, tc), lambda j, b, i: _next_map(ts, nhb, cj)(b, i, j)),
                         pl.BlockSpec((K, tc), lambda j, b, i: (0, j)),
                         pl.BlockSpec((1, tc), lambda j, b, i: (0, j))],
               out_specs=[pl.BlockSpec((1, ts, tc), lambda j, b, i: (b, i, j)),
                          pl.BlockSpec((K, tc), lambda j, b, i: (0, j)),
                          pl.BlockSpec((1, tc), lambda j, b, i: (0, j))],
               out_shape=[SDS((Bl, S, C), BF16), SDS((K, C), F32), SDS((1, C), F32)],
               scratch=[pltpu.VMEM((2 * HALO + ts, tc), F32), pltpu.VMEM((HALO + ts, tc), F32)],
               comm=comm)(proj3, proj3, proj3, dact, dact, w, b)


def ffn_act_fwd(pre3, w, b, *, name):
    Bl, S, _ = pre3.shape
    K = FFN_K
    ts, tc = _tile(S, 512, 8), 256
    nj = DFF // tc

    def body(gp_ref, g_ref, vp_ref, v_ref, wg_ref, wv_ref, bg_ref, bv_ref, o_ref, eg, ev):
        i = pl.program_id(1)
        for p_ref, m_ref, ext in ((gp_ref, g_ref, eg), (vp_ref, v_ref, ev)):
            ext[0:HALO, :] = jnp.where(i > 0, p_ref[0], 0.0)
            ext[HALO:HALO + ts, :] = m_ref[0]
        for r0, n in _row_chunks(ts, 64):
            ug, _ = _conv_rows(eg, wg_ref, bg_ref, r0, n, K)
            uv, _ = _conv_rows(ev, wv_ref, bv_ref, r0, n, K)
            o_ref[0, r0:r0 + n, :] = (_silu(ug) * uv).astype(BF16)

    main = lambda off: pl.BlockSpec((1, ts, tc), lambda b, i, j: (b, i, off + j))
    prev = lambda off: pl.BlockSpec((1, HALO, tc), _prev_map(ts, lambda j: off + j))
    wsp = lambda off: pl.BlockSpec((K, tc), lambda b, i, j: (0, off + j))
    bsp = lambda off: pl.BlockSpec((1, tc), lambda b, i, j: (0, off + j))
    return _pc(body, name=name, grid=(Bl, S // ts, nj),
               in_specs=[prev(0), main(0), prev(nj), main(nj), wsp(0), wsp(nj), bsp(0), bsp(nj)],
               out_specs=pl.BlockSpec((1, ts, tc), lambda b, i, j: (b, i, j)),
               out_shape=SDS((Bl, S, DFF), BF16),
               scratch=[pltpu.VMEM((HALO + ts, tc), F32), pltpu.VMEM((HALO + ts, tc), F32)],
               )(pre3, pre3, pre3, pre3, w, w, b, b)


def ffn_act_bwd(pre3, w, b, dact, *, comm=None, name):
    Bl, S, _ = pre3.shape
    K = FFN_K
    ts, tc = _tile(S, 512, 8), 256
    nj = DFF // tc
    ns = S // ts

    def body(gp_ref, g_ref, gn_ref, vp_ref, v_ref, vn_ref, d_ref, dn_ref, wg_ref, wv_ref, bg_ref, bv_ref,
             dg_ref, dv_ref, dwg_ref, dwv_ref, dbg_ref, dbv_ref, eg, ev, e2g, e2v):
        bb, i = pl.program_id(1), pl.program_id(2)
        last = i == ns - 1
        for p_ref, m_ref, n_ref, ext in ((gp_ref, g_ref, gn_ref, eg), (vp_ref, v_ref, vn_ref, ev)):
            ext[0:HALO, :] = jnp.where(i > 0, p_ref[0], 0.0)
            ext[HALO:HALO + ts, :] = m_ref[0]
            ext[HALO + ts:2 * HALO + ts, :] = jnp.where(last, 0.0, n_ref[0])
        zero8 = jnp.zeros((8, tc), F32)
        dwg, dwv, dbg, dbv = [zero8] * K, [zero8] * K, zero8, zero8
        for r0, n in _row_chunks(ts, 32) + [(ts, HALO)]:
            ug, tg = _conv_rows(eg, wg_ref, bg_ref, r0, n, K)
            uv, tv = _conv_rows(ev, wv_ref, bv_ref, r0, n, K)
            d = d_ref[0, r0:r0 + n, :] if r0 < ts else jnp.where(last, 0.0, dn_ref[0])
            dug = d * uv * _dsilu(ug)
            duv = d * _silu(ug)
            e2g[r0:r0 + n, :] = dug
            e2v[r0:r0 + n, :] = duv
            if r0 < ts:
                dwg = [a + _sum8(dug * t) for a, t in zip(dwg, tg)]
                dwv = [a + _sum8(duv * t) for a, t in zip(dwv, tv)]
                dbg, dbv = dbg + _sum8(dug), dbv + _sum8(duv)
        for w_ref, e2, o_ref in ((wg_ref, e2g, dg_ref), (wv_ref, e2v, dv_ref)):
            for r0, n in _row_chunks(ts, 64):
                o_ref[0, r0:r0 + n, :] = _conv_t_rows(e2, w_ref, r0, n, K).astype(BF16)

        @pl.when((bb == 0) & (i == 0))
        def _():
            for r in (dwg_ref, dwv_ref, dbg_ref, dbv_ref):
                r[...] = jnp.zeros_like(r)

        for dw_ref, dw, db_ref, db in ((dwg_ref, dwg, dbg_ref, dbg), (dwv_ref, dwv, dbv_ref, dbv)):
            for k in range(K):
                dw_ref[k:k + 1, :] += _csum(dw[k])
            db_ref[...] += _csum(db)

    nhb = S // HALO
    main = lambda off: pl.BlockSpec((1, ts, tc), lambda j, b, i: (b, i, off + j))
    prev = lambda off: pl.BlockSpec((1, HALO, tc), lambda j, b, i: _prev_map(ts, lambda jj: off + jj)(b, i, j))
    nxt = lambda off: pl.BlockSpec((1, HALO, tc), lambda j, b, i: _next_map(ts, nhb, lambda jj: off + jj)(b, i, j))
    wsp = lambda off: pl.BlockSpec((K, tc), lambda j, b, i: (0, off + j))
    bsp = lambda off: pl.BlockSpec((1, tc), lambda j, b, i: (0, off + j))
    outs = _pc(body, name=name, grid=(nj, Bl, ns),
               in_specs=[prev(0), main(0), nxt(0), prev(nj), main(nj), nxt(nj), main(0), nxt(0),
                         wsp(0), wsp(nj), bsp(0), bsp(nj)],
               out_specs=[main(0), main(0), wsp(0), wsp(0), bsp(0), bsp(0)],
               out_shape=[SDS((Bl, S, DFF), BF16), SDS((Bl, S, DFF), BF16), SDS((K, DFF), F32), SDS((K, DFF), F32),
                          SDS((1, DFF), F32), SDS((1, DFF), F32)],
               scratch=[pltpu.VMEM((2 * HALO + ts, tc), F32), pltpu.VMEM((2 * HALO + ts, tc), F32),
                        pltpu.VMEM((HALO + ts, tc), F32), pltpu.VMEM((HALO + ts, tc), F32)],
               comm=comm)(pre3, pre3, pre3, pre3, pre3, pre3, dact, dact, w, w, b, b)
    return outs


PHALO = 16


def _pool_window_sums(win, trailing):
    rows = win.shape[0]
    out, s = [], win
    for w in POOL_WIN:
        half = w // 2
        s = s + pltpu.roll(s, half if trailing else rows - half, 0)
        out.append(s)
    return out


def _pick(g, vals):
    r = vals[-1]
    for k in range(len(vals) - 2, -1, -1):
        r = jnp.where(g == k, vals[k], r)
    return r


def _pool_count(g, i, ts, rows, r0=0):
    t = (i * ts + r0 + lax.broadcasted_iota(jnp.int32, (rows, 1), 0) + 1).astype(F32)
    return jnp.minimum(t, _pick(g, [float(w) for w in POOL_WIN]))


def _fill_pool_ext(up_ref, u_ref, ext, i, ts):
    ext[0:PHALO, :] = jnp.where(i > 0, up_ref[0], 0.0)
    ext[PHALO:PHALO + ts, :] = u_ref[0]


def _pooled_rows(ext, g, i, ts, r0, n):
    win = ext[r0:r0 + n + PHALO, :]
    sums = _pool_window_sums(win, True)
    return _pick(g, sums)[PHALO:PHALO + n] / _pool_count(g, i, ts, n, r0) - win[PHALO:PHALO + n]


def pool_fwd(proj3, pool_w, scale, *, name):
    Bl, S, _ = proj3.shape
    ts = _tile(S, 512, 16)
    c0 = U0 // POOL_D

    def body(up_ref, u_ref, w_ref, s_ref, o_ref, ext):
        i, g = pl.program_id(1), pl.program_id(2)
        _fill_pool_ext(up_ref, u_ref, ext, i, ts)
        wm, sc = w_ref[0], s_ref[...]
        for r0, n in _row_chunks(ts, 128):
            o_ref[0, r0:r0 + n, :] = (_nn(_pooled_rows(ext, g, i, ts, r0, n), wm) * sc).astype(BF16)

    return _pc(body, name=name, grid=(Bl, S // ts, POOL_G),
               in_specs=[pl.BlockSpec((1, PHALO, POOL_D), lambda b, i, g: (b, jnp.maximum(i * (ts // PHALO) - 1, 0), c0 + g)),
                         pl.BlockSpec((1, ts, POOL_D), lambda b, i, g: (b, i, c0 + g)),
                         pl.BlockSpec((1, POOL_D, POOL_D), lambda b, i, g: (g, 0, 0)),
                         pl.BlockSpec((1, POOL_D), lambda b, i, g: (0, g))],
               out_specs=pl.BlockSpec((1, ts, POOL_D), lambda b, i, g: (b, i, g)),
               out_shape=SDS((Bl, S, POOL_W), BF16), scratch=[pltpu.VMEM((PHALO + ts, POOL_D), F32)],
               )(proj3, proj3, pool_w, scale)


def pool_bwd(proj3, pool_w, scale, dycat3, *, name):
    Bl, S, _ = proj3.shape
    ts = _tile(S, 512, 16)
    ns = S // ts
    c0 = U0 // POOL_D
    d0 = SSD_W // POOL_D
    nhb = S // PHALO

    def body(up_ref, u_ref, d_ref, dn_ref, w_ref, s_ref, du_ref, dw_ref, ds_ref, ext, ext2):
        g, bb, i = pl.program_id(0), pl.program_id(1), pl.program_id(2)
        last = i == ns - 1
        _fill_pool_ext(up_ref, u_ref, ext, i, ts)
        wm = w_ref[0]
        sc = s_ref[...]
        dwa = jnp.zeros((POOL_D, POOL_D), F32)
        dsa = jnp.zeros((8, POOL_D), F32)
        dpools = []
        for r0, n in _row_chunks(ts, 128):
            pooled = _pooled_rows(ext, g, i, ts, r0, n)
            dy = d_ref[0, r0:r0 + n, :]
            dp = dy * sc
            dpool = _nt(dp, wm)
            dpools.append(dpool)
            ext2[r0:r0 + n, :] = dpool / _pool_count(g, i, ts, n, r0)
            dwa = dwa + _tn(pooled, dp)
            dsa = dsa + _sum8(dy * _nn(pooled, wm))
        dpool_n = _nt(jnp.where(last, 0.0, dn_ref[0]) * sc, wm)
        ext2[ts:ts + PHALO, :] = dpool_n / _pool_count(g, i + 1, ts, PHALO)
        for (r0, n), dpool in zip(_row_chunks(ts, 128), dpools):
            sums = _pool_window_sums(ext2[r0:r0 + n + PHALO, :], False)
            du_ref[0, r0:r0 + n, :] = (_pick(g, sums)[0:n] - dpool).astype(BF16)

        @pl.when((bb == 0) & (i == 0))
        def _():
            dw_ref[...] = jnp.zeros_like(dw_ref)
            ds_ref[...] = jnp.zeros_like(ds_ref)

        dw_ref[0] += dwa
        ds_ref[...] += _csum(dsa)

    return _pc(body, name=name, grid=(POOL_G, Bl, ns),
               in_specs=[pl.BlockSpec((1, PHALO, POOL_D), lambda g, b, i: (b, jnp.maximum(i * (ts // PHALO) - 1, 0), c0 + g)),
                         pl.BlockSpec((1, ts, POOL_D), lambda g, b, i: (b, i, c0 + g)),
                         pl.BlockSpec((1, ts, POOL_D), lambda g, b, i: (b, i, d0 + g)),
                         pl.BlockSpec((1, PHALO, POOL_D), lambda g, b, i: (b, jnp.minimum((i + 1) * (ts // PHALO), nhb - 1), d0 + g)),
                         pl.BlockSpec((1, POOL_D, POOL_D), lambda g, b, i: (g, 0, 0)),
                         pl.BlockSpec((1, POOL_D), lambda g, b, i: (0, g))],
               out_specs=[pl.BlockSpec((1, ts, POOL_D), lambda g, b, i: (b, i, g)),
                          pl.BlockSpec((1, POOL_D, POOL_D), lambda g, b, i: (g, 0, 0)),
                          pl.BlockSpec((1, POOL_D), lambda g, b, i: (0, g))],
               out_shape=[SDS((Bl, S, POOL_W), BF16), SDS((POOL_G, POOL_D, POOL_D), F32), SDS((1, POOL_W), F32)],
               scratch=[pltpu.VMEM((PHALO + ts, POOL_D), F32), pltpu.VMEM((PHALO + ts, POOL_D), F32)],
               )(proj3, proj3, dycat3, dycat3, pool_w, scale)


NPAIR = SSD_HEADS // 2


def _ssd_common(sm, bias, alog):
    L = SSD_L
    dt = jax.nn.softplus(sm + bias)
    a = -jnp.exp(alog)
    da = dt * a
    r = lax.broadcasted_iota(jnp.int32, (L, L), 0)
    c = lax.broadcasted_iota(jnp.int32, (L, L), 1)
    tri = (r >= c).astype(F32)
    cum = _dg(tri, da, 1, 0, lax.Precision.HIGHEST)
    return dt, a, cum, cum.T, r >= c


def _lanes(lo, hi, shape=(1, LANES)):
    lane = lax.broadcasted_iota(jnp.int32, shape, len(shape) - 1)
    return (lane >= lo) & (lane < hi)


def _onehot_lane(h):
    return (lax.broadcasted_iota(jnp.int32, (1, LANES), 1) == h).astype(F32)


def _split_nn(a, e):
    hi = a.astype(BF16)
    lo = (a - hi.astype(F32)).astype(BF16)
    return _dg(hi, e, 1, 0) + _dg(lo, e, 1, 0)


def _head_spread():
    r = lax.broadcasted_iota(jnp.int32, (LANES, SSD_W), 0)
    c = lax.broadcasted_iota(jnp.int32, (LANES, SSD_W), 1)
    return (c // SSD_P == r).astype(BF16)


def _pair_gather(j):
    r = lax.broadcasted_iota(jnp.int32, (LANES, LANES), 0)
    c = lax.broadcasted_iota(jnp.int32, (LANES, LANES), 1)
    return (c == 2 * j + (r >= SSD_P).astype(jnp.int32)).astype(BF16)


def ssd_fwd(xbc3, proj3, bias, alog, dskip, *, comm=None, name):
    Bl, S, _ = xbc3.shape
    L = SSD_L
    nc = S // L

    def body(xbc_ref, sm_ref, bias_ref, alog_ref, d_ref, y_ref, hin_ref, H):
        c = pl.program_id(1)

        @pl.when(c == 0)
        def _():
            H[...] = jnp.zeros_like(H)

        dt, a, cum, cumT, mask = _ssd_common(sm_ref[0], bias_ref[...], alog_ref[...])
        lo = _lanes(0, SSD_P)
        rowlo = lax.broadcasted_iota(jnp.int32, (LANES, LANES), 0) < SSD_P
        spread = _head_spread()
        dt_x = _split_nn(dt, spread)
        el_x = _split_nn(jnp.exp(cum), spread)
        wl_x = _split_nn(jnp.exp(cum[L - 1:L, :] - cum), spread)
        cb = []
        for g in range(SSD_G):
            Bg = xbc_ref[0, :, SSD_W + g * SSD_N:SSD_W + (g + 1) * SSD_N]
            Cg = xbc_ref[0, :, SSD_W + SSD_G * SSD_N + g * SSD_N:SSD_W + SSD_G * SSD_N + (g + 1) * SSD_N]
            cb.append((Bg, Cg, _nt(Cg, Bg)))
        for j in range(NPAIR):
            h0, h1 = 2 * j, 2 * j + 1
            sl = slice(j * LANES, (j + 1) * LANES)
            Bg, Cg, CB = cb[j // (NPAIR // SSD_G)]
            X = xbc_ref[0, :, sl]
            c0, c1 = cum[:, h0:h0 + 1], cum[:, h1:h1 + 1]
            r0, r1 = cumT[h0:h0 + 1, :], cumT[h1:h1 + 1, :]
            cl0, cl1 = cum[L - 1:L, h0:h0 + 1], cum[L - 1:L, h1:h1 + 1]
            Xt = X * dt_x[:, sl]
            M0 = CB * jnp.exp(jnp.where(mask, c0 - r0, NEG))
            M1 = CB * jnp.exp(jnp.where(mask, c1 - r1, NEG))
            Yd = jnp.where(lo, _nn(M0, Xt), _nn(M1, Xt))
            Hp = H[j]
            hin_ref[0, 0, j] = Hp
            Z = _nt(Cg, Hp)
            y_ref[0, :, sl] = Yd + el_x[:, sl] * Z + X * d_ref[j:j + 1, :]
            H[j] = jnp.where(rowlo, jnp.exp(cl0), jnp.exp(cl1)) * Hp + _tn(wl_x[:, sl] * Xt, Bg)

    vec = pl.BlockSpec((1, LANES), lambda b, c: (0, 0))
    return _pc(body, name=name, grid=(Bl, nc),
               in_specs=[pl.BlockSpec((1, L, SSD_XBC), lambda b, c: (b, c, 0)),
                         pl.BlockSpec((1, L, LANES), lambda b, c: (b, c, DT0 // LANES)),
                         vec, vec, pl.BlockSpec((NPAIR, LANES), lambda b, c: (0, 0))],
               out_specs=[pl.BlockSpec((1, L, SSD_W), lambda b, c: (b, c, 0)),
                          pl.BlockSpec((1, 1, NPAIR, LANES, LANES), lambda b, c: (b, c, 0, 0, 0))],
               out_shape=[SDS((Bl, S, SSD_W), F32), SDS((Bl, nc, NPAIR, LANES, LANES), F32)],
               scratch=[pltpu.VMEM((NPAIR, LANES, LANES), F32)], comm=comm)(xbc3, proj3, bias, alog, dskip)


def ssd_bwd(xbc3, proj3, hin, dy3, bias, alog, dskip, *, comm=None, name):
    Bl, S, _ = xbc3.shape
    L = SSD_L
    nc = S // L

    def body(xbc_ref, sm_ref, hin_ref, dy_ref, bias_ref, alog_ref, d_ref, dx_ref, ddt_ref, dpar_ref, dd_ref, dH, ddacc):
        bb, i = pl.program_id(0), pl.program_id(1)

        @pl.when(i == 0)
        def _():
            dH[...] = jnp.zeros_like(dH)

        @pl.when((bb == 0) & (i == 0))
        def _():
            dpar_ref[...] = jnp.zeros_like(dpar_ref)
            ddacc[...] = jnp.zeros_like(ddacc)

        sm = sm_ref[0]
        dt, a, cum, cumT, mask = _ssd_common(sm, bias_ref[...], alog_ref[...])
        maskf = mask.astype(F32)
        lo = _lanes(0, SSD_P)
        rowlo = lax.broadcasted_iota(jnp.int32, (LANES, LANES), 0) < SSD_P
        lastrow = (lax.broadcasted_iota(jnp.int32, (L, 1), 0) == L - 1).astype(F32)
        dcum = jnp.zeros((L, LANES), F32)
        dcum_t = jnp.zeros((LANES, L), F32)
        ddt = jnp.zeros((L, LANES), F32)
        spread = _head_spread()
        ones = jnp.ones((L, LANES), BF16)
        ecum = jnp.exp(cum)
        wall = jnp.exp(cum[L - 1:L, :] - cum)
        dt_x = _split_nn(dt, spread)
        el_x = _split_nn(ecum, spread)
        wl_x = _split_nn(wall, spread)
        headrow = lax.broadcasted_iota(jnp.int32, (LANES, 1), 0)
        grp = []
        for g in range(SSD_G):
            Bg = xbc_ref[0, :, SSD_W + g * SSD_N:SSD_W + (g + 1) * SSD_N]
            Cg = xbc_ref[0, :, SSD_W + SSD_G * SSD_N + g * SSD_N:SSD_W + SSD_G * SSD_N + (g + 1) * SSD_N]
            grp.append(dict(B=Bg, C=Cg, CB=_nt(Cg, Bg), dB=jnp.zeros((L, SSD_N), F32), dC=jnp.zeros((L, SSD_N), F32),
                            dCB=jnp.zeros((L, L), F32)))
        for j in range(NPAIR):
            h0, h1 = 2 * j, 2 * j + 1
            sl = slice(j * LANES, (j + 1) * LANES)
            G = grp[j // (NPAIR // SSD_G)]
            Bg, Cg, CB = G["B"], G["C"], G["CB"]
            X = xbc_ref[0, :, sl]
            dY = dy_ref[0, :, sl]
            c0, c1 = cum[:, h0:h0 + 1], cum[:, h1:h1 + 1]
            r0, r1 = cumT[h0:h0 + 1, :], cumT[h1:h1 + 1, :]
            cl0, cl1 = cum[L - 1:L, h0:h0 + 1], cum[L - 1:L, h1:h1 + 1]
            oh0, oh1 = _onehot_lane(h0), _onehot_lane(h1)
            gather = _pair_gather(j)
            dtl, el, wl = dt_x[:, sl], el_x[:, sl], wl_x[:, sl]
            Xt = X * dtl
            Hp = hin_ref[0, 0, j]
            dS = dH[j]
            dX = dY * d_ref[j:j + 1, :]
            ddacc[j:j + 1, :] += _csum(dY * X)
            Z = _nt(Cg, Hp)
            dZ = dY * el
            dcum = dcum + _split_nn(dY * Z, gather) * ecum
            G["dC"] = G["dC"] + _nn(dZ, Hp)
            dHy = _tn(dZ, Cg)
            Gm = _nt(Bg, dS)
            dXt = wl * Gm
            q = _split_nn(Xt * Gm, gather) * wall
            dcum = dcum + lastrow * _csum(q) - q
            G["dB"] = G["dB"] + _nn(wl * Xt, dS)
            g0, g1 = jnp.exp(cl0), jnp.exp(cl1)
            rowsum = _nn(dS * Hp, ones)
            dg0 = _csum(jnp.where(rowlo, rowsum, 0.0))
            dg1 = _csum(jnp.where(rowlo, 0.0, rowsum))
            dcum = dcum + lastrow * ((dg0 * g0) * oh0 + (dg1 * g1) * oh1)
            dH[j] = jnp.where(rowlo, g0, g1) * dS + dHy
            for h, ch, rh, mh, oh in ((h0, c0, r0, lo, oh0), (h1, c1, r1, jnp.logical_not(lo), oh1)):
                decay = jnp.exp(jnp.where(mask, ch - rh, NEG))
                Mh = CB * decay
                dM = _nt(jnp.where(mh, dY, 0.0), Xt) * maskf
                dXt = dXt + jnp.where(mh, _tn(Mh, dY), 0.0)
                G["dCB"] = G["dCB"] + dM * decay
                Q = dM * Mh
                dcum = dcum + _rsum(Q) * oh
                dcum_t = dcum_t + (headrow == h).astype(F32) * _csum(Q)
            dX = dX + dXt * dtl
            ddt = ddt + _split_nn(dXt * X, gather)
            dx_ref[0, :, sl] = dX
        dcum = dcum - dcum_t.T
        for g in range(SSD_G):
            G = grp[g]
            dC = G["dC"] + _nn(G["dCB"], G["B"])
            dB = G["dB"] + _tn(G["dCB"], G["C"])
            dx_ref[0, :, SSD_W + g * SSD_N:SSD_W + (g + 1) * SSD_N] = dB
            dx_ref[0, :, SSD_W + SSD_G * SSD_N + g * SSD_N:SSD_W + SSD_G * SSD_N + (g + 1) * SSD_N] = dC
        r = lax.broadcasted_iota(jnp.int32, (L, L), 0)
        c = lax.broadcasted_iota(jnp.int32, (L, L), 1)
        dda = _dg((c >= r).astype(F32), dcum, 1, 0, lax.Precision.HIGHEST)
        heads = _lanes(0, SSD_HEADS)
        ddt = ddt + dda * a
        draw = jnp.where(heads, ddt * _sig(sm + bias_ref[...]), 0.0)
        ddt_ref[0] = draw.astype(BF16)
        dpar_ref[0:1, :] += _csum(draw)
        dpar_ref[1:2, :] += _csum(jnp.where(heads, dda * dt * a, 0.0))

        @pl.when((bb == Bl - 1) & (i == nc - 1))
        def _():
            acc = ddacc[...]
            lane = lax.broadcasted_iota(jnp.int32, (NPAIR, LANES), 1)
            s0 = _rsum(jnp.where(lane < SSD_P, acc, 0.0))
            s1 = _rsum(jnp.where(lane < SSD_P, 0.0, acc))
            dd_ref[...] = jnp.where(lane == 0, s0, jnp.where(lane == 1, s1, 0.0))

    vec = pl.BlockSpec((1, LANES), lambda b, i: (0, 0))
    par = pl.BlockSpec((NPAIR, LANES), lambda b, i: (0, 0))
    return _pc(body, name=name, grid=(Bl, nc),
               in_specs=[pl.BlockSpec((1, L, SSD_XBC), lambda b, i: (b, nc - 1 - i, 0)),
                         pl.BlockSpec((1, L, LANES), lambda b, i: (b, nc - 1 - i, DT0 // LANES)),
                         pl.BlockSpec((1, 1, NPAIR, LANES, LANES), lambda b, i: (b, nc - 1 - i, 0, 0, 0)),
                         pl.BlockSpec((1, L, SSD_W), lambda b, i: (b, nc - 1 - i, 0)),
                         vec, vec, par],
               out_specs=[pl.BlockSpec((1, L, SSD_XBC), lambda b, i: (b, nc - 1 - i, 0)),
                          pl.BlockSpec((1, L, LANES), lambda b, i: (b, nc - 1 - i, 0)),
                          par, par],
               out_shape=[SDS((Bl, S, SSD_XBC), F32), SDS((Bl, S, LANES), BF16), SDS((NPAIR, LANES), F32),
                          SDS((NPAIR, LANES), F32)],
               scratch=[pltpu.VMEM((NPAIR, LANES, LANES), F32), pltpu.VMEM((NPAIR, LANES), F32)],
               comm=comm)(xbc3, proj3, hin, dy3, bias, alog, dskip)


PE_LO, PE_MID, PE_HI = MLA_NOPE, MLA_NOPE + MLA_ROPE // 2, MLA_NOPE + MLA_ROPE
ATT_SCALE = 1.0 / math.sqrt(MLA_QK)


def _swap_matrix():
    src = lax.broadcasted_iota(jnp.int32, (LANES, LANES), 0)
    dst = lax.broadcasted_iota(jnp.int32, (LANES, LANES), 1)
    half = MLA_ROPE // 2
    first = (dst >= PE_LO) & (dst < PE_MID) & (src == dst + half)
    second = (dst >= PE_MID) & (dst < PE_HI) & (src == dst - half)
    return (second.astype(F32) - first.astype(F32)).astype(BF16)


def rope_tables(pos, invf, *, name):
    T = pos.shape[0]
    tm = _tile(T, 512, 8)

    def body(pos_ref, f_ref, c_ref, s_ref):
        ang = pos_ref[...] * f_ref[...]
        pe = _lanes(PE_LO, PE_HI)
        c_ref[...] = jnp.where(pe, jnp.cos(ang), 1.0)
        s_ref[...] = jnp.where(pe, jnp.sin(ang), 0.0)

    tile = pl.BlockSpec((tm, LANES), lambda i: (i, 0))
    return _pc(body, name=name, grid=(T // tm,),
               in_specs=[pl.BlockSpec((tm, 1), lambda i: (i, 0)), pl.BlockSpec((1, LANES), lambda i: (0, 0))],
               out_specs=[tile, tile], out_shape=[SDS((T, LANES), F32)] * 2)(pos, invf)


V_ONE = MLA_V


def mla_prep_fwd(qt, kvt, proj, cs, sn, *, name):
    T = qt.shape[0]
    tm = _tile(T, 256, 8)
    HW = MLA_H * LANES

    def body(q_ref, k_ref, v_ref, kpe_ref, c_ref, s_ref, qo_ref, ko_ref, vo_ref):
        c, s = c_ref[...], s_ref[...]
        kpe = kpe_ref[...]
        sw = _swap_matrix()
        one = _lanes(V_ONE, V_ONE + 1)
        for h in range(MLA_H):
            sl = slice(h * LANES, (h + 1) * LANES)
            q = q_ref[:, sl]
            k = k_ref[:, sl] + kpe
            qo_ref[:, sl] = ((q * c + _split_nn(q, sw) * s) * ATT_SCALE).astype(BF16)
            ko_ref[:, sl] = (k * c + _split_nn(k, sw) * s).astype(BF16)
            vo_ref[:, sl] = jnp.where(one, 1.0, v_ref[:, sl]).astype(BF16)

    row = pl.BlockSpec((tm, HW), lambda i: (i, 0))
    tab = pl.BlockSpec((tm, LANES), lambda i: (i, 0))
    return _pc(body, name=name, grid=(T // tm,),
               in_specs=[row, row, pl.BlockSpec((tm, HW), lambda i: (i, 1)),
                         pl.BlockSpec((tm, LANES), lambda i: (i, KPE0 // LANES)), tab, tab],
               out_specs=[row, row, row], out_shape=[SDS((T, HW), BF16)] * 3)(qt, kvt, kvt, proj, cs, sn)


def mla_prep_bwd(dqr, dkr, cs, sn, *, name):
    T = dqr.shape[0]
    tm = _tile(T, 256, 8)
    HW = MLA_H * LANES

    def body(dq_ref, dk_ref, c_ref, s_ref, qo_ref, ko_ref, kpe_ref):
        c, s = c_ref[...], s_ref[...]
        sw = _swap_matrix()
        pe = _lanes(PE_LO, PE_HI)
        dkpe = jnp.zeros((tm, LANES), F32)
        for h in range(MLA_H):
            sl = slice(h * LANES, (h + 1) * LANES)
            dq = dq_ref[:, sl] * ATT_SCALE
            dk = dk_ref[:, sl]
            qo_ref[:, sl] = (dq * c - _split_nn(dq * s, sw)).astype(BF16)
            dkk = dk * c - _split_nn(dk * s, sw)
            ko_ref[:, sl] = jnp.where(pe, 0.0, dkk).astype(BF16)
            dkpe = dkpe + jnp.where(pe, dkk, 0.0)
        kpe_ref[...] = dkpe.astype(BF16)

    row = pl.BlockSpec((tm, HW), lambda i: (i, 0))
    tab = pl.BlockSpec((tm, LANES), lambda i: (i, 0))
    return _pc(body, name=name, grid=(T // tm,), in_specs=[row, row, tab, tab], out_specs=[row, row, tab],
               out_shape=[SDS((T, HW), BF16), SDS((T, HW), BF16), SDS((T, LANES), BF16)])(dqr, dkr, cs, sn)


def _att_tile(S):
    return _tile(S, 512, LANES)


def _rep(x, n):
    return x if n == 1 else jnp.concatenate([x] * n, axis=1)


def _diag_mask(t, transposed=False):
    r = lax.broadcasted_iota(jnp.int32, (t, t), 0)
    c = lax.broadcasted_iota(jnp.int32, (t, t), 1)
    return (c >= r) if transposed else (c <= r)


def flash_fwd(qr, kr, vr, Bl, *, comm=None, name):
    T = qr.shape[0]
    S = T // Bl
    t = _att_tile(S)
    n = S // t
    nl = t // LANES

    def body(q_ref, k_ref, v_ref, o_ref, lset_ref, m, acc):
        qi = pl.program_id(2)
        q = q_ref[...]
        m[...] = jnp.full_like(m, NEG)
        acc[...] = jnp.zeros_like(acc)

        def block(kj, masked):
            off = pl.multiple_of(kj * t, t)
            s = _nt(q, k_ref[pl.ds(off, t), :])
            if masked:
                s = jnp.where(_diag_mask(t), s, NEG)
            mo = m[...]
            mn = jnp.maximum(mo, jnp.max(s, axis=1, keepdims=True))
            p = jnp.exp((s - _rep(mn, nl)).astype(BF16))
            acc[...] = jnp.exp(mo - mn) * acc[...] + _nn(p, v_ref[pl.ds(off, t), :])
            m[...] = mn

        def loop(kj, c):
            block(kj, False)
            return c

        lax.fori_loop(0, qi, loop, 0)
        block(qi, True)
        a = acc[...]
        l = a[:, V_ONE:V_ONE + 1]
        o_ref[...] = jnp.where(_lanes(0, MLA_V), a / l, 0.0).astype(BF16)
        lset_ref[...] = (m[...] + jnp.log(l)).T[0:8, :]

    qs = pl.BlockSpec((t, LANES), lambda b, h, qi: (b * n + qi, h))
    seq = pl.BlockSpec((S, LANES), lambda b, h, qi: (b, h))
    return _pc(body, name=name, grid=(Bl, MLA_H, n), in_specs=[qs, seq, seq],
               out_specs=[qs, pl.BlockSpec((8, t), lambda b, h, qi: (b * MLA_H + h, qi))],
               out_shape=[SDS((T, MLA_H * LANES), BF16), SDS((Bl * MLA_H * 8, S), F32)],
               scratch=[pltpu.VMEM((t, LANES), F32), pltpu.VMEM((t, LANES), F32)], comm=comm)(qr, kr, vr)


def flash_bwd(qr, kr, vr, o, lset, dycat, Bl, *, comm=None, name):
    T = qr.shape[0]
    S = T // Bl
    t = _att_tile(S)
    n = S // t
    do0 = (SSD_W + POOL_W) // LANES

    def body(q_ref, k_ref, v_ref, o_ref, lset_ref, do_ref, dq_ref, dk_ref, dv_ref, dka, dva, dlt_ref):
        kj = pl.program_id(2)

        @pl.when(kj == 0)
        def _():
            dq_ref[...] = jnp.zeros_like(dq_ref)
            for r in range(0, S, t):
                d = _rsum(do_ref[r:r + t, :].astype(F32) * o_ref[r:r + t, :].astype(F32))
                dlt_ref[:, r:r + t] = jnp.broadcast_to(d, (t, LANES)).T[0:8, :]

        k = k_ref[...]
        v = v_ref[...]
        dka[...] = jnp.zeros_like(dka)
        dva[...] = jnp.zeros_like(dva)

        def block(qi, masked):
            off = pl.multiple_of(qi * t, t)
            q = q_ref[pl.ds(off, t), :]
            do = do_ref[pl.ds(off, t), :].astype(BF16)
            st = _nt(k, q)
            if masked:
                st = jnp.where(_diag_mask(t, True), st, NEG)
            pt = jnp.exp((st - lset_ref[0:1, pl.ds(off, t)]).astype(BF16))
            dst = pt * (_nt(v, do) - dlt_ref[0:1, pl.ds(off, t)])
            dva[...] += _nn(pt, do)
            dka[...] += _nn(dst, q)
            dq_ref[pl.ds(off, t), :] += _tn(dst, k)

        def loop(qi, c):
            block(qi, False)
            return c

        block(kj, True)
        lax.fori_loop(kj + 1, n, loop, 0)
        dk_ref[...] = dka[...]
        dv_ref[...] = dva[...].astype(BF16)

    ks = pl.BlockSpec((t, LANES), lambda b, h, kj: (b * n + kj, h))
    seq = pl.BlockSpec((S, LANES), lambda b, h, kj: (b, h))
    rows = pl.BlockSpec((8, S), lambda b, h, kj: (b * MLA_H + h, 0))
    return _pc(body, name=name, grid=(Bl, MLA_H, n),
               in_specs=[seq, ks, ks, seq, rows, pl.BlockSpec((S, LANES), lambda b, h, kj: (b, do0 + h))],
               out_specs=[seq, ks, ks],
               out_shape=[SDS((T, MLA_H * LANES), F32), SDS((T, MLA_H * LANES), F32), SDS((T, MLA_H * LANES), BF16)],
               scratch=[pltpu.VMEM((t, LANES), F32), pltpu.VMEM((t, LANES), F32), pltpu.VMEM((8, S), F32)],
               comm=comm)(qr, kr, vr, o, lset, dycat)


def _rows2d(a):
    return a.reshape(-1, a.shape[-1])


def _scalar(i):
    return jnp.reshape(i, (1,)).astype(jnp.int32)


def chip_sum(g8, from_sibling, *, name):
    blk = g8.shape[1:]
    R, C = math.prod(blk[:-1]), blk[-1]
    tm = _tile(R, 512, 16)

    def body(c_ref, a_ref, b_ref, o_ref, ob_ref):
        s = a_ref[0, 0] + b_ref[0]
        o_ref[0] = s
        ob_ref[0] = s.astype(BF16)

    row = pl.BlockSpec((1, tm, C), lambda k, i, c: (k, i, 0))
    spec = pltpu.PrefetchScalarGridSpec(
        num_scalar_prefetch=1, grid=(4, R // tm),
        in_specs=[pl.BlockSpec((1, 1, tm, C), lambda k, i, c: (k, c[0], i, 0)), row], out_specs=[row, row])
    o, ob = pl.pallas_call(body, name=name, grid_spec=spec, out_shape=[SDS((4, R, C), F32), SDS((4, R, C), BF16)],
                           compiler_params=pltpu.CompilerParams(vmem_limit_bytes=VMEM_LIMIT),
                           )(_scalar(lax.axis_index("c")), g8.reshape(4, 2, R, C), from_sibling.reshape(4, R, C))
    return o.reshape((4,) + blk), ob.reshape((4,) + blk)


def adamw_sharded(w, m, v, sums, recv, layer, prev, *, name):
    blk = w.shape[1:]
    R, C = math.prod(blk[:-1]), blk[-1]
    tm = _tile(R, 256, 16)
    bc1 = 1.0 - ADAM_B1 ** ADAM_STEP
    bc2 = 1.0 - ADAM_B2 ** ADAM_STEP
    n_prev = 0 if prev is None else 4

    def body(chip_ref, w_ref, m_ref, v_ref, s_ref, r_ref, *rest):
        g_ref, d_ref, nm_ref, nv_ref = rest[n_prev:]
        g = s_ref[0] + r_ref[0].astype(F32) + r_ref[1].astype(F32) + r_ref[2].astype(F32)
        mm_ = ADAM_B1 * m_ref[0] + (1.0 - ADAM_B1) * g
        vv = ADAM_B2 * v_ref[0] + (1.0 - ADAM_B2) * (g * g)
        g_ref[0] = g
        nm_ref[0] = mm_
        nv_ref[0] = vv
        d_ref[0] = -ADAM_LR * ((mm_ / bc1) / (jnp.sqrt(vv / bc2) + ADAM_EPS) + ADAM_WD * w_ref[0])

    lay = pl.BlockSpec((1, tm, C), lambda i, c: (layer, i, 0))
    spec = pltpu.PrefetchScalarGridSpec(
        num_scalar_prefetch=1, grid=(R // tm,),
        in_specs=[lay, lay, lay, pl.BlockSpec((1, tm, C), lambda i, c: (c[0], i, 0)),
                  pl.BlockSpec((3, tm, C), lambda i, c: (0, i, 0))] + [ANY] * n_prev,
        out_specs=[lay] * 4)
    xi, yi, _ = _place()
    d3 = (w.shape[0], R, C)
    outs = pl.pallas_call(
        body, name=name, grid_spec=spec, out_shape=[SDS(d3, F32)] * 4,
        input_output_aliases={6 + i: i for i in range(n_prev)},
        compiler_params=pltpu.CompilerParams(vmem_limit_bytes=VMEM_LIMIT),
    )(_scalar(2 * xi + yi), w.reshape(d3), m.reshape(d3), v.reshape(d3), sums.reshape(4, R, C), recv.reshape(3, R, C),
      *([] if prev is None else prev))
    return list(outs)


def adamw(w, m, v, parts, *, name):
    shp = w.shape
    w2, m2, v2 = _rows2d(w), _rows2d(m), _rows2d(v)
    R, C = w2.shape
    p3 = [p.reshape(p.shape[0], R, C) for p in parts]
    tm = _tile(R, 256, 8)
    bc1 = 1.0 - ADAM_B1 ** ADAM_STEP
    bc2 = 1.0 - ADAM_B2 ** ADAM_STEP

    def body(w_ref, m_ref, v_ref, *refs):
        p_refs, (g_ref, d_ref, nm_ref, nv_ref) = refs[:len(p3)], refs[len(p3):]
        g = None
        for p_ref, p in zip(p_refs, p3):
            for k in range(p.shape[0]):
                term = p_ref[k].astype(F32)
                g = term if g is None else g + term
        mm_ = ADAM_B1 * m_ref[...] + (1.0 - ADAM_B1) * g
        vv = ADAM_B2 * v_ref[...] + (1.0 - ADAM_B2) * (g * g)
        g_ref[...] = g
        nm_ref[...] = mm_
        nv_ref[...] = vv
        d_ref[...] = -ADAM_LR * ((mm_ / bc1) / (jnp.sqrt(vv / bc2) + ADAM_EPS) + ADAM_WD * w_ref[...])

    blk = pl.BlockSpec((tm, C), lambda i: (i, 0))
    pspecs = [pl.BlockSpec((p.shape[0], tm, C), lambda i: (0, i, 0)) for p in p3]
    outs = _pc(body, name=name, grid=(R // tm,), in_specs=[blk, blk, blk] + pspecs,
               out_specs=[blk] * 4, out_shape=[SDS((R, C), F32)] * 4)(w2, m2, v2, *p3)
    return [o.reshape(shp) for o in outs]


def _place():
    return lax.axis_index("x"), lax.axis_index("y"), lax.axis_index("c")


def all_gather_many(xs, *, name):
    n = len(xs)

    def body(*refs):
        x_refs, o_refs = refs[:n], refs[n:2 * n]
        send_sems, recv_sems, local_sems = refs[2 * n:]
        x, y, c = _place()
        me, sibling = (x, y, c), (x, y, 1 - c)
        chips = [(1 - x, y), (x, 1 - y), (1 - x, 1 - y)]

        def rows(a, p):
            return o_refs[a].at[4 * p[0] + 2 * p[1] + p[2]]

        def copy(a, k, block, to, src=None):
            return pltpu.make_async_remote_copy(
                src_ref=rows(a, block) if src is None else src, dst_ref=rows(a, block),
                send_sem=send_sems.at[7 * a + k], recv_sem=recv_sems.at[7 * a + k], device_id=to, device_id_type=MESH)

        mine = [pltpu.make_async_copy(x_refs[a], rows(a, me), local_sems.at[a]) for a in range(n)]
        for cp in mine:
            cp.start()
        first = []
        for a in range(n):
            first.append(copy(a, 0, me, sibling, src=x_refs[a]))
            first += [copy(a, 1 + j, me, (*chip, c), src=x_refs[a]) for j, chip in enumerate(chips)]
        for cp in first:
            cp.start()
        passed = []
        for j, chip in enumerate(chips):
            for a in range(n):
                copy(a, 1 + j, (*chip, c), me).wait_recv()
                cp = copy(a, 4 + j, (*chip, c), sibling)
                cp.start()
                passed.append(cp)
        for a in range(n):
            copy(a, 0, sibling, me).wait_recv()
            for j, chip in enumerate(chips):
                copy(a, 4 + j, (*chip, 1 - c), me).wait_recv()
        for cp in first + passed:
            cp.wait_send()
        for cp in mine:
            cp.wait()

    return pl.pallas_call(
        body, name=name, in_specs=[ANY] * n, out_specs=[ANY] * n,
        out_shape=[SDS((N_DEV,) + a.shape, a.dtype) for a in xs],
        scratch_shapes=[pltpu.SemaphoreType.DMA((7 * n,)), pltpu.SemaphoreType.DMA((7 * n,)), pltpu.SemaphoreType.DMA((n,))],
    )(*xs)


def _stage(ins, out_shape, n_peers, copy_of, n_arrays=None, local_of=None):
    ins = list(ins)
    n = len(ins) if n_arrays is None else n_arrays

    def copies(in_refs, out_refs, send_sems, recv_sems):
        place = _place()
        out = []
        for a in range(n):
            for k in range(n_peers):
                src, dst, peer = copy_of(in_refs[a], out_refs[a], k, place)
                out.append(pltpu.make_async_remote_copy(
                    src_ref=src, dst_ref=dst, send_sem=send_sems.at[n_peers * a + k], recv_sem=recv_sems.at[n_peers * a + k],
                    device_id=peer, device_id_type=MESH))
        if local_of is not None:
            for i, (src, dst) in enumerate(local_of(in_refs, out_refs, place)):
                out.append(pltpu.make_async_copy(src, dst, send_sems.at[n_peers * n + i]))
        return out

    return dict(ins=ins, out_shape=list(out_shape), sems=n_peers * n + (n if local_of is not None else 0), copies=copies)


def _other_chips(x, y):
    return [(1 - x, y), (x, 1 - y), (1 - x, 1 - y)]


def stage_gather_direct(blocks):
    def copy_of(src, dst, k, place):
        x, y, c = place
        peer = (x, y, 1 - c) if k == 0 else (*_other_chips(x, y)[k - 1], c)
        return src, dst.at[4 * x + 2 * y + c], peer

    return _stage(blocks, [SDS((N_DEV,) + b.shape, b.dtype) for b in blocks], 4, copy_of)


def stage_gather_forward(bufs, own):
    n = len(bufs)

    def copy_of(src, dst, k, place):
        x, y, c = place
        cx, cy = _other_chips(x, y)[k]
        slot = 4 * cx + 2 * cy + c
        return src.at[slot], dst.at[slot], (x, y, 1 - c)

    def local_of(in_refs, out_refs, place):
        x, y, c = place
        return [(in_refs[n + a], out_refs[a].at[4 * x + 2 * y + c]) for a in range(n)]

    st = _stage(list(bufs) + list(own), [SDS(b.shape, b.dtype) for b in bufs], 3, copy_of, n_arrays=n, local_of=local_of)
    st["alias"] = n
    return st


def stage_rs_sibling(g8s):
    def copy_of(src, dst, k, place):
        x, y, c = place
        return src.at[2 * k + (1 - c)], dst.at[k], (x, y, 1 - c)

    return _stage(g8s, [SDS((4,) + g.shape[1:], g.dtype) for g in g8s], 4, copy_of)


def stage_rs_chips(sums):
    def copy_of(src, dst, k, place):
        x, y, c = place
        chip = _other_chips(x, y)[k]
        return src.at[2 * chip[0] + chip[1]], dst.at[k], (*chip, c)

    return _stage(sums, [SDS((3,) + s.shape[1:], s.dtype) for s in sums], 3, copy_of)


def run_stage(stage, *, name):
    n_in, n_out = len(stage["ins"]), len(stage["out_shape"])

    def body(*refs):
        cps = stage["copies"](refs[:n_in], refs[n_in:n_in + n_out], refs[-2], refs[-1])
        for cp in cps:
            cp.start()
        for cp in cps:
            cp.wait()

    return pl.pallas_call(
        body, name=name, in_specs=[ANY] * n_in, out_specs=[ANY] * n_out, out_shape=stage["out_shape"],
        scratch_shapes=[pltpu.SemaphoreType.DMA((stage["sems"],)), pltpu.SemaphoreType.DMA((stage["sems"],))],
    )(*stage["ins"])


def _owner_major(full, axis):
    shp = full.shape
    r = full.reshape(shp[:axis] + (N_DEV, shp[axis] // N_DEV) + shp[axis + 1:])
    return jnp.moveaxis(r, axis, 0)


def _from_owner_major(g8, axis):
    r = jnp.moveaxis(g8, 0, axis)
    shp = r.shape
    return r.reshape(shp[:axis] + (shp[axis] * shp[axis + 1],) + shp[axis + 2:])


def _perm_w_in(w):
    z = jnp.zeros((w.shape[0], LANES), w.dtype)
    dt = jnp.pad(w[:, 2560:2576], ((0, 0), (0, LANES - SSD_HEADS)))
    kpe = jnp.pad(w[:, 3728:3760], ((0, 0), (PE_LO, LANES - PE_HI)))
    return jnp.concatenate([w[:, 0:1024], w[:, 1024:2560], w[:, 2576:3088], w[:, 3088:3472], z, w[:, 3472:3728], dt, kpe], axis=1)


def _unperm_w_in(g):
    return jnp.concatenate([g[:, Z0:Z0 + 1024], g[:, XBC0:XBC0 + 1536], g[:, DT0:DT0 + SSD_HEADS], g[:, U0:U0 + 512],
                            g[:, CQ0:CQ0 + 384], g[:, CKV0:CKV0 + 256], g[:, KPE0 + PE_LO:KPE0 + PE_HI]], axis=1)


def _perm_w_uq(w):
    return jnp.pad(w.reshape(MLA_QR, MLA_H, MLA_QK), ((0, 0), (0, 0), (0, LANES - MLA_QK))).reshape(MLA_QR, MLA_H * LANES)


def _unperm_w_uq(g):
    return g.reshape(MLA_QR, MLA_H, LANES)[:, :, :MLA_QK].reshape(MLA_QR, MLA_H * MLA_QK)


def _perm_w_ukv(w):
    w3 = w.reshape(MLA_KVR, MLA_H, MLA_NOPE + MLA_V)
    pad = ((0, 0), (0, 0), (0, LANES - MLA_NOPE))
    k = jnp.pad(w3[:, :, :MLA_NOPE], pad).reshape(MLA_KVR, MLA_H * LANES)
    v = jnp.pad(w3[:, :, MLA_NOPE:], pad).reshape(MLA_KVR, MLA_H * LANES)
    return jnp.concatenate([k, v], axis=1)


def _unperm_w_ukv(g):
    k = g[:, :MLA_H * LANES].reshape(MLA_KVR, MLA_H, LANES)[:, :, :MLA_NOPE]
    v = g[:, MLA_H * LANES:].reshape(MLA_KVR, MLA_H, LANES)[:, :, :MLA_V]
    return jnp.concatenate([k, v], axis=2).reshape(MLA_KVR, MLA_H * (MLA_NOPE + MLA_V))


def _perm_w_out(w):
    m = jnp.pad(w[SSD_W + POOL_W:].reshape(MLA_H, MLA_V, D), ((0, 0), (0, LANES - MLA_V), (0, 0))).reshape(MLA_H * LANES, D)
    return jnp.concatenate([w[:SSD_W + POOL_W], m], axis=0)


def _unperm_w_out(g):
    m = g[SSD_W + POOL_W:].reshape(MLA_H, LANES, D)[:, :MLA_V].reshape(MLA_H * MLA_V, D)
    return jnp.concatenate([g[:SSD_W + POOL_W], m], axis=0)


def _lane_pad(v):
    return jnp.pad(v.reshape(1, -1), ((0, 0), (0, LANES - v.shape[-1])))


SMALL = ("attn_norm", "ssd_conv_b", "ssd_dt_bias", "ssd_a_log", "ssd_d", "ssd_norm", "pool_w", "pool_scale",
         "mla_q_norm", "mla_kv_norm", "ffn_norm", "ffn_conv_b", "final_norm")
SHARDED = {"w_in": 2, "ssd_conv_w": 2, "mla_w_uq": 2, "mla_w_ukv": 2, "w_out": 1, "ffn_w_up": 2, "ffn_conv_w": 2,
           "ffn_w_down": 1}
ALL_W = ("attn_norm", "w_in", "ssd_conv_w", "ssd_conv_b", "ssd_dt_bias", "ssd_a_log", "ssd_d", "ssd_norm", "pool_w",
         "pool_scale", "mla_q_norm", "mla_w_uq", "mla_kv_norm", "mla_w_ukv", "w_out", "ffn_norm", "ffn_w_up",
         "ffn_conv_w", "ffn_conv_b", "ffn_w_down", "final_norm")


def _pack_small(d):
    rows, layout = [], []
    for k in SMALL:
        a = d[k].reshape(-1)
        n = a.shape[0]
        r = -(-n // LANES)
        rows.append(jnp.pad(a, (0, r * LANES - n)).reshape(r, LANES))
        layout.append((k, n, r, d[k].shape))
    buf = jnp.concatenate(rows, axis=0)
    pad = (-buf.shape[0]) % 8
    return jnp.pad(buf, ((0, pad), (0, 0))), layout


def _unpack_small(buf, layout):
    out, r0 = {}, 0
    for k, n, r, shp in layout:
        out[k] = buf[r0:r0 + r].reshape(-1)[:n].reshape(shp)
        r0 += r
    return out


_PERM = {"w_in": _perm_w_in, "mla_w_uq": _perm_w_uq, "mla_w_ukv": _perm_w_ukv, "w_out": _perm_w_out}
FIRST = ("w_in", "ssd_conv_w")
REST = tuple(k for k in SHARDED if k not in FIRST)


def _sharded_entries(keys, gathered):
    return {k: _PERM.get(k, lambda t: t)(_from_owner_major(g8, SHARDED[k] - 1)) for k, g8 in zip(keys, gathered)}


def _layer_fwd(l, x, W, cs, sn, Bl, next_blocks=None, pending=None):
    T = x.shape[0]
    S = T // Bl
    n = f"l{l}_"
    h = rms_fwd(x, W["attn_norm"], name=n + "attn_norm")
    own_direct = stage_gather_direct(pending) if pending is not None else None
    proj = mm(h, W["w_in"], comm=own_direct, name=n + "w_in")
    proj3 = proj.reshape(Bl, S, PW)
    xbc3 = conv_silu_fwd(proj3, W["ssd_conv_w"], W["ssd_conv_b"], name=n + "ssd_conv")
    own_forward = stage_gather_forward(own_direct["result"], pending) if own_direct else None
    y3, hin = ssd_fwd(xbc3, proj3, W["ssd_dt_bias"], W["ssd_a_log"], W["ssd_d"], comm=own_forward, name=n + "ssd_scan")
    if own_direct:
        W = {**W, **_sharded_entries(REST, own_forward["result"])}
    y = y3.reshape(T, SSD_W)
    y_ssd = gated_rms_fwd(y, proj, W["ssd_norm"], name=n + "ssd_gate_norm")
    y_pool = pool_fwd(proj3, W["pool_w"], W["pool_scale"], name=n + "pool").reshape(T, POOL_W)
    qn = rms_fwd(proj, W["mla_q_norm"], col0=CQ0, width=MLA_QR, name=n + "q_norm")
    kvn = rms_fwd(proj, W["mla_kv_norm"], col0=CKV0, width=MLA_KVR, name=n + "kv_norm")
    qt = mm(qn, W["mla_w_uq"], name=n + "w_uq")
    kvt = mm(kvn, W["mla_w_ukv"], name=n + "w_ukv")
    qr, kr, vr = mla_prep_fwd(qt, kvt, proj, cs, sn, name=n + "rope")
    direct = stage_gather_direct(next_blocks) if next_blocks is not None else None
    o, lset = flash_fwd(qr, kr, vr, Bl, comm=direct, name=n + "attn")
    ycat = jnp.concatenate([y_ssd, y_pool, o], axis=1)
    x1 = mm(ycat, W["w_out"], add=x, name=n + "w_out")
    h2 = rms_fwd(x1, W["ffn_norm"], name=n + "ffn_norm")
    forward = stage_gather_forward(direct["result"], next_blocks) if direct else None
    pre = mm(h2, W["ffn_w_up"], comm=forward, name=n + "w_up")
    gathered = forward["result"] if direct else None
    pre3 = pre.reshape(Bl, S, 2 * DFF)
    act = ffn_act_fwd(pre3, W["ffn_conv_w"], W["ffn_conv_b"], name=n + "ffn_act").reshape(T, DFF)
    x2 = mm(act, W["ffn_w_down"], add=x1, name=n + "w_down")
    saved = dict(x=x, h=h, proj=proj, xbc3=xbc3, hin=hin, y=y, qn=qn, kvn=kvn, vr=vr, qr=qr, kr=kr, o=o, lset=lset,
                 ycat=ycat, x1=x1, h2=h2, pre3=pre3, act=act)
    return x2, saved, gathered, W


EARLY = ("ffn_w_up", "ffn_conv_w", "ffn_w_down", "w_out")
LATE = tuple(k for k in SHARDED if k not in EARLY)
_UNPERM = {"w_in": _unperm_w_in, "mla_w_uq": _unperm_w_uq, "mla_w_ukv": _unperm_w_ukv, "w_out": _unperm_w_out}


def _by_owner(g, keys):
    return [_owner_major(_UNPERM.get(k, lambda t: t)(g[k]), SHARDED[k] - 1) for k in keys]


def _chip_sums(g8s, from_sibling, tag):
    return [chip_sum(g8, r, name=f"{tag}{a}") for a, (g8, r) in enumerate(zip(g8s, from_sibling))]


def _layer_bwd(l, dx2, dx2b, W, sv, cs, sn, Bl, later_g8=None):
    T = dx2.shape[0]
    S = T // Bl
    n = f"l{l}_b_"
    g = {}
    g["ffn_w_down"] = mm(sv["act"], dx2b, ta=True, name=n + "dw_down")
    dact = mm(dx2b, W["ffn_w_down"], tb=True, name=n + "dact")
    to_sibling = stage_rs_sibling(later_g8) if later_g8 is not None else None
    dpg, dpv, dwg, dwv, dbg, dbv = ffn_act_bwd(sv["pre3"], W["ffn_conv_w"], W["ffn_conv_b"], dact.reshape(Bl, S, DFF),
                                               comm=to_sibling, name=n + "ffn_act")
    to_chips = sums = None
    if to_sibling:
        sums = _chip_sums(later_g8, to_sibling["result"], n + "rs_late_add")
        to_chips = stage_rs_chips([sb for _, sb in sums])
    g["ffn_conv_w"] = jnp.concatenate([dwg, dwv], axis=1)
    g["ffn_conv_b"] = jnp.concatenate([dbg, dbv], axis=1)
    dpg, dpv = dpg.reshape(T, DFF), dpv.reshape(T, DFF)
    g["ffn_w_up"] = jnp.concatenate([mm(sv["h2"], dpg, ta=True, name=n + "dw_up_g"),
                                     mm(sv["h2"], dpv, ta=True, name=n + "dw_up_v")], axis=1)
    dh2 = mm(dpg, W["ffn_w_up"], tb=True, name=n + "dh2_g")
    dh2 = mm(dpv, W["ffn_w_up"], tb=True, b_k0=DFF, add=dh2, name=n + "dh2_v")
    dx1, dx1b, g["ffn_norm"] = rms_bwd(sv["x1"], W["ffn_norm"], dh2, add=dx2, name=n + "ffn_norm")
    g["w_out"] = mm(sv["ycat"], dx1b, ta=True, name=n + "dw_out")
    dycat = mm(dx1b, W["w_out"], tb=True, out_dtype=BF16, name=n + "dycat")
    proj, proj3 = sv["proj"], sv["proj"].reshape(Bl, S, PW)
    dy, dz, g["ssd_norm"] = gated_rms_bwd(sv["y"], proj, W["ssd_norm"], dycat, name=n + "ssd_gate_norm")
    dxa, ddt, dpar, dd = ssd_bwd(sv["xbc3"], proj3, sv["hin"], dy.reshape(Bl, S, SSD_W), W["ssd_dt_bias"], W["ssd_a_log"],
                                 W["ssd_d"], comm=to_chips, name=n + "ssd_scan")
    reduced_late = ([s32 for s32, _ in sums], to_chips["result"]) if to_chips else None
    g["ssd_dt_bias"] = dpar[0, :SSD_HEADS]
    g["ssd_a_log"] = dpar[1, :SSD_HEADS]
    g["ssd_d"] = dd[:, :2].reshape(SSD_HEADS)
    early_g8 = _by_owner(g, EARLY)
    early_sibling = stage_rs_sibling(early_g8)
    dxbc, g["ssd_conv_w"], g["ssd_conv_b"] = conv_silu_bwd(proj3, W["ssd_conv_w"], W["ssd_conv_b"], dxa, comm=early_sibling,
                                                           name=n + "ssd_conv")
    du, g["pool_w"], g["pool_scale"] = pool_bwd(proj3, W["pool_w"], W["pool_scale"], dycat.reshape(Bl, S, YCAT), name=n + "pool")
    early_sums = _chip_sums(early_g8, early_sibling["result"], n + "rs_early_add")
    early_chips = stage_rs_chips([sb for _, sb in early_sums])
    dqr, dkr, dv = flash_bwd(sv["qr"], sv["kr"], sv["vr"], sv["o"], sv["lset"], dycat, Bl, comm=early_chips, name=n + "attn_bwd")
    reduced_early = ([s32 for s32, _ in early_sums], early_chips["result"])
    dqt, dkt, dkpe = mla_prep_bwd(dqr, dkr, cs, sn, name=n + "rope")
    g["mla_w_ukv"] = jnp.concatenate([mm(sv["kvn"], dkt, ta=True, name=n + "dw_uk"),
                                      mm(sv["kvn"], dv, ta=True, name=n + "dw_uv")], axis=1)
    dkvn = mm(dkt, W["mla_w_ukv"], tb=True, name=n + "dkvn_k")
    dkvn = mm(dv, W["mla_w_ukv"], tb=True, b_k0=MLA_H * LANES, add=dkvn, name=n + "dkvn_v")
    g["mla_w_uq"] = mm(sv["qn"], dqt, ta=True, name=n + "dw_uq")
    dqn = mm(dqt, W["mla_w_uq"], tb=True, name=n + "dqn")
    dcq, g["mla_q_norm"] = rms_bwd(proj, W["mla_q_norm"], dqn, col0=CQ0, width=MLA_QR, name=n + "q_norm")
    dckv, g["mla_kv_norm"] = rms_bwd(proj, W["mla_kv_norm"], dkvn, col0=CKV0, width=MLA_KVR, name=n + "kv_norm")
    dproj = jnp.concatenate([dz, dxbc.reshape(T, SSD_XBC), du.reshape(T, POOL_W), dcq, jnp.zeros((T, LANES), BF16), dckv,
                             ddt.reshape(T, LANES), dkpe], axis=1)
    g["w_in"] = mm(sv["h"], dproj, ta=True, name=n + "dw_in")
    dh = mm(dproj, W["w_in"], tb=True, name=n + "dh")
    dx, dxb, g["attn_norm"] = rms_bwd(sv["x"], W["attn_norm"], dh, add=dx1, name=n + "attn_norm")
    return dx, dxb, g, reduced_late, reduced_early


def kernel(x, positions, attn_norm, w_in, ssd_conv_w, ssd_conv_b, ssd_dt_bias, ssd_a_log, ssd_d, ssd_norm, pool_w, pool_scale, mla_q_norm, mla_w_uq, mla_kv_norm, mla_w_ukv, w_out, ffn_norm, ffn_w_up, ffn_conv_w, ffn_conv_b, ffn_w_down, final_norm, loss_target, m_attn_norm, m_w_in, m_ssd_conv_w, m_ssd_conv_b, m_ssd_dt_bias, m_ssd_a_log, m_ssd_d, m_ssd_norm, m_pool_w, m_pool_scale, m_mla_q_norm, m_mla_w_uq, m_mla_kv_norm, m_mla_w_ukv, m_w_out, m_ffn_norm, m_ffn_w_up, m_ffn_conv_w, m_ffn_conv_b, m_ffn_w_down, m_final_norm, v_attn_norm, v_w_in, v_ssd_conv_w, v_ssd_conv_b, v_ssd_dt_bias, v_ssd_a_log, v_ssd_d, v_ssd_norm, v_pool_w, v_pool_scale, v_mla_q_norm, v_mla_w_uq, v_mla_kv_norm, v_mla_w_ukv, v_w_out, v_ffn_norm, v_ffn_w_up, v_ffn_conv_w, v_ffn_conv_b, v_ffn_w_down, v_final_norm):
    a = locals()
    Wt = {k: a[k] for k in ALL_W}
    Mo = {k: a["m_" + k] for k in ALL_W}
    Vo = {k: a["v_" + k] for k in ALL_W}
    Bl, S, _ = x.shape
    T = Bl * S

    names = list(SHARDED)
    conv = ("ssd_conv_w", "ffn_conv_w")

    def blocks_of(l, keys=names):
        return [Wt[k][l] if k in conv else Wt[k][l].astype(BF16) for k in keys]

    def replicated(l):
        return {
            "attn_norm": attn_norm[l].reshape(1, D), "ssd_conv_b": ssd_conv_b[l].reshape(1, SSD_XBC),
            "ssd_dt_bias": _lane_pad(ssd_dt_bias[l]), "ssd_a_log": _lane_pad(ssd_a_log[l]),
            "ssd_d": jnp.repeat(ssd_d[l].reshape(NPAIR, 2), SSD_P, axis=1), "ssd_norm": ssd_norm[l].reshape(1, SSD_W),
            "pool_w": pool_w[l].astype(BF16), "pool_scale": pool_scale[l].reshape(1, POOL_W),
            "mla_q_norm": mla_q_norm[l].reshape(1, MLA_QR), "mla_kv_norm": mla_kv_norm[l].reshape(1, MLA_KVR),
            "ffn_norm": ffn_norm[l].reshape(1, D), "ffn_conv_b": ffn_conv_b[l].reshape(1, 2 * DFF)}

    pos = positions.astype(F32).reshape(T, 1)
    inv_freq = ROPE_THETA ** (-jnp.arange(0, MLA_ROPE, 2, dtype=F32) / MLA_ROPE)
    invf = jnp.pad(jnp.concatenate([inv_freq, inv_freq]), (PE_LO, LANES - PE_HI)).reshape(1, LANES)
    cs, sn = rope_tables(pos, invf, name="rope_tables")

    first = all_gather_many(blocks_of(0, FIRST), name="gather_weights_l0")
    layers = [{**replicated(0), **_sharded_entries(FIRST, first)}]
    xc = x.reshape(T, D)
    saved = []
    for l in range(DEPTH):
        xc, sv, gathered, layers[l] = _layer_fwd(l, xc, layers[l], cs, sn, Bl,
                                                 next_blocks=blocks_of(l + 1) if l + 1 < DEPTH else None,
                                                 pending=blocks_of(0, REST) if l == 0 else None)
        saved.append(sv)
        if gathered is not None:
            layers.append({**replicated(l + 1), **_sharded_entries(names, gathered)})
    dx, dxb, g_final, loss_part = final_loss(xc, final_norm.reshape(1, D), loss_target.reshape(T, D), name="final_loss")

    grads = [None] * DEPTH
    reduced = {}

    def record(l, keys, red):
        for a, k in enumerate(keys):
            reduced[(l, k)] = (red[0][a], red[1][a])

    later_g8 = None
    for l in reversed(range(DEPTH)):
        dx, dxb, grads[l], red_late, red_early = _layer_bwd(l, dx, dxb, layers[l], saved[l], cs, sn, Bl, later_g8=later_g8)
        if red_late is not None:
            record(l + 1, LATE, red_late)
        record(l, EARLY, red_early)
        later_g8 = _by_owner(grads[l], LATE)
    loss = lax.psum(loss_part[0, 0], AXES)
    sums = _chip_sums(later_g8, run_stage(stage_rs_sibling(later_g8), name="rs_sibling_l0"), "rs_add_l0_")
    record(0, LATE, ([s32 for s32, _ in sums], run_stage(stage_rs_chips([sb for _, sb in sums]), name="rs_chips_l0")))

    out_g, out_d, out_m, out_v = {}, {}, {}, {}
    for k in names:
        outs = None
        for l in reversed(range(DEPTH)):
            s32, recv = reduced[(l, k)]
            outs = adamw_sharded(Wt[k], Mo[k], Vo[k], s32, recv, l, outs, name=f"adamw_l{l}_{k}")
        out_g[k], out_d[k], out_m[k], out_v[k] = (o.reshape(Wt[k].shape) for o in outs)

    part = {k: g_final.reshape(D) if k == "final_norm" else
            jnp.stack([grads[l][k].reshape(Wt[k].shape[1:]) for l in range(DEPTH)]) for k in SMALL}
    pg, layout = _pack_small(part)
    pw, _ = _pack_small(Wt)
    pm, _ = _pack_small(Mo)
    pv, _ = _pack_small(Vo)
    (pg8,) = all_gather_many([pg], name="gather_small_grads")
    sg, sd, sm, sv_ = adamw(pw, pm, pv, [pg8], name="adamw_small")
    for dst, buf in ((out_g, sg), (out_d, sd), (out_m, sm), (out_v, sv_)):
        dst.update(_unpack_small(buf, layout))

    return (loss, dx.reshape(Bl, S, D), *[out_g[k] for k in ALL_W], *[out_d[k] for k in ALL_W],
            *[out_m[k] for k in ALL_W], *[out_v[k] for k in ALL_W])
```

```python
import functools
import math

import jax
import jax.numpy as jnp
from jax import lax
from jax.experimental import pallas as pl
from jax.experimental.pallas import tpu as pltpu

F32, BF16 = jnp.float32, jnp.bfloat16
SDS = jax.ShapeDtypeStruct
MESH = pl.DeviceIdType.MESH
AXES = ("x", "y", "c")
N_DEV = 8

D = 1024
EPS = 1e-6
SSD_HEADS, SSD_P, SSD_W, SSD_G, SSD_N, SSD_K, SSD_L, SSD_XBC = 16, 64, 1024, 2, 128, 4, 128, 1536
POOL_G, POOL_D, POOL_W, POOL_WIN = 4, 128, 512, (2, 4, 8, 16)
MLA_H, MLA_QR, MLA_KVR, MLA_NOPE, MLA_ROPE, MLA_V, MLA_QK = 8, 384, 256, 64, 32, 64, 96
ROPE_THETA = 10000.0
MIX = 2048
DFF, FFN_K = 2816, 3
DEPTH = 2
ADAM_LR, ADAM_B1, ADAM_B2, ADAM_EPS, ADAM_WD, ADAM_STEP = 0.001, 0.9, 0.999, 1e-08, 0.01, 10

Z0, XBC0, U0, CQ0, CKV0, DT0, KPE0, PW = 0, 1024, 2560, 3072, 3584, 3840, 3968, 4096
LANES = 128
YCAT = SSD_W + POOL_W + MLA_H * LANES
NEG = -1e30
VMEM_LIMIT = 56 * 1024 * 1024
MM_ROW_TILE, MM_LANE_TILE, MM_FULL_K = 1024, 1408, 2816


def _tile(n, pref, mult):
    if n <= pref:
        return n
    for d in range(pref, 0, -mult):
        if d % mult == 0 and n % d == 0:
            return d
    return n


def _dg(a, b, ca, cb, prec=None):
    return lax.dot_general(a, b, (((ca,), (cb,)), ((), ())), preferred_element_type=F32, precision=prec)


def _nn(a, b):
    return _dg(a.astype(BF16), b.astype(BF16), 1, 0)


def _nt(a, b):
    return _dg(a.astype(BF16), b.astype(BF16), 1, 1)


def _tn(a, b):
    return _dg(a.astype(BF16), b.astype(BF16), 0, 0)


def _sig(x):
    return jax.nn.sigmoid(x)


def _silu(x):
    return x * _sig(x)


def _dsilu(x):
    s = _sig(x)
    return s * (1.0 + x * (1.0 - s))


ANY = pl.BlockSpec(memory_space=pl.ANY)


def _pc(body, *, name, grid, in_specs, out_specs, out_shape, scratch=(), comm=None):
    params = pltpu.CompilerParams(vmem_limit_bytes=VMEM_LIMIT)
    if comm is None:
        return pl.pallas_call(body, name=name, grid=grid, in_specs=in_specs, out_specs=out_specs, out_shape=out_shape,
                              scratch_shapes=list(scratch), compiler_params=params)
    single = not isinstance(out_shape, (list, tuple))
    o_specs = [out_specs] if single else list(out_specs)
    o_shape = [out_shape] if single else list(out_shape)
    ni, no, ns = len(in_specs), len(o_specs), len(scratch)
    nci, nco = len(comm["ins"]), len(comm["out_shape"])

    def fused(*refs):
        ins, cins = refs[:ni], refs[ni:ni + nci]
        outs, couts = refs[ni + nci:ni + nci + no], refs[ni + nci + no:ni + nci + no + nco]
        scr = refs[ni + nci + no + nco:ni + nci + no + nco + ns]
        send_sems, recv_sems = refs[-2:]
        copies = comm["copies"](cins, couts, send_sems, recv_sems)
        first = functools.reduce(jnp.logical_and, [pl.program_id(d) == 0 for d in range(len(grid))])
        last = functools.reduce(jnp.logical_and, [pl.program_id(d) == grid[d] - 1 for d in range(len(grid))])

        @pl.when(first)
        def _():
            for cp in copies:
                cp.start()

        body(*ins, *outs, *scr)

        @pl.when(last)
        def _():
            for cp in copies:
                cp.wait()

    call = pl.pallas_call(
        fused, name=name, grid=grid, in_specs=list(in_specs) + [ANY] * nci, out_specs=o_specs + [ANY] * nco,
        out_shape=o_shape + list(comm["out_shape"]),
        scratch_shapes=list(scratch) + [pltpu.SemaphoreType.DMA((comm["sems"],)), pltpu.SemaphoreType.DMA((comm["sems"],))],
        input_output_aliases={ni + a: no + a for a in range(comm.get("alias", 0))},
        compiler_params=params)

    def run(*args):
        res = call(*args, *comm["ins"])
        comm["result"] = list(res[no:])
        return res[0] if single else list(res[:no])

    return run


def _rsum(x):
    return jnp.sum(x, axis=1, keepdims=True)


def _csum(x):
    return jnp.sum(x, axis=0, keepdims=True)


def mm(a, b, *, ta=False, tb=False, add=None, out_dtype=F32, b_k0=0, comm=None, name):
    M, K = (a.shape[1], a.shape[0]) if ta else a.shape
    N = b.shape[0] if tb else b.shape[1]
    assert tb or b_k0 == 0
    tm = _tile(M, MM_LANE_TILE, LANES) if ta else _tile(M, MM_ROW_TILE, 8)
    tn = _tile(N, MM_LANE_TILE, LANES)
    if ta:
        tk = _tile(K, MM_ROW_TILE, 8)
    else:
        tk = K if K <= MM_FULL_K else _tile(K, 2048, LANES)
    nk = K // tk

    def body(*refs):
        if add is None:
            a_ref, b_ref, o_ref = refs[:3]
        else:
            a_ref, b_ref, add_ref, o_ref = refs[:4]
        part = _dg(a_ref[...].astype(BF16), b_ref[...].astype(BF16), 0 if ta else 1, 1 if tb else 0)

        def finish(r):
            if add is not None:
                r = r + add_ref[...].astype(F32)
            o_ref[...] = r.astype(out_dtype)

        if nk == 1:
            finish(part)
            return
        acc = refs[-1]
        k = pl.program_id(2)

        @pl.when(k == 0)
        def _():
            acc[...] = part

        @pl.when(k > 0)
        def _():
            acc[...] += part

        @pl.when(k == nk - 1)
        def _():
            finish(acc[...])

    a_spec = pl.BlockSpec((tk, tm), lambda i, j, k: (k, i)) if ta else pl.BlockSpec((tm, tk), lambda i, j, k: (i, k))
    assert b_k0 % tk == 0
    kb0 = b_k0 // tk
    b_spec = pl.BlockSpec((tn, tk), lambda i, j, k: (j, kb0 + k)) if tb else pl.BlockSpec((tk, tn), lambda i, j, k: (k, j))
    o_spec = pl.BlockSpec((tm, tn), lambda i, j, k: (i, j))
    ins, specs = [a, b], [a_spec, b_spec]
    if add is not None:
        ins.append(add)
        specs.append(o_spec)
    return _pc(body, name=name, grid=(M // tm, N // tn, nk), in_specs=specs, out_specs=o_spec,
               out_shape=SDS((M, N), out_dtype), scratch=[pltpu.VMEM((tm, tn), F32)] if nk > 1 else [], comm=comm)(*ins)


def rms_fwd(x, g, *, col0=0, width=None, name):
    T = x.shape[0]
    W = width or x.shape[1]
    tm = _tile(T, 512, 8)

    def body(x_ref, g_ref, o_ref):
        v = x_ref[...]
        r = lax.rsqrt(jnp.mean(v * v, axis=1, keepdims=True) + EPS)
        o_ref[...] = ((v * r) * g_ref[...]).astype(BF16)

    return _pc(body, name=name, grid=(T // tm,),
               in_specs=[pl.BlockSpec((tm, W), lambda i: (i, col0 // W)), pl.BlockSpec((1, W), lambda i: (0, 0))],
               out_specs=pl.BlockSpec((tm, W), lambda i: (i, 0)), out_shape=SDS((T, W), BF16))(x, g)


def rms_bwd(x, g, dh, *, col0=0, width=None, add=None, name):
    T = x.shape[0]
    W = width or x.shape[1]
    tm = _tile(T, 512, 8)

    def body(*refs):
        if add is None:
            x_ref, g_ref, dh_ref, dx_ref, dg_ref = refs
        else:
            x_ref, g_ref, dh_ref, add_ref, dx_ref, dxb_ref, dg_ref = refs
        v = x_ref[...]
        r = lax.rsqrt(jnp.mean(v * v, axis=1, keepdims=True) + EPS)
        xh = v * r
        d = dh_ref[...].astype(F32)
        dxh = d * g_ref[...]
        dx = r * (dxh - xh * jnp.mean(dxh * xh, axis=1, keepdims=True))
        if add is not None:
            dx = dx + add_ref[...]
            dxb_ref[...] = dx.astype(BF16)
        dx_ref[...] = dx.astype(dx_ref.dtype)

        @pl.when(pl.program_id(0) == 0)
        def _():
            dg_ref[...] = jnp.zeros_like(dg_ref)

        dg_ref[...] += _csum(d * xh)

    row = pl.BlockSpec((tm, W), lambda i: (i, 0))
    vec = pl.BlockSpec((1, W), lambda i: (0, 0))
    ins = [x, g, dh] + ([] if add is None else [add])
    specs = [pl.BlockSpec((tm, W), lambda i: (i, col0 // W)), vec, row] + ([] if add is None else [row])
    if add is None:
        return _pc(body, name=name, grid=(T // tm,), in_specs=specs, out_specs=[row, vec],
                   out_shape=[SDS((T, W), BF16), SDS((1, W), F32)])(*ins)
    return _pc(body, name=name, grid=(T // tm,), in_specs=specs, out_specs=[row, row, vec],
               out_shape=[SDS((T, W), F32), SDS((T, W), BF16), SDS((1, W), F32)])(*ins)


def gated_rms_fwd(y, proj, g, *, name):
    T = y.shape[0]
    tm = _tile(T, 512, 8)

    def body(y_ref, z_ref, g_ref, o_ref):
        v = y_ref[...] * _silu(z_ref[...])
        r = lax.rsqrt(jnp.mean(v * v, axis=1, keepdims=True) + EPS)
        o_ref[...] = ((v * r) * g_ref[...]).astype(BF16)

    row = pl.BlockSpec((tm, SSD_W), lambda i: (i, 0))
    return _pc(body, name=name, grid=(T // tm,), in_specs=[row, row, pl.BlockSpec((1, SSD_W), lambda i: (0, 0))],
               out_specs=row, out_shape=SDS((T, SSD_W), BF16))(y, proj, g)


def gated_rms_bwd(y, proj, g, dycat, *, name):
    T = y.shape[0]
    tm = _tile(T, 512, 8)

    def body(y_ref, z_ref, g_ref, d_ref, dy_ref, dz_ref, dg_ref):
        yv, z = y_ref[...], z_ref[...]
        sg = _sig(z)
        sz = z * sg
        v = yv * sz
        r = lax.rsqrt(jnp.mean(v * v, axis=1, keepdims=True) + EPS)
        vh = v * r
        d = d_ref[...]
        dvh = d * g_ref[...]
        dv = r * (dvh - vh * jnp.mean(dvh * vh, axis=1, keepdims=True))
        dy_ref[...] = dv * sz
        dz_ref[...] = (dv * yv * (sg + sz * (1.0 - sg))).astype(BF16)

        @pl.when(pl.program_id(0) == 0)
        def _():
            dg_ref[...] = jnp.zeros_like(dg_ref)

        dg_ref[...] += _csum(d * vh)

    row = pl.BlockSpec((tm, SSD_W), lambda i: (i, 0))
    vec = pl.BlockSpec((1, SSD_W), lambda i: (0, 0))
    return _pc(body, name=name, grid=(T // tm,), in_specs=[row, row, vec, row], out_specs=[row, row, vec],
               out_shape=[SDS((T, SSD_W), F32), SDS((T, SSD_W), BF16), SDS((1, SSD_W), F32)])(y, proj, g, dycat)


def final_loss(x, g, tgt, *, name):
    T = x.shape[0]
    tm = _tile(T, 512, 8)

    def body(x_ref, g_ref, t_ref, dx_ref, dxb_ref, dg_ref, l_ref):
        v = x_ref[...]
        gg = g_ref[...]
        r = lax.rsqrt(jnp.mean(v * v, axis=1, keepdims=True) + EPS)
        xh = v * r
        err = xh * gg - t_ref[...]
        part = 0.5 * _csum(jnp.mean(err * err, axis=1, keepdims=True))
        d = err * (1.0 / D)
        dxh = d * gg
        dx = r * (dxh - xh * jnp.mean(dxh * xh, axis=1, keepdims=True))
        dx_ref[...] = dx
        dxb_ref[...] = dx.astype(BF16)

        @pl.when(pl.program_id(0) == 0)
        def _():
            dg_ref[...] = jnp.zeros_like(dg_ref)
            l_ref[...] = jnp.zeros_like(l_ref)

        dg_ref[...] += _csum(d * xh)
        l_ref[...] += jnp.broadcast_to(part, (1, LANES))

    row = pl.BlockSpec((tm, D), lambda i: (i, 0))
    vec = pl.BlockSpec((1, D), lambda i: (0, 0))
    return _pc(body, name=name, grid=(T // tm,), in_specs=[row, vec, row],
               out_specs=[row, row, vec, pl.BlockSpec((1, LANES), lambda i: (0, 0))],
               out_shape=[SDS((T, D), F32), SDS((T, D), BF16), SDS((1, D), F32), SDS((1, LANES), F32)])(x, g, tgt)


HALO = 8


def _prev_map(ts, col):
    return lambda b, i, j: (b, jnp.maximum(i * (ts // HALO) - 1, 0), col(j))


def _next_map(ts, n_halo_blocks, col):
    return lambda b, i, j: (b, jnp.minimum((i + 1) * (ts // HALO), n_halo_blocks - 1), col(j))


def _row_chunks(ts, rows):
    rows = min(rows, ts)
    return [(r, rows) for r in range(0, ts, rows)]


def _conv_rows(ext, w_ref, b_ref, r0, n, K):
    win = ext[r0:r0 + HALO + n, :]
    taps = [pltpu.roll(win, K - 1 - k, 0)[HALO:HALO + n] if k < K - 1 else win[HALO:HALO + n] for k in range(K)]
    acc = b_ref[...] + w_ref[0:1, :] * taps[0]
    for k in range(1, K):
        acc = acc + w_ref[k:k + 1, :] * taps[k]
    return acc, taps


def _conv_t_rows(ext2, w_ref, r0, n, K):
    win = ext2[r0:r0 + n + HALO, :]
    dx = w_ref[K - 1:K, :] * win[0:n]
    for k in range(K - 1):
        dx = dx + w_ref[k:k + 1, :] * pltpu.roll(win, n + HALO - (K - 1 - k), 0)[0:n]
    return dx


def _sum8(x):
    acc = x[0:8]
    for r in range(8, x.shape[0], 8):
        acc = acc + x[r:r + 8]
    return acc


def conv_silu_fwd(proj3, w, b, *, name):
    Bl, S, _ = proj3.shape
    C, K = SSD_XBC, SSD_K
    ts, tc = _tile(S, 512, 8), 512
    c0 = XBC0 // tc

    def body(xp_ref, x_ref, w_ref, b_ref, o_ref, ext):
        i = pl.program_id(1)
        ext[0:HALO, :] = jnp.where(i > 0, xp_ref[0], 0.0)
        ext[HALO:HALO + ts, :] = x_ref[0]
        for r0, n in _row_chunks(ts, 32):
            acc, _ = _conv_rows(ext, w_ref, b_ref, r0, n, K)
            o_ref[0, r0:r0 + n, :] = _silu(acc)

    return _pc(body, name=name, grid=(Bl, S // ts, C // tc),
               in_specs=[pl.BlockSpec((1, HALO, tc), _prev_map(ts, lambda j: c0 + j)),
                         pl.BlockSpec((1, ts, tc), lambda b, i, j: (b, i, c0 + j)),
                         pl.BlockSpec((K, tc), lambda b, i, j: (0, j)),
                         pl.BlockSpec((1, tc), lambda b, i, j: (0, j))],
               out_specs=pl.BlockSpec((1, ts, tc), lambda b, i, j: (b, i, j)),
               out_shape=SDS((Bl, S, C), F32), scratch=[pltpu.VMEM((HALO + ts, tc), F32)])(proj3, proj3, w, b)


def conv_silu_bwd(proj3, w, b, dact, *, comm=None, name):
    Bl, S, _ = proj3.shape
    C, K = SSD_XBC, SSD_K
    ts, tc = _tile(S, 512, 8), 512
    c0 = XBC0 // tc
    ns = S // ts

    def body(xp_ref, x_ref, xn_ref, d_ref, dn_ref, w_ref, b_ref, dx_ref, dw_ref, db_ref, ext, ext2):
        bb, i = pl.program_id(1), pl.program_id(2)
        last = i == ns - 1
        ext[0:HALO, :] = jnp.where(i > 0, xp_ref[0], 0.0)
        ext[HALO:HALO + ts, :] = x_ref[0]
        ext[HALO + ts:2 * HALO + ts, :] = jnp.where(last, 0.0, xn_ref[0])
        dw = [jnp.zeros((8, tc), F32) for _ in range(K)]
        db = jnp.zeros((8, tc), F32)
        for r0, n in _row_chunks(ts, 16) + [(ts, HALO)]:
            acc, taps = _conv_rows(ext, w_ref, b_ref, r0, n, K)
            d = d_ref[0, r0:r0 + n, :] if r0 < ts else jnp.where(last, 0.0, dn_ref[0])
            du = d * _dsilu(acc)
            ext2[r0:r0 + n, :] = du
            if r0 < ts:
                dw = [a + _sum8(du * t) for a, t in zip(dw, taps)]
                db = db + _sum8(du)
        for r0, n in _row_chunks(ts, 32):
            dx_ref[0, r0:r0 + n, :] = _conv_t_rows(ext2, w_ref, r0, n, K).astype(BF16)

        @pl.when((bb == 0) & (i == 0))
        def _():
            dw_ref[...] = jnp.zeros_like(dw_ref)
            db_ref[...] = jnp.zeros_like(db_ref)

        for k in range(K):
            dw_ref[k:k + 1, :] += _csum(dw[k])
        db_ref[...] += _csum(db)

    nhb = S // HALO
    cx = lambda j: c0 + j
    cj = lambda j: j
    return _pc(body, name=name, grid=(C // tc, Bl, ns),
               in_specs=[pl.BlockSpec((1, HALO, tc), lambda j, b, i: _prev_map(ts, cx)(b, i, j)),
                         pl.BlockSpec((1, ts, tc), lambda j, b, i: (b, i, c0 + j)),
                         pl.BlockSpec((1, HALO, tc), lambda j, b, i: _next_map(ts, nhb, cx)(b, i, j)),
                         pl.BlockSpec((1, ts, tc), lambda j, b, i: (b, i, j)),
                         pl.BlockSpec((1, HALO, tc), lambda j, b, i: _next_map(ts, nhb, cj)(b, i, j)),
                         pl.BlockSpec((K, tc), lambda j, b, i: (0, j)),
                         pl.BlockSpec((1, tc), lambda j, b, i: (0, j))],
               out_specs=[pl.BlockSpec((1, ts, tc), lambda j, b, i: (b, i, j)),
                          pl.BlockSpec((K, tc), lambda j, b, i: (0, j)),
                          pl.BlockSpec((1, tc), lambda j, b, i: (0, j))],
               out_shape=[SDS((Bl, S, C), BF16), SDS((K, C), F32), SDS((1, C), F32)],
               scratch=[pltpu.VMEM((2 * HALO + ts, tc), F32), pltpu.VMEM((HALO + ts, tc), F32)],
               comm=comm)(proj3, proj3, proj3, dact, dact, w, b)


def ffn_act_fwd(pre3, w, b, *, name):
    Bl, S, _ = pre3.shape
    K = FFN_K
    ts, tc = _tile(S, 512, 8), 256
    nj = DFF // tc

    def body(gp_ref, g_ref, vp_ref, v_ref, wg_ref, wv_ref, bg_ref, bv_ref, o_ref, eg, ev):
        i = pl.program_id(1)
        for p_ref, m_ref, ext in ((gp_ref, g_ref, eg), (vp_ref, v_ref, ev)):
            ext[0:HALO, :] = jnp.where(i > 0, p_ref[0], 0.0)
            ext[HALO:HALO + ts, :] = m_ref[0]
        for r0, n in _row_chunks(ts, 64):
            ug, _ = _conv_rows(eg, wg_ref, bg_ref, r0, n, K)
            uv, _ = _conv_rows(ev, wv_ref, bv_ref, r0, n, K)
            o_ref[0, r0:r0 + n, :] = (_silu(ug) * uv).astype(BF16)

    main = lambda off: pl.BlockSpec((1, ts, tc), lambda b, i, j: (b, i, off + j))
    prev = lambda off: pl.BlockSpec((1, HALO, tc), _prev_map(ts, lambda j: off + j))
    wsp = lambda off: pl.BlockSpec((K, tc), lambda b, i, j: (0, off + j))
    bsp = lambda off: pl.BlockSpec((1, tc), lambda b, i, j: (0, off + j))
    return _pc(body, name=name, grid=(Bl, S // ts, nj),
               in_specs=[prev(0), main(0), prev(nj), main(nj), wsp(0), wsp(nj), bsp(0), bsp(nj)],
               out_specs=pl.BlockSpec((1, ts, tc), lambda b, i, j: (b, i, j)),
               out_shape=SDS((Bl, S, DFF), BF16),
               scratch=[pltpu.VMEM((HALO + ts, tc), F32), pltpu.VMEM((HALO + ts, tc), F32)],
               )(pre3, pre3, pre3, pre3, w, w, b, b)


def ffn_act_bwd(pre3, w, b, dact, *, comm=None, name):
    Bl, S, _ = pre3.shape
    K = FFN_K
    ts, tc = _tile(S, 512, 8), 256
    nj = DFF // tc
    ns = S // ts

    def body(gp_ref, g_ref, gn_ref, vp_ref, v_ref, vn_ref, d_ref, dn_ref, wg_ref, wv_ref, bg_ref, bv_ref,
             dg_ref, dv_ref, dwg_ref, dwv_ref, dbg_ref, dbv_ref, eg, ev, e2g, e2v):
        bb, i = pl.program_id(1), pl.program_id(2)
        last = i == ns - 1
        for p_ref, m_ref, n_ref, ext in ((gp_ref, g_ref, gn_ref, eg), (vp_ref, v_ref, vn_ref, ev)):
            ext[0:HALO, :] = jnp.where(i > 0, p_ref[0], 0.0)
            ext[HALO:HALO + ts, :] = m_ref[0]
            ext[HALO + ts:2 * HALO + ts, :] = jnp.where(last, 0.0, n_ref[0])
        zero8 = jnp.zeros((8, tc), F32)
        dwg, dwv, dbg, dbv = [zero8] * K, [zero8] * K, zero8, zero8
        for r0, n in _row_chunks(ts, 32) + [(ts, HALO)]:
            ug, tg = _conv_rows(eg, wg_ref, bg_ref, r0, n, K)
            uv, tv = _conv_rows(ev, wv_ref, bv_ref, r0, n, K)
            d = d_ref[0, r0:r0 + n, :] if r0 < ts else jnp.where(last, 0.0, dn_ref[0])
            sg = _sig(ug)
            act = ug * sg
            dug = d * uv * (sg + act * (1.0 - sg))
            duv = d * act
            e2g[r0:r0 + n, :] = dug
            e2v[r0:r0 + n, :] = duv
            if r0 < ts:
                dwg = [a + _sum8(dug * t) for a, t in zip(dwg, tg)]
                dwv = [a + _sum8(duv * t) for a, t in zip(dwv, tv)]
                dbg, dbv = dbg + _sum8(dug), dbv + _sum8(duv)
        for w_ref, e2, o_ref in ((wg_ref, e2g, dg_ref), (wv_ref, e2v, dv_ref)):
            for r0, n in _row_chunks(ts, 64):
                o_ref[0, r0:r0 + n, :] = _conv_t_rows(e2, w_ref, r0, n, K).astype(BF16)

        @pl.when((bb == 0) & (i == 0))
        def _():
            for r in (dwg_ref, dwv_ref, dbg_ref, dbv_ref):
                r[...] = jnp.zeros_like(r)

        for dw_ref, dw, db_ref, db in ((dwg_ref, dwg, dbg_ref, dbg), (dwv_ref, dwv, dbv_ref, dbv)):
            for k in range(K):
                dw_ref[k:k + 1, :] += _csum(dw[k])
            db_ref[...] += _csum(db)

    nhb = S // HALO
    main = lambda off: pl.BlockSpec((1, ts, tc), lambda j, b, i: (b, i, off + j))
    prev = lambda off: pl.BlockSpec((1, HALO, tc), lambda j, b, i: _prev_map(ts, lambda jj: off + jj)(b, i, j))
    nxt = lambda off: pl.BlockSpec((1, HALO, tc), lambda j, b, i: _next_map(ts, nhb, lambda jj: off + jj)(b, i, j))
    wsp = lambda off: pl.BlockSpec((K, tc), lambda j, b, i: (0, off + j))
    bsp = lambda off: pl.BlockSpec((1, tc), lambda j, b, i: (0, off + j))
    outs = _pc(body, name=name, grid=(nj, Bl, ns),
               in_specs=[prev(0), main(0), nxt(0), prev(nj), main(nj), nxt(nj), main(0), nxt(0),
                         wsp(0), wsp(nj), bsp(0), bsp(nj)],
               out_specs=[main(0), main(0), wsp(0), wsp(0), bsp(0), bsp(0)],
               out_shape=[SDS((Bl, S, DFF), BF16), SDS((Bl, S, DFF), BF16), SDS((K, DFF), F32), SDS((K, DFF), F32),
                          SDS((1, DFF), F32), SDS((1, DFF), F32)],
               scratch=[pltpu.VMEM((2 * HALO + ts, tc), F32), pltpu.VMEM((2 * HALO + ts, tc), F32),
                        pltpu.VMEM((HALO + ts, tc), F32), pltpu.VMEM((HALO + ts, tc), F32)],
               comm=comm)(pre3, pre3, pre3, pre3, pre3, pre3, dact, dact, w, w, b, b)
    return outs


PHALO = 16


def _pool_window_sums(win, trailing):
    rows = win.shape[0]
    out, s = [], win
    for w in POOL_WIN:
        half = w // 2
        s = s + pltpu.roll(s, half if trailing else rows - half, 0)
        out.append(s)
    return out


def _pick(g, vals):
    r = vals[-1]
    for k in range(len(vals) - 2, -1, -1):
        r = jnp.where(g == k, vals[k], r)
    return r


def _pool_count(g, i, ts, rows, r0=0):
    t = (i * ts + r0 + lax.broadcasted_iota(jnp.int32, (rows, 1), 0) + 1).astype(F32)
    return jnp.minimum(t, _pick(g, [float(w) for w in POOL_WIN]))


def _fill_pool_ext(up_ref, u_ref, ext, i, ts):
    ext[0:PHALO, :] = jnp.where(i > 0, up_ref[0], 0.0)
    ext[PHALO:PHALO + ts, :] = u_ref[0]


def _pooled_rows(ext, g, i, ts, r0, n):
    win = ext[r0:r0 + n + PHALO, :]
    sums = _pool_window_sums(win, True)
    return _pick(g, sums)[PHALO:PHALO + n] / _pool_count(g, i, ts, n, r0) - win[PHALO:PHALO + n]


def pool_fwd(proj3, pool_w, scale, *, name):
    Bl, S, _ = proj3.shape
    ts = _tile(S, 512, 16)
    c0 = U0 // POOL_D

    def body(up_ref, u_ref, w_ref, s_ref, o_ref, ext):
        i, g = pl.program_id(1), pl.program_id(2)
        _fill_pool_ext(up_ref, u_ref, ext, i, ts)
        wm, sc = w_ref[0], s_ref[...]
        for r0, n in _row_chunks(ts, 128):
            o_ref[0, r0:r0 + n, :] = (_nn(_pooled_rows(ext, g, i, ts, r0, n), wm) * sc).astype(BF16)

    return _pc(body, name=name, grid=(Bl, S // ts, POOL_G),
               in_specs=[pl.BlockSpec((1, PHALO, POOL_D), lambda b, i, g: (b, jnp.maximum(i * (ts // PHALO) - 1, 0), c0 + g)),
                         pl.BlockSpec((1, ts, POOL_D), lambda b, i, g: (b, i, c0 + g)),
                         pl.BlockSpec((1, POOL_D, POOL_D), lambda b, i, g: (g, 0, 0)),
                         pl.BlockSpec((1, POOL_D), lambda b, i, g: (0, g))],
               out_specs=pl.BlockSpec((1, ts, POOL_D), lambda b, i, g: (b, i, g)),
               out_shape=SDS((Bl, S, POOL_W), BF16), scratch=[pltpu.VMEM((PHALO + ts, POOL_D), F32)],
               )(proj3, proj3, pool_w, scale)


def pool_bwd(proj3, pool_w, scale, dycat3, *, name):
    Bl, S, _ = proj3.shape
    ts = _tile(S, 512, 16)
    ns = S // ts
    c0 = U0 // POOL_D
    d0 = SSD_W // POOL_D
    nhb = S // PHALO

    def body(up_ref, u_ref, d_ref, dn_ref, w_ref, s_ref, du_ref, dw_ref, ds_ref, ext, ext2):
        g, bb, i = pl.program_id(0), pl.program_id(1), pl.program_id(2)
        last = i == ns - 1
        _fill_pool_ext(up_ref, u_ref, ext, i, ts)
        wm = w_ref[0]
        sc = s_ref[...]
        dwa = jnp.zeros((POOL_D, POOL_D), F32)
        dsa = jnp.zeros((8, POOL_D), F32)
        dpools = []
        for r0, n in _row_chunks(ts, 128):
            pooled = _pooled_rows(ext, g, i, ts, r0, n)
            dy = d_ref[0, r0:r0 + n, :]
            dp = dy * sc
            dpool = _nt(dp, wm)
            dpools.append(dpool)
            ext2[r0:r0 + n, :] = dpool / _pool_count(g, i, ts, n, r0)
            dwa = dwa + _tn(pooled, dp)
            dsa = dsa + _sum8(dy * _nn(pooled, wm))
        dpool_n = _nt(jnp.where(last, 0.0, dn_ref[0]) * sc, wm)
        ext2[ts:ts + PHALO, :] = dpool_n / _pool_count(g, i + 1, ts, PHALO)
        for (r0, n), dpool in zip(_row_chunks(ts, 128), dpools):
            sums = _pool_window_sums(ext2[r0:r0 + n + PHALO, :], False)
            du_ref[0, r0:r0 + n, :] = (_pick(g, sums)[0:n] - dpool).astype(BF16)

        @pl.when((bb == 0) & (i == 0))
        def _():
            dw_ref[...] = jnp.zeros_like(dw_ref)
            ds_ref[...] = jnp.zeros_like(ds_ref)

        dw_ref[0] += dwa
        ds_ref[...] += _csum(dsa)

    return _pc(body, name=name, grid=(POOL_G, Bl, ns),
               in_specs=[pl.BlockSpec((1, PHALO, POOL_D), lambda g, b, i: (b, jnp.maximum(i * (ts // PHALO) - 1, 0), c0 + g)),
                         pl.BlockSpec((1, ts, POOL_D), lambda g, b, i: (b, i, c0 + g)),
                         pl.BlockSpec((1, ts, POOL_D), lambda g, b, i: (b, i, d0 + g)),
                         pl.BlockSpec((1, PHALO, POOL_D), lambda g, b, i: (b, jnp.minimum((i + 1) * (ts // PHALO), nhb - 1), d0 + g)),
                         pl.BlockSpec((1, POOL_D, POOL_D), lambda g, b, i: (g, 0, 0)),
                         pl.BlockSpec((1, POOL_D), lambda g, b, i: (0, g))],
               out_specs=[pl.BlockSpec((1, ts, POOL_D), lambda g, b, i: (b, i, g)),
                          pl.BlockSpec((1, POOL_D, POOL_D), lambda g, b, i: (g, 0, 0)),
                          pl.BlockSpec((1, POOL_D), lambda g, b, i: (0, g))],
               out_shape=[SDS((Bl, S, POOL_W), BF16), SDS((POOL_G, POOL_D, POOL_D), F32), SDS((1, POOL_W), F32)],
               scratch=[pltpu.VMEM((PHALO + ts, POOL_D), F32), pltpu.VMEM((PHALO + ts, POOL_D), F32)],
               )(proj3, proj3, dycat3, dycat3, pool_w, scale)


NPAIR = SSD_HEADS // 2


def _ssd_common(sm, bias, alog):
    L = SSD_L
    dt = jax.nn.softplus(sm + bias)
    a = -jnp.exp(alog)
    da = dt * a
    r = lax.broadcasted_iota(jnp.int32, (L, L), 0)
    c = lax.broadcasted_iota(jnp.int32, (L, L), 1)
    tri = (r >= c).astype(F32)
    cum = _dg(tri, da, 1, 0, lax.Precision.HIGHEST)
    return dt, a, cum, cum.T, r >= c


def _lanes(lo, hi, shape=(1, LANES)):
    lane = lax.broadcasted_iota(jnp.int32, shape, len(shape) - 1)
    return (lane >= lo) & (lane < hi)


def _onehot_lane(h):
    return (lax.broadcasted_iota(jnp.int32, (1, LANES), 1) == h).astype(F32)


def _split_nn(a, e):
    hi = a.astype(BF16)
    lo = (a - hi.astype(F32)).astype(BF16)
    return _dg(hi, e, 1, 0) + _dg(lo, e, 1, 0)


def _head_spread():
    r = lax.broadcasted_iota(jnp.int32, (LANES, SSD_W), 0)
    c = lax.broadcasted_iota(jnp.int32, (LANES, SSD_W), 1)
    return (c // SSD_P == r).astype(BF16)


def _pair_gather(j):
    r = lax.broadcasted_iota(jnp.int32, (LANES, LANES), 0)
    c = lax.broadcasted_iota(jnp.int32, (LANES, LANES), 1)
    return (c == 2 * j + (r >= SSD_P).astype(jnp.int32)).astype(BF16)


def ssd_fwd(xbc3, proj3, bias, alog, dskip, *, comm=None, name):
    Bl, S, _ = xbc3.shape
    L = SSD_L
    nc = S // L

    def body(xbc_ref, sm_ref, bias_ref, alog_ref, d_ref, y_ref, hin_ref, H):
        c = pl.program_id(1)

        @pl.when(c == 0)
        def _():
            H[...] = jnp.zeros_like(H)

        dt, a, cum, cumT, mask = _ssd_common(sm_ref[0], bias_ref[...], alog_ref[...])
        lo = _lanes(0, SSD_P)
        rowlo = lax.broadcasted_iota(jnp.int32, (LANES, LANES), 0) < SSD_P
        spread = _head_spread()
        dt_x = _split_nn(dt, spread)
        el_x = _split_nn(jnp.exp(cum), spread)
        wl_x = _split_nn(jnp.exp(cum[L - 1:L, :] - cum), spread)
        cb = []
        for g in range(SSD_G):
            Bg = xbc_ref[0, :, SSD_W + g * SSD_N:SSD_W + (g + 1) * SSD_N]
            Cg = xbc_ref[0, :, SSD_W + SSD_G * SSD_N + g * SSD_N:SSD_W + SSD_G * SSD_N + (g + 1) * SSD_N]
            cb.append((Bg, Cg, _nt(Cg, Bg)))
        for j in range(NPAIR):
            h0, h1 = 2 * j, 2 * j + 1
            sl = slice(j * LANES, (j + 1) * LANES)
            Bg, Cg, CB = cb[j // (NPAIR // SSD_G)]
            X = xbc_ref[0, :, sl]
            c0, c1 = cum[:, h0:h0 + 1], cum[:, h1:h1 + 1]
            r0, r1 = cumT[h0:h0 + 1, :], cumT[h1:h1 + 1, :]
            cl0, cl1 = cum[L - 1:L, h0:h0 + 1], cum[L - 1:L, h1:h1 + 1]
            Xt = X * dt_x[:, sl]
            M0 = CB * jnp.exp(jnp.where(mask, c0 - r0, NEG))
            M1 = CB * jnp.exp(jnp.where(mask, c1 - r1, NEG))
            Yd = jnp.where(lo, _nn(M0, Xt), _nn(M1, Xt))
            Hp = H[j]
            hin_ref[0, 0, j] = Hp
            Z = _nt(Cg, Hp)
            y_ref[0, :, sl] = Yd + el_x[:, sl] * Z + X * d_ref[j:j + 1, :]
            H[j] = jnp.where(rowlo, jnp.exp(cl0), jnp.exp(cl1)) * Hp + _tn(wl_x[:, sl] * Xt, Bg)

    vec = pl.BlockSpec((1, LANES), lambda b, c: (0, 0))
    return _pc(body, name=name, grid=(Bl, nc),
               in_specs=[pl.BlockSpec((1, L, SSD_XBC), lambda b, c: (b, c, 0)),
                         pl.BlockSpec((1, L, LANES), lambda b, c: (b, c, DT0 // LANES)),
                         vec, vec, pl.BlockSpec((NPAIR, LANES), lambda b, c: (0, 0))],
               out_specs=[pl.BlockSpec((1, L, SSD_W), lambda b, c: (b, c, 0)),
                          pl.BlockSpec((1, 1, NPAIR, LANES, LANES), lambda b, c: (b, c, 0, 0, 0))],
               out_shape=[SDS((Bl, S, SSD_W), F32), SDS((Bl, nc, NPAIR, LANES, LANES), F32)],
               scratch=[pltpu.VMEM((NPAIR, LANES, LANES), F32)], comm=comm)(xbc3, proj3, bias, alog, dskip)


def ssd_bwd(xbc3, proj3, hin, dy3, bias, alog, dskip, *, comm=None, name):
    Bl, S, _ = xbc3.shape
    L = SSD_L
    nc = S // L

    def body(xbc_ref, sm_ref, hin_ref, dy_ref, bias_ref, alog_ref, d_ref, dx_ref, ddt_ref, dpar_ref, dd_ref, dH, ddacc):
        bb, i = pl.program_id(0), pl.program_id(1)

        @pl.when(i == 0)
        def _():
            dH[...] = jnp.zeros_like(dH)

        @pl.when((bb == 0) & (i == 0))
        def _():
            dpar_ref[...] = jnp.zeros_like(dpar_ref)
            ddacc[...] = jnp.zeros_like(ddacc)

        sm = sm_ref[0]
        dt, a, cum, cumT, mask = _ssd_common(sm, bias_ref[...], alog_ref[...])
        maskf = mask.astype(F32)
        lo = _lanes(0, SSD_P)
        rowlo = lax.broadcasted_iota(jnp.int32, (LANES, LANES), 0) < SSD_P
        lastrow = (lax.broadcasted_iota(jnp.int32, (L, 1), 0) == L - 1).astype(F32)
        dcum = jnp.zeros((L, LANES), F32)
        dcum_t = jnp.zeros((LANES, L), F32)
        ddt = jnp.zeros((L, LANES), F32)
        spread = _head_spread()
        ones = jnp.ones((L, LANES), BF16)
        ecum = jnp.exp(cum)
        wall = jnp.exp(cum[L - 1:L, :] - cum)
        dt_x = _split_nn(dt, spread)
        el_x = _split_nn(ecum, spread)
        wl_x = _split_nn(wall, spread)
        headrow = lax.broadcasted_iota(jnp.int32, (LANES, 1), 0)
        grp = []
        for g in range(SSD_G):
            Bg = xbc_ref[0, :, SSD_W + g * SSD_N:SSD_W + (g + 1) * SSD_N]
            Cg = xbc_ref[0, :, SSD_W + SSD_G * SSD_N + g * SSD_N:SSD_W + SSD_G * SSD_N + (g + 1) * SSD_N]
            grp.append(dict(B=Bg, C=Cg, CB=_nt(Cg, Bg), dB=jnp.zeros((L, SSD_N), F32), dC=jnp.zeros((L, SSD_N), F32),
                            dCB=jnp.zeros((L, L), F32)))
        for j in range(NPAIR):
            h0, h1 = 2 * j, 2 * j + 1
            sl = slice(j * LANES, (j + 1) * LANES)
            G = grp[j // (NPAIR // SSD_G)]
            Bg, Cg, CB = G["B"], G["C"], G["CB"]
            X = xbc_ref[0, :, sl]
            dY = dy_ref[0, :, sl]
            c0, c1 = cum[:, h0:h0 + 1], cum[:, h1:h1 + 1]
            r0, r1 = cumT[h0:h0 + 1, :], cumT[h1:h1 + 1, :]
            cl0, cl1 = cum[L - 1:L, h0:h0 + 1], cum[L - 1:L, h1:h1 + 1]
            oh0, oh1 = _onehot_lane(h0), _onehot_lane(h1)
            gather = _pair_gather(j)
            dtl, el, wl = dt_x[:, sl], el_x[:, sl], wl_x[:, sl]
            Xt = X * dtl
            Hp = hin_ref[0, 0, j]
            dS = dH[j]
            dX = dY * d_ref[j:j + 1, :]
            ddacc[j:j + 1, :] += _csum(dY * X)
            Z = _nt(Cg, Hp)
            dZ = dY * el
            dcum = dcum + _split_nn(dY * Z, gather) * ecum
            G["dC"] = G["dC"] + _nn(dZ, Hp)
            dHy = _tn(dZ, Cg)
            Gm = _nt(Bg, dS)
            dXt = wl * Gm
            q = _split_nn(Xt * Gm, gather) * wall
            dcum = dcum + lastrow * _csum(q) - q
            G["dB"] = G["dB"] + _nn(wl * Xt, dS)
            g0, g1 = jnp.exp(cl0), jnp.exp(cl1)
            rowsum = _nn(dS * Hp, ones)
            dg0 = _csum(jnp.where(rowlo, rowsum, 0.0))
            dg1 = _csum(jnp.where(rowlo, 0.0, rowsum))
            dcum = dcum + lastrow * ((dg0 * g0) * oh0 + (dg1 * g1) * oh1)
            dH[j] = jnp.where(rowlo, g0, g1) * dS + dHy
            for h, ch, rh, mh, oh in ((h0, c0, r0, lo, oh0), (h1, c1, r1, jnp.logical_not(lo), oh1)):
                decay = jnp.exp(jnp.where(mask, ch - rh, NEG))
                Mh = CB * decay
                dM = _nt(jnp.where(mh, dY, 0.0), Xt) * maskf
                dXt = dXt + jnp.where(mh, _tn(Mh, dY), 0.0)
                G["dCB"] = G["dCB"] + dM * decay
                Q = dM * Mh
                dcum = dcum + _rsum(Q) * oh
                dcum_t = dcum_t + (headrow == h).astype(F32) * _csum(Q)
            dX = dX + dXt * dtl
            ddt = ddt + _split_nn(dXt * X, gather)
            dx_ref[0, :, sl] = dX
        dcum = dcum - dcum_t.T
        for g in range(SSD_G):
            G = grp[g]
            dC = G["dC"] + _nn(G["dCB"], G["B"])
            dB = G["dB"] + _tn(G["dCB"], G["C"])
            dx_ref[0, :, SSD_W + g * SSD_N:SSD_W + (g + 1) * SSD_N] = dB
            dx_ref[0, :, SSD_W + SSD_G * SSD_N + g * SSD_N:SSD_W + SSD_G * SSD_N + (g + 1) * SSD_N] = dC
        r = lax.broadcasted_iota(jnp.int32, (L, L), 0)
        c = lax.broadcasted_iota(jnp.int32, (L, L), 1)
        dda = _dg((c >= r).astype(F32), dcum, 1, 0, lax.Precision.HIGHEST)
        heads = _lanes(0, SSD_HEADS)
        ddt = ddt + dda * a
        draw = jnp.where(heads, ddt * _sig(sm + bias_ref[...]), 0.0)
        ddt_ref[0] = draw.astype(BF16)
        dpar_ref[0:1, :] += _csum(draw)
        dpar_ref[1:2, :] += _csum(jnp.where(heads, dda * dt * a, 0.0))

        @pl.when((bb == Bl - 1) & (i == nc - 1))
        def _():
            acc = ddacc[...]
            lane = lax.broadcasted_iota(jnp.int32, (NPAIR, LANES), 1)
            s0 = _rsum(jnp.where(lane < SSD_P, acc, 0.0))
            s1 = _rsum(jnp.where(lane < SSD_P, 0.0, acc))
            dd_ref[...] = jnp.where(lane == 0, s0, jnp.where(lane == 1, s1, 0.0))

    vec = pl.BlockSpec((1, LANES), lambda b, i: (0, 0))
    par = pl.BlockSpec((NPAIR, LANES), lambda b, i: (0, 0))
    return _pc(body, name=name, grid=(Bl, nc),
               in_specs=[pl.BlockSpec((1, L, SSD_XBC), lambda b, i: (b, nc - 1 - i, 0)),
                         pl.BlockSpec((1, L, LANES), lambda b, i: (b, nc - 1 - i, DT0 // LANES)),
                         pl.BlockSpec((1, 1, NPAIR, LANES, LANES), lambda b, i: (b, nc - 1 - i, 0, 0, 0)),
                         pl.BlockSpec((1, L, SSD_W), lambda b, i: (b, nc - 1 - i, 0)),
                         vec, vec, par],
               out_specs=[pl.BlockSpec((1, L, SSD_XBC), lambda b, i: (b, nc - 1 - i, 0)),
                          pl.BlockSpec((1, L, LANES), lambda b, i: (b, nc - 1 - i, 0)),
                          par, par],
               out_shape=[SDS((Bl, S, SSD_XBC), F32), SDS((Bl, S, LANES), BF16), SDS((NPAIR, LANES), F32),
                          SDS((NPAIR, LANES), F32)],
               scratch=[pltpu.VMEM((NPAIR, LANES, LANES), F32), pltpu.VMEM((NPAIR, LANES), F32)],
               comm=comm)(xbc3, proj3, hin, dy3, bias, alog, dskip)


PE_LO, PE_MID, PE_HI = MLA_NOPE, MLA_NOPE + MLA_ROPE // 2, MLA_NOPE + MLA_ROPE
ATT_SCALE = 1.0 / math.sqrt(MLA_QK)


def _swap_matrix():
    src = lax.broadcasted_iota(jnp.int32, (LANES, LANES), 0)
    dst = lax.broadcasted_iota(jnp.int32, (LANES, LANES), 1)
    half = MLA_ROPE // 2
    first = (dst >= PE_LO) & (dst < PE_MID) & (src == dst + half)
    second = (dst >= PE_MID) & (dst < PE_HI) & (src == dst - half)
    return (second.astype(F32) - first.astype(F32)).astype(BF16)


def rope_tables(pos, invf, *, name):
    T = pos.shape[0]
    tm = _tile(T, 512, 8)

    def body(pos_ref, f_ref, c_ref, s_ref):
        ang = pos_ref[...] * f_ref[...]
        pe = _lanes(PE_LO, PE_HI)
        c_ref[...] = jnp.where(pe, jnp.cos(ang), 1.0)
        s_ref[...] = jnp.where(pe, jnp.sin(ang), 0.0)

    tile = pl.BlockSpec((tm, LANES), lambda i: (i, 0))
    return _pc(body, name=name, grid=(T // tm,),
               in_specs=[pl.BlockSpec((tm, 1), lambda i: (i, 0)), pl.BlockSpec((1, LANES), lambda i: (0, 0))],
               out_specs=[tile, tile], out_shape=[SDS((T, LANES), F32)] * 2)(pos, invf)


V_ONE = MLA_V


def mla_prep_fwd(qt, kvt, proj, cs, sn, *, name):
    T = qt.shape[0]
    tm = _tile(T, 256, 8)
    HW = MLA_H * LANES

    def body(q_ref, k_ref, v_ref, kpe_ref, c_ref, s_ref, qo_ref, ko_ref, vo_ref):
        c, s = c_ref[...], s_ref[...]
        kpe = kpe_ref[...]
        sw = _swap_matrix()
        one = _lanes(V_ONE, V_ONE + 1)
        for h in range(MLA_H):
            sl = slice(h * LANES, (h + 1) * LANES)
            q = q_ref[:, sl]
            k = k_ref[:, sl] + kpe
            qo_ref[:, sl] = ((q * c + _split_nn(q, sw) * s) * ATT_SCALE).astype(BF16)
            ko_ref[:, sl] = (k * c + _split_nn(k, sw) * s).astype(BF16)
            vo_ref[:, sl] = jnp.where(one, 1.0, v_ref[:, sl]).astype(BF16)

    row = pl.BlockSpec((tm, HW), lambda i: (i, 0))
    tab = pl.BlockSpec((tm, LANES), lambda i: (i, 0))
    return _pc(body, name=name, grid=(T // tm,),
               in_specs=[row, row, pl.BlockSpec((tm, HW), lambda i: (i, 1)),
                         pl.BlockSpec((tm, LANES), lambda i: (i, KPE0 // LANES)), tab, tab],
               out_specs=[row, row, row], out_shape=[SDS((T, HW), BF16)] * 3)(qt, kvt, kvt, proj, cs, sn)


def mla_prep_bwd(dqr, dkr, cs, sn, *, name):
    T = dqr.shape[0]
    tm = _tile(T, 256, 8)
    HW = MLA_H * LANES

    def body(dq_ref, dk_ref, c_ref, s_ref, qo_ref, ko_ref, kpe_ref):
        c, s = c_ref[...], s_ref[...]
        sw = _swap_matrix()
        pe = _lanes(PE_LO, PE_HI)
        dkpe = jnp.zeros((tm, LANES), F32)
        for h in range(MLA_H):
            sl = slice(h * LANES, (h + 1) * LANES)
            dq = dq_ref[:, sl] * ATT_SCALE
            dk = dk_ref[:, sl]
            qo_ref[:, sl] = (dq * c - _split_nn(dq * s, sw)).astype(BF16)
            dkk = dk * c - _split_nn(dk * s, sw)
            ko_ref[:, sl] = jnp.where(pe, 0.0, dkk).astype(BF16)
            dkpe = dkpe + jnp.where(pe, dkk, 0.0)
        kpe_ref[...] = dkpe.astype(BF16)

    row = pl.BlockSpec((tm, HW), lambda i: (i, 0))
    tab = pl.BlockSpec((tm, LANES), lambda i: (i, 0))
    return _pc(body, name=name, grid=(T // tm,), in_specs=[row, row, tab, tab], out_specs=[row, row, tab],
               out_shape=[SDS((T, HW), BF16), SDS((T, HW), BF16), SDS((T, LANES), BF16)])(dqr, dkr, cs, sn)


def _att_tile(S):
    return _tile(S, 512, LANES)


def _rep(x, n):
    return x if n == 1 else jnp.concatenate([x] * n, axis=1)


def _diag_mask(t, transposed=False):
    r = lax.broadcasted_iota(jnp.int32, (t, t), 0)
    c = lax.broadcasted_iota(jnp.int32, (t, t), 1)
    return (c >= r) if transposed else (c <= r)


def flash_fwd(qr, kr, vr, Bl, *, comm=None, name):
    T = qr.shape[0]
    S = T // Bl
    t = _att_tile(S)
    n = S // t
    nl = t // LANES

    def body(q_ref, k_ref, v_ref, o_ref, lset_ref, m, acc):
        qi = pl.program_id(2)
        q = q_ref[...]
        m[...] = jnp.full_like(m, NEG)
        acc[...] = jnp.zeros_like(acc)

        def block(kj, masked):
            off = pl.multiple_of(kj * t, t)
            s = _nt(q, k_ref[pl.ds(off, t), :])
            if masked:
                s = jnp.where(_diag_mask(t), s, NEG)
            mo = m[...]
            mn = jnp.maximum(mo, jnp.max(s, axis=1, keepdims=True))
            p = jnp.exp((s - _rep(mn, nl)).astype(BF16))
            acc[...] = jnp.exp(mo - mn) * acc[...] + _nn(p, v_ref[pl.ds(off, t), :])
            m[...] = mn

        def loop(kj, c):
            block(kj, False)
            return c

        lax.fori_loop(0, qi, loop, 0)
        block(qi, True)
        a = acc[...]
        l = a[:, V_ONE:V_ONE + 1]
        o_ref[...] = jnp.where(_lanes(0, MLA_V), a / l, 0.0).astype(BF16)
        lset_ref[...] = (m[...] + jnp.log(l)).T[0:8, :]

    qs = pl.BlockSpec((t, LANES), lambda b, h, qi: (b * n + qi, h))
    seq = pl.BlockSpec((S, LANES), lambda b, h, qi: (b, h))
    return _pc(body, name=name, grid=(Bl, MLA_H, n), in_specs=[qs, seq, seq],
               out_specs=[qs, pl.BlockSpec((8, t), lambda b, h, qi: (b * MLA_H + h, qi))],
               out_shape=[SDS((T, MLA_H * LANES), BF16), SDS((Bl * MLA_H * 8, S), F32)],
               scratch=[pltpu.VMEM((t, LANES), F32), pltpu.VMEM((t, LANES), F32)], comm=comm)(qr, kr, vr)


def flash_bwd(qr, kr, vr, o, lset, dycat, Bl, *, comm=None, name):
    T = qr.shape[0]
    S = T // Bl
    t = _att_tile(S)
    n = S // t
    do0 = (SSD_W + POOL_W) // LANES

    def body(q_ref, k_ref, v_ref, o_ref, lset_ref, do_ref, dq_ref, dk_ref, dv_ref, dka, dva, dlt_ref):
        kj = pl.program_id(2)

        @pl.when(kj == 0)
        def _():
            dq_ref[...] = jnp.zeros_like(dq_ref)
            for r in range(0, S, t):
                d = _rsum(do_ref[r:r + t, :].astype(F32) * o_ref[r:r + t, :].astype(F32))
                dlt_ref[:, r:r + t] = jnp.broadcast_to(d, (t, LANES)).T[0:8, :]

        k = k_ref[...]
        v = v_ref[...]
        dka[...] = jnp.zeros_like(dka)
        dva[...] = jnp.zeros_like(dva)

        def block(qi, masked):
            off = pl.multiple_of(qi * t, t)
            q = q_ref[pl.ds(off, t), :]
            do = do_ref[pl.ds(off, t), :].astype(BF16)
            st = _nt(k, q)
            if masked:
                st = jnp.where(_diag_mask(t, True), st, NEG)
            pt = jnp.exp((st - lset_ref[0:1, pl.ds(off, t)]).astype(BF16))
            dst = pt * (_nt(v, do) - dlt_ref[0:1, pl.ds(off, t)])
            dva[...] += _nn(pt, do)
            dka[...] += _nn(dst, q)
            dq_ref[pl.ds(off, t), :] += _tn(dst, k)

        def loop(qi, c):
            block(qi, False)
            return c

        block(kj, True)
        lax.fori_loop(kj + 1, n, loop, 0)
        dk_ref[...] = dka[...]
        dv_ref[...] = dva[...].astype(BF16)

    ks = pl.BlockSpec((t, LANES), lambda b, h, kj: (b * n + kj, h))
    seq = pl.BlockSpec((S, LANES), lambda b, h, kj: (b, h))
    rows = pl.BlockSpec((8, S), lambda b, h, kj: (b * MLA_H + h, 0))
    return _pc(body, name=name, grid=(Bl, MLA_H, n),
               in_specs=[seq, ks, ks, seq, rows, pl.BlockSpec((S, LANES), lambda b, h, kj: (b, do0 + h))],
               out_specs=[seq, ks, ks],
               out_shape=[SDS((T, MLA_H * LANES), F32), SDS((T, MLA_H * LANES), F32), SDS((T, MLA_H * LANES), BF16)],
               scratch=[pltpu.VMEM((t, LANES), F32), pltpu.VMEM((t, LANES), F32), pltpu.VMEM((8, S), F32)],
               comm=comm)(qr, kr, vr, o, lset, dycat)


def _rows2d(a):
    return a.reshape(-1, a.shape[-1])


def _scalar(i):
    return jnp.reshape(i, (1,)).astype(jnp.int32)


def chip_sum(g8, from_sibling, *, name):
    blk = g8.shape[1:]
    R, C = math.prod(blk[:-1]), blk[-1]
    tm = _tile(R, 512, 16)

    def body(c_ref, a_ref, b_ref, o_ref, ob_ref):
        s = a_ref[0, 0] + b_ref[0]
        o_ref[0] = s
        ob_ref[0] = s.astype(BF16)

    row = pl.BlockSpec((1, tm, C), lambda k, i, c: (k, i, 0))
    spec = pltpu.PrefetchScalarGridSpec(
        num_scalar_prefetch=1, grid=(4, R // tm),
        in_specs=[pl.BlockSpec((1, 1, tm, C), lambda k, i, c: (k, c[0], i, 0)), row], out_specs=[row, row])
    o, ob = pl.pallas_call(body, name=name, grid_spec=spec, out_shape=[SDS((4, R, C), F32), SDS((4, R, C), BF16)],
                           compiler_params=pltpu.CompilerParams(vmem_limit_bytes=VMEM_LIMIT),
                           )(_scalar(lax.axis_index("c")), g8.reshape(4, 2, R, C), from_sibling.reshape(4, R, C))
    return o.reshape((4,) + blk), ob.reshape((4,) + blk)


def adamw_sharded(w, m, v, sums, recv, layer, prev, *, name):
    blk = w.shape[1:]
    R, C = math.prod(blk[:-1]), blk[-1]
    tm = _tile(R, 256, 16)
    bc1 = 1.0 - ADAM_B1 ** ADAM_STEP
    bc2 = 1.0 - ADAM_B2 ** ADAM_STEP
    n_prev = 0 if prev is None else 4

    def body(chip_ref, w_ref, m_ref, v_ref, s_ref, r_ref, *rest):
        g_ref, d_ref, nm_ref, nv_ref = rest[n_prev:]
        g = s_ref[0] + r_ref[0].astype(F32) + r_ref[1].astype(F32) + r_ref[2].astype(F32)
        mm_ = ADAM_B1 * m_ref[0] + (1.0 - ADAM_B1) * g
        vv = ADAM_B2 * v_ref[0] + (1.0 - ADAM_B2) * (g * g)
        g_ref[0] = g
        nm_ref[0] = mm_
        nv_ref[0] = vv
        d_ref[0] = -ADAM_LR * ((mm_ / bc1) / (jnp.sqrt(vv / bc2) + ADAM_EPS) + ADAM_WD * w_ref[0])

    lay = pl.BlockSpec((1, tm, C), lambda i, c: (layer, i, 0))
    spec = pltpu.PrefetchScalarGridSpec(
        num_scalar_prefetch=1, grid=(R // tm,),
        in_specs=[lay, lay, lay, pl.BlockSpec((1, tm, C), lambda i, c: (c[0], i, 0)),
                  pl.BlockSpec((3, tm, C), lambda i, c: (0, i, 0))] + [ANY] * n_prev,
        out_specs=[lay] * 4)
    xi, yi, _ = _place()
    d3 = (w.shape[0], R, C)
    outs = pl.pallas_call(
        body, name=name, grid_spec=spec, out_shape=[SDS(d3, F32)] * 4,
        input_output_aliases={6 + i: i for i in range(n_prev)},
        compiler_params=pltpu.CompilerParams(vmem_limit_bytes=VMEM_LIMIT),
    )(_scalar(2 * xi + yi), w.reshape(d3), m.reshape(d3), v.reshape(d3), sums.reshape(4, R, C), recv.reshape(3, R, C),
      *([] if prev is None else prev))
    return list(outs)


def adamw(w, m, v, parts, *, name):
    shp = w.shape
    w2, m2, v2 = _rows2d(w), _rows2d(m), _rows2d(v)
    R, C = w2.shape
    p3 = [p.reshape(p.shape[0], R, C) for p in parts]
    tm = _tile(R, 256, 8)
    bc1 = 1.0 - ADAM_B1 ** ADAM_STEP
    bc2 = 1.0 - ADAM_B2 ** ADAM_STEP

    def body(w_ref, m_ref, v_ref, *refs):
        p_refs, (g_ref, d_ref, nm_ref, nv_ref) = refs[:len(p3)], refs[len(p3):]
        g = None
        for p_ref, p in zip(p_refs, p3):
            for k in range(p.shape[0]):
                term = p_ref[k].astype(F32)
                g = term if g is None else g + term
        mm_ = ADAM_B1 * m_ref[...] + (1.0 - ADAM_B1) * g
        vv = ADAM_B2 * v_ref[...] + (1.0 - ADAM_B2) * (g * g)
        g_ref[...] = g
        nm_ref[...] = mm_
        nv_ref[...] = vv
        d_ref[...] = -ADAM_LR * ((mm_ / bc1) / (jnp.sqrt(vv / bc2) + ADAM_EPS) + ADAM_WD * w_ref[...])

    blk = pl.BlockSpec((tm, C), lambda i: (i, 0))
    pspecs = [pl.BlockSpec((p.shape[0], tm, C), lambda i: (0, i, 0)) for p in p3]
    outs = _pc(body, name=name, grid=(R // tm,), in_specs=[blk, blk, blk] + pspecs,
               out_specs=[blk] * 4, out_shape=[SDS((R, C), F32)] * 4)(w2, m2, v2, *p3)
    return [o.reshape(shp) for o in outs]


def _place():
    return lax.axis_index("x"), lax.axis_index("y"), lax.axis_index("c")


def all_gather_many(xs, *, name):
    n = len(xs)

    def body(*refs):
        x_refs, o_refs = refs[:n], refs[n:2 * n]
        send_sems, recv_sems, local_sems = refs[2 * n:]
        x, y, c = _place()
        me, sibling = (x, y, c), (x, y, 1 - c)
        chips = [(1 - x, y), (x, 1 - y), (1 - x, 1 - y)]

        def rows(a, p):
            return o_refs[a].at[4 * p[0] + 2 * p[1] + p[2]]

        def copy(a, k, block, to, src=None):
            return pltpu.make_async_remote_copy(
                src_ref=rows(a, block) if src is None else src, dst_ref=rows(a, block),
                send_sem=send_sems.at[7 * a + k], recv_sem=recv_sems.at[7 * a + k], device_id=to, device_id_type=MESH)

        mine = [pltpu.make_async_copy(x_refs[a], rows(a, me), local_sems.at[a]) for a in range(n)]
        for cp in mine:
            cp.start()
        first = []
        for a in range(n):
            first.append(copy(a, 0, me, sibling, src=x_refs[a]))
            first += [copy(a, 1 + j, me, (*chip, c), src=x_refs[a]) for j, chip in enumerate(chips)]
        for cp in first:
            cp.start()
        passed = []
        for j, chip in enumerate(chips):
            for a in range(n):
                copy(a, 1 + j, (*chip, c), me).wait_recv()
                cp = copy(a, 4 + j, (*chip, c), sibling)
                cp.start()
                passed.append(cp)
        for a in range(n):
            copy(a, 0, sibling, me).wait_recv()
            for j, chip in enumerate(chips):
                copy(a, 4 + j, (*chip, 1 - c), me).wait_recv()
        for cp in first + passed:
            cp.wait_send()
        for cp in mine:
            cp.wait()

    return pl.pallas_call(
        body, name=name, in_specs=[ANY] * n, out_specs=[ANY] * n,
        out_shape=[SDS((N_DEV,) + a.shape, a.dtype) for a in xs],
        scratch_shapes=[pltpu.SemaphoreType.DMA((7 * n,)), pltpu.SemaphoreType.DMA((7 * n,)), pltpu.SemaphoreType.DMA((n,))],
    )(*xs)


def _stage(ins, out_shape, n_peers, copy_of, n_arrays=None, local_of=None):
    ins = list(ins)
    n = len(ins) if n_arrays is None else n_arrays

    def copies(in_refs, out_refs, send_sems, recv_sems):
        place = _place()
        out = []
        for a in range(n):
            for k in range(n_peers):
                src, dst, peer = copy_of(in_refs[a], out_refs[a], k, place)
                out.append(pltpu.make_async_remote_copy(
                    src_ref=src, dst_ref=dst, send_sem=send_sems.at[n_peers * a + k], recv_sem=recv_sems.at[n_peers * a + k],
                    device_id=peer, device_id_type=MESH))
        if local_of is not None:
            for i, (src, dst) in enumerate(local_of(in_refs, out_refs, place)):
                out.append(pltpu.make_async_copy(src, dst, send_sems.at[n_peers * n + i]))
        return out

    return dict(ins=ins, out_shape=list(out_shape), sems=n_peers * n + (n if local_of is not None else 0), copies=copies)


def _other_chips(x, y):
    return [(1 - x, y), (x, 1 - y), (1 - x, 1 - y)]


def stage_gather_direct(blocks):
    def copy_of(src, dst, k, place):
        x, y, c = place
        peer = (x, y, 1 - c) if k == 0 else (*_other_chips(x, y)[k - 1], c)
        return src, dst.at[4 * x + 2 * y + c], peer

    return _stage(blocks, [SDS((N_DEV,) + b.shape, b.dtype) for b in blocks], 4, copy_of)


def stage_gather_forward(bufs, own):
    n = len(bufs)

    def copy_of(src, dst, k, place):
        x, y, c = place
        cx, cy = _other_chips(x, y)[k]
        slot = 4 * cx + 2 * cy + c
        return src.at[slot], dst.at[slot], (x, y, 1 - c)

    def local_of(in_refs, out_refs, place):
        x, y, c = place
        return [(in_refs[n + a], out_refs[a].at[4 * x + 2 * y + c]) for a in range(n)]

    st = _stage(list(bufs) + list(own), [SDS(b.shape, b.dtype) for b in bufs], 3, copy_of, n_arrays=n, local_of=local_of)
    st["alias"] = n
    return st


def stage_rs_sibling(g8s):
    def copy_of(src, dst, k, place):
        x, y, c = place
        return src.at[2 * k + (1 - c)], dst.at[k], (x, y, 1 - c)

    return _stage(g8s, [SDS((4,) + g.shape[1:], g.dtype) for g in g8s], 4, copy_of)


def stage_rs_chips(sums):
    def copy_of(src, dst, k, place):
        x, y, c = place
        chip = _other_chips(x, y)[k]
        return src.at[2 * chip[0] + chip[1]], dst.at[k], (*chip, c)

    return _stage(sums, [SDS((3,) + s.shape[1:], s.dtype) for s in sums], 3, copy_of)


def run_stage(stage, *, name):
    n_in, n_out = len(stage["ins"]), len(stage["out_shape"])

    def body(*refs):
        cps = stage["copies"](refs[:n_in], refs[n_in:n_in + n_out], refs[-2], refs[-1])
        for cp in cps:
            cp.start()
        for cp in cps:
            cp.wait()

    return pl.pallas_call(
        body, name=name, in_specs=[ANY] * n_in, out_specs=[ANY] * n_out, out_shape=stage["out_shape"],
        scratch_shapes=[pltpu.SemaphoreType.DMA((stage["sems"],)), pltpu.SemaphoreType.DMA((stage["sems"],))],
    )(*stage["ins"])


def _owner_major(full, axis):
    shp = full.shape
    r = full.reshape(shp[:axis] + (N_DEV, shp[axis] // N_DEV) + shp[axis + 1:])
    return jnp.moveaxis(r, axis, 0)


def _from_owner_major(g8, axis):
    r = jnp.moveaxis(g8, 0, axis)
    shp = r.shape
    return r.reshape(shp[:axis] + (shp[axis] * shp[axis + 1],) + shp[axis + 2:])


def _perm_w_in(w):
    z = jnp.zeros((w.shape[0], LANES), w.dtype)
    dt = jnp.pad(w[:, 2560:2576], ((0, 0), (0, LANES - SSD_HEADS)))
    kpe = jnp.pad(w[:, 3728:3760], ((0, 0), (PE_LO, LANES - PE_HI)))
    return jnp.concatenate([w[:, 0:1024], w[:, 1024:2560], w[:, 2576:3088], w[:, 3088:3472], z, w[:, 3472:3728], dt, kpe], axis=1)


def _unperm_w_in(g):
    return jnp.concatenate([g[:, Z0:Z0 + 1024], g[:, XBC0:XBC0 + 1536], g[:, DT0:DT0 + SSD_HEADS], g[:, U0:U0 + 512],
                            g[:, CQ0:CQ0 + 384], g[:, CKV0:CKV0 + 256], g[:, KPE0 + PE_LO:KPE0 + PE_HI]], axis=1)


def _perm_w_uq(w):
    return jnp.pad(w.reshape(MLA_QR, MLA_H, MLA_QK), ((0, 0), (0, 0), (0, LANES - MLA_QK))).reshape(MLA_QR, MLA_H * LANES)


def _unperm_w_uq(g):
    return g.reshape(MLA_QR, MLA_H, LANES)[:, :, :MLA_QK].reshape(MLA_QR, MLA_H * MLA_QK)


def _perm_w_ukv(w):
    w3 = w.reshape(MLA_KVR, MLA_H, MLA_NOPE + MLA_V)
    pad = ((0, 0), (0, 0), (0, LANES - MLA_NOPE))
    k = jnp.pad(w3[:, :, :MLA_NOPE], pad).reshape(MLA_KVR, MLA_H * LANES)
    v = jnp.pad(w3[:, :, MLA_NOPE:], pad).reshape(MLA_KVR, MLA_H * LANES)
    return jnp.concatenate([k, v], axis=1)


def _unperm_w_ukv(g):
    k = g[:, :MLA_H * LANES].reshape(MLA_KVR, MLA_H, LANES)[:, :, :MLA_NOPE]
    v = g[:, MLA_H * LANES:].reshape(MLA_KVR, MLA_H, LANES)[:, :, :MLA_V]
    return jnp.concatenate([k, v], axis=2).reshape(MLA_KVR, MLA_H * (MLA_NOPE + MLA_V))


def _perm_w_out(w):
    m = jnp.pad(w[SSD_W + POOL_W:].reshape(MLA_H, MLA_V, D), ((0, 0), (0, LANES - MLA_V), (0, 0))).reshape(MLA_H * LANES, D)
    return jnp.concatenate([w[:SSD_W + POOL_W], m], axis=0)


def _unperm_w_out(g):
    m = g[SSD_W + POOL_W:].reshape(MLA_H, LANES, D)[:, :MLA_V].reshape(MLA_H * MLA_V, D)
    return jnp.concatenate([g[:SSD_W + POOL_W], m], axis=0)


def _lane_pad(v):
    return jnp.pad(v.reshape(1, -1), ((0, 0), (0, LANES - v.shape[-1])))


SMALL = ("attn_norm", "ssd_conv_b", "ssd_dt_bias", "ssd_a_log", "ssd_d", "ssd_norm", "pool_w", "pool_scale",
         "mla_q_norm", "mla_kv_norm", "ffn_norm", "ffn_conv_b", "final_norm")
SHARDED = {"w_in": 2, "ssd_conv_w": 2, "mla_w_uq": 2, "mla_w_ukv": 2, "w_out": 1, "ffn_w_up": 2, "ffn_conv_w": 2,
           "ffn_w_down": 1}
ALL_W = ("attn_norm", "w_in", "ssd_conv_w", "ssd_conv_b", "ssd_dt_bias", "ssd_a_log", "ssd_d", "ssd_norm", "pool_w",
         "pool_scale", "mla_q_norm", "mla_w_uq", "mla_kv_norm", "mla_w_ukv", "w_out", "ffn_norm", "ffn_w_up",
         "ffn_conv_w", "ffn_conv_b", "ffn_w_down", "final_norm")


def _pack_small(d):
    rows, layout = [], []
    for k in SMALL:
        a = d[k].reshape(-1)
        n = a.shape[0]
        r = -(-n // LANES)
        rows.append(jnp.pad(a, (0, r * LANES - n)).reshape(r, LANES))
        layout.append((k, n, r, d[k].shape))
    buf = jnp.concatenate(rows, axis=0)
    pad = (-buf.shape[0]) % 8
    return jnp.pad(buf, ((0, pad), (0, 0))), layout


def _unpack_small(buf, layout):
    out, r0 = {}, 0
    for k, n, r, shp in layout:
        out[k] = buf[r0:r0 + r].reshape(-1)[:n].reshape(shp)
        r0 += r
    return out


_PERM = {"w_in": _perm_w_in, "mla_w_uq": _perm_w_uq, "mla_w_ukv": _perm_w_ukv, "w_out": _perm_w_out}
FIRST = ("w_in", "ssd_conv_w")
REST = tuple(k for k in SHARDED if k not in FIRST)


def _sharded_entries(keys, gathered):
    return {k: _PERM.get(k, lambda t: t)(_from_owner_major(g8, SHARDED[k] - 1)) for k, g8 in zip(keys, gathered)}


def _layer_fwd(l, x, W, cs, sn, Bl, next_blocks=None, pending=None):
    T = x.shape[0]
    S = T // Bl
    n = f"l{l}_"
    h = rms_fwd(x, W["attn_norm"], name=n + "attn_norm")
    own_direct = stage_gather_direct(pending) if pending is not None else None
    proj = mm(h, W["w_in"], comm=own_direct, name=n + "w_in")
    proj3 = proj.reshape(Bl, S, PW)
    xbc3 = conv_silu_fwd(proj3, W["ssd_conv_w"], W["ssd_conv_b"], name=n + "ssd_conv")
    own_forward = stage_gather_forward(own_direct["result"], pending) if own_direct else None
    y3, hin = ssd_fwd(xbc3, proj3, W["ssd_dt_bias"], W["ssd_a_log"], W["ssd_d"], comm=own_forward, name=n + "ssd_scan")
    if own_direct:
        W = {**W, **_sharded_entries(REST, own_forward["result"])}
    y = y3.reshape(T, SSD_W)
    y_ssd = gated_rms_fwd(y, proj, W["ssd_norm"], name=n + "ssd_gate_norm")
    y_pool = pool_fwd(proj3, W["pool_w"], W["pool_scale"], name=n + "pool").reshape(T, POOL_W)
    qn = rms_fwd(proj, W["mla_q_norm"], col0=CQ0, width=MLA_QR, name=n + "q_norm")
    kvn = rms_fwd(proj, W["mla_kv_norm"], col0=CKV0, width=MLA_KVR, name=n + "kv_norm")
    qt = mm(qn, W["mla_w_uq"], name=n + "w_uq")
    kvt = mm(kvn, W["mla_w_ukv"], name=n + "w_ukv")
    qr, kr, vr = mla_prep_fwd(qt, kvt, proj, cs, sn, name=n + "rope")
    direct = stage_gather_direct(next_blocks) if next_blocks is not None else None
    o, lset = flash_fwd(qr, kr, vr, Bl, comm=direct, name=n + "attn")
    ycat = jnp.concatenate([y_ssd, y_pool, o], axis=1)
    x1 = mm(ycat, W["w_out"], add=x, name=n + "w_out")
    h2 = rms_fwd(x1, W["ffn_norm"], name=n + "ffn_norm")
    forward = stage_gather_forward(direct["result"], next_blocks) if direct else None
    pre = mm(h2, W["ffn_w_up"], comm=forward, name=n + "w_up")
    gathered = forward["result"] if direct else None
    pre3 = pre.reshape(Bl, S, 2 * DFF)
    act = ffn_act_fwd(pre3, W["ffn_conv_w"], W["ffn_conv_b"], name=n + "ffn_act").reshape(T, DFF)
    x2 = mm(act, W["ffn_w_down"], add=x1, name=n + "w_down")
    saved = dict(x=x, h=h, proj=proj, xbc3=xbc3, hin=hin, y=y, qn=qn, kvn=kvn, vr=vr, qr=qr, kr=kr, o=o, lset=lset,
                 ycat=ycat, x1=x1, h2=h2, pre3=pre3, act=act)
    return x2, saved, gathered, W


EARLY = ("ffn_w_up", "ffn_conv_w", "ffn_w_down", "w_out")
LATE = tuple(k for k in SHARDED if k not in EARLY)
_UNPERM = {"w_in": _unperm_w_in, "mla_w_uq": _unperm_w_uq, "mla_w_ukv": _unperm_w_ukv, "w_out": _unperm_w_out}


def _by_owner(g, keys):
    return [_owner_major(_UNPERM.get(k, lambda t: t)(g[k]), SHARDED[k] - 1) for k in keys]


def _chip_sums(g8s, from_sibling, tag):
    return [chip_sum(g8, r, name=f"{tag}{a}") for a, (g8, r) in enumerate(zip(g8s, from_sibling))]


def _layer_bwd(l, dx2, dx2b, W, sv, cs, sn, Bl, later_g8=None):
    T = dx2.shape[0]
    S = T // Bl
    n = f"l{l}_b_"
    g = {}
    g["ffn_w_down"] = mm(sv["act"], dx2b, ta=True, name=n + "dw_down")
    dact = mm(dx2b, W["ffn_w_down"], tb=True, name=n + "dact")
    to_sibling = stage_rs_sibling(later_g8) if later_g8 is not None else None
    dpg, dpv, dwg, dwv, dbg, dbv = ffn_act_bwd(sv["pre3"], W["ffn_conv_w"], W["ffn_conv_b"], dact.reshape(Bl, S, DFF),
                                               comm=to_sibling, name=n + "ffn_act")
    to_chips = sums = None
    if to_sibling:
        sums = _chip_sums(later_g8, to_sibling["result"], n + "rs_late_add")
        to_chips = stage_rs_chips([sb for _, sb in sums])
    g["ffn_conv_w"] = jnp.concatenate([dwg, dwv], axis=1)
    g["ffn_conv_b"] = jnp.concatenate([dbg, dbv], axis=1)
    dpg, dpv = dpg.reshape(T, DFF), dpv.reshape(T, DFF)
    g["ffn_w_up"] = jnp.concatenate([mm(sv["h2"], dpg, ta=True, name=n + "dw_up_g"),
                                     mm(sv["h2"], dpv, ta=True, name=n + "dw_up_v")], axis=1)
    dh2 = mm(dpg, W["ffn_w_up"], tb=True, name=n + "dh2_g")
    dh2 = mm(dpv, W["ffn_w_up"], tb=True, b_k0=DFF, add=dh2, name=n + "dh2_v")
    dx1, dx1b, g["ffn_norm"] = rms_bwd(sv["x1"], W["ffn_norm"], dh2, add=dx2, name=n + "ffn_norm")
    g["w_out"] = mm(sv["ycat"], dx1b, ta=True, name=n + "dw_out")
    dycat = mm(dx1b, W["w_out"], tb=True, out_dtype=BF16, name=n + "dycat")
    proj, proj3 = sv["proj"], sv["proj"].reshape(Bl, S, PW)
    dy, dz, g["ssd_norm"] = gated_rms_bwd(sv["y"], proj, W["ssd_norm"], dycat, name=n + "ssd_gate_norm")
    dxa, ddt, dpar, dd = ssd_bwd(sv["xbc3"], proj3, sv["hin"], dy.reshape(Bl, S, SSD_W), W["ssd_dt_bias"], W["ssd_a_log"],
                                 W["ssd_d"], comm=to_chips, name=n + "ssd_scan")
    reduced_late = ([s32 for s32, _ in sums], to_chips["result"]) if to_chips else None
    g["ssd_dt_bias"] = dpar[0, :SSD_HEADS]
    g["ssd_a_log"] = dpar[1, :SSD_HEADS]
    g["ssd_d"] = dd[:, :2].reshape(SSD_HEADS)
    early_g8 = _by_owner(g, EARLY)
    early_sibling = stage_rs_sibling(early_g8)
    dxbc, g["ssd_conv_w"], g["ssd_conv_b"] = conv_silu_bwd(proj3, W["ssd_conv_w"], W["ssd_conv_b"], dxa, comm=early_sibling,
                                                           name=n + "ssd_conv")
    du, g["pool_w"], g["pool_scale"] = pool_bwd(proj3, W["pool_w"], W["pool_scale"], dycat.reshape(Bl, S, YCAT), name=n + "pool")
    early_sums = _chip_sums(early_g8, early_sibling["result"], n + "rs_early_add")
    early_chips = stage_rs_chips([sb for _, sb in early_sums])
    dqr, dkr, dv = flash_bwd(sv["qr"], sv["kr"], sv["vr"], sv["o"], sv["lset"], dycat, Bl, comm=early_chips, name=n + "attn_bwd")
    reduced_early = ([s32 for s32, _ in early_sums], early_chips["result"])
    dqt, dkt, dkpe = mla_prep_bwd(dqr, dkr, cs, sn, name=n + "rope")
    g["mla_w_ukv"] = jnp.concatenate([mm(sv["kvn"], dkt, ta=True, name=n + "dw_uk"),
                                      mm(sv["kvn"], dv, ta=True, name=n + "dw_uv")], axis=1)
    dkvn = mm(dkt, W["mla_w_ukv"], tb=True, name=n + "dkvn_k")
    dkvn = mm(dv, W["mla_w_ukv"], tb=True, b_k0=MLA_H * LANES, add=dkvn, name=n + "dkvn_v")
    g["mla_w_uq"] = mm(sv["qn"], dqt, ta=True, name=n + "dw_uq")
    dqn = mm(dqt, W["mla_w_uq"], tb=True, name=n + "dqn")
    dcq, g["mla_q_norm"] = rms_bwd(proj, W["mla_q_norm"], dqn, col0=CQ0, width=MLA_QR, name=n + "q_norm")
    dckv, g["mla_kv_norm"] = rms_bwd(proj, W["mla_kv_norm"], dkvn, col0=CKV0, width=MLA_KVR, name=n + "kv_norm")
    dproj = jnp.concatenate([dz, dxbc.reshape(T, SSD_XBC), du.reshape(T, POOL_W), dcq, jnp.zeros((T, LANES), BF16), dckv,
                             ddt.reshape(T, LANES), dkpe], axis=1)
    g["w_in"] = mm(sv["h"], dproj, ta=True, name=n + "dw_in")
    dh = mm(dproj, W["w_in"], tb=True, name=n + "dh")
    dx, dxb, g["attn_norm"] = rms_bwd(sv["x"], W["attn_norm"], dh, add=dx1, name=n + "attn_norm")
    return dx, dxb, g, reduced_late, reduced_early


def kernel(x, positions, attn_norm, w_in, ssd_conv_w, ssd_conv_b, ssd_dt_bias, ssd_a_log, ssd_d, ssd_norm, pool_w, pool_scale, mla_q_norm, mla_w_uq, mla_kv_norm, mla_w_ukv, w_out, ffn_norm, ffn_w_up, ffn_conv_w, ffn_conv_b, ffn_w_down, final_norm, loss_target, m_attn_norm, m_w_in, m_ssd_conv_w, m_ssd_conv_b, m_ssd_dt_bias, m_ssd_a_log, m_ssd_d, m_ssd_norm, m_pool_w, m_pool_scale, m_mla_q_norm, m_mla_w_uq, m_mla_kv_norm, m_mla_w_ukv, m_w_out, m_ffn_norm, m_ffn_w_up, m_ffn_conv_w, m_ffn_conv_b, m_ffn_w_down, m_final_norm, v_attn_norm, v_w_in, v_ssd_conv_w, v_ssd_conv_b, v_ssd_dt_bias, v_ssd_a_log, v_ssd_d, v_ssd_norm, v_pool_w, v_pool_scale, v_mla_q_norm, v_mla_w_uq, v_mla_kv_norm, v_mla_w_ukv, v_w_out, v_ffn_norm, v_ffn_w_up, v_ffn_conv_w, v_ffn_conv_b, v_ffn_w_down, v_final_norm):
    a = locals()
    Wt = {k: a[k] for k in ALL_W}
    Mo = {k: a["m_" + k] for k in ALL_W}
    Vo = {k: a["v_" + k] for k in ALL_W}
    Bl, S, _ = x.shape
    T = Bl * S

    names = list(SHARDED)
    conv = ("ssd_conv_w", "ffn_conv_w")

    def blocks_of(l, keys=names):
        return [Wt[k][l] if k in conv else Wt[k][l].astype(BF16) for k in keys]

    def replicated(l):
        return {
            "attn_norm": attn_norm[l].reshape(1, D), "ssd_conv_b": ssd_conv_b[l].reshape(1, SSD_XBC),
            "ssd_dt_bias": _lane_pad(ssd_dt_bias[l]), "ssd_a_log": _lane_pad(ssd_a_log[l]),
            "ssd_d": jnp.repeat(ssd_d[l].reshape(NPAIR, 2), SSD_P, axis=1), "ssd_norm": ssd_norm[l].reshape(1, SSD_W),
            "pool_w": pool_w[l].astype(BF16), "pool_scale": pool_scale[l].reshape(1, POOL_W),
            "mla_q_norm": mla_q_norm[l].reshape(1, MLA_QR), "mla_kv_norm": mla_kv_norm[l].reshape(1, MLA_KVR),
            "ffn_norm": ffn_norm[l].reshape(1, D), "ffn_conv_b": ffn_conv_b[l].reshape(1, 2 * DFF)}

    pos = positions.astype(F32).reshape(T, 1)
    inv_freq = ROPE_THETA ** (-jnp.arange(0, MLA_ROPE, 2, dtype=F32) / MLA_ROPE)
    invf = jnp.pad(jnp.concatenate([inv_freq, inv_freq]), (PE_LO, LANES - PE_HI)).reshape(1, LANES)
    cs, sn = rope_tables(pos, invf, name="rope_tables")

    first = all_gather_many(blocks_of(0, FIRST), name="gather_weights_l0")
    layers = [{**replicated(0), **_sharded_entries(FIRST, first)}]
    xc = x.reshape(T, D)
    saved = []
    for l in range(DEPTH):
        xc, sv, gathered, layers[l] = _layer_fwd(l, xc, layers[l], cs, sn, Bl,
                                                 next_blocks=blocks_of(l + 1) if l + 1 < DEPTH else None,
                                                 pending=blocks_of(0, REST) if l == 0 else None)
        saved.append(sv)
        if gathered is not None:
            layers.append({**replicated(l + 1), **_sharded_entries(names, gathered)})
    dx, dxb, g_final, loss_part = final_loss(xc, final_norm.reshape(1, D), loss_target.reshape(T, D), name="final_loss")

    grads = [None] * DEPTH
    reduced = {}

    def record(l, keys, red):
        for a, k in enumerate(keys):
            reduced[(l, k)] = (red[0][a], red[1][a])

    later_g8 = None
    for l in reversed(range(DEPTH)):
        dx, dxb, grads[l], red_late, red_early = _layer_bwd(l, dx, dxb, layers[l], saved[l], cs, sn, Bl, later_g8=later_g8)
        if red_late is not None:
            record(l + 1, LATE, red_late)
        record(l, EARLY, red_early)
        later_g8 = _by_owner(grads[l], LATE)
    loss = lax.psum(loss_part[0, 0], AXES)
    sums = _chip_sums(later_g8, run_stage(stage_rs_sibling(later_g8), name="rs_sibling_l0"), "rs_add_l0_")
    record(0, LATE, ([s32 for s32, _ in sums], run_stage(stage_rs_chips([sb for _, sb in sums]), name="rs_chips_l0")))

    out_g, out_d, out_m, out_v = {}, {}, {}, {}
    for k in names:
        outs = None
        for l in reversed(range(DEPTH)):
            s32, recv = reduced[(l, k)]
            outs = adamw_sharded(Wt[k], Mo[k], Vo[k], s32, recv, l, outs, name=f"adamw_l{l}_{k}")
        out_g[k], out_d[k], out_m[k], out_v[k] = (o.reshape(Wt[k].shape) for o in outs)

    part = {k: g_final.reshape(D) if k == "final_norm" else
            jnp.stack([grads[l][k].reshape(Wt[k].shape[1:]) for l in range(DEPTH)]) for k in SMALL}
    pg, layout = _pack_small(part)
    pw, _ = _pack_small(Wt)
    pm, _ = _pack_small(Mo)
    pv, _ = _pack_small(Vo)
    (pg8,) = all_gather_many([pg], name="gather_small_grads")
    sg, sd, sm, sv_ = adamw(pw, pm, pv, [pg8], name="adamw_small")
    for dst, buf in ((out_g, sg), (out_d, sd), (out_m, sm), (out_v, sv_)):
        dst.update(_unpack_small(buf, layout))

    return (loss, dx.reshape(Bl, S, D), *[out_g[k] for k in ALL_W], *[out_d[k] for k in ALL_W],
            *[out_m[k] for k in ALL_W], *[out_v[k] for k in ALL_W])
```

```python
import functools
import math

import jax
import jax.numpy as jnp
from jax import lax
from jax.experimental import pallas as pl
from jax.experimental.pallas import tpu as pltpu

F32, BF16 = jnp.float32, jnp.bfloat16
SDS = jax.ShapeDtypeStruct
MESH = pl.DeviceIdType.MESH
AXES = ("x", "y", "c")
N_DEV = 8

D = 1024
EPS = 1e-6
SSD_HEADS, SSD_P, SSD_W, SSD_G, SSD_N, SSD_K, SSD_L, SSD_XBC = 16, 64, 1024, 2, 128, 4, 128, 1536
POOL_G, POOL_D, POOL_W, POOL_WIN = 4, 128, 512, (2, 4, 8, 16)
MLA_H, MLA_QR, MLA_KVR, MLA_NOPE, MLA_ROPE, MLA_V, MLA_QK = 8, 384, 256, 64, 32, 64, 96
ROPE_THETA = 10000.0
MIX = 2048
DFF, FFN_K = 2816, 3
DEPTH = 2
ADAM_LR, ADAM_B1, ADAM_B2, ADAM_EPS, ADAM_WD, ADAM_STEP = 0.001, 0.9, 0.999, 1e-08, 0.01, 10

Z0, XBC0, U0, CQ0, CKV0, DT0, KPE0, PW = 0, 1024, 2560, 3072, 3584, 3840, 3968, 4096
LANES = 128
YCAT = SSD_W + POOL_W + MLA_H * LANES
NEG = -1e30
VMEM_LIMIT = 56 * 1024 * 1024
MM_ROW_TILE, MM_LANE_TILE, MM_FULL_K = 1024, 1408, 2816


def _tile(n, pref, mult):
    if n <= pref:
        return n
    for d in range(pref, 0, -mult):
        if d % mult == 0 and n % d == 0:
            return d
    return n


def _dg(a, b, ca, cb, prec=None):
    return lax.dot_general(a, b, (((ca,), (cb,)), ((), ())), preferred_element_type=F32, precision=prec)


def _nn(a, b):
    return _dg(a.astype(BF16), b.astype(BF16), 1, 0)


def _nt(a, b):
    return _dg(a.astype(BF16), b.astype(BF16), 1, 1)


def _tn(a, b):
    return _dg(a.astype(BF16), b.astype(BF16), 0, 0)


def _sig(x):
    return jax.nn.sigmoid(x)


def _silu(x):
    return x * _sig(x)


def _dsilu(x):
    s = _sig(x)
    return s * (1.0 + x * (1.0 - s))


ANY = pl.BlockSpec(memory_space=pl.ANY)


def _pc(body, *, name, grid, in_specs, out_specs, out_shape, scratch=(), comm=None):
    params = pltpu.CompilerParams(vmem_limit_bytes=VMEM_LIMIT)
    if comm is None:
        return pl.pallas_call(body, name=name, grid=grid, in_specs=in_specs, out_specs=out_specs, out_shape=out_shape,
                              scratch_shapes=list(scratch), compiler_params=params)
    single = not isinstance(out_shape, (list, tuple))
    o_specs = [out_specs] if single else list(out_specs)
    o_shape = [out_shape] if single else list(out_shape)
    ni, no, ns = len(in_specs), len(o_specs), len(scratch)
    nci, nco = len(comm["ins"]), len(comm["out_shape"])

    def fused(*refs):
        ins, cins = refs[:ni], refs[ni:ni + nci]
        outs, couts = refs[ni + nci:ni + nci + no], refs[ni + nci + no:ni + nci + no + nco]
        scr = refs[ni + nci + no + nco:ni + nci + no + nco + ns]
        send_sems, recv_sems = refs[-2:]
        copies = comm["copies"](cins, couts, send_sems, recv_sems)
        first = functools.reduce(jnp.logical_and, [pl.program_id(d) == 0 for d in range(len(grid))])
        last = functools.reduce(jnp.logical_and, [pl.program_id(d) == grid[d] - 1 for d in range(len(grid))])

        @pl.when(first)
        def _():
            for cp in copies:
                cp.start()

        body(*ins, *outs, *scr)

        @pl.when(last)
        def _():
            for cp in copies:
                cp.wait()

    call = pl.pallas_call(
        fused, name=name, grid=grid, in_specs=list(in_specs) + [ANY] * nci, out_specs=o_specs + [ANY] * nco,
        out_shape=o_shape + list(comm["out_shape"]),
        scratch_shapes=list(scratch) + [pltpu.SemaphoreType.DMA((comm["sems"],)), pltpu.SemaphoreType.DMA((comm["sems"],))],
        input_output_aliases={ni + a: no + a for a in range(comm.get("alias", 0))},
        compiler_params=params)

    def run(*args):
        res = call(*args, *comm["ins"])
        comm["result"] = list(res[no:])
        return res[0] if single else list(res[:no])

    return run


def _rsum(x):
    return jnp.sum(x, axis=1, keepdims=True)


def _csum(x):
    return jnp.sum(x, axis=0, keepdims=True)


def mm(a, b, *, ta=False, tb=False, add=None, out_dtype=F32, b_k0=0, comm=None, name):
    M, K = (a.shape[1], a.shape[0]) if ta else a.shape
    N = b.shape[0] if tb else b.shape[1]
    assert tb or b_k0 == 0
    tm = _tile(M, MM_LANE_TILE, LANES) if ta else _tile(M, MM_ROW_TILE, 8)
    tn = _tile(N, MM_LANE_TILE, LANES)
    if ta:
        tk = _tile(K, MM_ROW_TILE, 8)
    else:
        tk = K if K <= MM_FULL_K else _tile(K, 2048, LANES)
    nk = K // tk

    def body(*refs):
        if add is None:
            a_ref, b_ref, o_ref = refs[:3]
        else:
            a_ref, b_ref, add_ref, o_ref = refs[:4]
        part = _dg(a_ref[...].astype(BF16), b_ref[...].astype(BF16), 0 if ta else 1, 1 if tb else 0)

        def finish(r):
            if add is not None:
                r = r + add_ref[...].astype(F32)
            o_ref[...] = r.astype(out_dtype)

        if nk == 1:
            finish(part)
            return
        acc = refs[-1]
        k = pl.program_id(2)

        @pl.when(k == 0)
        def _():
            acc[...] = part

        @pl.when(k > 0)
        def _():
            acc[...] += part

        @pl.when(k == nk - 1)
        def _():
            finish(acc[...])

    a_spec = pl.BlockSpec((tk, tm), lambda i, j, k: (k, i)) if ta else pl.BlockSpec((tm, tk), lambda i, j, k: (i, k))
    assert b_k0 % tk == 0
    kb0 = b_k0 // tk
    b_spec = pl.BlockSpec((tn, tk), lambda i, j, k: (j, kb0 + k)) if tb else pl.BlockSpec((tk, tn), lambda i, j, k: (k, j))
    o_spec = pl.BlockSpec((tm, tn), lambda i, j, k: (i, j))
    ins, specs = [a, b], [a_spec, b_spec]
    if add is not None:
        ins.append(add)
        specs.append(o_spec)
    return _pc(body, name=name, grid=(M // tm, N // tn, nk), in_specs=specs, out_specs=o_spec,
               out_shape=SDS((M, N), out_dtype), scratch=[pltpu.VMEM((tm, tn), F32)] if nk > 1 else [], comm=comm)(*ins)


def rms_fwd(x, g, *, col0=0, width=None, name):
    T = x.shape[0]
    W = width or x.shape[1]
    tm = _tile(T, 512, 8)

    def body(x_ref, g_ref, o_ref):
        v = x_ref[...]
        r = lax.rsqrt(jnp.mean(v * v, axis=1, keepdims=True) + EPS)
        o_ref[...] = ((v * r) * g_ref[...]).astype(BF16)

    return _pc(body, name=name, grid=(T // tm,),
               in_specs=[pl.BlockSpec((tm, W), lambda i: (i, col0 // W)), pl.BlockSpec((1, W), lambda i: (0, 0))],
               out_specs=pl.BlockSpec((tm, W), lambda i: (i, 0)), out_shape=SDS((T, W), BF16))(x, g)


def rms_bwd(x, g, dh, *, col0=0, width=None, add=None, name):
    T = x.shape[0]
    W = width or x.shape[1]
    tm = _tile(T, 512, 8)

    def body(*refs):
        if add is None:
            x_ref, g_ref, dh_ref, dx_ref, dg_ref = refs
        else:
            x_ref, g_ref, dh_ref, add_ref, dx_ref, dxb_ref, dg_ref = refs
        v = x_ref[...]
        r = lax.rsqrt(jnp.mean(v * v, axis=1, keepdims=True) + EPS)
        xh = v * r
        d = dh_ref[...].astype(F32)
        dxh = d * g_ref[...]
        dx = r * (dxh - xh * jnp.mean(dxh * xh, axis=1, keepdims=True))
        if add is not None:
            dx = dx + add_ref[...]
            dxb_ref[...] = dx.astype(BF16)
        dx_ref[...] = dx.astype(dx_ref.dtype)

        @pl.when(pl.program_id(0) == 0)
        def _():
            dg_ref[...] = jnp.zeros_like(dg_ref)

        dg_ref[...] += _csum(d * xh)

    row = pl.BlockSpec((tm, W), lambda i: (i, 0))
    vec = pl.BlockSpec((1, W), lambda i: (0, 0))
    ins = [x, g, dh] + ([] if add is None else [add])
    specs = [pl.BlockSpec((tm, W), lambda i: (i, col0 // W)), vec, row] + ([] if add is None else [row])
    if add is None:
        return _pc(body, name=name, grid=(T // tm,), in_specs=specs, out_specs=[row, vec],
                   out_shape=[SDS((T, W), BF16), SDS((1, W), F32)])(*ins)
    return _pc(body, name=name, grid=(T // tm,), in_specs=specs, out_specs=[row, row, vec],
               out_shape=[SDS((T, W), F32), SDS((T, W), BF16), SDS((1, W), F32)])(*ins)


def gated_rms_fwd(y, proj, g, *, name):
    T = y.shape[0]
    tm = _tile(T, 512, 8)

    def body(y_ref, z_ref, g_ref, o_ref):
        v = y_ref[...] * _silu(z_ref[...])
        r = lax.rsqrt(jnp.mean(v * v, axis=1, keepdims=True) + EPS)
        o_ref[...] = ((v * r) * g_ref[...]).astype(BF16)

    row = pl.BlockSpec((tm, SSD_W), lambda i: (i, 0))
    return _pc(body, name=name, grid=(T // tm,), in_specs=[row, row, pl.BlockSpec((1, SSD_W), lambda i: (0, 0))],
               out_specs=row, out_shape=SDS((T, SSD_W), BF16))(y, proj, g)


def gated_rms_bwd(y, proj, g, dycat, *, name):
    T = y.shape[0]
    tm = _tile(T, 512, 8)

    def body(y_ref, z_ref, g_ref, d_ref, dy_ref, dz_ref, dg_ref):
        yv, z = y_ref[...], z_ref[...]
        sg = _sig(z)
        sz = z * sg
        v = yv * sz
        r = lax.rsqrt(jnp.mean(v * v, axis=1, keepdims=True) + EPS)
        vh = v * r
        d = d_ref[...]
        dvh = d * g_ref[...]
        dv = r * (dvh - vh * jnp.mean(dvh * vh, axis=1, keepdims=True))
        dy_ref[...] = dv * sz
        dz_ref[...] = (dv * yv * (sg + sz * (1.0 - sg))).astype(BF16)

        @pl.when(pl.program_id(0) == 0)
        def _():
            dg_ref[...] = jnp.zeros_like(dg_ref)

        dg_ref[...] += _csum(d * vh)

    row = pl.BlockSpec((tm, SSD_W), lambda i: (i, 0))
    vec = pl.BlockSpec((1, SSD_W), lambda i: (0, 0))
    return _pc(body, name=name, grid=(T // tm,), in_specs=[row, row, vec, row], out_specs=[row, row, vec],
               out_shape=[SDS((T, SSD_W), F32), SDS((T, SSD_W), BF16), SDS((1, SSD_W), F32)])(y, proj, g, dycat)


def final_loss(x, g, tgt, *, name):
    T = x.shape[0]
    tm = _tile(T, 512, 8)

    def body(x_ref, g_ref, t_ref, dx_ref, dxb_ref, dg_ref, l_ref):
        v = x_ref[...]
        gg = g_ref[...]
        r = lax.rsqrt(jnp.mean(v * v, axis=1, keepdims=True) + EPS)
        xh = v * r
        err = xh * gg - t_ref[...]
        part = 0.5 * _csum(jnp.mean(err * err, axis=1, keepdims=True))
        d = err * (1.0 / D)
        dxh = d * gg
        dx = r * (dxh - xh * jnp.mean(dxh * xh, axis=1, keepdims=True))
        dx_ref[...] = dx
        dxb_ref[...] = dx.astype(BF16)

        @pl.when(pl.program_id(0) == 0)
        def _():
            dg_ref[...] = jnp.zeros_like(dg_ref)
            l_ref[...] = jnp.zeros_like(l_ref)

        dg_ref[...] += _csum(d * xh)
        l_ref[...] += jnp.broadcast_to(part, (1, LANES))

    row = pl.BlockSpec((tm, D), lambda i: (i, 0))
    vec = pl.BlockSpec((1, D), lambda i: (0, 0))
    return _pc(body, name=name, grid=(T // tm,), in_specs=[row, vec, row],
               out_specs=[row, row, vec, pl.BlockSpec((1, LANES), lambda i: (0, 0))],
               out_shape=[SDS((T, D), F32), SDS((T, D), BF16), SDS((1, D), F32), SDS((1, LANES), F32)])(x, g, tgt)


HALO = 8


def _prev_map(ts, col):
    return lambda b, i, j: (b, jnp.maximum(i * (ts // HALO) - 1, 0), col(j))


def _next_map(ts, n_halo_blocks, col):
    return lambda b, i, j: (b, jnp.minimum((i + 1) * (ts // HALO), n_halo_blocks - 1), col(j))


def _row_chunks(ts, rows):
    rows = min(rows, ts)
    return [(r, rows) for r in range(0, ts, rows)]


def _conv_rows(ext, w_ref, b_ref, r0, n, K):
    win = ext[r0:r0 + HALO + n, :]
    taps = [pltpu.roll(win, K - 1 - k, 0)[HALO:HALO + n] if k < K - 1 else win[HALO:HALO + n] for k in range(K)]
    acc = b_ref[...] + w_ref[0:1, :] * taps[0]
    for k in range(1, K):
        acc = acc + w_ref[k:k + 1, :] * taps[k]
    return acc, taps


def _conv_t_rows(ext2, w_ref, r0, n, K):
    win = ext2[r0:r0 + n + HALO, :]
    dx = w_ref[K - 1:K, :] * win[0:n]
    for k in range(K - 1):
        dx = dx + w_ref[k:k + 1, :] * pltpu.roll(win, n + HALO - (K - 1 - k), 0)[0:n]
    return dx


def _sum8(x):
    acc = x[0:8]
    for r in range(8, x.shape[0], 8):
        acc = acc + x[r:r + 8]
    return acc


def conv_silu_fwd(proj3, w, b, *, name):
    Bl, S, _ = proj3.shape
    C, K = SSD_XBC, SSD_K
    ts, tc = _tile(S, 512, 8), 512
    c0 = XBC0 // tc

    def body(xp_ref, x_ref, w_ref, b_ref, o_ref, ext):
        i = pl.program_id(1)
        ext[0:HALO, :] = jnp.where(i > 0, xp_ref[0], 0.0)
        ext[HALO:HALO + ts, :] = x_ref[0]
        for r0, n in _row_chunks(ts, 32):
            acc, _ = _conv_rows(ext, w_ref, b_ref, r0, n, K)
            o_ref[0, r0:r0 + n, :] = _silu(acc)

    return _pc(body, name=name, grid=(Bl, S // ts, C // tc),
               in_specs=[pl.BlockSpec((1, HALO, tc), _prev_map(ts, lambda j: c0 + j)),
                         pl.BlockSpec((1, ts, tc), lambda b, i, j: (b, i, c0 + j)),
                         pl.BlockSpec((K, tc), lambda b, i, j: (0, j)),
                         pl.BlockSpec((1, tc), lambda b, i, j: (0, j))],
               out_specs=pl.BlockSpec((1, ts, tc), lambda b, i, j: (b, i, j)),
               out_shape=SDS((Bl, S, C), F32), scratch=[pltpu.VMEM((HALO + ts, tc), F32)])(proj3, proj3, w, b)


def conv_silu_bwd(proj3, w, b, dact, *, comm=None, name):
    Bl, S, _ = proj3.shape
    C, K = SSD_XBC, SSD_K
    ts, tc = _tile(S, 512, 8), 512
    c0 = XBC0 // tc
    ns = S // ts

    def body(xp_ref, x_ref, xn_ref, d_ref, dn_ref, w_ref, b_ref, dx_ref, dw_ref, db_ref, ext, ext2):
        bb, i = pl.program_id(1), pl.program_id(2)
        last = i == ns - 1
        ext[0:HALO, :] = jnp.where(i > 0, xp_ref[0], 0.0)
        ext[HALO:HALO + ts, :] = x_ref[0]
        ext[HALO + ts:2 * HALO + ts, :] = jnp.where(last, 0.0, xn_ref[0])
        dw = [jnp.zeros((8, tc), F32) for _ in range(K)]
        db = jnp.zeros((8, tc), F32)
        for r0, n in _row_chunks(ts, 16) + [(ts, HALO)]:
            acc, taps = _conv_rows(ext, w_ref, b_ref, r0, n, K)
            d = d_ref[0, r0:r0 + n, :] if r0 < ts else jnp.where(last, 0.0, dn_ref[0])
            du = d * _dsilu(acc)
            ext2[r0:r0 + n, :] = du
            if r0 < ts:
                dw = [a + _sum8(du * t) for a, t in zip(dw, taps)]
                db = db + _sum8(du)
        for r0, n in _row_chunks(ts, 32):
            dx_ref[0, r0:r0 + n, :] = _conv_t_rows(ext2, w_ref, r0, n, K).astype(BF16)

        @pl.when((bb == 0) & (i == 0))
        def _():
            dw_ref[...] = jnp.zeros_like(dw_ref)
            db_ref[...] = jnp.zeros_like(db_ref)

        for k in range(K):
            dw_ref[k:k + 1, :] += _csum(dw[k])
        db_ref[...] += _csum(db)

    nhb = S // HALO
    cx = lambda j: c0 + j
    cj = lambda j: j
    return _pc(body, name=name, grid=(C // tc, Bl, ns),
               in_specs=[pl.BlockSpec((1, HALO, tc), lambda j, b, i: _prev_map(ts, cx)(b, i, j)),
                         pl.BlockSpec((1, ts, tc), lambda j, b, i: (b, i, c0 + j)),
                         pl.BlockSpec((1, HALO, tc), lambda j, b, i: _next_map(ts, nhb, cx)(b, i, j)),
                         pl.BlockSpec((1, ts, tc), lambda j, b, i: (b, i, j)),
                         pl.BlockSpec((1, HALO, tc), lambda j, b, i: _next_map(ts, nhb, cj)(b, i, j)),
                         pl.BlockSpec((K, tc), lambda j, b, i: (0, j)),
                         pl.BlockSpec((1, tc), lambda j, b, i: (0, j))],
               out_specs=[pl.BlockSpec((1, ts, tc), lambda j, b, i: (b, i, j)),
                          pl.BlockSpec((K, tc), lambda j, b, i: (0, j)),
                          pl.BlockSpec((1, tc), lambda j, b, i: (0, j))],
               out_shape=[SDS((Bl, S, C), BF16), SDS((K, C), F32), SDS((1, C), F32)],
               scratch=[pltpu.VMEM((2 * HALO + ts, tc), F32), pltpu.VMEM((HALO + ts, tc), F32)],
               comm=comm)(proj3, proj3, proj3, dact, dact, w, b)


def ffn_act_fwd(pre3, w, b, *, name):
    Bl, S, _ = pre3.shape
    K = FFN_K
    ts, tc = _tile(S, 512, 8), 256
    nj = DFF // tc

    def body(gp_ref, g_ref, vp_ref, v_ref, wg_ref, wv_ref, bg_ref, bv_ref, o_ref, eg, ev):
        i = pl.program_id(1)
        for p_ref, m_ref, ext in ((gp_ref, g_ref, eg), (vp_ref, v_ref, ev)):
            ext[0:HALO, :] = jnp.where(i > 0, p_ref[0], 0.0)
            ext[HALO:HALO + ts, :] = m_ref[0]
        for r0, n in _row_chunks(ts, 64):
            ug, _ = _conv_rows(eg, wg_ref, bg_ref, r0, n, K)
            uv, _ = _conv_rows(ev, wv_ref, bv_ref, r0, n, K)
            o_ref[0, r0:r0 + n, :] = (_silu(ug) * uv).astype(BF16)

    main = lambda off: pl.BlockSpec((1, ts, tc), lambda b, i, j: (b, i, off + j))
    prev = lambda off: pl.BlockSpec((1, HALO, tc), _prev_map(ts, lambda j: off + j))
    wsp = lambda off: pl.BlockSpec((K, tc), lambda b, i, j: (0, off + j))
    bsp = lambda off: pl.BlockSpec((1, tc), lambda b, i, j: (0, off + j))
    return _pc(body, name=name, grid=(Bl, S // ts, nj),
               in_specs=[prev(0), main(0), prev(nj), main(nj), wsp(0), wsp(nj), bsp(0), bsp(nj)],
               out_specs=pl.BlockSpec((1, ts, tc), lambda b, i, j: (b, i, j)),
               out_shape=SDS((Bl, S, DFF), BF16),
               scratch=[pltpu.VMEM((HALO + ts, tc), F32), pltpu.VMEM((HALO + ts, tc), F32)],
               )(pre3, pre3, pre3, pre3, w, w, b, b)


def ffn_act_bwd(pre3, w, b, dact, *, comm=None, name):
    Bl, S, _ = pre3.shape
    K = FFN_K
    ts, tc = _tile(S, 512, 8), 256
    nj = DFF // tc
    ns = S // ts

    def body(gp_ref, g_ref, gn_ref, vp_ref, v_ref, vn_ref, d_ref, dn_ref, wg_ref, wv_ref, bg_ref, bv_ref,
             dg_ref, dv_ref, dwg_ref, dwv_ref, dbg_ref, dbv_ref, eg, ev, e2g, e2v):
        bb, i = pl.program_id(1), pl.program_id(2)
        last = i == ns - 1
        for p_ref, m_ref, n_ref, ext in ((gp_ref, g_ref, gn_ref, eg), (vp_ref, v_ref, vn_ref, ev)):
            ext[0:HALO, :] = jnp.where(i > 0, p_ref[0], 0.0)
            ext[HALO:HALO + ts, :] = m_ref[0]
            ext[HALO + ts:2 * HALO + ts, :] = jnp.where(last, 0.0, n_ref[0])
        zero8 = jnp.zeros((8, tc), F32)
        dwg, dwv, dbg, dbv = [zero8] * K, [zero8] * K, zero8, zero8
        for r0, n in _row_chunks(ts, 32) + [(ts, HALO)]:
            ug, tg = _conv_rows(eg, wg_ref, bg_ref, r0, n, K)
            uv, tv = _conv_rows(ev, wv_ref, bv_ref, r0, n, K)
            d = d_ref[0, r0:r0 + n, :] if r0 < ts else jnp.where(last, 0.0, dn_ref[0])
            sg = _sig(ug)
            act = ug * sg
            dug = d * uv * (sg + act * (1.0 - sg))
            duv = d * act
            e2g[r0:r0 + n, :] = dug
            e2v[r0:r0 + n, :] = duv
            if r0 < ts:
                dwg = [a + _sum8(dug * t) for a, t in zip(dwg, tg)]
                dwv = [a + _sum8(duv * t) for a, t in zip(dwv, tv)]
                dbg, dbv = dbg + _sum8(dug), dbv + _sum8(duv)
        for w_ref, e2, o_ref in ((wg_ref, e2g, dg_ref), (wv_ref, e2v, dv_ref)):
            for r0, n in _row_chunks(ts, 64):
                o_ref[0, r0:r0 + n, :] = _conv_t_rows(e2, w_ref, r0, n, K).astype(BF16)

        @pl.when((bb == 0) & (i == 0))
        def _():
            for r in (dwg_ref, dwv_ref, dbg_ref, dbv_ref):
                r[...] = jnp.zeros_like(r)

        for dw_ref, dw, db_ref, db in ((dwg_ref, dwg, dbg_ref, dbg), (dwv_ref, dwv, dbv_ref, dbv)):
            for k in range(K):
                dw_ref[k:k + 1, :] += _csum(dw[k])
            db_ref[...] += _csum(db)

    nhb = S // HALO
    main = lambda off: pl.BlockSpec((1, ts, tc), lambda j, b, i: (b, i, off + j))
    prev = lambda off: pl.BlockSpec((1, HALO, tc), lambda j, b, i: _prev_map(ts, lambda jj: off + jj)(b, i, j))
    nxt = lambda off: pl.BlockSpec((1, HALO, tc), lambda j, b, i: _next_map(ts, nhb, lambda jj: off + jj)(b, i, j))
    wsp = lambda off: pl.BlockSpec((K, tc), lambda j, b, i: (0, off + j))
    bsp = lambda off: pl.BlockSpec((1, tc), lambda j, b, i: (0, off + j))
    outs = _pc(body, name=name, grid=(nj, Bl, ns),
               in_specs=[prev(0), main(0), nxt(0), prev(nj), main(nj), nxt(nj), main(0), nxt(0),
                         wsp(0), wsp(nj), bsp(0), bsp(nj)],
               out_specs=[main(0), main(0), wsp(0), wsp(0), bsp(0), bsp(0)],
               out_shape=[SDS((Bl, S, DFF), BF16), SDS((Bl, S, DFF), BF16), SDS((K, DFF), F32), SDS((K, DFF), F32),
                          SDS((1, DFF), F32), SDS((1, DFF), F32)],
               scratch=[pltpu.VMEM((2 * HALO + ts, tc), F32), pltpu.VMEM((2 * HALO + ts, tc), F32),
                        pltpu.VMEM((HALO + ts, tc), F32), pltpu.VMEM((HALO + ts, tc), F32)],
               comm=comm)(pre3, pre3, pre3, pre3, pre3, pre3, dact, dact, w, w, b, b)
    return outs


PHALO = 16


def _pool_window_sums(win, trailing):
    rows = win.shape[0]
    out, s = [], win
    for w in POOL_WIN:
        half = w // 2
        s = s + pltpu.roll(s, half if trailing else rows - half, 0)
        out.append(s)
    return out


def _pick(g, vals):
    r = vals[-1]
    for k in range(len(vals) - 2, -1, -1):
        r = jnp.where(g == k, vals[k], r)
    return r


def _pool_count(g, i, ts, rows, r0=0):
    t = (i * ts + r0 + lax.broadcasted_iota(jnp.int32, (rows, 1), 0) + 1).astype(F32)
    return jnp.minimum(t, _pick(g, [float(w) for w in POOL_WIN]))


def _fill_pool_ext(up_ref, u_ref, ext, i, ts):
    ext[0:PHALO, :] = jnp.where(i > 0, up_ref[0], 0.0)
    ext[PHALO:PHALO + ts, :] = u_ref[0]


def _pooled_rows(ext, g, i, ts, r0, n):
    win = ext[r0:r0 + n + PHALO, :]
    sums = _pool_window_sums(win, True)
    return _pick(g, sums)[PHALO:PHALO + n] / _pool_count(g, i, ts, n, r0) - win[PHALO:PHALO + n]


def pool_fwd(proj3, pool_w, scale, *, name):
    Bl, S, _ = proj3.shape
    ts = _tile(S, 512, 16)
    c0 = U0 // POOL_D

    def body(up_ref, u_ref, w_ref, s_ref, o_ref, ext):
        i, g = pl.program_id(1), pl.program_id(2)
        _fill_pool_ext(up_ref, u_ref, ext, i, ts)
        wm, sc = w_ref[0], s_ref[...]
        for r0, n in _row_chunks(ts, 128):
            o_ref[0, r0:r0 + n, :] = (_nn(_pooled_rows(ext, g, i, ts, r0, n), wm) * sc).astype(BF16)

    return _pc(body, name=name, grid=(Bl, S // ts, POOL_G),
               in_specs=[pl.BlockSpec((1, PHALO, POOL_D), lambda b, i, g: (b, jnp.maximum(i * (ts // PHALO) - 1, 0), c0 + g)),
                         pl.BlockSpec((1, ts, POOL_D), lambda b, i, g: (b, i, c0 + g)),
                         pl.BlockSpec((1, POOL_D, POOL_D), lambda b, i, g: (g, 0, 0)),
                         pl.BlockSpec((1, POOL_D), lambda b, i, g: (0, g))],
               out_specs=pl.BlockSpec((1, ts, POOL_D), lambda b, i, g: (b, i, g)),
               out_shape=SDS((Bl, S, POOL_W), BF16), scratch=[pltpu.VMEM((PHALO + ts, POOL_D), F32)],
               )(proj3, proj3, pool_w, scale)


def pool_bwd(proj3, pool_w, scale, dycat3, *, name):
    Bl, S, _ = proj3.shape
    ts = _tile(S, 512, 16)
    ns = S // ts
    c0 = U0 // POOL_D
    d0 = SSD_W // POOL_D
    nhb = S // PHALO

    def body(up_ref, u_ref, d_ref, dn_ref, w_ref, s_ref, du_ref, dw_ref, ds_ref, ext, ext2):
        g, bb, i = pl.program_id(0), pl.program_id(1), pl.program_id(2)
        last = i == ns - 1
        _fill_pool_ext(up_ref, u_ref, ext, i, ts)
        wm = w_ref[0]
        sc = s_ref[...]
        dwa = jnp.zeros((POOL_D, POOL_D), F32)
        dsa = jnp.zeros((8, POOL_D), F32)
        dpools = []
        for r0, n in _row_chunks(ts, 128):
            pooled = _pooled_rows(ext, g, i, ts, r0, n)
            dy = d_ref[0, r0:r0 + n, :]
            dp = dy * sc
            dpool = _nt(dp, wm)
            dpools.append(dpool)
            ext2[r0:r0 + n, :] = dpool / _pool_count(g, i, ts, n, r0)
            dwa = dwa + _tn(pooled, dp)
            dsa = dsa + _sum8(dy * _nn(pooled, wm))
        dpool_n = _nt(jnp.where(last, 0.0, dn_ref[0]) * sc, wm)
        ext2[ts:ts + PHALO, :] = dpool_n / _pool_count(g, i + 1, ts, PHALO)
        for (r0, n), dpool in zip(_row_chunks(ts, 128), dpools):
            sums = _pool_window_sums(ext2[r0:r0 + n + PHALO, :], False)
            du_ref[0, r0:r0 + n, :] = (_pick(g, sums)[0:n] - dpool).astype(BF16)

        @pl.when((bb == 0) & (i == 0))
        def _():
            dw_ref[...] = jnp.zeros_like(dw_ref)
            ds_ref[...] = jnp.zeros_like(ds_ref)

        dw_ref[0] += dwa
        ds_ref[...] += _csum(dsa)

    return _pc(body, name=name, grid=(POOL_G, Bl, ns),
               in_specs=[pl.BlockSpec((1, PHALO, POOL_D), lambda g, b, i: (b, jnp.maximum(i * (ts // PHALO) - 1, 0), c0 + g)),
                         pl.BlockSpec((1, ts, POOL_D), lambda g, b, i: (b, i, c0 + g)),
                         pl.BlockSpec((1, ts, POOL_D), lambda g, b, i: (b, i, d0 + g)),
                         pl.BlockSpec((1, PHALO, POOL_D), lambda g, b, i: (b, jnp.minimum((i + 1) * (ts // PHALO), nhb - 1), d0 + g)),
                         pl.BlockSpec((1, POOL_D, POOL_D), lambda g, b, i: (g, 0, 0)),
                         pl.BlockSpec((1, POOL_D), lambda g, b, i: (0, g))],
               out_specs=[pl.BlockSpec((1, ts, POOL_D), lambda g, b, i: (b, i, g)),
                          pl.BlockSpec((1, POOL_D, POOL_D), lambda g, b, i: (g, 0, 0)),
                          pl.BlockSpec((1, POOL_D), lambda g, b, i: (0, g))],
               out_shape=[SDS((Bl, S, POOL_W), BF16), SDS((POOL_G, POOL_D, POOL_D), F32), SDS((1, POOL_W), F32)],
               scratch=[pltpu.VMEM((PHALO + ts, POOL_D), F32), pltpu.VMEM((PHALO + ts, POOL_D), F32)],
               )(proj3, proj3, dycat3, dycat3, pool_w, scale)


NPAIR = SSD_HEADS // 2


def _ssd_common(sm, bias, alog):
    L = SSD_L
    dt = jax.nn.softplus(sm + bias)
    a = -jnp.exp(alog)
    da = dt * a
    r = lax.broadcasted_iota(jnp.int32, (L, L), 0)
    c = lax.broadcasted_iota(jnp.int32, (L, L), 1)
    tri = (r >= c).astype(F32)
    cum = _dg(tri, da, 1, 0, lax.Precision.HIGHEST)
    return dt, a, cum, cum.T, r >= c


def _lanes(lo, hi, shape=(1, LANES)):
    lane = lax.broadcasted_iota(jnp.int32, shape, len(shape) - 1)
    return (lane >= lo) & (lane < hi)


def _onehot_lane(h):
    return (lax.broadcasted_iota(jnp.int32, (1, LANES), 1) == h).astype(F32)


def _split_nn(a, e):
    hi = a.astype(BF16)
    lo = (a - hi.astype(F32)).astype(BF16)
    return _dg(hi, e, 1, 0) + _dg(lo, e, 1, 0)


def _head_spread():
    r = lax.broadcasted_iota(jnp.int32, (LANES, SSD_W), 0)
    c = lax.broadcasted_iota(jnp.int32, (LANES, SSD_W), 1)
    return (c // SSD_P == r).astype(BF16)


def _pair_gather(j):
    r = lax.broadcasted_iota(jnp.int32, (LANES, LANES), 0)
    c = lax.broadcasted_iota(jnp.int32, (LANES, LANES), 1)
    return (c == 2 * j + (r >= SSD_P).astype(jnp.int32)).astype(BF16)


def ssd_fwd(xbc3, proj3, bias, alog, dskip, *, comm=None, name):
    Bl, S, _ = xbc3.shape
    L = SSD_L
    nc = S // L

    def body(xbc_ref, sm_ref, bias_ref, alog_ref, d_ref, y_ref, hin_ref, H):
        c = pl.program_id(1)

        @pl.when(c == 0)
        def _():
            H[...] = jnp.zeros_like(H)

        dt, a, cum, cumT, mask = _ssd_common(sm_ref[0], bias_ref[...], alog_ref[...])
        lo = _lanes(0, SSD_P)
        rowlo = lax.broadcasted_iota(jnp.int32, (LANES, LANES), 0) < SSD_P
        spread = _head_spread()
        dt_x = _split_nn(dt, spread)
        el_x = _split_nn(jnp.exp(cum), spread)
        wl_x = _split_nn(jnp.exp(cum[L - 1:L, :] - cum), spread)
        cb = []
        for g in range(SSD_G):
            Bg = xbc_ref[0, :, SSD_W + g * SSD_N:SSD_W + (g + 1) * SSD_N]
            Cg = xbc_ref[0, :, SSD_W + SSD_G * SSD_N + g * SSD_N:SSD_W + SSD_G * SSD_N + (g + 1) * SSD_N]
            cb.append((Bg, Cg, _nt(Cg, Bg)))
        for j in range(NPAIR):
            h0, h1 = 2 * j, 2 * j + 1
            sl = slice(j * LANES, (j + 1) * LANES)
            Bg, Cg, CB = cb[j // (NPAIR // SSD_G)]
            X = xbc_ref[0, :, sl]
            c0, c1 = cum[:, h0:h0 + 1], cum[:, h1:h1 + 1]
            r0, r1 = cumT[h0:h0 + 1, :], cumT[h1:h1 + 1, :]
            cl0, cl1 = cum[L - 1:L, h0:h0 + 1], cum[L - 1:L, h1:h1 + 1]
            Xt = X * dt_x[:, sl]
            M0 = CB * jnp.exp(jnp.where(mask, c0 - r0, NEG))
            M1 = CB * jnp.exp(jnp.where(mask, c1 - r1, NEG))
            Yd = jnp.where(lo, _nn(M0, Xt), _nn(M1, Xt))
            Hp = H[j]
            hin_ref[0, 0, j] = Hp
            Z = _nt(Cg, Hp)
            y_ref[0, :, sl] = Yd + el_x[:, sl] * Z + X * d_ref[j:j + 1, :]
            H[j] = jnp.where(rowlo, jnp.exp(cl0), jnp.exp(cl1)) * Hp + _tn(wl_x[:, sl] * Xt, Bg)

    vec = pl.BlockSpec((1, LANES), lambda b, c: (0, 0))
    return _pc(body, name=name, grid=(Bl, nc),
               in_specs=[pl.BlockSpec((1, L, SSD_XBC), lambda b, c: (b, c, 0)),
                         pl.BlockSpec((1, L, LANES), lambda b, c: (b, c, DT0 // LANES)),
                         vec, vec, pl.BlockSpec((NPAIR, LANES), lambda b, c: (0, 0))],
               out_specs=[pl.BlockSpec((1, L, SSD_W), lambda b, c: (b, c, 0)),
                          pl.BlockSpec((1, 1, NPAIR, LANES, LANES), lambda b, c: (b, c, 0, 0, 0))],
               out_shape=[SDS((Bl, S, SSD_W), F32), SDS((Bl, nc, NPAIR, LANES, LANES), F32)],
               scratch=[pltpu.VMEM((NPAIR, LANES, LANES), F32)], comm=comm)(xbc3, proj3, bias, alog, dskip)


def ssd_bwd(xbc3, proj3, hin, dy3, bias, alog, dskip, *, comm=None, name):
    Bl, S, _ = xbc3.shape
    L = SSD_L
    nc = S // L

    def body(xbc_ref, sm_ref, hin_ref, dy_ref, bias_ref, alog_ref, d_ref, dx_ref, ddt_ref, dpar_ref, dd_ref, dH, ddacc):
        bb, i = pl.program_id(0), pl.program_id(1)

        @pl.when(i == 0)
        def _():
            dH[...] = jnp.zeros_like(dH)

        @pl.when((bb == 0) & (i == 0))
        def _():
            dpar_ref[...] = jnp.zeros_like(dpar_ref)
            ddacc[...] = jnp.zeros_like(ddacc)

        sm = sm_ref[0]
        dt, a, cum, cumT, mask = _ssd_common(sm, bias_ref[...], alog_ref[...])
        maskf = mask.astype(F32)
        lo = _lanes(0, SSD_P)
        rowlo = lax.broadcasted_iota(jnp.int32, (LANES, LANES), 0) < SSD_P
        lastrow = (lax.broadcasted_iota(jnp.int32, (L, 1), 0) == L - 1).astype(F32)
        dcum = jnp.zeros((L, LANES), F32)
        dcum_t = jnp.zeros((LANES, L), F32)
        ddt = jnp.zeros((L, LANES), F32)
        spread = _head_spread()
        ones = jnp.ones((L, LANES), BF16)
        ecum = jnp.exp(cum)
        wall = jnp.exp(cum[L - 1:L, :] - cum)
        dt_x = _split_nn(dt, spread)
        el_x = _split_nn(ecum, spread)
        wl_x = _split_nn(wall, spread)
        headrow = lax.broadcasted_iota(jnp.int32, (LANES, 1), 0)
        grp = []
        for g in range(SSD_G):
            Bg = xbc_ref[0, :, SSD_W + g * SSD_N:SSD_W + (g + 1) * SSD_N]
            Cg = xbc_ref[0, :, SSD_W + SSD_G * SSD_N + g * SSD_N:SSD_W + SSD_G * SSD_N + (g + 1) * SSD_N]
            grp.append(dict(B=Bg, C=Cg, CB=_nt(Cg, Bg), dB=jnp.zeros((L, SSD_N), F32), dC=jnp.zeros((L, SSD_N), F32),
                            dCB=jnp.zeros((L, L), F32)))
        for j in range(NPAIR):
            h0, h1 = 2 * j, 2 * j + 1
            sl = slice(j * LANES, (j + 1) * LANES)
            G = grp[j // (NPAIR // SSD_G)]
            Bg, Cg, CB = G["B"], G["C"], G["CB"]
            X = xbc_ref[0, :, sl]
            dY = dy_ref[0, :, sl]
            c0, c1 = cum[:, h0:h0 + 1], cum[:, h1:h1 + 1]
            r0, r1 = cumT[h0:h0 + 1, :], cumT[h1:h1 + 1, :]
            cl0, cl1 = cum[L - 1:L, h0:h0 + 1], cum[L - 1:L, h1:h1 + 1]
            oh0, oh1 = _onehot_lane(h0), _onehot_lane(h1)
            gather = _pair_gather(j)
            dtl, el, wl = dt_x[:, sl], el_x[:, sl], wl_x[:, sl]
            Xt = X * dtl
            Hp = hin_ref[0, 0, j]
            dS = dH[j]
            dX = dY * d_ref[j:j + 1, :]
            ddacc[j:j + 1, :] += _csum(dY * X)
            Z = _nt(Cg, Hp)
            dZ = dY * el
            dcum = dcum + _split_nn(dY * Z, gather) * ecum
            G["dC"] = G["dC"] + _nn(dZ, Hp)
            dHy = _tn(dZ, Cg)
            Gm = _nt(Bg, dS)
            dXt = wl * Gm
            q = _split_nn(Xt * Gm, gather) * wall
            dcum = dcum + lastrow * _csum(q) - q
            G["dB"] = G["dB"] + _nn(wl * Xt, dS)
            g0, g1 = jnp.exp(cl0), jnp.exp(cl1)
            rowsum = _nn(dS * Hp, ones)
            dg0 = _csum(jnp.where(rowlo, rowsum, 0.0))
            dg1 = _csum(jnp.where(rowlo, 0.0, rowsum))
            dcum = dcum + lastrow * ((dg0 * g0) * oh0 + (dg1 * g1) * oh1)
            dH[j] = jnp.where(rowlo, g0, g1) * dS + dHy
            for h, ch, rh, mh, oh in ((h0, c0, r0, lo, oh0), (h1, c1, r1, jnp.logical_not(lo), oh1)):
                decay = jnp.exp(jnp.where(mask, ch - rh, NEG))
                Mh = CB * decay
                dM = _nt(jnp.where(mh, dY, 0.0), Xt) * maskf
                dXt = dXt + jnp.where(mh, _tn(Mh, dY), 0.0)
                G["dCB"] = G["dCB"] + dM * decay
                Q = dM * Mh
                dcum = dcum + _rsum(Q) * oh
                dcum_t = dcum_t + (headrow == h).astype(F32) * _csum(Q)
            dX = dX + dXt * dtl
            ddt = ddt + _split_nn(dXt * X, gather)
            dx_ref[0, :, sl] = dX
        dcum = dcum - dcum_t.T
        for g in range(SSD_G):
            G = grp[g]
            dC = G["dC"] + _nn(G["dCB"], G["B"])
            dB = G["dB"] + _tn(G["dCB"], G["C"])
            dx_ref[0, :, SSD_W + g * SSD_N:SSD_W + (g + 1) * SSD_N] = dB
            dx_ref[0, :, SSD_W + SSD_G * SSD_N + g * SSD_N:SSD_W + SSD_G * SSD_N + (g + 1) * SSD_N] = dC
        r = lax.broadcasted_iota(jnp.int32, (L, L), 0)
        c = lax.broadcasted_iota(jnp.int32, (L, L), 1)
        dda = _dg((c >= r).astype(F32), dcum, 1, 0, lax.Precision.HIGHEST)
        heads = _lanes(0, SSD_HEADS)
        ddt = ddt + dda * a
        draw = jnp.where(heads, ddt * _sig(sm + bias_ref[...]), 0.0)
        ddt_ref[0] = draw.astype(BF16)
        dpar_ref[0:1, :] += _csum(draw)
        dpar_ref[1:2, :] += _csum(jnp.where(heads, dda * dt * a, 0.0))

        @pl.when((bb == Bl - 1) & (i == nc - 1))
        def _():
            acc = ddacc[...]
            lane = lax.broadcasted_iota(jnp.int32, (NPAIR, LANES), 1)
            s0 = _rsum(jnp.where(lane < SSD_P, acc, 0.0))
            s1 = _rsum(jnp.where(lane < SSD_P, 0.0, acc))
            dd_ref[...] = jnp.where(lane == 0, s0, jnp.where(lane == 1, s1, 0.0))

    vec = pl.BlockSpec((1, LANES), lambda b, i: (0, 0))
    par = pl.BlockSpec((NPAIR, LANES), lambda b, i: (0, 0))
    return _pc(body, name=name, grid=(Bl, nc),
               in_specs=[pl.BlockSpec((1, L, SSD_XBC), lambda b, i: (b, nc - 1 - i, 0)),
                         pl.BlockSpec((1, L, LANES), lambda b, i: (b, nc - 1 - i, DT0 // LANES)),
                         pl.BlockSpec((1, 1, NPAIR, LANES, LANES), lambda b, i: (b, nc - 1 - i, 0, 0, 0)),
                         pl.BlockSpec((1, L, SSD_W), lambda b, i: (b, nc - 1 - i, 0)),
                         vec, vec, par],
               out_specs=[pl.BlockSpec((1, L, SSD_XBC), lambda b, i: (b, nc - 1 - i, 0)),
                          pl.BlockSpec((1, L, LANES), lambda b, i: (b, nc - 1 - i, 0)),
                          par, par],
               out_shape=[SDS((Bl, S, SSD_XBC), F32), SDS((Bl, S, LANES), BF16), SDS((NPAIR, LANES), F32),
                          SDS((NPAIR, LANES), F32)],
               scratch=[pltpu.VMEM((NPAIR, LANES, LANES), F32), pltpu.VMEM((NPAIR, LANES), F32)],
               comm=comm)(xbc3, proj3, hin, dy3, bias, alog, dskip)


PE_LO, PE_MID, PE_HI = MLA_NOPE, MLA_NOPE + MLA_ROPE // 2, MLA_NOPE + MLA_ROPE
ATT_SCALE = 1.0 / math.sqrt(MLA_QK)


def _swap_matrix():
    src = lax.broadcasted_iota(jnp.int32, (LANES, LANES), 0)
    dst = lax.broadcasted_iota(jnp.int32, (LANES, LANES), 1)
    half = MLA_ROPE // 2
    first = (dst >= PE_LO) & (dst < PE_MID) & (src == dst + half)
    second = (dst >= PE_MID) & (dst < PE_HI) & (src == dst - half)
    return (second.astype(F32) - first.astype(F32)).astype(BF16)


def rope_tables(pos, invf, *, name):
    T = pos.shape[0]
    tm = _tile(T, 512, 8)

    def body(pos_ref, f_ref, c_ref, s_ref):
        ang = pos_ref[...] * f_ref[...]
        pe = _lanes(PE_LO, PE_HI)
        c_ref[...] = jnp.where(pe, jnp.cos(ang), 1.0)
        s_ref[...] = jnp.where(pe, jnp.sin(ang), 0.0)

    tile = pl.BlockSpec((tm, LANES), lambda i: (i, 0))
    return _pc(body, name=name, grid=(T // tm,),
               in_specs=[pl.BlockSpec((tm, 1), lambda i: (i, 0)), pl.BlockSpec((1, LANES), lambda i: (0, 0))],
               out_specs=[tile, tile], out_shape=[SDS((T, LANES), F32)] * 2)(pos, invf)


V_ONE = MLA_V


def mla_prep_fwd(qt, kvt, proj, cs, sn, *, name):
    T = qt.shape[0]
    tm = _tile(T, 256, 8)
    HW = MLA_H * LANES

    def body(q_ref, k_ref, v_ref, kpe_ref, c_ref, s_ref, qo_ref, ko_ref, vo_ref):
        c, s = c_ref[...], s_ref[...]
        kpe = kpe_ref[...]
        sw = _swap_matrix()
        one = _lanes(V_ONE, V_ONE + 1)
        for h in range(MLA_H):
            sl = slice(h * LANES, (h + 1) * LANES)
            q = q_ref[:, sl]
            k = k_ref[:, sl] + kpe
            qo_ref[:, sl] = ((q * c + _split_nn(q, sw) * s) * ATT_SCALE).astype(BF16)
            ko_ref[:, sl] = (k * c + _split_nn(k, sw) * s).astype(BF16)
            vo_ref[:, sl] = jnp.where(one, 1.0, v_ref[:, sl]).astype(BF16)

    row = pl.BlockSpec((tm, HW), lambda i: (i, 0))
    tab = pl.BlockSpec((tm, LANES), lambda i: (i, 0))
    return _pc(body, name=name, grid=(T // tm,),
               in_specs=[row, row, pl.BlockSpec((tm, HW), lambda i: (i, 1)),
                         pl.BlockSpec((tm, LANES), lambda i: (i, KPE0 // LANES)), tab, tab],
               out_specs=[row, row, row], out_shape=[SDS((T, HW), BF16)] * 3)(qt, kvt, kvt, proj, cs, sn)


def mla_prep_bwd(dqr, dkr, cs, sn, *, name):
    T = dqr.shape[0]
    tm = _tile(T, 256, 8)
    HW = MLA_H * LANES

    def body(dq_ref, dk_ref, c_ref, s_ref, qo_ref, ko_ref, kpe_ref):
        c, s = c_ref[...], s_ref[...]
        sw = _swap_matrix()
        pe = _lanes(PE_LO, PE_HI)
        dkpe = jnp.zeros((tm, LANES), F32)
        for h in range(MLA_H):
            sl = slice(h * LANES, (h + 1) * LANES)
            dq = dq_ref[:, sl] * ATT_SCALE
            dk = dk_ref[:, sl]
            qo_ref[:, sl] = (dq * c - _split_nn(dq * s, sw)).astype(BF16)
            dkk = dk * c - _split_nn(dk * s, sw)
            ko_ref[:, sl] = jnp.where(pe, 0.0, dkk).astype(BF16)
            dkpe = dkpe + jnp.where(pe, dkk, 0.0)
        kpe_ref[...] = dkpe.astype(BF16)

    row = pl.BlockSpec((tm, HW), lambda i: (i, 0))
    tab = pl.BlockSpec((tm, LANES), lambda i: (i, 0))
    return _pc(body, name=name, grid=(T // tm,), in_specs=[row, row, tab, tab], out_specs=[row, row, tab],
               out_shape=[SDS((T, HW), BF16), SDS((T, HW), BF16), SDS((T, LANES), BF16)])(dqr, dkr, cs, sn)


def _att_tile(S):
    return _tile(S, 512, LANES)


def _rep(x, n):
    return x if n == 1 else jnp.concatenate([x] * n, axis=1)


def _diag_mask(t, transposed=False):
    r = lax.broadcasted_iota(jnp.int32, (t, t), 0)
    c = lax.broadcasted_iota(jnp.int32, (t, t), 1)
    return (c >= r) if transposed else (c <= r)


def flash_fwd(qr, kr, vr, Bl, *, comm=None, name):
    T = qr.shape[0]
    S = T // Bl
    t = _att_tile(S)
    n = S // t
    nl = t // LANES

    def body(q_ref, k_ref, v_ref, o_ref, lset_ref, m, acc):
        qi = pl.program_id(2)
        q = q_ref[...]
        m[...] = jnp.full_like(m, NEG)
        acc[...] = jnp.zeros_like(acc)

        def block(kj, masked):
            off = pl.multiple_of(kj * t, t)
            s = _nt(q, k_ref[pl.ds(off, t), :])
            if masked:
                s = jnp.where(_diag_mask(t), s, NEG)
            mo = m[...]
            mn = jnp.maximum(mo, jnp.max(s, axis=1, keepdims=True))
            p = jnp.exp((s - _rep(mn, nl)).astype(BF16))
            acc[...] = jnp.exp(mo - mn) * acc[...] + _nn(p, v_ref[pl.ds(off, t), :])
            m[...] = mn

        def loop(kj, c):
            block(kj, False)
            return c

        lax.fori_loop(0, qi, loop, 0)
        block(qi, True)
        a = acc[...]
        l = a[:, V_ONE:V_ONE + 1]
        o_ref[...] = jnp.where(_lanes(0, MLA_V), a / l, 0.0).astype(BF16)
        lset_ref[...] = (m[...] + jnp.log(l)).T[0:8, :]

    qs = pl.BlockSpec((t, LANES), lambda b, h, qi: (b * n + qi, h))
    seq = pl.BlockSpec((S, LANES), lambda b, h, qi: (b, h))
    return _pc(body, name=name, grid=(Bl, MLA_H, n), in_specs=[qs, seq, seq],
               out_specs=[qs, pl.BlockSpec((8, t), lambda b, h, qi: (b * MLA_H + h, qi))],
               out_shape=[SDS((T, MLA_H * LANES), BF16), SDS((Bl * MLA_H * 8, S), F32)],
               scratch=[pltpu.VMEM((t, LANES), F32), pltpu.VMEM((t, LANES), F32)], comm=comm)(qr, kr, vr)


def flash_bwd(qr, kr, vr, o, lset, dycat, Bl, *, comm=None, name):
    T = qr.shape[0]
    S = T // Bl
    t = _att_tile(S)
    n = S // t
    do0 = (SSD_W + POOL_W) // LANES

    def body(q_ref, k_ref, v_ref, o_ref, lset_ref, do_ref, dq_ref, dk_ref, dv_ref, dka, dva, dlt_ref):
        kj = pl.program_id(2)

        @pl.when(kj == 0)
        def _():
            dq_ref[...] = jnp.zeros_like(dq_ref)
            for r in range(0, S, t):
                d = _rsum(do_ref[r:r + t, :].astype(F32) * o_ref[r:r + t, :].astype(F32))
                dlt_ref[:, r:r + t] = jnp.broadcast_to(d, (t, LANES)).T[0:8, :]

        k = k_ref[...]
        v = v_ref[...]
        dka[...] = jnp.zeros_like(dka)
        dva[...] = jnp.zeros_like(dva)

        def block(qi, masked):
            off = pl.multiple_of(qi * t, t)
            q = q_ref[pl.ds(off, t), :]
            do = do_ref[pl.ds(off, t), :].astype(BF16)
            st = _nt(k, q)
            if masked:
                st = jnp.where(_diag_mask(t, True), st, NEG)
            pt = jnp.exp((st - lset_ref[0:1, pl.ds(off, t)]).astype(BF16))
            dst = pt * (_nt(v, do) - dlt_ref[0:1, pl.ds(off, t)])
            dva[...] += _nn(pt, do)
            dka[...] += _nn(dst, q)
            dq_ref[pl.ds(off, t), :] += _tn(dst, k)

        def loop(qi, c):
            block(qi, False)
            return c

        block(kj, True)
        lax.fori_loop(kj + 1, n, loop, 0)
        dk_ref[...] = dka[...]
        dv_ref[...] = dva[...].astype(BF16)

    ks = pl.BlockSpec((t, LANES), lambda b, h, kj: (b * n + kj, h))
    seq = pl.BlockSpec((S, LANES), lambda b, h, kj: (b, h))
    rows = pl.BlockSpec((8, S), lambda b, h, kj: (b * MLA_H + h, 0))
    return _pc(body, name=name, grid=(Bl, MLA_H, n),
               in_specs=[seq, ks, ks, seq, rows, pl.BlockSpec((S, LANES), lambda b, h, kj: (b, do0 + h))],
               out_specs=[seq, ks, ks],
               out_shape=[SDS((T, MLA_H * LANES), F32), SDS((T, MLA_H * LANES), F32), SDS((T, MLA_H * LANES), BF16)],
               scratch=[pltpu.VMEM((t, LANES), F32), pltpu.VMEM((t, LANES), F32), pltpu.VMEM((8, S), F32)],
               comm=comm)(qr, kr, vr, o, lset, dycat)


def _rows2d(a):
    return a.reshape(-1, a.shape[-1])


def _scalar(i):
    return jnp.reshape(i, (1,)).astype(jnp.int32)


def chip_sum(g8, from_sibling, *, name):
    blk = g8.shape[1:]
    R, C = math.prod(blk[:-1]), blk[-1]
    tm = _tile(R, 512, 16)

    def body(c_ref, a_ref, b_ref, o_ref, ob_ref):
        s = a_ref[0, 0] + b_ref[0]
        o_ref[0] = s
        ob_ref[0] = s.astype(BF16)

    row = pl.BlockSpec((1, tm, C), lambda k, i, c: (k, i, 0))
    spec = pltpu.PrefetchScalarGridSpec(
        num_scalar_prefetch=1, grid=(4, R // tm),
        in_specs=[pl.BlockSpec((1, 1, tm, C), lambda k, i, c: (k, c[0], i, 0)), row], out_specs=[row, row])
    o, ob = pl.pallas_call(body, name=name, grid_spec=spec, out_shape=[SDS((4, R, C), F32), SDS((4, R, C), BF16)],
                           compiler_params=pltpu.CompilerParams(vmem_limit_bytes=VMEM_LIMIT),
                           )(_scalar(lax.axis_index("c")), g8.reshape(4, 2, R, C), from_sibling.reshape(4, R, C))
    return o.reshape((4,) + blk), ob.reshape((4,) + blk)


def adamw_sharded(w, m, v, sums, recv, layer, prev, *, name):
    blk = w.shape[1:]
    R, C = math.prod(blk[:-1]), blk[-1]
    tm = _tile(R, 256, 16)
    bc1 = 1.0 - ADAM_B1 ** ADAM_STEP
    bc2 = 1.0 - ADAM_B2 ** ADAM_STEP
    n_prev = 0 if prev is None else 4

    def body(chip_ref, w_ref, m_ref, v_ref, s_ref, r_ref, *rest):
        g_ref, d_ref, nm_ref, nv_ref = rest[n_prev:]
        g = s_ref[0] + r_ref[0].astype(F32) + r_ref[1].astype(F32) + r_ref[2].astype(F32)
        mm_ = ADAM_B1 * m_ref[0] + (1.0 - ADAM_B1) * g
        vv = ADAM_B2 * v_ref[0] + (1.0 - ADAM_B2) * (g * g)
        g_ref[0] = g
        nm_ref[0] = mm_
        nv_ref[0] = vv
        d_ref[0] = -ADAM_LR * ((mm_ / bc1) / (jnp.sqrt(vv / bc2) + ADAM_EPS) + ADAM_WD * w_ref[0])

    lay = pl.BlockSpec((1, tm, C), lambda i, c: (layer, i, 0))
    spec = pltpu.PrefetchScalarGridSpec(
        num_scalar_prefetch=1, grid=(R // tm,),
        in_specs=[lay, lay, lay, pl.BlockSpec((1, tm, C), lambda i, c: (c[0], i, 0)),
                  pl.BlockSpec((3, tm, C), lambda i, c: (0, i, 0))] + [ANY] * n_prev,
        out_specs=[lay] * 4)
    xi, yi, _ = _place()
    d3 = (w.shape[0], R, C)
    outs = pl.pallas_call(
        body, name=name, grid_spec=spec, out_shape=[SDS(d3, F32)] * 4,
        input_output_aliases={6 + i: i for i in range(n_prev)},
        compiler_params=pltpu.CompilerParams(vmem_limit_bytes=VMEM_LIMIT),
    )(_scalar(2 * xi + yi), w.reshape(d3), m.reshape(d3), v.reshape(d3), sums.reshape(4, R, C), recv.reshape(3, R, C),
      *([] if prev is None else prev))
    return list(outs)


def adamw(w, m, v, parts, *, name):
    shp = w.shape
    w2, m2, v2 = _rows2d(w), _rows2d(m), _rows2d(v)
    R, C = w2.shape
    p3 = [p.reshape(p.shape[0], R, C) for p in parts]
    tm = _tile(R, 256, 8)
    bc1 = 1.0 - ADAM_B1 ** ADAM_STEP
    bc2 = 1.0 - ADAM_B2 ** ADAM_STEP

    def body(w_ref, m_ref, v_ref, *refs):
        p_refs, (g_ref, d_ref, nm_ref, nv_ref) = refs[:len(p3)], refs[len(p3):]
        g = None
        for p_ref, p in zip(p_refs, p3):
            for k in range(p.shape[0]):
                term = p_ref[k].astype(F32)
                g = term if g is None else g + term
        mm_ = ADAM_B1 * m_ref[...] + (1.0 - ADAM_B1) * g
        vv = ADAM_B2 * v_ref[...] + (1.0 - ADAM_B2) * (g * g)
        g_ref[...] = g
        nm_ref[...] = mm_
        nv_ref[...] = vv
        d_ref[...] = -ADAM_LR * ((mm_ / bc1) / (jnp.sqrt(vv / bc2) + ADAM_EPS) + ADAM_WD * w_ref[...])

    blk = pl.BlockSpec((tm, C), lambda i: (i, 0))
    pspecs = [pl.BlockSpec((p.shape[0], tm, C), lambda i: (0, i, 0)) for p in p3]
    outs = _pc(body, name=name, grid=(R // tm,), in_specs=[blk, blk, blk] + pspecs,
               out_specs=[blk] * 4, out_shape=[SDS((R, C), F32)] * 4)(w2, m2, v2, *p3)
    return [o.reshape(shp) for o in outs]


def _place():
    return lax.axis_index("x"), lax.axis_index("y"), lax.axis_index("c")


def all_gather_many(xs, *, name):
    n = len(xs)

    def body(*refs):
        x_refs, o_refs = refs[:n], refs[n:2 * n]
        send_sems, recv_sems, local_sems = refs[2 * n:]
        x, y, c = _place()
        me, sibling = (x, y, c), (x, y, 1 - c)
        chips = [(1 - x, y), (x, 1 - y), (1 - x, 1 - y)]

        def rows(a, p):
            return o_refs[a].at[4 * p[0] + 2 * p[1] + p[2]]

        def copy(a, k, block, to, src=None):
            return pltpu.make_async_remote_copy(
                src_ref=rows(a, block) if src is None else src, dst_ref=rows(a, block),
                send_sem=send_sems.at[7 * a + k], recv_sem=recv_sems.at[7 * a + k], device_id=to, device_id_type=MESH)

        mine = [pltpu.make_async_copy(x_refs[a], rows(a, me), local_sems.at[a]) for a in range(n)]
        for cp in mine:
            cp.start()
        first = []
        for a in range(n):
            first.append(copy(a, 0, me, sibling, src=x_refs[a]))
            first += [copy(a, 1 + j, me, (*chip, c), src=x_refs[a]) for j, chip in enumerate(chips)]
        for cp in first:
            cp.start()
        passed = []
        for j, chip in enumerate(chips):
            for a in range(n):
                copy(a, 1 + j, (*chip, c), me).wait_recv()
                cp = copy(a, 4 + j, (*chip, c), sibling)
                cp.start()
                passed.append(cp)
        for a in range(n):
            copy(a, 0, sibling, me).wait_recv()
            for j, chip in enumerate(chips):
                copy(a, 4 + j, (*chip, 1 - c), me).wait_recv()
        for cp in first + passed:
            cp.wait_send()
        for cp in mine:
            cp.wait()

    return pl.pallas_call(
        body, name=name, in_specs=[ANY] * n, out_specs=[ANY] * n,
        out_shape=[SDS((N_DEV,) + a.shape, a.dtype) for a in xs],
        scratch_shapes=[pltpu.SemaphoreType.DMA((7 * n,)), pltpu.SemaphoreType.DMA((7 * n,)), pltpu.SemaphoreType.DMA((n,))],
    )(*xs)


def _stage(ins, out_shape, n_peers, copy_of, n_arrays=None, local_of=None):
    ins = list(ins)
    n = len(ins) if n_arrays is None else n_arrays

    def copies(in_refs, out_refs, send_sems, recv_sems):
        place = _place()
        out = []
        for a in range(n):
            for k in range(n_peers):
                src, dst, peer = copy_of(in_refs[a], out_refs[a], k, place)
                out.append(pltpu.make_async_remote_copy(
                    src_ref=src, dst_ref=dst, send_sem=send_sems.at[n_peers * a + k], recv_sem=recv_sems.at[n_peers * a + k],
                    device_id=peer, device_id_type=MESH))
        if local_of is not None:
            for i, (src, dst) in enumerate(local_of(in_refs, out_refs, place)):
                out.append(pltpu.make_async_copy(src, dst, send_sems.at[n_peers * n + i]))
        return out

    return dict(ins=ins, out_shape=list(out_shape), sems=n_peers * n + (n if local_of is not None else 0), copies=copies)


def _other_chips(x, y):
    return [(1 - x, y), (x, 1 - y), (1 - x, 1 - y)]


def stage_gather_direct(blocks):
    def copy_of(src, dst, k, place):
        x, y, c = place
        peer = (x, y, 1 - c) if k == 0 else (*_other_chips(x, y)[k - 1], c)
        return src, dst.at[4 * x + 2 * y + c], peer

    return _stage(blocks, [SDS((N_DEV,) + b.shape, b.dtype) for b in blocks], 4, copy_of)


def stage_gather_forward(bufs, own):
    n = len(bufs)

    def copy_of(src, dst, k, place):
        x, y, c = place
        cx, cy = _other_chips(x, y)[k]
        slot = 4 * cx + 2 * cy + c
        return src.at[slot], dst.at[slot], (x, y, 1 - c)

    def local_of(in_refs, out_refs, place):
        x, y, c = place
        return [(in_refs[n + a], out_refs[a].at[4 * x + 2 * y + c]) for a in range(n)]

    st = _stage(list(bufs) + list(own), [SDS(b.shape, b.dtype) for b in bufs], 3, copy_of, n_arrays=n, local_of=local_of)
    st["alias"] = n
    return st


def stage_rs_sibling(g8s):
    def copy_of(src, dst, k, place):
        x, y, c = place
        return src.at[2 * k + (1 - c)], dst.at[k], (x, y, 1 - c)

    return _stage(g8s, [SDS((4,) + g.shape[1:], g.dtype) for g in g8s], 4, copy_of)


def stage_rs_chips(sums):
    def copy_of(src, dst, k, place):
        x, y, c = place
        chip = _other_chips(x, y)[k]
        return src.at[2 * chip[0] + chip[1]], dst.at[k], (*chip, c)

    return _stage(sums, [SDS((3,) + s.shape[1:], s.dtype) for s in sums], 3, copy_of)


def run_stage(stage, *, name):
    n_in, n_out = len(stage["ins"]), len(stage["out_shape"])

    def body(*refs):
        cps = stage["copies"](refs[:n_in], refs[n_in:n_in + n_out], refs[-2], refs[-1])
        for cp in cps:
            cp.start()
        for cp in cps:
            cp.wait()

    return pl.pallas_call(
        body, name=name, in_specs=[ANY] * n_in, out_specs=[ANY] * n_out, out_shape=stage["out_shape"],
        scratch_shapes=[pltpu.SemaphoreType.DMA((stage["sems"],)), pltpu.SemaphoreType.DMA((stage["sems"],))],
    )(*stage["ins"])


def _owner_major(full, axis):
    shp = full.shape
    r = full.reshape(shp[:axis] + (N_DEV, shp[axis] // N_DEV) + shp[axis + 1:])
    return jnp.moveaxis(r, axis, 0)


def _from_owner_major(g8, axis):
    r = jnp.moveaxis(g8, 0, axis)
    shp = r.shape
    return r.reshape(shp[:axis] + (shp[axis] * shp[axis + 1],) + shp[axis + 2:])


def _perm_w_in(w):
    z = jnp.zeros((w.shape[0], LANES), w.dtype)
    dt = jnp.pad(w[:, 2560:2576], ((0, 0), (0, LANES - SSD_HEADS)))
    kpe = jnp.pad(w[:, 3728:3760], ((0, 0), (PE_LO, LANES - PE_HI)))
    return jnp.concatenate([w[:, 0:1024], w[:, 1024:2560], w[:, 2576:3088], w[:, 3088:3472], z, w[:, 3472:3728], dt, kpe], axis=1)


def _unperm_w_in(g):
    return jnp.concatenate([g[:, Z0:Z0 + 1024], g[:, XBC0:XBC0 + 1536], g[:, DT0:DT0 + SSD_HEADS], g[:, U0:U0 + 512],
                            g[:, CQ0:CQ0 + 384], g[:, CKV0:CKV0 + 256], g[:, KPE0 + PE_LO:KPE0 + PE_HI]], axis=1)


def _perm_w_uq(w):
    return jnp.pad(w.reshape(MLA_QR, MLA_H, MLA_QK), ((0, 0), (0, 0), (0, LANES - MLA_QK))).reshape(MLA_QR, MLA_H * LANES)


def _unperm_w_uq(g):
    return g.reshape(MLA_QR, MLA_H, LANES)[:, :, :MLA_QK].reshape(MLA_QR, MLA_H * MLA_QK)


def _perm_w_ukv(w):
    w3 = w.reshape(MLA_KVR, MLA_H, MLA_NOPE + MLA_V)
    pad = ((0, 0), (0, 0), (0, LANES - MLA_NOPE))
    k = jnp.pad(w3[:, :, :MLA_NOPE], pad).reshape(MLA_KVR, MLA_H * LANES)
    v = jnp.pad(w3[:, :, MLA_NOPE:], pad).reshape(MLA_KVR, MLA_H * LANES)
    return jnp.concatenate([k, v], axis=1)


def _unperm_w_ukv(g):
    k = g[:, :MLA_H * LANES].reshape(MLA_KVR, MLA_H, LANES)[:, :, :MLA_NOPE]
    v = g[:, MLA_H * LANES:].reshape(MLA_KVR, MLA_H, LANES)[:, :, :MLA_V]
    return jnp.concatenate([k, v], axis=2).reshape(MLA_KVR, MLA_H * (MLA_NOPE + MLA_V))


def _perm_w_out(w):
    m = jnp.pad(w[SSD_W + POOL_W:].reshape(MLA_H, MLA_V, D), ((0, 0), (0, LANES - MLA_V), (0, 0))).reshape(MLA_H * LANES, D)
    return jnp.concatenate([w[:SSD_W + POOL_W], m], axis=0)


def _unperm_w_out(g):
    m = g[SSD_W + POOL_W:].reshape(MLA_H, LANES, D)[:, :MLA_V].reshape(MLA_H * MLA_V, D)
    return jnp.concatenate([g[:SSD_W + POOL_W], m], axis=0)


def _lane_pad(v):
    return jnp.pad(v.reshape(1, -1), ((0, 0), (0, LANES - v.shape[-1])))


SMALL = ("attn_norm", "ssd_conv_b", "ssd_dt_bias", "ssd_a_log", "ssd_d", "ssd_norm", "pool_w", "pool_scale",
         "mla_q_norm", "mla_kv_norm", "ffn_norm", "ffn_conv_b", "final_norm")
SHARDED = {"w_in": 2, "ssd_conv_w": 2, "mla_w_uq": 2, "mla_w_ukv": 2, "w_out": 1, "ffn_w_up": 2, "ffn_conv_w": 2,
           "ffn_w_down": 1}
ALL_W = ("attn_norm", "w_in", "ssd_conv_w", "ssd_conv_b", "ssd_dt_bias", "ssd_a_log", "ssd_d", "ssd_norm", "pool_w",
         "pool_scale", "mla_q_norm", "mla_w_uq", "mla_kv_norm", "mla_w_ukv", "w_out", "ffn_norm", "ffn_w_up",
         "ffn_conv_w", "ffn_conv_b", "ffn_w_down", "final_norm")


def _pack_small(d):
    rows, layout = [], []
    for k in SMALL:
        a = d[k].reshape(-1)
        n = a.shape[0]
        r = -(-n // LANES)
        rows.append(jnp.pad(a, (0, r * LANES - n)).reshape(r, LANES))
        layout.append((k, n, r, d[k].shape))
    buf = jnp.concatenate(rows, axis=0)
    pad = (-buf.shape[0]) % 8
    return jnp.pad(buf, ((0, pad), (0, 0))), layout


def _unpack_small(buf, layout):
    out, r0 = {}, 0
    for k, n, r, shp in layout:
        out[k] = buf[r0:r0 + r].reshape(-1)[:n].reshape(shp)
        r0 += r
    return out


_PERM = {"w_in": _perm_w_in, "mla_w_uq": _perm_w_uq, "mla_w_ukv": _perm_w_ukv, "w_out": _perm_w_out}
FIRST = ("w_in", "ssd_conv_w")
REST = tuple(k for k in SHARDED if k not in FIRST)


def _sharded_entries(keys, gathered):
    return {k: _PERM.get(k, lambda t: t)(_from_owner_major(g8, SHARDED[k] - 1)) for k, g8 in zip(keys, gathered)}


def _layer_fwd(l, x, W, cs, sn, Bl, next_blocks=None, pending=None):
    T = x.shape[0]
    S = T // Bl
    n = f"l{l}_"
    h = rms_fwd(x, W["attn_norm"], name=n + "attn_norm")
    own_direct = stage_gather_direct(pending) if pending is not None else None
    proj = mm(h, W["w_in"], comm=own_direct, name=n + "w_in")
    proj3 = proj.reshape(Bl, S, PW)
    xbc3 = conv_silu_fwd(proj3, W["ssd_conv_w"], W["ssd_conv_b"], name=n + "ssd_conv")
    own_forward = stage_gather_forward(own_direct["result"], pending) if own_direct else None
    y3, hin = ssd_fwd(xbc3, proj3, W["ssd_dt_bias"], W["ssd_a_log"], W["ssd_d"], comm=own_forward, name=n + "ssd_scan")
    if own_direct:
        W = {**W, **_sharded_entries(REST, own_forward["result"])}
    y = y3.reshape(T, SSD_W)
    y_ssd = gated_rms_fwd(y, proj, W["ssd_norm"], name=n + "ssd_gate_norm")
    y_pool = pool_fwd(proj3, W["pool_w"], W["pool_scale"], name=n + "pool").reshape(T, POOL_W)
    qn = rms_fwd(proj, W["mla_q_norm"], col0=CQ0, width=MLA_QR, name=n + "q_norm")
    kvn = rms_fwd(proj, W["mla_kv_norm"], col0=CKV0, width=MLA_KVR, name=n + "kv_norm")
    qt = mm(qn, W["mla_w_uq"], name=n + "w_uq")
    kvt = mm(kvn, W["mla_w_ukv"], name=n + "w_ukv")
    qr, kr, vr = mla_prep_fwd(qt, kvt, proj, cs, sn, name=n + "rope")
    direct = stage_gather_direct(next_blocks) if next_blocks is not None else None
    o, lset = flash_fwd(qr, kr, vr, Bl, comm=direct, name=n + "attn")
    ycat = jnp.concatenate([y_ssd, y_pool, o], axis=1)
    x1 = mm(ycat, W["w_out"], add=x, name=n + "w_out")
    h2 = rms_fwd(x1, W["ffn_norm"], name=n + "ffn_norm")
    forward = stage_gather_forward(direct["result"], next_blocks) if direct else None
    pre = mm(h2, W["ffn_w_up"], comm=forward, name=n + "w_up")
    gathered = forward["result"] if direct else None
    pre3 = pre.reshape(Bl, S, 2 * DFF)
    act = ffn_act_fwd(pre3, W["ffn_conv_w"], W["ffn_conv_b"], name=n + "ffn_act").reshape(T, DFF)
    x2 = mm(act, W["ffn_w_down"], add=x1, name=n + "w_down")
    saved = dict(x=x, h=h, proj=proj, xbc3=xbc3, hin=hin, y=y, qn=qn, kvn=kvn, vr=vr, qr=qr, kr=kr, o=o, lset=lset,
                 ycat=ycat, x1=x1, h2=h2, pre3=pre3, act=act)
    return x2, saved, gathered, W


EARLY = ("ffn_w_up", "ffn_conv_w", "ffn_w_down", "w_out")
LATE = tuple(k for k in SHARDED if k not in EARLY)
_UNPERM = {"w_in": _unperm_w_in, "mla_w_uq": _unperm_w_uq, "mla_w_ukv": _unperm_w_ukv, "w_out": _unperm_w_out}


def _by_owner(g, keys):
    return [_owner_major(_UNPERM.get(k, lambda t: t)(g[k]), SHARDED[k] - 1) for k in keys]


def _chip_sums(g8s, from_sibling, tag):
    return [chip_sum(g8, r, name=f"{tag}{a}") for a, (g8, r) in enumerate(zip(g8s, from_sibling))]


def _layer_bwd(l, dx2, dx2b, W, sv, cs, sn, Bl, later_g8=None):
    T = dx2.shape[0]
    S = T // Bl
    n = f"l{l}_b_"
    g = {}
    g["ffn_w_down"] = mm(sv["act"], dx2b, ta=True, name=n + "dw_down")
    dact = mm(dx2b, W["ffn_w_down"], tb=True, name=n + "dact")
    to_sibling = stage_rs_sibling(later_g8) if later_g8 is not None else None
    dpg, dpv, dwg, dwv, dbg, dbv = ffn_act_bwd(sv["pre3"], W["ffn_conv_w"], W["ffn_conv_b"], dact.reshape(Bl, S, DFF),
                                               comm=to_sibling, name=n + "ffn_act")
    to_chips = sums = None
    if to_sibling:
        sums = _chip_sums(later_g8, to_sibling["result"], n + "rs_late_add")
        to_chips = stage_rs_chips([sb for _, sb in sums])
    g["ffn_conv_w"] = jnp.concatenate([dwg, dwv], axis=1)
    g["ffn_conv_b"] = jnp.concatenate([dbg, dbv], axis=1)
    dpg, dpv = dpg.reshape(T, DFF), dpv.reshape(T, DFF)
    g["ffn_w_up"] = jnp.concatenate([mm(sv["h2"], dpg, ta=True, name=n + "dw_up_g"),
                                     mm(sv["h2"], dpv, ta=True, name=n + "dw_up_v")], axis=1)
    dh2 = mm(dpg, W["ffn_w_up"], tb=True, name=n + "dh2_g")
    dh2 = mm(dpv, W["ffn_w_up"], tb=True, b_k0=DFF, add=dh2, out_dtype=BF16, name=n + "dh2_v")
    dx1, dx1b, g["ffn_norm"] = rms_bwd(sv["x1"], W["ffn_norm"], dh2, add=dx2, name=n + "ffn_norm")
    g["w_out"] = mm(sv["ycat"], dx1b, ta=True, name=n + "dw_out")
    dycat = mm(dx1b, W["w_out"], tb=True, out_dtype=BF16, name=n + "dycat")
    proj, proj3 = sv["proj"], sv["proj"].reshape(Bl, S, PW)
    dy, dz, g["ssd_norm"] = gated_rms_bwd(sv["y"], proj, W["ssd_norm"], dycat, name=n + "ssd_gate_norm")
    dxa, ddt, dpar, dd = ssd_bwd(sv["xbc3"], proj3, sv["hin"], dy.reshape(Bl, S, SSD_W), W["ssd_dt_bias"], W["ssd_a_log"],
                                 W["ssd_d"], comm=to_chips, name=n + "ssd_scan")
    reduced_late = ([s32 for s32, _ in sums], to_chips["result"]) if to_chips else None
    g["ssd_dt_bias"] = dpar[0, :SSD_HEADS]
    g["ssd_a_log"] = dpar[1, :SSD_HEADS]
    g["ssd_d"] = dd[:, :2].reshape(SSD_HEADS)
    early_g8 = _by_owner(g, EARLY)
    early_sibling = stage_rs_sibling(early_g8)
    dxbc, g["ssd_conv_w"], g["ssd_conv_b"] = conv_silu_bwd(proj3, W["ssd_conv_w"], W["ssd_conv_b"], dxa, comm=early_sibling,
                                                           name=n + "ssd_conv")
    du, g["pool_w"], g["pool_scale"] = pool_bwd(proj3, W["pool_w"], W["pool_scale"], dycat.reshape(Bl, S, YCAT), name=n + "pool")
    early_sums = _chip_sums(early_g8, early_sibling["result"], n + "rs_early_add")
    early_chips = stage_rs_chips([sb for _, sb in early_sums])
    dqr, dkr, dv = flash_bwd(sv["qr"], sv["kr"], sv["vr"], sv["o"], sv["lset"], dycat, Bl, comm=early_chips, name=n + "attn_bwd")
    reduced_early = ([s32 for s32, _ in early_sums], early_chips["result"])
    dqt, dkt, dkpe = mla_prep_bwd(dqr, dkr, cs, sn, name=n + "rope")
    g["mla_w_ukv"] = jnp.concatenate([mm(sv["kvn"], dkt, ta=True, name=n + "dw_uk"),
                                      mm(sv["kvn"], dv, ta=True, name=n + "dw_uv")], axis=1)
    dkvn = mm(dkt, W["mla_w_ukv"], tb=True, name=n + "dkvn_k")
    dkvn = mm(dv, W["mla_w_ukv"], tb=True, b_k0=MLA_H * LANES, add=dkvn, name=n + "dkvn_v")
    g["mla_w_uq"] = mm(sv["qn"], dqt, ta=True, name=n + "dw_uq")
    dqn = mm(dqt, W["mla_w_uq"], tb=True, name=n + "dqn")
    dcq, g["mla_q_norm"] = rms_bwd(proj, W["mla_q_norm"], dqn, col0=CQ0, width=MLA_QR, name=n + "q_norm")
    dckv, g["mla_kv_norm"] = rms_bwd(proj, W["mla_kv_norm"], dkvn, col0=CKV0, width=MLA_KVR, name=n + "kv_norm")
    dproj = jnp.concatenate([dz, dxbc.reshape(T, SSD_XBC), du.reshape(T, POOL_W), dcq, jnp.zeros((T, LANES), BF16), dckv,
                             ddt.reshape(T, LANES), dkpe], axis=1)
    g["w_in"] = mm(sv["h"], dproj, ta=True, name=n + "dw_in")
    dh = mm(dproj, W["w_in"], tb=True, out_dtype=BF16, name=n + "dh")
    dx, dxb, g["attn_norm"] = rms_bwd(sv["x"], W["attn_norm"], dh, add=dx1, name=n + "attn_norm")
    return dx, dxb, g, reduced_late, reduced_early


def kernel(x, positions, attn_norm, w_in, ssd_conv_w, ssd_conv_b, ssd_dt_bias, ssd_a_log, ssd_d, ssd_norm, pool_w, pool_scale, mla_q_norm, mla_w_uq, mla_kv_norm, mla_w_ukv, w_out, ffn_norm, ffn_w_up, ffn_conv_w, ffn_conv_b, ffn_w_down, final_norm, loss_target, m_attn_norm, m_w_in, m_ssd_conv_w, m_ssd_conv_b, m_ssd_dt_bias, m_ssd_a_log, m_ssd_d, m_ssd_norm, m_pool_w, m_pool_scale, m_mla_q_norm, m_mla_w_uq, m_mla_kv_norm, m_mla_w_ukv, m_w_out, m_ffn_norm, m_ffn_w_up, m_ffn_conv_w, m_ffn_conv_b, m_ffn_w_down, m_final_norm, v_attn_norm, v_w_in, v_ssd_conv_w, v_ssd_conv_b, v_ssd_dt_bias, v_ssd_a_log, v_ssd_d, v_ssd_norm, v_pool_w, v_pool_scale, v_mla_q_norm, v_mla_w_uq, v_mla_kv_norm, v_mla_w_ukv, v_w_out, v_ffn_norm, v_ffn_w_up, v_ffn_conv_w, v_ffn_conv_b, v_ffn_w_down, v_final_norm):
    a = locals()
    Wt = {k: a[k] for k in ALL_W}
    Mo = {k: a["m_" + k] for k in ALL_W}
    Vo = {k: a["v_" + k] for k in ALL_W}
    Bl, S, _ = x.shape
    T = Bl * S

    names = list(SHARDED)
    conv = ("ssd_conv_w", "ffn_conv_w")

    def blocks_of(l, keys=names):
        return [Wt[k][l] if k in conv else Wt[k][l].astype(BF16) for k in keys]

    def replicated(l):
        return {
            "attn_norm": attn_norm[l].reshape(1, D), "ssd_conv_b": ssd_conv_b[l].reshape(1, SSD_XBC),
            "ssd_dt_bias": _lane_pad(ssd_dt_bias[l]), "ssd_a_log": _lane_pad(ssd_a_log[l]),
            "ssd_d": jnp.repeat(ssd_d[l].reshape(NPAIR, 2), SSD_P, axis=1), "ssd_norm": ssd_norm[l].reshape(1, SSD_W),
            "pool_w": pool_w[l].astype(BF16), "pool_scale": pool_scale[l].reshape(1, POOL_W),
            "mla_q_norm": mla_q_norm[l].reshape(1, MLA_QR), "mla_kv_norm": mla_kv_norm[l].reshape(1, MLA_KVR),
            "ffn_norm": ffn_norm[l].reshape(1, D), "ffn_conv_b": ffn_conv_b[l].reshape(1, 2 * DFF)}

    pos = positions.astype(F32).reshape(T, 1)
    inv_freq = ROPE_THETA ** (-jnp.arange(0, MLA_ROPE, 2, dtype=F32) / MLA_ROPE)
    invf = jnp.pad(jnp.concatenate([inv_freq, inv_freq]), (PE_LO, LANES - PE_HI)).reshape(1, LANES)
    cs, sn = rope_tables(pos, invf, name="rope_tables")

    first = all_gather_many(blocks_of(0, FIRST), name="gather_weights_l0")
    layers = [{**replicated(0), **_sharded_entries(FIRST, first)}]
    xc = x.reshape(T, D)
    saved = []
    for l in range(DEPTH):
        xc, sv, gathered, layers[l] = _layer_fwd(l, xc, layers[l], cs, sn, Bl,
                                                 next_blocks=blocks_of(l + 1) if l + 1 < DEPTH else None,
                                                 pending=blocks_of(0, REST) if l == 0 else None)
        saved.append(sv)
        if gathered is not None:
            layers.append({**replicated(l + 1), **_sharded_entries(names, gathered)})
    dx, dxb, g_final, loss_part = final_loss(xc, final_norm.reshape(1, D), loss_target.reshape(T, D), name="final_loss")

    grads = [None] * DEPTH
    reduced = {}

    def record(l, keys, red):
        for a, k in enumerate(keys):
            reduced[(l, k)] = (red[0][a], red[1][a])

    later_g8 = None
    for l in reversed(range(DEPTH)):
        dx, dxb, grads[l], red_late, red_early = _layer_bwd(l, dx, dxb, layers[l], saved[l], cs, sn, Bl, later_g8=later_g8)
        if red_late is not None:
            record(l + 1, LATE, red_late)
        record(l, EARLY, red_early)
        later_g8 = _by_owner(grads[l], LATE)
    loss = lax.psum(loss_part[0, 0], AXES)
    sums = _chip_sums(later_g8, run_stage(stage_rs_sibling(later_g8), name="rs_sibling_l0"), "rs_add_l0_")
    record(0, LATE, ([s32 for s32, _ in sums], run_stage(stage_rs_chips([sb for _, sb in sums]), name="rs_chips_l0")))

    out_g, out_d, out_m, out_v = {}, {}, {}, {}
    for k in names:
        outs = None
        for l in reversed(range(DEPTH)):
            s32, recv = reduced[(l, k)]
            outs = adamw_sharded(Wt[k], Mo[k], Vo[k], s32, recv, l, outs, name=f"adamw_l{l}_{k}")
        out_g[k], out_d[k], out_m[k], out_v[k] = (o.reshape(Wt[k].shape) for o in outs)

    part = {k: g_final.reshape(D) if k == "final_norm" else
            jnp.stack([grads[l][k].reshape(Wt[k].shape[1:]) for l in range(DEPTH)]) for k in SMALL}
    pg, layout = _pack_small(part)
    pw, _ = _pack_small(Wt)
    pm, _ = _pack_small(Mo)
    pv, _ = _pack_small(Vo)
    (pg8,) = all_gather_many([pg], name="gather_small_grads")
    sg, sd, sm, sv_ = adamw(pw, pm, pv, [pg8], name="adamw_small")
    for dst, buf in ((out_g, sg), (out_d, sd), (out_m, sm), (out_v, sv_)):
        dst.update(_unpack_small(buf, layout))

    return (loss, dx.reshape(Bl, S, D), *[out_g[k] for k in ALL_W], *[out_d[k] for k in ALL_W],
            *[out_m[k] for k in ALL_W], *[out_v[k] for k in ALL_W])
```

```python
import functools
import math

import jax
import jax.numpy as jnp
from jax import lax
from jax.experimental import pallas as pl
from jax.experimental.pallas import tpu as pltpu

F32, BF16 = jnp.float32, jnp.bfloat16
SDS = jax.ShapeDtypeStruct
MESH = pl.DeviceIdType.MESH
AXES = ("x", "y", "c")
N_DEV = 8

D = 1024
EPS = 1e-6
SSD_HEADS, SSD_P, SSD_W, SSD_G, SSD_N, SSD_K, SSD_L, SSD_XBC = 16, 64, 1024, 2, 128, 4, 128, 1536
POOL_G, POOL_D, POOL_W, POOL_WIN = 4, 128, 512, (2, 4, 8, 16)
MLA_H, MLA_QR, MLA_KVR, MLA_NOPE, MLA_ROPE, MLA_V, MLA_QK = 8, 384, 256, 64, 32, 64, 96
ROPE_THETA = 10000.0
MIX = 2048
DFF, FFN_K = 2816, 3
DEPTH = 2
ADAM_LR, ADAM_B1, ADAM_B2, ADAM_EPS, ADAM_WD, ADAM_STEP = 0.001, 0.9, 0.999, 1e-08, 0.01, 10

Z0, XBC0, U0, CQ0, CKV0, DT0, KPE0, PW = 0, 1024, 2560, 3072, 3584, 3840, 3968, 4096
LANES = 128
YCAT = SSD_W + POOL_W + MLA_H * LANES
NEG = -1e30
VMEM_LIMIT = 56 * 1024 * 1024
MM_ROW_TILE, MM_LANE_TILE, MM_FULL_K = 1024, 1408, 2816


def _tile(n, pref, mult):
    if n <= pref:
        return n
    for d in range(pref, 0, -mult):
        if d % mult == 0 and n % d == 0:
            return d
    return n


def _dg(a, b, ca, cb, prec=None):
    return lax.dot_general(a, b, (((ca,), (cb,)), ((), ())), preferred_element_type=F32, precision=prec)


def _nn(a, b):
    return _dg(a.astype(BF16), b.astype(BF16), 1, 0)


def _nt(a, b):
    return _dg(a.astype(BF16), b.astype(BF16), 1, 1)


def _tn(a, b):
    return _dg(a.astype(BF16), b.astype(BF16), 0, 0)


def _sig(x):
    return jax.nn.sigmoid(x)


def _silu(x):
    return x * _sig(x)


def _dsilu(x):
    s = _sig(x)
    return s * (1.0 + x * (1.0 - s))


ANY = pl.BlockSpec(memory_space=pl.ANY)


def _pc(body, *, name, grid, in_specs, out_specs, out_shape, scratch=(), comm=None):
    params = pltpu.CompilerParams(vmem_limit_bytes=VMEM_LIMIT)
    if comm is None:
        return pl.pallas_call(body, name=name, grid=grid, in_specs=in_specs, out_specs=out_specs, out_shape=out_shape,
                              scratch_shapes=list(scratch), compiler_params=params)
    single = not isinstance(out_shape, (list, tuple))
    o_specs = [out_specs] if single else list(out_specs)
    o_shape = [out_shape] if single else list(out_shape)
    ni, no, ns = len(in_specs), len(o_specs), len(scratch)
    nci, nco = len(comm["ins"]), len(comm["out_shape"])

    def fused(*refs):
        ins, cins = refs[:ni], refs[ni:ni + nci]
        outs, couts = refs[ni + nci:ni + nci + no], refs[ni + nci + no:ni + nci + no + nco]
        scr = refs[ni + nci + no + nco:ni + nci + no + nco + ns]
        send_sems, recv_sems = refs[-2:]
        copies = comm["copies"](cins, couts, send_sems, recv_sems)
        first = functools.reduce(jnp.logical_and, [pl.program_id(d) == 0 for d in range(len(grid))])
        last = functools.reduce(jnp.logical_and, [pl.program_id(d) == grid[d] - 1 for d in range(len(grid))])

        @pl.when(first)
        def _():
            for cp in copies:
                cp.start()

        body(*ins, *outs, *scr)

        @pl.when(last)
        def _():
            for cp in copies:
                cp.wait()

    call = pl.pallas_call(
        fused, name=name, grid=grid, in_specs=list(in_specs) + [ANY] * nci, out_specs=o_specs + [ANY] * nco,
        out_shape=o_shape + list(comm["out_shape"]),
        scratch_shapes=list(scratch) + [pltpu.SemaphoreType.DMA((comm["sems"],)), pltpu.SemaphoreType.DMA((comm["sems"],))],
        input_output_aliases={ni + a: no + a for a in range(comm.get("alias", 0))},
        compiler_params=params)

    def run(*args):
        res = call(*args, *comm["ins"])
        comm["result"] = list(res[no:])
        return res[0] if single else list(res[:no])

    return run


def _rsum(x):
    return jnp.sum(x, axis=1, keepdims=True)


def _csum(x):
    return jnp.sum(x, axis=0, keepdims=True)


def mm(a, b, *, ta=False, tb=False, add=None, out_dtype=F32, b_k0=0, comm=None, name):
    M, K = (a.shape[1], a.shape[0]) if ta else a.shape
    N = b.shape[0] if tb else b.shape[1]
    assert tb or b_k0 == 0
    tm = _tile(M, MM_LANE_TILE, LANES) if ta else _tile(M, MM_ROW_TILE, 8)
    tn = _tile(N, MM_LANE_TILE, LANES)
    if ta:
        tk = _tile(K, MM_ROW_TILE, 8)
    else:
        tk = K if K <= MM_FULL_K else _tile(K, 2048, LANES)
    nk = K // tk

    def body(*refs):
        if add is None:
            a_ref, b_ref, o_ref = refs[:3]
        else:
            a_ref, b_ref, add_ref, o_ref = refs[:4]
        part = _dg(a_ref[...].astype(BF16), b_ref[...].astype(BF16), 0 if ta else 1, 1 if tb else 0)

        def finish(r):
            if add is not None:
                r = r + add_ref[...].astype(F32)
            o_ref[...] = r.astype(out_dtype)

        if nk == 1:
            finish(part)
            return
        acc = refs[-1]
        k = pl.program_id(2)

        @pl.when(k == 0)
        def _():
            acc[...] = part

        @pl.when(k > 0)
        def _():
            acc[...] += part

        @pl.when(k == nk - 1)
        def _():
            finish(acc[...])

    a_spec = pl.BlockSpec((tk, tm), lambda i, j, k: (k, i)) if ta else pl.BlockSpec((tm, tk), lambda i, j, k: (i, k))
    assert b_k0 % tk == 0
    kb0 = b_k0 // tk
    b_spec = pl.BlockSpec((tn, tk), lambda i, j, k: (j, kb0 + k)) if tb else pl.BlockSpec((tk, tn), lambda i, j, k: (k, j))
    o_spec = pl.BlockSpec((tm, tn), lambda i, j, k: (i, j))
    ins, specs = [a, b], [a_spec, b_spec]
    if add is not None:
        ins.append(add)
        specs.append(o_spec)
    return _pc(body, name=name, grid=(M // tm, N // tn, nk), in_specs=specs, out_specs=o_spec,
               out_shape=SDS((M, N), out_dtype), scratch=[pltpu.VMEM((tm, tn), F32)] if nk > 1 else [], comm=comm)(*ins)


def rms_fwd(x, g, *, col0=0, width=None, name):
    T = x.shape[0]
    W = width or x.shape[1]
    tm = _tile(T, 512, 8)

    def body(x_ref, g_ref, o_ref):
        v = x_ref[...]
        r = lax.rsqrt(jnp.mean(v * v, axis=1, keepdims=True) + EPS)
        o_ref[...] = ((v * r) * g_ref[...]).astype(BF16)

    return _pc(body, name=name, grid=(T // tm,),
               in_specs=[pl.BlockSpec((tm, W), lambda i: (i, col0 // W)), pl.BlockSpec((1, W), lambda i: (0, 0))],
               out_specs=pl.BlockSpec((tm, W), lambda i: (i, 0)), out_shape=SDS((T, W), BF16))(x, g)


def rms_bwd(x, g, dh, *, col0=0, width=None, add=None, name):
    T = x.shape[0]
    W = width or x.shape[1]
    tm = _tile(T, 512, 8)

    def body(*refs):
        if add is None:
            x_ref, g_ref, dh_ref, dx_ref, dg_ref = refs
        else:
            x_ref, g_ref, dh_ref, add_ref, dx_ref, dxb_ref, dg_ref = refs
        v = x_ref[...]
        r = lax.rsqrt(jnp.mean(v * v, axis=1, keepdims=True) + EPS)
        xh = v * r
        d = dh_ref[...].astype(F32)
        dxh = d * g_ref[...]
        dx = r * (dxh - xh * jnp.mean(dxh * xh, axis=1, keepdims=True))
        if add is not None:
            dx = dx + add_ref[...]
            dxb_ref[...] = dx.astype(BF16)
        dx_ref[...] = dx.astype(dx_ref.dtype)

        @pl.when(pl.program_id(0) == 0)
        def _():
            dg_ref[...] = jnp.zeros_like(dg_ref)

        dg_ref[...] += _csum(d * xh)

    row = pl.BlockSpec((tm, W), lambda i: (i, 0))
    vec = pl.BlockSpec((1, W), lambda i: (0, 0))
    ins = [x, g, dh] + ([] if add is None else [add])
    specs = [pl.BlockSpec((tm, W), lambda i: (i, col0 // W)), vec, row] + ([] if add is None else [row])
    if add is None:
        return _pc(body, name=name, grid=(T // tm,), in_specs=specs, out_specs=[row, vec],
                   out_shape=[SDS((T, W), BF16), SDS((1, W), F32)])(*ins)
    return _pc(body, name=name, grid=(T // tm,), in_specs=specs, out_specs=[row, row, vec],
               out_shape=[SDS((T, W), F32), SDS((T, W), BF16), SDS((1, W), F32)])(*ins)


def gated_rms_fwd(y, proj, g, *, name):
    T = y.shape[0]
    tm = _tile(T, 512, 8)

    def body(y_ref, z_ref, g_ref, o_ref):
        v = y_ref[...] * _silu(z_ref[...])
        r = lax.rsqrt(jnp.mean(v * v, axis=1, keepdims=True) + EPS)
        o_ref[...] = ((v * r) * g_ref[...]).astype(BF16)

    row = pl.BlockSpec((tm, SSD_W), lambda i: (i, 0))
    return _pc(body, name=name, grid=(T // tm,), in_specs=[row, row, pl.BlockSpec((1, SSD_W), lambda i: (0, 0))],
               out_specs=row, out_shape=SDS((T, SSD_W), BF16))(y, proj, g)


def gated_rms_bwd(y, proj, g, dycat, *, name):
    T = y.shape[0]
    tm = _tile(T, 512, 8)

    def body(y_ref, z_ref, g_ref, d_ref, dy_ref, dz_ref, dg_ref):
        yv, z = y_ref[...], z_ref[...]
        sg = _sig(z)
        sz = z * sg
        v = yv * sz
        r = lax.rsqrt(jnp.mean(v * v, axis=1, keepdims=True) + EPS)
        vh = v * r
        d = d_ref[...]
        dvh = d * g_ref[...]
        dv = r * (dvh - vh * jnp.mean(dvh * vh, axis=1, keepdims=True))
        dy_ref[...] = dv * sz
        dz_ref[...] = (dv * yv * (sg + sz * (1.0 - sg))).astype(BF16)

        @pl.when(pl.program_id(0) == 0)
        def _():
            dg_ref[...] = jnp.zeros_like(dg_ref)

        dg_ref[...] += _csum(d * vh)

    row = pl.BlockSpec((tm, SSD_W), lambda i: (i, 0))
    vec = pl.BlockSpec((1, SSD_W), lambda i: (0, 0))
    return _pc(body, name=name, grid=(T // tm,), in_specs=[row, row, vec, row], out_specs=[row, row, vec],
               out_shape=[SDS((T, SSD_W), F32), SDS((T, SSD_W), BF16), SDS((1, SSD_W), F32)])(y, proj, g, dycat)


def final_loss(x, g, tgt, *, name):
    T = x.shape[0]
    tm = _tile(T, 512, 8)

    def body(x_ref, g_ref, t_ref, dx_ref, dxb_ref, dg_ref, l_ref):
        v = x_ref[...]
        gg = g_ref[...]
        r = lax.rsqrt(jnp.mean(v * v, axis=1, keepdims=True) + EPS)
        xh = v * r
        err = xh * gg - t_ref[...]
        part = 0.5 * _csum(jnp.mean(err * err, axis=1, keepdims=True))
        d = err * (1.0 / D)
        dxh = d * gg
        dx = r * (dxh - xh * jnp.mean(dxh * xh, axis=1, keepdims=True))
        dx_ref[...] = dx
        dxb_ref[...] = dx.astype(BF16)

        @pl.when(pl.program_id(0) == 0)
        def _():
            dg_ref[...] = jnp.zeros_like(dg_ref)
            l_ref[...] = jnp.zeros_like(l_ref)

        dg_ref[...] += _csum(d * xh)
        l_ref[...] += jnp.broadcast_to(part, (1, LANES))

    row = pl.BlockSpec((tm, D), lambda i: (i, 0))
    vec = pl.BlockSpec((1, D), lambda i: (0, 0))
    return _pc(body, name=name, grid=(T // tm,), in_specs=[row, vec, row],
               out_specs=[row, row, vec, pl.BlockSpec((1, LANES), lambda i: (0, 0))],
               out_shape=[SDS((T, D), F32), SDS((T, D), BF16), SDS((1, D), F32), SDS((1, LANES), F32)])(x, g, tgt)


HALO = 8


def _prev_map(ts, col):
    return lambda b, i, j: (b, jnp.maximum(i * (ts // HALO) - 1, 0), col(j))


def _next_map(ts, n_halo_blocks, col):
    return lambda b, i, j: (b, jnp.minimum((i + 1) * (ts // HALO), n_halo_blocks - 1), col(j))


def _row_chunks(ts, rows):
    rows = min(rows, ts)
    return [(r, rows) for r in range(0, ts, rows)]


def _conv_rows(ext, w_ref, b_ref, r0, n, K):
    win = ext[r0:r0 + HALO + n, :]
    taps = [pltpu.roll(win, K - 1 - k, 0)[HALO:HALO + n] if k < K - 1 else win[HALO:HALO + n] for k in range(K)]
    acc = b_ref[...] + w_ref[0:1, :] * taps[0]
    for k in range(1, K):
        acc = acc + w_ref[k:k + 1, :] * taps[k]
    return acc, taps


def _conv_t_rows(ext2, w_ref, r0, n, K):
    win = ext2[r0:r0 + n + HALO, :]
    dx = w_ref[K - 1:K, :] * win[0:n]
    for k in range(K - 1):
        dx = dx + w_ref[k:k + 1, :] * pltpu.roll(win, n + HALO - (K - 1 - k), 0)[0:n]
    return dx


def _sum8(x):
    acc = x[0:8]
    for r in range(8, x.shape[0], 8):
        acc = acc + x[r:r + 8]
    return acc


def conv_silu_fwd(proj3, w, b, *, name):
    Bl, S, _ = proj3.shape
    C, K = SSD_XBC, SSD_K
    ts, tc = _tile(S, 512, 8), 512
    c0 = XBC0 // tc

    def body(xp_ref, x_ref, w_ref, b_ref, o_ref, ext):
        i = pl.program_id(1)
        ext[0:HALO, :] = jnp.where(i > 0, xp_ref[0], 0.0)
        ext[HALO:HALO + ts, :] = x_ref[0]
        for r0, n in _row_chunks(ts, 32):
            acc, _ = _conv_rows(ext, w_ref, b_ref, r0, n, K)
            o_ref[0, r0:r0 + n, :] = _silu(acc)

    return _pc(body, name=name, grid=(Bl, S // ts, C // tc),
               in_specs=[pl.BlockSpec((1, HALO, tc), _prev_map(ts, lambda j: c0 + j)),
                         pl.BlockSpec((1, ts, tc), lambda b, i, j: (b, i, c0 + j)),
                         pl.BlockSpec((K, tc), lambda b, i, j: (0, j)),
                         pl.BlockSpec((1, tc), lambda b, i, j: (0, j))],
               out_specs=pl.BlockSpec((1, ts, tc), lambda b, i, j: (b, i, j)),
               out_shape=SDS((Bl, S, C), F32), scratch=[pltpu.VMEM((HALO + ts, tc), F32)])(proj3, proj3, w, b)


def conv_silu_bwd(proj3, w, b, dact, *, comm=None, name):
    Bl, S, _ = proj3.shape
    C, K = SSD_XBC, SSD_K
    ts, tc = _tile(S, 512, 8), 512
    c0 = XBC0 // tc
    ns = S // ts

    def body(xp_ref, x_ref, xn_ref, d_ref, dn_ref, w_ref, b_ref, dx_ref, dw_ref, db_ref, ext, ext2):
        bb, i = pl.program_id(1), pl.program_id(2)
        last = i == ns - 1
        ext[0:HALO, :] = jnp.where(i > 0, xp_ref[0], 0.0)
        ext[HALO:HALO + ts, :] = x_ref[0]
        ext[HALO + ts:2 * HALO + ts, :] = jnp.where(last, 0.0, xn_ref[0])
        dw = [jnp.zeros((8, tc), F32) for _ in range(K)]
        db = jnp.zeros((8, tc), F32)
        for r0, n in _row_chunks(ts, 16) + [(ts, HALO)]:
            acc, taps = _conv_rows(ext, w_ref, b_ref, r0, n, K)
            d = d_ref[0, r0:r0 + n, :] if r0 < ts else jnp.where(last, 0.0, dn_ref[0])
            du = d * _dsilu(acc)
            ext2[r0:r0 + n, :] = du
            if r0 < ts:
                dw = [a + _sum8(du * t) for a, t in zip(dw, taps)]
                db = db + _sum8(du)
        for r0, n in _row_chunks(ts, 32):
            dx_ref[0, r0:r0 + n, :] = _conv_t_rows(ext2, w_ref, r0, n, K).astype(BF16)

        @pl.when((bb == 0) & (i == 0))
        def _():
            dw_ref[...] = jnp.zeros_like(dw_ref)
            db_ref[...] = jnp.zeros_like(db_ref)

        for k in range(K):
            dw_ref[k:k + 1, :] += _csum(dw[k])
        db_ref[...] += _csum(db)

    nhb = S // HALO
    cx = lambda j: c0 + j
    cj = lambda j: j
    return _pc(body, name=name, grid=(C // tc, Bl, ns),
               in_specs=[pl.BlockSpec((1, HALO, tc), lambda j, b, i: _prev_map(ts, cx)(b, i, j)),
                         pl.BlockSpec((1, ts, tc), lambda j, b, i: (b, i, c0 + j)),
                         pl.BlockSpec((1, HALO, tc), lambda j, b, i: _next_map(ts, nhb, cx)(b, i, j)),
                         pl.BlockSpec((1, ts, tc), lambda j, b, i: (b, i, j)),
                         pl.BlockSpec((1, HALO, tc), lambda j, b, i: _next_map(ts, nhb, cj)(b, i, j)),
                         pl.BlockSpec((K, tc), lambda j, b, i: (0, j)),
                         pl.BlockSpec((1, tc), lambda j, b, i: (0, j))],
               out_specs=[pl.BlockSpec((1, ts, tc), lambda j, b, i: (b, i, j)),
                          pl.BlockSpec((K, tc), lambda j, b, i: (0, j)),
                          pl.BlockSpec((1, tc), lambda j, b, i: (0, j))],
               out_shape=[SDS((Bl, S, C), BF16), SDS((K, C), F32), SDS((1, C), F32)],
               scratch=[pltpu.VMEM((2 * HALO + ts, tc), F32), pltpu.VMEM((HALO + ts, tc), F32)],
               comm=comm)(proj3, proj3, proj3, dact, dact, w, b)


def ffn_act_fwd(pre3, w, b, *, name):
    Bl, S, _ = pre3.shape
    K = FFN_K
    ts, tc = _tile(S, 512, 8), 256
    nj = DFF // tc

    def body(gp_ref, g_ref, vp_ref, v_ref, wg_ref, wv_ref, bg_ref, bv_ref, o_ref, eg, ev):
        i = pl.program_id(1)
        for p_ref, m_ref, ext in ((gp_ref, g_ref, eg), (vp_ref, v_ref, ev)):
            ext[0:HALO, :] = jnp.where(i > 0, p_ref[0], 0.0)
            ext[HALO:HALO + ts, :] = m_ref[0]
        for r0, n in _row_chunks(ts, 64):
            ug, _ = _conv_rows(eg, wg_ref, bg_ref, r0, n, K)
            uv, _ = _conv_rows(ev, wv_ref, bv_ref, r0, n, K)
            o_ref[0, r0:r0 + n, :] = (_silu(ug) * uv).astype(BF16)

    main = lambda off: pl.BlockSpec((1, ts, tc), lambda b, i, j: (b, i, off + j))
    prev = lambda off: pl.BlockSpec((1, HALO, tc), _prev_map(ts, lambda j: off + j))
    wsp = lambda off: pl.BlockSpec((K, tc), lambda b, i, j: (0, off + j))
    bsp = lambda off: pl.BlockSpec((1, tc), lambda b, i, j: (0, off + j))
    return _pc(body, name=name, grid=(Bl, S // ts, nj),
               in_specs=[prev(0), main(0), prev(nj), main(nj), wsp(0), wsp(nj), bsp(0), bsp(nj)],
               out_specs=pl.BlockSpec((1, ts, tc), lambda b, i, j: (b, i, j)),
               out_shape=SDS((Bl, S, DFF), BF16),
               scratch=[pltpu.VMEM((HALO + ts, tc), F32), pltpu.VMEM((HALO + ts, tc), F32)],
               )(pre3, pre3, pre3, pre3, w, w, b, b)


def ffn_act_bwd(pre3, w, b, dact, *, comm=None, name):
    Bl, S, _ = pre3.shape
    K = FFN_K
    ts, tc = _tile(S, 512, 8), 256
    nj = DFF // tc
    ns = S // ts

    def body(gp_ref, g_ref, gn_ref, vp_ref, v_ref, vn_ref, d_ref, dn_ref, wg_ref, wv_ref, bg_ref, bv_ref,
             dg_ref, dv_ref, dwg_ref, dwv_ref, dbg_ref, dbv_ref, eg, ev, e2g, e2v):
        bb, i = pl.program_id(1), pl.program_id(2)
        last = i == ns - 1
        for p_ref, m_ref, n_ref, ext in ((gp_ref, g_ref, gn_ref, eg), (vp_ref, v_ref, vn_ref, ev)):
            ext[0:HALO, :] = jnp.where(i > 0, p_ref[0], 0.0)
            ext[HALO:HALO + ts, :] = m_ref[0]
            ext[HALO + ts:2 * HALO + ts, :] = jnp.where(last, 0.0, n_ref[0])
        zero8 = jnp.zeros((8, tc), F32)
        dwg, dwv, dbg, dbv = [zero8] * K, [zero8] * K, zero8, zero8
        for r0, n in _row_chunks(ts, 32) + [(ts, HALO)]:
            ug, tg = _conv_rows(eg, wg_ref, bg_ref, r0, n, K)
            uv, tv = _conv_rows(ev, wv_ref, bv_ref, r0, n, K)
            d = d_ref[0, r0:r0 + n, :] if r0 < ts else jnp.where(last, 0.0, dn_ref[0])
            sg = _sig(ug)
            act = ug * sg
            dug = d * uv * (sg + act * (1.0 - sg))
            duv = d * act
            e2g[r0:r0 + n, :] = dug
            e2v[r0:r0 + n, :] = duv
            if r0 < ts:
                dwg = [a + _sum8(dug * t) for a, t in zip(dwg, tg)]
                dwv = [a + _sum8(duv * t) for a, t in zip(dwv, tv)]
                dbg, dbv = dbg + _sum8(dug), dbv + _sum8(duv)
        for w_ref, e2, o_ref in ((wg_ref, e2g, dg_ref), (wv_ref, e2v, dv_ref)):
            for r0, n in _row_chunks(ts, 64):
                o_ref[0, r0:r0 + n, :] = _conv_t_rows(e2, w_ref, r0, n, K).astype(BF16)

        @pl.when((bb == 0) & (i == 0))
        def _():
            for r in (dwg_ref, dwv_ref, dbg_ref, dbv_ref):
                r[...] = jnp.zeros_like(r)

        for dw_ref, dw, db_ref, db in ((dwg_ref, dwg, dbg_ref, dbg), (dwv_ref, dwv, dbv_ref, dbv)):
            for k in range(K):
                dw_ref[k:k + 1, :] += _csum(dw[k])
            db_ref[...] += _csum(db)

    nhb = S // HALO
    main = lambda off: pl.BlockSpec((1, ts, tc), lambda j, b, i: (b, i, off + j))
    prev = lambda off: pl.BlockSpec((1, HALO, tc), lambda j, b, i: _prev_map(ts, lambda jj: off + jj)(b, i, j))
    nxt = lambda off: pl.BlockSpec((1, HALO, tc), lambda j, b, i: _next_map(ts, nhb, lambda jj: off + jj)(b, i, j))
    wsp = lambda off: pl.BlockSpec((K, tc), lambda j, b, i: (0, off + j))
    bsp = lambda off: pl.BlockSpec((1, tc), lambda j, b, i: (0, off + j))
    outs = _pc(body, name=name, grid=(nj, Bl, ns),
               in_specs=[prev(0), main(0), nxt(0), prev(nj), main(nj), nxt(nj), main(0), nxt(0),
                         wsp(0), wsp(nj), bsp(0), bsp(nj)],
               out_specs=[main(0), main(0), wsp(0), wsp(0), bsp(0), bsp(0)],
               out_shape=[SDS((Bl, S, DFF), BF16), SDS((Bl, S, DFF), BF16), SDS((K, DFF), F32), SDS((K, DFF), F32),
                          SDS((1, DFF), F32), SDS((1, DFF), F32)],
               scratch=[pltpu.VMEM((2 * HALO + ts, tc), F32), pltpu.VMEM((2 * HALO + ts, tc), F32),
                        pltpu.VMEM((HALO + ts, tc), F32), pltpu.VMEM((HALO + ts, tc), F32)],
               comm=comm)(pre3, pre3, pre3, pre3, pre3, pre3, dact, dact, w, w, b, b)
    return outs


PHALO = 16


def _pool_window_sums(win, trailing):
    rows = win.shape[0]
    out, s = [], win
    for w in POOL_WIN:
        half = w // 2
        s = s + pltpu.roll(s, half if trailing else rows - half, 0)
        out.append(s)
    return out


def _pick(g, vals):
    r = vals[-1]
    for k in range(len(vals) - 2, -1, -1):
        r = jnp.where(g == k, vals[k], r)
    return r


def _pool_count(g, i, ts, rows, r0=0):
    t = (i * ts + r0 + lax.broadcasted_iota(jnp.int32, (rows, 1), 0) + 1).astype(F32)
    return jnp.minimum(t, _pick(g, [float(w) for w in POOL_WIN]))


def _fill_pool_ext(up_ref, u_ref, ext, i, ts):
    ext[0:PHALO, :] = jnp.where(i > 0, up_ref[0], 0.0)
    ext[PHALO:PHALO + ts, :] = u_ref[0]


def _pooled_rows(ext, g, i, ts, r0, n):
    win = ext[r0:r0 + n + PHALO, :]
    sums = _pool_window_sums(win, True)
    return _pick(g, sums)[PHALO:PHALO + n] / _pool_count(g, i, ts, n, r0) - win[PHALO:PHALO + n]


def pool_fwd(proj3, pool_w, scale, *, name):
    Bl, S, _ = proj3.shape
    ts = _tile(S, 512, 16)
    c0 = U0 // POOL_D

    def body(up_ref, u_ref, w_ref, s_ref, o_ref, ext):
        i, g = pl.program_id(1), pl.program_id(2)
        _fill_pool_ext(up_ref, u_ref, ext, i, ts)
        wm, sc = w_ref[0], s_ref[...]
        for r0, n in _row_chunks(ts, 128):
            o_ref[0, r0:r0 + n, :] = (_nn(_pooled_rows(ext, g, i, ts, r0, n), wm) * sc).astype(BF16)

    return _pc(body, name=name, grid=(Bl, S // ts, POOL_G),
               in_specs=[pl.BlockSpec((1, PHALO, POOL_D), lambda b, i, g: (b, jnp.maximum(i * (ts // PHALO) - 1, 0), c0 + g)),
                         pl.BlockSpec((1, ts, POOL_D), lambda b, i, g: (b, i, c0 + g)),
                         pl.BlockSpec((1, POOL_D, POOL_D), lambda b, i, g: (g, 0, 0)),
                         pl.BlockSpec((1, POOL_D), lambda b, i, g: (0, g))],
               out_specs=pl.BlockSpec((1, ts, POOL_D), lambda b, i, g: (b, i, g)),
               out_shape=SDS((Bl, S, POOL_W), BF16), scratch=[pltpu.VMEM((PHALO + ts, POOL_D), F32)],
               )(proj3, proj3, pool_w, scale)


def pool_bwd(proj3, pool_w, scale, dycat3, *, name):
    Bl, S, _ = proj3.shape
    ts = _tile(S, 512, 16)
    ns = S // ts
    c0 = U0 // POOL_D
    d0 = SSD_W // POOL_D
    nhb = S // PHALO

    def body(up_ref, u_ref, d_ref, dn_ref, w_ref, s_ref, du_ref, dw_ref, ds_ref, ext, ext2):
        g, bb, i = pl.program_id(0), pl.program_id(1), pl.program_id(2)
        last = i == ns - 1
        _fill_pool_ext(up_ref, u_ref, ext, i, ts)
        wm = w_ref[0]
        sc = s_ref[...]
        dwa = jnp.zeros((POOL_D, POOL_D), F32)
        dsa = jnp.zeros((8, POOL_D), F32)
        dpools = []
        for r0, n in _row_chunks(ts, 128):
            pooled = _pooled_rows(ext, g, i, ts, r0, n)
            dy = d_ref[0, r0:r0 + n, :]
            dp = dy * sc
            dpool = _nt(dp, wm)
            dpools.append(dpool)
            ext2[r0:r0 + n, :] = dpool / _pool_count(g, i, ts, n, r0)
            dwa = dwa + _tn(pooled, dp)
            dsa = dsa + _sum8(dy * _nn(pooled, wm))
        dpool_n = _nt(jnp.where(last, 0.0, dn_ref[0]) * sc, wm)
        ext2[ts:ts + PHALO, :] = dpool_n / _pool_count(g, i + 1, ts, PHALO)
        for (r0, n), dpool in zip(_row_chunks(ts, 128), dpools):
            sums = _pool_window_sums(ext2[r0:r0 + n + PHALO, :], False)
            du_ref[0, r0:r0 + n, :] = (_pick(g, sums)[0:n] - dpool).astype(BF16)

        @pl.when((bb == 0) & (i == 0))
        def _():
            dw_ref[...] = jnp.zeros_like(dw_ref)
            ds_ref[...] = jnp.zeros_like(ds_ref)

        dw_ref[0] += dwa
        ds_ref[...] += _csum(dsa)

    return _pc(body, name=name, grid=(POOL_G, Bl, ns),
               in_specs=[pl.BlockSpec((1, PHALO, POOL_D), lambda g, b, i: (b, jnp.maximum(i * (ts // PHALO) - 1, 0), c0 + g)),
                         pl.BlockSpec((1, ts, POOL_D), lambda g, b, i: (b, i, c0 + g)),
                         pl.BlockSpec((1, ts, POOL_D), lambda g, b, i: (b, i, d0 + g)),
                         pl.BlockSpec((1, PHALO, POOL_D), lambda g, b, i: (b, jnp.minimum((i + 1) * (ts // PHALO), nhb - 1), d0 + g)),
                         pl.BlockSpec((1, POOL_D, POOL_D), lambda g, b, i: (g, 0, 0)),
                         pl.BlockSpec((1, POOL_D), lambda g, b, i: (0, g))],
               out_specs=[pl.BlockSpec((1, ts, POOL_D), lambda g, b, i: (b, i, g)),
                          pl.BlockSpec((1, POOL_D, POOL_D), lambda g, b, i: (g, 0, 0)),
                          pl.BlockSpec((1, POOL_D), lambda g, b, i: (0, g))],
               out_shape=[SDS((Bl, S, POOL_W), BF16), SDS((POOL_G, POOL_D, POOL_D), F32), SDS((1, POOL_W), F32)],
               scratch=[pltpu.VMEM((PHALO + ts, POOL_D), F32), pltpu.VMEM((PHALO + ts, POOL_D), F32)],
               )(proj3, proj3, dycat3, dycat3, pool_w, scale)


NPAIR = SSD_HEADS // 2


def _ssd_common(sm, bias, alog):
    L = SSD_L
    dt = jax.nn.softplus(sm + bias)
    a = -jnp.exp(alog)
    da = dt * a
    r = lax.broadcasted_iota(jnp.int32, (L, L), 0)
    c = lax.broadcasted_iota(jnp.int32, (L, L), 1)
    tri = (r >= c).astype(F32)
    cum = _dg(tri, da, 1, 0, lax.Precision.HIGHEST)
    return dt, a, cum, cum.T, r >= c


def _lanes(lo, hi, shape=(1, LANES)):
    lane = lax.broadcasted_iota(jnp.int32, shape, len(shape) - 1)
    return (lane >= lo) & (lane < hi)


def _onehot_lane(h):
    return (lax.broadcasted_iota(jnp.int32, (1, LANES), 1) == h).astype(F32)


def _split_nn(a, e):
    hi = a.astype(BF16)
    lo = (a - hi.astype(F32)).astype(BF16)
    return _dg(hi, e, 1, 0) + _dg(lo, e, 1, 0)


def _head_spread():
    r = lax.broadcasted_iota(jnp.int32, (LANES, SSD_W), 0)
    c = lax.broadcasted_iota(jnp.int32, (LANES, SSD_W), 1)
    return (c // SSD_P == r).astype(BF16)


def _pair_gather(j):
    r = lax.broadcasted_iota(jnp.int32, (LANES, LANES), 0)
    c = lax.broadcasted_iota(jnp.int32, (LANES, LANES), 1)
    return (c == 2 * j + (r >= SSD_P).astype(jnp.int32)).astype(BF16)


def ssd_fwd(xbc3, proj3, bias, alog, dskip, *, comm=None, name):
    Bl, S, _ = xbc3.shape
    L = SSD_L
    nc = S // L

    def body(xbc_ref, sm_ref, bias_ref, alog_ref, d_ref, y_ref, hin_ref, H):
        c = pl.program_id(1)

        @pl.when(c == 0)
        def _():
            H[...] = jnp.zeros_like(H)

        dt, a, cum, cumT, mask = _ssd_common(sm_ref[0], bias_ref[...], alog_ref[...])
        lo = _lanes(0, SSD_P)
        rowlo = lax.broadcasted_iota(jnp.int32, (LANES, LANES), 0) < SSD_P
        spread = _head_spread()
        dt_x = _split_nn(dt, spread)
        el_x = _split_nn(jnp.exp(cum), spread)
        wl_x = _split_nn(jnp.exp(cum[L - 1:L, :] - cum), spread)
        cb = []
        for g in range(SSD_G):
            Bg = xbc_ref[0, :, SSD_W + g * SSD_N:SSD_W + (g + 1) * SSD_N]
            Cg = xbc_ref[0, :, SSD_W + SSD_G * SSD_N + g * SSD_N:SSD_W + SSD_G * SSD_N + (g + 1) * SSD_N]
            cb.append((Bg, Cg, _nt(Cg, Bg)))
        for j in range(NPAIR):
            h0, h1 = 2 * j, 2 * j + 1
            sl = slice(j * LANES, (j + 1) * LANES)
            Bg, Cg, CB = cb[j // (NPAIR // SSD_G)]
            X = xbc_ref[0, :, sl]
            c0, c1 = cum[:, h0:h0 + 1], cum[:, h1:h1 + 1]
            r0, r1 = cumT[h0:h0 + 1, :], cumT[h1:h1 + 1, :]
            cl0, cl1 = cum[L - 1:L, h0:h0 + 1], cum[L - 1:L, h1:h1 + 1]
            Xt = X * dt_x[:, sl]
            M0 = CB * jnp.exp(jnp.where(mask, c0 - r0, NEG))
            M1 = CB * jnp.exp(jnp.where(mask, c1 - r1, NEG))
            Yd = jnp.where(lo, _nn(M0, Xt), _nn(M1, Xt))
            Hp = H[j]
            hin_ref[0, 0, j] = Hp
            Z = _nt(Cg, Hp)
            y_ref[0, :, sl] = Yd + el_x[:, sl] * Z + X * d_ref[j:j + 1, :]
            H[j] = jnp.where(rowlo, jnp.exp(cl0), jnp.exp(cl1)) * Hp + _tn(wl_x[:, sl] * Xt, Bg)

    vec = pl.BlockSpec((1, LANES), lambda b, c: (0, 0))
    return _pc(body, name=name, grid=(Bl, nc),
               in_specs=[pl.BlockSpec((1, L, SSD_XBC), lambda b, c: (b, c, 0)),
                         pl.BlockSpec((1, L, LANES), lambda b, c: (b, c, DT0 // LANES)),
                         vec, vec, pl.BlockSpec((NPAIR, LANES), lambda b, c: (0, 0))],
               out_specs=[pl.BlockSpec((1, L, SSD_W), lambda b, c: (b, c, 0)),
                          pl.BlockSpec((1, 1, NPAIR, LANES, LANES), lambda b, c: (b, c, 0, 0, 0))],
               out_shape=[SDS((Bl, S, SSD_W), F32), SDS((Bl, nc, NPAIR, LANES, LANES), F32)],
               scratch=[pltpu.VMEM((NPAIR, LANES, LANES), F32)], comm=comm)(xbc3, proj3, bias, alog, dskip)


def ssd_bwd(xbc3, proj3, hin, dy3, bias, alog, dskip, *, comm=None, name):
    Bl, S, _ = xbc3.shape
    L = SSD_L
    nc = S // L

    def body(xbc_ref, sm_ref, hin_ref, dy_ref, bias_ref, alog_ref, d_ref, dx_ref, ddt_ref, dpar_ref, dd_ref, dH, ddacc):
        bb, i = pl.program_id(0), pl.program_id(1)

        @pl.when(i == 0)
        def _():
            dH[...] = jnp.zeros_like(dH)

        @pl.when((bb == 0) & (i == 0))
        def _():
            dpar_ref[...] = jnp.zeros_like(dpar_ref)
            ddacc[...] = jnp.zeros_like(ddacc)

        sm = sm_ref[0]
        dt, a, cum, cumT, mask = _ssd_common(sm, bias_ref[...], alog_ref[...])
        maskf = mask.astype(F32)
        lo = _lanes(0, SSD_P)
        rowlo = lax.broadcasted_iota(jnp.int32, (LANES, LANES), 0) < SSD_P
        lastrow = (lax.broadcasted_iota(jnp.int32, (L, 1), 0) == L - 1).astype(F32)
        dcum = jnp.zeros((L, LANES), F32)
        dcum_t = jnp.zeros((LANES, L), F32)
        ddt = jnp.zeros((L, LANES), F32)
        spread = _head_spread()
        ones = jnp.ones((L, LANES), BF16)
        ecum = jnp.exp(cum)
        wall = jnp.exp(cum[L - 1:L, :] - cum)
        dt_x = _split_nn(dt, spread)
        el_x = _split_nn(ecum, spread)
        wl_x = _split_nn(wall, spread)
        headrow = lax.broadcasted_iota(jnp.int32, (LANES, 1), 0)
        grp = []
        for g in range(SSD_G):
            Bg = xbc_ref[0, :, SSD_W + g * SSD_N:SSD_W + (g + 1) * SSD_N]
            Cg = xbc_ref[0, :, SSD_W + SSD_G * SSD_N + g * SSD_N:SSD_W + SSD_G * SSD_N + (g + 1) * SSD_N]
            grp.append(dict(B=Bg, C=Cg, CB=_nt(Cg, Bg), dB=jnp.zeros((L, SSD_N), F32), dC=jnp.zeros((L, SSD_N), F32),
                            dCB=jnp.zeros((L, L), F32)))
        for j in range(NPAIR):
            h0, h1 = 2 * j, 2 * j + 1
            sl = slice(j * LANES, (j + 1) * LANES)
            G = grp[j // (NPAIR // SSD_G)]
            Bg, Cg, CB = G["B"], G["C"], G["CB"]
            X = xbc_ref[0, :, sl]
            dY = dy_ref[0, :, sl]
            c0, c1 = cum[:, h0:h0 + 1], cum[:, h1:h1 + 1]
            r0, r1 = cumT[h0:h0 + 1, :], cumT[h1:h1 + 1, :]
            cl0, cl1 = cum[L - 1:L, h0:h0 + 1], cum[L - 1:L, h1:h1 + 1]
            oh0, oh1 = _onehot_lane(h0), _onehot_lane(h1)
            gather = _pair_gather(j)
            dtl, el, wl = dt_x[:, sl], el_x[:, sl], wl_x[:, sl]
            Xt = X * dtl
            Hp = hin_ref[0, 0, j]
            dS = dH[j]
            dX = dY * d_ref[j:j + 1, :]
            ddacc[j:j + 1, :] += _csum(dY * X)
            Z = _nt(Cg, Hp)
            dZ = dY * el
            dcum = dcum + _split_nn(dY * Z, gather) * ecum
            G["dC"] = G["dC"] + _nn(dZ, Hp)
            dHy = _tn(dZ, Cg)
            Gm = _nt(Bg, dS)
            dXt = wl * Gm
            q = _split_nn(Xt * Gm, gather) * wall
            dcum = dcum + lastrow * _csum(q) - q
            G["dB"] = G["dB"] + _nn(wl * Xt, dS)
            g0, g1 = jnp.exp(cl0), jnp.exp(cl1)
            rowsum = _nn(dS * Hp, ones)
            dg0 = _csum(jnp.where(rowlo, rowsum, 0.0))
            dg1 = _csum(jnp.where(rowlo, 0.0, rowsum))
            dcum = dcum + lastrow * ((dg0 * g0) * oh0 + (dg1 * g1) * oh1)
            dH[j] = jnp.where(rowlo, g0, g1) * dS + dHy
            for h, ch, rh, mh, oh in ((h0, c0, r0, lo, oh0), (h1, c1, r1, jnp.logical_not(lo), oh1)):
                decay = jnp.exp(jnp.where(mask, ch - rh, NEG))
                Mh = CB * decay
                dM = _nt(jnp.where(mh, dY, 0.0), Xt) * maskf
                dXt = dXt + jnp.where(mh, _tn(Mh, dY), 0.0)
                G["dCB"] = G["dCB"] + dM * decay
                Q = dM * Mh
                dcum = dcum + _rsum(Q) * oh
                dcum_t = dcum_t + (headrow == h).astype(F32) * _csum(Q)
            dX = dX + dXt * dtl
            ddt = ddt + _split_nn(dXt * X, gather)
            dx_ref[0, :, sl] = dX
        dcum = dcum - dcum_t.T
        for g in range(SSD_G):
            G = grp[g]
            dC = G["dC"] + _nn(G["dCB"], G["B"])
            dB = G["dB"] + _tn(G["dCB"], G["C"])
            dx_ref[0, :, SSD_W + g * SSD_N:SSD_W + (g + 1) * SSD_N] = dB
            dx_ref[0, :, SSD_W + SSD_G * SSD_N + g * SSD_N:SSD_W + SSD_G * SSD_N + (g + 1) * SSD_N] = dC
        r = lax.broadcasted_iota(jnp.int32, (L, L), 0)
        c = lax.broadcasted_iota(jnp.int32, (L, L), 1)
        dda = _dg((c >= r).astype(F32), dcum, 1, 0, lax.Precision.HIGHEST)
        heads = _lanes(0, SSD_HEADS)
        ddt = ddt + dda * a
        draw = jnp.where(heads, ddt * _sig(sm + bias_ref[...]), 0.0)
        ddt_ref[0] = draw.astype(BF16)
        dpar_ref[0:1, :] += _csum(draw)
        dpar_ref[1:2, :] += _csum(jnp.where(heads, dda * dt * a, 0.0))

        @pl.when((bb == Bl - 1) & (i == nc - 1))
        def _():
            acc = ddacc[...]
            lane = lax.broadcasted_iota(jnp.int32, (NPAIR, LANES), 1)
            s0 = _rsum(jnp.where(lane < SSD_P, acc, 0.0))
            s1 = _rsum(jnp.where(lane < SSD_P, 0.0, acc))
            dd_ref[...] = jnp.where(lane == 0, s0, jnp.where(lane == 1, s1, 0.0))

    vec = pl.BlockSpec((1, LANES), lambda b, i: (0, 0))
    par = pl.BlockSpec((NPAIR, LANES), lambda b, i: (0, 0))
    return _pc(body, name=name, grid=(Bl, nc),
               in_specs=[pl.BlockSpec((1, L, SSD_XBC), lambda b, i: (b, nc - 1 - i, 0)),
                         pl.BlockSpec((1, L, LANES), lambda b, i: (b, nc - 1 - i, DT0 // LANES)),
                         pl.BlockSpec((1, 1, NPAIR, LANES, LANES), lambda b, i: (b, nc - 1 - i, 0, 0, 0)),
                         pl.BlockSpec((1, L, SSD_W), lambda b, i: (b, nc - 1 - i, 0)),
                         vec, vec, par],
               out_specs=[pl.BlockSpec((1, L, SSD_XBC), lambda b, i: (b, nc - 1 - i, 0)),
                          pl.BlockSpec((1, L, LANES), lambda b, i: (b, nc - 1 - i, 0)),
                          par, par],
               out_shape=[SDS((Bl, S, SSD_XBC), F32), SDS((Bl, S, LANES), BF16), SDS((NPAIR, LANES), F32),
                          SDS((NPAIR, LANES), F32)],
               scratch=[pltpu.VMEM((NPAIR, LANES, LANES), F32), pltpu.VMEM((NPAIR, LANES), F32)],
               comm=comm)(xbc3, proj3, hin, dy3, bias, alog, dskip)


PE_LO, PE_MID, PE_HI = MLA_NOPE, MLA_NOPE + MLA_ROPE // 2, MLA_NOPE + MLA_ROPE
ATT_SCALE = 1.0 / math.sqrt(MLA_QK)


def _swap_matrix():
    src = lax.broadcasted_iota(jnp.int32, (LANES, LANES), 0)
    dst = lax.broadcasted_iota(jnp.int32, (LANES, LANES), 1)
    half = MLA_ROPE // 2
    first = (dst >= PE_LO) & (dst < PE_MID) & (src == dst + half)
    second = (dst >= PE_MID) & (dst < PE_HI) & (src == dst - half)
    return (second.astype(F32) - first.astype(F32)).astype(BF16)


def rope_tables(pos, invf, *, name):
    T = pos.shape[0]
    tm = _tile(T, 512, 8)

    def body(pos_ref, f_ref, c_ref, s_ref):
        ang = pos_ref[...] * f_ref[...]
        pe = _lanes(PE_LO, PE_HI)
        c_ref[...] = jnp.where(pe, jnp.cos(ang), 1.0)
        s_ref[...] = jnp.where(pe, jnp.sin(ang), 0.0)

    tile = pl.BlockSpec((tm, LANES), lambda i: (i, 0))
    return _pc(body, name=name, grid=(T // tm,),
               in_specs=[pl.BlockSpec((tm, 1), lambda i: (i, 0)), pl.BlockSpec((1, LANES), lambda i: (0, 0))],
               out_specs=[tile, tile], out_shape=[SDS((T, LANES), F32)] * 2)(pos, invf)


V_ONE = MLA_V


def mla_prep_fwd(qt, kvt, proj, cs, sn, *, name):
    T = qt.shape[0]
    tm = _tile(T, 256, 8)
    HW = MLA_H * LANES

    def body(q_ref, k_ref, v_ref, kpe_ref, c_ref, s_ref, qo_ref, ko_ref, vo_ref):
        c, s = c_ref[...], s_ref[...]
        kpe = kpe_ref[...]
        sw = _swap_matrix()
        one = _lanes(V_ONE, V_ONE + 1)
        for h in range(MLA_H):
            sl = slice(h * LANES, (h + 1) * LANES)
            q = q_ref[:, sl]
            k = k_ref[:, sl] + kpe
            qo_ref[:, sl] = ((q * c + _split_nn(q, sw) * s) * ATT_SCALE).astype(BF16)
            ko_ref[:, sl] = (k * c + _split_nn(k, sw) * s).astype(BF16)
            vo_ref[:, sl] = jnp.where(one, 1.0, v_ref[:, sl]).astype(BF16)

    row = pl.BlockSpec((tm, HW), lambda i: (i, 0))
    tab = pl.BlockSpec((tm, LANES), lambda i: (i, 0))
    return _pc(body, name=name, grid=(T // tm,),
               in_specs=[row, row, pl.BlockSpec((tm, HW), lambda i: (i, 1)),
                         pl.BlockSpec((tm, LANES), lambda i: (i, KPE0 // LANES)), tab, tab],
               out_specs=[row, row, row], out_shape=[SDS((T, HW), BF16)] * 3)(qt, kvt, kvt, proj, cs, sn)


def mla_prep_bwd(dqr, dkr, cs, sn, *, name):
    T = dqr.shape[0]
    tm = _tile(T, 256, 8)
    HW = MLA_H * LANES

    def body(dq_ref, dk_ref, c_ref, s_ref, qo_ref, ko_ref, kpe_ref):
        c, s = c_ref[...], s_ref[...]
        sw = _swap_matrix()
        pe = _lanes(PE_LO, PE_HI)
        dkpe = jnp.zeros((tm, LANES), F32)
        for h in range(MLA_H):
            sl = slice(h * LANES, (h + 1) * LANES)
            dq = dq_ref[:, sl] * ATT_SCALE
            dk = dk_ref[:, sl]
            qo_ref[:, sl] = (dq * c - _split_nn(dq * s, sw)).astype(BF16)
            dkk = dk * c - _split_nn(dk * s, sw)
            ko_ref[:, sl] = jnp.where(pe, 0.0, dkk).astype(BF16)
            dkpe = dkpe + jnp.where(pe, dkk, 0.0)
        kpe_ref[...] = dkpe.astype(BF16)

    row = pl.BlockSpec((tm, HW), lambda i: (i, 0))
    tab = pl.BlockSpec((tm, LANES), lambda i: (i, 0))
    return _pc(body, name=name, grid=(T // tm,), in_specs=[row, row, tab, tab], out_specs=[row, row, tab],
               out_shape=[SDS((T, HW), BF16), SDS((T, HW), BF16), SDS((T, LANES), BF16)])(dqr, dkr, cs, sn)


def _att_tile(S):
    return _tile(S, 512, LANES)


def _rep(x, n):
    return x if n == 1 else jnp.concatenate([x] * n, axis=1)


def _diag_mask(t, transposed=False):
    r = lax.broadcasted_iota(jnp.int32, (t, t), 0)
    c = lax.broadcasted_iota(jnp.int32, (t, t), 1)
    return (c >= r) if transposed else (c <= r)


def flash_fwd(qr, kr, vr, Bl, *, comm=None, name):
    T = qr.shape[0]
    S = T // Bl
    t = _att_tile(S)
    n = S // t
    nl = t // LANES

    def body(q_ref, k_ref, v_ref, o_ref, lset_ref, m, acc):
        qi = pl.program_id(2)
        q = q_ref[...]
        m[...] = jnp.full_like(m, NEG)
        acc[...] = jnp.zeros_like(acc)

        def block(kj, masked):
            off = pl.multiple_of(kj * t, t)
            s = _nt(q, k_ref[pl.ds(off, t), :])
            if masked:
                s = jnp.where(_diag_mask(t), s, NEG)
            mo = m[...]
            mn = jnp.maximum(mo, jnp.max(s, axis=1, keepdims=True))
            p = jnp.exp((s - _rep(mn, nl)).astype(BF16))
            acc[...] = jnp.exp(mo - mn) * acc[...] + _nn(p, v_ref[pl.ds(off, t), :])
            m[...] = mn

        def loop(kj, c):
            block(kj, False)
            return c

        lax.fori_loop(0, qi, loop, 0)
        block(qi, True)
        a = acc[...]
        l = a[:, V_ONE:V_ONE + 1]
        o_ref[...] = jnp.where(_lanes(0, MLA_V), a / l, 0.0).astype(BF16)
        lset_ref[...] = (m[...] + jnp.log(l)).T[0:8, :]

    qs = pl.BlockSpec((t, LANES), lambda b, h, qi: (b * n + qi, h))
    seq = pl.BlockSpec((S, LANES), lambda b, h, qi: (b, h))
    return _pc(body, name=name, grid=(Bl, MLA_H, n), in_specs=[qs, seq, seq],
               out_specs=[qs, pl.BlockSpec((8, t), lambda b, h, qi: (b * MLA_H + h, qi))],
               out_shape=[SDS((T, MLA_H * LANES), BF16), SDS((Bl * MLA_H * 8, S), F32)],
               scratch=[pltpu.VMEM((t, LANES), F32), pltpu.VMEM((t, LANES), F32)], comm=comm)(qr, kr, vr)


def flash_bwd(qr, kr, vr, o, lset, dycat, Bl, *, comm=None, name):
    T = qr.shape[0]
    S = T // Bl
    t = _att_tile(S)
    n = S // t
    do0 = (SSD_W + POOL_W) // LANES

    def body(q_ref, k_ref, v_ref, o_ref, lset_ref, do_ref, dq_ref, dk_ref, dv_ref, dka, dva, dlt_ref):
        kj = pl.program_id(2)

        @pl.when(kj == 0)
        def _():
            dq_ref[...] = jnp.zeros_like(dq_ref)
            for r in range(0, S, t):
                d = _rsum(do_ref[r:r + t, :].astype(F32) * o_ref[r:r + t, :].astype(F32))
                dlt_ref[:, r:r + t] = jnp.broadcast_to(d, (t, LANES)).T[0:8, :]

        k = k_ref[...]
        v = v_ref[...]
        dka[...] = jnp.zeros_like(dka)
        dva[...] = jnp.zeros_like(dva)

        def block(qi, masked):
            off = pl.multiple_of(qi * t, t)
            q = q_ref[pl.ds(off, t), :]
            do = do_ref[pl.ds(off, t), :].astype(BF16)
            st = _nt(k, q)
            if masked:
                st = jnp.where(_diag_mask(t, True), st, NEG)
            pt = jnp.exp((st - lset_ref[0:1, pl.ds(off, t)]).astype(BF16))
            dst = pt * (_nt(v, do) - dlt_ref[0:1, pl.ds(off, t)])
            dva[...] += _nn(pt, do)
            dka[...] += _nn(dst, q)
            dq_ref[pl.ds(off, t), :] += _tn(dst, k)

        def loop(qi, c):
            block(qi, False)
            return c

        block(kj, True)
        lax.fori_loop(kj + 1, n, loop, 0)
        dk_ref[...] = dka[...]
        dv_ref[...] = dva[...].astype(BF16)

    ks = pl.BlockSpec((t, LANES), lambda b, h, kj: (b * n + kj, h))
    seq = pl.BlockSpec((S, LANES), lambda b, h, kj: (b, h))
    rows = pl.BlockSpec((8, S), lambda b, h, kj: (b * MLA_H + h, 0))
    return _pc(body, name=name, grid=(Bl, MLA_H, n),
               in_specs=[seq, ks, ks, seq, rows, pl.BlockSpec((S, LANES), lambda b, h, kj: (b, do0 + h))],
               out_specs=[seq, ks, ks],
               out_shape=[SDS((T, MLA_H * LANES), F32), SDS((T, MLA_H * LANES), F32), SDS((T, MLA_H * LANES), BF16)],
               scratch=[pltpu.VMEM((t, LANES), F32), pltpu.VMEM((t, LANES), F32), pltpu.VMEM((8, S), F32)],
               comm=comm)(qr, kr, vr, o, lset, dycat)


def _rows2d(a):
    return a.reshape(-1, a.shape[-1])


def _scalar(i):
    return jnp.reshape(i, (1,)).astype(jnp.int32)


def chip_sum(g8, from_sibling, *, name):
    blk = g8.shape[1:]
    R, C = math.prod(blk[:-1]), blk[-1]
    tm = _tile(R, 512, 16)

    def body(c_ref, a_ref, b_ref, o_ref, ob_ref):
        s = a_ref[0, 0] + b_ref[0]
        o_ref[0] = s
        ob_ref[0] = s.astype(BF16)

    row = pl.BlockSpec((1, tm, C), lambda k, i, c: (k, i, 0))
    spec = pltpu.PrefetchScalarGridSpec(
        num_scalar_prefetch=1, grid=(4, R // tm),
        in_specs=[pl.BlockSpec((1, 1, tm, C), lambda k, i, c: (k, c[0], i, 0)), row], out_specs=[row, row])
    o, ob = pl.pallas_call(body, name=name, grid_spec=spec, out_shape=[SDS((4, R, C), F32), SDS((4, R, C), BF16)],
                           compiler_params=pltpu.CompilerParams(vmem_limit_bytes=VMEM_LIMIT),
                           )(_scalar(lax.axis_index("c")), g8.reshape(4, 2, R, C), from_sibling.reshape(4, R, C))
    return o.reshape((4,) + blk), ob.reshape((4,) + blk)


def adamw_sharded(w, m, v, sums, recv, layer, prev, *, name):
    blk = w.shape[1:]
    R, C = math.prod(blk[:-1]), blk[-1]
    tm = _tile(R, 256, 16)
    bc1 = 1.0 - ADAM_B1 ** ADAM_STEP
    bc2 = 1.0 - ADAM_B2 ** ADAM_STEP
    n_prev = 0 if prev is None else 4

    def body(chip_ref, w_ref, m_ref, v_ref, s_ref, r_ref, *rest):
        g_ref, d_ref, nm_ref, nv_ref = rest[n_prev:]
        g = s_ref[0] + r_ref[0].astype(F32) + r_ref[1].astype(F32) + r_ref[2].astype(F32)
        mm_ = ADAM_B1 * m_ref[0] + (1.0 - ADAM_B1) * g
        vv = ADAM_B2 * v_ref[0] + (1.0 - ADAM_B2) * (g * g)
        g_ref[0] = g
        nm_ref[0] = mm_
        nv_ref[0] = vv
        d_ref[0] = -ADAM_LR * ((mm_ / bc1) / (jnp.sqrt(vv / bc2) + ADAM_EPS) + ADAM_WD * w_ref[0])

    lay = pl.BlockSpec((1, tm, C), lambda i, c: (layer, i, 0))
    spec = pltpu.PrefetchScalarGridSpec(
        num_scalar_prefetch=1, grid=(R // tm,),
        in_specs=[lay, lay, lay, pl.BlockSpec((1, tm, C), lambda i, c: (c[0], i, 0)),
                  pl.BlockSpec((3, tm, C), lambda i, c: (0, i, 0))] + [ANY] * n_prev,
        out_specs=[lay] * 4)
    xi, yi, _ = _place()
    d3 = (w.shape[0], R, C)
    outs = pl.pallas_call(
        body, name=name, grid_spec=spec, out_shape=[SDS(d3, F32)] * 4,
        input_output_aliases={6 + i: i for i in range(n_prev)},
        compiler_params=pltpu.CompilerParams(vmem_limit_bytes=VMEM_LIMIT),
    )(_scalar(2 * xi + yi), w.reshape(d3), m.reshape(d3), v.reshape(d3), sums.reshape(4, R, C), recv.reshape(3, R, C),
      *([] if prev is None else prev))
    return list(outs)


def adamw(w, m, v, parts, *, name):
    shp = w.shape
    w2, m2, v2 = _rows2d(w), _rows2d(m), _rows2d(v)
    R, C = w2.shape
    p3 = [p.reshape(p.shape[0], R, C) for p in parts]
    tm = _tile(R, 256, 8)
    bc1 = 1.0 - ADAM_B1 ** ADAM_STEP
    bc2 = 1.0 - ADAM_B2 ** ADAM_STEP

    def body(w_ref, m_ref, v_ref, *refs):
        p_refs, (g_ref, d_ref, nm_ref, nv_ref) = refs[:len(p3)], refs[len(p3):]
        g = None
        for p_ref, p in zip(p_refs, p3):
            for k in range(p.shape[0]):
                term = p_ref[k].astype(F32)
                g = term if g is None else g + term
        mm_ = ADAM_B1 * m_ref[...] + (1.0 - ADAM_B1) * g
        vv = ADAM_B2 * v_ref[...] + (1.0 - ADAM_B2) * (g * g)
        g_ref[...] = g
        nm_ref[...] = mm_
        nv_ref[...] = vv
        d_ref[...] = -ADAM_LR * ((mm_ / bc1) / (jnp.sqrt(vv / bc2) + ADAM_EPS) + ADAM_WD * w_ref[...])

    blk = pl.BlockSpec((tm, C), lambda i: (i, 0))
    pspecs = [pl.BlockSpec((p.shape[0], tm, C), lambda i: (0, i, 0)) for p in p3]
    outs = _pc(body, name=name, grid=(R // tm,), in_specs=[blk, blk, blk] + pspecs,
               out_specs=[blk] * 4, out_shape=[SDS((R, C), F32)] * 4)(w2, m2, v2, *p3)
    return [o.reshape(shp) for o in outs]


def _place():
    return lax.axis_index("x"), lax.axis_index("y"), lax.axis_index("c")


def all_gather_many(xs, *, name):
    n = len(xs)

    def body(*refs):
        x_refs, o_refs = refs[:n], refs[n:2 * n]
        send_sems, recv_sems, local_sems = refs[2 * n:]
        x, y, c = _place()
        me, sibling = (x, y, c), (x, y, 1 - c)
        chips = [(1 - x, y), (x, 1 - y), (1 - x, 1 - y)]

        def rows(a, p):
            return o_refs[a].at[4 * p[0] + 2 * p[1] + p[2]]

        def copy(a, k, block, to, src=None):
            return pltpu.make_async_remote_copy(
                src_ref=rows(a, block) if src is None else src, dst_ref=rows(a, block),
                send_sem=send_sems.at[7 * a + k], recv_sem=recv_sems.at[7 * a + k], device_id=to, device_id_type=MESH)

        mine = [pltpu.make_async_copy(x_refs[a], rows(a, me), local_sems.at[a]) for a in range(n)]
        for cp in mine:
            cp.start()
        first = []
        for a in range(n):
            first.append(copy(a, 0, me, sibling, src=x_refs[a]))
            first += [copy(a, 1 + j, me, (*chip, c), src=x_refs[a]) for j, chip in enumerate(chips)]
        for cp in first:
            cp.start()
        passed = []
        for j, chip in enumerate(chips):
            for a in range(n):
                copy(a, 1 + j, (*chip, c), me).wait_recv()
                cp = copy(a, 4 + j, (*chip, c), sibling)
                cp.start()
                passed.append(cp)
        for a in range(n):
            copy(a, 0, sibling, me).wait_recv()
            for j, chip in enumerate(chips):
                copy(a, 4 + j, (*chip, 1 - c), me).wait_recv()
        for cp in first + passed:
            cp.wait_send()
        for cp in mine:
            cp.wait()

    return pl.pallas_call(
        body, name=name, in_specs=[ANY] * n, out_specs=[ANY] * n,
        out_shape=[SDS((N_DEV,) + a.shape, a.dtype) for a in xs],
        scratch_shapes=[pltpu.SemaphoreType.DMA((7 * n,)), pltpu.SemaphoreType.DMA((7 * n,)), pltpu.SemaphoreType.DMA((n,))],
    )(*xs)


def _stage(ins, out_shape, n_peers, copy_of, n_arrays=None, local_of=None):
    ins = list(ins)
    n = len(ins) if n_arrays is None else n_arrays

    def copies(in_refs, out_refs, send_sems, recv_sems):
        place = _place()
        out = []
        for a in range(n):
            for k in range(n_peers):
                src, dst, peer = copy_of(in_refs[a], out_refs[a], k, place)
                out.append(pltpu.make_async_remote_copy(
                    src_ref=src, dst_ref=dst, send_sem=send_sems.at[n_peers * a + k], recv_sem=recv_sems.at[n_peers * a + k],
                    device_id=peer, device_id_type=MESH))
        if local_of is not None:
            for i, (src, dst) in enumerate(local_of(in_refs, out_refs, place)):
                out.append(pltpu.make_async_copy(src, dst, send_sems.at[n_peers * n + i]))
        return out

    return dict(ins=ins, out_shape=list(out_shape), sems=n_peers * n + (n if local_of is not None else 0), copies=copies)


def _other_chips(x, y):
    return [(1 - x, y), (x, 1 - y), (1 - x, 1 - y)]


def stage_gather_direct(blocks):
    def copy_of(src, dst, k, place):
        x, y, c = place
        peer = (x, y, 1 - c) if k == 0 else (*_other_chips(x, y)[k - 1], c)
        return src, dst.at[4 * x + 2 * y + c], peer

    return _stage(blocks, [SDS((N_DEV,) + b.shape, b.dtype) for b in blocks], 4, copy_of)


def stage_gather_forward(bufs, own):
    n = len(bufs)

    def copy_of(src, dst, k, place):
        x, y, c = place
        cx, cy = _other_chips(x, y)[k]
        slot = 4 * cx + 2 * cy + c
        return src.at[slot], dst.at[slot], (x, y, 1 - c)

    def local_of(in_refs, out_refs, place):
        x, y, c = place
        return [(in_refs[n + a], out_refs[a].at[4 * x + 2 * y + c]) for a in range(n)]

    st = _stage(list(bufs) + list(own), [SDS(b.shape, b.dtype) for b in bufs], 3, copy_of, n_arrays=n, local_of=local_of)
    st["alias"] = n
    return st


def stage_rs_sibling(g8s):
    def copy_of(src, dst, k, place):
        x, y, c = place
        return src.at[2 * k + (1 - c)], dst.at[k], (x, y, 1 - c)

    return _stage(g8s, [SDS((4,) + g.shape[1:], g.dtype) for g in g8s], 4, copy_of)


def stage_rs_chips(sums):
    def copy_of(src, dst, k, place):
        x, y, c = place
        chip = _other_chips(x, y)[k]
        return src.at[2 * chip[0] + chip[1]], dst.at[k], (*chip, c)

    return _stage(sums, [SDS((3,) + s.shape[1:], s.dtype) for s in sums], 3, copy_of)


def run_stage(stage, *, name):
    n_in, n_out = len(stage["ins"]), len(stage["out_shape"])

    def body(*refs):
        cps = stage["copies"](refs[:n_in], refs[n_in:n_in + n_out], refs[-2], refs[-1])
        for cp in cps:
            cp.start()
        for cp in cps:
            cp.wait()

    return pl.pallas_call(
        body, name=name, in_specs=[ANY] * n_in, out_specs=[ANY] * n_out, out_shape=stage["out_shape"],
        scratch_shapes=[pltpu.SemaphoreType.DMA((stage["sems"],)), pltpu.SemaphoreType.DMA((stage["sems"],))],
    )(*stage["ins"])


def _owner_major(full, axis):
    shp = full.shape
    r = full.reshape(shp[:axis] + (N_DEV, shp[axis] // N_DEV) + shp[axis + 1:])
    return jnp.moveaxis(r, axis, 0)


def _from_owner_major(g8, axis):
    r = jnp.moveaxis(g8, 0, axis)
    shp = r.shape
    return r.reshape(shp[:axis] + (shp[axis] * shp[axis + 1],) + shp[axis + 2:])


def _perm_w_in(w):
    z = jnp.zeros((w.shape[0], LANES), w.dtype)
    dt = jnp.pad(w[:, 2560:2576], ((0, 0), (0, LANES - SSD_HEADS)))
    kpe = jnp.pad(w[:, 3728:3760], ((0, 0), (PE_LO, LANES - PE_HI)))
    return jnp.concatenate([w[:, 0:1024], w[:, 1024:2560], w[:, 2576:3088], w[:, 3088:3472], z, w[:, 3472:3728], dt, kpe], axis=1)


def _unperm_w_in(g):
    return jnp.concatenate([g[:, Z0:Z0 + 1024], g[:, XBC0:XBC0 + 1536], g[:, DT0:DT0 + SSD_HEADS], g[:, U0:U0 + 512],
                            g[:, CQ0:CQ0 + 384], g[:, CKV0:CKV0 + 256], g[:, KPE0 + PE_LO:KPE0 + PE_HI]], axis=1)


def _perm_w_uq(w):
    return jnp.pad(w.reshape(MLA_QR, MLA_H, MLA_QK), ((0, 0), (0, 0), (0, LANES - MLA_QK))).reshape(MLA_QR, MLA_H * LANES)


def _unperm_w_uq(g):
    return g.reshape(MLA_QR, MLA_H, LANES)[:, :, :MLA_QK].reshape(MLA_QR, MLA_H * MLA_QK)


def _perm_w_ukv(w):
    w3 = w.reshape(MLA_KVR, MLA_H, MLA_NOPE + MLA_V)
    pad = ((0, 0), (0, 0), (0, LANES - MLA_NOPE))
    k = jnp.pad(w3[:, :, :MLA_NOPE], pad).reshape(MLA_KVR, MLA_H * LANES)
    v = jnp.pad(w3[:, :, MLA_NOPE:], pad).reshape(MLA_KVR, MLA_H * LANES)
    return jnp.concatenate([k, v], axis=1)


def _unperm_w_ukv(g):
    k = g[:, :MLA_H * LANES].reshape(MLA_KVR, MLA_H, LANES)[:, :, :MLA_NOPE]
    v = g[:, MLA_H * LANES:].reshape(MLA_KVR, MLA_H, LANES)[:, :, :MLA_V]
    return jnp.concatenate([k, v], axis=2).reshape(MLA_KVR, MLA_H * (MLA_NOPE + MLA_V))


def _perm_w_out(w):
    m = jnp.pad(w[SSD_W + POOL_W:].reshape(MLA_H, MLA_V, D), ((0, 0), (0, LANES - MLA_V), (0, 0))).reshape(MLA_H * LANES, D)
    return jnp.concatenate([w[:SSD_W + POOL_W], m], axis=0)


def _unperm_w_out(g):
    m = g[SSD_W + POOL_W:].reshape(MLA_H, LANES, D)[:, :MLA_V].reshape(MLA_H * MLA_V, D)
    return jnp.concatenate([g[:SSD_W + POOL_W], m], axis=0)


def _lane_pad(v):
    return jnp.pad(v.reshape(1, -1), ((0, 0), (0, LANES - v.shape[-1])))


SMALL = ("attn_norm", "ssd_conv_b", "ssd_dt_bias", "ssd_a_log", "ssd_d", "ssd_norm", "pool_w", "pool_scale",
         "mla_q_norm", "mla_kv_norm", "ffn_norm", "ffn_conv_b", "final_norm")
SHARDED = {"w_in": 2, "ssd_conv_w": 2, "mla_w_uq": 2, "mla_w_ukv": 2, "w_out": 1, "ffn_w_up": 2, "ffn_conv_w": 2,
           "ffn_w_down": 1}
ALL_W = ("attn_norm", "w_in", "ssd_conv_w", "ssd_conv_b", "ssd_dt_bias", "ssd_a_log", "ssd_d", "ssd_norm", "pool_w",
         "pool_scale", "mla_q_norm", "mla_w_uq", "mla_kv_norm", "mla_w_ukv", "w_out", "ffn_norm", "ffn_w_up",
         "ffn_conv_w", "ffn_conv_b", "ffn_w_down", "final_norm")


def _pack_small(d):
    rows, layout = [], []
    for k in SMALL:
        a = d[k].reshape(-1)
        n = a.shape[0]
        r = -(-n // LANES)
        rows.append(jnp.pad(a, (0, r * LANES - n)).reshape(r, LANES))
        layout.append((k, n, r, d[k].shape))
    buf = jnp.concatenate(rows, axis=0)
    pad = (-buf.shape[0]) % 8
    return jnp.pad(buf, ((0, pad), (0, 0))), layout


def _unpack_small(buf, layout):
    out, r0 = {}, 0
    for k, n, r, shp in layout:
        out[k] = buf[r0:r0 + r].reshape(-1)[:n].reshape(shp)
        r0 += r
    return out


_PERM = {"w_in": _perm_w_in, "mla_w_uq": _perm_w_uq, "mla_w_ukv": _perm_w_ukv, "w_out": _perm_w_out}
FIRST = ("w_in", "ssd_conv_w")
REST = tuple(k for k in SHARDED if k not in FIRST)


def _sharded_entries(keys, gathered):
    return {k: _PERM.get(k, lambda t: t)(_from_owner_major(g8, SHARDED[k] - 1)) for k, g8 in zip(keys, gathered)}


def _layer_fwd(l, x, W, cs, sn, Bl, next_blocks=None, pending=None):
    T = x.shape[0]
    S = T // Bl
    n = f"l{l}_"
    h = rms_fwd(x, W["attn_norm"], name=n + "attn_norm")
    own_direct = stage_gather_direct(pending) if pending is not None else None
    proj = mm(h, W["w_in"], comm=own_direct, name=n + "w_in")
    proj3 = proj.reshape(Bl, S, PW)
    xbc3 = conv_silu_fwd(proj3, W["ssd_conv_w"], W["ssd_conv_b"], name=n + "ssd_conv")
    own_forward = stage_gather_forward(own_direct["result"], pending) if own_direct else None
    y3, hin = ssd_fwd(xbc3, proj3, W["ssd_dt_bias"], W["ssd_a_log"], W["ssd_d"], comm=own_forward, name=n + "ssd_scan")
    if own_direct:
        W = {**W, **_sharded_entries(REST, own_forward["result"])}
    y = y3.reshape(T, SSD_W)
    y_ssd = gated_rms_fwd(y, proj, W["ssd_norm"], name=n + "ssd_gate_norm")
    y_pool = pool_fwd(proj3, W["pool_w"], W["pool_scale"], name=n + "pool").reshape(T, POOL_W)
    qn = rms_fwd(proj, W["mla_q_norm"], col0=CQ0, width=MLA_QR, name=n + "q_norm")
    kvn = rms_fwd(proj, W["mla_kv_norm"], col0=CKV0, width=MLA_KVR, name=n + "kv_norm")
    qt = mm(qn, W["mla_w_uq"], out_dtype=BF16, name=n + "w_uq")
    kvt = mm(kvn, W["mla_w_ukv"], out_dtype=BF16, name=n + "w_ukv")
    qr, kr, vr = mla_prep_fwd(qt, kvt, proj, cs, sn, name=n + "rope")
    direct = stage_gather_direct(next_blocks) if next_blocks is not None else None
    o, lset = flash_fwd(qr, kr, vr, Bl, comm=direct, name=n + "attn")
    ycat = jnp.concatenate([y_ssd, y_pool, o], axis=1)
    x1 = mm(ycat, W["w_out"], add=x, name=n + "w_out")
    h2 = rms_fwd(x1, W["ffn_norm"], name=n + "ffn_norm")
    forward = stage_gather_forward(direct["result"], next_blocks) if direct else None
    pre = mm(h2, W["ffn_w_up"], comm=forward, name=n + "w_up")
    gathered = forward["result"] if direct else None
    pre3 = pre.reshape(Bl, S, 2 * DFF)
    act = ffn_act_fwd(pre3, W["ffn_conv_w"], W["ffn_conv_b"], name=n + "ffn_act").reshape(T, DFF)
    x2 = mm(act, W["ffn_w_down"], add=x1, name=n + "w_down")
    saved = dict(x=x, h=h, proj=proj, xbc3=xbc3, hin=hin, y=y, qn=qn, kvn=kvn, vr=vr, qr=qr, kr=kr, o=o, lset=lset,
                 ycat=ycat, x1=x1, h2=h2, pre3=pre3, act=act)
    return x2, saved, gathered, W


EARLY = ("ffn_w_up", "ffn_conv_w", "ffn_w_down", "w_out")
LATE = tuple(k for k in SHARDED if k not in EARLY)
_UNPERM = {"w_in": _unperm_w_in, "mla_w_uq": _unperm_w_uq, "mla_w_ukv": _unperm_w_ukv, "w_out": _unperm_w_out}


def _by_owner(g, keys):
    return [_owner_major(_UNPERM.get(k, lambda t: t)(g[k]), SHARDED[k] - 1) for k in keys]


def _chip_sums(g8s, from_sibling, tag):
    return [chip_sum(g8, r, name=f"{tag}{a}") for a, (g8, r) in enumerate(zip(g8s, from_sibling))]


def _layer_bwd(l, dx2, dx2b, W, sv, cs, sn, Bl, later_g8=None):
    T = dx2.shape[0]
    S = T // Bl
    n = f"l{l}_b_"
    g = {}
    g["ffn_w_down"] = mm(sv["act"], dx2b, ta=True, name=n + "dw_down")
    dact = mm(dx2b, W["ffn_w_down"], tb=True, name=n + "dact")
    to_sibling = stage_rs_sibling(later_g8) if later_g8 is not None else None
    dpg, dpv, dwg, dwv, dbg, dbv = ffn_act_bwd(sv["pre3"], W["ffn_conv_w"], W["ffn_conv_b"], dact.reshape(Bl, S, DFF),
                                               comm=to_sibling, name=n + "ffn_act")
    to_chips = sums = None
    if to_sibling:
        sums = _chip_sums(later_g8, to_sibling["result"], n + "rs_late_add")
        to_chips = stage_rs_chips([sb for _, sb in sums])
    g["ffn_conv_w"] = jnp.concatenate([dwg, dwv], axis=1)
    g["ffn_conv_b"] = jnp.concatenate([dbg, dbv], axis=1)
    dpg, dpv = dpg.reshape(T, DFF), dpv.reshape(T, DFF)
    g["ffn_w_up"] = jnp.concatenate([mm(sv["h2"], dpg, ta=True, name=n + "dw_up_g"),
                                     mm(sv["h2"], dpv, ta=True, name=n + "dw_up_v")], axis=1)
    dh2 = mm(dpg, W["ffn_w_up"], tb=True, name=n + "dh2_g")
    dh2 = mm(dpv, W["ffn_w_up"], tb=True, b_k0=DFF, add=dh2, out_dtype=BF16, name=n + "dh2_v")
    dx1, dx1b, g["ffn_norm"] = rms_bwd(sv["x1"], W["ffn_norm"], dh2, add=dx2, name=n + "ffn_norm")
    g["w_out"] = mm(sv["ycat"], dx1b, ta=True, name=n + "dw_out")
    dycat = mm(dx1b, W["w_out"], tb=True, out_dtype=BF16, name=n + "dycat")
    proj, proj3 = sv["proj"], sv["proj"].reshape(Bl, S, PW)
    dy, dz, g["ssd_norm"] = gated_rms_bwd(sv["y"], proj, W["ssd_norm"], dycat, name=n + "ssd_gate_norm")
    dxa, ddt, dpar, dd = ssd_bwd(sv["xbc3"], proj3, sv["hin"], dy.reshape(Bl, S, SSD_W), W["ssd_dt_bias"], W["ssd_a_log"],
                                 W["ssd_d"], comm=to_chips, name=n + "ssd_scan")
    reduced_late = ([s32 for s32, _ in sums], to_chips["result"]) if to_chips else None
    g["ssd_dt_bias"] = dpar[0, :SSD_HEADS]
    g["ssd_a_log"] = dpar[1, :SSD_HEADS]
    g["ssd_d"] = dd[:, :2].reshape(SSD_HEADS)
    early_g8 = _by_owner(g, EARLY)
    early_sibling = stage_rs_sibling(early_g8)
    dxbc, g["ssd_conv_w"], g["ssd_conv_b"] = conv_silu_bwd(proj3, W["ssd_conv_w"], W["ssd_conv_b"], dxa, comm=early_sibling,
                                                           name=n + "ssd_conv")
    du, g["pool_w"], g["pool_scale"] = pool_bwd(proj3, W["pool_w"], W["pool_scale"], dycat.reshape(Bl, S, YCAT), name=n + "pool")
    early_sums = _chip_sums(early_g8, early_sibling["result"], n + "rs_early_add")
    early_chips = stage_rs_chips([sb for _, sb in early_sums])
    dqr, dkr, dv = flash_bwd(sv["qr"], sv["kr"], sv["vr"], sv["o"], sv["lset"], dycat, Bl, comm=early_chips, name=n + "attn_bwd")
    reduced_early = ([s32 for s32, _ in early_sums], early_chips["result"])
    dqt, dkt, dkpe = mla_prep_bwd(dqr, dkr, cs, sn, name=n + "rope")
    g["mla_w_ukv"] = jnp.concatenate([mm(sv["kvn"], dkt, ta=True, name=n + "dw_uk"),
                                      mm(sv["kvn"], dv, ta=True, name=n + "dw_uv")], axis=1)
    dkvn = mm(dkt, W["mla_w_ukv"], tb=True, name=n + "dkvn_k")
    dkvn = mm(dv, W["mla_w_ukv"], tb=True, b_k0=MLA_H * LANES, add=dkvn, name=n + "dkvn_v")
    g["mla_w_uq"] = mm(sv["qn"], dqt, ta=True, name=n + "dw_uq")
    dqn = mm(dqt, W["mla_w_uq"], tb=True, name=n + "dqn")
    dcq, g["mla_q_norm"] = rms_bwd(proj, W["mla_q_norm"], dqn, col0=CQ0, width=MLA_QR, name=n + "q_norm")
    dckv, g["mla_kv_norm"] = rms_bwd(proj, W["mla_kv_norm"], dkvn, col0=CKV0, width=MLA_KVR, name=n + "kv_norm")
    dproj = jnp.concatenate([dz, dxbc.reshape(T, SSD_XBC), du.reshape(T, POOL_W), dcq, jnp.zeros((T, LANES), BF16), dckv,
                             ddt.reshape(T, LANES), dkpe], axis=1)
    g["w_in"] = mm(sv["h"], dproj, ta=True, name=n + "dw_in")
    dh = mm(dproj, W["w_in"], tb=True, out_dtype=BF16, name=n + "dh")
    dx, dxb, g["attn_norm"] = rms_bwd(sv["x"], W["attn_norm"], dh, add=dx1, name=n + "attn_norm")
    return dx, dxb, g, reduced_late, reduced_early


def kernel(x, positions, attn_norm, w_in, ssd_conv_w, ssd_conv_b, ssd_dt_bias, ssd_a_log, ssd_d, ssd_norm, pool_w, pool_scale, mla_q_norm, mla_w_uq, mla_kv_norm, mla_w_ukv, w_out, ffn_norm, ffn_w_up, ffn_conv_w, ffn_conv_b, ffn_w_down, final_norm, loss_target, m_attn_norm, m_w_in, m_ssd_conv_w, m_ssd_conv_b, m_ssd_dt_bias, m_ssd_a_log, m_ssd_d, m_ssd_norm, m_pool_w, m_pool_scale, m_mla_q_norm, m_mla_w_uq, m_mla_kv_norm, m_mla_w_ukv, m_w_out, m_ffn_norm, m_ffn_w_up, m_ffn_conv_w, m_ffn_conv_b, m_ffn_w_down, m_final_norm, v_attn_norm, v_w_in, v_ssd_conv_w, v_ssd_conv_b, v_ssd_dt_bias, v_ssd_a_log, v_ssd_d, v_ssd_norm, v_pool_w, v_pool_scale, v_mla_q_norm, v_mla_w_uq, v_mla_kv_norm, v_mla_w_ukv, v_w_out, v_ffn_norm, v_ffn_w_up, v_ffn_conv_w, v_ffn_conv_b, v_ffn_w_down, v_final_norm):
    a = locals()
    Wt = {k: a[k] for k in ALL_W}
    Mo = {k: a["m_" + k] for k in ALL_W}
    Vo = {k: a["v_" + k] for k in ALL_W}
    Bl, S, _ = x.shape
    T = Bl * S

    names = list(SHARDED)
    conv = ("ssd_conv_w", "ffn_conv_w")

    def blocks_of(l, keys=names):
        return [Wt[k][l] if k in conv else Wt[k][l].astype(BF16) for k in keys]

    def replicated(l):
        return {
            "attn_norm": attn_norm[l].reshape(1, D), "ssd_conv_b": ssd_conv_b[l].reshape(1, SSD_XBC),
            "ssd_dt_bias": _lane_pad(ssd_dt_bias[l]), "ssd_a_log": _lane_pad(ssd_a_log[l]),
            "ssd_d": jnp.repeat(ssd_d[l].reshape(NPAIR, 2), SSD_P, axis=1), "ssd_norm": ssd_norm[l].reshape(1, SSD_W),
            "pool_w": pool_w[l].astype(BF16), "pool_scale": pool_scale[l].reshape(1, POOL_W),
            "mla_q_norm": mla_q_norm[l].reshape(1, MLA_QR), "mla_kv_norm": mla_kv_norm[l].reshape(1, MLA_KVR),
            "ffn_norm": ffn_norm[l].reshape(1, D), "ffn_conv_b": ffn_conv_b[l].reshape(1, 2 * DFF)}

    pos = positions.astype(F32).reshape(T, 1)
    inv_freq = ROPE_THETA ** (-jnp.arange(0, MLA_ROPE, 2, dtype=F32) / MLA_ROPE)
    invf = jnp.pad(jnp.concatenate([inv_freq, inv_freq]), (PE_LO, LANES - PE_HI)).reshape(1, LANES)
    cs, sn = rope_tables(pos, invf, name="rope_tables")

    first = all_gather_many(blocks_of(0, FIRST), name="gather_weights_l0")
    layers = [{**replicated(0), **_sharded_entries(FIRST, first)}]
    xc = x.reshape(T, D)
    saved = []
    for l in range(DEPTH):
        xc, sv, gathered, layers[l] = _layer_fwd(l, xc, layers[l], cs, sn, Bl,
                                                 next_blocks=blocks_of(l + 1) if l + 1 < DEPTH else None,
                                                 pending=blocks_of(0, REST) if l == 0 else None)
        saved.append(sv)
        if gathered is not None:
            layers.append({**replicated(l + 1), **_sharded_entries(names, gathered)})
    dx, dxb, g_final, loss_part = final_loss(xc, final_norm.reshape(1, D), loss_target.reshape(T, D), name="final_loss")

    grads = [None] * DEPTH
    reduced = {}

    def record(l, keys, red):
        for a, k in enumerate(keys):
            reduced[(l, k)] = (red[0][a], red[1][a])

    later_g8 = None
    for l in reversed(range(DEPTH)):
        dx, dxb, grads[l], red_late, red_early = _layer_bwd(l, dx, dxb, layers[l], saved[l], cs, sn, Bl, later_g8=later_g8)
        if red_late is not None:
            record(l + 1, LATE, red_late)
        record(l, EARLY, red_early)
        later_g8 = _by_owner(grads[l], LATE)
    loss = lax.psum(loss_part[0, 0], AXES)
    sums = _chip_sums(later_g8, run_stage(stage_rs_sibling(later_g8), name="rs_sibling_l0"), "rs_add_l0_")
    record(0, LATE, ([s32 for s32, _ in sums], run_stage(stage_rs_chips([sb for _, sb in sums]), name="rs_chips_l0")))

    out_g, out_d, out_m, out_v = {}, {}, {}, {}
    for k in names:
        outs = None
        for l in reversed(range(DEPTH)):
            s32, recv = reduced[(l, k)]
            outs = adamw_sharded(Wt[k], Mo[k], Vo[k], s32, recv, l, outs, name=f"adamw_l{l}_{k}")
        out_g[k], out_d[k], out_m[k], out_v[k] = (o.reshape(Wt[k].shape) for o in outs)

    part = {k: g_final.reshape(D) if k == "final_norm" else
            jnp.stack([grads[l][k].reshape(Wt[k].shape[1:]) for l in range(DEPTH)]) for k in SMALL}
    pg, layout = _pack_small(part)
    pw, _ = _pack_small(Wt)
    pm, _ = _pack_small(Mo)
    pv, _ = _pack_small(Vo)
    (pg8,) = all_gather_many([pg], name="gather_small_grads")
    sg, sd, sm, sv_ = adamw(pw, pm, pv, [pg8], name="adamw_small")
    for dst, buf in ((out_g, sg), (out_d, sd), (out_m, sm), (out_v, sv_)):
        dst.update(_unpack_small(buf, layout))

    return (loss, dx.reshape(Bl, S, D), *[out_g[k] for k in ALL_W], *[out_d[k] for k in ALL_W],
            *[out_m[k] for k in ALL_W], *[out_v[k] for k in ALL_W])
```
